```python
import math
import jax, jax.numpy as jnp
from jax import lax
import numpy as np

D_MODEL = 1024
BATCH = 8
SEQ = 8192
DEPTH = 2

N_MIXERS = 2
N_POOL_LAYERS = (DEPTH + 1) // 2
N_GDN_LAYERS = DEPTH // 2
PLE_DIM = 256
D_FF = 4 * D_MODEL
POOL_WINDOWS = (2, 4, 8, 16)
N_POOL_GROUPS = len(POOL_WINDOWS)
POOL_GROUP = D_MODEL // N_POOL_GROUPS
GDN_HEADS = 8
GDN_HEAD_DIM = 128
GDN_KEY_DIM = GDN_HEADS * GDN_HEAD_DIM
GDN_VAL_DIM = GDN_HEADS * GDN_HEAD_DIM
GDN_CONV_DIM = 2 * GDN_KEY_DIM + GDN_VAL_DIM
GDN_IN_DIM = GDN_CONV_DIM + GDN_VAL_DIM + 2 * GDN_HEADS
CONV_WIDTH = 4
CHUNK = 64
DEEPNORM_ALPHA = (2.0 * DEPTH) ** 0.25
DEEPNORM_BETA = (8.0 * DEPTH) ** -0.25
LN_EPS = 1e-5
RMS_EPS = 1e-6
L2_EPS = 1e-6

kernel_name = "pool_gdn_deepnorm_hybrid"


def layer_norm(x, g, b):
    xf = x.astype(jnp.float32)
    mu = jnp.mean(xf, axis=-1, keepdims=True)
    var = jnp.mean(jnp.square(xf - mu), axis=-1, keepdims=True)
    return ((xf - mu) * lax.rsqrt(var + LN_EPS) * g + b).astype(x.dtype)


def pool_mixer(x, w_grp, b_grp, scale):
    B, S, _ = x.shape
    xg = x.astype(jnp.float32).reshape(B, S, N_POOL_GROUPS, POOL_GROUP)
    cs = jnp.cumsum(xg, axis=1)
    pos = jnp.arange(1, S + 1, dtype=jnp.float32)[None, :, None]
    outs = []
    for gi, w in enumerate(POOL_WINDOWS):
        c = cs[:, :, gi]
        c_prev = jnp.pad(c, ((0, 0), (w, 0), (0, 0)))[:, :S]
        mean = (c - c_prev) / jnp.minimum(pos, float(w))
        outs.append(mean - xg[:, :, gi])
    pooled = jnp.stack(outs, axis=2).astype(x.dtype)
    y = jnp.einsum('bsgc,gcd->bsgd', pooled, w_grp) + b_grp
    return y.reshape(B, S, D_MODEL) * scale


def causal_depthwise_conv(x, w):
    C = x.shape[-1]
    return lax.conv_general_dilated(
        x, w[:, None, :].astype(x.dtype), window_strides=(1,),
        padding=[(CONV_WIDTH - 1, 0)], dimension_numbers=('NWC', 'WIO', 'NWC'),
        feature_group_count=C)


def l2_normalize(t):
    return t * lax.rsqrt(jnp.sum(jnp.square(t), axis=-1, keepdims=True) + L2_EPS)


def chunk_gated_delta_rule(q, k, v, g, beta):
    B, S, H, Dk = q.shape
    Dv = v.shape[-1]
    N = S // CHUNK

    def to_chunks(t):
        return t.reshape(B, N, CHUNK, H, -1).transpose(0, 3, 1, 2, 4)

    q, k, v = to_chunks(q), to_chunks(k), to_chunks(v)
    g = g.reshape(B, N, CHUNK, H).transpose(0, 3, 1, 2)
    beta = beta.reshape(B, N, CHUNK, H).transpose(0, 3, 1, 2)
    g = jnp.cumsum(g, axis=-1)

    idx = jnp.arange(CHUNK)
    causal = idx[:, None] >= idx[None, :]
    strict = idx[:, None] > idx[None, :]
    decay = jnp.exp(jnp.where(causal, g[..., :, None] - g[..., None, :], -jnp.inf))

    k_beta = k * beta[..., None]
    a = jnp.einsum('bhncd,bhnmd->bhncm', k_beta, k) * decay
    m = jnp.where(strict, a, 0.0) + jnp.eye(CHUNK, dtype=jnp.float32)
    rhs = jnp.concatenate([v * beta[..., None], k_beta * jnp.exp(g)[..., None]], axis=-1)
    sol = lax.linalg.triangular_solve(m, rhs, left_side=True, lower=True, unit_diagonal=True)
    u, w = sol[..., :Dv], sol[..., Dv:]

    qk = jnp.einsum('bhncd,bhnmd->bhncm', q, k) * decay
    q_dec = q * jnp.exp(g)[..., None]
    k_dec = k * jnp.exp(g[..., -1:] - g)[..., None]
    g_last = jnp.exp(g[..., -1])

    def step(state, xs):
        u_c, w_c, qk_c, qd_c, kd_c, gl_c = xs
        v_new = u_c - jnp.einsum('bhcd,bhde->bhce', w_c, state)
        o_c = (jnp.einsum('bhcd,bhde->bhce', qd_c, state)
               + jnp.einsum('bhcm,bhme->bhce', qk_c, v_new))
        state = state * gl_c[..., None, None] + jnp.einsum('bhcd,bhce->bhde', kd_c, v_new)
        return state, o_c

    xs = tuple(jnp.moveaxis(t, 2, 0) for t in (u, w, qk, q_dec, k_dec, g_last))
    s0 = jnp.zeros((B, H, Dk, Dv), jnp.float32)
    _, o = lax.scan(step, s0, xs)
    return o.transpose(1, 0, 3, 2, 4).reshape(B, S, H, Dv)


def gated_deltanet(x, w_in, conv_w, a_log, dt_bias, norm_w, w_out):
    B, S, _ = x.shape
    H, Dh = GDN_HEADS, GDN_HEAD_DIM
    proj = x @ w_in
    qkv, z, b_logit, a_logit = jnp.split(
        proj, [GDN_CONV_DIM, GDN_CONV_DIM + GDN_VAL_DIM, GDN_CONV_DIM + GDN_VAL_DIM + H], axis=-1)
    qkv = jax.nn.silu(causal_depthwise_conv(qkv, conv_w))
    q, k, v = jnp.split(qkv, [GDN_KEY_DIM, 2 * GDN_KEY_DIM], axis=-1)
    q = l2_normalize(q.reshape(B, S, H, Dh).astype(jnp.float32)) * (Dh ** -0.5)
    k = l2_normalize(k.reshape(B, S, H, Dh).astype(jnp.float32))
    v = v.reshape(B, S, H, Dh).astype(jnp.float32)
    beta = jax.nn.sigmoid(b_logit.astype(jnp.float32))
    g = -jnp.exp(a_log.astype(jnp.float32)) * jax.nn.softplus(
        a_logit.astype(jnp.float32) + dt_bias.astype(jnp.float32))
    o = chunk_gated_delta_rule(q, k, v, g, beta)
    zf = z.reshape(B, S, H, Dh).astype(jnp.float32)
    o = o * lax.rsqrt(jnp.mean(jnp.square(o), axis=-1, keepdims=True) + RMS_EPS) * norm_w * jax.nn.silu(zf)
    return o.reshape(B, S, GDN_VAL_DIM).astype(x.dtype) @ w_out


def squared_relu_mlp(x, w1, w2):
    return jnp.square(jax.nn.relu(x @ w1)) @ w2


def _fwd_setup_inputs(seed: int = 0) -> dict:
    key = jax.random.key(seed)
    ks = jax.random.split(key, 20)
    f32 = jnp.float32
    nrm = lambda k, s: jax.random.normal(k, s, f32)
    dt = jnp.exp(jax.random.uniform(ks[9], (N_GDN_LAYERS, GDN_HEADS), f32,
                                    math.log(1e-3), math.log(1e-1)))
    return {
        "x": nrm(ks[0], (BATCH, SEQ, D_MODEL)),
        "p": nrm(ks[1], (DEPTH, BATCH, SEQ, PLE_DIM)),
        "ln_gain": 1.0 + 0.02 * nrm(ks[2], (DEPTH, 2, D_MODEL)),
        "ln_bias": 0.02 * nrm(ks[3], (DEPTH, 2, D_MODEL)),
        "pool_w": nrm(ks[4], (N_POOL_LAYERS, N_POOL_GROUPS, POOL_GROUP, POOL_GROUP)) * (POOL_GROUP ** -0.5) * DEEPNORM_BETA,
        "pool_b": 0.02 * nrm(ks[5], (N_POOL_LAYERS, N_POOL_GROUPS, POOL_GROUP)),
        "pool_scale": 1.0 + 0.1 * nrm(ks[6], (N_POOL_LAYERS, D_MODEL)),
        "gdn_w_in": nrm(ks[7], (N_GDN_LAYERS, D_MODEL, GDN_IN_DIM)) * (D_MODEL ** -0.5),
        "gdn_conv": nrm(ks[8], (N_GDN_LAYERS, CONV_WIDTH, GDN_CONV_DIM)) * (CONV_WIDTH ** -0.5),
        "gdn_a_log": jnp.log(jax.random.uniform(ks[10], (N_GDN_LAYERS, GDN_HEADS), f32, 1.0, 16.0)),
        "gdn_dt_bias": dt + jnp.log(-jnp.expm1(-dt)),
        "gdn_norm_w": 1.0 + 0.02 * nrm(ks[11], (N_GDN_LAYERS, GDN_HEAD_DIM)),
        "gdn_w_out": nrm(ks[12], (N_GDN_LAYERS, GDN_VAL_DIM, D_MODEL)) * (GDN_VAL_DIM ** -0.5) * DEEPNORM_BETA,
        "mlp_w1": nrm(ks[13], (DEPTH, D_MODEL, D_FF)) * (D_MODEL ** -0.5),
        "mlp_w2": nrm(ks[14], (DEPTH, D_FF, D_MODEL)) * (D_FF ** -0.5) * DEEPNORM_BETA,
        "ple_gate_w": nrm(ks[15], (DEPTH, D_MODEL, D_MODEL)) * (D_MODEL ** -0.5),
        "ple_gate_b": 0.02 * nrm(ks[16], (DEPTH, D_MODEL)),
        "ple_proj": nrm(ks[17], (DEPTH, PLE_DIM, D_MODEL)) * (PLE_DIM ** -0.5),
    }


def _fwd_reference(x, p, ln_gain, ln_bias, pool_w, pool_b, pool_scale, gdn_w_in, gdn_conv,
              gdn_a_log, gdn_dt_bias, gdn_norm_w, gdn_w_out, mlp_w1, mlp_w2,
              ple_gate_w, ple_gate_b, ple_proj):
    for i in range(DEPTH):
        j = i // N_MIXERS
        if i % N_MIXERS == 0:
            mix = pool_mixer(x, pool_w[j], pool_b[j], pool_scale[j])
        else:
            mix = gated_deltanet(x, gdn_w_in[j], gdn_conv[j], gdn_a_log[j], gdn_dt_bias[j],
                                 gdn_norm_w[j], gdn_w_out[j])
        x = layer_norm(DEEPNORM_ALPHA * x + mix, ln_gain[i, 0], ln_bias[i, 0])
        ff = squared_relu_mlp(x, mlp_w1[i], mlp_w2[i])
        pe = jax.nn.sigmoid(x @ ple_gate_w[i] + ple_gate_b[i]) * (p[i] @ ple_proj[i])
        x = layer_norm(DEEPNORM_ALPHA * x + ff + pe, ln_gain[i, 1], ln_bias[i, 1])
    return x


import jax as _jax
import jax.numpy as _jnp

TWIN_FORMAT = 'train_step'
FWD_PARAMS = ['x', 'p', 'ln_gain', 'ln_bias', 'pool_w', 'pool_b', 'pool_scale', 'gdn_w_in', 'gdn_conv', 'gdn_a_log', 'gdn_dt_bias', 'gdn_norm_w', 'gdn_w_out', 'mlp_w1', 'mlp_w2', 'ple_gate_w', 'ple_gate_b', 'ple_proj']
TWIN_WEIGHTS = ['ln_gain', 'ln_bias', 'pool_w', 'pool_b', 'pool_scale', 'gdn_w_in', 'gdn_conv', 'gdn_a_log', 'gdn_dt_bias', 'gdn_norm_w', 'gdn_w_out', 'mlp_w1', 'mlp_w2', 'ple_gate_w', 'ple_gate_b', 'ple_proj']
TWIN_DIFF_INPUT = 'x'
TWIN_INPUTS = ['x', 'p', 'ln_gain', 'ln_bias', 'pool_w', 'pool_b', 'pool_scale', 'gdn_w_in', 'gdn_conv', 'gdn_a_log', 'gdn_dt_bias', 'gdn_norm_w', 'gdn_w_out', 'mlp_w1', 'mlp_w2', 'ple_gate_w', 'ple_gate_b', 'ple_proj', 'loss_target', 'm_ln_gain', 'm_ln_bias', 'm_pool_w', 'm_pool_b', 'm_pool_scale', 'm_gdn_w_in', 'm_gdn_conv', 'm_gdn_a_log', 'm_gdn_dt_bias', 'm_gdn_norm_w', 'm_gdn_w_out', 'm_mlp_w1', 'm_mlp_w2', 'm_ple_gate_w', 'm_ple_gate_b', 'm_ple_proj', 'v_ln_gain', 'v_ln_bias', 'v_pool_w', 'v_pool_b', 'v_pool_scale', 'v_gdn_w_in', 'v_gdn_conv', 'v_gdn_a_log', 'v_gdn_dt_bias', 'v_gdn_norm_w', 'v_gdn_w_out', 'v_mlp_w1', 'v_mlp_w2', 'v_ple_gate_w', 'v_ple_gate_b', 'v_ple_proj']
TWIN_OUTPUTS = ['loss', 'grad_x', 'grad_ln_gain', 'grad_ln_bias', 'grad_pool_w', 'grad_pool_b', 'grad_pool_scale', 'grad_gdn_w_in', 'grad_gdn_conv', 'grad_gdn_a_log', 'grad_gdn_dt_bias', 'grad_gdn_norm_w', 'grad_gdn_w_out', 'grad_mlp_w1', 'grad_mlp_w2', 'grad_ple_gate_w', 'grad_ple_gate_b', 'grad_ple_proj', 'delta_ln_gain', 'delta_ln_bias', 'delta_pool_w', 'delta_pool_b', 'delta_pool_scale', 'delta_gdn_w_in', 'delta_gdn_conv', 'delta_gdn_a_log', 'delta_gdn_dt_bias', 'delta_gdn_norm_w', 'delta_gdn_w_out', 'delta_mlp_w1', 'delta_mlp_w2', 'delta_ple_gate_w', 'delta_ple_gate_b', 'delta_ple_proj', 'new_m_ln_gain', 'new_m_ln_bias', 'new_m_pool_w', 'new_m_pool_b', 'new_m_pool_scale', 'new_m_gdn_w_in', 'new_m_gdn_conv', 'new_m_gdn_a_log', 'new_m_gdn_dt_bias', 'new_m_gdn_norm_w', 'new_m_gdn_w_out', 'new_m_mlp_w1', 'new_m_mlp_w2', 'new_m_ple_gate_w', 'new_m_ple_gate_b', 'new_m_ple_proj', 'new_v_ln_gain', 'new_v_ln_bias', 'new_v_pool_w', 'new_v_pool_b', 'new_v_pool_scale', 'new_v_gdn_w_in', 'new_v_gdn_conv', 'new_v_gdn_a_log', 'new_v_gdn_dt_bias', 'new_v_gdn_norm_w', 'new_v_gdn_w_out', 'new_v_mlp_w1', 'new_v_mlp_w2', 'new_v_ple_gate_w', 'new_v_ple_gate_b', 'new_v_ple_proj']
TWIN_LEAF_KINDS = {'loss': 'loss', 'grad_x': 'grad_x', 'grad_ln_gain': 'grad_w', 'grad_ln_bias': 'grad_w', 'grad_pool_w': 'grad_w', 'grad_pool_b': 'grad_w', 'grad_pool_scale': 'grad_w', 'grad_gdn_w_in': 'grad_w', 'grad_gdn_conv': 'grad_w', 'grad_gdn_a_log': 'grad_w', 'grad_gdn_dt_bias': 'grad_w', 'grad_gdn_norm_w': 'grad_w', 'grad_gdn_w_out': 'grad_w', 'grad_mlp_w1': 'grad_w', 'grad_mlp_w2': 'grad_w', 'grad_ple_gate_w': 'grad_w', 'grad_ple_gate_b': 'grad_w', 'grad_ple_proj': 'grad_w', 'delta_ln_gain': 'delta_w', 'delta_ln_bias': 'delta_w', 'delta_pool_w': 'delta_w', 'delta_pool_b': 'delta_w', 'delta_pool_scale': 'delta_w', 'delta_gdn_w_in': 'delta_w', 'delta_gdn_conv': 'delta_w', 'delta_gdn_a_log': 'delta_w', 'delta_gdn_dt_bias': 'delta_w', 'delta_gdn_norm_w': 'delta_w', 'delta_gdn_w_out': 'delta_w', 'delta_mlp_w1': 'delta_w', 'delta_mlp_w2': 'delta_w', 'delta_ple_gate_w': 'delta_w', 'delta_ple_gate_b': 'delta_w', 'delta_ple_proj': 'delta_w', 'new_m_ln_gain': 'new_m', 'new_m_ln_bias': 'new_m', 'new_m_pool_w': 'new_m', 'new_m_pool_b': 'new_m', 'new_m_pool_scale': 'new_m', 'new_m_gdn_w_in': 'new_m', 'new_m_gdn_conv': 'new_m', 'new_m_gdn_a_log': 'new_m', 'new_m_gdn_dt_bias': 'new_m', 'new_m_gdn_norm_w': 'new_m', 'new_m_gdn_w_out': 'new_m', 'new_m_mlp_w1': 'new_m', 'new_m_mlp_w2': 'new_m', 'new_m_ple_gate_w': 'new_m', 'new_m_ple_gate_b': 'new_m', 'new_m_ple_proj': 'new_m', 'new_v_ln_gain': 'new_v', 'new_v_ln_bias': 'new_v', 'new_v_pool_w': 'new_v', 'new_v_pool_b': 'new_v', 'new_v_pool_scale': 'new_v', 'new_v_gdn_w_in': 'new_v', 'new_v_gdn_conv': 'new_v', 'new_v_gdn_a_log': 'new_v', 'new_v_gdn_dt_bias': 'new_v', 'new_v_gdn_norm_w': 'new_v', 'new_v_gdn_w_out': 'new_v', 'new_v_mlp_w1': 'new_v', 'new_v_mlp_w2': 'new_v', 'new_v_ple_gate_w': 'new_v', 'new_v_ple_gate_b': 'new_v', 'new_v_ple_proj': 'new_v'}


def _forward(args):
    return _fwd_reference(*[args[k] for k in FWD_PARAMS])


def _output_shape():
    def fwd():
        inp = _fwd_setup_inputs(0)
        return _fwd_reference(*[inp[k] for k in FWD_PARAMS])
    out = _jax.eval_shape(fwd)
    return out.shape, out.dtype

N_MICROBATCH = 1
ADAM_LR = 0.001
ADAM_B1 = 0.9
ADAM_B2 = 0.999
ADAM_EPS = 1e-08
ADAM_WD = 0.01
ADAM_STEP = 10
PER_EXAMPLE_BATCH_AXIS = {'x': 0, 'p': 1, 'loss_target': 0}
SHARED_INPUTS = []
_WEIGHT_DTYPES = {'ln_gain': _jnp.float32, 'ln_bias': _jnp.float32, 'pool_w': _jnp.float32, 'pool_b': _jnp.float32, 'pool_scale': _jnp.float32, 'gdn_w_in': _jnp.float32, 'gdn_conv': _jnp.float32, 'gdn_a_log': _jnp.float32, 'gdn_dt_bias': _jnp.float32, 'gdn_norm_w': _jnp.float32, 'gdn_w_out': _jnp.float32, 'mlp_w1': _jnp.float32, 'mlp_w2': _jnp.float32, 'ple_gate_w': _jnp.float32, 'ple_gate_b': _jnp.float32, 'ple_proj': _jnp.float32}
MOMENT_SCALE = {'ln_gain': 3.212695e+01, 'ln_bias': 6.641921e+00, 'pool_w': 1.494952e-01, 'pool_b': 6.718551e-01, 'pool_scale': 1.031981e-01, 'gdn_w_in': 4.250903e-02, 'gdn_conv': 4.238242e-02, 'gdn_a_log': 3.134466e-01, 'gdn_dt_bias': 3.045735e-01, 'gdn_norm_w': 2.778701e-01, 'gdn_w_out': 1.226849e-01, 'mlp_w1': 5.609394e-02, 'mlp_w2': 3.265213e-01, 'ple_gate_w': 3.324379e-02, 'ple_gate_b': 6.242495e-02, 'ple_proj': 8.397146e-02}


def _to_microbatches(a, axis):
    t = _jnp.moveaxis(a, axis, 0)
    t = t.reshape((N_MICROBATCH, t.shape[0] // N_MICROBATCH) + t.shape[1:])
    return _jnp.moveaxis(t, 1, axis + 1)


def setup_inputs(seed: int = 0) -> dict:
    inp = _fwd_setup_inputs(seed)
    key = _jax.random.fold_in(_jax.random.key(seed), 7919)
    shape, _ = _output_shape()
    out = dict(inp)
    out["loss_target"] = _jax.random.normal(_jax.random.fold_in(key, 0), shape, _jnp.float32)
    for i, name in enumerate(TWIN_WEIGHTS):
        w = inp[name].astype(_jnp.float32)
        if MOMENT_SCALE is None:
            s = _jnp.sqrt(_jnp.mean(_jnp.square(w)) + 1e-30)
        else:
            s = MOMENT_SCALE[name]
        km, kv = _jax.random.split(_jax.random.fold_in(key, i + 1))
        out[name] = w
        out["m_" + name] = s * _jax.random.normal(km, w.shape, _jnp.float32)
        out["v_" + name] = (s * s) * _jax.random.uniform(kv, w.shape, _jnp.float32, 0.5, 1.5)
    if N_MICROBATCH > 1:
        for name, axis in PER_EXAMPLE_BATCH_AXIS.items():
            out[name] = _to_microbatches(out[name], axis)
    return {'x': out['x'], 'p': out['p'], 'ln_gain': out['ln_gain'], 'ln_bias': out['ln_bias'], 'pool_w': out['pool_w'], 'pool_b': out['pool_b'], 'pool_scale': out['pool_scale'], 'gdn_w_in': out['gdn_w_in'], 'gdn_conv': out['gdn_conv'], 'gdn_a_log': out['gdn_a_log'], 'gdn_dt_bias': out['gdn_dt_bias'], 'gdn_norm_w': out['gdn_norm_w'], 'gdn_w_out': out['gdn_w_out'], 'mlp_w1': out['mlp_w1'], 'mlp_w2': out['mlp_w2'], 'ple_gate_w': out['ple_gate_w'], 'ple_gate_b': out['ple_gate_b'], 'ple_proj': out['ple_proj'], 'loss_target': out['loss_target'], 'm_ln_gain': out['m_ln_gain'], 'm_ln_bias': out['m_ln_bias'], 'm_pool_w': out['m_pool_w'], 'm_pool_b': out['m_pool_b'], 'm_pool_scale': out['m_pool_scale'], 'm_gdn_w_in': out['m_gdn_w_in'], 'm_gdn_conv': out['m_gdn_conv'], 'm_gdn_a_log': out['m_gdn_a_log'], 'm_gdn_dt_bias': out['m_gdn_dt_bias'], 'm_gdn_norm_w': out['m_gdn_norm_w'], 'm_gdn_w_out': out['m_gdn_w_out'], 'm_mlp_w1': out['m_mlp_w1'], 'm_mlp_w2': out['m_mlp_w2'], 'm_ple_gate_w': out['m_ple_gate_w'], 'm_ple_gate_b': out['m_ple_gate_b'], 'm_ple_proj': out['m_ple_proj'], 'v_ln_gain': out['v_ln_gain'], 'v_ln_bias': out['v_ln_bias'], 'v_pool_w': out['v_pool_w'], 'v_pool_b': out['v_pool_b'], 'v_pool_scale': out['v_pool_scale'], 'v_gdn_w_in': out['v_gdn_w_in'], 'v_gdn_conv': out['v_gdn_conv'], 'v_gdn_a_log': out['v_gdn_a_log'], 'v_gdn_dt_bias': out['v_gdn_dt_bias'], 'v_gdn_norm_w': out['v_gdn_norm_w'], 'v_gdn_w_out': out['v_gdn_w_out'], 'v_mlp_w1': out['v_mlp_w1'], 'v_mlp_w2': out['v_mlp_w2'], 'v_ple_gate_w': out['v_ple_gate_w'], 'v_ple_gate_b': out['v_ple_gate_b'], 'v_ple_proj': out['v_ple_proj']}


def _loss(weights, diff, rest, loss_target):
    with _jax.named_scope("forward"):
        args = {**rest, TWIN_DIFF_INPUT: diff, **{k: w.astype(_WEIGHT_DTYPES[k]) for k, w in weights.items()}}
        y = _forward(args)
    with _jax.named_scope("loss_head"):
        err = _jnp.square(y.astype(_jnp.float32) - loss_target)
        return 0.5 * _jnp.sum(_jnp.mean(err, axis=-1)) if err.ndim else 0.5 * err


def _adamw(w, g, m, v):
    m = ADAM_B1 * m + (1.0 - ADAM_B1) * g
    v = ADAM_B2 * v + (1.0 - ADAM_B2) * _jnp.square(g)
    m_hat = m / (1.0 - ADAM_B1 ** ADAM_STEP)
    v_hat = v / (1.0 - ADAM_B2 ** ADAM_STEP)
    delta = -ADAM_LR * (m_hat / (_jnp.sqrt(v_hat) + ADAM_EPS) + ADAM_WD * w)
    return delta, m, v


def reference(x, p, ln_gain, ln_bias, pool_w, pool_b, pool_scale, gdn_w_in, gdn_conv, gdn_a_log, gdn_dt_bias, gdn_norm_w, gdn_w_out, mlp_w1, mlp_w2, ple_gate_w, ple_gate_b, ple_proj, loss_target, m_ln_gain, m_ln_bias, m_pool_w, m_pool_b, m_pool_scale, m_gdn_w_in, m_gdn_conv, m_gdn_a_log, m_gdn_dt_bias, m_gdn_norm_w, m_gdn_w_out, m_mlp_w1, m_mlp_w2, m_ple_gate_w, m_ple_gate_b, m_ple_proj, v_ln_gain, v_ln_bias, v_pool_w, v_pool_b, v_pool_scale, v_gdn_w_in, v_gdn_conv, v_gdn_a_log, v_gdn_dt_bias, v_gdn_norm_w, v_gdn_w_out, v_mlp_w1, v_mlp_w2, v_ple_gate_w, v_ple_gate_b, v_ple_proj):
    given = dict(x=x, p=p, ln_gain=ln_gain, ln_bias=ln_bias, pool_w=pool_w, pool_b=pool_b, pool_scale=pool_scale, gdn_w_in=gdn_w_in, gdn_conv=gdn_conv, gdn_a_log=gdn_a_log, gdn_dt_bias=gdn_dt_bias, gdn_norm_w=gdn_norm_w, gdn_w_out=gdn_w_out, mlp_w1=mlp_w1, mlp_w2=mlp_w2, ple_gate_w=ple_gate_w, ple_gate_b=ple_gate_b, ple_proj=ple_proj, loss_target=loss_target, m_ln_gain=m_ln_gain, m_ln_bias=m_ln_bias, m_pool_w=m_pool_w, m_pool_b=m_pool_b, m_pool_scale=m_pool_scale, m_gdn_w_in=m_gdn_w_in, m_gdn_conv=m_gdn_conv, m_gdn_a_log=m_gdn_a_log, m_gdn_dt_bias=m_gdn_dt_bias, m_gdn_norm_w=m_gdn_norm_w, m_gdn_w_out=m_gdn_w_out, m_mlp_w1=m_mlp_w1, m_mlp_w2=m_mlp_w2, m_ple_gate_w=m_ple_gate_w, m_ple_gate_b=m_ple_gate_b, m_ple_proj=m_ple_proj, v_ln_gain=v_ln_gain, v_ln_bias=v_ln_bias, v_pool_w=v_pool_w, v_pool_b=v_pool_b, v_pool_scale=v_pool_scale, v_gdn_w_in=v_gdn_w_in, v_gdn_conv=v_gdn_conv, v_gdn_a_log=v_gdn_a_log, v_gdn_dt_bias=v_gdn_dt_bias, v_gdn_norm_w=v_gdn_norm_w, v_gdn_w_out=v_gdn_w_out, v_mlp_w1=v_mlp_w1, v_mlp_w2=v_mlp_w2, v_ple_gate_w=v_ple_gate_w, v_ple_gate_b=v_ple_gate_b, v_ple_proj=v_ple_proj)
    weights = {n: given[n] for n in TWIN_WEIGHTS}
    shared = {n: given[n] for n in SHARED_INPUTS}
    per_example = {n: given[n] for n in ['x', 'p']}
    grad_fn = _jax.value_and_grad(_loss, argnums=(0, 1))

    def one_microbatch(ex, loss_target):
        ex = dict(ex)
        diff = ex.pop(TWIN_DIFF_INPUT)
        return grad_fn(weights, diff, {**shared, **ex}, loss_target)

    if N_MICROBATCH == 1:
        loss, (grad_w, grad_x) = one_microbatch(per_example, given["loss_target"])
    else:
        def body(carry, xs):
            loss_sum, grad_sum = carry
            l_k, (gw_k, gx_k) = one_microbatch(xs[0], xs[1])
            with _jax.named_scope("update"):
                return (loss_sum + l_k, _jax.tree.map(_jnp.add, grad_sum, gw_k)), gx_k

        init = (_jnp.zeros((), _jnp.float32), _jax.tree.map(_jnp.zeros_like, weights))
        (loss, grad_w), grad_x = _jax.lax.scan(body, init, (per_example, given["loss_target"]))
    with _jax.named_scope("update"):
        delta_w, new_m, new_v = {}, {}, {}
        for n in TWIN_WEIGHTS:
            delta_w[n], new_m[n], new_v[n] = _adamw(weights[n], grad_w[n], given["m_" + n], given["v_" + n])
    return (loss, grad_x, *[grad_w[n] for n in TWIN_WEIGHTS], *[delta_w[n] for n in TWIN_WEIGHTS],
            *[new_m[n] for n in TWIN_WEIGHTS], *[new_v[n] for n in TWIN_WEIGHTS])
```

```python
import functools
import math

import jax
import jax.numpy as jnp
from jax import lax
from jax.experimental import pallas as pl
from jax.experimental.pallas import tpu as pltpu

F32 = jnp.float32
BF16 = jnp.bfloat16

D_MODEL = 1024
D_FF = 4096
PLE_DIM = 256
N_SHARD = 4
POOL_WINDOWS = (2, 4, 8, 16)
POOL_GROUP = 256
POOL_HALO = 16
HEADS = 8
HEAD_DIM = 128
CHUNK = 64
CONV_WIDTH = 4
CONV_HALO = 8
QKV_DIM = 3 * D_MODEL
GDN_IN_DIM = QKV_DIM + D_MODEL + 2 * HEADS
GDN_IN_PAD = 4224
BA_BLOCK = (QKV_DIM + D_MODEL) // 128
ALPHA = (2.0 * 2) ** 0.25
LN_EPS = 1e-5
RMS_EPS = 1e-6
L2_EPS = 1e-6
ADAM_LR, ADAM_B1, ADAM_B2, ADAM_EPS, ADAM_WD, ADAM_STEP = 0.001, 0.9, 0.999, 1e-08, 0.01, 10

ROW_TILE = 512
CONV_TILE = 256
PREP_CHUNKS = 4
LANES = 1024
ADAM_ROWS = 256

NN = (((1,), (0,)), ((), ()))
NT = (((1,), (1,)), ((), ()))
TN = (((0,), (0,)), ((), ()))
BNN = (((2,), (1,)), ((0,), (0,)))
BNT = (((2,), (2,)), ((0,), (0,)))
BTN = (((1,), (1,)), ((0,), (0,)))
MESH = pl.DeviceIdType.MESH
ANY = pl.BlockSpec(memory_space=pl.ANY)


def _bdot(a, b, dims):
    return lax.dot_general(a.astype(BF16), b.astype(BF16), dims, preferred_element_type=F32)


def _hdot(a, b, dims):
    return lax.dot_general(a, b, dims, precision=lax.Precision.HIGHEST, preferred_element_type=F32)


def _sigmoid(x):
    return 1.0 / (1.0 + jnp.exp(-x))


def _silu(x):
    return x * _sigmoid(x)


def _softplus(x):
    return jnp.maximum(x, 0.0) + jnp.log1p(jnp.exp(-jnp.abs(x)))


def _call(body, name, grid, in_specs, out_specs, out_shape, scratch=(), sem=None):
    params = pltpu.CompilerParams(dimension_semantics=sem) if sem else None
    return pl.pallas_call(
        body, name=name, grid=grid, in_specs=in_specs, out_specs=out_specs, out_shape=out_shape,
        scratch_shapes=list(scratch), compiler_params=params)


def _row(d):
    return pl.BlockSpec((1, d), lambda *_: (0, 0))


def _full(shape):
    n = len(shape)
    return pl.BlockSpec(shape, lambda *_: (0,) * n)


def _sds(shape, dtype=F32):
    return jax.ShapeDtypeStruct(shape, dtype)


def _ln_fwd(r, gain, bias):
    mu = jnp.mean(r, axis=-1, keepdims=True)
    xc = r - mu
    rstd = lax.rsqrt(jnp.mean(xc * xc, axis=-1, keepdims=True) + LN_EPS)
    xhat = xc * rstd
    return xhat * gain + bias, xhat, rstd


def _ln_bwd(dy, xhat, rstd, gain):
    dxh = dy * gain
    m1 = jnp.mean(dxh, axis=-1, keepdims=True)
    m2 = jnp.mean(dxh * xhat, axis=-1, keepdims=True)
    return rstd * (dxh - m1 - xhat * m2)


def _acc(ref, first, val):
    @pl.when(first)
    def _():
        ref[...] = val

    @pl.when(jnp.logical_not(first))
    def _():
        ref[...] += val


def _pooled_groups(xe, t0, ts):
    pos = (t0 + lax.broadcasted_iota(jnp.int32, (ts, 1), 0) + 1).astype(F32)
    outs = []
    for gi, win in enumerate(POOL_WINDOWS):
        xs = xe[:, gi * POOL_GROUP:(gi + 1) * POOL_GROUP]
        s, k = xs, 1
        while k < win:
            s = s + pltpu.roll(s, k, 0)
            k *= 2
        mean = s[POOL_HALO:] / jnp.minimum(pos, float(win))
        outs.append(mean - xs[POOL_HALO:])
    return outs


def _pool_fwd(x, wp, pb, ps, gain, bias):
    s_len = x.shape[0]
    ts = min(ROW_TILE, s_len)
    hb = ts // POOL_HALO

    def body(x_ref, halo_ref, w_ref, pb_ref, ps_ref, g_ref, b_ref, y_ref, xhat_ref, rstd_ref):
        i = pl.program_id(0)
        x_t = x_ref[...]
        halo = jnp.where(i > 0, halo_ref[...], 0.0)
        pooled = _pooled_groups(jnp.concatenate([halo, x_t], axis=0), i * ts, ts)
        y = jnp.concatenate([_bdot(pooled[g], w_ref[g], NN) for g in range(4)], axis=1) + pb_ref[...]
        r = ALPHA * x_t + y * ps_ref[...]
        y_ref[...], xhat_ref[...], rstd_ref[...] = _ln_fwd(r, g_ref[...], b_ref[...])

    tile = pl.BlockSpec((ts, D_MODEL), lambda i: (i, 0))
    return _call(
        body, "pool_fwd", (s_len // ts,),
        [tile, pl.BlockSpec((POOL_HALO, D_MODEL), lambda i: (jnp.maximum(i * hb - 1, 0), 0)),
         _full((4, POOL_GROUP, POOL_GROUP)), _row(D_MODEL), _row(D_MODEL), _row(D_MODEL), _row(D_MODEL)],
        [tile, tile, pl.BlockSpec((ts, 1), lambda i: (i, 0))],
        [_sds((s_len, D_MODEL)), _sds((s_len, D_MODEL)), _sds((s_len, 1))],
        sem=("parallel",))(x, x, wp, pb, ps, gain, bias)


def _pool_bwd(dy, xhat, rstd, x, wp, pb, ps, gain):
    s_len = x.shape[0]
    ts = min(ROW_TILE, s_len)
    hb = ts // POOL_HALO
    n_t = s_len // ts
    ne = ts + POOL_HALO

    def body(dy_ref, dyn_ref, xh_ref, xhn_ref, rs_ref, rsn_ref, x_ref, xp_ref, w_ref, pb_ref, ps_ref, g_ref,
             dx_ref, dg_ref, db_ref, dps_ref, dpb_ref, dw_ref):
        i = pl.program_id(0)
        more = i < n_t - 1
        dy_t, xh_t = dy_ref[...], xh_ref[...]
        dy_e = jnp.concatenate([dy_t, jnp.where(more, dyn_ref[...], 0.0)], axis=0)
        xh_e = jnp.concatenate([xh_t, xhn_ref[...]], axis=0)
        rs_e = jnp.concatenate([rs_ref[...], rsn_ref[...]], axis=0)
        dr_e = _ln_bwd(dy_e, xh_e, rs_e, g_ref[...])
        dyy_e = dr_e * ps_ref[...]
        pos_e = (i * ts + lax.broadcasted_iota(jnp.int32, (ne, 1), 0) + 1).astype(F32)
        dxs = []
        for gi, win in enumerate(POOL_WINDOWS):
            sl = slice(gi * POOL_GROUP, (gi + 1) * POOL_GROUP)
            dpool = _bdot(dyy_e[:, sl], w_ref[gi], NT)
            s, k = dpool / jnp.minimum(pos_e, float(win)), 1
            while k < win:
                s = s + pltpu.roll(s, ne - k, 0)
                k *= 2
            dxs.append(s[:ts] - dpool[:ts])
        dx_ref[...] = ALPHA * dr_e[:ts] + jnp.concatenate(dxs, axis=1)

        x_t = x_ref[...]
        halo = jnp.where(i > 0, xp_ref[...], 0.0)
        pooled = _pooled_groups(jnp.concatenate([halo, x_t], axis=0), i * ts, ts)
        y = jnp.concatenate([_bdot(pooled[g], w_ref[g], NN) for g in range(4)], axis=1) + pb_ref[...]
        dr_t, dyy_t = dr_e[:ts], dyy_e[:ts]
        first = i == 0
        _acc(dg_ref, first, jnp.sum(dy_t * xh_t, axis=0, keepdims=True))
        _acc(db_ref, first, jnp.sum(dy_t, axis=0, keepdims=True))
        _acc(dps_ref, first, jnp.sum(dr_t * y, axis=0, keepdims=True))
        _acc(dpb_ref, first, jnp.sum(dyy_t, axis=0, keepdims=True))
        for g in range(4):
            _acc(dw_ref.at[g], first, _bdot(pooled[g], dyy_t[:, g * POOL_GROUP:(g + 1) * POOL_GROUP], TN))

    tile = pl.BlockSpec((ts, D_MODEL), lambda i: (i, 0))
    nxt = pl.BlockSpec((POOL_HALO, D_MODEL), lambda i: (jnp.minimum((i + 1) * hb, n_t * hb - 1), 0))
    prv = pl.BlockSpec((POOL_HALO, D_MODEL), lambda i: (jnp.maximum(i * hb - 1, 0), 0))
    rs_t = pl.BlockSpec((ts, 1), lambda i: (i, 0))
    rs_n = pl.BlockSpec((POOL_HALO, 1), lambda i: (jnp.minimum((i + 1) * hb, n_t * hb - 1), 0))
    row = _row(D_MODEL)
    return _call(
        body, "pool_bwd", (n_t,),
        [tile, nxt, tile, nxt, rs_t, rs_n, tile, prv, _full((4, POOL_GROUP, POOL_GROUP)), row, row, row],
        [tile, row, row, row, row, _full((4, POOL_GROUP, POOL_GROUP))],
        [_sds((s_len, D_MODEL))] + [_sds((1, D_MODEL))] * 4 + [_sds((4, POOL_GROUP, POOL_GROUP))],
        sem=("arbitrary",))(dy, dy, xhat, xhat, rstd, rstd, x, x, wp, pb, ps, gain)


def _mlp_fwd(x, p, w1s, w2s, gw, gb, proj, gain, bias):
    s_len = x.shape[0]
    ts = min(ROW_TILE, s_len)
    fc = D_FF // N_SHARD

    def body(x_ref, p_ref, w1_ref, w2_ref, gw_ref, gb_ref, pj_ref, g_ref, b_ref, y_ref, xhat_ref, rstd_ref,
             acc_ref, xb_ref):
        j = pl.program_id(1)

        @pl.when(j == 0)
        def _():
            x_t = x_ref[...]
            xb_ref[...] = x_t.astype(BF16)
            gate = _sigmoid(_bdot(x_t, gw_ref[...], NN) + gb_ref[...])
            acc_ref[...] = ALPHA * x_t + gate * _bdot(p_ref[...], pj_ref[...], NN)

        h = jnp.maximum(_bdot(xb_ref[...], w1_ref[0], NN), 0.0)
        acc_ref[...] += _bdot(h * h, w2_ref[0], NN)

        @pl.when(j == N_SHARD - 1)
        def _():
            y_ref[...], xhat_ref[...], rstd_ref[...] = _ln_fwd(acc_ref[...], g_ref[...], b_ref[...])

    tile = pl.BlockSpec((ts, D_MODEL), lambda i, j: (i, 0))
    row = _row(D_MODEL)
    return _call(
        body, "mlp_fwd", (s_len // ts, N_SHARD),
        [tile, pl.BlockSpec((ts, PLE_DIM), lambda i, j: (i, 0)),
         pl.BlockSpec((1, D_MODEL, fc), lambda i, j: (j, 0, 0)), pl.BlockSpec((1, fc, D_MODEL), lambda i, j: (j, 0, 0)),
         _full((D_MODEL, D_MODEL)), row, _full((PLE_DIM, D_MODEL)), row, row],
        [tile, tile, pl.BlockSpec((ts, 1), lambda i, j: (i, 0))],
        [_sds((s_len, D_MODEL)), _sds((s_len, D_MODEL)), _sds((s_len, 1))],
        scratch=[pltpu.VMEM((ts, D_MODEL), F32), pltpu.VMEM((ts, D_MODEL), BF16)],
        sem=("parallel", "arbitrary"))(x, p, w1s, w2s, gw, gb, proj, gain, bias)


def _mlp_bwd(dy, xhat, rstd, x, p, w1s, w2s, gw, gb, proj, gain):
    s_len = x.shape[0]
    ts = min(ROW_TILE, s_len)
    fc = D_FF // N_SHARD

    def body(dy_ref, xh_ref, rs_ref, x_ref, p_ref, w1_ref, w2_ref, gw_ref, gb_ref, pj_ref, g_ref,
             dx_ref, a_ref, dh_ref, dzg_ref, dpp_ref, drb_ref, dg_ref, db_ref, dgb_ref, acc_ref, xb_ref, dr_ref):
        i, j = pl.program_id(0), pl.program_id(1)

        @pl.when(j == 0)
        def _():
            dy_t, xh_t, x_t = dy_ref[...], xh_ref[...], x_ref[...]
            dr = _ln_bwd(dy_t, xh_t, rs_ref[...], g_ref[...])
            xb_ref[...] = x_t.astype(BF16)
            drb = dr.astype(BF16)
            dr_ref[...] = drb
            drb_ref[...] = drb
            gate = _sigmoid(_bdot(x_t, gw_ref[...], NN) + gb_ref[...])
            pp = _bdot(p_ref[...], pj_ref[...], NN)
            dzg = dr * pp * gate * (1.0 - gate)
            dzg_ref[...] = dzg.astype(BF16)
            dpp_ref[...] = (dr * gate).astype(BF16)
            acc_ref[...] = ALPHA * dr + _bdot(dzg, gw_ref[...], NT)
            first = i == 0
            _acc(dg_ref, first, jnp.sum(dy_t * xh_t, axis=0, keepdims=True))
            _acc(db_ref, first, jnp.sum(dy_t, axis=0, keepdims=True))
            _acc(dgb_ref, first, jnp.sum(dzg, axis=0, keepdims=True))

        h = jnp.maximum(_bdot(xb_ref[...], w1_ref[0], NN), 0.0)
        a_ref[...] = (h * h).astype(BF16)
        dh = (_bdot(dr_ref[...], w2_ref[0], NT) * (2.0 * h)).astype(BF16)
        dh_ref[...] = dh
        acc_ref[...] += _bdot(dh, w1_ref[0], NT)

        @pl.when(j == N_SHARD - 1)
        def _():
            dx_ref[...] = acc_ref[...]

    tile = pl.BlockSpec((ts, D_MODEL), lambda i, j: (i, 0))
    ftile = pl.BlockSpec((ts, fc), lambda i, j: (i, j))
    row = _row(D_MODEL)
    return _call(
        body, "mlp_bwd", (s_len // ts, N_SHARD),
        [tile, tile, pl.BlockSpec((ts, 1), lambda i, j: (i, 0)), tile, pl.BlockSpec((ts, PLE_DIM), lambda i, j: (i, 0)),
         pl.BlockSpec((1, D_MODEL, fc), lambda i, j: (j, 0, 0)), pl.BlockSpec((1, fc, D_MODEL), lambda i, j: (j, 0, 0)),
         _full((D_MODEL, D_MODEL)), row, _full((PLE_DIM, D_MODEL)), row],
        [tile, ftile, ftile, tile, tile, tile, row, row, row],
        [_sds((s_len, D_MODEL)), _sds((s_len, D_FF), BF16), _sds((s_len, D_FF), BF16)]
        + [_sds((s_len, D_MODEL), BF16)] * 3 + [_sds((1, D_MODEL))] * 3,
        scratch=[pltpu.VMEM((ts, D_MODEL), F32), pltpu.VMEM((ts, D_MODEL), BF16), pltpu.VMEM((ts, D_MODEL), BF16)],
        sem=("arbitrary", "arbitrary"))(dy, xhat, rstd, x, p, w1s, w2s, gw, gb, proj, gain)


def _wgrad(a, b, name, stack_cols=False):
    s_len, m = a.shape
    n = b.shape[1]
    ts = min(512, s_len)
    tm = min(m, 1024)
    tn = n // N_SHARD if stack_cols else (1408 if n == GDN_IN_PAD else min(n, 1024))
    n_s = s_len // ts

    def body(a_ref, b_ref, o_ref):
        _acc(o_ref, pl.program_id(2) == 0, _bdot(a_ref[...], b_ref[...], TN))

    if stack_cols:
        out_spec = pl.BlockSpec((None, tm, tn), lambda mi, nj, s: (nj, mi, 0))
        out_shape = _sds((N_SHARD, m, tn))
    else:
        out_spec = pl.BlockSpec((tm, tn), lambda mi, nj, s: (mi, nj))
        out_shape = _sds((m, n))
    return _call(
        body, name, (m // tm, n // tn, n_s),
        [pl.BlockSpec((ts, tm), lambda mi, nj, s: (s, mi)), pl.BlockSpec((ts, tn), lambda mi, nj, s: (s, nj))],
        out_spec, out_shape, sem=("parallel", "parallel", "arbitrary"))(a, b)


def _matmul_nn(a, b, name, tn):
    s_len, k = a.shape
    n = b.shape[1]
    ts = min(512, s_len)

    def body(a_ref, b_ref, o_ref):
        o_ref[...] = _bdot(a_ref[...], b_ref[...], NN)

    return _call(
        body, name, (s_len // ts, n // tn),
        [pl.BlockSpec((ts, k), lambda i, j: (i, 0)), pl.BlockSpec((k, tn), lambda i, j: (0, j))],
        pl.BlockSpec((ts, tn), lambda i, j: (i, j)), _sds((s_len, n)), sem=("parallel", "parallel"))(a, b)


def _matmul_nt_add(a, b, res, name):
    s_len, k = a.shape
    n = b.shape[0]
    ts = min(512, s_len)

    def body(a_ref, b_ref, r_ref, o_ref):
        o_ref[...] = r_ref[...] + _bdot(a_ref[...], b_ref[...], NT)

    return _call(
        body, name, (s_len // ts,),
        [pl.BlockSpec((ts, k), lambda i: (i, 0)), _full((n, k)), pl.BlockSpec((ts, n), lambda i: (i, 0))],
        pl.BlockSpec((ts, n), lambda i: (i, 0)), _sds((s_len, n)), sem=("parallel",))(a, b, res)


def _act_qkv(y):
    qkv = _silu(y)
    qs, ks = [], []
    for h in range(HEADS):
        qh = qkv[:, h * HEAD_DIM:(h + 1) * HEAD_DIM]
        kh = qkv[:, D_MODEL + h * HEAD_DIM:D_MODEL + (h + 1) * HEAD_DIM]
        qs.append(qh * (lax.rsqrt(jnp.sum(qh * qh, axis=-1, keepdims=True) + L2_EPS) * HEAD_DIM ** -0.5))
        ks.append(kh * lax.rsqrt(jnp.sum(kh * kh, axis=-1, keepdims=True) + L2_EPS))
    return jnp.concatenate(qs, axis=1), jnp.concatenate(ks, axis=1), qkv[:, 2 * D_MODEL:]


def _act_gb(ba, alog_l, dtb_l, tril):
    lane = lax.broadcasted_iota(jnp.int32, ba.shape, 1)
    g = jnp.where((lane >= HEADS) & (lane < 2 * HEADS), -jnp.exp(alog_l) * _softplus(ba + dtb_l), 0.0)
    return jnp.where(lane < HEADS, _sigmoid(ba), _hdot(tril, g, NN))


def _chunk_tril(t):
    ii = lax.broadcasted_iota(jnp.int32, (t, t), 0)
    jj = lax.broadcasted_iota(jnp.int32, (t, t), 1)
    return ((ii // CHUNK == jj // CHUNK) & (ii >= jj)).astype(F32)


def _conv_rows(xe, w, n_rows):
    y = xe[CONV_HALO:CONV_HALO + n_rows] * w[CONV_WIDTH - 1]
    for j in range(CONV_WIDTH - 1):
        y = y + pltpu.roll(xe, CONV_WIDTH - 1 - j, 0)[CONV_HALO:CONV_HALO + n_rows] * w[j]
    return y


def _conv_fwd(proj, conv_w, alog_l, dtb_l):
    s_len = proj.shape[0]
    ts = min(CONV_TILE, s_len)
    hb = ts // CONV_HALO

    def body(x_ref, xp_ref, ba_ref, w_ref, al_ref, dt_ref, q_ref, k_ref, v_ref, gcb_ref):
        i = pl.program_id(0)
        halo = jnp.where(i > 0, xp_ref[...], 0.0)
        taps = [w_ref[pl.ds(j, 1), :] for j in range(CONV_WIDTH)]
        y = _conv_rows(jnp.concatenate([halo, x_ref[...]], axis=0), taps, ts)
        q_ref[...], k_ref[...], v_ref[...] = _act_qkv(y)
        gcb_ref[...] = _act_gb(ba_ref[...], al_ref[...], dt_ref[...], _chunk_tril(ts))

    tile = pl.BlockSpec((ts, D_MODEL), lambda i: (i, 0))
    return _call(
        body, "gdn_conv_fwd", (s_len // ts,),
        [pl.BlockSpec((ts, QKV_DIM), lambda i: (i, 0)),
         pl.BlockSpec((CONV_HALO, QKV_DIM), lambda i: (jnp.maximum(i * hb - 1, 0), 0)),
         pl.BlockSpec((ts, 128), lambda i: (i, BA_BLOCK)), _full((CONV_WIDTH, QKV_DIM)), _row(128), _row(128)],
        [tile, tile, tile, pl.BlockSpec((ts, 128), lambda i: (i, 0))],
        [_sds((s_len, D_MODEL))] * 3 + [_sds((s_len, 128))],
        sem=("parallel",))(proj, proj, proj, conv_w, alog_l, dtb_l)


def _conv_bwd(proj, dq, dk, dv, dgcb, dz, conv_w, alog_l, dtb_l):
    s_len = proj.shape[0]
    ts = min(CONV_TILE, s_len)
    hb = ts // CONV_HALO
    n_t = s_len // ts
    te = ts + CONV_HALO

    def body(x_ref, xp_ref, xn_ref, ba_ref, dq_ref, dqn_ref, dk_ref, dkn_ref, dv_ref, dvn_ref, dgcb_ref, dz_ref,
             w_ref, al_ref, dt_ref, dp_ref, dw_ref, dal_ref, ddt_ref):
        i = pl.program_id(0)
        more = i < n_t - 1
        w = [w_ref[pl.ds(j, 1), :] for j in range(CONV_WIDTH)]
        x_t = x_ref[...]
        xe = jnp.concatenate([jnp.where(i > 0, xp_ref[...], 0.0), x_t, xn_ref[...]], axis=0)
        y_e, act_vjp = jax.vjp(_act_qkv, _conv_rows(xe, w, te))
        ct = tuple(jnp.concatenate([t[...], jnp.where(more, n[...], 0.0)], axis=0)
                   for t, n in ((dq_ref, dqn_ref), (dk_ref, dkn_ref), (dv_ref, dvn_ref)))
        (dy_e,) = act_vjp(ct)
        dx = dy_e[:ts] * w[CONV_WIDTH - 1]
        for j in range(CONV_WIDTH - 1):
            dx = dx + pltpu.roll(dy_e, te - (CONV_WIDTH - 1 - j), 0)[:ts] * w[j]
        dy_t = dy_e[:ts]
        xe_t = xe[:te]
        dws = [jnp.sum(dy_t * pltpu.roll(xe_t, CONV_WIDTH - 1 - j, 0)[CONV_HALO:], axis=0, keepdims=True)
               for j in range(CONV_WIDTH - 1)]
        dws.append(jnp.sum(dy_t * x_t, axis=0, keepdims=True))
        _, gb_vjp = jax.vjp(lambda ba, al, dt: _act_gb(ba, al, dt, _chunk_tril(ts)), ba_ref[...], al_ref[...], dt_ref[...])
        dba, dal, ddt = gb_vjp(dgcb_ref[...])
        dp_ref[...] = jnp.concatenate([dx.astype(BF16), dz_ref[...], dba.astype(BF16)], axis=1)
        first = i == 0
        for j in range(CONV_WIDTH):
            _acc(dw_ref.at[pl.ds(j, 1), :], first, dws[j])
        _acc(dal_ref, first, dal)
        _acc(ddt_ref, first, ddt)

    tile = pl.BlockSpec((ts, D_MODEL), lambda i: (i, 0))
    nxt = pl.BlockSpec((CONV_HALO, D_MODEL), lambda i: (jnp.minimum((i + 1) * hb, n_t * hb - 1), 0))
    return _call(
        body, "gdn_conv_bwd", (n_t,),
        [pl.BlockSpec((ts, QKV_DIM), lambda i: (i, 0)),
         pl.BlockSpec((CONV_HALO, QKV_DIM), lambda i: (jnp.maximum(i * hb - 1, 0), 0)),
         pl.BlockSpec((CONV_HALO, QKV_DIM), lambda i: (jnp.minimum((i + 1) * hb, n_t * hb - 1), 0)),
         pl.BlockSpec((ts, 128), lambda i: (i, BA_BLOCK)),
         tile, nxt, tile, nxt, tile, nxt, pl.BlockSpec((ts, 128), lambda i: (i, 0)), tile,
         _full((CONV_WIDTH, QKV_DIM)), _row(128), _row(128)],
        [pl.BlockSpec((ts, GDN_IN_PAD), lambda i: (i, 0)), _full((CONV_WIDTH, QKV_DIM)), _row(128), _row(128)],
        [_sds((s_len, GDN_IN_PAD), BF16), _sds((CONV_WIDTH, QKV_DIM)), _sds((1, 128)), _sds((1, 128))],
        sem=("arbitrary",))(proj, proj, proj, proj, dq, dq, dk, dk, dv, dv, dgcb, dz, conv_w, alog_l, dtb_l)


def _tri_inv(a_strict):
    ii = lax.broadcasted_iota(jnp.int32, (CHUNK, CHUNK), 0)
    jj = lax.broadcasted_iota(jnp.int32, (CHUNK, CHUNK), 1)
    x = (ii == jj).astype(F32) - a_strict
    pw = _hdot(a_strict, a_strict, BNN)
    for step in range(5):
        x = x + _hdot(x, pw, BNN)
        if step < 4:
            pw = _hdot(pw, pw, BNN)
    return x


@jax.custom_vjp
def _tri_solve(a_strict, rhs):
    return _hdot(_tri_inv(a_strict), rhs, BNN)


def _tri_solve_fwd(a_strict, rhs):
    t = _tri_inv(a_strict)
    sol = _hdot(t, rhs, BNN)
    return sol, (t, sol)


def _tri_solve_bwd(res, dsol):
    t, sol = res
    drhs = _hdot(t, dsol, BTN)
    return -_hdot(drhs, sol, BNT), drhs


_tri_solve.defvjp(_tri_solve_fwd, _tri_solve_bwd)


def _prep(q, k, v, gc, beta):
    ii = lax.broadcasted_iota(jnp.int32, (CHUNK, CHUNK), 0)
    jj = lax.broadcasted_iota(jnp.int32, (CHUNK, CHUNK), 1)
    causal, strict = ii >= jj, ii > jj
    gc_row = jnp.sum((ii == jj).astype(F32) * gc, axis=1, keepdims=True)
    decay = jnp.where(causal, jnp.exp(jnp.where(causal, gc - gc_row, 0.0)), 0.0)
    kb = k * beta
    a = jnp.where(strict, _bdot(kb, k, BNT) * decay, 0.0)
    eg = jnp.exp(gc)
    sol = _tri_solve(a, jnp.concatenate([v * beta, kb * eg], axis=-1))
    qk = _bdot(q, k, BNT) * decay
    last = lax.broadcasted_iota(jnp.int32, (CHUNK, 1), 0) == CHUNK - 1
    g_last = jnp.sum(jnp.where(last, gc, 0.0), axis=1, keepdims=True)
    kd = k * jnp.exp(g_last - gc)
    gl = jnp.exp(g_last) + jnp.zeros((1, 1, HEAD_DIM), F32)
    return sol[..., :HEAD_DIM], sol[..., HEAD_DIM:], qk, q * eg, kd, gl


def _prep_specs(s_len):
    rows = min(PREP_CHUNKS, s_len // CHUNK) * CHUNK
    m = rows // CHUNK
    hd = pl.BlockSpec((rows, HEAD_DIM), lambda h, c: (c, h))
    col = pl.BlockSpec((None, rows, 1), lambda h, c: (h, c, 0))
    qk = pl.BlockSpec((None, rows, CHUNK), lambda h, c: (h, c, 0))
    gl = pl.BlockSpec((None, None, m, HEAD_DIM), lambda h, c: (h, c, 0, 0))
    return rows, m, hd, col, qk, gl


def _gdn_prep(q, k, v, gc_t, beta_t):
    s_len = q.shape[0]
    rows, m, hd, col, qk_spec, gl_spec = _prep_specs(s_len)

    def body(q_ref, k_ref, v_ref, gc_ref, bt_ref, u_ref, w_ref, qd_ref, kd_ref, qk_ref, gl_ref):
        r3 = lambda ref, d: ref[...].reshape(m, CHUNK, d)
        u, w, qk, qd, kd, gl = _prep(r3(q_ref, HEAD_DIM), r3(k_ref, HEAD_DIM), r3(v_ref, HEAD_DIM), r3(gc_ref, 1), r3(bt_ref, 1))
        u_ref[...] = u.reshape(rows, HEAD_DIM)
        w_ref[...] = w.reshape(rows, HEAD_DIM).astype(BF16)
        qd_ref[...] = qd.reshape(rows, HEAD_DIM).astype(BF16)
        kd_ref[...] = kd.reshape(rows, HEAD_DIM).astype(BF16)
        qk_ref[...] = qk.reshape(rows, CHUNK).astype(BF16)
        gl_ref[...] = gl.reshape(m, HEAD_DIM)

    n_g = s_len // rows
    return _call(
        body, "gdn_prep", (HEADS, n_g), [hd, hd, hd, col, col], [hd, hd, hd, hd, qk_spec, gl_spec],
        [_sds((s_len, D_MODEL))] + [_sds((s_len, D_MODEL), BF16)] * 3
        + [_sds((HEADS, s_len, CHUNK), BF16), _sds((HEADS, n_g, m, HEAD_DIM))],
        sem=("parallel", "parallel"))(q, k, v, gc_t, beta_t)


def _gdn_prep_bwd(q, k, v, gc_t, beta_t, du, dw, dqd, dkd, dqk, dgl):
    s_len = q.shape[0]
    rows, m, hd, col, qk_spec, gl_spec = _prep_specs(s_len)

    def body(q_ref, k_ref, v_ref, gc_ref, bt_ref, du_ref, dw_ref, dqd_ref, dkd_ref, dqk_ref, dgl_ref,
             dq_ref, dk_ref, dv_ref, dgc_ref, dbt_ref):
        r3 = lambda ref, d: ref[...].reshape(m, CHUNK, d)
        _, vjp = jax.vjp(_prep, r3(q_ref, HEAD_DIM), r3(k_ref, HEAD_DIM), r3(v_ref, HEAD_DIM), r3(gc_ref, 1), r3(bt_ref, 1))
        ct = (r3(du_ref, HEAD_DIM), r3(dw_ref, HEAD_DIM), r3(dqk_ref, CHUNK), r3(dqd_ref, HEAD_DIM), r3(dkd_ref, HEAD_DIM),
              dgl_ref[...].reshape(m, 1, HEAD_DIM) * (1.0 / HEAD_DIM))
        dq, dk, dv, dgc, dbt = vjp(ct)
        dq_ref[...] = dq.reshape(rows, HEAD_DIM)
        dk_ref[...] = dk.reshape(rows, HEAD_DIM)
        dv_ref[...] = dv.reshape(rows, HEAD_DIM)
        dgc_ref[...] = dgc.reshape(rows, 1)
        dbt_ref[...] = dbt.reshape(rows, 1)

    return _call(
        body, "gdn_prep_bwd", (HEADS, s_len // rows),
        [hd, hd, hd, col, col, hd, hd, hd, hd, qk_spec, gl_spec], [hd, hd, hd, col, col],
        [_sds((s_len, D_MODEL))] * 3 + [_sds((HEADS, s_len, 1))] * 2,
        sem=("parallel", "parallel"))(q, k, v, gc_t, beta_t, du, dw, dqd, dkd, dqk, dgl)


def _scan_specs():
    row = pl.BlockSpec((CHUNK, D_MODEL), lambda n: (n, 0))
    qk = pl.BlockSpec((HEADS, CHUNK, CHUNK), lambda n: (0, n, 0))
    gl = pl.BlockSpec((1, HEADS, HEAD_DIM), lambda n: (n, 0, 0))
    st = pl.BlockSpec((1, HEADS, HEAD_DIM, HEAD_DIM), lambda n: (n, 0, 0, 0))
    return row, qk, gl, st


def _gdn_scan(u, w, qd, kd, qk, gl):
    s_len = u.shape[0]
    n_c = s_len // CHUNK
    row, qk_spec, gl_spec, st_spec = _scan_specs()

    def body(u_ref, w_ref, qd_ref, kd_ref, qk_ref, gl_ref, o_ref, st_ref, state):
        @pl.when(pl.program_id(0) == 0)
        def _():
            state[...] = jnp.zeros_like(state)

        for h in range(HEADS):
            sl = slice(h * HEAD_DIM, (h + 1) * HEAD_DIM)
            s_f = state[h]
            st_ref[0, h] = s_f
            s_b = s_f.astype(BF16)
            vn = (u_ref[:, sl] - _bdot(w_ref[:, sl], s_b, NN)).astype(BF16)
            o_ref[:, sl] = _bdot(qd_ref[:, sl], s_b, NN) + _bdot(qk_ref[h], vn, NN)
            state[h] = s_f * gl_ref[0, pl.ds(h, 1), :] + _bdot(kd_ref[:, sl], vn, TN)

    return _call(
        body, "gdn_scan", (n_c,), [row, row, row, row, qk_spec, gl_spec], [row, st_spec],
        [_sds((s_len, D_MODEL)), _sds((n_c, HEADS, HEAD_DIM, HEAD_DIM))],
        scratch=[pltpu.VMEM((HEADS, HEAD_DIM, HEAD_DIM), F32)], sem=("arbitrary",))(u, w, qd, kd, qk, gl)


def _gdn_scan_bwd(do, u, w, qd, kd, qk, gl, states):
    s_len = u.shape[0]
    n_c = s_len // CHUNK
    row = pl.BlockSpec((CHUNK, D_MODEL), lambda n: (n_c - 1 - n, 0))
    qk_spec = pl.BlockSpec((HEADS, CHUNK, CHUNK), lambda n: (0, n_c - 1 - n, 0))
    gl_spec = pl.BlockSpec((1, HEADS, HEAD_DIM), lambda n: (n_c - 1 - n, 0, 0))
    st_spec = pl.BlockSpec((1, HEADS, HEAD_DIM, HEAD_DIM), lambda n: (n_c - 1 - n, 0, 0, 0))

    def body(do_ref, u_ref, w_ref, qd_ref, kd_ref, qk_ref, gl_ref, st_ref,
             du_ref, dw_ref, dqd_ref, dkd_ref, dqk_ref, dgl_ref, dstate):
        @pl.when(pl.program_id(0) == 0)
        def _():
            dstate[...] = jnp.zeros_like(dstate)

        for h in range(HEADS):
            sl = slice(h * HEAD_DIM, (h + 1) * HEAD_DIM)
            s_f = st_ref[0, h]
            s_b = s_f.astype(BF16)
            ds_f = dstate[h]
            ds_b = ds_f.astype(BF16)
            do_b = do_ref[:, sl].astype(BF16)
            w_b, qd_b, kd_b, qk_b = w_ref[:, sl], qd_ref[:, sl], kd_ref[:, sl], qk_ref[h]
            vn = (u_ref[:, sl] - _bdot(w_b, s_b, NN)).astype(BF16)
            dvn = _bdot(qk_b, do_b, TN) + _bdot(kd_b, ds_b, NN)
            dvn_b = dvn.astype(BF16)
            du_ref[:, sl] = dvn
            dw_ref[:, sl] = -_bdot(dvn_b, s_b, NT)
            dqd_ref[:, sl] = _bdot(do_b, s_b, NT)
            dkd_ref[:, sl] = _bdot(vn, ds_b, NT)
            dqk_ref[h] = _bdot(do_b, vn, NT)
            dgl_ref[0, pl.ds(h, 1), :] = jnp.sum(s_f * ds_f) + jnp.zeros((1, HEAD_DIM), F32)
            dstate[h] = ds_f * gl_ref[0, pl.ds(h, 1), :] + _bdot(qd_b, do_b, TN) - _bdot(w_b, dvn_b, TN)

    return _call(
        body, "gdn_scan_bwd", (n_c,), [row, row, row, row, row, qk_spec, gl_spec, st_spec],
        [row, row, row, row, qk_spec, gl_spec],
        [_sds((s_len, D_MODEL))] * 4 + [_sds((HEADS, s_len, CHUNK)), _sds((n_c, HEADS, HEAD_DIM))],
        scratch=[pltpu.VMEM((HEADS, HEAD_DIM, HEAD_DIM), F32)], sem=("arbitrary",))(do, u, w, qd, kd, qk, gl, states)


def _gate_norm(o, z, nw):
    outs = []
    for h in range(HEADS):
        oh = o[:, h * HEAD_DIM:(h + 1) * HEAD_DIM]
        outs.append(oh * lax.rsqrt(jnp.mean(oh * oh, axis=-1, keepdims=True) + RMS_EPS))
    return jnp.concatenate(outs, axis=1) * nw * _silu(z)


def _gdn_out_fwd(o, proj, x, w_out, nw, gain, bias):
    s_len = x.shape[0]
    ts = min(ROW_TILE, s_len)

    def body(o_ref, z_ref, x_ref, w_ref, nw_ref, g_ref, b_ref, y_ref, xhat_ref, rstd_ref):
        on = _gate_norm(o_ref[...], z_ref[...], nw_ref[...])
        r = ALPHA * x_ref[...] + _bdot(on, w_ref[...], NN)
        y_ref[...], xhat_ref[...], rstd_ref[...] = _ln_fwd(r, g_ref[...], b_ref[...])

    tile = pl.BlockSpec((ts, D_MODEL), lambda i: (i, 0))
    row = _row(D_MODEL)
    return _call(
        body, "gdn_out_fwd", (s_len // ts,),
        [tile, pl.BlockSpec((ts, D_MODEL), lambda i: (i, QKV_DIM // D_MODEL)), tile, _full((D_MODEL, D_MODEL)), row, row, row],
        [tile, tile, pl.BlockSpec((ts, 1), lambda i: (i, 0))],
        [_sds((s_len, D_MODEL)), _sds((s_len, D_MODEL)), _sds((s_len, 1))], sem=("parallel",))(o, proj, x, w_out, nw, gain, bias)


def _gdn_out_bwd(dy, xhat, rstd, o, proj, w_out, nw, gain):
    s_len = o.shape[0]
    ts = min(ROW_TILE, s_len)

    def body(dy_ref, xh_ref, rs_ref, o_ref, z_ref, w_ref, nw_ref, g_ref,
             dres_ref, do_ref, dz_ref, on_ref, drb_ref, dg_ref, db_ref, dnw_ref):
        dy_t, xh_t = dy_ref[...], xh_ref[...]
        dr = _ln_bwd(dy_t, xh_t, rs_ref[...], g_ref[...])
        dres_ref[...] = ALPHA * dr
        drb_ref[...] = dr.astype(BF16)
        on, vjp = jax.vjp(_gate_norm, o_ref[...], z_ref[...], nw_ref[...])
        on_ref[...] = on.astype(BF16)
        do, dz, dnw = vjp(_bdot(dr, w_ref[...], NT))
        do_ref[...] = do
        dz_ref[...] = dz.astype(BF16)
        first = pl.program_id(0) == 0
        _acc(dg_ref, first, jnp.sum(dy_t * xh_t, axis=0, keepdims=True))
        _acc(db_ref, first, jnp.sum(dy_t, axis=0, keepdims=True))
        _acc(dnw_ref, first, sum(dnw[:, h * HEAD_DIM:(h + 1) * HEAD_DIM] for h in range(HEADS)))

    tile = pl.BlockSpec((ts, D_MODEL), lambda i: (i, 0))
    row = _row(D_MODEL)
    return _call(
        body, "gdn_out_bwd", (s_len // ts,),
        [tile, tile, pl.BlockSpec((ts, 1), lambda i: (i, 0)), tile,
         pl.BlockSpec((ts, D_MODEL), lambda i: (i, QKV_DIM // D_MODEL)), _full((D_MODEL, D_MODEL)), row, row],
        [tile, tile, tile, tile, tile, row, row, _row(HEAD_DIM)],
        [_sds((s_len, D_MODEL))] * 2 + [_sds((s_len, D_MODEL), BF16)] * 3 + [_sds((1, D_MODEL))] * 2 + [_sds((1, HEAD_DIM))],
        sem=("arbitrary",))(dy, xhat, rstd, o, proj, w_out, nw, gain)


def _loss_head(y, target):
    s_len = y.shape[0]
    ts = min(ROW_TILE, s_len)

    def body(y_ref, t_ref, dy_ref, l_ref):
        err = y_ref[...] - t_ref[...]
        dy_ref[...] = err * (1.0 / D_MODEL)
        part = 0.5 * jnp.sum(jnp.mean(err * err, axis=-1, keepdims=True))
        _acc(l_ref, pl.program_id(0) == 0, part + jnp.zeros((1, 128), F32))

    tile = pl.BlockSpec((ts, D_MODEL), lambda i: (i, 0))
    return _call(body, "loss_head", (s_len // ts,), [tile, tile], [tile, _row(128)],
                 [_sds((s_len, D_MODEL)), _sds((1, 128))], sem=("arbitrary",))(y, target)


def _adamw(w, g, m, v, name):
    r, c = w.shape
    tr = min(ADAM_ROWS, r)

    def body(w_ref, g_ref, m_ref, v_ref, d_ref, nm_ref, nv_ref):
        g_t = g_ref[...]
        nm = ADAM_B1 * m_ref[...] + (1.0 - ADAM_B1) * g_t
        nv = ADAM_B2 * v_ref[...] + (1.0 - ADAM_B2) * (g_t * g_t)
        m_hat = nm / (1.0 - ADAM_B1 ** ADAM_STEP)
        v_hat = nv / (1.0 - ADAM_B2 ** ADAM_STEP)
        d_ref[...] = -ADAM_LR * (m_hat / (jnp.sqrt(v_hat) + ADAM_EPS) + ADAM_WD * w_ref[...])
        nm_ref[...] = nm
        nv_ref[...] = nv

    tile = pl.BlockSpec((tr, c), lambda i: (i, 0))
    return _call(body, name, (r // tr,), [tile] * 4, [tile] * 3, [_sds((r, c))] * 3, sem=("parallel",))(w, g, m, v)


def _place():
    x, y, c = lax.axis_index("x"), lax.axis_index("y"), lax.axis_index("c")
    return x, y, c, [(1 - x, y), (x, 1 - y), (1 - x, 1 - y)]


def _all_gather(wb, wf):
    def body(wb_ref, wf_ref, ob_ref, of_ref, send_sems, recv_sems, local_sems):
        x, y, c, chips = _place()
        me = 2 * x + y
        sibling = (x, y, 1 - c)
        local = [pltpu.make_async_copy(src, dst.at[me], local_sems.at[n])
                 for n, (src, dst) in enumerate(((wb_ref, ob_ref), (wf_ref, of_ref)))]
        for cp in local:
            cp.start()
        sends, fwds, lands = [], [], []
        for n, (src, dst) in enumerate(((wb_ref, ob_ref), (wf_ref, of_ref))):
            for j, (px, py) in enumerate(chips):
                k = 6 * n + j
                peer = 2 * px + py
                sends.append(pltpu.make_async_remote_copy(
                    src_ref=src.at[c], dst_ref=dst.at[me, c], send_sem=send_sems.at[k], recv_sem=recv_sems.at[k],
                    device_id=(px, py, c), device_id_type=MESH))
                fwds.append(pltpu.make_async_remote_copy(
                    src_ref=dst.at[peer, c], dst_ref=dst.at[peer, c], send_sem=send_sems.at[k + 3], recv_sem=recv_sems.at[k + 3],
                    device_id=sibling, device_id_type=MESH))
                lands.append(pltpu.make_async_remote_copy(
                    src_ref=dst.at[peer, 1 - c], dst_ref=dst.at[peer, 1 - c], send_sem=send_sems.at[k + 3],
                    recv_sem=recv_sems.at[k + 3], device_id=sibling, device_id_type=MESH))
        for cp in sends:
            cp.start()
        for n in range(2):
            for j in range(3):
                arrive = pltpu.make_async_remote_copy(
                    src_ref=(wb_ref, wf_ref)[n].at[c], dst_ref=(ob_ref, of_ref)[n].at[2 * chips[j][0] + chips[j][1], c],
                    send_sem=send_sems.at[6 * n + j], recv_sem=recv_sems.at[6 * n + j],
                    device_id=(chips[j][0], chips[j][1], c), device_id_type=MESH)
                arrive.wait_recv()
                fwds[3 * n + j].start()
        for cp in lands:
            cp.wait_recv()
        for cp in sends + fwds:
            cp.wait_send()
        for cp in local:
            cp.wait()

    return pl.pallas_call(
        body, name="weights_all_gather",
        out_shape=[_sds((N_SHARD,) + wb.shape, BF16), _sds((N_SHARD,) + wf.shape, F32)],
        in_specs=[ANY, ANY], out_specs=[ANY, ANY],
        scratch_shapes=[pltpu.SemaphoreType.DMA((12,)), pltpu.SemaphoreType.DMA((12,)), pltpu.SemaphoreType.DMA((2,))],
    )(wb, wf)


def _swap_halves(g):
    def body(g_ref, o_ref, send_sem, recv_sem):
        x, y, c, _ = _place()
        cp = pltpu.make_async_remote_copy(
            src_ref=g_ref.at[1 - c], dst_ref=o_ref, send_sem=send_sem, recv_sem=recv_sem,
            device_id=(x, y, 1 - c), device_id_type=MESH)
        cp.start()
        cp.wait()

    return pl.pallas_call(
        body, name="grads_swap_halves", out_shape=_sds(g.shape[1:]), in_specs=[ANY], out_specs=ANY,
        scratch_shapes=[pltpu.SemaphoreType.DMA, pltpu.SemaphoreType.DMA])(g)


def _scatter_chips(pc):
    def body(p_ref, o_ref, send_sems, recv_sems, local_sem):
        x, y, c, chips = _place()
        me = 2 * x + y
        local = pltpu.make_async_copy(p_ref.at[me], o_ref.at[me], local_sem)
        local.start()
        sends = [pltpu.make_async_remote_copy(
            src_ref=p_ref.at[2 * px + py], dst_ref=o_ref.at[me], send_sem=send_sems.at[j], recv_sem=recv_sems.at[j],
            device_id=(px, py, c), device_id_type=MESH) for j, (px, py) in enumerate(chips)]
        for cp in sends:
            cp.start()
        for j, (px, py) in enumerate(chips):
            pltpu.make_async_remote_copy(
                src_ref=p_ref.at[me], dst_ref=o_ref.at[2 * px + py], send_sem=send_sems.at[j], recv_sem=recv_sems.at[j],
                device_id=(px, py, c), device_id_type=MESH).wait_recv()
        for cp in sends:
            cp.wait_send()
        local.wait()

    return pl.pallas_call(
        body, name="grads_scatter_chips", out_shape=_sds(pc.shape), in_specs=[ANY], out_specs=ANY,
        scratch_shapes=[pltpu.SemaphoreType.DMA((3,)), pltpu.SemaphoreType.DMA((3,)), pltpu.SemaphoreType.DMA])(pc)


def _join_halves(r):
    def body(r_ref, o_ref, send_sem, recv_sem, local_sem):
        x, y, c, _ = _place()
        local = pltpu.make_async_copy(r_ref, o_ref.at[c], local_sem)
        local.start()
        cp = pltpu.make_async_remote_copy(
            src_ref=r_ref, dst_ref=o_ref.at[c], send_sem=send_sem, recv_sem=recv_sem,
            device_id=(x, y, 1 - c), device_id_type=MESH)
        cp.start()
        cp.wait()
        local.wait()

    return pl.pallas_call(
        body, name="grads_join_halves", out_shape=_sds((2,) + r.shape), in_specs=[ANY], out_specs=ANY,
        scratch_shapes=[pltpu.SemaphoreType.DMA, pltpu.SemaphoreType.DMA, pltpu.SemaphoreType.DMA])(r)


def _add_halves(g, other, c):
    _, n, r, lanes = g.shape
    tr = 512

    def body(c_ref, g_ref, o_ref, out_ref):
        out_ref[...] = g_ref[...] + o_ref[...]

    tile = pl.BlockSpec((None, tr, lanes), lambda s, i, c_ref: (s, i, 0))
    return pl.pallas_call(
        body, name="grads_add_halves", out_shape=_sds(other.shape),
        grid_spec=pltpu.PrefetchScalarGridSpec(
            num_scalar_prefetch=1, grid=(n, r // tr),
            in_specs=[pl.BlockSpec((None, None, tr, lanes), lambda s, i, c_ref: (c_ref[0], s, i, 0)), tile],
            out_specs=tile))(c, g, other)


def _sum_chips(q):
    _, r, lanes = q.shape
    tr = 512

    def body(q_ref, o_ref):
        o_ref[...] = ((q_ref[0] + q_ref[1]) + q_ref[2]) + q_ref[3]

    return _call(body, "grads_sum_chips", (r // tr,), [pl.BlockSpec((4, tr, lanes), lambda i: (0, i, 0))],
                 pl.BlockSpec((tr, lanes), lambda i: (i, 0)), _sds((r, lanes)), sem=("parallel",))(q)


GATHER_BF16 = ("pool_w", "gdn_w_in", "gdn_w_out", "mlp_w1", "mlp_w2", "ple_gate_w", "ple_proj")
GATHER_F32 = ("ln_gain", "ln_bias", "pool_b", "gdn_conv")
REPLICATED = ("pool_scale", "gdn_a_log", "gdn_dt_bias", "gdn_norm_w", "ple_gate_b")
WEIGHTS = ("ln_gain", "ln_bias", "pool_w", "pool_b", "pool_scale", "gdn_w_in", "gdn_conv", "gdn_a_log", "gdn_dt_bias",
           "gdn_norm_w", "gdn_w_out", "mlp_w1", "mlp_w2", "ple_gate_w", "ple_gate_b", "ple_proj")
GRAD_HALF_ROWS = 3072


def _pack(parts, lanes, row_multiple, dtype):
    flat = jnp.concatenate([a.astype(dtype).reshape(-1) for a in parts])
    rows = -(-flat.shape[0] // (2 * lanes * row_multiple)) * row_multiple
    return jnp.pad(flat, (0, 2 * rows * lanes - flat.shape[0])).reshape(2, rows, lanes)


def _unpack(flat, shapes):
    out, off = [], 0
    for shp in shapes:
        n = math.prod(shp)
        out.append(flat[..., off:off + n].reshape(flat.shape[:-1] + tuple(shp)))
        off += n
    return out


def _pad_lanes(a, offset, width=128):
    return jnp.pad(a, ((0, 0), (offset, width - offset - a.shape[1])))


def kernel(x, p, ln_gain, ln_bias, pool_w, pool_b, pool_scale, gdn_w_in, gdn_conv, gdn_a_log, gdn_dt_bias, gdn_norm_w, gdn_w_out, mlp_w1, mlp_w2, ple_gate_w, ple_gate_b, ple_proj, loss_target, m_ln_gain, m_ln_bias, m_pool_w, m_pool_b, m_pool_scale, m_gdn_w_in, m_gdn_conv, m_gdn_a_log, m_gdn_dt_bias, m_gdn_norm_w, m_gdn_w_out, m_mlp_w1, m_mlp_w2, m_ple_gate_w, m_ple_gate_b, m_ple_proj, v_ln_gain, v_ln_bias, v_pool_w, v_pool_b, v_pool_scale, v_gdn_w_in, v_gdn_conv, v_gdn_a_log, v_gdn_dt_bias, v_gdn_norm_w, v_gdn_w_out, v_mlp_w1, v_mlp_w2, v_ple_gate_w, v_ple_gate_b, v_ple_proj):
    shard = dict(ln_gain=ln_gain, ln_bias=ln_bias, pool_w=pool_w, pool_b=pool_b, pool_scale=pool_scale, gdn_w_in=gdn_w_in,
                 gdn_conv=gdn_conv, gdn_a_log=gdn_a_log, gdn_dt_bias=gdn_dt_bias, gdn_norm_w=gdn_norm_w, gdn_w_out=gdn_w_out,
                 mlp_w1=mlp_w1, mlp_w2=mlp_w2, ple_gate_w=ple_gate_w, ple_gate_b=ple_gate_b, ple_proj=ple_proj)
    mom = dict(ln_gain=m_ln_gain, ln_bias=m_ln_bias, pool_w=m_pool_w, pool_b=m_pool_b, pool_scale=m_pool_scale,
               gdn_w_in=m_gdn_w_in, gdn_conv=m_gdn_conv, gdn_a_log=m_gdn_a_log, gdn_dt_bias=m_gdn_dt_bias,
               gdn_norm_w=m_gdn_norm_w, gdn_w_out=m_gdn_w_out, mlp_w1=m_mlp_w1, mlp_w2=m_mlp_w2, ple_gate_w=m_ple_gate_w,
               ple_gate_b=m_ple_gate_b, ple_proj=m_ple_proj)
    var = dict(ln_gain=v_ln_gain, ln_bias=v_ln_bias, pool_w=v_pool_w, pool_b=v_pool_b, pool_scale=v_pool_scale,
               gdn_w_in=v_gdn_w_in, gdn_conv=v_gdn_conv, gdn_a_log=v_gdn_a_log, gdn_dt_bias=v_gdn_dt_bias,
               gdn_norm_w=v_gdn_norm_w, gdn_w_out=v_gdn_w_out, mlp_w1=v_mlp_w1, mlp_w2=v_mlp_w2, ple_gate_w=v_ple_gate_w,
               ple_gate_b=v_ple_gate_b, ple_proj=v_ple_proj)
    c_idx = lax.axis_index("c")

    gb, gf = _all_gather(_pack([shard[n] for n in GATHER_BF16], LANES, 16, BF16),
                         _pack([shard[n] for n in GATHER_F32], 128, 8, F32))
    st = dict(zip(GATHER_BF16, _unpack(gb.reshape(N_SHARD, -1), [shard[n].shape for n in GATHER_BF16])))
    st.update(zip(GATHER_F32, _unpack(gf.reshape(N_SHARD, -1), [shard[n].shape for n in GATHER_F32])))

    cat_last = lambda a: jnp.moveaxis(a, 0, -2).reshape(a.shape[1:-1] + (N_SHARD * a.shape[-1],))
    gain = cat_last(st["ln_gain"])
    bias = cat_last(st["ln_bias"])
    wp = jnp.moveaxis(st["pool_w"][:, 0], 0, 1).reshape(4, POOL_GROUP, POOL_GROUP)
    pb = cat_last(st["pool_b"]).reshape(1, D_MODEL)
    ps = pool_scale
    w_in = jnp.pad(cat_last(st["gdn_w_in"])[0], ((0, 0), (0, GDN_IN_PAD - GDN_IN_DIM)))
    conv_w = cat_last(st["gdn_conv"])[0]
    w_out = st["gdn_w_out"][:, 0].reshape(D_MODEL, D_MODEL)
    w1s = [st["mlp_w1"][:, i] for i in range(2)]
    w2s = [st["mlp_w2"][:, i] for i in range(2)]
    gws = [st["ple_gate_w"][:, i].reshape(D_MODEL, D_MODEL) for i in range(2)]
    pjs = [cat_last(st["ple_proj"])[i] for i in range(2)]
    gbs = [ple_gate_b[i:i + 1] for i in range(2)]
    alog_l = _pad_lanes(gdn_a_log, HEADS)
    dtb_l = _pad_lanes(gdn_dt_bias, HEADS)
    nw = jnp.tile(gdn_norm_w, (1, HEADS))
    ln = lambda i, k: (gain[i, k][None], bias[i, k][None])

    x0 = x[0]
    p0, p1 = p[0, 0], p[1, 0]
    s_len = x0.shape[0]
    n_c = s_len // CHUNK

    x1, xh1, rs1 = _pool_fwd(x0, wp, pb, ps, *ln(0, 0))
    x2, xh2, rs2 = _mlp_fwd(x1, p0, w1s[0], w2s[0], gws[0], gbs[0], pjs[0], *ln(0, 1))
    proj = _matmul_nn(x2, w_in, "gdn_in_proj", GDN_IN_PAD // 3)
    q, k, v, gcb = _conv_fwd(proj, conv_w, alog_l, dtb_l)
    beta_t = gcb[:, :HEADS].T[:, :, None]
    gc_t = gcb[:, HEADS:2 * HEADS].T[:, :, None]
    u, w, qd, kd, qk, gl = _gdn_prep(q, k, v, gc_t, beta_t)
    gl_n = jnp.moveaxis(gl.reshape(HEADS, n_c, HEAD_DIM), 0, 1)
    o, states = _gdn_scan(u, w, qd, kd, qk, gl_n)
    x3, xh3, rs3 = _gdn_out_fwd(o, proj, x2, w_out, nw, *ln(1, 0))
    x4, xh4, rs4 = _mlp_fwd(x3, p1, w1s[1], w2s[1], gws[1], gbs[1], pjs[1], *ln(1, 1))
    dy4, loss_l = _loss_head(x4, loss_target[0])

    grad = {}
    g_gain = [[None, None], [None, None]]
    g_bias = [[None, None], [None, None]]

    def mlp_grads(i, dy, xh, rs, x_mid, p_i):
        dx, a, dh, dzg, dpp, drb, dg, db, dgb = _mlp_bwd(dy, xh, rs, x_mid, p_i, w1s[i], w2s[i], gws[i], gbs[i], pjs[i], ln(i, 1)[0])
        g_gain[i][1], g_bias[i][1] = dg, db
        return dx, dict(
            mlp_w1=_wgrad(x_mid, dh, f"dw1_{i}", stack_cols=True), mlp_w2=_wgrad(a, drb, f"dw2_{i}").reshape(N_SHARD, -1, D_MODEL),
            ple_gate_w=_wgrad(x_mid, dzg, f"dgate_w_{i}").reshape(N_SHARD, -1, D_MODEL),
            ple_proj=_wgrad(p_i, dpp, f"dproj_{i}", stack_cols=True), ple_gate_b=dgb)

    dx3, gl1 = mlp_grads(1, dy4, xh4, rs4, x3, p1)
    dres, do, dz, on_b, drb3, g_gain[1][0], g_bias[1][0], d_nw = _gdn_out_bwd(dx3, xh3, rs3, o, proj, w_out, nw, ln(1, 0)[0])
    d_wout = _wgrad(on_b, drb3, "dw_out").reshape(N_SHARD, -1, D_MODEL)
    du, dw, dqd, dkd, dqk, dgl_n = _gdn_scan_bwd(do, u, w, qd, kd, qk, gl_n, states)
    dgl = jnp.moveaxis(dgl_n, 0, 1).reshape(gl.shape)
    dq, dk, dv, dgc_t, dbeta_t = _gdn_prep_bwd(q, k, v, gc_t, beta_t, du, dw, dqd, dkd, dqk, dgl)
    dgcb = jnp.pad(jnp.concatenate([dbeta_t[:, :, 0].T, dgc_t[:, :, 0].T], axis=1), ((0, 0), (0, 128 - 2 * HEADS)))
    dproj, d_conv, d_alog_l, d_dtb_l = _conv_bwd(proj, dq, dk, dv, dgcb, dz, conv_w, alog_l, dtb_l)
    dx2 = _matmul_nt_add(dproj, w_in, dres, "gdn_in_bwd")
    d_win = _wgrad(x2, dproj, "dw_in")[:, :GDN_IN_DIM]
    dx1, gl0 = mlp_grads(0, dx2, xh2, rs2, x1, p0)
    dx0, g_gain[0][0], g_bias[0][0], d_ps, d_pb, d_wp = _pool_bwd(dx1, xh1, rs1, x0, wp, pb, ps, ln(0, 0)[0])

    split_last = lambda a: jnp.moveaxis(a.reshape(a.shape[:-1] + (N_SHARD, a.shape[-1] // N_SHARD)), -2, 0)
    grad_st = dict(
        ln_gain=split_last(jnp.stack([jnp.concatenate(r, axis=0) for r in g_gain])),
        ln_bias=split_last(jnp.stack([jnp.concatenate(r, axis=0) for r in g_bias])),
        pool_w=jnp.moveaxis(d_wp.reshape(4, N_SHARD, POOL_GROUP // N_SHARD, POOL_GROUP), 1, 0)[:, None],
        pool_b=split_last(d_pb.reshape(1, 4, POOL_GROUP)),
        gdn_w_in=split_last(d_win)[:, None],
        gdn_conv=split_last(d_conv)[:, None],
        gdn_w_out=d_wout[:, None],
        mlp_w1=jnp.stack([gl0["mlp_w1"], gl1["mlp_w1"]], axis=1),
        mlp_w2=jnp.stack([gl0["mlp_w2"], gl1["mlp_w2"]], axis=1),
        ple_gate_w=jnp.stack([gl0["ple_gate_w"], gl1["ple_gate_w"]], axis=1),
        ple_proj=jnp.stack([gl0["ple_proj"], gl1["ple_proj"]], axis=1),
    )
    grad_rep = dict(pool_scale=d_ps, gdn_a_log=d_alog_l[:, HEADS:2 * HEADS], gdn_dt_bias=d_dtb_l[:, HEADS:2 * HEADS],
                    gdn_norm_w=d_nw, ple_gate_b=jnp.concatenate([gl0["ple_gate_b"], gl1["ple_gate_b"]], axis=0))
    for n in REPLICATED:
        grad_st[n] = jnp.broadcast_to(grad_rep[n][None], (N_SHARD,) + grad_rep[n].shape)

    flat = jnp.concatenate([grad_st[n].reshape(N_SHARD, -1) for n in WEIGHTS], axis=1)
    flat = jnp.pad(flat, ((0, 0), (0, 2 * GRAD_HALF_ROWS * LANES - flat.shape[1])))
    g_halves = jnp.moveaxis(flat.reshape(N_SHARD, 2, GRAD_HALF_ROWS, LANES), 1, 0)
    chip_sum = _add_halves(g_halves, _swap_halves(g_halves), c_idx.reshape(1).astype(jnp.int32))
    reduced = _join_halves(_sum_chips(_scatter_chips(chip_sum)))
    grads = dict(zip(WEIGHTS, _unpack(reduced.reshape(-1), [shard[n].shape for n in WEIGHTS])))

    delta, new_m, new_v = {}, {}, {}
    small = [n for n in WEIGHTS if shard[n].size < 128 * 128]
    for n in WEIGHTS:
        if n in small:
            continue
        to2d = lambda a, n=n: a.reshape(-1, shard[n].shape[-1])
        d2, m2, v2 = _adamw(to2d(shard[n]), to2d(grads[n]), to2d(mom[n]), to2d(var[n]), "adamw_" + n)
        delta[n], new_m[n], new_v[n] = (t.reshape(shard[n].shape) for t in (d2, m2, v2))
    pk = lambda d: _pack([d[n] for n in small], 128, 8, F32).reshape(-1, 128)
    d2, m2, v2 = _adamw(pk(shard), pk(grads), pk(mom), pk(var), "adamw_small")
    for dst, t in ((delta, d2), (new_m, m2), (new_v, v2)):
        dst.update(zip(small, _unpack(t.reshape(-1), [shard[n].shape for n in small])))

    loss = lax.psum(loss_l[0, 0], ("x", "y", "c"))
    return (loss, dx0[None], *[grads[n] for n in WEIGHTS], *[delta[n] for n in WEIGHTS],
            *[new_m[n] for n in WEIGHTS], *[new_v[n] for n in WEIGHTS])
```

```python
import math

import jax
import jax.numpy as jnp
from jax import lax
from jax.experimental import pallas as pl
from jax.experimental.pallas import tpu as pltpu

F32 = jnp.float32
BF16 = jnp.bfloat16

D_MODEL = 1024
D_FF = 4096
PLE_DIM = 256
N_SHARD = 4
POOL_WINDOWS = (2, 4, 8, 16)
POOL_GROUP = 256
POOL_HALO = 16
HEADS = 8
HEAD_DIM = 128
CHUNK = 64
CONV_WIDTH = 4
CONV_HALO = 8
QKV_DIM = 3 * D_MODEL
GDN_IN_DIM = QKV_DIM + D_MODEL + 2 * HEADS
GDN_IN_PAD = 4224
BA_BLOCK = (QKV_DIM + D_MODEL) // 128
ALPHA = (2.0 * 2) ** 0.25
LN_EPS = 1e-5
RMS_EPS = 1e-6
L2_EPS = 1e-6
ADAM_LR, ADAM_B1, ADAM_B2, ADAM_EPS, ADAM_WD, ADAM_STEP = 0.001, 0.9, 0.999, 1e-08, 0.01, 10

ROW_TILE = 512
CONV_TILE = 256
PREP_CHUNKS = 4
LANES = 1024
ADAM_ROWS = 256

NN = (((1,), (0,)), ((), ()))
NT = (((1,), (1,)), ((), ()))
TN = (((0,), (0,)), ((), ()))
BNN = (((2,), (1,)), ((0,), (0,)))
BNT = (((2,), (2,)), ((0,), (0,)))
BTN = (((1,), (1,)), ((0,), (0,)))
MESH = pl.DeviceIdType.MESH
ANY = pl.BlockSpec(memory_space=pl.ANY)


def _bdot(a, b, dims):
    return lax.dot_general(a.astype(BF16), b.astype(BF16), dims, preferred_element_type=F32)


def _hdot(a, b, dims):
    return lax.dot_general(a, b, dims, precision=lax.Precision.HIGHEST, preferred_element_type=F32)


def _mdot(a, b, dims):
    return lax.dot_general(a, b, dims, precision=lax.Precision.HIGH, preferred_element_type=F32)


def _sigmoid(x):
    return 1.0 / (1.0 + jnp.exp(-x))


def _silu(x):
    return x * _sigmoid(x)


def _softplus(x):
    return jnp.maximum(x, 0.0) + jnp.log1p(jnp.exp(-jnp.abs(x)))


def _call(body, name, grid, in_specs, out_specs, out_shape, scratch=(), sem=None):
    params = pltpu.CompilerParams(dimension_semantics=sem) if sem else None
    return pl.pallas_call(
        body, name=name, grid=grid, in_specs=in_specs, out_specs=out_specs, out_shape=out_shape,
        scratch_shapes=list(scratch), compiler_params=params)


def _row(d):
    return pl.BlockSpec((1, d), lambda *_: (0, 0))


def _full(shape):
    n = len(shape)
    return pl.BlockSpec(shape, lambda *_: (0,) * n)


def _sds(shape, dtype=F32):
    return jax.ShapeDtypeStruct(shape, dtype)


def _ln_fwd(r, gain, bias):
    mu = jnp.mean(r, axis=-1, keepdims=True)
    xc = r - mu
    rstd = lax.rsqrt(jnp.mean(xc * xc, axis=-1, keepdims=True) + LN_EPS)
    xhat = xc * rstd
    return xhat * gain + bias, xhat, rstd


def _ln_bwd(dy, xhat, rstd, gain):
    dxh = dy * gain
    m1 = jnp.mean(dxh, axis=-1, keepdims=True)
    m2 = jnp.mean(dxh * xhat, axis=-1, keepdims=True)
    return rstd * (dxh - m1 - xhat * m2)


def _acc(ref, first, val):
    @pl.when(first)
    def _():
        ref[...] = val

    @pl.when(jnp.logical_not(first))
    def _():
        ref[...] += val


def _pooled_groups(xe, t0, ts):
    pos = (t0 + lax.broadcasted_iota(jnp.int32, (ts, 1), 0) + 1).astype(F32)
    outs = []
    for gi, win in enumerate(POOL_WINDOWS):
        xs = xe[:, gi * POOL_GROUP:(gi + 1) * POOL_GROUP]
        s, k = xs, 1
        while k < win:
            s = s + pltpu.roll(s, k, 0)
            k *= 2
        mean = s[POOL_HALO:] / jnp.minimum(pos, float(win))
        outs.append(mean - xs[POOL_HALO:])
    return outs


def _pool_groups_w(w_ref):
    return [jnp.concatenate([w_ref[s, g] for s in range(N_SHARD)], axis=0) for g in range(4)]


def _pool_fwd(x, wp, pb, ps, gain, bias):
    s_len = x.shape[0]
    ts = min(ROW_TILE, s_len)
    hb = ts // POOL_HALO

    def body(x_ref, halo_ref, w_ref, pb_ref, ps_ref, g_ref, b_ref, y_ref, xhat_ref, rstd_ref):
        i = pl.program_id(0)
        x_t = x_ref[...]
        halo = jnp.where(i > 0, halo_ref[...], 0.0)
        pooled = _pooled_groups(jnp.concatenate([halo, x_t], axis=0), i * ts, ts)
        wg = _pool_groups_w(w_ref)
        y = jnp.concatenate([_bdot(pooled[g], wg[g], NN) for g in range(4)], axis=1) + pb_ref[...]
        r = ALPHA * x_t + y * ps_ref[...]
        y_ref[...], xhat_ref[...], rstd_ref[...] = _ln_fwd(r, g_ref[...], b_ref[...])

    tile = pl.BlockSpec((ts, D_MODEL), lambda i: (i, 0))
    return _call(
        body, "pool_fwd", (s_len // ts,),
        [tile, pl.BlockSpec((POOL_HALO, D_MODEL), lambda i: (jnp.maximum(i * hb - 1, 0), 0)),
         _full(wp.shape), _row(D_MODEL), _row(D_MODEL), _row(D_MODEL), _row(D_MODEL)],
        [tile, tile, pl.BlockSpec((ts, 1), lambda i: (i, 0))],
        [_sds((s_len, D_MODEL)), _sds((s_len, D_MODEL)), _sds((s_len, 1))],
        sem=("parallel",))(x, x, wp, pb, ps, gain, bias)


def _pool_bwd(dy, xhat, rstd, x, wp, pb, ps, gain):
    s_len = x.shape[0]
    ts = min(ROW_TILE, s_len)
    hb = ts // POOL_HALO
    n_t = s_len // ts
    ne = ts + POOL_HALO

    def body(dy_ref, dyn_ref, xh_ref, xhn_ref, rs_ref, rsn_ref, x_ref, xp_ref, w_ref, pb_ref, ps_ref, g_ref,
             dx_ref, dg_ref, db_ref, dps_ref, dpb_ref, dw_ref):
        i = pl.program_id(0)
        more = i < n_t - 1
        dy_t, xh_t = dy_ref[...], xh_ref[...]
        dy_e = jnp.concatenate([dy_t, jnp.where(more, dyn_ref[...], 0.0)], axis=0)
        xh_e = jnp.concatenate([xh_t, xhn_ref[...]], axis=0)
        rs_e = jnp.concatenate([rs_ref[...], rsn_ref[...]], axis=0)
        dr_e = _ln_bwd(dy_e, xh_e, rs_e, g_ref[...])
        dyy_e = dr_e * ps_ref[...]
        pos_e = (i * ts + lax.broadcasted_iota(jnp.int32, (ne, 1), 0) + 1).astype(F32)
        dxs = []
        wg = _pool_groups_w(w_ref)
        for gi, win in enumerate(POOL_WINDOWS):
            sl = slice(gi * POOL_GROUP, (gi + 1) * POOL_GROUP)
            dpool = _bdot(dyy_e[:, sl], wg[gi], NT)
            s, k = dpool / jnp.minimum(pos_e, float(win)), 1
            while k < win:
                s = s + pltpu.roll(s, ne - k, 0)
                k *= 2
            dxs.append(s[:ts] - dpool[:ts])
        dx_ref[...] = ALPHA * dr_e[:ts] + jnp.concatenate(dxs, axis=1)

        x_t = x_ref[...]
        halo = jnp.where(i > 0, xp_ref[...], 0.0)
        pooled = _pooled_groups(jnp.concatenate([halo, x_t], axis=0), i * ts, ts)
        y = jnp.concatenate([_bdot(pooled[g], wg[g], NN) for g in range(4)], axis=1) + pb_ref[...]
        dr_t, dyy_t = dr_e[:ts], dyy_e[:ts]
        first = i == 0
        _acc(dg_ref, first, jnp.sum(dy_t * xh_t, axis=0, keepdims=True))
        _acc(db_ref, first, jnp.sum(dy_t, axis=0, keepdims=True))
        _acc(dps_ref, first, jnp.sum(dr_t * y, axis=0, keepdims=True))
        _acc(dpb_ref, first, jnp.sum(dyy_t, axis=0, keepdims=True))
        for g in range(4):
            _acc(dw_ref.at[g], first, _bdot(pooled[g], dyy_t[:, g * POOL_GROUP:(g + 1) * POOL_GROUP], TN))

    tile = pl.BlockSpec((ts, D_MODEL), lambda i: (i, 0))
    nxt = pl.BlockSpec((POOL_HALO, D_MODEL), lambda i: (jnp.minimum((i + 1) * hb, n_t * hb - 1), 0))
    prv = pl.BlockSpec((POOL_HALO, D_MODEL), lambda i: (jnp.maximum(i * hb - 1, 0), 0))
    rs_t = pl.BlockSpec((ts, 1), lambda i: (i, 0))
    rs_n = pl.BlockSpec((POOL_HALO, 1), lambda i: (jnp.minimum((i + 1) * hb, n_t * hb - 1), 0))
    row = _row(D_MODEL)
    return _call(
        body, "pool_bwd", (n_t,),
        [tile, nxt, tile, nxt, rs_t, rs_n, tile, prv, _full(wp.shape), row, row, row],
        [tile, row, row, row, row, _full((4, POOL_GROUP, POOL_GROUP))],
        [_sds((s_len, D_MODEL))] + [_sds((1, D_MODEL))] * 4 + [_sds((4, POOL_GROUP, POOL_GROUP))],
        sem=("arbitrary",))(dy, dy, xhat, xhat, rstd, rstd, x, x, wp, pb, ps, gain)


def _mlp_weight_specs(layer):
    fc = D_FF // N_SHARD
    return [pl.BlockSpec((None, None, D_MODEL, fc), lambda i, j: (j, layer, 0, 0)),
            pl.BlockSpec((None, None, fc, D_MODEL), lambda i, j: (j, layer, 0, 0)),
            pl.BlockSpec((N_SHARD, None, D_MODEL // N_SHARD, D_MODEL), lambda i, j: (0, layer, 0, 0)),
            _row(D_MODEL),
            pl.BlockSpec((N_SHARD, None, PLE_DIM, D_MODEL // N_SHARD), lambda i, j: (0, layer, 0, 0))]


def _gate_w(gw_ref):
    return gw_ref[...].reshape(D_MODEL, D_MODEL)


def _ple_proj(pj_ref):
    return jnp.concatenate([pj_ref[s] for s in range(N_SHARD)], axis=1)


def _mlp_fwd(x, p, w1s, w2s, gw, gb, proj, gain, bias, layer):
    s_len = x.shape[0]
    ts = min(ROW_TILE, s_len)

    def body(x_ref, p_ref, w1_ref, w2_ref, gw_ref, gb_ref, pj_ref, g_ref, b_ref, y_ref, xhat_ref, rstd_ref,
             acc_ref, xb_ref):
        j = pl.program_id(1)

        @pl.when(j == 0)
        def _():
            x_t = x_ref[...]
            xb_ref[...] = x_t.astype(BF16)
            gate = _sigmoid(_bdot(x_t, _gate_w(gw_ref), NN) + gb_ref[...])
            acc_ref[...] = ALPHA * x_t + gate * _bdot(p_ref[...], _ple_proj(pj_ref), NN)

        h = jnp.maximum(_bdot(xb_ref[...], w1_ref[...], NN), 0.0)
        acc_ref[...] += _bdot(h * h, w2_ref[...], NN)

        @pl.when(j == N_SHARD - 1)
        def _():
            y_ref[...], xhat_ref[...], rstd_ref[...] = _ln_fwd(acc_ref[...], g_ref[...], b_ref[...])

    tile = pl.BlockSpec((ts, D_MODEL), lambda i, j: (i, 0))
    row = _row(D_MODEL)
    return _call(
        body, "mlp_fwd", (s_len // ts, N_SHARD),
        [tile, pl.BlockSpec((ts, PLE_DIM), lambda i, j: (i, 0))] + _mlp_weight_specs(layer) + [row, row],
        [tile, tile, pl.BlockSpec((ts, 1), lambda i, j: (i, 0))],
        [_sds((s_len, D_MODEL)), _sds((s_len, D_MODEL)), _sds((s_len, 1))],
        scratch=[pltpu.VMEM((ts, D_MODEL), F32), pltpu.VMEM((ts, D_MODEL), BF16)],
        sem=("parallel", "arbitrary"))(x, p, w1s, w2s, gw, gb, proj, gain, bias)


def _mlp_bwd(dy, xhat, rstd, x, p, w1s, w2s, gw, gb, proj, gain, layer):
    s_len = x.shape[0]
    ts = min(ROW_TILE, s_len)
    fc = D_FF // N_SHARD

    def body(dy_ref, xh_ref, rs_ref, x_ref, p_ref, w1_ref, w2_ref, gw_ref, gb_ref, pj_ref, g_ref,
             dx_ref, a_ref, dh_ref, dzg_ref, dpp_ref, drb_ref, dg_ref, db_ref, dgb_ref, acc_ref, xb_ref, dr_ref):
        i, j = pl.program_id(0), pl.program_id(1)

        @pl.when(j == 0)
        def _():
            dy_t, xh_t, x_t = dy_ref[...], xh_ref[...], x_ref[...]
            dr = _ln_bwd(dy_t, xh_t, rs_ref[...], g_ref[...])
            xb_ref[...] = x_t.astype(BF16)
            drb = dr.astype(BF16)
            dr_ref[...] = drb
            drb_ref[...] = drb
            gw_full = _gate_w(gw_ref)
            gate = _sigmoid(_bdot(x_t, gw_full, NN) + gb_ref[...])
            pp = _bdot(p_ref[...], _ple_proj(pj_ref), NN)
            dzg = dr * pp * gate * (1.0 - gate)
            dzg_ref[...] = dzg.astype(BF16)
            dpp_ref[...] = (dr * gate).astype(BF16)
            acc_ref[...] = ALPHA * dr + _bdot(dzg, gw_full, NT)
            first = i == 0
            _acc(dg_ref, first, jnp.sum(dy_t * xh_t, axis=0, keepdims=True))
            _acc(db_ref, first, jnp.sum(dy_t, axis=0, keepdims=True))
            _acc(dgb_ref, first, jnp.sum(dzg, axis=0, keepdims=True))

        h = jnp.maximum(_bdot(xb_ref[...], w1_ref[...], NN), 0.0)
        a_ref[...] = (h * h).astype(BF16)
        dh = (_bdot(dr_ref[...], w2_ref[...], NT) * (2.0 * h)).astype(BF16)
        dh_ref[...] = dh
        acc_ref[...] += _bdot(dh, w1_ref[...], NT)

        @pl.when(j == N_SHARD - 1)
        def _():
            dx_ref[...] = acc_ref[...]

    tile = pl.BlockSpec((ts, D_MODEL), lambda i, j: (i, 0))
    ftile = pl.BlockSpec((ts, fc), lambda i, j: (i, j))
    row = _row(D_MODEL)
    return _call(
        body, "mlp_bwd", (s_len // ts, N_SHARD),
        [tile, tile, pl.BlockSpec((ts, 1), lambda i, j: (i, 0)), tile, pl.BlockSpec((ts, PLE_DIM), lambda i, j: (i, 0))]
        + _mlp_weight_specs(layer) + [row],
        [tile, ftile, ftile, tile, tile, tile, row, row, row],
        [_sds((s_len, D_MODEL)), _sds((s_len, D_FF), BF16), _sds((s_len, D_FF), BF16)]
        + [_sds((s_len, D_MODEL), BF16)] * 3 + [_sds((1, D_MODEL))] * 3,
        scratch=[pltpu.VMEM((ts, D_MODEL), F32), pltpu.VMEM((ts, D_MODEL), BF16), pltpu.VMEM((ts, D_MODEL), BF16)],
        sem=("arbitrary", "arbitrary"))(dy, xhat, rstd, x, p, w1s, w2s, gw, gb, proj, gain)


def _wgrad(a, b, name, stack_cols=False):
    s_len, m = a.shape
    n = b.shape[1]
    ts = min(512, s_len)
    tm = min(m, 1024)
    tn = n // N_SHARD if stack_cols else (1408 if n == GDN_IN_PAD else min(n, 1024))
    n_s = s_len // ts

    def body(a_ref, b_ref, o_ref):
        _acc(o_ref, pl.program_id(2) == 0, _bdot(a_ref[...], b_ref[...], TN))

    if stack_cols:
        out_spec = pl.BlockSpec((None, tm, tn), lambda mi, nj, s: (nj, mi, 0))
        out_shape = _sds((N_SHARD, m, tn))
    else:
        out_spec = pl.BlockSpec((tm, tn), lambda mi, nj, s: (mi, nj))
        out_shape = _sds((m, n))
    return _call(
        body, name, (m // tm, n // tn, n_s),
        [pl.BlockSpec((ts, tm), lambda mi, nj, s: (s, mi)), pl.BlockSpec((ts, tn), lambda mi, nj, s: (s, nj))],
        out_spec, out_shape, sem=("parallel", "parallel", "arbitrary"))(a, b)


def _matmul_nn(a, b, name, tn):
    s_len, k = a.shape
    n = b.shape[1]
    ts = min(512, s_len)

    def body(a_ref, b_ref, o_ref):
        o_ref[...] = _bdot(a_ref[...], b_ref[...], NN)

    return _call(
        body, name, (s_len // ts, n // tn),
        [pl.BlockSpec((ts, k), lambda i, j: (i, 0)), pl.BlockSpec((k, tn), lambda i, j: (0, j))],
        pl.BlockSpec((ts, tn), lambda i, j: (i, j)), _sds((s_len, n)), sem=("parallel", "parallel"))(a, b)


def _matmul_nt_add(a, b, res, name):
    s_len, k = a.shape
    n = b.shape[0]
    ts = min(512, s_len)

    def body(a_ref, b_ref, r_ref, o_ref):
        o_ref[...] = r_ref[...] + _bdot(a_ref[...], b_ref[...], NT)

    return _call(
        body, name, (s_len // ts,),
        [pl.BlockSpec((ts, k), lambda i: (i, 0)), _full((n, k)), pl.BlockSpec((ts, n), lambda i: (i, 0))],
        pl.BlockSpec((ts, n), lambda i: (i, 0)), _sds((s_len, n)), sem=("parallel",))(a, b, res)


def _act_qkv(y):
    qkv = _silu(y)
    qs, ks = [], []
    for h in range(HEADS):
        qh = qkv[:, h * HEAD_DIM:(h + 1) * HEAD_DIM]
        kh = qkv[:, D_MODEL + h * HEAD_DIM:D_MODEL + (h + 1) * HEAD_DIM]
        qs.append(qh * (lax.rsqrt(jnp.sum(qh * qh, axis=-1, keepdims=True) + L2_EPS) * HEAD_DIM ** -0.5))
        ks.append(kh * lax.rsqrt(jnp.sum(kh * kh, axis=-1, keepdims=True) + L2_EPS))
    return jnp.concatenate(qs, axis=1), jnp.concatenate(ks, axis=1), qkv[:, 2 * D_MODEL:]


def _act_gb(ba, alog_l, dtb_l, tril):
    lane = lax.broadcasted_iota(jnp.int32, ba.shape, 1)
    g = jnp.where((lane >= HEADS) & (lane < 2 * HEADS), -jnp.exp(alog_l) * _softplus(ba + dtb_l), 0.0)
    return jnp.where(lane < HEADS, _sigmoid(ba), _hdot(tril, g, NN))


def _chunk_tril(t):
    ii = lax.broadcasted_iota(jnp.int32, (t, t), 0)
    jj = lax.broadcasted_iota(jnp.int32, (t, t), 1)
    return ((ii // CHUNK == jj // CHUNK) & (ii >= jj)).astype(F32)


def _conv_rows(xe, w, n_rows):
    y = xe[CONV_HALO:CONV_HALO + n_rows] * w[CONV_WIDTH - 1]
    for j in range(CONV_WIDTH - 1):
        y = y + pltpu.roll(xe, CONV_WIDTH - 1 - j, 0)[CONV_HALO:CONV_HALO + n_rows] * w[j]
    return y


def _conv_fwd(proj, conv_w, alog_l, dtb_l):
    s_len = proj.shape[0]
    ts = min(CONV_TILE, s_len)
    hb = ts // CONV_HALO

    def body(x_ref, xp_ref, ba_ref, w_ref, al_ref, dt_ref, q_ref, k_ref, v_ref, gcb_ref):
        i = pl.program_id(0)
        halo = jnp.where(i > 0, xp_ref[...], 0.0)
        taps = [w_ref[pl.ds(j, 1), :] for j in range(CONV_WIDTH)]
        y = _conv_rows(jnp.concatenate([halo, x_ref[...]], axis=0), taps, ts)
        q_ref[...], k_ref[...], v_ref[...] = _act_qkv(y)
        gcb_ref[...] = _act_gb(ba_ref[...], al_ref[...], dt_ref[...], _chunk_tril(ts))

    tile = pl.BlockSpec((ts, D_MODEL), lambda i: (i, 0))
    return _call(
        body, "gdn_conv_fwd", (s_len // ts,),
        [pl.BlockSpec((ts, QKV_DIM), lambda i: (i, 0)),
         pl.BlockSpec((CONV_HALO, QKV_DIM), lambda i: (jnp.maximum(i * hb - 1, 0), 0)),
         pl.BlockSpec((ts, 128), lambda i: (i, BA_BLOCK)), _full((CONV_WIDTH, QKV_DIM)), _row(128), _row(128)],
        [tile, tile, tile, pl.BlockSpec((ts, 128), lambda i: (i, 0))],
        [_sds((s_len, D_MODEL))] * 3 + [_sds((s_len, 128))],
        sem=("parallel",))(proj, proj, proj, conv_w, alog_l, dtb_l)


def _conv_bwd(proj, dq, dk, dv, dgcb, dz, conv_w, alog_l, dtb_l):
    s_len = proj.shape[0]
    ts = min(CONV_TILE, s_len)
    hb = ts // CONV_HALO
    n_t = s_len // ts
    te = ts + CONV_HALO

    def body(x_ref, xp_ref, xn_ref, ba_ref, dq_ref, dqn_ref, dk_ref, dkn_ref, dv_ref, dvn_ref, dgcb_ref, dz_ref,
             w_ref, al_ref, dt_ref, dp_ref, dw_ref, dal_ref, ddt_ref):
        i = pl.program_id(0)
        more = i < n_t - 1
        w = [w_ref[pl.ds(j, 1), :] for j in range(CONV_WIDTH)]
        x_t = x_ref[...]
        xe = jnp.concatenate([jnp.where(i > 0, xp_ref[...], 0.0), x_t, xn_ref[...]], axis=0)
        y_e, act_vjp = jax.vjp(_act_qkv, _conv_rows(xe, w, te))
        ct = tuple(jnp.concatenate([t[...], jnp.where(more, n[...], 0.0)], axis=0)
                   for t, n in ((dq_ref, dqn_ref), (dk_ref, dkn_ref), (dv_ref, dvn_ref)))
        (dy_e,) = act_vjp(ct)
        dx = dy_e[:ts] * w[CONV_WIDTH - 1]
        for j in range(CONV_WIDTH - 1):
            dx = dx + pltpu.roll(dy_e, te - (CONV_WIDTH - 1 - j), 0)[:ts] * w[j]
        dy_t = dy_e[:ts]
        xe_t = xe[:te]
        dws = [jnp.sum(dy_t * pltpu.roll(xe_t, CONV_WIDTH - 1 - j, 0)[CONV_HALO:], axis=0, keepdims=True)
               for j in range(CONV_WIDTH - 1)]
        dws.append(jnp.sum(dy_t * x_t, axis=0, keepdims=True))
        _, gb_vjp = jax.vjp(lambda ba, al, dt: _act_gb(ba, al, dt, _chunk_tril(ts)), ba_ref[...], al_ref[...], dt_ref[...])
        dba, dal, ddt = gb_vjp(dgcb_ref[...])
        dp_ref[...] = jnp.concatenate([dx.astype(BF16), dz_ref[...], dba.astype(BF16)], axis=1)
        first = i == 0
        for j in range(CONV_WIDTH):
            _acc(dw_ref.at[pl.ds(j, 1), :], first, dws[j])
        _acc(dal_ref, first, dal)
        _acc(ddt_ref, first, ddt)

    tile = pl.BlockSpec((ts, D_MODEL), lambda i: (i, 0))
    nxt = pl.BlockSpec((CONV_HALO, D_MODEL), lambda i: (jnp.minimum((i + 1) * hb, n_t * hb - 1), 0))
    return _call(
        body, "gdn_conv_bwd", (n_t,),
        [pl.BlockSpec((ts, QKV_DIM), lambda i: (i, 0)),
         pl.BlockSpec((CONV_HALO, QKV_DIM), lambda i: (jnp.maximum(i * hb - 1, 0), 0)),
         pl.BlockSpec((CONV_HALO, QKV_DIM), lambda i: (jnp.minimum((i + 1) * hb, n_t * hb - 1), 0)),
         pl.BlockSpec((ts, 128), lambda i: (i, BA_BLOCK)),
         tile, nxt, tile, nxt, tile, nxt, pl.BlockSpec((ts, 128), lambda i: (i, 0)), tile,
         _full((CONV_WIDTH, QKV_DIM)), _row(128), _row(128)],
        [pl.BlockSpec((ts, GDN_IN_PAD), lambda i: (i, 0)), _full((CONV_WIDTH, QKV_DIM)), _row(128), _row(128)],
        [_sds((s_len, GDN_IN_PAD), BF16), _sds((CONV_WIDTH, QKV_DIM)), _sds((1, 128)), _sds((1, 128))],
        sem=("arbitrary",))(proj, proj, proj, proj, dq, dq, dk, dk, dv, dv, dgcb, dz, conv_w, alog_l, dtb_l)


def _tri_inv(a_strict):
    ii = lax.broadcasted_iota(jnp.int32, (CHUNK, CHUNK), 0)
    jj = lax.broadcasted_iota(jnp.int32, (CHUNK, CHUNK), 1)
    x = (ii == jj).astype(F32) - a_strict
    pw = _mdot(a_strict, a_strict, BNN)
    for step in range(5):
        x = x + _mdot(x, pw, BNN)
        if step < 4:
            pw = _mdot(pw, pw, BNN)
    return x


@jax.custom_vjp
def _solved(a_strict, rhs, t, sol):
    return sol


def _solved_fwd(a_strict, rhs, t, sol):
    return sol, (t, sol)


def _solved_bwd(res, dsol):
    t, sol = res
    drhs = _mdot(t, dsol, BTN)
    return -_mdot(drhs, sol, BNT), drhs, jnp.zeros_like(t), jnp.zeros_like(sol)


_solved.defvjp(_solved_fwd, _solved_bwd)


def _prep(q, k, v, gc, beta, solve):
    ii = lax.broadcasted_iota(jnp.int32, (CHUNK, CHUNK), 0)
    jj = lax.broadcasted_iota(jnp.int32, (CHUNK, CHUNK), 1)
    causal, strict = ii >= jj, ii > jj
    gc_row = jnp.sum((ii == jj).astype(F32) * gc, axis=1, keepdims=True)
    decay = jnp.where(causal, jnp.exp(jnp.where(causal, gc - gc_row, 0.0)), 0.0)
    kb = k * beta
    a = jnp.where(strict, _bdot(kb, k, BNT) * decay, 0.0)
    eg = jnp.exp(gc)
    sol = solve(a, jnp.concatenate([v * beta, kb * eg], axis=-1))
    qk = _bdot(q, k, BNT) * decay
    last = lax.broadcasted_iota(jnp.int32, (CHUNK, 1), 0) == CHUNK - 1
    g_last = jnp.sum(jnp.where(last, gc, 0.0), axis=1, keepdims=True)
    kd = k * jnp.exp(g_last - gc)
    gl = jnp.exp(g_last) + jnp.zeros((1, 1, HEAD_DIM), F32)
    return sol[..., :HEAD_DIM], sol[..., HEAD_DIM:], qk, q * eg, kd, gl


def _prep_specs(s_len):
    rows = min(PREP_CHUNKS, s_len // CHUNK) * CHUNK
    m = rows // CHUNK
    hd = pl.BlockSpec((rows, HEAD_DIM), lambda c, h: (c, h))
    gcb = pl.BlockSpec((rows, 128), lambda c, h: (c, 0))
    qk = pl.BlockSpec((None, rows, CHUNK), lambda c, h: (h, c, 0))
    gl = pl.BlockSpec((None, m, HEADS, HEAD_DIM), lambda c, h: (c, 0, 0, 0))
    return rows, m, hd, gcb, qk, gl


def _head_cols(gcb, h, m):
    lane = lax.broadcasted_iota(jnp.int32, gcb.shape, 1)
    pick = lambda at: jnp.sum(jnp.where(lane == at, gcb, 0.0), axis=1, keepdims=True).reshape(m, CHUNK, 1)
    return pick(h + HEADS), pick(h)


def _gdn_prep(q, k, v, gcb):
    s_len = q.shape[0]
    rows, m, hd, gcb_spec, qk_spec, gl_spec = _prep_specs(s_len)

    def body(q_ref, k_ref, v_ref, gcb_ref, u_ref, w_ref, qd_ref, kd_ref, qk_ref, gl_ref, t_ref):
        r3 = lambda ref, d: ref[...].reshape(m, CHUNK, d)
        gc, beta = _head_cols(gcb_ref[...], pl.program_id(1), m)

        def solve(a, rhs):
            t = _tri_inv(a)
            t_ref[...] = t.reshape(rows, CHUNK)
            return _mdot(t, rhs, BNN)

        u, w, qk, qd, kd, gl = _prep(r3(q_ref, HEAD_DIM), r3(k_ref, HEAD_DIM), r3(v_ref, HEAD_DIM), gc, beta, solve)
        u_ref[...] = u.reshape(rows, HEAD_DIM)
        w_ref[...] = w.reshape(rows, HEAD_DIM)
        qd_ref[...] = qd.reshape(rows, HEAD_DIM).astype(BF16)
        kd_ref[...] = kd.reshape(rows, HEAD_DIM).astype(BF16)
        qk_ref[...] = qk.reshape(rows, CHUNK).astype(BF16)
        gl_ref[:, pl.ds(pl.program_id(1), 1), :] = gl

    n_g = s_len // rows
    return _call(
        body, "gdn_prep", (n_g, HEADS), [hd, hd, hd, gcb_spec], [hd, hd, hd, hd, qk_spec, gl_spec, qk_spec],
        [_sds((s_len, D_MODEL))] * 2 + [_sds((s_len, D_MODEL), BF16)] * 2
        + [_sds((HEADS, s_len, CHUNK), BF16), _sds((n_g, m, HEADS, HEAD_DIM)), _sds((HEADS, s_len, CHUNK))],
        sem=("parallel", "arbitrary"))(q, k, v, gcb)


def _gdn_prep_bwd(q, k, v, gcb, t_inv, u, w, du, dw, dqd, dkd, dqk, dgl):
    s_len = q.shape[0]
    rows, m, hd, gcb_spec, qk_spec, gl_spec = _prep_specs(s_len)

    def body(q_ref, k_ref, v_ref, gcb_ref, t_ref, u_ref, w_ref, du_ref, dw_ref, dqd_ref, dkd_ref, dqk_ref, dgl_ref,
             dq_ref, dk_ref, dv_ref, dgcb_ref):
        h = pl.program_id(1)
        r3 = lambda ref, d: ref[...].reshape(m, CHUNK, d)
        gc, beta = _head_cols(gcb_ref[...], h, m)
        t = r3(t_ref, CHUNK)
        sol = jnp.concatenate([r3(u_ref, HEAD_DIM), r3(w_ref, HEAD_DIM)], axis=-1)
        fn = lambda q_, k_, v_, gc_, bt_: _prep(q_, k_, v_, gc_, bt_, lambda a, rhs: _solved(a, rhs, t, sol))
        _, vjp = jax.vjp(fn, r3(q_ref, HEAD_DIM), r3(k_ref, HEAD_DIM), r3(v_ref, HEAD_DIM), gc, beta)
        ct = (r3(du_ref, HEAD_DIM), r3(dw_ref, HEAD_DIM), r3(dqk_ref, CHUNK), r3(dqd_ref, HEAD_DIM), r3(dkd_ref, HEAD_DIM),
              dgl_ref[:, pl.ds(h, 1), :] * (1.0 / HEAD_DIM))
        dq, dk, dv, dgc, dbt = vjp(ct)
        dq_ref[...] = dq.reshape(rows, HEAD_DIM)
        dk_ref[...] = dk.reshape(rows, HEAD_DIM)
        dv_ref[...] = dv.reshape(rows, HEAD_DIM)
        lane = lax.broadcasted_iota(jnp.int32, (rows, 128), 1)
        mine = jnp.where(lane == h, dbt.reshape(rows, 1), 0.0) + jnp.where(lane == h + HEADS, dgc.reshape(rows, 1), 0.0)
        _acc(dgcb_ref, h == 0, mine)

    return _call(
        body, "gdn_prep_bwd", (s_len // rows, HEADS),
        [hd, hd, hd, gcb_spec, qk_spec, hd, hd, hd, hd, hd, hd, qk_spec, gl_spec], [hd, hd, hd, gcb_spec],
        [_sds((s_len, D_MODEL))] * 3 + [_sds((s_len, 128))],
        sem=("parallel", "arbitrary"))(q, k, v, gcb, t_inv, u, w, du, dw, dqd, dkd, dqk, dgl)


def _scan_specs(n_c, m, reverse):
    at = (lambda n: n_c - 1 - n) if reverse else (lambda n: n)
    row = pl.BlockSpec((CHUNK, D_MODEL), lambda n: (at(n), 0))
    qk = pl.BlockSpec((HEADS, CHUNK, CHUNK), lambda n: (0, at(n), 0))
    gl = pl.BlockSpec((None, None, HEADS, HEAD_DIM), lambda n: (at(n) // m, at(n) % m, 0, 0))
    st = pl.BlockSpec((1, HEADS, HEAD_DIM, HEAD_DIM), lambda n: (at(n), 0, 0, 0))
    return row, qk, gl, st


def _gdn_scan(u, w, qd, kd, qk, gl):
    s_len = u.shape[0]
    n_c = s_len // CHUNK
    row, qk_spec, gl_spec, st_spec = _scan_specs(n_c, gl.shape[1], False)

    def body(u_ref, w_ref, qd_ref, kd_ref, qk_ref, gl_ref, o_ref, st_ref, state):
        @pl.when(pl.program_id(0) == 0)
        def _():
            state[...] = jnp.zeros_like(state)

        for h in range(HEADS):
            sl = slice(h * HEAD_DIM, (h + 1) * HEAD_DIM)
            s_f = state[h]
            st_ref[0, h] = s_f
            s_b = s_f.astype(BF16)
            vn = (u_ref[:, sl] - _bdot(w_ref[:, sl], s_b, NN)).astype(BF16)
            o_ref[:, sl] = _bdot(qd_ref[:, sl], s_b, NN) + _bdot(qk_ref[h], vn, NN)
            state[h] = s_f * gl_ref[pl.ds(h, 1), :] + _bdot(kd_ref[:, sl], vn, TN)

    return _call(
        body, "gdn_scan", (n_c,), [row, row, row, row, qk_spec, gl_spec], [row, st_spec],
        [_sds((s_len, D_MODEL)), _sds((n_c, HEADS, HEAD_DIM, HEAD_DIM))],
        scratch=[pltpu.VMEM((HEADS, HEAD_DIM, HEAD_DIM), F32)], sem=("arbitrary",))(u, w, qd, kd, qk, gl)


def _gdn_scan_bwd(do, u, w, qd, kd, qk, gl, states):
    s_len = u.shape[0]
    n_c = s_len // CHUNK
    row, qk_spec, gl_spec, st_spec = _scan_specs(n_c, gl.shape[1], True)

    def body(do_ref, u_ref, w_ref, qd_ref, kd_ref, qk_ref, gl_ref, st_ref,
             du_ref, dw_ref, dqd_ref, dkd_ref, dqk_ref, dgl_ref, dstate):
        @pl.when(pl.program_id(0) == 0)
        def _():
            dstate[...] = jnp.zeros_like(dstate)

        for h in range(HEADS):
            sl = slice(h * HEAD_DIM, (h + 1) * HEAD_DIM)
            s_f = st_ref[0, h]
            s_b = s_f.astype(BF16)
            ds_f = dstate[h]
            ds_b = ds_f.astype(BF16)
            do_b = do_ref[:, sl].astype(BF16)
            w_b, qd_b, kd_b, qk_b = w_ref[:, sl].astype(BF16), qd_ref[:, sl], kd_ref[:, sl], qk_ref[h]
            vn = (u_ref[:, sl] - _bdot(w_b, s_b, NN)).astype(BF16)
            dvn = _bdot(qk_b, do_b, TN) + _bdot(kd_b, ds_b, NN)
            dvn_b = dvn.astype(BF16)
            du_ref[:, sl] = dvn
            dw_ref[:, sl] = -_bdot(dvn_b, s_b, NT)
            dqd_ref[:, sl] = _bdot(do_b, s_b, NT)
            dkd_ref[:, sl] = _bdot(vn, ds_b, NT)
            dqk_ref[h] = _bdot(do_b, vn, NT)
            dgl_ref[pl.ds(h, 1), :] = jnp.sum(s_f * ds_f) + jnp.zeros((1, HEAD_DIM), F32)
            dstate[h] = ds_f * gl_ref[pl.ds(h, 1), :] + _bdot(qd_b, do_b, TN) - _bdot(w_b, dvn_b, TN)

    return _call(
        body, "gdn_scan_bwd", (n_c,), [row, row, row, row, row, qk_spec, gl_spec, st_spec],
        [row, row, row, row, qk_spec, gl_spec],
        [_sds((s_len, D_MODEL))] * 4 + [_sds((HEADS, s_len, CHUNK)), _sds(gl.shape)],
        scratch=[pltpu.VMEM((HEADS, HEAD_DIM, HEAD_DIM), F32)], sem=("arbitrary",))(do, u, w, qd, kd, qk, gl, states)


def _gate_norm(o, z, nw):
    outs = []
    for h in range(HEADS):
        oh = o[:, h * HEAD_DIM:(h + 1) * HEAD_DIM]
        outs.append(oh * lax.rsqrt(jnp.mean(oh * oh, axis=-1, keepdims=True) + RMS_EPS))
    return jnp.concatenate(outs, axis=1) * nw * _silu(z)


def _gdn_out_fwd(o, proj, x, w_out, nw, gain, bias):
    s_len = x.shape[0]
    ts = min(ROW_TILE, s_len)

    def body(o_ref, z_ref, x_ref, w_ref, nw_ref, g_ref, b_ref, y_ref, xhat_ref, rstd_ref):
        on = _gate_norm(o_ref[...], z_ref[...], nw_ref[...])
        r = ALPHA * x_ref[...] + _bdot(on, w_ref[...], NN)
        y_ref[...], xhat_ref[...], rstd_ref[...] = _ln_fwd(r, g_ref[...], b_ref[...])

    tile = pl.BlockSpec((ts, D_MODEL), lambda i: (i, 0))
    row = _row(D_MODEL)
    return _call(
        body, "gdn_out_fwd", (s_len // ts,),
        [tile, pl.BlockSpec((ts, D_MODEL), lambda i: (i, QKV_DIM // D_MODEL)), tile, _full((D_MODEL, D_MODEL)), row, row, row],
        [tile, tile, pl.BlockSpec((ts, 1), lambda i: (i, 0))],
        [_sds((s_len, D_MODEL)), _sds((s_len, D_MODEL)), _sds((s_len, 1))], sem=("parallel",))(o, proj, x, w_out, nw, gain, bias)


def _gdn_out_bwd(dy, xhat, rstd, o, proj, w_out, nw, gain):
    s_len = o.shape[0]
    ts = min(ROW_TILE, s_len)

    def body(dy_ref, xh_ref, rs_ref, o_ref, z_ref, w_ref, nw_ref, g_ref,
             dres_ref, do_ref, dz_ref, on_ref, drb_ref, dg_ref, db_ref, dnw_ref):
        dy_t, xh_t = dy_ref[...], xh_ref[...]
        dr = _ln_bwd(dy_t, xh_t, rs_ref[...], g_ref[...])
        dres_ref[...] = ALPHA * dr
        drb_ref[...] = dr.astype(BF16)
        on, vjp = jax.vjp(_gate_norm, o_ref[...], z_ref[...], nw_ref[...])
        on_ref[...] = on.astype(BF16)
        do, dz, dnw = vjp(_bdot(dr, w_ref[...], NT))
        do_ref[...] = do
        dz_ref[...] = dz.astype(BF16)
        first = pl.program_id(0) == 0
        _acc(dg_ref, first, jnp.sum(dy_t * xh_t, axis=0, keepdims=True))
        _acc(db_ref, first, jnp.sum(dy_t, axis=0, keepdims=True))
        _acc(dnw_ref, first, sum(dnw[:, h * HEAD_DIM:(h + 1) * HEAD_DIM] for h in range(HEADS)))

    tile = pl.BlockSpec((ts, D_MODEL), lambda i: (i, 0))
    row = _row(D_MODEL)
    return _call(
        body, "gdn_out_bwd", (s_len // ts,),
        [tile, tile, pl.BlockSpec((ts, 1), lambda i: (i, 0)), tile,
         pl.BlockSpec((ts, D_MODEL), lambda i: (i, QKV_DIM // D_MODEL)), _full((D_MODEL, D_MODEL)), row, row],
        [tile, tile, tile, tile, tile, row, row, _row(HEAD_DIM)],
        [_sds((s_len, D_MODEL))] * 2 + [_sds((s_len, D_MODEL), BF16)] * 3 + [_sds((1, D_MODEL))] * 2 + [_sds((1, HEAD_DIM))],
        sem=("arbitrary",))(dy, xhat, rstd, o, proj, w_out, nw, gain)


def _loss_head(y, target):
    s_len = y.shape[0]
    ts = min(ROW_TILE, s_len)

    def body(y_ref, t_ref, dy_ref, l_ref):
        err = y_ref[...] - t_ref[...]
        dy_ref[...] = err * (1.0 / D_MODEL)
        part = 0.5 * jnp.sum(jnp.mean(err * err, axis=-1, keepdims=True))
        _acc(l_ref, pl.program_id(0) == 0, part + jnp.zeros((1, 128), F32))

    tile = pl.BlockSpec((ts, D_MODEL), lambda i: (i, 0))
    return _call(body, "loss_head", (s_len // ts,), [tile, tile], [tile, _row(128)],
                 [_sds((s_len, D_MODEL)), _sds((1, 128))], sem=("arbitrary",))(y, target)


def _adamw(w, g, m, v, name):
    r, c = w.shape
    tr = min(ADAM_ROWS, r)

    def body(w_ref, g_ref, m_ref, v_ref, d_ref, nm_ref, nv_ref):
        g_t = g_ref[...]
        nm = ADAM_B1 * m_ref[...] + (1.0 - ADAM_B1) * g_t
        nv = ADAM_B2 * v_ref[...] + (1.0 - ADAM_B2) * (g_t * g_t)
        m_hat = nm / (1.0 - ADAM_B1 ** ADAM_STEP)
        v_hat = nv / (1.0 - ADAM_B2 ** ADAM_STEP)
        d_ref[...] = -ADAM_LR * (m_hat / (jnp.sqrt(v_hat) + ADAM_EPS) + ADAM_WD * w_ref[...])
        nm_ref[...] = nm
        nv_ref[...] = nv

    tile = pl.BlockSpec((tr, c), lambda i: (i, 0))
    return _call(body, name, (r // tr,), [tile] * 4, [tile] * 3, [_sds((r, c))] * 3, sem=("parallel",))(w, g, m, v)


def _assemble_w_in(shards):
    rows = 256
    width = GDN_IN_DIM // N_SHARD

    def body(s_ref, o_ref):
        pad = jnp.zeros((rows, GDN_IN_PAD - GDN_IN_DIM), shards.dtype)
        o_ref[...] = jnp.concatenate([s_ref[j] for j in range(N_SHARD)] + [pad], axis=1)

    return _call(body, "w_in_assemble", (D_MODEL // rows,), [pl.BlockSpec((N_SHARD, rows, width), lambda i: (0, i, 0))],
                 pl.BlockSpec((rows, GDN_IN_PAD), lambda i: (i, 0)), _sds((D_MODEL, GDN_IN_PAD), shards.dtype),
                 sem=("parallel",))(shards)


def _split_w_in(full):
    rows = 256
    width = GDN_IN_DIM // N_SHARD

    def body(f_ref, o_ref):
        f = f_ref[...]
        for j in range(N_SHARD):
            o_ref[j] = f[:, j * width:(j + 1) * width]

    return _call(body, "w_in_split", (D_MODEL // rows,), [pl.BlockSpec((rows, GDN_IN_PAD), lambda i: (i, 0))],
                 pl.BlockSpec((N_SHARD, rows, width), lambda i: (0, i, 0)), _sds((N_SHARD, D_MODEL, width), full.dtype),
                 sem=("parallel",))(full)


def _place():
    x, y, c = lax.axis_index("x"), lax.axis_index("y"), lax.axis_index("c")
    return x, y, c, [(1 - x, y), (x, 1 - y), (1 - x, 1 - y)]


def _all_gather(parts):
    n = len(parts)

    def body(*refs):
        srcs, dsts = refs[:n], refs[n:2 * n]
        send_sems, recv_sems, local_sems = refs[2 * n:]
        x, y, c, chips = _place()
        me = 2 * x + y
        sibling = (x, y, 1 - c)
        local = [pltpu.make_async_copy(srcs[k], dsts[k].at[me], local_sems.at[k]) for k in range(n)]
        for cp in local:
            cp.start()

        def ici(k, j, slot):
            px, py = chips[j]
            return pltpu.make_async_remote_copy(
                src_ref=srcs[k].at[c], dst_ref=dsts[k].at[slot, c], send_sem=send_sems.at[6 * k + j],
                recv_sem=recv_sems.at[6 * k + j], device_id=(px, py, c), device_id_type=MESH)

        def d2d(k, j, half):
            px, py = chips[j]
            view = dsts[k].at[2 * px + py, half]
            return pltpu.make_async_remote_copy(
                src_ref=view, dst_ref=view, send_sem=send_sems.at[6 * k + 3 + j], recv_sem=recv_sems.at[6 * k + 3 + j],
                device_id=sibling, device_id_type=MESH)

        sends = [ici(k, j, me) for k in range(n) for j in range(3)]
        for cp in sends:
            cp.start()
        fwds = []
        for k in range(n):
            for j, (px, py) in enumerate(chips):
                ici(k, j, 2 * px + py).wait_recv()
                fwds.append(d2d(k, j, c))
                fwds[-1].start()
        for k in range(n):
            for j in range(3):
                d2d(k, j, 1 - c).wait_recv()
        for cp in sends + fwds:
            cp.wait_send()
        for cp in local:
            cp.wait()

    return pl.pallas_call(
        body, name="weights_all_gather", out_shape=[_sds((N_SHARD,) + a.shape, a.dtype) for a in parts],
        in_specs=[ANY] * n, out_specs=[ANY] * n,
        scratch_shapes=[pltpu.SemaphoreType.DMA((6 * n,)), pltpu.SemaphoreType.DMA((6 * n,)), pltpu.SemaphoreType.DMA((n,))],
    )(*parts)


class _Group:
    def __init__(self, pieces, wire_dtype):
        self.pieces = pieces
        self.cols = pieces[0].shape[2]
        self.half = [a.shape[1] // 2 for a in pieces]
        self.offsets = [sum(self.half[:k]) for k in range(len(pieces))]
        self.half_rows = sum(self.half)
        self.wire_dtype = wire_dtype


def _swap_pack(groups):
    flat = [(g, k) for g in range(len(groups)) for k in range(len(groups[g].pieces))]
    n, n_g = len(flat), len(groups)

    def body(*refs):
        srcs, mine, other = refs[:n], refs[n:n + n_g], refs[n + n_g:n + 2 * n_g]
        send_sems, recv_sems, local_sems = refs[n + 2 * n_g:]
        x, y, c, _ = _place()
        copies = []
        for idx, (g, k) in enumerate(flat):
            grp = groups[g]
            hr = grp.half[k]
            at = pl.ds(grp.offsets[k], hr)
            copies.append(pltpu.make_async_copy(srcs[idx].at[:, pl.ds(c * hr, hr), :], mine[g].at[:, at, :], local_sems.at[idx]))
            copies.append(pltpu.make_async_remote_copy(
                src_ref=srcs[idx].at[:, pl.ds((1 - c) * hr, hr), :], dst_ref=other[g].at[:, at, :],
                send_sem=send_sems.at[idx], recv_sem=recv_sems.at[idx], device_id=(x, y, 1 - c), device_id_type=MESH))
        for cp in copies:
            cp.start()
        for cp in copies:
            cp.wait()

    shapes = [_sds((N_SHARD, g.half_rows, g.cols)) for g in groups]
    out = pl.pallas_call(
        body, name="grads_swap_pack", out_shape=shapes + shapes, in_specs=[ANY] * n, out_specs=[ANY] * (2 * n_g),
        scratch_shapes=[pltpu.SemaphoreType.DMA((n,)), pltpu.SemaphoreType.DMA((n,)), pltpu.SemaphoreType.DMA((n,))],
    )(*[groups[g].pieces[k] for g, k in flat])
    return out[:n_g], out[n_g:]


def _scatter_chips(parts):
    n = len(parts)

    def body(*refs):
        srcs, dsts = refs[:n], refs[n:2 * n]
        send_sems, recv_sems, local_sems = refs[2 * n:]
        x, y, c, chips = _place()
        me = 2 * x + y
        local = [pltpu.make_async_copy(srcs[k].at[me], dsts[k].at[me], local_sems.at[k]) for k in range(n)]
        for cp in local:
            cp.start()

        def ici(k, j, src_slot, dst_slot):
            px, py = chips[j]
            return pltpu.make_async_remote_copy(
                src_ref=srcs[k].at[src_slot], dst_ref=dsts[k].at[dst_slot], send_sem=send_sems.at[3 * k + j],
                recv_sem=recv_sems.at[3 * k + j], device_id=(px, py, c), device_id_type=MESH)

        sends = [ici(k, j, 2 * chips[j][0] + chips[j][1], me) for k in range(n) for j in range(3)]
        for cp in sends:
            cp.start()
        for k in range(n):
            for j, (px, py) in enumerate(chips):
                ici(k, j, me, 2 * px + py).wait_recv()
        for cp in sends:
            cp.wait_send()
        for cp in local:
            cp.wait()

    return pl.pallas_call(
        body, name="grads_scatter_chips", out_shape=[_sds(a.shape, a.dtype) for a in parts],
        in_specs=[ANY] * n, out_specs=[ANY] * n,
        scratch_shapes=[pltpu.SemaphoreType.DMA((3 * n,)), pltpu.SemaphoreType.DMA((3 * n,)), pltpu.SemaphoreType.DMA((n,))],
    )(*parts)


def _join_unpack(groups, reduced, outs):
    flat = [(g, k) for g in range(len(groups)) for k in range(len(groups[g].pieces))]
    n, n_g = len(flat), len(groups)
    names, shapes = [], []
    for name, shape, _ in outs:
        if name not in names:
            names.append(name)
            shapes.append(shape)

    def body(*refs):
        srcs, dsts = refs[:n_g], refs[n_g:n_g + len(shapes)]
        send_sems, recv_sems, local_sems = refs[n_g + len(shapes):]
        x, y, c, _ = _place()
        copies = []
        for idx, (g, k) in enumerate(flat):
            grp = groups[g]
            hr = grp.half[k]
            name, _, layer = outs[idx]
            dst = dsts[names.index(name)]
            dst = dst if layer is None else dst.at[layer]
            dst = dst.at[pl.ds(c * hr, hr), :]
            src = srcs[g].at[pl.ds(grp.offsets[k], hr), :]
            copies.append(pltpu.make_async_copy(src, dst, local_sems.at[idx]))
            copies.append(pltpu.make_async_remote_copy(
                src_ref=src, dst_ref=dst, send_sem=send_sems.at[idx], recv_sem=recv_sems.at[idx],
                device_id=(x, y, 1 - c), device_id_type=MESH))
        for cp in copies:
            cp.start()
        for cp in copies:
            cp.wait()

    out = pl.pallas_call(
        body, name="grads_join_unpack", out_shape=[_sds(s) for s in shapes], in_specs=[ANY] * n_g, out_specs=[ANY] * len(shapes),
        scratch_shapes=[pltpu.SemaphoreType.DMA((n,)), pltpu.SemaphoreType.DMA((n,)), pltpu.SemaphoreType.DMA((n,))],
    )(*reduced)
    return dict(zip(names, out))


def _row_tile(rows):
    return max(t for t in range(8, min(rows, 640) + 1, 8) if rows % t == 0)


def _add_halves(mine, other, dtype, name):
    n, r, cols = mine.shape
    tr = _row_tile(r)

    def body(a_ref, b_ref, o_ref):
        o_ref[...] = (a_ref[...] + b_ref[...]).astype(dtype)

    tile = pl.BlockSpec((None, tr, cols), lambda s, i: (s, i, 0))
    return _call(body, name, (n, r // tr), [tile, tile], tile, _sds(mine.shape, dtype), sem=("parallel", "parallel"))(mine, other)


def _sum_chips(q, name):
    _, r, cols = q.shape
    tr = _row_tile(r)

    def body(q_ref, o_ref):
        f = lambda j: q_ref[j].astype(F32)
        o_ref[...] = ((f(0) + f(1)) + f(2)) + f(3)

    return _call(body, name, (r // tr,), [pl.BlockSpec((N_SHARD, tr, cols), lambda i: (0, i, 0))],
                 pl.BlockSpec((tr, cols), lambda i: (i, 0)), _sds((r, cols)), sem=("parallel",))(q)


GATHER_F32 = ("ln_gain", "ln_bias", "pool_b", "gdn_conv")
REPLICATED = ("pool_scale", "gdn_a_log", "gdn_dt_bias", "gdn_norm_w", "ple_gate_b")
WEIGHTS = ("ln_gain", "ln_bias", "pool_w", "pool_b", "pool_scale", "gdn_w_in", "gdn_conv", "gdn_a_log", "gdn_dt_bias",
           "gdn_norm_w", "gdn_w_out", "mlp_w1", "mlp_w2", "ple_gate_w", "ple_gate_b", "ple_proj")
SMALL_GRADS = ("ple_proj", "pool_w", "ln_gain", "ln_bias", "pool_b", "gdn_conv") + REPLICATED


def _pack(parts, lanes, row_multiple):
    flat = jnp.concatenate([a.reshape(-1) for a in parts])
    rows = -(-flat.shape[0] // (2 * lanes * row_multiple)) * row_multiple
    return jnp.pad(flat, (0, 2 * rows * lanes - flat.shape[0])).reshape(2, rows, lanes)


def _unpack(flat, shapes):
    out, off = [], 0
    for shp in shapes:
        n = math.prod(shp)
        out.append(flat[..., off:off + n].reshape(flat.shape[:-1] + tuple(shp)))
        off += n
    return out


def _pad_lanes(a, offset, width=128):
    return jnp.pad(a, ((0, 0), (offset, width - offset - a.shape[1])))


def kernel(x, p, ln_gain, ln_bias, pool_w, pool_b, pool_scale, gdn_w_in, gdn_conv, gdn_a_log, gdn_dt_bias, gdn_norm_w, gdn_w_out, mlp_w1, mlp_w2, ple_gate_w, ple_gate_b, ple_proj, loss_target, m_ln_gain, m_ln_bias, m_pool_w, m_pool_b, m_pool_scale, m_gdn_w_in, m_gdn_conv, m_gdn_a_log, m_gdn_dt_bias, m_gdn_norm_w, m_gdn_w_out, m_mlp_w1, m_mlp_w2, m_ple_gate_w, m_ple_gate_b, m_ple_proj, v_ln_gain, v_ln_bias, v_pool_w, v_pool_b, v_pool_scale, v_gdn_w_in, v_gdn_conv, v_gdn_a_log, v_gdn_dt_bias, v_gdn_norm_w, v_gdn_w_out, v_mlp_w1, v_mlp_w2, v_ple_gate_w, v_ple_gate_b, v_ple_proj):
    shard = dict(ln_gain=ln_gain, ln_bias=ln_bias, pool_w=pool_w, pool_b=pool_b, pool_scale=pool_scale, gdn_w_in=gdn_w_in,
                 gdn_conv=gdn_conv, gdn_a_log=gdn_a_log, gdn_dt_bias=gdn_dt_bias, gdn_norm_w=gdn_norm_w, gdn_w_out=gdn_w_out,
                 mlp_w1=mlp_w1, mlp_w2=mlp_w2, ple_gate_w=ple_gate_w, ple_gate_b=ple_gate_b, ple_proj=ple_proj)
    mom = dict(ln_gain=m_ln_gain, ln_bias=m_ln_bias, pool_w=m_pool_w, pool_b=m_pool_b, pool_scale=m_pool_scale,
               gdn_w_in=m_gdn_w_in, gdn_conv=m_gdn_conv, gdn_a_log=m_gdn_a_log, gdn_dt_bias=m_gdn_dt_bias,
               gdn_norm_w=m_gdn_norm_w, gdn_w_out=m_gdn_w_out, mlp_w1=m_mlp_w1, mlp_w2=m_mlp_w2, ple_gate_w=m_ple_gate_w,
               ple_gate_b=m_ple_gate_b, ple_proj=m_ple_proj)
    var = dict(ln_gain=v_ln_gain, ln_bias=v_ln_bias, pool_w=v_pool_w, pool_b=v_pool_b, pool_scale=v_pool_scale,
               gdn_w_in=v_gdn_w_in, gdn_conv=v_gdn_conv, gdn_a_log=v_gdn_a_log, gdn_dt_bias=v_gdn_dt_bias,
               gdn_norm_w=v_gdn_norm_w, gdn_w_out=v_gdn_w_out, mlp_w1=v_mlp_w1, mlp_w2=v_mlp_w2, ple_gate_w=v_ple_gate_w,
               ple_gate_b=v_ple_gate_b, ple_proj=v_ple_proj)

    halves = dict(mlp_w1=mlp_w1, mlp_w2=mlp_w2, ple_gate_w=ple_gate_w, ple_proj=ple_proj,
                  gdn_w_out=gdn_w_out.reshape(2, -1, D_MODEL), gdn_w_in=gdn_w_in.reshape(2, -1, GDN_IN_DIM // N_SHARD),
                  pool_w=pool_w.reshape(2, 2, POOL_GROUP // N_SHARD, POOL_GROUP))
    names = list(halves)
    gathered = _all_gather([halves[n].astype(BF16) for n in names] + [_pack([shard[n] for n in GATHER_F32], 128, 8)])
    st = {n: g.reshape((N_SHARD,) + shard[n].shape) for n, g in zip(names, gathered)}
    st.update(zip(GATHER_F32, _unpack(gathered[-1].reshape(N_SHARD, -1), [shard[n].shape for n in GATHER_F32])))

    cat_last = lambda a: jnp.moveaxis(a, 0, -2).reshape(a.shape[1:-1] + (N_SHARD * a.shape[-1],))
    gain = cat_last(st["ln_gain"])
    bias = cat_last(st["ln_bias"])
    wp = st["pool_w"][:, 0]
    pb = cat_last(st["pool_b"]).reshape(1, D_MODEL)
    ps = pool_scale
    w_in = _assemble_w_in(st["gdn_w_in"][:, 0])
    conv_w = cat_last(st["gdn_conv"])[0]
    w_out = st["gdn_w_out"].reshape(D_MODEL, D_MODEL)
    mlp_w = lambda i: (st["mlp_w1"], st["mlp_w2"], st["ple_gate_w"], ple_gate_b[i:i + 1], st["ple_proj"])
    alog_l = _pad_lanes(gdn_a_log, HEADS)
    dtb_l = _pad_lanes(gdn_dt_bias, HEADS)
    nw = jnp.tile(gdn_norm_w, (1, HEADS))
    ln = lambda i, k: (gain[i, k][None], bias[i, k][None])

    x0 = x[0]
    p0, p1 = p[0, 0], p[1, 0]

    x1, xh1, rs1 = _pool_fwd(x0, wp, pb, ps, *ln(0, 0))
    x2, xh2, rs2 = _mlp_fwd(x1, p0, *mlp_w(0), *ln(0, 1), 0)
    proj = _matmul_nn(x2, w_in, "gdn_in_proj", GDN_IN_PAD // 3)
    q, k, v, gcb = _conv_fwd(proj, conv_w, alog_l, dtb_l)
    u, w, qd, kd, qk, gl, t_inv = _gdn_prep(q, k, v, gcb)
    o, states = _gdn_scan(u, w, qd, kd, qk, gl)
    x3, xh3, rs3 = _gdn_out_fwd(o, proj, x2, w_out, nw, *ln(1, 0))
    x4, xh4, rs4 = _mlp_fwd(x3, p1, *mlp_w(1), *ln(1, 1), 1)
    dy4, loss_l = _loss_head(x4, loss_target[0])

    g_gain = [[None, None], [None, None]]
    g_bias = [[None, None], [None, None]]

    def mlp_grads(i, dy, xh, rs, x_mid, p_i):
        dx, a, dh, dzg, dpp, drb, dg, db, dgb = _mlp_bwd(dy, xh, rs, x_mid, p_i, *mlp_w(i), ln(i, 1)[0], i)
        g_gain[i][1], g_bias[i][1] = dg, db
        return dx, dict(
            mlp_w1=_wgrad(x_mid, dh, f"dw1_{i}", stack_cols=True), mlp_w2=_wgrad(a, drb, f"dw2_{i}").reshape(N_SHARD, -1, D_MODEL),
            ple_gate_w=_wgrad(x_mid, dzg, f"dgate_w_{i}").reshape(N_SHARD, -1, D_MODEL),
            ple_proj=_wgrad(p_i, dpp, f"dproj_{i}", stack_cols=True), ple_gate_b=dgb)

    dx3, gl1 = mlp_grads(1, dy4, xh4, rs4, x3, p1)
    dres, do, dz, on_b, drb3, g_gain[1][0], g_bias[1][0], d_nw = _gdn_out_bwd(dx3, xh3, rs3, o, proj, w_out, nw, ln(1, 0)[0])
    d_wout = _wgrad(on_b, drb3, "dw_out").reshape(N_SHARD, -1, D_MODEL)
    du, dw, dqd, dkd, dqk, dgl = _gdn_scan_bwd(do, u, w, qd, kd, qk, gl, states)
    dq, dk, dv, dgcb = _gdn_prep_bwd(q, k, v, gcb, t_inv, u, w, du, dw, dqd, dkd, dqk, dgl)
    dproj, d_conv, d_alog_l, d_dtb_l = _conv_bwd(proj, dq, dk, dv, dgcb, dz, conv_w, alog_l, dtb_l)
    dx2 = _matmul_nt_add(dproj, w_in, dres, "gdn_in_bwd")
    d_win = _split_w_in(_wgrad(x2, dproj, "dw_in"))
    dx1, gl0 = mlp_grads(0, dx2, xh2, rs2, x1, p0)
    dx0, g_gain[0][0], g_bias[0][0], d_ps, d_pb, d_wp = _pool_bwd(dx1, xh1, rs1, x0, wp, pb, ps, ln(0, 0)[0])

    split_last = lambda a: jnp.moveaxis(a.reshape(a.shape[:-1] + (N_SHARD, a.shape[-1] // N_SHARD)), -2, 0)
    small_st = dict(
        ple_proj=jnp.stack([gl0["ple_proj"], gl1["ple_proj"]], axis=1),
        pool_w=jnp.moveaxis(d_wp.reshape(4, N_SHARD, POOL_GROUP // N_SHARD, POOL_GROUP), 1, 0)[:, None],
        ln_gain=split_last(jnp.stack([jnp.concatenate(r, axis=0) for r in g_gain])),
        ln_bias=split_last(jnp.stack([jnp.concatenate(r, axis=0) for r in g_bias])),
        pool_b=split_last(d_pb.reshape(1, 4, POOL_GROUP)),
        gdn_conv=split_last(d_conv)[:, None],
    )
    rep = dict(pool_scale=d_ps, gdn_a_log=d_alog_l[:, HEADS:2 * HEADS], gdn_dt_bias=d_dtb_l[:, HEADS:2 * HEADS],
               gdn_norm_w=d_nw, ple_gate_b=jnp.concatenate([gl0["ple_gate_b"], gl1["ple_gate_b"]], axis=0))
    for n in REPLICATED:
        small_st[n] = jnp.broadcast_to(rep[n][None], (N_SHARD,) + rep[n].shape)
    small_flat = jnp.concatenate([small_st[n].reshape(N_SHARD, -1) for n in SMALL_GRADS], axis=1)
    small_rows = -(-small_flat.shape[1] // (16 * LANES)) * 16
    small_piece = jnp.pad(small_flat, ((0, 0), (0, small_rows * LANES - small_flat.shape[1]))).reshape(N_SHARD, small_rows, LANES)

    groups = [
        _Group([gl0["mlp_w1"], gl1["mlp_w1"], gl0["mlp_w2"], gl1["mlp_w2"], gl0["ple_gate_w"], gl1["ple_gate_w"], d_wout], BF16),
        _Group([d_win], BF16),
        _Group([small_piece], F32),
    ]
    mine, other = _swap_pack(groups)
    chip_sums = [_add_halves(a, b, g.wire_dtype, f"grads_add_halves_{i}") for i, (a, b, g) in enumerate(zip(mine, other, groups))]
    reduced = [_sum_chips(q_, f"grads_sum_chips_{i}") for i, q_ in enumerate(_scatter_chips(chip_sums))]
    outs = [(n, shard[n].shape, i) for n in ("mlp_w1", "mlp_w2", "ple_gate_w") for i in range(2)]
    outs += [("gdn_w_out", shard["gdn_w_out"].shape, 0), ("gdn_w_in", shard["gdn_w_in"].shape, 0), ("small", (small_rows, LANES), None)]
    grads = _join_unpack(groups, reduced, outs)
    grads.update(zip(SMALL_GRADS, _unpack(grads.pop("small").reshape(-1), [shard[n].shape for n in SMALL_GRADS])))

    delta, new_m, new_v = {}, {}, {}
    small = [n for n in WEIGHTS if shard[n].size < 128 * 128]
    for n in WEIGHTS:
        if n in small:
            continue
        to2d = lambda a, n=n: a.reshape(-1, shard[n].shape[-1])
        d2, m2, v2 = _adamw(to2d(shard[n]), to2d(grads[n]), to2d(mom[n]), to2d(var[n]), "adamw_" + n)
        delta[n], new_m[n], new_v[n] = (t.reshape(shard[n].shape) for t in (d2, m2, v2))
    pk = lambda d: _pack([d[n] for n in small], 128, 8).reshape(-1, 128)
    d2, m2, v2 = _adamw(pk(shard), pk(grads), pk(mom), pk(var), "adamw_small")
    for dst, t in ((delta, d2), (new_m, m2), (new_v, v2)):
        dst.update(zip(small, _unpack(t.reshape(-1), [shard[n].shape for n in small])))

    loss = lax.psum(loss_l[0, 0], ("x", "y", "c"))
    return (loss, dx0[None], *[grads[n] for n in WEIGHTS], *[delta[n] for n in WEIGHTS],
            *[new_m[n] for n in WEIGHTS], *[new_v[n] for n in WEIGHTS])
```

```python
import math

import jax
import jax.numpy as jnp
from jax import lax
from jax.experimental import pallas as pl
from jax.experimental.pallas import tpu as pltpu

F32 = jnp.float32
BF16 = jnp.bfloat16

D_MODEL = 1024
D_FF = 4096
PLE_DIM = 256
N_SHARD = 4
POOL_WINDOWS = (2, 4, 8, 16)
POOL_GROUP = 256
POOL_HALO = 16
HEADS = 8
HEAD_DIM = 128
CHUNK = 64
CONV_WIDTH = 4
CONV_HALO = 8
QKV_DIM = 3 * D_MODEL
GDN_IN_DIM = QKV_DIM + D_MODEL + 2 * HEADS
GDN_IN_PAD = 4224
BA_BLOCK = (QKV_DIM + D_MODEL) // 128
ALPHA = (2.0 * 2) ** 0.25
LN_EPS = 1e-5
RMS_EPS = 1e-6
L2_EPS = 1e-6
ADAM_LR, ADAM_B1, ADAM_B2, ADAM_EPS, ADAM_WD, ADAM_STEP = 0.001, 0.9, 0.999, 1e-08, 0.01, 10

ROW_TILE = 512
CONV_TILE = 256
PREP_CHUNKS = 4
LANES = 1024
ADAM_ROWS = 256

NN = (((1,), (0,)), ((), ()))
NT = (((1,), (1,)), ((), ()))
TN = (((0,), (0,)), ((), ()))
BNN = (((2,), (1,)), ((0,), (0,)))
BNT = (((2,), (2,)), ((0,), (0,)))
BTN = (((1,), (1,)), ((0,), (0,)))
MESH = pl.DeviceIdType.MESH
ANY = pl.BlockSpec(memory_space=pl.ANY)


def _bdot(a, b, dims):
    return lax.dot_general(a.astype(BF16), b.astype(BF16), dims, preferred_element_type=F32)


def _hdot(a, b, dims):
    return lax.dot_general(a, b, dims, precision=lax.Precision.HIGHEST, preferred_element_type=F32)


def _mdot(a, b, dims):
    return lax.dot_general(a, b, dims, precision=lax.Precision.HIGH, preferred_element_type=F32)


def _sigmoid(x):
    return 1.0 / (1.0 + jnp.exp(-x))


def _silu(x):
    return x * _sigmoid(x)


def _softplus(x):
    return jnp.maximum(x, 0.0) + jnp.log1p(jnp.exp(-jnp.abs(x)))


def _call(body, name, grid, in_specs, out_specs, out_shape, scratch=(), sem=None):
    params = pltpu.CompilerParams(dimension_semantics=sem) if sem else None
    return pl.pallas_call(
        body, name=name, grid=grid, in_specs=in_specs, out_specs=out_specs, out_shape=out_shape,
        scratch_shapes=list(scratch), compiler_params=params)


def _row(d):
    return pl.BlockSpec((1, d), lambda *_: (0, 0))


def _full(shape):
    n = len(shape)
    return pl.BlockSpec(shape, lambda *_: (0,) * n)


def _sds(shape, dtype=F32):
    return jax.ShapeDtypeStruct(shape, dtype)


def _ln_fwd(r, gain, bias):
    mu = jnp.mean(r, axis=-1, keepdims=True)
    xc = r - mu
    rstd = lax.rsqrt(jnp.mean(xc * xc, axis=-1, keepdims=True) + LN_EPS)
    xhat = xc * rstd
    return xhat * gain + bias, xhat, rstd


def _ln_bwd(dy, xhat, rstd, gain):
    dxh = dy * gain
    m1 = jnp.mean(dxh, axis=-1, keepdims=True)
    m2 = jnp.mean(dxh * xhat, axis=-1, keepdims=True)
    return rstd * (dxh - m1 - xhat * m2)


def _acc(ref, first, val):
    @pl.when(first)
    def _():
        ref[...] = val

    @pl.when(jnp.logical_not(first))
    def _():
        ref[...] += val


def _pooled_groups(xe, t0, ts):
    pos = (t0 + lax.broadcasted_iota(jnp.int32, (ts, 1), 0) + 1).astype(F32)
    outs = []
    for gi, win in enumerate(POOL_WINDOWS):
        xs = xe[:, gi * POOL_GROUP:(gi + 1) * POOL_GROUP]
        s, k = xs, 1
        while k < win:
            s = s + pltpu.roll(s, k, 0)
            k *= 2
        mean = s[POOL_HALO:] / jnp.minimum(pos, float(win))
        outs.append(mean - xs[POOL_HALO:])
    return outs


def _pool_groups_w(w_ref):
    return [jnp.concatenate([w_ref[s, g] for s in range(N_SHARD)], axis=0) for g in range(4)]


def _pool_fwd(x, wp, pb, ps, gain, bias):
    s_len = x.shape[0]
    ts = min(ROW_TILE, s_len)
    hb = ts // POOL_HALO

    def body(x_ref, halo_ref, w_ref, pb_ref, ps_ref, g_ref, b_ref, y_ref, xhat_ref, rstd_ref):
        i = pl.program_id(0)
        x_t = x_ref[...]
        halo = jnp.where(i > 0, halo_ref[...], 0.0)
        pooled = _pooled_groups(jnp.concatenate([halo, x_t], axis=0), i * ts, ts)
        wg = _pool_groups_w(w_ref)
        y = jnp.concatenate([_bdot(pooled[g], wg[g], NN) for g in range(4)], axis=1) + pb_ref[...]
        r = ALPHA * x_t + y * ps_ref[...]
        y_ref[...], xhat_ref[...], rstd_ref[...] = _ln_fwd(r, g_ref[...], b_ref[...])

    tile = pl.BlockSpec((ts, D_MODEL), lambda i: (i, 0))
    return _call(
        body, "pool_fwd", (s_len // ts,),
        [tile, pl.BlockSpec((POOL_HALO, D_MODEL), lambda i: (jnp.maximum(i * hb - 1, 0), 0)),
         _full(wp.shape), _row(D_MODEL), _row(D_MODEL), _row(D_MODEL), _row(D_MODEL)],
        [tile, tile, pl.BlockSpec((ts, 1), lambda i: (i, 0))],
        [_sds((s_len, D_MODEL)), _sds((s_len, D_MODEL)), _sds((s_len, 1))],
        sem=("parallel",))(x, x, wp, pb, ps, gain, bias)


def _pool_bwd(dy, xhat, rstd, x, wp, pb, ps, gain):
    s_len = x.shape[0]
    ts = min(ROW_TILE, s_len)
    hb = ts // POOL_HALO
    n_t = s_len // ts
    ne = ts + POOL_HALO

    def body(dy_ref, dyn_ref, xh_ref, xhn_ref, rs_ref, rsn_ref, x_ref, xp_ref, w_ref, pb_ref, ps_ref, g_ref,
             dx_ref, dg_ref, db_ref, dps_ref, dpb_ref, dw_ref):
        i = pl.program_id(0)
        more = i < n_t - 1
        dy_t, xh_t = dy_ref[...], xh_ref[...]
        dy_e = jnp.concatenate([dy_t, jnp.where(more, dyn_ref[...], 0.0)], axis=0)
        xh_e = jnp.concatenate([xh_t, xhn_ref[...]], axis=0)
        rs_e = jnp.concatenate([rs_ref[...], rsn_ref[...]], axis=0)
        dr_e = _ln_bwd(dy_e, xh_e, rs_e, g_ref[...])
        dyy_e = dr_e * ps_ref[...]
        pos_e = (i * ts + lax.broadcasted_iota(jnp.int32, (ne, 1), 0) + 1).astype(F32)
        dxs = []
        wg = _pool_groups_w(w_ref)
        for gi, win in enumerate(POOL_WINDOWS):
            sl = slice(gi * POOL_GROUP, (gi + 1) * POOL_GROUP)
            dpool = _bdot(dyy_e[:, sl], wg[gi], NT)
            s, k = dpool / jnp.minimum(pos_e, float(win)), 1
            while k < win:
                s = s + pltpu.roll(s, ne - k, 0)
                k *= 2
            dxs.append(s[:ts] - dpool[:ts])
        dx_ref[...] = ALPHA * dr_e[:ts] + jnp.concatenate(dxs, axis=1)

        x_t = x_ref[...]
        halo = jnp.where(i > 0, xp_ref[...], 0.0)
        pooled = _pooled_groups(jnp.concatenate([halo, x_t], axis=0), i * ts, ts)
        y = jnp.concatenate([_bdot(pooled[g], wg[g], NN) for g in range(4)], axis=1) + pb_ref[...]
        dr_t, dyy_t = dr_e[:ts], dyy_e[:ts]
        first = i == 0
        _acc(dg_ref, first, jnp.sum(dy_t * xh_t, axis=0, keepdims=True))
        _acc(db_ref, first, jnp.sum(dy_t, axis=0, keepdims=True))
        _acc(dps_ref, first, jnp.sum(dr_t * y, axis=0, keepdims=True))
        _acc(dpb_ref, first, jnp.sum(dyy_t, axis=0, keepdims=True))
        for g in range(4):
            _acc(dw_ref.at[g], first, _bdot(pooled[g], dyy_t[:, g * POOL_GROUP:(g + 1) * POOL_GROUP], TN))

    tile = pl.BlockSpec((ts, D_MODEL), lambda i: (i, 0))
    nxt = pl.BlockSpec((POOL_HALO, D_MODEL), lambda i: (jnp.minimum((i + 1) * hb, n_t * hb - 1), 0))
    prv = pl.BlockSpec((POOL_HALO, D_MODEL), lambda i: (jnp.maximum(i * hb - 1, 0), 0))
    rs_t = pl.BlockSpec((ts, 1), lambda i: (i, 0))
    rs_n = pl.BlockSpec((POOL_HALO, 1), lambda i: (jnp.minimum((i + 1) * hb, n_t * hb - 1), 0))
    row = _row(D_MODEL)
    return _call(
        body, "pool_bwd", (n_t,),
        [tile, nxt, tile, nxt, rs_t, rs_n, tile, prv, _full(wp.shape), row, row, row],
        [tile, row, row, row, row, _full((4, POOL_GROUP, POOL_GROUP))],
        [_sds((s_len, D_MODEL))] + [_sds((1, D_MODEL))] * 4 + [_sds((4, POOL_GROUP, POOL_GROUP))],
        sem=("arbitrary",))(dy, dy, xhat, xhat, rstd, rstd, x, x, wp, pb, ps, gain)


def _mlp_weight_specs(layer):
    fc = D_FF // N_SHARD
    return [pl.BlockSpec((None, None, D_MODEL, fc), lambda i, j: (j, layer, 0, 0)),
            pl.BlockSpec((None, None, fc, D_MODEL), lambda i, j: (j, layer, 0, 0)),
            pl.BlockSpec((N_SHARD, None, D_MODEL // N_SHARD, D_MODEL), lambda i, j: (0, layer, 0, 0)),
            _row(D_MODEL),
            pl.BlockSpec((N_SHARD, None, PLE_DIM, D_MODEL // N_SHARD), lambda i, j: (0, layer, 0, 0))]


def _gate_w(gw_ref):
    return gw_ref[...].reshape(D_MODEL, D_MODEL)


def _ple_proj(pj_ref):
    return jnp.concatenate([pj_ref[s] for s in range(N_SHARD)], axis=1)


def _mlp_fwd(x, p, w1s, w2s, gw, gb, proj, gain, bias, layer):
    s_len = x.shape[0]
    ts = min(ROW_TILE, s_len)

    def body(x_ref, p_ref, w1_ref, w2_ref, gw_ref, gb_ref, pj_ref, g_ref, b_ref, y_ref, xhat_ref, rstd_ref,
             acc_ref, xb_ref):
        j = pl.program_id(1)

        @pl.when(j == 0)
        def _():
            x_t = x_ref[...]
            xb_ref[...] = x_t.astype(BF16)
            gate = _sigmoid(_bdot(x_t, _gate_w(gw_ref), NN) + gb_ref[...])
            acc_ref[...] = ALPHA * x_t + gate * _bdot(p_ref[...], _ple_proj(pj_ref), NN)

        h = jnp.maximum(_bdot(xb_ref[...], w1_ref[...], NN), 0.0)
        acc_ref[...] += _bdot(h * h, w2_ref[...], NN)

        @pl.when(j == N_SHARD - 1)
        def _():
            y_ref[...], xhat_ref[...], rstd_ref[...] = _ln_fwd(acc_ref[...], g_ref[...], b_ref[...])

    tile = pl.BlockSpec((ts, D_MODEL), lambda i, j: (i, 0))
    row = _row(D_MODEL)
    return _call(
        body, "mlp_fwd", (s_len // ts, N_SHARD),
        [tile, pl.BlockSpec((ts, PLE_DIM), lambda i, j: (i, 0))] + _mlp_weight_specs(layer) + [row, row],
        [tile, tile, pl.BlockSpec((ts, 1), lambda i, j: (i, 0))],
        [_sds((s_len, D_MODEL)), _sds((s_len, D_MODEL)), _sds((s_len, 1))],
        scratch=[pltpu.VMEM((ts, D_MODEL), F32), pltpu.VMEM((ts, D_MODEL), BF16)],
        sem=("parallel", "arbitrary"))(x, p, w1s, w2s, gw, gb, proj, gain, bias)


def _mlp_bwd(dy, xhat, rstd, x, p, w1s, w2s, gw, gb, proj, gain, layer):
    s_len = x.shape[0]
    ts = min(ROW_TILE, s_len)
    fc = D_FF // N_SHARD

    def body(dy_ref, xh_ref, rs_ref, x_ref, p_ref, w1_ref, w2_ref, gw_ref, gb_ref, pj_ref, g_ref,
             dx_ref, a_ref, dh_ref, dzg_ref, dpp_ref, drb_ref, dg_ref, db_ref, dgb_ref, acc_ref, xb_ref, dr_ref):
        i, j = pl.program_id(0), pl.program_id(1)

        @pl.when(j == 0)
        def _():
            dy_t, xh_t, x_t = dy_ref[...], xh_ref[...], x_ref[...]
            dr = _ln_bwd(dy_t, xh_t, rs_ref[...], g_ref[...])
            xb_ref[...] = x_t.astype(BF16)
            drb = dr.astype(BF16)
            dr_ref[...] = drb
            drb_ref[...] = drb
            gw_full = _gate_w(gw_ref)
            gate = _sigmoid(_bdot(x_t, gw_full, NN) + gb_ref[...])
            pp = _bdot(p_ref[...], _ple_proj(pj_ref), NN)
            dzg = dr * pp * gate * (1.0 - gate)
            dzg_ref[...] = dzg.astype(BF16)
            dpp_ref[...] = (dr * gate).astype(BF16)
            acc_ref[...] = ALPHA * dr + _bdot(dzg, gw_full, NT)
            first = i == 0
            _acc(dg_ref, first, jnp.sum(dy_t * xh_t, axis=0, keepdims=True))
            _acc(db_ref, first, jnp.sum(dy_t, axis=0, keepdims=True))
            _acc(dgb_ref, first, jnp.sum(dzg, axis=0, keepdims=True))

        h = jnp.maximum(_bdot(xb_ref[...], w1_ref[...], NN), 0.0)
        a_ref[...] = (h * h).astype(BF16)
        dh = (_bdot(dr_ref[...], w2_ref[...], NT) * (2.0 * h)).astype(BF16)
        dh_ref[...] = dh
        acc_ref[...] += _bdot(dh, w1_ref[...], NT)

        @pl.when(j == N_SHARD - 1)
        def _():
            dx_ref[...] = acc_ref[...]

    tile = pl.BlockSpec((ts, D_MODEL), lambda i, j: (i, 0))
    ftile = pl.BlockSpec((ts, fc), lambda i, j: (i, j))
    row = _row(D_MODEL)
    return _call(
        body, "mlp_bwd", (s_len // ts, N_SHARD),
        [tile, tile, pl.BlockSpec((ts, 1), lambda i, j: (i, 0)), tile, pl.BlockSpec((ts, PLE_DIM), lambda i, j: (i, 0))]
        + _mlp_weight_specs(layer) + [row],
        [tile, ftile, ftile, tile, tile, tile, row, row, row],
        [_sds((s_len, D_MODEL)), _sds((s_len, D_FF), BF16), _sds((s_len, D_FF), BF16)]
        + [_sds((s_len, D_MODEL), BF16)] * 3 + [_sds((1, D_MODEL))] * 3,
        scratch=[pltpu.VMEM((ts, D_MODEL), F32), pltpu.VMEM((ts, D_MODEL), BF16), pltpu.VMEM((ts, D_MODEL), BF16)],
        sem=("arbitrary", "arbitrary"))(dy, xhat, rstd, x, p, w1s, w2s, gw, gb, proj, gain)


def _wgrad(a, b, name, stack_cols=False):
    s_len, m = a.shape
    n = b.shape[1]
    ts = min(512, s_len)
    tm = min(m, 1024)
    tn = n // N_SHARD if stack_cols else (1408 if n == GDN_IN_PAD else min(n, 1024))
    n_s = s_len // ts

    def body(a_ref, b_ref, o_ref):
        _acc(o_ref, pl.program_id(2) == 0, _bdot(a_ref[...], b_ref[...], TN))

    if stack_cols:
        out_spec = pl.BlockSpec((None, tm, tn), lambda mi, nj, s: (nj, mi, 0))
        out_shape = _sds((N_SHARD, m, tn))
    else:
        out_spec = pl.BlockSpec((tm, tn), lambda mi, nj, s: (mi, nj))
        out_shape = _sds((m, n))
    return _call(
        body, name, (m // tm, n // tn, n_s),
        [pl.BlockSpec((ts, tm), lambda mi, nj, s: (s, mi)), pl.BlockSpec((ts, tn), lambda mi, nj, s: (s, nj))],
        out_spec, out_shape, sem=("parallel", "parallel", "arbitrary"))(a, b)


def _matmul_nn(a, b, name, tn):
    s_len, k = a.shape
    n = b.shape[1]
    ts = min(512, s_len)

    def body(a_ref, b_ref, o_ref):
        o_ref[...] = _bdot(a_ref[...], b_ref[...], NN)

    return _call(
        body, name, (s_len // ts, n // tn),
        [pl.BlockSpec((ts, k), lambda i, j: (i, 0)), pl.BlockSpec((k, tn), lambda i, j: (0, j))],
        pl.BlockSpec((ts, tn), lambda i, j: (i, j)), _sds((s_len, n)), sem=("parallel", "parallel"))(a, b)


def _matmul_nt_add(a, b, res, name):
    s_len, k = a.shape
    n = b.shape[0]
    ts = min(512, s_len)

    def body(a_ref, b_ref, r_ref, o_ref):
        o_ref[...] = r_ref[...] + _bdot(a_ref[...], b_ref[...], NT)

    return _call(
        body, name, (s_len // ts,),
        [pl.BlockSpec((ts, k), lambda i: (i, 0)), _full((n, k)), pl.BlockSpec((ts, n), lambda i: (i, 0))],
        pl.BlockSpec((ts, n), lambda i: (i, 0)), _sds((s_len, n)), sem=("parallel",))(a, b, res)


def _act_qkv(y):
    qkv = _silu(y)
    qs, ks = [], []
    for h in range(HEADS):
        qh = qkv[:, h * HEAD_DIM:(h + 1) * HEAD_DIM]
        kh = qkv[:, D_MODEL + h * HEAD_DIM:D_MODEL + (h + 1) * HEAD_DIM]
        qs.append(qh * (lax.rsqrt(jnp.sum(qh * qh, axis=-1, keepdims=True) + L2_EPS) * HEAD_DIM ** -0.5))
        ks.append(kh * lax.rsqrt(jnp.sum(kh * kh, axis=-1, keepdims=True) + L2_EPS))
    return jnp.concatenate(qs, axis=1), jnp.concatenate(ks, axis=1), qkv[:, 2 * D_MODEL:]


def _act_gb(ba, alog_l, dtb_l, tril):
    lane = lax.broadcasted_iota(jnp.int32, ba.shape, 1)
    g = jnp.where((lane >= HEADS) & (lane < 2 * HEADS), -jnp.exp(alog_l) * _softplus(ba + dtb_l), 0.0)
    return jnp.where(lane < HEADS, _sigmoid(ba), _hdot(tril, g, NN))


def _chunk_tril(t):
    ii = lax.broadcasted_iota(jnp.int32, (t, t), 0)
    jj = lax.broadcasted_iota(jnp.int32, (t, t), 1)
    return ((ii // CHUNK == jj // CHUNK) & (ii >= jj)).astype(F32)


def _conv_rows(xe, w, n_rows):
    y = xe[CONV_HALO:CONV_HALO + n_rows] * w[CONV_WIDTH - 1]
    for j in range(CONV_WIDTH - 1):
        y = y + pltpu.roll(xe, CONV_WIDTH - 1 - j, 0)[CONV_HALO:CONV_HALO + n_rows] * w[j]
    return y


def _conv_fwd(proj, conv_w, alog_l, dtb_l):
    s_len = proj.shape[0]
    ts = min(CONV_TILE, s_len)
    hb = ts // CONV_HALO

    def body(x_ref, xp_ref, ba_ref, w_ref, al_ref, dt_ref, q_ref, k_ref, v_ref, gcb_ref):
        i = pl.program_id(0)
        halo = jnp.where(i > 0, xp_ref[...], 0.0)
        taps = [w_ref[pl.ds(j, 1), :] for j in range(CONV_WIDTH)]
        y = _conv_rows(jnp.concatenate([halo, x_ref[...]], axis=0), taps, ts)
        q_ref[...], k_ref[...], v_ref[...] = _act_qkv(y)
        gcb_ref[...] = _act_gb(ba_ref[...], al_ref[...], dt_ref[...], _chunk_tril(ts))

    tile = pl.BlockSpec((ts, D_MODEL), lambda i: (i, 0))
    return _call(
        body, "gdn_conv_fwd", (s_len // ts,),
        [pl.BlockSpec((ts, QKV_DIM), lambda i: (i, 0)),
         pl.BlockSpec((CONV_HALO, QKV_DIM), lambda i: (jnp.maximum(i * hb - 1, 0), 0)),
         pl.BlockSpec((ts, 128), lambda i: (i, BA_BLOCK)), _full((CONV_WIDTH, QKV_DIM)), _row(128), _row(128)],
        [tile, tile, tile, pl.BlockSpec((ts, 128), lambda i: (i, 0))],
        [_sds((s_len, D_MODEL))] * 3 + [_sds((s_len, 128))],
        sem=("parallel",))(proj, proj, proj, conv_w, alog_l, dtb_l)


def _conv_bwd(proj, dq, dk, dv, dgcb, dz, conv_w, alog_l, dtb_l):
    s_len = proj.shape[0]
    ts = min(CONV_TILE, s_len)
    hb = ts // CONV_HALO
    n_t = s_len // ts
    te = ts + CONV_HALO

    def body(x_ref, xp_ref, xn_ref, ba_ref, dq_ref, dqn_ref, dk_ref, dkn_ref, dv_ref, dvn_ref, dgcb_ref, dz_ref,
             w_ref, al_ref, dt_ref, dp_ref, dw_ref, dal_ref, ddt_ref):
        i = pl.program_id(0)
        more = i < n_t - 1
        w = [w_ref[pl.ds(j, 1), :] for j in range(CONV_WIDTH)]
        x_t = x_ref[...]
        xe = jnp.concatenate([jnp.where(i > 0, xp_ref[...], 0.0), x_t, xn_ref[...]], axis=0)
        y_e, act_vjp = jax.vjp(_act_qkv, _conv_rows(xe, w, te))
        ct = tuple(jnp.concatenate([t[...], jnp.where(more, n[...], 0.0)], axis=0)
                   for t, n in ((dq_ref, dqn_ref), (dk_ref, dkn_ref), (dv_ref, dvn_ref)))
        (dy_e,) = act_vjp(ct)
        dx = dy_e[:ts] * w[CONV_WIDTH - 1]
        for j in range(CONV_WIDTH - 1):
            dx = dx + pltpu.roll(dy_e, te - (CONV_WIDTH - 1 - j), 0)[:ts] * w[j]
        dy_t = dy_e[:ts]
        xe_t = xe[:te]
        dws = [jnp.sum(dy_t * pltpu.roll(xe_t, CONV_WIDTH - 1 - j, 0)[CONV_HALO:], axis=0, keepdims=True)
               for j in range(CONV_WIDTH - 1)]
        dws.append(jnp.sum(dy_t * x_t, axis=0, keepdims=True))
        _, gb_vjp = jax.vjp(lambda ba, al, dt: _act_gb(ba, al, dt, _chunk_tril(ts)), ba_ref[...], al_ref[...], dt_ref[...])
        dba, dal, ddt = gb_vjp(dgcb_ref[...])
        dp_ref[...] = jnp.concatenate([dx.astype(BF16), dz_ref[...], dba.astype(BF16)], axis=1)
        first = i == 0
        for j in range(CONV_WIDTH):
            _acc(dw_ref.at[pl.ds(j, 1), :], first, dws[j])
        _acc(dal_ref, first, dal)
        _acc(ddt_ref, first, ddt)

    tile = pl.BlockSpec((ts, D_MODEL), lambda i: (i, 0))
    nxt = pl.BlockSpec((CONV_HALO, D_MODEL), lambda i: (jnp.minimum((i + 1) * hb, n_t * hb - 1), 0))
    return _call(
        body, "gdn_conv_bwd", (n_t,),
        [pl.BlockSpec((ts, QKV_DIM), lambda i: (i, 0)),
         pl.BlockSpec((CONV_HALO, QKV_DIM), lambda i: (jnp.maximum(i * hb - 1, 0), 0)),
         pl.BlockSpec((CONV_HALO, QKV_DIM), lambda i: (jnp.minimum((i + 1) * hb, n_t * hb - 1), 0)),
         pl.BlockSpec((ts, 128), lambda i: (i, BA_BLOCK)),
         tile, nxt, tile, nxt, tile, nxt, pl.BlockSpec((ts, 128), lambda i: (i, 0)), tile,
         _full((CONV_WIDTH, QKV_DIM)), _row(128), _row(128)],
        [pl.BlockSpec((ts, GDN_IN_PAD), lambda i: (i, 0)), _full((CONV_WIDTH, QKV_DIM)), _row(128), _row(128)],
        [_sds((s_len, GDN_IN_PAD), BF16), _sds((CONV_WIDTH, QKV_DIM)), _sds((1, 128)), _sds((1, 128))],
        sem=("arbitrary",))(proj, proj, proj, proj, dq, dq, dk, dk, dv, dv, dgcb, dz, conv_w, alog_l, dtb_l)


def _tri_inv(a_strict):
    ii = lax.broadcasted_iota(jnp.int32, (CHUNK, CHUNK), 0)
    jj = lax.broadcasted_iota(jnp.int32, (CHUNK, CHUNK), 1)
    x = (ii == jj).astype(F32) - a_strict
    pw = _mdot(a_strict, a_strict, BNN)
    for step in range(5):
        x = x + _mdot(x, pw, BNN)
        if step < 4:
            pw = _mdot(pw, pw, BNN)
    return x


@jax.custom_vjp
def _solved(a_strict, rhs, t, sol):
    return sol


def _solved_fwd(a_strict, rhs, t, sol):
    return sol, (t, sol)


def _solved_bwd(res, dsol):
    t, sol = res
    drhs = _mdot(t, dsol, BTN)
    return -_mdot(drhs, sol, BNT), drhs, jnp.zeros_like(t), jnp.zeros_like(sol)


_solved.defvjp(_solved_fwd, _solved_bwd)


def _prep(q, k, v, gc, beta, solve):
    ii = lax.broadcasted_iota(jnp.int32, (CHUNK, CHUNK), 0)
    jj = lax.broadcasted_iota(jnp.int32, (CHUNK, CHUNK), 1)
    causal, strict = ii >= jj, ii > jj
    gc_row = jnp.sum((ii == jj).astype(F32) * gc, axis=1, keepdims=True)
    decay = jnp.where(causal, jnp.exp(jnp.where(causal, gc - gc_row, 0.0)), 0.0)
    kb = k * beta
    a = jnp.where(strict, _bdot(kb, k, BNT) * decay, 0.0)
    eg = jnp.exp(gc)
    sol = solve(a, jnp.concatenate([v * beta, kb * eg], axis=-1))
    qk = _bdot(q, k, BNT) * decay
    last = lax.broadcasted_iota(jnp.int32, (CHUNK, 1), 0) == CHUNK - 1
    g_last = jnp.sum(jnp.where(last, gc, 0.0), axis=1, keepdims=True)
    kd = k * jnp.exp(g_last - gc)
    gl = jnp.exp(g_last) + jnp.zeros((1, 1, HEAD_DIM), F32)
    return sol[..., :HEAD_DIM], sol[..., HEAD_DIM:], qk, q * eg, kd, gl


def _prep_specs(s_len):
    rows = min(PREP_CHUNKS, s_len // CHUNK) * CHUNK
    m = rows // CHUNK
    hd = pl.BlockSpec((rows, HEAD_DIM), lambda c, h: (c, h))
    gcb = pl.BlockSpec((rows, 128), lambda c, h: (c, 0))
    qk = pl.BlockSpec((None, rows, CHUNK), lambda c, h: (h, c, 0))
    gl = pl.BlockSpec((None, m, HEADS, HEAD_DIM), lambda c, h: (c, 0, 0, 0))
    return rows, m, hd, gcb, qk, gl


def _head_cols(gcb, h, m):
    lane = lax.broadcasted_iota(jnp.int32, gcb.shape, 1)
    pick = lambda at: jnp.sum(jnp.where(lane == at, gcb, 0.0), axis=1, keepdims=True).reshape(m, CHUNK, 1)
    return pick(h + HEADS), pick(h)


def _gdn_prep(q, k, v, gcb):
    s_len = q.shape[0]
    rows, m, hd, gcb_spec, qk_spec, gl_spec = _prep_specs(s_len)

    def body(q_ref, k_ref, v_ref, gcb_ref, u_ref, w_ref, qd_ref, kd_ref, qk_ref, gl_ref, t_ref):
        r3 = lambda ref, d: ref[...].reshape(m, CHUNK, d)
        gc, beta = _head_cols(gcb_ref[...], pl.program_id(1), m)

        def solve(a, rhs):
            t = _tri_inv(a)
            t_ref[...] = t.reshape(rows, CHUNK)
            return _mdot(t, rhs, BNN)

        u, w, qk, qd, kd, gl = _prep(r3(q_ref, HEAD_DIM), r3(k_ref, HEAD_DIM), r3(v_ref, HEAD_DIM), gc, beta, solve)
        u_ref[...] = u.reshape(rows, HEAD_DIM)
        w_ref[...] = w.reshape(rows, HEAD_DIM)
        qd_ref[...] = qd.reshape(rows, HEAD_DIM).astype(BF16)
        kd_ref[...] = kd.reshape(rows, HEAD_DIM).astype(BF16)
        qk_ref[...] = qk.reshape(rows, CHUNK).astype(BF16)
        gl_ref[:, pl.ds(pl.program_id(1), 1), :] = gl

    n_g = s_len // rows
    return _call(
        body, "gdn_prep", (n_g, HEADS), [hd, hd, hd, gcb_spec], [hd, hd, hd, hd, qk_spec, gl_spec, qk_spec],
        [_sds((s_len, D_MODEL))] * 2 + [_sds((s_len, D_MODEL), BF16)] * 2
        + [_sds((HEADS, s_len, CHUNK), BF16), _sds((n_g, m, HEADS, HEAD_DIM)), _sds((HEADS, s_len, CHUNK))],
        sem=("parallel", "arbitrary"))(q, k, v, gcb)


def _gdn_prep_bwd(q, k, v, gcb, t_inv, u, w, du, dw, dqd, dkd, dqk, dgl):
    s_len = q.shape[0]
    rows, m, hd, gcb_spec, qk_spec, gl_spec = _prep_specs(s_len)

    def body(q_ref, k_ref, v_ref, gcb_ref, t_ref, u_ref, w_ref, du_ref, dw_ref, dqd_ref, dkd_ref, dqk_ref, dgl_ref,
             dq_ref, dk_ref, dv_ref, dgcb_ref):
        h = pl.program_id(1)
        r3 = lambda ref, d: ref[...].reshape(m, CHUNK, d)
        gc, beta = _head_cols(gcb_ref[...], h, m)
        t = r3(t_ref, CHUNK)
        sol = jnp.concatenate([r3(u_ref, HEAD_DIM), r3(w_ref, HEAD_DIM)], axis=-1)
        fn = lambda q_, k_, v_, gc_, bt_: _prep(q_, k_, v_, gc_, bt_, lambda a, rhs: _solved(a, rhs, t, sol))
        _, vjp = jax.vjp(fn, r3(q_ref, HEAD_DIM), r3(k_ref, HEAD_DIM), r3(v_ref, HEAD_DIM), gc, beta)
        ct = (r3(du_ref, HEAD_DIM), r3(dw_ref, HEAD_DIM), r3(dqk_ref, CHUNK), r3(dqd_ref, HEAD_DIM), r3(dkd_ref, HEAD_DIM),
              dgl_ref[:, pl.ds(h, 1), :] * (1.0 / HEAD_DIM))
        dq, dk, dv, dgc, dbt = vjp(ct)
        dq_ref[...] = dq.reshape(rows, HEAD_DIM)
        dk_ref[...] = dk.reshape(rows, HEAD_DIM)
        dv_ref[...] = dv.reshape(rows, HEAD_DIM)
        lane = lax.broadcasted_iota(jnp.int32, (rows, 128), 1)
        mine = jnp.where(lane == h, dbt.reshape(rows, 1), 0.0) + jnp.where(lane == h + HEADS, dgc.reshape(rows, 1), 0.0)
        _acc(dgcb_ref, h == 0, mine)

    return _call(
        body, "gdn_prep_bwd", (s_len // rows, HEADS),
        [hd, hd, hd, gcb_spec, qk_spec, hd, hd, hd, hd, hd, hd, qk_spec, gl_spec], [hd, hd, hd, gcb_spec],
        [_sds((s_len, D_MODEL))] * 3 + [_sds((s_len, 128))],
        sem=("parallel", "arbitrary"))(q, k, v, gcb, t_inv, u, w, du, dw, dqd, dkd, dqk, dgl)


def _scan_specs(n_c, m, reverse):
    at = (lambda n: n_c - 1 - n) if reverse else (lambda n: n)
    row = pl.BlockSpec((CHUNK, D_MODEL), lambda n: (at(n), 0))
    qk = pl.BlockSpec((HEADS, CHUNK, CHUNK), lambda n: (0, at(n), 0))
    gl = pl.BlockSpec((None, None, HEADS, HEAD_DIM), lambda n: (at(n) // m, at(n) % m, 0, 0))
    st = pl.BlockSpec((1, HEADS, HEAD_DIM, HEAD_DIM), lambda n: (at(n), 0, 0, 0))
    return row, qk, gl, st


def _gdn_scan(u, w, qd, kd, qk, gl):
    s_len = u.shape[0]
    n_c = s_len // CHUNK
    row, qk_spec, gl_spec, st_spec = _scan_specs(n_c, gl.shape[1], False)

    def body(u_ref, w_ref, qd_ref, kd_ref, qk_ref, gl_ref, o_ref, st_ref, state):
        @pl.when(pl.program_id(0) == 0)
        def _():
            state[...] = jnp.zeros_like(state)

        for h in range(HEADS):
            sl = slice(h * HEAD_DIM, (h + 1) * HEAD_DIM)
            s_f = state[h]
            st_ref[0, h] = s_f
            s_b = s_f.astype(BF16)
            vn = (u_ref[:, sl] - _bdot(w_ref[:, sl], s_b, NN)).astype(BF16)
            o_ref[:, sl] = _bdot(qd_ref[:, sl], s_b, NN) + _bdot(qk_ref[h], vn, NN)
            state[h] = s_f * gl_ref[pl.ds(h, 1), :] + _bdot(kd_ref[:, sl], vn, TN)

    return _call(
        body, "gdn_scan", (n_c,), [row, row, row, row, qk_spec, gl_spec], [row, st_spec],
        [_sds((s_len, D_MODEL)), _sds((n_c, HEADS, HEAD_DIM, HEAD_DIM))],
        scratch=[pltpu.VMEM((HEADS, HEAD_DIM, HEAD_DIM), F32)], sem=("arbitrary",))(u, w, qd, kd, qk, gl)


def _gdn_scan_bwd(do, u, w, qd, kd, qk, gl, states):
    s_len = u.shape[0]
    n_c = s_len // CHUNK
    row, qk_spec, gl_spec, st_spec = _scan_specs(n_c, gl.shape[1], True)

    def body(do_ref, u_ref, w_ref, qd_ref, kd_ref, qk_ref, gl_ref, st_ref,
             du_ref, dw_ref, dqd_ref, dkd_ref, dqk_ref, dgl_ref, dstate):
        @pl.when(pl.program_id(0) == 0)
        def _():
            dstate[...] = jnp.zeros_like(dstate)

        for h in range(HEADS):
            sl = slice(h * HEAD_DIM, (h + 1) * HEAD_DIM)
            s_f = st_ref[0, h]
            s_b = s_f.astype(BF16)
            ds_f = dstate[h]
            ds_b = ds_f.astype(BF16)
            do_b = do_ref[:, sl].astype(BF16)
            w_b, qd_b, kd_b, qk_b = w_ref[:, sl].astype(BF16), qd_ref[:, sl], kd_ref[:, sl], qk_ref[h]
            vn = (u_ref[:, sl] - _bdot(w_b, s_b, NN)).astype(BF16)
            dvn = _bdot(qk_b, do_b, TN) + _bdot(kd_b, ds_b, NN)
            dvn_b = dvn.astype(BF16)
            du_ref[:, sl] = dvn
            dw_ref[:, sl] = -_bdot(dvn_b, s_b, NT)
            dqd_ref[:, sl] = _bdot(do_b, s_b, NT)
            dkd_ref[:, sl] = _bdot(vn, ds_b, NT)
            dqk_ref[h] = _bdot(do_b, vn, NT)
            dgl_ref[pl.ds(h, 1), :] = jnp.sum(s_f * ds_f) + jnp.zeros((1, HEAD_DIM), F32)
            dstate[h] = ds_f * gl_ref[pl.ds(h, 1), :] + _bdot(qd_b, do_b, TN) - _bdot(w_b, dvn_b, TN)

    return _call(
        body, "gdn_scan_bwd", (n_c,), [row, row, row, row, row, qk_spec, gl_spec, st_spec],
        [row, row, row, row, qk_spec, gl_spec],
        [_sds((s_len, D_MODEL))] * 4 + [_sds((HEADS, s_len, CHUNK)), _sds(gl.shape)],
        scratch=[pltpu.VMEM((HEADS, HEAD_DIM, HEAD_DIM), F32)], sem=("arbitrary",))(do, u, w, qd, kd, qk, gl, states)


def _gate_norm(o, z, nw):
    outs = []
    for h in range(HEADS):
        oh = o[:, h * HEAD_DIM:(h + 1) * HEAD_DIM]
        outs.append(oh * lax.rsqrt(jnp.mean(oh * oh, axis=-1, keepdims=True) + RMS_EPS))
    return jnp.concatenate(outs, axis=1) * nw * _silu(z)


def _gdn_out_fwd(o, proj, x, w_out, nw, gain, bias):
    s_len = x.shape[0]
    ts = min(ROW_TILE, s_len)

    def body(o_ref, z_ref, x_ref, w_ref, nw_ref, g_ref, b_ref, y_ref, xhat_ref, rstd_ref):
        on = _gate_norm(o_ref[...], z_ref[...], nw_ref[...])
        r = ALPHA * x_ref[...] + _bdot(on, w_ref[...], NN)
        y_ref[...], xhat_ref[...], rstd_ref[...] = _ln_fwd(r, g_ref[...], b_ref[...])

    tile = pl.BlockSpec((ts, D_MODEL), lambda i: (i, 0))
    row = _row(D_MODEL)
    return _call(
        body, "gdn_out_fwd", (s_len // ts,),
        [tile, pl.BlockSpec((ts, D_MODEL), lambda i: (i, QKV_DIM // D_MODEL)), tile, _full((D_MODEL, D_MODEL)), row, row, row],
        [tile, tile, pl.BlockSpec((ts, 1), lambda i: (i, 0))],
        [_sds((s_len, D_MODEL)), _sds((s_len, D_MODEL)), _sds((s_len, 1))], sem=("parallel",))(o, proj, x, w_out, nw, gain, bias)


def _gdn_out_bwd(dy, xhat, rstd, o, proj, w_out, nw, gain):
    s_len = o.shape[0]
    ts = min(ROW_TILE, s_len)

    def body(dy_ref, xh_ref, rs_ref, o_ref, z_ref, w_ref, nw_ref, g_ref,
             dres_ref, do_ref, dz_ref, on_ref, drb_ref, dg_ref, db_ref, dnw_ref):
        dy_t, xh_t = dy_ref[...], xh_ref[...]
        dr = _ln_bwd(dy_t, xh_t, rs_ref[...], g_ref[...])
        dres_ref[...] = ALPHA * dr
        drb_ref[...] = dr.astype(BF16)
        on, vjp = jax.vjp(_gate_norm, o_ref[...], z_ref[...], nw_ref[...])
        on_ref[...] = on.astype(BF16)
        do, dz, dnw = vjp(_bdot(dr, w_ref[...], NT))
        do_ref[...] = do
        dz_ref[...] = dz.astype(BF16)
        first = pl.program_id(0) == 0
        _acc(dg_ref, first, jnp.sum(dy_t * xh_t, axis=0, keepdims=True))
        _acc(db_ref, first, jnp.sum(dy_t, axis=0, keepdims=True))
        _acc(dnw_ref, first, sum(dnw[:, h * HEAD_DIM:(h + 1) * HEAD_DIM] for h in range(HEADS)))

    tile = pl.BlockSpec((ts, D_MODEL), lambda i: (i, 0))
    row = _row(D_MODEL)
    return _call(
        body, "gdn_out_bwd", (s_len // ts,),
        [tile, tile, pl.BlockSpec((ts, 1), lambda i: (i, 0)), tile,
         pl.BlockSpec((ts, D_MODEL), lambda i: (i, QKV_DIM // D_MODEL)), _full((D_MODEL, D_MODEL)), row, row],
        [tile, tile, tile, tile, tile, row, row, _row(HEAD_DIM)],
        [_sds((s_len, D_MODEL))] * 2 + [_sds((s_len, D_MODEL), BF16)] * 3 + [_sds((1, D_MODEL))] * 2 + [_sds((1, HEAD_DIM))],
        sem=("arbitrary",))(dy, xhat, rstd, o, proj, w_out, nw, gain)


def _loss_head(y, target):
    s_len = y.shape[0]
    ts = min(ROW_TILE, s_len)

    def body(y_ref, t_ref, dy_ref, l_ref):
        err = y_ref[...] - t_ref[...]
        dy_ref[...] = err * (1.0 / D_MODEL)
        part = 0.5 * jnp.sum(jnp.mean(err * err, axis=-1, keepdims=True))
        _acc(l_ref, pl.program_id(0) == 0, part + jnp.zeros((1, 128), F32))

    tile = pl.BlockSpec((ts, D_MODEL), lambda i: (i, 0))
    return _call(body, "loss_head", (s_len // ts,), [tile, tile], [tile, _row(128)],
                 [_sds((s_len, D_MODEL)), _sds((1, 128))], sem=("arbitrary",))(y, target)


def _adamw(w, g, m, v, name):
    r, c = w.shape
    tr = min(ADAM_ROWS, r)

    def body(w_ref, g_ref, m_ref, v_ref, d_ref, nm_ref, nv_ref):
        g_t = g_ref[...]
        nm = ADAM_B1 * m_ref[...] + (1.0 - ADAM_B1) * g_t
        nv = ADAM_B2 * v_ref[...] + (1.0 - ADAM_B2) * (g_t * g_t)
        m_hat = nm / (1.0 - ADAM_B1 ** ADAM_STEP)
        v_hat = nv / (1.0 - ADAM_B2 ** ADAM_STEP)
        d_ref[...] = -ADAM_LR * (m_hat / (jnp.sqrt(v_hat) + ADAM_EPS) + ADAM_WD * w_ref[...])
        nm_ref[...] = nm
        nv_ref[...] = nv

    tile = pl.BlockSpec((tr, c), lambda i: (i, 0))
    return _call(body, name, (r // tr,), [tile] * 4, [tile] * 3, [_sds((r, c))] * 3, sem=("parallel",))(w, g, m, v)


def _assemble_w_in(shards):
    rows = 256
    width = GDN_IN_DIM // N_SHARD

    def body(s_ref, o_ref):
        pad = jnp.zeros((rows, GDN_IN_PAD - GDN_IN_DIM), shards.dtype)
        o_ref[...] = jnp.concatenate([s_ref[j] for j in range(N_SHARD)] + [pad], axis=1)

    return _call(body, "w_in_assemble", (D_MODEL // rows,), [pl.BlockSpec((N_SHARD, rows, width), lambda i: (0, i, 0))],
                 pl.BlockSpec((rows, GDN_IN_PAD), lambda i: (i, 0)), _sds((D_MODEL, GDN_IN_PAD), shards.dtype),
                 sem=("parallel",))(shards)


def _split_w_in(full):
    rows = 256
    width = GDN_IN_DIM // N_SHARD

    def body(f_ref, o_ref):
        f = f_ref[...]
        for j in range(N_SHARD):
            o_ref[j] = f[:, j * width:(j + 1) * width]

    return _call(body, "w_in_split", (D_MODEL // rows,), [pl.BlockSpec((rows, GDN_IN_PAD), lambda i: (i, 0))],
                 pl.BlockSpec((N_SHARD, rows, width), lambda i: (0, i, 0)), _sds((N_SHARD, D_MODEL, width), full.dtype),
                 sem=("parallel",))(full)


def _place():
    x, y, c = lax.axis_index("x"), lax.axis_index("y"), lax.axis_index("c")
    return x, y, c, [(1 - x, y), (x, 1 - y), (1 - x, 1 - y)]


def _row_tile(rows):
    return max(t for t in range(8, min(rows, 640) + 1, 8) if rows % t == 0)


def _place_shard(part, me, dtype, name):
    _, r, c = part.shape
    tr = _row_tile(r)

    def body(me_ref, p_ref, o_ref):
        o_ref[...] = p_ref[...].astype(dtype)

    return pl.pallas_call(
        body, name=name, out_shape=_sds((N_SHARD, 2, r, c), dtype),
        grid_spec=pltpu.PrefetchScalarGridSpec(
            num_scalar_prefetch=1, grid=(2, r // tr),
            in_specs=[pl.BlockSpec((None, tr, c), lambda h, i, me_ref: (h, i, 0))],
            out_specs=pl.BlockSpec((None, None, tr, c), lambda h, i, me_ref: (me_ref[0], h, i, 0))))(me, part)


def _all_gather(bufs):
    n = len(bufs)

    def body(*refs):
        dsts = refs[n:2 * n]
        send_sems, recv_sems = refs[2 * n:]
        x, y, c, chips = _place()
        me = 2 * x + y
        sibling = (x, y, 1 - c)

        def ici(k, j, slot):
            px, py = chips[j]
            view = dsts[k].at[slot, c]
            return pltpu.make_async_remote_copy(
                src_ref=view, dst_ref=view, send_sem=send_sems.at[6 * k + j],
                recv_sem=recv_sems.at[6 * k + j], device_id=(px, py, c), device_id_type=MESH)

        def d2d(k, j, half):
            px, py = chips[j]
            view = dsts[k].at[2 * px + py, half]
            return pltpu.make_async_remote_copy(
                src_ref=view, dst_ref=view, send_sem=send_sems.at[6 * k + 3 + j], recv_sem=recv_sems.at[6 * k + 3 + j],
                device_id=sibling, device_id_type=MESH)

        sends = [ici(k, j, me) for k in range(n) for j in range(3)]
        for cp in sends:
            cp.start()
        fwds = []
        for k in range(n):
            for j, (px, py) in enumerate(chips):
                ici(k, j, 2 * px + py).wait_recv()
                fwds.append(d2d(k, j, c))
                fwds[-1].start()
        for k in range(n):
            for j in range(3):
                d2d(k, j, 1 - c).wait_recv()
        for cp in sends + fwds:
            cp.wait_send()

    return pl.pallas_call(
        body, name="weights_all_gather", out_shape=[_sds(a.shape, a.dtype) for a in bufs],
        in_specs=[ANY] * n, out_specs=[ANY] * n, input_output_aliases={k: k for k in range(n)},
        scratch_shapes=[pltpu.SemaphoreType.DMA((6 * n,)), pltpu.SemaphoreType.DMA((6 * n,))],
    )(*bufs)


def _swap_halves(pieces):
    n = len(pieces)

    def body(*refs):
        srcs, dsts = refs[:n], refs[n:2 * n]
        send_sems, recv_sems = refs[2 * n:]
        x, y, c, _ = _place()
        copies = []
        for k in range(n):
            hr = pieces[k].shape[1] // 2
            copies.append(pltpu.make_async_remote_copy(
                src_ref=srcs[k].at[:, pl.ds((1 - c) * hr, hr), :], dst_ref=dsts[k],
                send_sem=send_sems.at[k], recv_sem=recv_sems.at[k], device_id=(x, y, 1 - c), device_id_type=MESH))
        for cp in copies:
            cp.start()
        for cp in copies:
            cp.wait()

    return pl.pallas_call(
        body, name="grads_swap_halves", out_shape=[_sds((N_SHARD, a.shape[1] // 2, a.shape[2])) for a in pieces],
        in_specs=[ANY] * n, out_specs=[ANY] * n,
        scratch_shapes=[pltpu.SemaphoreType.DMA((n,)), pltpu.SemaphoreType.DMA((n,))])(*pieces)


def _add_half(piece, other, place, dtype, name):
    n, hr, cols = other.shape
    tr = _row_tile(hr)

    def body(pl_ref, a_ref, b_ref, o_ref):
        o_ref[...] = (a_ref[...] + b_ref[...]).astype(dtype)

    tile = pl.BlockSpec((None, tr, cols), lambda s, i, pl_ref: (s, i, 0))
    return pl.pallas_call(
        body, name=name, out_shape=_sds(other.shape, dtype),
        grid_spec=pltpu.PrefetchScalarGridSpec(
            num_scalar_prefetch=1, grid=(n, hr // tr),
            in_specs=[pl.BlockSpec((None, None, tr, cols), lambda s, i, pl_ref: (s, pl_ref[1], i, 0)), tile],
            out_specs=tile))(place, piece.reshape(n, 2, hr, cols), other)


def _scatter_chips(parts):
    n = len(parts)

    def body(*refs):
        srcs, dsts = refs[:n], refs[n:2 * n]
        send_sems, recv_sems = refs[2 * n:]
        x, y, c, chips = _place()
        me = 2 * x + y

        def ici(k, j, src_slot, dst_slot):
            px, py = chips[j]
            return pltpu.make_async_remote_copy(
                src_ref=srcs[k].at[src_slot], dst_ref=dsts[k].at[dst_slot], send_sem=send_sems.at[3 * k + j],
                recv_sem=recv_sems.at[3 * k + j], device_id=(px, py, c), device_id_type=MESH)

        sends = [ici(k, j, 2 * chips[j][0] + chips[j][1], me) for k in range(n) for j in range(3)]
        for cp in sends:
            cp.start()
        for k in range(n):
            for j, (px, py) in enumerate(chips):
                ici(k, j, me, 2 * px + py).wait_recv()
        for cp in sends:
            cp.wait_send()

    return pl.pallas_call(
        body, name="grads_scatter_chips", out_shape=[_sds(a.shape, a.dtype) for a in parts],
        in_specs=[ANY] * n, out_specs=[ANY] * n,
        scratch_shapes=[pltpu.SemaphoreType.DMA((3 * n,)), pltpu.SemaphoreType.DMA((3 * n,))],
    )(*parts)


def _sum_chips(landed, own, place, name):
    _, r, cols = landed.shape
    tr = _row_tile(r)

    def body(pl_ref, q_ref, p_ref, o_ref):
        me = pl_ref[0]
        f = lambda j: jnp.where(me == j, p_ref[...], q_ref[j]).astype(F32)
        o_ref[...] = ((f(0) + f(1)) + f(2)) + f(3)

    return pl.pallas_call(
        body, name=name, out_shape=_sds((2, r, cols)),
        grid_spec=pltpu.PrefetchScalarGridSpec(
            num_scalar_prefetch=1, grid=(r // tr,),
            in_specs=[pl.BlockSpec((N_SHARD, tr, cols), lambda i, pl_ref: (0, i, 0)),
                      pl.BlockSpec((None, tr, cols), lambda i, pl_ref: (pl_ref[0], i, 0))],
            out_specs=pl.BlockSpec((None, tr, cols), lambda i, pl_ref: (pl_ref[1], i, 0))))(place, landed, own)


def _join_halves(bufs):
    n = len(bufs)

    def body(*refs):
        dsts = refs[n:2 * n]
        send_sems, recv_sems = refs[2 * n:]
        x, y, c, _ = _place()
        copies = [pltpu.make_async_remote_copy(
            src_ref=dsts[k].at[c], dst_ref=dsts[k].at[c], send_sem=send_sems.at[k], recv_sem=recv_sems.at[k],
            device_id=(x, y, 1 - c), device_id_type=MESH) for k in range(n)]
        for cp in copies:
            cp.start()
        for cp in copies:
            cp.wait()

    return pl.pallas_call(
        body, name="grads_join_halves", out_shape=[_sds(a.shape) for a in bufs], in_specs=[ANY] * n, out_specs=[ANY] * n,
        input_output_aliases={k: k for k in range(n)},
        scratch_shapes=[pltpu.SemaphoreType.DMA((n,)), pltpu.SemaphoreType.DMA((n,))])(*bufs)


GATHER_F32 = ("ln_gain", "ln_bias", "pool_b", "gdn_conv")
REPLICATED = ("pool_scale", "gdn_a_log", "gdn_dt_bias", "gdn_norm_w", "ple_gate_b")
WEIGHTS = ("ln_gain", "ln_bias", "pool_w", "pool_b", "pool_scale", "gdn_w_in", "gdn_conv", "gdn_a_log", "gdn_dt_bias",
           "gdn_norm_w", "gdn_w_out", "mlp_w1", "mlp_w2", "ple_gate_w", "ple_gate_b", "ple_proj")
SMALL_GRADS = ("ple_proj", "pool_w", "ln_gain", "ln_bias", "pool_b", "gdn_conv") + REPLICATED


def _pack(parts, lanes, row_multiple):
    flat = jnp.concatenate([a.reshape(-1) for a in parts])
    rows = -(-flat.shape[0] // (2 * lanes * row_multiple)) * row_multiple
    return jnp.pad(flat, (0, 2 * rows * lanes - flat.shape[0])).reshape(2, rows, lanes)


def _unpack(flat, shapes):
    out, off = [], 0
    for shp in shapes:
        n = math.prod(shp)
        out.append(flat[..., off:off + n].reshape(flat.shape[:-1] + tuple(shp)))
        off += n
    return out


def _pad_lanes(a, offset, width=128):
    return jnp.pad(a, ((0, 0), (offset, width - offset - a.shape[1])))


def kernel(x, p, ln_gain, ln_bias, pool_w, pool_b, pool_scale, gdn_w_in, gdn_conv, gdn_a_log, gdn_dt_bias, gdn_norm_w, gdn_w_out, mlp_w1, mlp_w2, ple_gate_w, ple_gate_b, ple_proj, loss_target, m_ln_gain, m_ln_bias, m_pool_w, m_pool_b, m_pool_scale, m_gdn_w_in, m_gdn_conv, m_gdn_a_log, m_gdn_dt_bias, m_gdn_norm_w, m_gdn_w_out, m_mlp_w1, m_mlp_w2, m_ple_gate_w, m_ple_gate_b, m_ple_proj, v_ln_gain, v_ln_bias, v_pool_w, v_pool_b, v_pool_scale, v_gdn_w_in, v_gdn_conv, v_gdn_a_log, v_gdn_dt_bias, v_gdn_norm_w, v_gdn_w_out, v_mlp_w1, v_mlp_w2, v_ple_gate_w, v_ple_gate_b, v_ple_proj):
    shard = dict(ln_gain=ln_gain, ln_bias=ln_bias, pool_w=pool_w, pool_b=pool_b, pool_scale=pool_scale, gdn_w_in=gdn_w_in,
                 gdn_conv=gdn_conv, gdn_a_log=gdn_a_log, gdn_dt_bias=gdn_dt_bias, gdn_norm_w=gdn_norm_w, gdn_w_out=gdn_w_out,
                 mlp_w1=mlp_w1, mlp_w2=mlp_w2, ple_gate_w=ple_gate_w, ple_gate_b=ple_gate_b, ple_proj=ple_proj)
    mom = dict(ln_gain=m_ln_gain, ln_bias=m_ln_bias, pool_w=m_pool_w, pool_b=m_pool_b, pool_scale=m_pool_scale,
               gdn_w_in=m_gdn_w_in, gdn_conv=m_gdn_conv, gdn_a_log=m_gdn_a_log, gdn_dt_bias=m_gdn_dt_bias,
               gdn_norm_w=m_gdn_norm_w, gdn_w_out=m_gdn_w_out, mlp_w1=m_mlp_w1, mlp_w2=m_mlp_w2, ple_gate_w=m_ple_gate_w,
               ple_gate_b=m_ple_gate_b, ple_proj=m_ple_proj)
    var = dict(ln_gain=v_ln_gain, ln_bias=v_ln_bias, pool_w=v_pool_w, pool_b=v_pool_b, pool_scale=v_pool_scale,
               gdn_w_in=v_gdn_w_in, gdn_conv=v_gdn_conv, gdn_a_log=v_gdn_a_log, gdn_dt_bias=v_gdn_dt_bias,
               gdn_norm_w=v_gdn_norm_w, gdn_w_out=v_gdn_w_out, mlp_w1=v_mlp_w1, mlp_w2=v_mlp_w2, ple_gate_w=v_ple_gate_w,
               ple_gate_b=v_ple_gate_b, ple_proj=v_ple_proj)

    xi, yi, ci = lax.axis_index("x"), lax.axis_index("y"), lax.axis_index("c")
    me = (2 * xi + yi).reshape(1).astype(jnp.int32)
    place = jnp.stack([2 * xi + yi, ci]).astype(jnp.int32)
    names = ("mlp_w1", "mlp_w2", "ple_gate_w", "ple_proj", "gdn_w_out", "gdn_w_in", "pool_w")
    halves = [shard[n].reshape(2, -1, shard[n].shape[-1]) for n in names] + [_pack([shard[n] for n in GATHER_F32], 128, 8)]
    gathered = _all_gather([_place_shard(a, me, BF16 if k < len(names) else F32, f"place_shard_{k}") for k, a in enumerate(halves)])
    st = {n: g.reshape((N_SHARD,) + shard[n].shape) for n, g in zip(names, gathered)}
    st.update(zip(GATHER_F32, _unpack(gathered[-1].reshape(N_SHARD, -1), [shard[n].shape for n in GATHER_F32])))

    cat_last = lambda a: jnp.moveaxis(a, 0, -2).reshape(a.shape[1:-1] + (N_SHARD * a.shape[-1],))
    gain = cat_last(st["ln_gain"])
    bias = cat_last(st["ln_bias"])
    wp = st["pool_w"][:, 0]
    pb = cat_last(st["pool_b"]).reshape(1, D_MODEL)
    ps = pool_scale
    w_in = _assemble_w_in(st["gdn_w_in"][:, 0])
    conv_w = cat_last(st["gdn_conv"])[0]
    w_out = st["gdn_w_out"].reshape(D_MODEL, D_MODEL)
    mlp_w = lambda i: (st["mlp_w1"], st["mlp_w2"], st["ple_gate_w"], ple_gate_b[i:i + 1], st["ple_proj"])
    alog_l = _pad_lanes(gdn_a_log, HEADS)
    dtb_l = _pad_lanes(gdn_dt_bias, HEADS)
    nw = jnp.tile(gdn_norm_w, (1, HEADS))
    ln = lambda i, k: (gain[i, k][None], bias[i, k][None])

    x0 = x[0]
    p0, p1 = p[0, 0], p[1, 0]

    x1, xh1, rs1 = _pool_fwd(x0, wp, pb, ps, *ln(0, 0))
    x2, xh2, rs2 = _mlp_fwd(x1, p0, *mlp_w(0), *ln(0, 1), 0)
    proj = _matmul_nn(x2, w_in, "gdn_in_proj", GDN_IN_PAD // 3)
    q, k, v, gcb = _conv_fwd(proj, conv_w, alog_l, dtb_l)
    u, w, qd, kd, qk, gl, t_inv = _gdn_prep(q, k, v, gcb)
    o, states = _gdn_scan(u, w, qd, kd, qk, gl)
    x3, xh3, rs3 = _gdn_out_fwd(o, proj, x2, w_out, nw, *ln(1, 0))
    x4, xh4, rs4 = _mlp_fwd(x3, p1, *mlp_w(1), *ln(1, 1), 1)
    dy4, loss_l = _loss_head(x4, loss_target[0])

    g_gain = [[None, None], [None, None]]
    g_bias = [[None, None], [None, None]]

    def mlp_grads(i, dy, xh, rs, x_mid, p_i):
        dx, a, dh, dzg, dpp, drb, dg, db, dgb = _mlp_bwd(dy, xh, rs, x_mid, p_i, *mlp_w(i), ln(i, 1)[0], i)
        g_gain[i][1], g_bias[i][1] = dg, db
        return dx, dict(
            mlp_w1=_wgrad(x_mid, dh, f"dw1_{i}", stack_cols=True), mlp_w2=_wgrad(a, drb, f"dw2_{i}").reshape(N_SHARD, -1, D_MODEL),
            ple_gate_w=_wgrad(x_mid, dzg, f"dgate_w_{i}").reshape(N_SHARD, -1, D_MODEL),
            ple_proj=_wgrad(p_i, dpp, f"dproj_{i}", stack_cols=True), ple_gate_b=dgb)

    dx3, gl1 = mlp_grads(1, dy4, xh4, rs4, x3, p1)
    dres, do, dz, on_b, drb3, g_gain[1][0], g_bias[1][0], d_nw = _gdn_out_bwd(dx3, xh3, rs3, o, proj, w_out, nw, ln(1, 0)[0])
    d_wout = _wgrad(on_b, drb3, "dw_out").reshape(N_SHARD, -1, D_MODEL)
    du, dw, dqd, dkd, dqk, dgl = _gdn_scan_bwd(do, u, w, qd, kd, qk, gl, states)
    dq, dk, dv, dgcb = _gdn_prep_bwd(q, k, v, gcb, t_inv, u, w, du, dw, dqd, dkd, dqk, dgl)
    dproj, d_conv, d_alog_l, d_dtb_l = _conv_bwd(proj, dq, dk, dv, dgcb, dz, conv_w, alog_l, dtb_l)
    dx2 = _matmul_nt_add(dproj, w_in, dres, "gdn_in_bwd")
    d_win = _split_w_in(_wgrad(x2, dproj, "dw_in"))
    dx1, gl0 = mlp_grads(0, dx2, xh2, rs2, x1, p0)
    dx0, g_gain[0][0], g_bias[0][0], d_ps, d_pb, d_wp = _pool_bwd(dx1, xh1, rs1, x0, wp, pb, ps, ln(0, 0)[0])

    split_last = lambda a: jnp.moveaxis(a.reshape(a.shape[:-1] + (N_SHARD, a.shape[-1] // N_SHARD)), -2, 0)
    small_st = dict(
        ple_proj=jnp.stack([gl0["ple_proj"], gl1["ple_proj"]], axis=1),
        pool_w=jnp.moveaxis(d_wp.reshape(4, N_SHARD, POOL_GROUP // N_SHARD, POOL_GROUP), 1, 0)[:, None],
        ln_gain=split_last(jnp.stack([jnp.concatenate(r, axis=0) for r in g_gain])),
        ln_bias=split_last(jnp.stack([jnp.concatenate(r, axis=0) for r in g_bias])),
        pool_b=split_last(d_pb.reshape(1, 4, POOL_GROUP)),
        gdn_conv=split_last(d_conv)[:, None],
    )
    rep = dict(pool_scale=d_ps, gdn_a_log=d_alog_l[:, HEADS:2 * HEADS], gdn_dt_bias=d_dtb_l[:, HEADS:2 * HEADS],
               gdn_norm_w=d_nw, ple_gate_b=jnp.concatenate([gl0["ple_gate_b"], gl1["ple_gate_b"]], axis=0))
    for n in REPLICATED:
        small_st[n] = jnp.broadcast_to(rep[n][None], (N_SHARD,) + rep[n].shape)
    small_flat = jnp.concatenate([small_st[n].reshape(N_SHARD, -1) for n in SMALL_GRADS], axis=1)
    small_rows = -(-small_flat.shape[1] // (16 * LANES)) * 16
    small_piece = jnp.pad(small_flat, ((0, 0), (0, small_rows * LANES - small_flat.shape[1]))).reshape(N_SHARD, small_rows, LANES)

    pieces = [gl0["mlp_w1"], gl1["mlp_w1"], gl0["mlp_w2"], gl1["mlp_w2"], gl0["ple_gate_w"], gl1["ple_gate_w"], d_wout, d_win,
              small_piece]
    wire = [BF16] * (len(pieces) - 1) + [F32]
    others = _swap_halves(pieces)
    chip_sums = [_add_half(a, b, place, t, f"grads_add_half_{i}") for i, (a, b, t) in enumerate(zip(pieces, others, wire))]
    landed = _scatter_chips(chip_sums)
    red = _join_halves([_sum_chips(q_, p_, place, f"grads_sum_chips_{i}") for i, (q_, p_) in enumerate(zip(landed, chip_sums))])
    red = [r.reshape(-1, r.shape[-1]) for r in red]
    grads = dict(mlp_w1=jnp.stack(red[0:2]), mlp_w2=jnp.stack(red[2:4]), ple_gate_w=jnp.stack(red[4:6]),
                 gdn_w_out=red[6][None], gdn_w_in=red[7][None])
    grads.update(zip(SMALL_GRADS, _unpack(red[8].reshape(-1), [shard[n].shape for n in SMALL_GRADS])))

    delta, new_m, new_v = {}, {}, {}
    small = [n for n in WEIGHTS if shard[n].size < 128 * 128]
    for n in WEIGHTS:
        if n in small:
            continue
        to2d = lambda a, n=n: a.reshape(-1, shard[n].shape[-1])
        d2, m2, v2 = _adamw(to2d(shard[n]), to2d(grads[n]), to2d(mom[n]), to2d(var[n]), "adamw_" + n)
        delta[n], new_m[n], new_v[n] = (t.reshape(shard[n].shape) for t in (d2, m2, v2))
    pk = lambda d: _pack([d[n] for n in small], 128, 8).reshape(-1, 128)
    d2, m2, v2 = _adamw(pk(shard), pk(grads), pk(mom), pk(var), "adamw_small")
    for dst, t in ((delta, d2), (new_m, m2), (new_v, v2)):
        dst.update(zip(small, _unpack(t.reshape(-1), [shard[n].shape for n in small])))

    loss = lax.psum(loss_l[0, 0], ("x", "y", "c"))
    return (loss, dx0[None], *[grads[n] for n in WEIGHTS], *[delta[n] for n in WEIGHTS],
            *[new_m[n] for n in WEIGHTS], *[new_v[n] for n in WEIGHTS])
```

```python
import math

import jax
import jax.numpy as jnp
from jax import lax
from jax.experimental import pallas as pl
from jax.experimental.pallas import tpu as pltpu

F32 = jnp.float32
BF16 = jnp.bfloat16

D_MODEL = 1024
D_FF = 4096
PLE_DIM = 256
N_SHARD = 4
POOL_WINDOWS = (2, 4, 8, 16)
POOL_GROUP = 256
POOL_HALO = 16
HEADS = 8
HEAD_DIM = 128
CHUNK = 64
CONV_WIDTH = 4
CONV_HALO = 8
QKV_DIM = 3 * D_MODEL
GDN_IN_DIM = QKV_DIM + D_MODEL + 2 * HEADS
GDN_IN_PAD = 4224
BA_BLOCK = (QKV_DIM + D_MODEL) // 128
ALPHA = (2.0 * 2) ** 0.25
LN_EPS = 1e-5
RMS_EPS = 1e-6
L2_EPS = 1e-6
ADAM_LR, ADAM_B1, ADAM_B2, ADAM_EPS, ADAM_WD, ADAM_STEP = 0.001, 0.9, 0.999, 1e-08, 0.01, 10

ROW_TILE = 512
CONV_TILE = 256
PREP_CHUNKS = 4
LANES = 1024
ADAM_ROWS = 256

NN = (((1,), (0,)), ((), ()))
NT = (((1,), (1,)), ((), ()))
TN = (((0,), (0,)), ((), ()))
BNN = (((2,), (1,)), ((0,), (0,)))
BNT = (((2,), (2,)), ((0,), (0,)))
BTN = (((1,), (1,)), ((0,), (0,)))
MESH = pl.DeviceIdType.MESH
ANY = pl.BlockSpec(memory_space=pl.ANY)


def _bdot(a, b, dims):
    return lax.dot_general(a.astype(BF16), b.astype(BF16), dims, preferred_element_type=F32)


def _hdot(a, b, dims):
    return lax.dot_general(a, b, dims, precision=lax.Precision.HIGHEST, preferred_element_type=F32)


def _mdot(a, b, dims):
    return lax.dot_general(a, b, dims, precision=lax.Precision.HIGH, preferred_element_type=F32)


def _sigmoid(x):
    return 0.5 * jnp.tanh(0.5 * x) + 0.5


def _silu(x):
    return x * _sigmoid(x)


def _softplus(x):
    return jnp.maximum(x, 0.0) + jnp.log1p(jnp.exp(-jnp.abs(x)))


def _call(body, name, grid, in_specs, out_specs, out_shape, scratch=(), sem=None, aliases=None):
    params = pltpu.CompilerParams(dimension_semantics=sem) if sem else None
    return pl.pallas_call(
        body, name=name, grid=grid, in_specs=in_specs, out_specs=out_specs, out_shape=out_shape,
        scratch_shapes=list(scratch), compiler_params=params, input_output_aliases=aliases or {})


def _row(d):
    return pl.BlockSpec((1, d), lambda *_: (0, 0))


def _full(shape):
    n = len(shape)
    return pl.BlockSpec(shape, lambda *_: (0,) * n)


def _sds(shape, dtype=F32):
    return jax.ShapeDtypeStruct(shape, dtype)


def _ln_fwd(r, gain, bias):
    mu = jnp.mean(r, axis=-1, keepdims=True)
    xc = r - mu
    rstd = lax.rsqrt(jnp.mean(xc * xc, axis=-1, keepdims=True) + LN_EPS)
    xhat = xc * rstd
    return xhat * gain + bias, xhat, rstd


def _ln_bwd(dy, xhat, rstd, gain):
    dxh = dy * gain
    m1 = jnp.mean(dxh, axis=-1, keepdims=True)
    m2 = jnp.mean(dxh * xhat, axis=-1, keepdims=True)
    return rstd * (dxh - m1 - xhat * m2)


def _acc(ref, first, val):
    @pl.when(first)
    def _():
        ref[...] = val

    @pl.when(jnp.logical_not(first))
    def _():
        ref[...] += val


def _pooled_groups(xe, t0, ts):
    pos = (t0 + lax.broadcasted_iota(jnp.int32, (ts, 1), 0) + 1).astype(F32)
    outs = []
    for gi, win in enumerate(POOL_WINDOWS):
        xs = xe[:, gi * POOL_GROUP:(gi + 1) * POOL_GROUP]
        s, k = xs, 1
        while k < win:
            s = s + pltpu.roll(s, k, 0)
            k *= 2
        mean = s[POOL_HALO:] / jnp.minimum(pos, float(win))
        outs.append(mean - xs[POOL_HALO:])
    return outs


def _pool_groups_w(w_ref):
    return [jnp.concatenate([w_ref[s, g] for s in range(N_SHARD)], axis=0) for g in range(4)]


def _pool_fwd(x, wp, pb, ps, gain, bias):
    s_len = x.shape[0]
    ts = min(ROW_TILE, s_len)
    hb = ts // POOL_HALO

    def body(x_ref, halo_ref, w_ref, pb_ref, ps_ref, g_ref, b_ref, y_ref, xhat_ref, rstd_ref):
        i = pl.program_id(0)
        x_t = x_ref[...]
        halo = jnp.where(i > 0, halo_ref[...], 0.0)
        pooled = _pooled_groups(jnp.concatenate([halo, x_t], axis=0), i * ts, ts)
        wg = _pool_groups_w(w_ref)
        y = jnp.concatenate([_bdot(pooled[g], wg[g], NN) for g in range(4)], axis=1) + pb_ref[...]
        r = ALPHA * x_t + y * ps_ref[...]
        y_ref[...], xhat_ref[...], rstd_ref[...] = _ln_fwd(r, g_ref[...], b_ref[...])

    tile = pl.BlockSpec((ts, D_MODEL), lambda i: (i, 0))
    return _call(
        body, "pool_fwd", (s_len // ts,),
        [tile, pl.BlockSpec((POOL_HALO, D_MODEL), lambda i: (jnp.maximum(i * hb - 1, 0), 0)),
         _full(wp.shape), _row(D_MODEL), _row(D_MODEL), _row(D_MODEL), _row(D_MODEL)],
        [tile, tile, pl.BlockSpec((ts, 1), lambda i: (i, 0))],
        [_sds((s_len, D_MODEL)), _sds((s_len, D_MODEL)), _sds((s_len, 1))],
        sem=("parallel",))(x, x, wp, pb, ps, gain, bias)


def _pool_bwd(dy, xhat, rstd, x, wp, pb, ps, gain):
    s_len = x.shape[0]
    ts = min(ROW_TILE, s_len)
    hb = ts // POOL_HALO
    n_t = s_len // ts
    ne = ts + POOL_HALO

    def body(dy_ref, dyn_ref, xh_ref, xhn_ref, rs_ref, rsn_ref, x_ref, xp_ref, w_ref, pb_ref, ps_ref, g_ref,
             dx_ref, dg_ref, db_ref, dps_ref, dpb_ref, dw_ref):
        i = pl.program_id(0)
        more = i < n_t - 1
        dy_t, xh_t = dy_ref[...], xh_ref[...]
        dy_e = jnp.concatenate([dy_t, jnp.where(more, dyn_ref[...], 0.0)], axis=0)
        xh_e = jnp.concatenate([xh_t, xhn_ref[...]], axis=0)
        rs_e = jnp.concatenate([rs_ref[...], rsn_ref[...]], axis=0)
        dr_e = _ln_bwd(dy_e, xh_e, rs_e, g_ref[...])
        dyy_e = dr_e * ps_ref[...]
        pos_e = (i * ts + lax.broadcasted_iota(jnp.int32, (ne, 1), 0) + 1).astype(F32)
        dxs = []
        wg = _pool_groups_w(w_ref)
        for gi, win in enumerate(POOL_WINDOWS):
            sl = slice(gi * POOL_GROUP, (gi + 1) * POOL_GROUP)
            dpool = _bdot(dyy_e[:, sl], wg[gi], NT)
            s, k = dpool / jnp.minimum(pos_e, float(win)), 1
            while k < win:
                s = s + pltpu.roll(s, ne - k, 0)
                k *= 2
            dxs.append(s[:ts] - dpool[:ts])
        dx_ref[...] = ALPHA * dr_e[:ts] + jnp.concatenate(dxs, axis=1)

        x_t = x_ref[...]
        halo = jnp.where(i > 0, xp_ref[...], 0.0)
        pooled = _pooled_groups(jnp.concatenate([halo, x_t], axis=0), i * ts, ts)
        y = jnp.concatenate([_bdot(pooled[g], wg[g], NN) for g in range(4)], axis=1) + pb_ref[...]
        dr_t, dyy_t = dr_e[:ts], dyy_e[:ts]
        first = i == 0
        _acc(dg_ref, first, jnp.sum(dy_t * xh_t, axis=0, keepdims=True))
        _acc(db_ref, first, jnp.sum(dy_t, axis=0, keepdims=True))
        _acc(dps_ref, first, jnp.sum(dr_t * y, axis=0, keepdims=True))
        _acc(dpb_ref, first, jnp.sum(dyy_t, axis=0, keepdims=True))
        for g in range(4):
            _acc(dw_ref.at[g], first, _bdot(pooled[g], dyy_t[:, g * POOL_GROUP:(g + 1) * POOL_GROUP], TN))

    tile = pl.BlockSpec((ts, D_MODEL), lambda i: (i, 0))
    nxt = pl.BlockSpec((POOL_HALO, D_MODEL), lambda i: (jnp.minimum((i + 1) * hb, n_t * hb - 1), 0))
    prv = pl.BlockSpec((POOL_HALO, D_MODEL), lambda i: (jnp.maximum(i * hb - 1, 0), 0))
    rs_t = pl.BlockSpec((ts, 1), lambda i: (i, 0))
    rs_n = pl.BlockSpec((POOL_HALO, 1), lambda i: (jnp.minimum((i + 1) * hb, n_t * hb - 1), 0))
    row = _row(D_MODEL)
    return _call(
        body, "pool_bwd", (n_t,),
        [tile, nxt, tile, nxt, rs_t, rs_n, tile, prv, _full(wp.shape), row, row, row],
        [tile, row, row, row, row, _full((4, POOL_GROUP, POOL_GROUP))],
        [_sds((s_len, D_MODEL))] + [_sds((1, D_MODEL))] * 4 + [_sds((4, POOL_GROUP, POOL_GROUP))],
        sem=("arbitrary",))(dy, dy, xhat, xhat, rstd, rstd, x, x, wp, pb, ps, gain)


def _mlp_weight_specs():
    fc = D_FF // N_SHARD
    return [pl.BlockSpec((None, D_MODEL, fc), lambda i, j: (j, 0, 0)),
            pl.BlockSpec((None, fc, D_MODEL), lambda i, j: (j, 0, 0)),
            _full((N_SHARD, D_MODEL // N_SHARD, D_MODEL)),
            _row(D_MODEL),
            _full((N_SHARD, PLE_DIM, D_MODEL // N_SHARD))]


def _gate_w(gw_ref):
    return gw_ref[...].reshape(D_MODEL, D_MODEL)


def _ple_proj(pj_ref):
    return jnp.concatenate([pj_ref[s] for s in range(N_SHARD)], axis=1)


def _mlp_fwd(x, p, w1s, w2s, gw, gb, proj, gain, bias, name, gather=()):
    s_len = x.shape[0]
    ts = min(ROW_TILE, s_len)
    n_i = s_len // ts
    n_g = len(gather)

    def body(*refs):
        x_ref, p_ref, w1_ref, w2_ref, gw_ref, gb_ref, pj_ref, g_ref, b_ref = refs[:9]
        y_ref, xhat_ref, rstd_ref = refs[9 + n_g:12 + n_g]
        acc_ref, xb_ref = refs[12 + 2 * n_g:14 + 2 * n_g]
        i, j = pl.program_id(0), pl.program_id(1)
        if n_g:
            start, finish = _gather_steps(refs[12 + n_g:12 + 2 * n_g], *refs[14 + 2 * n_g:])
            pl.when((i == 0) & (j == 0))(start)

        @pl.when(j == 0)
        def _():
            x_t = x_ref[...]
            xb_ref[...] = x_t.astype(BF16)
            gate = _sigmoid(_bdot(x_t, _gate_w(gw_ref), NN) + gb_ref[...])
            acc_ref[...] = ALPHA * x_t + gate * _bdot(p_ref[...], _ple_proj(pj_ref), NN)

        h = jnp.maximum(_bdot(xb_ref[...], w1_ref[...], NN), 0.0)
        acc_ref[...] += _bdot(h * h, w2_ref[...], NN)

        @pl.when(j == N_SHARD - 1)
        def _():
            y_ref[...], xhat_ref[...], rstd_ref[...] = _ln_fwd(acc_ref[...], g_ref[...], b_ref[...])

        if n_g:
            pl.when((i == n_i - 1) & (j == N_SHARD - 1))(finish)

    tile = pl.BlockSpec((ts, D_MODEL), lambda i, j: (i, 0))
    row = _row(D_MODEL)
    out = _call(
        body, name, (n_i, N_SHARD),
        [tile, pl.BlockSpec((ts, PLE_DIM), lambda i, j: (i, 0))] + _mlp_weight_specs() + [row, row] + [ANY] * n_g,
        [tile, tile, pl.BlockSpec((ts, 1), lambda i, j: (i, 0))] + [ANY] * n_g,
        [_sds((s_len, D_MODEL)), _sds((s_len, D_MODEL)), _sds((s_len, 1))] + [_sds(a.shape, a.dtype) for a in gather],
        scratch=[pltpu.VMEM((ts, D_MODEL), F32), pltpu.VMEM((ts, D_MODEL), BF16)] + (_gather_sems(n_g) if n_g else []),
        sem=("arbitrary", "arbitrary"), aliases={9 + k: 3 + k for k in range(n_g)},
    )(x, p, w1s, w2s, gw, gb, proj, gain, bias, *gather)
    return out[:3], out[3:]


def _mlp_bwd(dy, xhat, rstd, x, p, w1s, w2s, gw, gb, proj, gain, name, scatter=()):
    s_len = x.shape[0]
    ts = min(ROW_TILE, s_len)
    fc = D_FF // N_SHARD
    n_i = s_len // ts
    n_s = len(scatter)

    def body(*refs):
        dy_ref, xh_ref, rs_ref, x_ref, p_ref, w1_ref, w2_ref, gw_ref, gb_ref, pj_ref, g_ref = refs[:11]
        dx_ref, a_ref, dh_ref, dzg_ref, dpp_ref, drb_ref, dg_ref, db_ref, dgb_ref = refs[11 + n_s:20 + n_s]
        acc_ref, xb_ref, dr_ref = refs[20 + 2 * n_s:23 + 2 * n_s]
        i, j = pl.program_id(0), pl.program_id(1)
        if n_s:
            start, finish = _scatter_steps(refs[11:11 + n_s], refs[20 + n_s:20 + 2 * n_s], *refs[23 + 2 * n_s:])
            pl.when((i == 0) & (j == 0))(start)

        @pl.when(j == 0)
        def _():
            dy_t, xh_t, x_t = dy_ref[...], xh_ref[...], x_ref[...]
            dr = _ln_bwd(dy_t, xh_t, rs_ref[...], g_ref[...])
            xb_ref[...] = x_t.astype(BF16)
            drb = dr.astype(BF16)
            dr_ref[...] = drb
            drb_ref[...] = drb
            gw_full = _gate_w(gw_ref)
            gate = _sigmoid(_bdot(x_t, gw_full, NN) + gb_ref[...])
            pp = _bdot(p_ref[...], _ple_proj(pj_ref), NN)
            dzg = dr * pp * gate * (1.0 - gate)
            dzg_ref[...] = dzg.astype(BF16)
            dpp_ref[...] = (dr * gate).astype(BF16)
            acc_ref[...] = ALPHA * dr + _bdot(dzg, gw_full, NT)
            first = i == 0
            _acc(dg_ref, first, jnp.sum(dy_t * xh_t, axis=0, keepdims=True))
            _acc(db_ref, first, jnp.sum(dy_t, axis=0, keepdims=True))
            _acc(dgb_ref, first, jnp.sum(dzg, axis=0, keepdims=True))

        h = jnp.maximum(_bdot(xb_ref[...], w1_ref[...], NN), 0.0)
        a_ref[...] = (h * h).astype(BF16)
        dh = (_bdot(dr_ref[...], w2_ref[...], NT) * (2.0 * h)).astype(BF16)
        dh_ref[...] = dh
        acc_ref[...] += _bdot(dh, w1_ref[...], NT)

        @pl.when(j == N_SHARD - 1)
        def _():
            dx_ref[...] = acc_ref[...]

        if n_s:
            pl.when((i == n_i - 1) & (j == N_SHARD - 1))(finish)

    tile = pl.BlockSpec((ts, D_MODEL), lambda i, j: (i, 0))
    ftile = pl.BlockSpec((ts, fc), lambda i, j: (i, j))
    row = _row(D_MODEL)
    out = _call(
        body, name, (n_i, N_SHARD),
        [tile, tile, pl.BlockSpec((ts, 1), lambda i, j: (i, 0)), tile, pl.BlockSpec((ts, PLE_DIM), lambda i, j: (i, 0))]
        + _mlp_weight_specs() + [row] + [ANY] * n_s,
        [tile, ftile, ftile, tile, tile, tile, row, row, row] + [ANY] * n_s,
        [_sds((s_len, D_MODEL)), _sds((s_len, D_FF), BF16), _sds((s_len, D_FF), BF16)]
        + [_sds((s_len, D_MODEL), BF16)] * 3 + [_sds((1, D_MODEL))] * 3 + [_sds(a.shape, a.dtype) for a in scatter],
        scratch=[pltpu.VMEM((ts, D_MODEL), F32), pltpu.VMEM((ts, D_MODEL), BF16), pltpu.VMEM((ts, D_MODEL), BF16)]
        + (_scatter_sems(n_s) if n_s else []),
        sem=("arbitrary", "arbitrary"))(dy, xhat, rstd, x, p, w1s, w2s, gw, gb, proj, gain, *scatter)
    return out[:9], out[9:]


def _wgrad(a, b, name, stack_cols=False):
    s_len, m = a.shape
    n = b.shape[1]
    ts = min(2048 if a.dtype == BF16 and b.dtype == BF16 else 1024, s_len)
    tm = min(m, 1024)
    tn = n // N_SHARD if stack_cols else (1408 if n == GDN_IN_PAD else min(n, 1024))
    n_s = s_len // ts

    def body(a_ref, b_ref, o_ref):
        _acc(o_ref, pl.program_id(2) == 0, _bdot(a_ref[...], b_ref[...], TN))

    if stack_cols:
        out_spec = pl.BlockSpec((None, tm, tn), lambda mi, nj, s: (nj, mi, 0))
        out_shape = _sds((N_SHARD, m, tn))
    else:
        out_spec = pl.BlockSpec((tm, tn), lambda mi, nj, s: (mi, nj))
        out_shape = _sds((m, n))
    return _call(
        body, name, (m // tm, n // tn, n_s),
        [pl.BlockSpec((ts, tm), lambda mi, nj, s: (s, mi)), pl.BlockSpec((ts, tn), lambda mi, nj, s: (s, nj))],
        out_spec, out_shape, sem=("parallel", "parallel", "arbitrary"))(a, b)


def _matmul_nn(a, b, name, tn):
    s_len, k = a.shape
    n = b.shape[1]
    ts = min(512, s_len)

    def body(a_ref, b_ref, o_ref):
        o_ref[...] = _bdot(a_ref[...], b_ref[...], NN)

    return _call(
        body, name, (s_len // ts, n // tn),
        [pl.BlockSpec((ts, k), lambda i, j: (i, 0)), pl.BlockSpec((k, tn), lambda i, j: (0, j))],
        pl.BlockSpec((ts, tn), lambda i, j: (i, j)), _sds((s_len, n)), sem=("parallel", "parallel"))(a, b)


def _matmul_nt_add(a, b, res, name):
    s_len, k = a.shape
    n = b.shape[0]
    ts = min(512, s_len)

    def body(a_ref, b_ref, r_ref, o_ref):
        o_ref[...] = r_ref[...] + _bdot(a_ref[...], b_ref[...], NT)

    return _call(
        body, name, (s_len // ts,),
        [pl.BlockSpec((ts, k), lambda i: (i, 0)), _full((n, k)), pl.BlockSpec((ts, n), lambda i: (i, 0))],
        pl.BlockSpec((ts, n), lambda i: (i, 0)), _sds((s_len, n)), sem=("parallel",))(a, b, res)


def _act_qkv(y):
    qkv = _silu(y)
    qs, ks = [], []
    for h in range(HEADS):
        qh = qkv[:, h * HEAD_DIM:(h + 1) * HEAD_DIM]
        kh = qkv[:, D_MODEL + h * HEAD_DIM:D_MODEL + (h + 1) * HEAD_DIM]
        qs.append(qh * (lax.rsqrt(jnp.sum(qh * qh, axis=-1, keepdims=True) + L2_EPS) * HEAD_DIM ** -0.5))
        ks.append(kh * lax.rsqrt(jnp.sum(kh * kh, axis=-1, keepdims=True) + L2_EPS))
    return jnp.concatenate(qs, axis=1), jnp.concatenate(ks, axis=1), qkv[:, 2 * D_MODEL:]


def _act_gb(ba, alog_l, dtb_l, tril):
    lane = lax.broadcasted_iota(jnp.int32, ba.shape, 1)
    g = jnp.where((lane >= HEADS) & (lane < 2 * HEADS), -jnp.exp(alog_l) * _softplus(ba + dtb_l), 0.0)
    return jnp.where(lane < HEADS, _sigmoid(ba), _hdot(tril, g, NN))


def _chunk_tril(t):
    ii = lax.broadcasted_iota(jnp.int32, (t, t), 0)
    jj = lax.broadcasted_iota(jnp.int32, (t, t), 1)
    return ((ii // CHUNK == jj // CHUNK) & (ii >= jj)).astype(F32)


def _conv_rows(xe, w, n_rows):
    y = xe[CONV_HALO:CONV_HALO + n_rows] * w[CONV_WIDTH - 1]
    for j in range(CONV_WIDTH - 1):
        y = y + pltpu.roll(xe, CONV_WIDTH - 1 - j, 0)[CONV_HALO:CONV_HALO + n_rows] * w[j]
    return y


def _conv_fwd(proj, conv_w, alog_l, dtb_l):
    s_len = proj.shape[0]
    ts = min(CONV_TILE, s_len)
    hb = ts // CONV_HALO

    def body(x_ref, xp_ref, ba_ref, w_ref, al_ref, dt_ref, q_ref, k_ref, v_ref, gcb_ref):
        i = pl.program_id(0)
        halo = jnp.where(i > 0, xp_ref[...], 0.0)
        taps = [w_ref[pl.ds(j, 1), :] for j in range(CONV_WIDTH)]
        y = _conv_rows(jnp.concatenate([halo, x_ref[...]], axis=0), taps, ts)
        q_ref[...], k_ref[...], v_ref[...] = _act_qkv(y)
        gcb_ref[...] = _act_gb(ba_ref[...], al_ref[...], dt_ref[...], _chunk_tril(ts))

    tile = pl.BlockSpec((ts, D_MODEL), lambda i: (i, 0))
    return _call(
        body, "gdn_conv_fwd", (s_len // ts,),
        [pl.BlockSpec((ts, QKV_DIM), lambda i: (i, 0)),
         pl.BlockSpec((CONV_HALO, QKV_DIM), lambda i: (jnp.maximum(i * hb - 1, 0), 0)),
         pl.BlockSpec((ts, 128), lambda i: (i, BA_BLOCK)), _full((CONV_WIDTH, QKV_DIM)), _row(128), _row(128)],
        [tile, tile, tile, pl.BlockSpec((ts, 128), lambda i: (i, 0))],
        [_sds((s_len, D_MODEL))] * 3 + [_sds((s_len, 128))],
        sem=("parallel",))(proj, proj, proj, conv_w, alog_l, dtb_l)


def _conv_bwd(proj, dq, dk, dv, dgcb, dz, conv_w, alog_l, dtb_l):
    s_len = proj.shape[0]
    ts = min(CONV_TILE, s_len)
    hb = ts // CONV_HALO
    n_t = s_len // ts
    te = ts + CONV_HALO

    def body(x_ref, xp_ref, xn_ref, ba_ref, dq_ref, dqn_ref, dk_ref, dkn_ref, dv_ref, dvn_ref, dgcb_ref, dz_ref,
             w_ref, al_ref, dt_ref, dp_ref, dw_ref, dal_ref, ddt_ref):
        i = pl.program_id(0)
        more = i < n_t - 1
        w = [w_ref[pl.ds(j, 1), :] for j in range(CONV_WIDTH)]
        x_t = x_ref[...]
        xe = jnp.concatenate([jnp.where(i > 0, xp_ref[...], 0.0), x_t, xn_ref[...]], axis=0)
        y_e, act_vjp = jax.vjp(_act_qkv, _conv_rows(xe, w, te))
        ct = tuple(jnp.concatenate([t[...], jnp.where(more, n[...], 0.0)], axis=0)
                   for t, n in ((dq_ref, dqn_ref), (dk_ref, dkn_ref), (dv_ref, dvn_ref)))
        (dy_e,) = act_vjp(ct)
        dx = dy_e[:ts] * w[CONV_WIDTH - 1]
        for j in range(CONV_WIDTH - 1):
            dx = dx + pltpu.roll(dy_e, te - (CONV_WIDTH - 1 - j), 0)[:ts] * w[j]
        dy_t = dy_e[:ts]
        xe_t = xe[:te]
        dws = [jnp.sum(dy_t * pltpu.roll(xe_t, CONV_WIDTH - 1 - j, 0)[CONV_HALO:], axis=0, keepdims=True)
               for j in range(CONV_WIDTH - 1)]
        dws.append(jnp.sum(dy_t * x_t, axis=0, keepdims=True))
        _, gb_vjp = jax.vjp(lambda ba, al, dt: _act_gb(ba, al, dt, _chunk_tril(ts)), ba_ref[...], al_ref[...], dt_ref[...])
        dba, dal, ddt = gb_vjp(dgcb_ref[...])
        dp_ref[...] = jnp.concatenate([dx.astype(BF16), dz_ref[...], dba.astype(BF16)], axis=1)
        first = i == 0
        for j in range(CONV_WIDTH):
            _acc(dw_ref.at[pl.ds(j, 1), :], first, dws[j])
        _acc(dal_ref, first, dal)
        _acc(ddt_ref, first, ddt)

    tile = pl.BlockSpec((ts, D_MODEL), lambda i: (i, 0))
    nxt = pl.BlockSpec((CONV_HALO, D_MODEL), lambda i: (jnp.minimum((i + 1) * hb, n_t * hb - 1), 0))
    return _call(
        body, "gdn_conv_bwd", (n_t,),
        [pl.BlockSpec((ts, QKV_DIM), lambda i: (i, 0)),
         pl.BlockSpec((CONV_HALO, QKV_DIM), lambda i: (jnp.maximum(i * hb - 1, 0), 0)),
         pl.BlockSpec((CONV_HALO, QKV_DIM), lambda i: (jnp.minimum((i + 1) * hb, n_t * hb - 1), 0)),
         pl.BlockSpec((ts, 128), lambda i: (i, BA_BLOCK)),
         tile, nxt, tile, nxt, tile, nxt, pl.BlockSpec((ts, 128), lambda i: (i, 0)), tile,
         _full((CONV_WIDTH, QKV_DIM)), _row(128), _row(128)],
        [pl.BlockSpec((ts, GDN_IN_PAD), lambda i: (i, 0)), _full((CONV_WIDTH, QKV_DIM)), _row(128), _row(128)],
        [_sds((s_len, GDN_IN_PAD), BF16), _sds((CONV_WIDTH, QKV_DIM)), _sds((1, 128)), _sds((1, 128))],
        sem=("arbitrary",))(proj, proj, proj, proj, dq, dq, dk, dk, dv, dv, dgcb, dz, conv_w, alog_l, dtb_l)


def _tri_inv(a_strict):
    ii = lax.broadcasted_iota(jnp.int32, (CHUNK, CHUNK), 0)
    jj = lax.broadcasted_iota(jnp.int32, (CHUNK, CHUNK), 1)
    x = (ii == jj).astype(F32) - a_strict
    pw = _mdot(a_strict, a_strict, BNN)
    for step in range(5):
        x = x + _mdot(x, pw, BNN)
        if step < 4:
            pw = _mdot(pw, pw, BNN)
    return x


@jax.custom_vjp
def _solved(a_strict, rhs, t, sol):
    return sol


def _solved_fwd(a_strict, rhs, t, sol):
    return sol, (t, sol)


def _solved_bwd(res, dsol):
    t, sol = res
    drhs = _mdot(t, dsol, BTN)
    return -_mdot(drhs, sol, BNT), drhs, jnp.zeros_like(t), jnp.zeros_like(sol)


_solved.defvjp(_solved_fwd, _solved_bwd)


def _prep(q, k, v, gc, beta, solve):
    ii = lax.broadcasted_iota(jnp.int32, (CHUNK, CHUNK), 0)
    jj = lax.broadcasted_iota(jnp.int32, (CHUNK, CHUNK), 1)
    causal, strict = ii >= jj, ii > jj
    gc_row = jnp.sum((ii == jj).astype(F32) * gc, axis=1, keepdims=True)
    decay = jnp.where(causal, jnp.exp(jnp.where(causal, gc - gc_row, 0.0)), 0.0)
    kb = k * beta
    a = jnp.where(strict, _bdot(kb, k, BNT) * decay, 0.0)
    eg = jnp.exp(gc)
    sol = solve(a, jnp.concatenate([v * beta, kb * eg], axis=-1))
    qk = _bdot(q, k, BNT) * decay
    last = lax.broadcasted_iota(jnp.int32, (CHUNK, 1), 0) == CHUNK - 1
    g_last = jnp.sum(jnp.where(last, gc, 0.0), axis=1, keepdims=True)
    kd = k * jnp.exp(g_last - gc)
    gl = jnp.exp(g_last) + jnp.zeros((1, 1, HEAD_DIM), F32)
    return sol[..., :HEAD_DIM], sol[..., HEAD_DIM:], qk, q * eg, kd, gl


def _prep_specs(s_len):
    rows = min(PREP_CHUNKS, s_len // CHUNK) * CHUNK
    m = rows // CHUNK
    hd = pl.BlockSpec((rows, HEAD_DIM), lambda c, h: (c, h))
    gcb = pl.BlockSpec((rows, 128), lambda c, h: (c, 0))
    qk = pl.BlockSpec((None, rows, CHUNK), lambda c, h: (h, c, 0))
    gl = pl.BlockSpec((None, m, HEADS, HEAD_DIM), lambda c, h: (c, 0, 0, 0))
    return rows, m, hd, gcb, qk, gl


def _head_cols(gcb, h, m):
    lane = lax.broadcasted_iota(jnp.int32, gcb.shape, 1)
    pick = lambda at: jnp.sum(jnp.where(lane == at, gcb, 0.0), axis=1, keepdims=True).reshape(m, CHUNK, 1)
    return pick(h + HEADS), pick(h)


def _gdn_prep(q, k, v, gcb):
    s_len = q.shape[0]
    rows, m, hd, gcb_spec, qk_spec, gl_spec = _prep_specs(s_len)

    def body(q_ref, k_ref, v_ref, gcb_ref, u_ref, w_ref, qd_ref, kd_ref, qk_ref, gl_ref, t_ref):
        r3 = lambda ref, d: ref[...].reshape(m, CHUNK, d)
        gc, beta = _head_cols(gcb_ref[...], pl.program_id(1), m)

        def solve(a, rhs):
            t = _tri_inv(a)
            t_ref[...] = t.reshape(rows, CHUNK)
            return _mdot(t, rhs, BNN)

        u, w, qk, qd, kd, gl = _prep(r3(q_ref, HEAD_DIM), r3(k_ref, HEAD_DIM), r3(v_ref, HEAD_DIM), gc, beta, solve)
        u_ref[...] = u.reshape(rows, HEAD_DIM)
        w_ref[...] = w.reshape(rows, HEAD_DIM)
        qd_ref[...] = qd.reshape(rows, HEAD_DIM).astype(BF16)
        kd_ref[...] = kd.reshape(rows, HEAD_DIM).astype(BF16)
        qk_ref[...] = qk.reshape(rows, CHUNK).astype(BF16)
        gl_ref[:, pl.ds(pl.program_id(1), 1), :] = gl

    n_g = s_len // rows
    return _call(
        body, "gdn_prep", (n_g, HEADS), [hd, hd, hd, gcb_spec], [hd, hd, hd, hd, qk_spec, gl_spec, qk_spec],
        [_sds((s_len, D_MODEL))] * 2 + [_sds((s_len, D_MODEL), BF16)] * 2
        + [_sds((HEADS, s_len, CHUNK), BF16), _sds((n_g, m, HEADS, HEAD_DIM)), _sds((HEADS, s_len, CHUNK))],
        sem=("parallel", "arbitrary"))(q, k, v, gcb)


def _gdn_prep_bwd(q, k, v, gcb, t_inv, u, w, du, dw, dqd, dkd, dqk, dgl):
    s_len = q.shape[0]
    rows, m, hd, gcb_spec, qk_spec, gl_spec = _prep_specs(s_len)

    def body(q_ref, k_ref, v_ref, gcb_ref, t_ref, u_ref, w_ref, du_ref, dw_ref, dqd_ref, dkd_ref, dqk_ref, dgl_ref,
             dq_ref, dk_ref, dv_ref, dgcb_ref):
        h = pl.program_id(1)
        r3 = lambda ref, d: ref[...].reshape(m, CHUNK, d)
        gc, beta = _head_cols(gcb_ref[...], h, m)
        t = r3(t_ref, CHUNK)
        sol = jnp.concatenate([r3(u_ref, HEAD_DIM), r3(w_ref, HEAD_DIM)], axis=-1)
        fn = lambda q_, k_, v_, gc_, bt_: _prep(q_, k_, v_, gc_, bt_, lambda a, rhs: _solved(a, rhs, t, sol))
        _, vjp = jax.vjp(fn, r3(q_ref, HEAD_DIM), r3(k_ref, HEAD_DIM), r3(v_ref, HEAD_DIM), gc, beta)
        ct = (r3(du_ref, HEAD_DIM), r3(dw_ref, HEAD_DIM), r3(dqk_ref, CHUNK), r3(dqd_ref, HEAD_DIM), r3(dkd_ref, HEAD_DIM),
              dgl_ref[:, pl.ds(h, 1), :] * (1.0 / HEAD_DIM))
        dq, dk, dv, dgc, dbt = vjp(ct)
        dq_ref[...] = dq.reshape(rows, HEAD_DIM)
        dk_ref[...] = dk.reshape(rows, HEAD_DIM)
        dv_ref[...] = dv.reshape(rows, HEAD_DIM)
        lane = lax.broadcasted_iota(jnp.int32, (rows, 128), 1)
        mine = jnp.where(lane == h, dbt.reshape(rows, 1), 0.0) + jnp.where(lane == h + HEADS, dgc.reshape(rows, 1), 0.0)
        _acc(dgcb_ref, h == 0, mine)

    return _call(
        body, "gdn_prep_bwd", (s_len // rows, HEADS),
        [hd, hd, hd, gcb_spec, qk_spec, hd, hd, hd, hd, hd, hd, qk_spec, gl_spec], [hd, hd, hd, gcb_spec],
        [_sds((s_len, D_MODEL))] * 3 + [_sds((s_len, 128))],
        sem=("parallel", "arbitrary"))(q, k, v, gcb, t_inv, u, w, du, dw, dqd, dkd, dqk, dgl)


def _scan_specs(n_c, m, reverse):
    at = (lambda n: n_c - 1 - n) if reverse else (lambda n: n)
    row = pl.BlockSpec((CHUNK, D_MODEL), lambda n: (at(n), 0))
    qk = pl.BlockSpec((HEADS, CHUNK, CHUNK), lambda n: (0, at(n), 0))
    gl = pl.BlockSpec((None, None, HEADS, HEAD_DIM), lambda n: (at(n) // m, at(n) % m, 0, 0))
    st = pl.BlockSpec((1, HEADS, HEAD_DIM, HEAD_DIM), lambda n: (at(n), 0, 0, 0))
    return row, qk, gl, st


def _gdn_scan(u, w, qd, kd, qk, gl):
    s_len = u.shape[0]
    n_c = s_len // CHUNK
    row, qk_spec, gl_spec, st_spec = _scan_specs(n_c, gl.shape[1], False)

    def body(u_ref, w_ref, qd_ref, kd_ref, qk_ref, gl_ref, o_ref, st_ref, state):
        @pl.when(pl.program_id(0) == 0)
        def _():
            state[...] = jnp.zeros_like(state)

        s_all = [state[h] for h in range(HEADS)]
        gl_all = [gl_ref[pl.ds(h, 1), :] for h in range(HEADS)]
        outs, nxt = [], []
        for h in range(HEADS):
            sl = slice(h * HEAD_DIM, (h + 1) * HEAD_DIM)
            s_b = s_all[h].astype(BF16)
            vn = (u_ref[:, sl] - _bdot(w_ref[:, sl], s_b, NN)).astype(BF16)
            outs.append(_bdot(qd_ref[:, sl], s_b, NN) + _bdot(qk_ref[h], vn, NN))
            nxt.append(s_all[h] * gl_all[h] + _bdot(kd_ref[:, sl], vn, TN))
        for h in range(HEADS):
            st_ref[0, h] = s_all[h]
            o_ref[:, h * HEAD_DIM:(h + 1) * HEAD_DIM] = outs[h]
            state[h] = nxt[h]

    return _call(
        body, "gdn_scan", (n_c,), [row, row, row, row, qk_spec, gl_spec], [row, st_spec],
        [_sds((s_len, D_MODEL)), _sds((n_c, HEADS, HEAD_DIM, HEAD_DIM))],
        scratch=[pltpu.VMEM((HEADS, HEAD_DIM, HEAD_DIM), F32)], sem=("arbitrary",))(u, w, qd, kd, qk, gl)


def _gdn_scan_bwd(do, u, w, qd, kd, qk, gl, states):
    s_len = u.shape[0]
    n_c = s_len // CHUNK
    row, qk_spec, gl_spec, st_spec = _scan_specs(n_c, gl.shape[1], True)

    def body(do_ref, u_ref, w_ref, qd_ref, kd_ref, qk_ref, gl_ref, st_ref,
             du_ref, dw_ref, dqd_ref, dkd_ref, dqk_ref, dgl_ref, dstate):
        @pl.when(pl.program_id(0) == 0)
        def _():
            dstate[...] = jnp.zeros_like(dstate)

        ds_all = [dstate[h] for h in range(HEADS)]
        gl_all = [gl_ref[pl.ds(h, 1), :] for h in range(HEADS)]
        res = []
        for h in range(HEADS):
            sl = slice(h * HEAD_DIM, (h + 1) * HEAD_DIM)
            s_f = st_ref[0, h]
            s_b = s_f.astype(BF16)
            ds_f = ds_all[h]
            ds_b = ds_f.astype(BF16)
            do_b = do_ref[:, sl].astype(BF16)
            w_b, qd_b, kd_b, qk_b = w_ref[:, sl].astype(BF16), qd_ref[:, sl], kd_ref[:, sl], qk_ref[h]
            vn = (u_ref[:, sl] - _bdot(w_b, s_b, NN)).astype(BF16)
            dvn = _bdot(qk_b, do_b, TN) + _bdot(kd_b, ds_b, NN)
            dvn_b = dvn.astype(BF16)
            res.append((dvn, -_bdot(dvn_b, s_b, NT), _bdot(do_b, s_b, NT), _bdot(vn, ds_b, NT), _bdot(do_b, vn, NT),
                        jnp.sum(s_f * ds_f) + jnp.zeros((1, HEAD_DIM), F32),
                        ds_f * gl_all[h] + _bdot(qd_b, do_b, TN) - _bdot(w_b, dvn_b, TN)))
        for h in range(HEADS):
            sl = slice(h * HEAD_DIM, (h + 1) * HEAD_DIM)
            du_ref[:, sl], dw_ref[:, sl], dqd_ref[:, sl], dkd_ref[:, sl] = res[h][:4]
            dqk_ref[h] = res[h][4]
            dgl_ref[pl.ds(h, 1), :] = res[h][5]
            dstate[h] = res[h][6]

    return _call(
        body, "gdn_scan_bwd", (n_c,), [row, row, row, row, row, qk_spec, gl_spec, st_spec],
        [row, row, row, row, qk_spec, gl_spec],
        [_sds((s_len, D_MODEL))] * 4 + [_sds((HEADS, s_len, CHUNK)), _sds(gl.shape)],
        scratch=[pltpu.VMEM((HEADS, HEAD_DIM, HEAD_DIM), F32)], sem=("arbitrary",))(do, u, w, qd, kd, qk, gl, states)


def _gate_norm(o, z, nw):
    outs = []
    for h in range(HEADS):
        oh = o[:, h * HEAD_DIM:(h + 1) * HEAD_DIM]
        outs.append(oh * lax.rsqrt(jnp.mean(oh * oh, axis=-1, keepdims=True) + RMS_EPS))
    return jnp.concatenate(outs, axis=1) * nw * _silu(z)


def _gdn_out_fwd(o, proj, x, w_out, nw, gain, bias):
    s_len = x.shape[0]
    ts = min(ROW_TILE, s_len)

    def body(o_ref, z_ref, x_ref, w_ref, nw_ref, g_ref, b_ref, y_ref, xhat_ref, rstd_ref):
        on = _gate_norm(o_ref[...], z_ref[...], nw_ref[...])
        r = ALPHA * x_ref[...] + _bdot(on, w_ref[...], NN)
        y_ref[...], xhat_ref[...], rstd_ref[...] = _ln_fwd(r, g_ref[...], b_ref[...])

    tile = pl.BlockSpec((ts, D_MODEL), lambda i: (i, 0))
    row = _row(D_MODEL)
    return _call(
        body, "gdn_out_fwd", (s_len // ts,),
        [tile, pl.BlockSpec((ts, D_MODEL), lambda i: (i, QKV_DIM // D_MODEL)), tile, _full((D_MODEL, D_MODEL)), row, row, row],
        [tile, tile, pl.BlockSpec((ts, 1), lambda i: (i, 0))],
        [_sds((s_len, D_MODEL)), _sds((s_len, D_MODEL)), _sds((s_len, 1))], sem=("parallel",))(o, proj, x, w_out, nw, gain, bias)


def _gdn_out_bwd(dy, xhat, rstd, o, proj, w_out, nw, gain):
    s_len = o.shape[0]
    ts = min(ROW_TILE, s_len)

    def body(dy_ref, xh_ref, rs_ref, o_ref, z_ref, w_ref, nw_ref, g_ref,
             dres_ref, do_ref, dz_ref, on_ref, drb_ref, dg_ref, db_ref, dnw_ref):
        dy_t, xh_t = dy_ref[...], xh_ref[...]
        dr = _ln_bwd(dy_t, xh_t, rs_ref[...], g_ref[...])
        dres_ref[...] = ALPHA * dr
        drb_ref[...] = dr.astype(BF16)
        on, vjp = jax.vjp(_gate_norm, o_ref[...], z_ref[...], nw_ref[...])
        on_ref[...] = on.astype(BF16)
        do, dz, dnw = vjp(_bdot(dr, w_ref[...], NT))
        do_ref[...] = do
        dz_ref[...] = dz.astype(BF16)
        first = pl.program_id(0) == 0
        _acc(dg_ref, first, jnp.sum(dy_t * xh_t, axis=0, keepdims=True))
        _acc(db_ref, first, jnp.sum(dy_t, axis=0, keepdims=True))
        _acc(dnw_ref, first, sum(dnw[:, h * HEAD_DIM:(h + 1) * HEAD_DIM] for h in range(HEADS)))

    tile = pl.BlockSpec((ts, D_MODEL), lambda i: (i, 0))
    row = _row(D_MODEL)
    return _call(
        body, "gdn_out_bwd", (s_len // ts,),
        [tile, tile, pl.BlockSpec((ts, 1), lambda i: (i, 0)), tile,
         pl.BlockSpec((ts, D_MODEL), lambda i: (i, QKV_DIM // D_MODEL)), _full((D_MODEL, D_MODEL)), row, row],
        [tile, tile, tile, tile, tile, row, row, _row(HEAD_DIM)],
        [_sds((s_len, D_MODEL))] * 2 + [_sds((s_len, D_MODEL), BF16)] * 3 + [_sds((1, D_MODEL))] * 2 + [_sds((1, HEAD_DIM))],
        sem=("arbitrary",))(dy, xhat, rstd, o, proj, w_out, nw, gain)


def _loss_head(y, target):
    s_len = y.shape[0]
    ts = min(ROW_TILE, s_len)

    def body(y_ref, t_ref, dy_ref, l_ref):
        err = y_ref[...] - t_ref[...]
        dy_ref[...] = err * (1.0 / D_MODEL)
        part = 0.5 * jnp.sum(jnp.mean(err * err, axis=-1, keepdims=True))
        _acc(l_ref, pl.program_id(0) == 0, part + jnp.zeros((1, 128), F32))

    tile = pl.BlockSpec((ts, D_MODEL), lambda i: (i, 0))
    return _call(body, "loss_head", (s_len // ts,), [tile, tile], [tile, _row(128)],
                 [_sds((s_len, D_MODEL)), _sds((1, 128))], sem=("arbitrary",))(y, target)


def _adamw(w, g, m, v, name):
    r, c = w.shape
    tr = min(ADAM_ROWS, r)

    def body(w_ref, g_ref, m_ref, v_ref, d_ref, nm_ref, nv_ref):
        g_t = g_ref[...]
        nm = ADAM_B1 * m_ref[...] + (1.0 - ADAM_B1) * g_t
        nv = ADAM_B2 * v_ref[...] + (1.0 - ADAM_B2) * (g_t * g_t)
        m_hat = nm / (1.0 - ADAM_B1 ** ADAM_STEP)
        v_hat = nv / (1.0 - ADAM_B2 ** ADAM_STEP)
        d_ref[...] = -ADAM_LR * (m_hat / (jnp.sqrt(v_hat) + ADAM_EPS) + ADAM_WD * w_ref[...])
        nm_ref[...] = nm
        nv_ref[...] = nv

    tile = pl.BlockSpec((tr, c), lambda i: (i, 0))
    return _call(body, name, (r // tr,), [tile] * 4, [tile] * 3, [_sds((r, c))] * 3, sem=("parallel",))(w, g, m, v)


def _assemble_w_in(shards):
    rows = 256
    width = GDN_IN_DIM // N_SHARD

    def body(s_ref, o_ref):
        pad = jnp.zeros((rows, GDN_IN_PAD - GDN_IN_DIM), shards.dtype)
        o_ref[...] = jnp.concatenate([s_ref[j] for j in range(N_SHARD)] + [pad], axis=1)

    return _call(body, "w_in_assemble", (D_MODEL // rows,), [pl.BlockSpec((N_SHARD, rows, width), lambda i: (0, i, 0))],
                 pl.BlockSpec((rows, GDN_IN_PAD), lambda i: (i, 0)), _sds((D_MODEL, GDN_IN_PAD), shards.dtype),
                 sem=("parallel",))(shards)


def _split_w_in(full):
    rows = 256
    width = GDN_IN_DIM // N_SHARD

    def body(f_ref, o_ref):
        f = f_ref[...]
        for j in range(N_SHARD):
            o_ref[j] = f[:, j * width:(j + 1) * width]

    return _call(body, "w_in_split", (D_MODEL // rows,), [pl.BlockSpec((rows, GDN_IN_PAD), lambda i: (i, 0))],
                 pl.BlockSpec((N_SHARD, rows, width), lambda i: (0, i, 0)), _sds((N_SHARD, D_MODEL, width), full.dtype),
                 sem=("parallel",))(full)


def _place():
    x, y, c = lax.axis_index("x"), lax.axis_index("y"), lax.axis_index("c")
    return x, y, c, [(1 - x, y), (x, 1 - y), (1 - x, 1 - y)]


def _row_tile(rows):
    return max(t for t in range(8, min(rows, 640) + 1, 8) if rows % t == 0)


def _place_shard(part, me, dtype, name, layer=0):
    _, _, r, c = part.shape
    tr = _row_tile(r)

    def body(me_ref, p_ref, o_ref):
        o_ref[...] = p_ref[...].astype(dtype)

    return pl.pallas_call(
        body, name=name, out_shape=_sds((N_SHARD, 2, r, c), dtype),
        grid_spec=pltpu.PrefetchScalarGridSpec(
            num_scalar_prefetch=1, grid=(2, r // tr),
            in_specs=[pl.BlockSpec((None, None, tr, c), lambda h, i, me_ref: (layer, h, i, 0))],
            out_specs=pl.BlockSpec((None, None, tr, c), lambda h, i, me_ref: (me_ref[0], h, i, 0))))(me, part)


def _gather_sems(n):
    return [pltpu.SemaphoreType.DMA((6 * n,)), pltpu.SemaphoreType.DMA((6 * n,))]


def _gather_steps(dsts, send_sems, recv_sems):
    n = len(dsts)
    x, y, c, chips = _place()
    me = 2 * x + y
    sibling = (x, y, 1 - c)

    def ici(k, j, slot):
        px, py = chips[j]
        view = dsts[k].at[slot, c]
        return pltpu.make_async_remote_copy(
            src_ref=view, dst_ref=view, send_sem=send_sems.at[6 * k + j],
            recv_sem=recv_sems.at[6 * k + j], device_id=(px, py, c), device_id_type=MESH)

    def d2d(k, j, half):
        px, py = chips[j]
        view = dsts[k].at[2 * px + py, half]
        return pltpu.make_async_remote_copy(
            src_ref=view, dst_ref=view, send_sem=send_sems.at[6 * k + 3 + j], recv_sem=recv_sems.at[6 * k + 3 + j],
            device_id=sibling, device_id_type=MESH)

    def start():
        for k in range(n):
            for j in range(3):
                ici(k, j, me).start()

    def finish():
        fwds = []
        for k in range(n):
            for j, (px, py) in enumerate(chips):
                ici(k, j, 2 * px + py).wait_recv()
                fwds.append(d2d(k, j, c))
                fwds[-1].start()
        for k in range(n):
            for j in range(3):
                d2d(k, j, 1 - c).wait_recv()
        for k in range(n):
            for j in range(3):
                ici(k, j, me).wait_send()
        for cp in fwds:
            cp.wait_send()

    return start, finish


def _all_gather(bufs, name):
    n = len(bufs)

    def body(*refs):
        start, finish = _gather_steps(refs[n:2 * n], *refs[2 * n:])
        start()
        finish()

    return pl.pallas_call(
        body, name=name, out_shape=[_sds(a.shape, a.dtype) for a in bufs],
        in_specs=[ANY] * n, out_specs=[ANY] * n, input_output_aliases={k: k for k in range(n)},
        scratch_shapes=_gather_sems(n))(*bufs)


def _swap_halves(pieces, name):
    n = len(pieces)

    def body(*refs):
        srcs, dsts = refs[:n], refs[n:2 * n]
        send_sems, recv_sems = refs[2 * n:]
        x, y, c, _ = _place()
        copies = []
        for k in range(n):
            hr = pieces[k].shape[1] // 2
            copies.append(pltpu.make_async_remote_copy(
                src_ref=srcs[k].at[:, pl.ds((1 - c) * hr, hr), :], dst_ref=dsts[k],
                send_sem=send_sems.at[k], recv_sem=recv_sems.at[k], device_id=(x, y, 1 - c), device_id_type=MESH))
        for cp in copies:
            cp.start()
        for cp in copies:
            cp.wait()

    return pl.pallas_call(
        body, name=name, out_shape=[_sds((N_SHARD, a.shape[1] // 2, a.shape[2])) for a in pieces],
        in_specs=[ANY] * n, out_specs=[ANY] * n,
        scratch_shapes=[pltpu.SemaphoreType.DMA((n,)), pltpu.SemaphoreType.DMA((n,))])(*pieces)


def _add_half(piece, other, place, dtype, name):
    n, hr, cols = other.shape
    tr = _row_tile(hr)

    def body(pl_ref, a_ref, b_ref, o_ref):
        o_ref[...] = (a_ref[...] + b_ref[...]).astype(dtype)

    tile = pl.BlockSpec((None, tr, cols), lambda s, i, pl_ref: (s, i, 0))
    return pl.pallas_call(
        body, name=name, out_shape=_sds(other.shape, dtype),
        grid_spec=pltpu.PrefetchScalarGridSpec(
            num_scalar_prefetch=1, grid=(n, hr // tr),
            in_specs=[pl.BlockSpec((None, None, tr, cols), lambda s, i, pl_ref: (s, pl_ref[1], i, 0)), tile],
            out_specs=tile))(place, piece.reshape(n, 2, hr, cols), other)


def _scatter_sems(n):
    return [pltpu.SemaphoreType.DMA((3 * n,)), pltpu.SemaphoreType.DMA((3 * n,))]


def _scatter_steps(srcs, dsts, send_sems, recv_sems):
    n = len(srcs)
    x, y, c, chips = _place()
    me = 2 * x + y

    def ici(k, j, src_slot, dst_slot):
        px, py = chips[j]
        return pltpu.make_async_remote_copy(
            src_ref=srcs[k].at[src_slot], dst_ref=dsts[k].at[dst_slot], send_sem=send_sems.at[3 * k + j],
            recv_sem=recv_sems.at[3 * k + j], device_id=(px, py, c), device_id_type=MESH)

    def start():
        for k in range(n):
            for j, (px, py) in enumerate(chips):
                ici(k, j, 2 * px + py, me).start()

    def finish():
        for k in range(n):
            for j, (px, py) in enumerate(chips):
                ici(k, j, me, 2 * px + py).wait_recv()
        for k in range(n):
            for j, (px, py) in enumerate(chips):
                ici(k, j, 2 * px + py, me).wait_send()

    return start, finish


def _scatter_chips(parts, name):
    n = len(parts)

    def body(*refs):
        start, finish = _scatter_steps(refs[:n], refs[n:2 * n], *refs[2 * n:])
        start()
        finish()

    return pl.pallas_call(
        body, name=name, out_shape=[_sds(a.shape, a.dtype) for a in parts],
        in_specs=[ANY] * n, out_specs=[ANY] * n, scratch_shapes=_scatter_sems(n))(*parts)


def _sum_chips(landed, own, place, name):
    _, r, cols = landed.shape
    tr = _row_tile(r)

    def body(pl_ref, q_ref, p_ref, o_ref):
        me = pl_ref[0]
        f = lambda j: jnp.where(me == j, p_ref[...], q_ref[j]).astype(F32)
        o_ref[...] = ((f(0) + f(1)) + f(2)) + f(3)

    return pl.pallas_call(
        body, name=name, out_shape=_sds((2, r, cols)),
        grid_spec=pltpu.PrefetchScalarGridSpec(
            num_scalar_prefetch=1, grid=(r // tr,),
            in_specs=[pl.BlockSpec((N_SHARD, tr, cols), lambda i, pl_ref: (0, i, 0)),
                      pl.BlockSpec((None, tr, cols), lambda i, pl_ref: (pl_ref[0], i, 0))],
            out_specs=pl.BlockSpec((None, tr, cols), lambda i, pl_ref: (pl_ref[1], i, 0))))(place, landed, own)


def _join_halves(bufs):
    n = len(bufs)

    def body(*refs):
        dsts = refs[n:2 * n]
        send_sems, recv_sems = refs[2 * n:]
        x, y, c, _ = _place()
        copies = [pltpu.make_async_remote_copy(
            src_ref=dsts[k].at[c], dst_ref=dsts[k].at[c], send_sem=send_sems.at[k], recv_sem=recv_sems.at[k],
            device_id=(x, y, 1 - c), device_id_type=MESH) for k in range(n)]
        for cp in copies:
            cp.start()
        for cp in copies:
            cp.wait()

    return pl.pallas_call(
        body, name="grads_join_halves", out_shape=[_sds(a.shape) for a in bufs], in_specs=[ANY] * n, out_specs=[ANY] * n,
        input_output_aliases={k: k for k in range(n)},
        scratch_shapes=[pltpu.SemaphoreType.DMA((n,)), pltpu.SemaphoreType.DMA((n,))])(*bufs)


GATHER_F32 = ("ln_gain", "ln_bias", "pool_b", "gdn_conv")
REPLICATED = ("pool_scale", "gdn_a_log", "gdn_dt_bias", "gdn_norm_w", "ple_gate_b")
WEIGHTS = ("ln_gain", "ln_bias", "pool_w", "pool_b", "pool_scale", "gdn_w_in", "gdn_conv", "gdn_a_log", "gdn_dt_bias",
           "gdn_norm_w", "gdn_w_out", "mlp_w1", "mlp_w2", "ple_gate_w", "ple_gate_b", "ple_proj")
SMALL_GRADS = ("ple_proj", "pool_w", "ln_gain", "ln_bias", "pool_b", "gdn_conv") + REPLICATED


def _pack(parts, lanes, row_multiple):
    flat = jnp.concatenate([a.reshape(-1) for a in parts])
    rows = -(-flat.shape[0] // (2 * lanes * row_multiple)) * row_multiple
    return jnp.pad(flat, (0, 2 * rows * lanes - flat.shape[0])).reshape(2, rows, lanes)


def _unpack(flat, shapes):
    out, off = [], 0
    for shp in shapes:
        n = math.prod(shp)
        out.append(flat[..., off:off + n].reshape(flat.shape[:-1] + tuple(shp)))
        off += n
    return out


def _pad_lanes(a, offset, width=128):
    return jnp.pad(a, ((0, 0), (offset, width - offset - a.shape[1])))


def kernel(x, p, ln_gain, ln_bias, pool_w, pool_b, pool_scale, gdn_w_in, gdn_conv, gdn_a_log, gdn_dt_bias, gdn_norm_w, gdn_w_out, mlp_w1, mlp_w2, ple_gate_w, ple_gate_b, ple_proj, loss_target, m_ln_gain, m_ln_bias, m_pool_w, m_pool_b, m_pool_scale, m_gdn_w_in, m_gdn_conv, m_gdn_a_log, m_gdn_dt_bias, m_gdn_norm_w, m_gdn_w_out, m_mlp_w1, m_mlp_w2, m_ple_gate_w, m_ple_gate_b, m_ple_proj, v_ln_gain, v_ln_bias, v_pool_w, v_pool_b, v_pool_scale, v_gdn_w_in, v_gdn_conv, v_gdn_a_log, v_gdn_dt_bias, v_gdn_norm_w, v_gdn_w_out, v_mlp_w1, v_mlp_w2, v_ple_gate_w, v_ple_gate_b, v_ple_proj):
    shard = dict(ln_gain=ln_gain, ln_bias=ln_bias, pool_w=pool_w, pool_b=pool_b, pool_scale=pool_scale, gdn_w_in=gdn_w_in,
                 gdn_conv=gdn_conv, gdn_a_log=gdn_a_log, gdn_dt_bias=gdn_dt_bias, gdn_norm_w=gdn_norm_w, gdn_w_out=gdn_w_out,
                 mlp_w1=mlp_w1, mlp_w2=mlp_w2, ple_gate_w=ple_gate_w, ple_gate_b=ple_gate_b, ple_proj=ple_proj)
    mom = dict(ln_gain=m_ln_gain, ln_bias=m_ln_bias, pool_w=m_pool_w, pool_b=m_pool_b, pool_scale=m_pool_scale,
               gdn_w_in=m_gdn_w_in, gdn_conv=m_gdn_conv, gdn_a_log=m_gdn_a_log, gdn_dt_bias=m_gdn_dt_bias,
               gdn_norm_w=m_gdn_norm_w, gdn_w_out=m_gdn_w_out, mlp_w1=m_mlp_w1, mlp_w2=m_mlp_w2, ple_gate_w=m_ple_gate_w,
               ple_gate_b=m_ple_gate_b, ple_proj=m_ple_proj)
    var = dict(ln_gain=v_ln_gain, ln_bias=v_ln_bias, pool_w=v_pool_w, pool_b=v_pool_b, pool_scale=v_pool_scale,
               gdn_w_in=v_gdn_w_in, gdn_conv=v_gdn_conv, gdn_a_log=v_gdn_a_log, gdn_dt_bias=v_gdn_dt_bias,
               gdn_norm_w=v_gdn_norm_w, gdn_w_out=v_gdn_w_out, mlp_w1=v_mlp_w1, mlp_w2=v_mlp_w2, ple_gate_w=v_ple_gate_w,
               ple_gate_b=v_ple_gate_b, ple_proj=v_ple_proj)

    xi, yi, ci = lax.axis_index("x"), lax.axis_index("y"), lax.axis_index("c")
    me = (2 * xi + yi).reshape(1).astype(jnp.int32)
    place = jnp.stack([2 * xi + yi, ci]).astype(jnp.int32)
    early = [("mlp_w1", 0), ("mlp_w2", 0), ("ple_gate_w", 0), ("ple_proj", 0), ("pool_w", 0)]
    late = [("mlp_w1", 1), ("mlp_w2", 1), ("ple_gate_w", 1), ("ple_proj", 1), ("gdn_w_out", 0), ("gdn_w_in", 0)]
    halved = lambda n: shard[n].reshape(shard[n].shape[0], 2, -1, shard[n].shape[-1])
    placed = lambda ops, tag: [_place_shard(halved(n), me, BF16, f"place_{tag}_{n}", l) for n, l in ops]
    small_in = _place_shard(_pack([shard[n] for n in GATHER_F32], 128, 8)[None], me, F32, "place_small")
    got_early = _all_gather(placed(early, "early") + [small_in], "weights_all_gather_early")
    placed_late = placed(late, "late")
    st = dict(zip(GATHER_F32, _unpack(got_early[-1].reshape(N_SHARD, -1), [shard[n].shape for n in GATHER_F32])))

    cat_last = lambda a: jnp.moveaxis(a, 0, -2).reshape(a.shape[1:-1] + (N_SHARD * a.shape[-1],))
    gain = cat_last(st["ln_gain"])
    bias = cat_last(st["ln_bias"])
    wp = got_early[4].reshape(N_SHARD, 4, POOL_GROUP // N_SHARD, POOL_GROUP)
    pb = cat_last(st["pool_b"]).reshape(1, D_MODEL)
    ps = pool_scale
    conv_w = cat_last(st["gdn_conv"])[0]
    merged = lambda g: g.reshape(N_SHARD, -1, g.shape[-1])
    mlp_w = lambda i, got: (merged(got[0]), merged(got[1]), merged(got[2]), ple_gate_b[i:i + 1], merged(got[3]))
    alog_l = _pad_lanes(gdn_a_log, HEADS)
    dtb_l = _pad_lanes(gdn_dt_bias, HEADS)
    nw = jnp.tile(gdn_norm_w, (1, HEADS))
    ln = lambda i, k: (gain[i, k][None], bias[i, k][None])

    x0 = x[0]
    p0, p1 = p[0, 0], p[1, 0]

    x1, xh1, rs1 = _pool_fwd(x0, wp, pb, ps, *ln(0, 0))
    (x2, xh2, rs2), got_late = _mlp_fwd(x1, p0, *mlp_w(0, got_early), *ln(0, 1), "mlp_fwd_0", gather=placed_late)
    w_out = got_late[4].reshape(D_MODEL, D_MODEL)
    w_in = _assemble_w_in(merged(got_late[5]))
    proj = _matmul_nn(x2, w_in, "gdn_in_proj", GDN_IN_PAD // 3)
    q, k, v, gcb = _conv_fwd(proj, conv_w, alog_l, dtb_l)
    u, w, qd, kd, qk, gl, t_inv = _gdn_prep(q, k, v, gcb)
    o, states = _gdn_scan(u, w, qd, kd, qk, gl)
    x3, xh3, rs3 = _gdn_out_fwd(o, proj, x2, w_out, nw, *ln(1, 0))
    (x4, xh4, rs4), _ = _mlp_fwd(x3, p1, *mlp_w(1, got_late), *ln(1, 1), "mlp_fwd_1")
    dy4, loss_l = _loss_head(x4, loss_target[0])

    g_gain = [[None, None], [None, None]]
    g_bias = [[None, None], [None, None]]

    def mlp_grads(i, dy, xh, rs, x_mid, p_i, got, scatter=()):
        (dx, a, dh, dzg, dpp, drb, dg, db, dgb), landed = _mlp_bwd(
            dy, xh, rs, x_mid, p_i, *mlp_w(i, got), ln(i, 1)[0], f"mlp_bwd_{i}", scatter=scatter)
        g_gain[i][1], g_bias[i][1] = dg, db
        return dx, dict(
            mlp_w1=_wgrad(x_mid, dh, f"dw1_{i}", stack_cols=True), mlp_w2=_wgrad(a, drb, f"dw2_{i}").reshape(N_SHARD, -1, D_MODEL),
            ple_gate_w=_wgrad(x_mid, dzg, f"dgate_w_{i}").reshape(N_SHARD, -1, D_MODEL),
            ple_proj=_wgrad(p_i, dpp, f"dproj_{i}", stack_cols=True), ple_gate_b=dgb), landed

    def chip_sums(pieces, wire, tag):
        others = _swap_halves(pieces, f"grads_swap_halves_{tag}")
        return [_add_half(a, b, place, t, f"grads_add_half_{tag}{i}") for i, (a, b, t) in enumerate(zip(pieces, others, wire))]

    dx3, gl1, _ = mlp_grads(1, dy4, xh4, rs4, x3, p1, got_late)
    dres, do, dz, on_b, drb3, g_gain[1][0], g_bias[1][0], d_nw = _gdn_out_bwd(dx3, xh3, rs3, o, proj, w_out, nw, ln(1, 0)[0])
    d_wout = _wgrad(on_b, drb3, "dw_out").reshape(N_SHARD, -1, D_MODEL)
    du, dw, dqd, dkd, dqk, dgl = _gdn_scan_bwd(do, u, w, qd, kd, qk, gl, states)
    dq, dk, dv, dgcb = _gdn_prep_bwd(q, k, v, gcb, t_inv, u, w, du, dw, dqd, dkd, dqk, dgl)
    dproj, d_conv, d_alog_l, d_dtb_l = _conv_bwd(proj, dq, dk, dv, dgcb, dz, conv_w, alog_l, dtb_l)
    dx2 = _matmul_nt_add(dproj, w_in, dres, "gdn_in_bwd")
    d_win = _split_w_in(_wgrad(x2, dproj, "dw_in"))
    sums_late = chip_sums([gl1["mlp_w1"], gl1["mlp_w2"], gl1["ple_gate_w"], d_wout, d_win], [BF16] * 5, "late")
    dx1, gl0, landed_late = mlp_grads(0, dx2, xh2, rs2, x1, p0, got_early, scatter=sums_late)
    dx0, g_gain[0][0], g_bias[0][0], d_ps, d_pb, d_wp = _pool_bwd(dx1, xh1, rs1, x0, wp, pb, ps, ln(0, 0)[0])

    split_last = lambda a: jnp.moveaxis(a.reshape(a.shape[:-1] + (N_SHARD, a.shape[-1] // N_SHARD)), -2, 0)
    small_st = dict(
        ple_proj=jnp.stack([gl0["ple_proj"], gl1["ple_proj"]], axis=1),
        pool_w=jnp.moveaxis(d_wp.reshape(4, N_SHARD, POOL_GROUP // N_SHARD, POOL_GROUP), 1, 0)[:, None],
        ln_gain=split_last(jnp.stack([jnp.concatenate(r, axis=0) for r in g_gain])),
        ln_bias=split_last(jnp.stack([jnp.concatenate(r, axis=0) for r in g_bias])),
        pool_b=split_last(d_pb.reshape(1, 4, POOL_GROUP)),
        gdn_conv=split_last(d_conv)[:, None],
    )
    rep = dict(pool_scale=d_ps, gdn_a_log=d_alog_l[:, HEADS:2 * HEADS], gdn_dt_bias=d_dtb_l[:, HEADS:2 * HEADS],
               gdn_norm_w=d_nw, ple_gate_b=jnp.concatenate([gl0["ple_gate_b"], gl1["ple_gate_b"]], axis=0))
    for n in REPLICATED:
        small_st[n] = jnp.broadcast_to(rep[n][None], (N_SHARD,) + rep[n].shape)
    small_flat = jnp.concatenate([small_st[n].reshape(N_SHARD, -1) for n in SMALL_GRADS], axis=1)
    small_rows = -(-small_flat.shape[1] // (16 * LANES)) * 16
    small_piece = jnp.pad(small_flat, ((0, 0), (0, small_rows * LANES - small_flat.shape[1]))).reshape(N_SHARD, small_rows, LANES)

    sums_early = chip_sums([gl0["mlp_w1"], gl0["mlp_w2"], gl0["ple_gate_w"], small_piece], [BF16] * 3 + [F32], "early")
    landed_early = _scatter_chips(sums_early, "grads_scatter_chips_early")
    red = _join_halves([_sum_chips(q_, p_, place, f"grads_sum_chips_{i}") for i, (q_, p_) in
                        enumerate(zip(list(landed_early) + list(landed_late), sums_early + sums_late))])
    red = [r.reshape(-1, r.shape[-1]) for r in red]
    grads = dict(mlp_w1=jnp.stack([red[0], red[4]]), mlp_w2=jnp.stack([red[1], red[5]]), ple_gate_w=jnp.stack([red[2], red[6]]),
                 gdn_w_out=red[7][None], gdn_w_in=red[8][None])
    grads.update(zip(SMALL_GRADS, _unpack(red[3].reshape(-1), [shard[n].shape for n in SMALL_GRADS])))

    delta, new_m, new_v = {}, {}, {}
    small = [n for n in WEIGHTS if shard[n].size < 128 * 128]
    for n in WEIGHTS:
        if n in small:
            continue
        to2d = lambda a, n=n: a.reshape(-1, shard[n].shape[-1])
        d2, m2, v2 = _adamw(to2d(shard[n]), to2d(grads[n]), to2d(mom[n]), to2d(var[n]), "adamw_" + n)
        delta[n], new_m[n], new_v[n] = (t.reshape(shard[n].shape) for t in (d2, m2, v2))
    pk = lambda d: _pack([d[n] for n in small], 128, 8).reshape(-1, 128)
    d2, m2, v2 = _adamw(pk(shard), pk(grads), pk(mom), pk(var), "adamw_small")
    for dst, t in ((delta, d2), (new_m, m2), (new_v, v2)):
        dst.update(zip(small, _unpack(t.reshape(-1), [shard[n].shape for n in small])))

    loss = lax.psum(loss_l[0, 0], ("x", "y", "c"))
    return (loss, dx0[None], *[grads[n] for n in WEIGHTS], *[delta[n] for n in WEIGHTS],
            *[new_m[n] for n in WEIGHTS], *[new_v[n] for n in WEIGHTS])
```

```python
import math

import jax
import jax.numpy as jnp
from jax import lax
from jax.experimental import pallas as pl
from jax.experimental.pallas import tpu as pltpu

F32 = jnp.float32
BF16 = jnp.bfloat16

D_MODEL = 1024
D_FF = 4096
PLE_DIM = 256
N_SHARD = 4
POOL_WINDOWS = (2, 4, 8, 16)
POOL_GROUP = 256
POOL_HALO = 16
HEADS = 8
HEAD_DIM = 128
CHUNK = 64
CONV_WIDTH = 4
CONV_HALO = 8
QKV_DIM = 3 * D_MODEL
GDN_IN_DIM = QKV_DIM + D_MODEL + 2 * HEADS
GDN_IN_PAD = 4224
BA_BLOCK = (QKV_DIM + D_MODEL) // 128
ALPHA = (2.0 * 2) ** 0.25
LN_EPS = 1e-5
RMS_EPS = 1e-6
L2_EPS = 1e-6
ADAM_LR, ADAM_B1, ADAM_B2, ADAM_EPS, ADAM_WD, ADAM_STEP = 0.001, 0.9, 0.999, 1e-08, 0.01, 10

ROW_TILE = 512
CONV_TILE = 256
PREP_CHUNKS = 16
LANES = 1024
ADAM_ROWS = 256

NN = (((1,), (0,)), ((), ()))
NT = (((1,), (1,)), ((), ()))
TN = (((0,), (0,)), ((), ()))
BNN = (((2,), (1,)), ((0,), (0,)))
BNT = (((2,), (2,)), ((0,), (0,)))
BTN = (((1,), (1,)), ((0,), (0,)))
MESH = pl.DeviceIdType.MESH
ANY = pl.BlockSpec(memory_space=pl.ANY)


def _bdot(a, b, dims):
    return lax.dot_general(a.astype(BF16), b.astype(BF16), dims, preferred_element_type=F32)


def _hdot(a, b, dims):
    return lax.dot_general(a, b, dims, precision=lax.Precision.HIGHEST, preferred_element_type=F32)


def _mdot(a, b, dims):
    return lax.dot_general(a, b, dims, precision=lax.Precision.HIGH, preferred_element_type=F32)


def _sigmoid(x):
    return 0.5 * jnp.tanh(0.5 * x) + 0.5


def _silu(x):
    return x * _sigmoid(x)


def _softplus(x):
    return jnp.maximum(x, 0.0) + jnp.log1p(jnp.exp(-jnp.abs(x)))


def _call(body, name, grid, in_specs, out_specs, out_shape, scratch=(), sem=None, aliases=None):
    params = pltpu.CompilerParams(dimension_semantics=sem) if sem else None
    return pl.pallas_call(
        body, name=name, grid=grid, in_specs=in_specs, out_specs=out_specs, out_shape=out_shape,
        scratch_shapes=list(scratch), compiler_params=params, input_output_aliases=aliases or {})


def _row(d):
    return pl.BlockSpec((1, d), lambda *_: (0, 0))


def _full(shape):
    n = len(shape)
    return pl.BlockSpec(shape, lambda *_: (0,) * n)


def _sds(shape, dtype=F32):
    return jax.ShapeDtypeStruct(shape, dtype)


def _ln_fwd(r, gain, bias):
    mu = jnp.mean(r, axis=-1, keepdims=True)
    xc = r - mu
    rstd = lax.rsqrt(jnp.mean(xc * xc, axis=-1, keepdims=True) + LN_EPS)
    xhat = xc * rstd
    return xhat * gain + bias, xhat, rstd


def _ln_bwd(dy, xhat, rstd, gain):
    dxh = dy * gain
    m1 = jnp.mean(dxh, axis=-1, keepdims=True)
    m2 = jnp.mean(dxh * xhat, axis=-1, keepdims=True)
    return rstd * (dxh - m1 - xhat * m2)


def _acc(ref, first, val):
    @pl.when(first)
    def _():
        ref[...] = val

    @pl.when(jnp.logical_not(first))
    def _():
        ref[...] += val


def _pooled_groups(xe, t0, ts):
    pos = (t0 + lax.broadcasted_iota(jnp.int32, (ts, 1), 0) + 1).astype(F32)
    outs = []
    for gi, win in enumerate(POOL_WINDOWS):
        xs = xe[:, gi * POOL_GROUP:(gi + 1) * POOL_GROUP]
        s, k = xs, 1
        while k < win:
            s = s + pltpu.roll(s, k, 0)
            k *= 2
        mean = s[POOL_HALO:] / jnp.minimum(pos, float(win))
        outs.append(mean - xs[POOL_HALO:])
    return outs


def _pool_groups_w(w_ref):
    return [jnp.concatenate([w_ref[s, g] for s in range(N_SHARD)], axis=0) for g in range(4)]


def _pool_fwd(x, wp, pb, ps, gain, bias):
    s_len = x.shape[0]
    ts = min(ROW_TILE, s_len)
    hb = ts // POOL_HALO

    def body(x_ref, halo_ref, w_ref, pb_ref, ps_ref, g_ref, b_ref, y_ref, xhat_ref, rstd_ref):
        i = pl.program_id(0)
        x_t = x_ref[...]
        halo = jnp.where(i > 0, halo_ref[...], 0.0)
        pooled = _pooled_groups(jnp.concatenate([halo, x_t], axis=0), i * ts, ts)
        wg = _pool_groups_w(w_ref)
        y = jnp.concatenate([_bdot(pooled[g], wg[g], NN) for g in range(4)], axis=1) + pb_ref[...]
        r = ALPHA * x_t + y * ps_ref[...]
        y_ref[...], xhat_ref[...], rstd_ref[...] = _ln_fwd(r, g_ref[...], b_ref[...])

    tile = pl.BlockSpec((ts, D_MODEL), lambda i: (i, 0))
    return _call(
        body, "pool_fwd", (s_len // ts,),
        [tile, pl.BlockSpec((POOL_HALO, D_MODEL), lambda i: (jnp.maximum(i * hb - 1, 0), 0)),
         _full(wp.shape), _row(D_MODEL), _row(D_MODEL), _row(D_MODEL), _row(D_MODEL)],
        [tile, tile, pl.BlockSpec((ts, 1), lambda i: (i, 0))],
        [_sds((s_len, D_MODEL)), _sds((s_len, D_MODEL)), _sds((s_len, 1))],
        sem=("parallel",))(x, x, wp, pb, ps, gain, bias)


def _pool_bwd(dy, xhat, rstd, x, wp, pb, ps, gain):
    s_len = x.shape[0]
    ts = min(ROW_TILE, s_len)
    hb = ts // POOL_HALO
    n_t = s_len // ts
    ne = ts + POOL_HALO

    def body(dy_ref, dyn_ref, xh_ref, xhn_ref, rs_ref, rsn_ref, x_ref, xp_ref, w_ref, pb_ref, ps_ref, g_ref,
             dx_ref, dg_ref, db_ref, dps_ref, dpb_ref, dw_ref):
        i = pl.program_id(0)
        more = i < n_t - 1
        dy_t, xh_t = dy_ref[...], xh_ref[...]
        dy_e = jnp.concatenate([dy_t, jnp.where(more, dyn_ref[...], 0.0)], axis=0)
        xh_e = jnp.concatenate([xh_t, xhn_ref[...]], axis=0)
        rs_e = jnp.concatenate([rs_ref[...], rsn_ref[...]], axis=0)
        dr_e = _ln_bwd(dy_e, xh_e, rs_e, g_ref[...])
        dyy_e = dr_e * ps_ref[...]
        pos_e = (i * ts + lax.broadcasted_iota(jnp.int32, (ne, 1), 0) + 1).astype(F32)
        dxs = []
        wg = _pool_groups_w(w_ref)
        for gi, win in enumerate(POOL_WINDOWS):
            sl = slice(gi * POOL_GROUP, (gi + 1) * POOL_GROUP)
            dpool = _bdot(dyy_e[:, sl], wg[gi], NT)
            s, k = dpool / jnp.minimum(pos_e, float(win)), 1
            while k < win:
                s = s + pltpu.roll(s, ne - k, 0)
                k *= 2
            dxs.append(s[:ts] - dpool[:ts])
        dx_ref[...] = ALPHA * dr_e[:ts] + jnp.concatenate(dxs, axis=1)

        x_t = x_ref[...]
        halo = jnp.where(i > 0, xp_ref[...], 0.0)
        pooled = _pooled_groups(jnp.concatenate([halo, x_t], axis=0), i * ts, ts)
        y = jnp.concatenate([_bdot(pooled[g], wg[g], NN) for g in range(4)], axis=1) + pb_ref[...]
        dr_t, dyy_t = dr_e[:ts], dyy_e[:ts]
        first = i == 0
        _acc(dg_ref, first, jnp.sum(dy_t * xh_t, axis=0, keepdims=True))
        _acc(db_ref, first, jnp.sum(dy_t, axis=0, keepdims=True))
        _acc(dps_ref, first, jnp.sum(dr_t * y, axis=0, keepdims=True))
        _acc(dpb_ref, first, jnp.sum(dyy_t, axis=0, keepdims=True))
        for g in range(4):
            _acc(dw_ref.at[g], first, _bdot(pooled[g], dyy_t[:, g * POOL_GROUP:(g + 1) * POOL_GROUP], TN))

    tile = pl.BlockSpec((ts, D_MODEL), lambda i: (i, 0))
    nxt = pl.BlockSpec((POOL_HALO, D_MODEL), lambda i: (jnp.minimum((i + 1) * hb, n_t * hb - 1), 0))
    prv = pl.BlockSpec((POOL_HALO, D_MODEL), lambda i: (jnp.maximum(i * hb - 1, 0), 0))
    rs_t = pl.BlockSpec((ts, 1), lambda i: (i, 0))
    rs_n = pl.BlockSpec((POOL_HALO, 1), lambda i: (jnp.minimum((i + 1) * hb, n_t * hb - 1), 0))
    row = _row(D_MODEL)
    return _call(
        body, "pool_bwd", (n_t,),
        [tile, nxt, tile, nxt, rs_t, rs_n, tile, prv, _full(wp.shape), row, row, row],
        [tile, row, row, row, row, _full((4, POOL_GROUP, POOL_GROUP))],
        [_sds((s_len, D_MODEL))] + [_sds((1, D_MODEL))] * 4 + [_sds((4, POOL_GROUP, POOL_GROUP))],
        sem=("arbitrary",))(dy, dy, xhat, xhat, rstd, rstd, x, x, wp, pb, ps, gain)


def _mlp_weight_specs():
    fc = D_FF // N_SHARD
    return [pl.BlockSpec((None, D_MODEL, fc), lambda i, j: (j, 0, 0)),
            pl.BlockSpec((None, fc, D_MODEL), lambda i, j: (j, 0, 0)),
            _full((N_SHARD, D_MODEL // N_SHARD, D_MODEL)),
            _row(D_MODEL),
            _full((N_SHARD, PLE_DIM, D_MODEL // N_SHARD))]


def _gate_w(gw_ref):
    return gw_ref[...].reshape(D_MODEL, D_MODEL)


def _ple_proj(pj_ref):
    return jnp.concatenate([pj_ref[s] for s in range(N_SHARD)], axis=1)


def _mlp_fwd(x, p, w1s, w2s, gw, gb, proj, gain, bias, name, gather=()):
    s_len = x.shape[0]
    ts = min(ROW_TILE, s_len)
    n_i = s_len // ts
    n_g = len(gather)

    def body(*refs):
        x_ref, p_ref, w1_ref, w2_ref, gw_ref, gb_ref, pj_ref, g_ref, b_ref = refs[:9]
        y_ref, xhat_ref, rstd_ref = refs[9 + n_g:12 + n_g]
        acc_ref, xb_ref = refs[12 + 2 * n_g:14 + 2 * n_g]
        i, j = pl.program_id(0), pl.program_id(1)
        if n_g:
            start, finish = _gather_steps(refs[12 + n_g:12 + 2 * n_g], *refs[14 + 2 * n_g:])
            pl.when((i == 0) & (j == 0))(start)

        @pl.when(j == 0)
        def _():
            x_t = x_ref[...]
            xb_ref[...] = x_t.astype(BF16)
            gate = _sigmoid(_bdot(x_t, _gate_w(gw_ref), NN) + gb_ref[...])
            acc_ref[...] = ALPHA * x_t + gate * _bdot(p_ref[...], _ple_proj(pj_ref), NN)

        h = jnp.maximum(_bdot(xb_ref[...], w1_ref[...], NN), 0.0)
        acc_ref[...] += _bdot(h * h, w2_ref[...], NN)

        @pl.when(j == N_SHARD - 1)
        def _():
            y_ref[...], xhat_ref[...], rstd_ref[...] = _ln_fwd(acc_ref[...], g_ref[...], b_ref[...])

        if n_g:
            pl.when((i == n_i - 1) & (j == N_SHARD - 1))(finish)

    tile = pl.BlockSpec((ts, D_MODEL), lambda i, j: (i, 0))
    row = _row(D_MODEL)
    out = _call(
        body, name, (n_i, N_SHARD),
        [tile, pl.BlockSpec((ts, PLE_DIM), lambda i, j: (i, 0))] + _mlp_weight_specs() + [row, row] + [ANY] * n_g,
        [tile, tile, pl.BlockSpec((ts, 1), lambda i, j: (i, 0))] + [ANY] * n_g,
        [_sds((s_len, D_MODEL)), _sds((s_len, D_MODEL)), _sds((s_len, 1))] + [_sds(a.shape, a.dtype) for a in gather],
        scratch=[pltpu.VMEM((ts, D_MODEL), F32), pltpu.VMEM((ts, D_MODEL), BF16)] + (_gather_sems(n_g) if n_g else []),
        sem=("arbitrary", "arbitrary"), aliases={9 + k: 3 + k for k in range(n_g)},
    )(x, p, w1s, w2s, gw, gb, proj, gain, bias, *gather)
    return out[:3], out[3:]


def _mlp_bwd(dy, xhat, rstd, x, p, w1s, w2s, gw, gb, proj, gain, name, scatter=()):
    s_len = x.shape[0]
    ts = min(ROW_TILE, s_len)
    fc = D_FF // N_SHARD
    n_i = s_len // ts
    n_s = len(scatter)

    def body(*refs):
        dy_ref, xh_ref, rs_ref, x_ref, p_ref, w1_ref, w2_ref, gw_ref, gb_ref, pj_ref, g_ref = refs[:11]
        dx_ref, a_ref, dh_ref, dzg_ref, dpp_ref, drb_ref, dg_ref, db_ref, dgb_ref = refs[11 + n_s:20 + n_s]
        acc_ref, xb_ref, dr_ref = refs[20 + 2 * n_s:23 + 2 * n_s]
        i, j = pl.program_id(0), pl.program_id(1)
        if n_s:
            start, finish = _scatter_steps(refs[11:11 + n_s], refs[20 + n_s:20 + 2 * n_s], *refs[23 + 2 * n_s:])
            pl.when((i == 0) & (j == 0))(start)

        @pl.when(j == 0)
        def _():
            dy_t, xh_t, x_t = dy_ref[...], xh_ref[...], x_ref[...]
            dr = _ln_bwd(dy_t, xh_t, rs_ref[...], g_ref[...])
            xb_ref[...] = x_t.astype(BF16)
            drb = dr.astype(BF16)
            dr_ref[...] = drb
            drb_ref[...] = drb
            gw_full = _gate_w(gw_ref)
            gate = _sigmoid(_bdot(x_t, gw_full, NN) + gb_ref[...])
            pp = _bdot(p_ref[...], _ple_proj(pj_ref), NN)
            dzg = dr * pp * gate * (1.0 - gate)
            dzg_ref[...] = dzg.astype(BF16)
            dpp_ref[...] = (dr * gate).astype(BF16)
            acc_ref[...] = ALPHA * dr + _bdot(dzg, gw_full, NT)
            first = i == 0
            _acc(dg_ref, first, jnp.sum(dy_t * xh_t, axis=0, keepdims=True))
            _acc(db_ref, first, jnp.sum(dy_t, axis=0, keepdims=True))
            _acc(dgb_ref, first, jnp.sum(dzg, axis=0, keepdims=True))

        h = jnp.maximum(_bdot(xb_ref[...], w1_ref[...], NN), 0.0)
        a_ref[...] = (h * h).astype(BF16)
        dh = (_bdot(dr_ref[...], w2_ref[...], NT) * (2.0 * h)).astype(BF16)
        dh_ref[...] = dh
        acc_ref[...] += _bdot(dh, w1_ref[...], NT)

        @pl.when(j == N_SHARD - 1)
        def _():
            dx_ref[...] = acc_ref[...]

        if n_s:
            pl.when((i == n_i - 1) & (j == N_SHARD - 1))(finish)

    tile = pl.BlockSpec((ts, D_MODEL), lambda i, j: (i, 0))
    ftile = pl.BlockSpec((ts, fc), lambda i, j: (i, j))
    row = _row(D_MODEL)
    out = _call(
        body, name, (n_i, N_SHARD),
        [tile, tile, pl.BlockSpec((ts, 1), lambda i, j: (i, 0)), tile, pl.BlockSpec((ts, PLE_DIM), lambda i, j: (i, 0))]
        + _mlp_weight_specs() + [row] + [ANY] * n_s,
        [tile, ftile, ftile, tile, tile, tile, row, row, row] + [ANY] * n_s,
        [_sds((s_len, D_MODEL)), _sds((s_len, D_FF), BF16), _sds((s_len, D_FF), BF16)]
        + [_sds((s_len, D_MODEL), BF16)] * 3 + [_sds((1, D_MODEL))] * 3 + [_sds(a.shape, a.dtype) for a in scatter],
        scratch=[pltpu.VMEM((ts, D_MODEL), F32), pltpu.VMEM((ts, D_MODEL), BF16), pltpu.VMEM((ts, D_MODEL), BF16)]
        + (_scatter_sems(n_s) if n_s else []),
        sem=("arbitrary", "arbitrary"))(dy, xhat, rstd, x, p, w1s, w2s, gw, gb, proj, gain, *scatter)
    return out[:9], out[9:]


def _wgrad(a, b, name, stack_cols=False):
    s_len, m = a.shape
    n = b.shape[1]
    ts = min(2048 if a.dtype == BF16 and b.dtype == BF16 else 1024, s_len)
    tm = min(m, 1024)
    tn = n // N_SHARD if stack_cols else (1408 if n == GDN_IN_PAD else min(n, 1024))
    n_s = s_len // ts

    def body(a_ref, b_ref, o_ref):
        _acc(o_ref, pl.program_id(2) == 0, _bdot(a_ref[...], b_ref[...], TN))

    if stack_cols:
        out_spec = pl.BlockSpec((None, tm, tn), lambda mi, nj, s: (nj, mi, 0))
        out_shape = _sds((N_SHARD, m, tn))
    else:
        out_spec = pl.BlockSpec((tm, tn), lambda mi, nj, s: (mi, nj))
        out_shape = _sds((m, n))
    return _call(
        body, name, (m // tm, n // tn, n_s),
        [pl.BlockSpec((ts, tm), lambda mi, nj, s: (s, mi)), pl.BlockSpec((ts, tn), lambda mi, nj, s: (s, nj))],
        out_spec, out_shape, sem=("parallel", "parallel", "arbitrary"))(a, b)


def _matmul_nn(a, b, name, tn):
    s_len, k = a.shape
    n = b.shape[1]
    ts = min(512, s_len)

    def body(a_ref, b_ref, o_ref):
        o_ref[...] = _bdot(a_ref[...], b_ref[...], NN)

    return _call(
        body, name, (s_len // ts, n // tn),
        [pl.BlockSpec((ts, k), lambda i, j: (i, 0)), pl.BlockSpec((k, tn), lambda i, j: (0, j))],
        pl.BlockSpec((ts, tn), lambda i, j: (i, j)), _sds((s_len, n)), sem=("parallel", "parallel"))(a, b)


def _matmul_nt_add(a, b, res, name):
    s_len, k = a.shape
    n = b.shape[0]
    ts = min(512, s_len)

    def body(a_ref, b_ref, r_ref, o_ref):
        o_ref[...] = r_ref[...] + _bdot(a_ref[...], b_ref[...], NT)

    return _call(
        body, name, (s_len // ts,),
        [pl.BlockSpec((ts, k), lambda i: (i, 0)), _full((n, k)), pl.BlockSpec((ts, n), lambda i: (i, 0))],
        pl.BlockSpec((ts, n), lambda i: (i, 0)), _sds((s_len, n)), sem=("parallel",))(a, b, res)


def _act_qkv(y):
    qkv = _silu(y)
    qs, ks = [], []
    for h in range(HEADS):
        qh = qkv[:, h * HEAD_DIM:(h + 1) * HEAD_DIM]
        kh = qkv[:, D_MODEL + h * HEAD_DIM:D_MODEL + (h + 1) * HEAD_DIM]
        qs.append(qh * (lax.rsqrt(jnp.sum(qh * qh, axis=-1, keepdims=True) + L2_EPS) * HEAD_DIM ** -0.5))
        ks.append(kh * lax.rsqrt(jnp.sum(kh * kh, axis=-1, keepdims=True) + L2_EPS))
    return jnp.concatenate(qs, axis=1), jnp.concatenate(ks, axis=1), qkv[:, 2 * D_MODEL:]


def _act_gb(ba, alog_l, dtb_l, tril):
    lane = lax.broadcasted_iota(jnp.int32, ba.shape, 1)
    g = jnp.where((lane >= HEADS) & (lane < 2 * HEADS), -jnp.exp(alog_l) * _softplus(ba + dtb_l), 0.0)
    return jnp.where(lane < HEADS, _sigmoid(ba), _hdot(tril, g, NN))


def _chunk_tril(t):
    ii = lax.broadcasted_iota(jnp.int32, (t, t), 0)
    jj = lax.broadcasted_iota(jnp.int32, (t, t), 1)
    return ((ii // CHUNK == jj // CHUNK) & (ii >= jj)).astype(F32)


def _conv_rows(xe, w, n_rows):
    y = xe[CONV_HALO:CONV_HALO + n_rows] * w[CONV_WIDTH - 1]
    for j in range(CONV_WIDTH - 1):
        y = y + pltpu.roll(xe, CONV_WIDTH - 1 - j, 0)[CONV_HALO:CONV_HALO + n_rows] * w[j]
    return y


def _conv_fwd(proj, conv_w, alog_l, dtb_l):
    s_len = proj.shape[0]
    ts = min(CONV_TILE, s_len)
    hb = ts // CONV_HALO

    def body(x_ref, xp_ref, ba_ref, w_ref, al_ref, dt_ref, q_ref, k_ref, v_ref, gcb_ref):
        i = pl.program_id(0)
        halo = jnp.where(i > 0, xp_ref[...], 0.0)
        taps = [w_ref[pl.ds(j, 1), :] for j in range(CONV_WIDTH)]
        y = _conv_rows(jnp.concatenate([halo, x_ref[...]], axis=0), taps, ts)
        q_ref[...], k_ref[...], v_ref[...] = _act_qkv(y)
        gcb_ref[...] = _act_gb(ba_ref[...], al_ref[...], dt_ref[...], _chunk_tril(ts))

    tile = pl.BlockSpec((ts, D_MODEL), lambda i: (i, 0))
    return _call(
        body, "gdn_conv_fwd", (s_len // ts,),
        [pl.BlockSpec((ts, QKV_DIM), lambda i: (i, 0)),
         pl.BlockSpec((CONV_HALO, QKV_DIM), lambda i: (jnp.maximum(i * hb - 1, 0), 0)),
         pl.BlockSpec((ts, 128), lambda i: (i, BA_BLOCK)), _full((CONV_WIDTH, QKV_DIM)), _row(128), _row(128)],
        [tile, tile, tile, pl.BlockSpec((ts, 128), lambda i: (i, 0))],
        [_sds((s_len, D_MODEL))] * 3 + [_sds((s_len, 128))],
        sem=("parallel",))(proj, proj, proj, conv_w, alog_l, dtb_l)


def _conv_bwd(proj, dq, dk, dv, dgcb, dz, conv_w, alog_l, dtb_l):
    s_len = proj.shape[0]
    ts = min(CONV_TILE, s_len)
    hb = ts // CONV_HALO
    n_t = s_len // ts
    te = ts + CONV_HALO

    def body(x_ref, xp_ref, xn_ref, ba_ref, dq_ref, dqn_ref, dk_ref, dkn_ref, dv_ref, dvn_ref, dgcb_ref, dz_ref,
             w_ref, al_ref, dt_ref, dp_ref, dw_ref, dal_ref, ddt_ref):
        i = pl.program_id(0)
        more = i < n_t - 1
        w = [w_ref[pl.ds(j, 1), :] for j in range(CONV_WIDTH)]
        x_t = x_ref[...]
        xe = jnp.concatenate([jnp.where(i > 0, xp_ref[...], 0.0), x_t, xn_ref[...]], axis=0)
        y_e, act_vjp = jax.vjp(_act_qkv, _conv_rows(xe, w, te))
        ct = tuple(jnp.concatenate([t[...], jnp.where(more, n[...], 0.0)], axis=0)
                   for t, n in ((dq_ref, dqn_ref), (dk_ref, dkn_ref), (dv_ref, dvn_ref)))
        (dy_e,) = act_vjp(ct)
        dx = dy_e[:ts] * w[CONV_WIDTH - 1]
        for j in range(CONV_WIDTH - 1):
            dx = dx + pltpu.roll(dy_e, te - (CONV_WIDTH - 1 - j), 0)[:ts] * w[j]
        dy_t = dy_e[:ts]
        xe_t = xe[:te]
        dws = [jnp.sum(dy_t * pltpu.roll(xe_t, CONV_WIDTH - 1 - j, 0)[CONV_HALO:], axis=0, keepdims=True)
               for j in range(CONV_WIDTH - 1)]
        dws.append(jnp.sum(dy_t * x_t, axis=0, keepdims=True))
        _, gb_vjp = jax.vjp(lambda ba, al, dt: _act_gb(ba, al, dt, _chunk_tril(ts)), ba_ref[...], al_ref[...], dt_ref[...])
        dba, dal, ddt = gb_vjp(dgcb_ref[...])
        dp_ref[...] = jnp.concatenate([dx.astype(BF16), dz_ref[...], dba.astype(BF16)], axis=1)
        first = i == 0
        for j in range(CONV_WIDTH):
            _acc(dw_ref.at[pl.ds(j, 1), :], first, dws[j])
        _acc(dal_ref, first, dal)
        _acc(ddt_ref, first, ddt)

    tile = pl.BlockSpec((ts, D_MODEL), lambda i: (i, 0))
    nxt = pl.BlockSpec((CONV_HALO, D_MODEL), lambda i: (jnp.minimum((i + 1) * hb, n_t * hb - 1), 0))
    return _call(
        body, "gdn_conv_bwd", (n_t,),
        [pl.BlockSpec((ts, QKV_DIM), lambda i: (i, 0)),
         pl.BlockSpec((CONV_HALO, QKV_DIM), lambda i: (jnp.maximum(i * hb - 1, 0), 0)),
         pl.BlockSpec((CONV_HALO, QKV_DIM), lambda i: (jnp.minimum((i + 1) * hb, n_t * hb - 1), 0)),
         pl.BlockSpec((ts, 128), lambda i: (i, BA_BLOCK)),
         tile, nxt, tile, nxt, tile, nxt, pl.BlockSpec((ts, 128), lambda i: (i, 0)), tile,
         _full((CONV_WIDTH, QKV_DIM)), _row(128), _row(128)],
        [pl.BlockSpec((ts, GDN_IN_PAD), lambda i: (i, 0)), _full((CONV_WIDTH, QKV_DIM)), _row(128), _row(128)],
        [_sds((s_len, GDN_IN_PAD), BF16), _sds((CONV_WIDTH, QKV_DIM)), _sds((1, 128)), _sds((1, 128))],
        sem=("arbitrary",))(proj, proj, proj, proj, dq, dq, dk, dk, dv, dv, dgcb, dz, conv_w, alog_l, dtb_l)


def _tri_inv(a_strict):
    ii = lax.broadcasted_iota(jnp.int32, (CHUNK, CHUNK), 0)
    jj = lax.broadcasted_iota(jnp.int32, (CHUNK, CHUNK), 1)
    x = (ii == jj).astype(F32) - a_strict
    pw = _mdot(a_strict, a_strict, BNN)
    for step in range(5):
        x = x + _mdot(x, pw, BNN)
        if step < 4:
            pw = _mdot(pw, pw, BNN)
    return x


@jax.custom_vjp
def _solved(a_strict, rhs, t, sol):
    return sol


def _solved_fwd(a_strict, rhs, t, sol):
    return sol, (t, sol)


def _solved_bwd(res, dsol):
    t, sol = res
    drhs = _mdot(t, dsol, BTN)
    return -_mdot(drhs, sol, BNT), drhs, jnp.zeros_like(t), jnp.zeros_like(sol)


_solved.defvjp(_solved_fwd, _solved_bwd)


def _prep(q, k, v, gc, beta, solve):
    ii = lax.broadcasted_iota(jnp.int32, (CHUNK, CHUNK), 0)
    jj = lax.broadcasted_iota(jnp.int32, (CHUNK, CHUNK), 1)
    causal, strict = ii >= jj, ii > jj
    gc_row = jnp.sum((ii == jj).astype(F32) * gc, axis=1, keepdims=True)
    decay = jnp.where(causal, jnp.exp(jnp.where(causal, gc - gc_row, 0.0)), 0.0)
    kb = k * beta
    a = jnp.where(strict, _bdot(kb, k, BNT) * decay, 0.0)
    eg = jnp.exp(gc)
    sol = solve(a, jnp.concatenate([v * beta, kb * eg], axis=-1))
    qk = _bdot(q, k, BNT) * decay
    last = lax.broadcasted_iota(jnp.int32, (CHUNK, 1), 0) == CHUNK - 1
    g_last = jnp.sum(jnp.where(last, gc, 0.0), axis=1, keepdims=True)
    kd = k * jnp.exp(g_last - gc)
    gl = jnp.exp(g_last) + jnp.zeros((1, 1, HEAD_DIM), F32)
    return sol[..., :HEAD_DIM], sol[..., HEAD_DIM:], qk, q * eg, kd, gl


def _prep_specs(s_len):
    rows = min(PREP_CHUNKS, s_len // CHUNK) * CHUNK
    m = rows // CHUNK
    hd = pl.BlockSpec((rows, HEAD_DIM), lambda c, h: (c, h))
    gcb = pl.BlockSpec((rows, 128), lambda c, h: (c, 0))
    qk = pl.BlockSpec((None, rows, CHUNK), lambda c, h: (h, c, 0))
    gl = pl.BlockSpec((None, m, HEADS, HEAD_DIM), lambda c, h: (c, 0, 0, 0))
    return rows, m, hd, gcb, qk, gl


def _head_cols(gcb, h, m):
    lane = lax.broadcasted_iota(jnp.int32, gcb.shape, 1)
    pick = lambda at: jnp.sum(jnp.where(lane == at, gcb, 0.0), axis=1, keepdims=True).reshape(m, CHUNK, 1)
    return pick(h + HEADS), pick(h)


def _gdn_prep(q, k, v, gcb):
    s_len = q.shape[0]
    rows, m, hd, gcb_spec, qk_spec, gl_spec = _prep_specs(s_len)

    def body(q_ref, k_ref, v_ref, gcb_ref, u_ref, w_ref, qd_ref, kd_ref, qk_ref, gl_ref, t_ref):
        r3 = lambda ref, d: ref[...].reshape(m, CHUNK, d)
        gc, beta = _head_cols(gcb_ref[...], pl.program_id(1), m)

        def solve(a, rhs):
            t = _tri_inv(a)
            t_ref[...] = t.reshape(rows, CHUNK)
            return _mdot(t, rhs, BNN)

        u, w, qk, qd, kd, gl = _prep(r3(q_ref, HEAD_DIM), r3(k_ref, HEAD_DIM), r3(v_ref, HEAD_DIM), gc, beta, solve)
        u_ref[...] = u.reshape(rows, HEAD_DIM)
        w_ref[...] = w.reshape(rows, HEAD_DIM)
        qd_ref[...] = qd.reshape(rows, HEAD_DIM).astype(BF16)
        kd_ref[...] = kd.reshape(rows, HEAD_DIM).astype(BF16)
        qk_ref[...] = qk.reshape(rows, CHUNK).astype(BF16)
        gl_ref[:, pl.ds(pl.program_id(1), 1), :] = gl

    n_g = s_len // rows
    return _call(
        body, "gdn_prep", (n_g, HEADS), [hd, hd, hd, gcb_spec], [hd, hd, hd, hd, qk_spec, gl_spec, qk_spec],
        [_sds((s_len, D_MODEL))] * 2 + [_sds((s_len, D_MODEL), BF16)] * 2
        + [_sds((HEADS, s_len, CHUNK), BF16), _sds((n_g, m, HEADS, HEAD_DIM)), _sds((HEADS, s_len, CHUNK))],
        sem=("parallel", "arbitrary"))(q, k, v, gcb)


def _gdn_prep_bwd(q, k, v, gcb, t_inv, u, w, du, dw, dqd, dkd, dqk, dgl):
    s_len = q.shape[0]
    rows, m, hd, gcb_spec, qk_spec, gl_spec = _prep_specs(s_len)

    def body(q_ref, k_ref, v_ref, gcb_ref, t_ref, u_ref, w_ref, du_ref, dw_ref, dqd_ref, dkd_ref, dqk_ref, dgl_ref,
             dq_ref, dk_ref, dv_ref, dgcb_ref):
        h = pl.program_id(1)
        r3 = lambda ref, d: ref[...].reshape(m, CHUNK, d)
        gc, beta = _head_cols(gcb_ref[...], h, m)
        t = r3(t_ref, CHUNK)
        sol = jnp.concatenate([r3(u_ref, HEAD_DIM), r3(w_ref, HEAD_DIM)], axis=-1)
        fn = lambda q_, k_, v_, gc_, bt_: _prep(q_, k_, v_, gc_, bt_, lambda a, rhs: _solved(a, rhs, t, sol))
        _, vjp = jax.vjp(fn, r3(q_ref, HEAD_DIM), r3(k_ref, HEAD_DIM), r3(v_ref, HEAD_DIM), gc, beta)
        ct = (r3(du_ref, HEAD_DIM), r3(dw_ref, HEAD_DIM), r3(dqk_ref, CHUNK), r3(dqd_ref, HEAD_DIM), r3(dkd_ref, HEAD_DIM),
              dgl_ref[:, pl.ds(h, 1), :] * (1.0 / HEAD_DIM))
        dq, dk, dv, dgc, dbt = vjp(ct)
        dq_ref[...] = dq.reshape(rows, HEAD_DIM)
        dk_ref[...] = dk.reshape(rows, HEAD_DIM)
        dv_ref[...] = dv.reshape(rows, HEAD_DIM)
        lane = lax.broadcasted_iota(jnp.int32, (rows, 128), 1)
        mine = jnp.where(lane == h, dbt.reshape(rows, 1), 0.0) + jnp.where(lane == h + HEADS, dgc.reshape(rows, 1), 0.0)
        _acc(dgcb_ref, h == 0, mine)

    return _call(
        body, "gdn_prep_bwd", (s_len // rows, HEADS),
        [hd, hd, hd, gcb_spec, qk_spec, hd, hd, hd, hd, hd, hd, qk_spec, gl_spec], [hd, hd, hd, gcb_spec],
        [_sds((s_len, D_MODEL))] * 3 + [_sds((s_len, 128))],
        sem=("parallel", "arbitrary"))(q, k, v, gcb, t_inv, u, w, du, dw, dqd, dkd, dqk, dgl)


def _scan_specs(n_c, m, reverse):
    at = (lambda n: n_c - 1 - n) if reverse else (lambda n: n)
    row = pl.BlockSpec((CHUNK, D_MODEL), lambda n: (at(n), 0))
    qk = pl.BlockSpec((HEADS, CHUNK, CHUNK), lambda n: (0, at(n), 0))
    gl = pl.BlockSpec((None, None, HEADS, HEAD_DIM), lambda n: (at(n) // m, at(n) % m, 0, 0))
    st = pl.BlockSpec((1, HEADS, HEAD_DIM, HEAD_DIM), lambda n: (at(n), 0, 0, 0))
    return row, qk, gl, st


def _gdn_scan(u, w, qd, kd, qk, gl):
    s_len = u.shape[0]
    n_c = s_len // CHUNK
    row, qk_spec, gl_spec, st_spec = _scan_specs(n_c, gl.shape[1], False)

    def body(u_ref, w_ref, qd_ref, kd_ref, qk_ref, gl_ref, o_ref, st_ref, state):
        hs = range(HEADS)
        sl = [slice(h * HEAD_DIM, (h + 1) * HEAD_DIM) for h in hs]
        first = pl.program_id(0) == 0
        s_all = [jnp.where(first, 0.0, state[h]) for h in hs]
        s_b = [s.astype(BF16) for s in s_all]
        ws = [_bdot(w_ref[:, sl[h]], s_b[h], NN) for h in hs]
        qs = [_bdot(qd_ref[:, sl[h]], s_b[h], NN) for h in hs]
        vn = [(u_ref[:, sl[h]] - ws[h]).astype(BF16) for h in hs]
        outs = [qs[h] + _bdot(qk_ref[h], vn[h], NN) for h in hs]
        nxt = [s_all[h] * gl_ref[pl.ds(h, 1), :] + _bdot(kd_ref[:, sl[h]], vn[h], TN) for h in hs]
        for h in hs:
            st_ref[0, h] = s_all[h]
            o_ref[:, sl[h]] = outs[h]
            state[h] = nxt[h]

    return _call(
        body, "gdn_scan", (n_c,), [row, row, row, row, qk_spec, gl_spec], [row, st_spec],
        [_sds((s_len, D_MODEL)), _sds((n_c, HEADS, HEAD_DIM, HEAD_DIM))],
        scratch=[pltpu.VMEM((HEADS, HEAD_DIM, HEAD_DIM), F32)], sem=("arbitrary",))(u, w, qd, kd, qk, gl)


def _gdn_scan_bwd(do, u, w, qd, kd, qk, gl, states):
    s_len = u.shape[0]
    n_c = s_len // CHUNK
    row, qk_spec, gl_spec, st_spec = _scan_specs(n_c, gl.shape[1], True)

    def body(do_ref, u_ref, w_ref, qd_ref, kd_ref, qk_ref, gl_ref, st_ref,
             du_ref, dw_ref, dqd_ref, dkd_ref, dqk_ref, dgl_ref, dstate):
        hs = range(HEADS)
        sl = [slice(h * HEAD_DIM, (h + 1) * HEAD_DIM) for h in hs]
        first = pl.program_id(0) == 0
        s_f = [st_ref[0, h] for h in hs]
        s_b = [s.astype(BF16) for s in s_f]
        ds_f = [jnp.where(first, 0.0, dstate[h]) for h in hs]
        ds_b = [d.astype(BF16) for d in ds_f]
        do_b = [do_ref[:, sl[h]].astype(BF16) for h in hs]
        w_b = [w_ref[:, sl[h]].astype(BF16) for h in hs]
        ws = [_bdot(w_b[h], s_b[h], NN) for h in hs]
        dvn = [_bdot(qk_ref[h], do_b[h], TN) + _bdot(kd_ref[:, sl[h]], ds_b[h], NN) for h in hs]
        dqd = [_bdot(do_b[h], s_b[h], NT) for h in hs]
        t_do = [_bdot(qd_ref[:, sl[h]], do_b[h], TN) for h in hs]
        vn = [(u_ref[:, sl[h]] - ws[h]).astype(BF16) for h in hs]
        dvn_b = [d.astype(BF16) for d in dvn]
        dw = [-_bdot(dvn_b[h], s_b[h], NT) for h in hs]
        dkd = [_bdot(vn[h], ds_b[h], NT) for h in hs]
        dqk = [_bdot(do_b[h], vn[h], NT) for h in hs]
        t_dv = [_bdot(w_b[h], dvn_b[h], TN) for h in hs]
        for h in hs:
            du_ref[:, sl[h]], dw_ref[:, sl[h]], dqd_ref[:, sl[h]], dkd_ref[:, sl[h]] = dvn[h], dw[h], dqd[h], dkd[h]
            dqk_ref[h] = dqk[h]
            dgl_ref[pl.ds(h, 1), :] = jnp.sum(s_f[h] * ds_f[h]) + jnp.zeros((1, HEAD_DIM), F32)
            dstate[h] = ds_f[h] * gl_ref[pl.ds(h, 1), :] + t_do[h] - t_dv[h]

    return _call(
        body, "gdn_scan_bwd", (n_c,), [row, row, row, row, row, qk_spec, gl_spec, st_spec],
        [row, row, row, row, qk_spec, gl_spec],
        [_sds((s_len, D_MODEL))] * 4 + [_sds((HEADS, s_len, CHUNK)), _sds(gl.shape)],
        scratch=[pltpu.VMEM((HEADS, HEAD_DIM, HEAD_DIM), F32)], sem=("arbitrary",))(do, u, w, qd, kd, qk, gl, states)


def _gate_norm(o, z, nw):
    outs = []
    for h in range(HEADS):
        oh = o[:, h * HEAD_DIM:(h + 1) * HEAD_DIM]
        outs.append(oh * lax.rsqrt(jnp.mean(oh * oh, axis=-1, keepdims=True) + RMS_EPS))
    return jnp.concatenate(outs, axis=1) * nw * _silu(z)


def _gdn_out_fwd(o, proj, x, w_out, nw, gain, bias):
    s_len = x.shape[0]
    ts = min(ROW_TILE, s_len)

    def body(o_ref, z_ref, x_ref, w_ref, nw_ref, g_ref, b_ref, y_ref, xhat_ref, rstd_ref):
        on = _gate_norm(o_ref[...], z_ref[...], nw_ref[...])
        r = ALPHA * x_ref[...] + _bdot(on, w_ref[...], NN)
        y_ref[...], xhat_ref[...], rstd_ref[...] = _ln_fwd(r, g_ref[...], b_ref[...])

    tile = pl.BlockSpec((ts, D_MODEL), lambda i: (i, 0))
    row = _row(D_MODEL)
    return _call(
        body, "gdn_out_fwd", (s_len // ts,),
        [tile, pl.BlockSpec((ts, D_MODEL), lambda i: (i, QKV_DIM // D_MODEL)), tile, _full((D_MODEL, D_MODEL)), row, row, row],
        [tile, tile, pl.BlockSpec((ts, 1), lambda i: (i, 0))],
        [_sds((s_len, D_MODEL)), _sds((s_len, D_MODEL)), _sds((s_len, 1))], sem=("parallel",))(o, proj, x, w_out, nw, gain, bias)


def _gdn_out_bwd(dy, xhat, rstd, o, proj, w_out, nw, gain):
    s_len = o.shape[0]
    ts = min(ROW_TILE, s_len)

    def body(dy_ref, xh_ref, rs_ref, o_ref, z_ref, w_ref, nw_ref, g_ref,
             dres_ref, do_ref, dz_ref, on_ref, drb_ref, dg_ref, db_ref, dnw_ref):
        dy_t, xh_t = dy_ref[...], xh_ref[...]
        dr = _ln_bwd(dy_t, xh_t, rs_ref[...], g_ref[...])
        dres_ref[...] = ALPHA * dr
        drb_ref[...] = dr.astype(BF16)
        on, vjp = jax.vjp(_gate_norm, o_ref[...], z_ref[...], nw_ref[...])
        on_ref[...] = on.astype(BF16)
        do, dz, dnw = vjp(_bdot(dr, w_ref[...], NT))
        do_ref[...] = do
        dz_ref[...] = dz.astype(BF16)
        first = pl.program_id(0) == 0
        _acc(dg_ref, first, jnp.sum(dy_t * xh_t, axis=0, keepdims=True))
        _acc(db_ref, first, jnp.sum(dy_t, axis=0, keepdims=True))
        _acc(dnw_ref, first, sum(dnw[:, h * HEAD_DIM:(h + 1) * HEAD_DIM] for h in range(HEADS)))

    tile = pl.BlockSpec((ts, D_MODEL), lambda i: (i, 0))
    row = _row(D_MODEL)
    return _call(
        body, "gdn_out_bwd", (s_len // ts,),
        [tile, tile, pl.BlockSpec((ts, 1), lambda i: (i, 0)), tile,
         pl.BlockSpec((ts, D_MODEL), lambda i: (i, QKV_DIM // D_MODEL)), _full((D_MODEL, D_MODEL)), row, row],
        [tile, tile, tile, tile, tile, row, row, _row(HEAD_DIM)],
        [_sds((s_len, D_MODEL))] * 2 + [_sds((s_len, D_MODEL), BF16)] * 3 + [_sds((1, D_MODEL))] * 2 + [_sds((1, HEAD_DIM))],
        sem=("arbitrary",))(dy, xhat, rstd, o, proj, w_out, nw, gain)


def _loss_head(y, target):
    s_len = y.shape[0]
    ts = min(ROW_TILE, s_len)

    def body(y_ref, t_ref, dy_ref, l_ref):
        err = y_ref[...] - t_ref[...]
        dy_ref[...] = err * (1.0 / D_MODEL)
        part = 0.5 * jnp.sum(jnp.mean(err * err, axis=-1, keepdims=True))
        _acc(l_ref, pl.program_id(0) == 0, part + jnp.zeros((1, 128), F32))

    tile = pl.BlockSpec((ts, D_MODEL), lambda i: (i, 0))
    return _call(body, "loss_head", (s_len // ts,), [tile, tile], [tile, _row(128)],
                 [_sds((s_len, D_MODEL)), _sds((1, 128))], sem=("arbitrary",))(y, target)


def _adamw(w, g, m, v, name):
    r, c = w.shape
    tr = min(ADAM_ROWS, r)

    def body(w_ref, g_ref, m_ref, v_ref, d_ref, nm_ref, nv_ref):
        g_t = g_ref[...]
        nm = ADAM_B1 * m_ref[...] + (1.0 - ADAM_B1) * g_t
        nv = ADAM_B2 * v_ref[...] + (1.0 - ADAM_B2) * (g_t * g_t)
        m_hat = nm / (1.0 - ADAM_B1 ** ADAM_STEP)
        v_hat = nv / (1.0 - ADAM_B2 ** ADAM_STEP)
        d_ref[...] = -ADAM_LR * (m_hat / (jnp.sqrt(v_hat) + ADAM_EPS) + ADAM_WD * w_ref[...])
        nm_ref[...] = nm
        nv_ref[...] = nv

    tile = pl.BlockSpec((tr, c), lambda i: (i, 0))
    return _call(body, name, (r // tr,), [tile] * 4, [tile] * 3, [_sds((r, c))] * 3, sem=("parallel",))(w, g, m, v)


def _assemble_w_in(shards):
    rows = 256
    width = GDN_IN_DIM // N_SHARD

    def body(s_ref, o_ref):
        pad = jnp.zeros((rows, GDN_IN_PAD - GDN_IN_DIM), shards.dtype)
        o_ref[...] = jnp.concatenate([s_ref[j] for j in range(N_SHARD)] + [pad], axis=1)

    return _call(body, "w_in_assemble", (D_MODEL // rows,), [pl.BlockSpec((N_SHARD, rows, width), lambda i: (0, i, 0))],
                 pl.BlockSpec((rows, GDN_IN_PAD), lambda i: (i, 0)), _sds((D_MODEL, GDN_IN_PAD), shards.dtype),
                 sem=("parallel",))(shards)


def _split_w_in(full):
    rows = 256
    width = GDN_IN_DIM // N_SHARD

    def body(f_ref, o_ref):
        f = f_ref[...]
        for j in range(N_SHARD):
            o_ref[j] = f[:, j * width:(j + 1) * width]

    return _call(body, "w_in_split", (D_MODEL // rows,), [pl.BlockSpec((rows, GDN_IN_PAD), lambda i: (i, 0))],
                 pl.BlockSpec((N_SHARD, rows, width), lambda i: (0, i, 0)), _sds((N_SHARD, D_MODEL, width), full.dtype),
                 sem=("parallel",))(full)


def _place():
    x, y, c = lax.axis_index("x"), lax.axis_index("y"), lax.axis_index("c")
    return x, y, c, [(1 - x, y), (x, 1 - y), (1 - x, 1 - y)]


def _row_tile(rows):
    return max(t for t in range(8, min(rows, 640) + 1, 8) if rows % t == 0)


def _place_shard(part, me, dtype, name, layer=0):
    _, _, r, c = part.shape
    tr = _row_tile(r)

    def body(me_ref, p_ref, o_ref):
        o_ref[...] = p_ref[...].astype(dtype)

    return pl.pallas_call(
        body, name=name, out_shape=_sds((N_SHARD, 2, r, c), dtype),
        grid_spec=pltpu.PrefetchScalarGridSpec(
            num_scalar_prefetch=1, grid=(2, r // tr),
            in_specs=[pl.BlockSpec((None, None, tr, c), lambda h, i, me_ref: (layer, h, i, 0))],
            out_specs=pl.BlockSpec((None, None, tr, c), lambda h, i, me_ref: (me_ref[0], h, i, 0))))(me, part)


def _gather_sems(n):
    return [pltpu.SemaphoreType.DMA((6 * n,)), pltpu.SemaphoreType.DMA((6 * n,))]


def _gather_steps(dsts, send_sems, recv_sems):
    n = len(dsts)
    x, y, c, chips = _place()
    me = 2 * x + y
    sibling = (x, y, 1 - c)

    def ici(k, j, slot):
        px, py = chips[j]
        view = dsts[k].at[slot, c]
        return pltpu.make_async_remote_copy(
            src_ref=view, dst_ref=view, send_sem=send_sems.at[6 * k + j],
            recv_sem=recv_sems.at[6 * k + j], device_id=(px, py, c), device_id_type=MESH)

    def d2d(k, j, half):
        px, py = chips[j]
        view = dsts[k].at[2 * px + py, half]
        return pltpu.make_async_remote_copy(
            src_ref=view, dst_ref=view, send_sem=send_sems.at[6 * k + 3 + j], recv_sem=recv_sems.at[6 * k + 3 + j],
            device_id=sibling, device_id_type=MESH)

    def start():
        for k in range(n):
            for j in range(3):
                ici(k, j, me).start()

    def finish():
        fwds = []
        for k in range(n):
            for j, (px, py) in enumerate(chips):
                ici(k, j, 2 * px + py).wait_recv()
                fwds.append(d2d(k, j, c))
                fwds[-1].start()
        for k in range(n):
            for j in range(3):
                d2d(k, j, 1 - c).wait_recv()
        for k in range(n):
            for j in range(3):
                ici(k, j, me).wait_send()
        for cp in fwds:
            cp.wait_send()

    return start, finish


def _all_gather(bufs, name):
    n = len(bufs)

    def body(*refs):
        start, finish = _gather_steps(refs[n:2 * n], *refs[2 * n:])
        start()
        finish()

    return pl.pallas_call(
        body, name=name, out_shape=[_sds(a.shape, a.dtype) for a in bufs],
        in_specs=[ANY] * n, out_specs=[ANY] * n, input_output_aliases={k: k for k in range(n)},
        scratch_shapes=_gather_sems(n))(*bufs)


def _swap_halves(pieces, name):
    n = len(pieces)

    def body(*refs):
        srcs, dsts = refs[:n], refs[n:2 * n]
        send_sems, recv_sems = refs[2 * n:]
        x, y, c, _ = _place()
        copies = []
        for k in range(n):
            hr = pieces[k].shape[1] // 2
            copies.append(pltpu.make_async_remote_copy(
                src_ref=srcs[k].at[:, pl.ds((1 - c) * hr, hr), :], dst_ref=dsts[k],
                send_sem=send_sems.at[k], recv_sem=recv_sems.at[k], device_id=(x, y, 1 - c), device_id_type=MESH))
        for cp in copies:
            cp.start()
        for cp in copies:
            cp.wait()

    return pl.pallas_call(
        body, name=name, out_shape=[_sds((N_SHARD, a.shape[1] // 2, a.shape[2])) for a in pieces],
        in_specs=[ANY] * n, out_specs=[ANY] * n,
        scratch_shapes=[pltpu.SemaphoreType.DMA((n,)), pltpu.SemaphoreType.DMA((n,))])(*pieces)


def _add_half(piece, other, place, dtype, name):
    n, hr, cols = other.shape
    tr = _row_tile(hr)

    def body(pl_ref, a_ref, b_ref, o_ref):
        o_ref[...] = (a_ref[...] + b_ref[...]).astype(dtype)

    tile = pl.BlockSpec((None, tr, cols), lambda s, i, pl_ref: (s, i, 0))
    return pl.pallas_call(
        body, name=name, out_shape=_sds(other.shape, dtype),
        grid_spec=pltpu.PrefetchScalarGridSpec(
            num_scalar_prefetch=1, grid=(n, hr // tr),
            in_specs=[pl.BlockSpec((None, None, tr, cols), lambda s, i, pl_ref: (s, pl_ref[1], i, 0)), tile],
            out_specs=tile))(place, piece.reshape(n, 2, hr, cols), other)


def _scatter_sems(n):
    return [pltpu.SemaphoreType.DMA((3 * n,)), pltpu.SemaphoreType.DMA((3 * n,))]


def _scatter_steps(srcs, dsts, send_sems, recv_sems):
    n = len(srcs)
    x, y, c, chips = _place()
    me = 2 * x + y

    def ici(k, j, src_slot, dst_slot):
        px, py = chips[j]
        return pltpu.make_async_remote_copy(
            src_ref=srcs[k].at[src_slot], dst_ref=dsts[k].at[dst_slot], send_sem=send_sems.at[3 * k + j],
            recv_sem=recv_sems.at[3 * k + j], device_id=(px, py, c), device_id_type=MESH)

    def start():
        for k in range(n):
            for j, (px, py) in enumerate(chips):
                ici(k, j, 2 * px + py, me).start()

    def finish():
        for k in range(n):
            for j, (px, py) in enumerate(chips):
                ici(k, j, me, 2 * px + py).wait_recv()
        for k in range(n):
            for j, (px, py) in enumerate(chips):
                ici(k, j, 2 * px + py, me).wait_send()

    return start, finish


def _scatter_chips(parts, name):
    n = len(parts)

    def body(*refs):
        start, finish = _scatter_steps(refs[:n], refs[n:2 * n], *refs[2 * n:])
        start()
        finish()

    return pl.pallas_call(
        body, name=name, out_shape=[_sds(a.shape, a.dtype) for a in parts],
        in_specs=[ANY] * n, out_specs=[ANY] * n, scratch_shapes=_scatter_sems(n))(*parts)


def _sum_chips(landed, own, place, name):
    _, r, cols = landed.shape
    tr = _row_tile(r)

    def body(pl_ref, q_ref, p_ref, o_ref):
        me = pl_ref[0]
        f = lambda j: jnp.where(me == j, p_ref[...], q_ref[j]).astype(F32)
        o_ref[...] = ((f(0) + f(1)) + f(2)) + f(3)

    return pl.pallas_call(
        body, name=name, out_shape=_sds((2, r, cols)),
        grid_spec=pltpu.PrefetchScalarGridSpec(
            num_scalar_prefetch=1, grid=(r // tr,),
            in_specs=[pl.BlockSpec((N_SHARD, tr, cols), lambda i, pl_ref: (0, i, 0)),
                      pl.BlockSpec((None, tr, cols), lambda i, pl_ref: (pl_ref[0], i, 0))],
            out_specs=pl.BlockSpec((None, tr, cols), lambda i, pl_ref: (pl_ref[1], i, 0))))(place, landed, own)


def _join_halves(bufs):
    n = len(bufs)

    def body(*refs):
        dsts = refs[n:2 * n]
        send_sems, recv_sems = refs[2 * n:]
        x, y, c, _ = _place()
        copies = [pltpu.make_async_remote_copy(
            src_ref=dsts[k].at[c], dst_ref=dsts[k].at[c], send_sem=send_sems.at[k], recv_sem=recv_sems.at[k],
            device_id=(x, y, 1 - c), device_id_type=MESH) for k in range(n)]
        for cp in copies:
            cp.start()
        for cp in copies:
            cp.wait()

    return pl.pallas_call(
        body, name="grads_join_halves", out_shape=[_sds(a.shape) for a in bufs], in_specs=[ANY] * n, out_specs=[ANY] * n,
        input_output_aliases={k: k for k in range(n)},
        scratch_shapes=[pltpu.SemaphoreType.DMA((n,)), pltpu.SemaphoreType.DMA((n,))])(*bufs)


GATHER_F32 = ("ln_gain", "ln_bias", "pool_b", "gdn_conv")
REPLICATED = ("pool_scale", "gdn_a_log", "gdn_dt_bias", "gdn_norm_w", "ple_gate_b")
WEIGHTS = ("ln_gain", "ln_bias", "pool_w", "pool_b", "pool_scale", "gdn_w_in", "gdn_conv", "gdn_a_log", "gdn_dt_bias",
           "gdn_norm_w", "gdn_w_out", "mlp_w1", "mlp_w2", "ple_gate_w", "ple_gate_b", "ple_proj")
SMALL_GRADS = ("ple_proj", "pool_w", "ln_gain", "ln_bias", "pool_b", "gdn_conv") + REPLICATED


def _pack(parts, lanes, row_multiple):
    flat = jnp.concatenate([a.reshape(-1) for a in parts])
    rows = -(-flat.shape[0] // (2 * lanes * row_multiple)) * row_multiple
    return jnp.pad(flat, (0, 2 * rows * lanes - flat.shape[0])).reshape(2, rows, lanes)


def _unpack(flat, shapes):
    out, off = [], 0
    for shp in shapes:
        n = math.prod(shp)
        out.append(flat[..., off:off + n].reshape(flat.shape[:-1] + tuple(shp)))
        off += n
    return out


def _pad_lanes(a, offset, width=128):
    return jnp.pad(a, ((0, 0), (offset, width - offset - a.shape[1])))


def kernel(x, p, ln_gain, ln_bias, pool_w, pool_b, pool_scale, gdn_w_in, gdn_conv, gdn_a_log, gdn_dt_bias, gdn_norm_w, gdn_w_out, mlp_w1, mlp_w2, ple_gate_w, ple_gate_b, ple_proj, loss_target, m_ln_gain, m_ln_bias, m_pool_w, m_pool_b, m_pool_scale, m_gdn_w_in, m_gdn_conv, m_gdn_a_log, m_gdn_dt_bias, m_gdn_norm_w, m_gdn_w_out, m_mlp_w1, m_mlp_w2, m_ple_gate_w, m_ple_gate_b, m_ple_proj, v_ln_gain, v_ln_bias, v_pool_w, v_pool_b, v_pool_scale, v_gdn_w_in, v_gdn_conv, v_gdn_a_log, v_gdn_dt_bias, v_gdn_norm_w, v_gdn_w_out, v_mlp_w1, v_mlp_w2, v_ple_gate_w, v_ple_gate_b, v_ple_proj):
    shard = dict(ln_gain=ln_gain, ln_bias=ln_bias, pool_w=pool_w, pool_b=pool_b, pool_scale=pool_scale, gdn_w_in=gdn_w_in,
                 gdn_conv=gdn_conv, gdn_a_log=gdn_a_log, gdn_dt_bias=gdn_dt_bias, gdn_norm_w=gdn_norm_w, gdn_w_out=gdn_w_out,
                 mlp_w1=mlp_w1, mlp_w2=mlp_w2, ple_gate_w=ple_gate_w, ple_gate_b=ple_gate_b, ple_proj=ple_proj)
    mom = dict(ln_gain=m_ln_gain, ln_bias=m_ln_bias, pool_w=m_pool_w, pool_b=m_pool_b, pool_scale=m_pool_scale,
               gdn_w_in=m_gdn_w_in, gdn_conv=m_gdn_conv, gdn_a_log=m_gdn_a_log, gdn_dt_bias=m_gdn_dt_bias,
               gdn_norm_w=m_gdn_norm_w, gdn_w_out=m_gdn_w_out, mlp_w1=m_mlp_w1, mlp_w2=m_mlp_w2, ple_gate_w=m_ple_gate_w,
               ple_gate_b=m_ple_gate_b, ple_proj=m_ple_proj)
    var = dict(ln_gain=v_ln_gain, ln_bias=v_ln_bias, pool_w=v_pool_w, pool_b=v_pool_b, pool_scale=v_pool_scale,
               gdn_w_in=v_gdn_w_in, gdn_conv=v_gdn_conv, gdn_a_log=v_gdn_a_log, gdn_dt_bias=v_gdn_dt_bias,
               gdn_norm_w=v_gdn_norm_w, gdn_w_out=v_gdn_w_out, mlp_w1=v_mlp_w1, mlp_w2=v_mlp_w2, ple_gate_w=v_ple_gate_w,
               ple_gate_b=v_ple_gate_b, ple_proj=v_ple_proj)

    xi, yi, ci = lax.axis_index("x"), lax.axis_index("y"), lax.axis_index("c")
    me = (2 * xi + yi).reshape(1).astype(jnp.int32)
    place = jnp.stack([2 * xi + yi, ci]).astype(jnp.int32)
    early = [("mlp_w1", 0), ("mlp_w2", 0), ("ple_gate_w", 0), ("ple_proj", 0), ("pool_w", 0)]
    late = [("mlp_w1", 1), ("mlp_w2", 1), ("ple_gate_w", 1), ("ple_proj", 1), ("gdn_w_out", 0), ("gdn_w_in", 0)]
    halved = lambda n: shard[n].reshape(shard[n].shape[0], 2, -1, shard[n].shape[-1])
    placed = lambda ops, tag: [_place_shard(halved(n), me, BF16, f"place_{tag}_{n}", l) for n, l in ops]
    small_in = _place_shard(_pack([shard[n] for n in GATHER_F32], 128, 8)[None], me, F32, "place_small")
    got_early = _all_gather(placed(early, "early") + [small_in], "weights_all_gather_early")
    placed_late = placed(late, "late")
    st = dict(zip(GATHER_F32, _unpack(got_early[-1].reshape(N_SHARD, -1), [shard[n].shape for n in GATHER_F32])))

    cat_last = lambda a: jnp.moveaxis(a, 0, -2).reshape(a.shape[1:-1] + (N_SHARD * a.shape[-1],))
    gain = cat_last(st["ln_gain"])
    bias = cat_last(st["ln_bias"])
    wp = got_early[4].reshape(N_SHARD, 4, POOL_GROUP // N_SHARD, POOL_GROUP)
    pb = cat_last(st["pool_b"]).reshape(1, D_MODEL)
    ps = pool_scale
    conv_w = cat_last(st["gdn_conv"])[0]
    merged = lambda g: g.reshape(N_SHARD, -1, g.shape[-1])
    mlp_w = lambda i, got: (merged(got[0]), merged(got[1]), merged(got[2]), ple_gate_b[i:i + 1], merged(got[3]))
    alog_l = _pad_lanes(gdn_a_log, HEADS)
    dtb_l = _pad_lanes(gdn_dt_bias, HEADS)
    nw = jnp.tile(gdn_norm_w, (1, HEADS))
    ln = lambda i, k: (gain[i, k][None], bias[i, k][None])

    x0 = x[0]
    p0, p1 = p[0, 0], p[1, 0]

    x1, xh1, rs1 = _pool_fwd(x0, wp, pb, ps, *ln(0, 0))
    (x2, xh2, rs2), got_late = _mlp_fwd(x1, p0, *mlp_w(0, got_early), *ln(0, 1), "mlp_fwd_0", gather=placed_late)
    w_out = got_late[4].reshape(D_MODEL, D_MODEL)
    w_in = _assemble_w_in(merged(got_late[5]))
    proj = _matmul_nn(x2, w_in, "gdn_in_proj", GDN_IN_PAD // 3)
    q, k, v, gcb = _conv_fwd(proj, conv_w, alog_l, dtb_l)
    u, w, qd, kd, qk, gl, t_inv = _gdn_prep(q, k, v, gcb)
    o, states = _gdn_scan(u, w, qd, kd, qk, gl)
    x3, xh3, rs3 = _gdn_out_fwd(o, proj, x2, w_out, nw, *ln(1, 0))
    (x4, xh4, rs4), _ = _mlp_fwd(x3, p1, *mlp_w(1, got_late), *ln(1, 1), "mlp_fwd_1")
    dy4, loss_l = _loss_head(x4, loss_target[0])

    g_gain = [[None, None], [None, None]]
    g_bias = [[None, None], [None, None]]

    def mlp_grads(i, dy, xh, rs, x_mid, p_i, got, scatter=()):
        (dx, a, dh, dzg, dpp, drb, dg, db, dgb), landed = _mlp_bwd(
            dy, xh, rs, x_mid, p_i, *mlp_w(i, got), ln(i, 1)[0], f"mlp_bwd_{i}", scatter=scatter)
        g_gain[i][1], g_bias[i][1] = dg, db
        return dx, dict(
            mlp_w1=_wgrad(x_mid, dh, f"dw1_{i}", stack_cols=True), mlp_w2=_wgrad(a, drb, f"dw2_{i}").reshape(N_SHARD, -1, D_MODEL),
            ple_gate_w=_wgrad(x_mid, dzg, f"dgate_w_{i}").reshape(N_SHARD, -1, D_MODEL),
            ple_proj=_wgrad(p_i, dpp, f"dproj_{i}", stack_cols=True), ple_gate_b=dgb), landed

    def chip_sums(pieces, wire, tag):
        others = _swap_halves(pieces, f"grads_swap_halves_{tag}")
        return [_add_half(a, b, place, t, f"grads_add_half_{tag}{i}") for i, (a, b, t) in enumerate(zip(pieces, others, wire))]

    dx3, gl1, _ = mlp_grads(1, dy4, xh4, rs4, x3, p1, got_late)
    dres, do, dz, on_b, drb3, g_gain[1][0], g_bias[1][0], d_nw = _gdn_out_bwd(dx3, xh3, rs3, o, proj, w_out, nw, ln(1, 0)[0])
    d_wout = _wgrad(on_b, drb3, "dw_out").reshape(N_SHARD, -1, D_MODEL)
    du, dw, dqd, dkd, dqk, dgl = _gdn_scan_bwd(do, u, w, qd, kd, qk, gl, states)
    dq, dk, dv, dgcb = _gdn_prep_bwd(q, k, v, gcb, t_inv, u, w, du, dw, dqd, dkd, dqk, dgl)
    dproj, d_conv, d_alog_l, d_dtb_l = _conv_bwd(proj, dq, dk, dv, dgcb, dz, conv_w, alog_l, dtb_l)
    dx2 = _matmul_nt_add(dproj, w_in, dres, "gdn_in_bwd")
    d_win = _split_w_in(_wgrad(x2, dproj, "dw_in"))
    sums_late = chip_sums([gl1["mlp_w1"], gl1["mlp_w2"], gl1["ple_gate_w"], d_wout, d_win], [BF16] * 5, "late")
    dx1, gl0, landed_late = mlp_grads(0, dx2, xh2, rs2, x1, p0, got_early, scatter=sums_late)
    dx0, g_gain[0][0], g_bias[0][0], d_ps, d_pb, d_wp = _pool_bwd(dx1, xh1, rs1, x0, wp, pb, ps, ln(0, 0)[0])

    split_last = lambda a: jnp.moveaxis(a.reshape(a.shape[:-1] + (N_SHARD, a.shape[-1] // N_SHARD)), -2, 0)
    small_st = dict(
        ple_proj=jnp.stack([gl0["ple_proj"], gl1["ple_proj"]], axis=1),
        pool_w=jnp.moveaxis(d_wp.reshape(4, N_SHARD, POOL_GROUP // N_SHARD, POOL_GROUP), 1, 0)[:, None],
        ln_gain=split_last(jnp.stack([jnp.concatenate(r, axis=0) for r in g_gain])),
        ln_bias=split_last(jnp.stack([jnp.concatenate(r, axis=0) for r in g_bias])),
        pool_b=split_last(d_pb.reshape(1, 4, POOL_GROUP)),
        gdn_conv=split_last(d_conv)[:, None],
    )
    rep = dict(pool_scale=d_ps, gdn_a_log=d_alog_l[:, HEADS:2 * HEADS], gdn_dt_bias=d_dtb_l[:, HEADS:2 * HEADS],
               gdn_norm_w=d_nw, ple_gate_b=jnp.concatenate([gl0["ple_gate_b"], gl1["ple_gate_b"]], axis=0))
    for n in REPLICATED:
        small_st[n] = jnp.broadcast_to(rep[n][None], (N_SHARD,) + rep[n].shape)
    small_flat = jnp.concatenate([small_st[n].reshape(N_SHARD, -1) for n in SMALL_GRADS], axis=1)
    small_rows = -(-small_flat.shape[1] // (16 * LANES)) * 16
    small_piece = jnp.pad(small_flat, ((0, 0), (0, small_rows * LANES - small_flat.shape[1]))).reshape(N_SHARD, small_rows, LANES)

    sums_early = chip_sums([gl0["mlp_w1"], gl0["mlp_w2"], gl0["ple_gate_w"], small_piece], [BF16] * 3 + [F32], "early")
    landed_early = _scatter_chips(sums_early, "grads_scatter_chips_early")
    red = _join_halves([_sum_chips(q_, p_, place, f"grads_sum_chips_{i}") for i, (q_, p_) in
                        enumerate(zip(list(landed_early) + list(landed_late), sums_early + sums_late))])
    red = [r.reshape(-1, r.shape[-1]) for r in red]
    grads = dict(mlp_w1=jnp.stack([red[0], red[4]]), mlp_w2=jnp.stack([red[1], red[5]]), ple_gate_w=jnp.stack([red[2], red[6]]),
                 gdn_w_out=red[7][None], gdn_w_in=red[8][None])
    grads.update(zip(SMALL_GRADS, _unpack(red[3].reshape(-1), [shard[n].shape for n in SMALL_GRADS])))

    delta, new_m, new_v = {}, {}, {}
    small = [n for n in WEIGHTS if shard[n].size < 128 * 128]
    for n in WEIGHTS:
        if n in small:
            continue
        to2d = lambda a, n=n: a.reshape(-1, shard[n].shape[-1])
        d2, m2, v2 = _adamw(to2d(shard[n]), to2d(grads[n]), to2d(mom[n]), to2d(var[n]), "adamw_" + n)
        delta[n], new_m[n], new_v[n] = (t.reshape(shard[n].shape) for t in (d2, m2, v2))
    pk = lambda d: _pack([d[n] for n in small], 128, 8).reshape(-1, 128)
    d2, m2, v2 = _adamw(pk(shard), pk(grads), pk(mom), pk(var), "adamw_small")
    for dst, t in ((delta, d2), (new_m, m2), (new_v, v2)):
        dst.update(zip(small, _unpack(t.reshape(-1), [shard[n].shape for n in small])))

    loss = lax.psum(loss_l[0, 0], ("x", "y", "c"))
    return (loss, dx0[None], *[grads[n] for n in WEIGHTS], *[delta[n] for n in WEIGHTS],
            *[new_m[n] for n in WEIGHTS], *[new_v[n] for n in WEIGHTS])
```

```python
import math

import jax
import jax.numpy as jnp
from jax import lax
from jax.experimental import pallas as pl
from jax.experimental.pallas import tpu as pltpu

F32 = jnp.float32
BF16 = jnp.bfloat16

D_MODEL = 1024
D_FF = 4096
PLE_DIM = 256
N_SHARD = 4
POOL_WINDOWS = (2, 4, 8, 16)
POOL_GROUP = 256
POOL_HALO = 16
HEADS = 8
HEAD_DIM = 128
CHUNK = 64
CONV_WIDTH = 4
CONV_HALO = 8
QKV_DIM = 3 * D_MODEL
GDN_IN_DIM = QKV_DIM + D_MODEL + 2 * HEADS
GDN_IN_PAD = 4224
BA_BLOCK = (QKV_DIM + D_MODEL) // 128
ALPHA = (2.0 * 2) ** 0.25
LN_EPS = 1e-5
RMS_EPS = 1e-6
L2_EPS = 1e-6
ADAM_LR, ADAM_B1, ADAM_B2, ADAM_EPS, ADAM_WD, ADAM_STEP = 0.001, 0.9, 0.999, 1e-08, 0.01, 10

ROW_TILE = 512
CONV_TILE = 256
PREP_CHUNKS = 16
LANES = 1024
ADAM_ROWS = 256

NN = (((1,), (0,)), ((), ()))
NT = (((1,), (1,)), ((), ()))
TN = (((0,), (0,)), ((), ()))
BNN = (((2,), (1,)), ((0,), (0,)))
BNT = (((2,), (2,)), ((0,), (0,)))
BTN = (((1,), (1,)), ((0,), (0,)))
MESH = pl.DeviceIdType.MESH
ANY = pl.BlockSpec(memory_space=pl.ANY)


def _bdot(a, b, dims):
    return lax.dot_general(a.astype(BF16), b.astype(BF16), dims, preferred_element_type=F32)


def _hdot(a, b, dims):
    return lax.dot_general(a, b, dims, precision=lax.Precision.HIGHEST, preferred_element_type=F32)


def _mdot(a, b, dims):
    return lax.dot_general(a, b, dims, precision=lax.Precision.HIGH, preferred_element_type=F32)


def _sigmoid(x):
    return 0.5 * jnp.tanh(0.5 * x) + 0.5


def _silu(x):
    return x * _sigmoid(x)


def _softplus(x):
    return jnp.maximum(x, 0.0) + jnp.log1p(jnp.exp(-jnp.abs(x)))


def _call(body, name, grid, in_specs, out_specs, out_shape, scratch=(), sem=None, aliases=None):
    params = pltpu.CompilerParams(dimension_semantics=sem) if sem else None
    return pl.pallas_call(
        body, name=name, grid=grid, in_specs=in_specs, out_specs=out_specs, out_shape=out_shape,
        scratch_shapes=list(scratch), compiler_params=params, input_output_aliases=aliases or {})


def _row(d):
    return pl.BlockSpec((1, d), lambda *_: (0, 0))


def _full(shape):
    n = len(shape)
    return pl.BlockSpec(shape, lambda *_: (0,) * n)


def _sds(shape, dtype=F32):
    return jax.ShapeDtypeStruct(shape, dtype)


def _ln_fwd(r, gain, bias):
    mu = jnp.mean(r, axis=-1, keepdims=True)
    xc = r - mu
    rstd = lax.rsqrt(jnp.mean(xc * xc, axis=-1, keepdims=True) + LN_EPS)
    xhat = xc * rstd
    return xhat * gain + bias, xhat, rstd


def _ln_bwd(dy, xhat, rstd, gain):
    dxh = dy * gain
    m1 = jnp.mean(dxh, axis=-1, keepdims=True)
    m2 = jnp.mean(dxh * xhat, axis=-1, keepdims=True)
    return rstd * (dxh - m1 - xhat * m2)


def _acc(ref, first, val):
    @pl.when(first)
    def _():
        ref[...] = val

    @pl.when(jnp.logical_not(first))
    def _():
        ref[...] += val


def _pooled_groups(xe, t0, ts):
    pos = (t0 + lax.broadcasted_iota(jnp.int32, (ts, 1), 0) + 1).astype(F32)
    outs = []
    for gi, win in enumerate(POOL_WINDOWS):
        xs = xe[:, gi * POOL_GROUP:(gi + 1) * POOL_GROUP]
        s, k = xs, 1
        while k < win:
            s = s + pltpu.roll(s, k, 0)
            k *= 2
        mean = s[POOL_HALO:] / jnp.minimum(pos, float(win))
        outs.append(mean - xs[POOL_HALO:])
    return outs


def _pool_groups_w(w_ref):
    return [jnp.concatenate([w_ref[s, g] for s in range(N_SHARD)], axis=0) for g in range(4)]


def _pool_fwd(x, wp, pb, ps, gain, bias):
    s_len = x.shape[0]
    ts = min(ROW_TILE, s_len)
    hb = ts // POOL_HALO

    def body(x_ref, halo_ref, w_ref, pb_ref, ps_ref, g_ref, b_ref, y_ref, xhat_ref, rstd_ref):
        i = pl.program_id(0)
        x_t = x_ref[...]
        halo = jnp.where(i > 0, halo_ref[...], 0.0)
        pooled = _pooled_groups(jnp.concatenate([halo, x_t], axis=0), i * ts, ts)
        wg = _pool_groups_w(w_ref)
        y = jnp.concatenate([_bdot(pooled[g], wg[g], NN) for g in range(4)], axis=1) + pb_ref[...]
        r = ALPHA * x_t + y * ps_ref[...]
        y_ref[...], xhat_ref[...], rstd_ref[...] = _ln_fwd(r, g_ref[...], b_ref[...])

    tile = pl.BlockSpec((ts, D_MODEL), lambda i: (i, 0))
    return _call(
        body, "pool_fwd", (s_len // ts,),
        [tile, pl.BlockSpec((POOL_HALO, D_MODEL), lambda i: (jnp.maximum(i * hb - 1, 0), 0)),
         _full(wp.shape), _row(D_MODEL), _row(D_MODEL), _row(D_MODEL), _row(D_MODEL)],
        [tile, tile, pl.BlockSpec((ts, 1), lambda i: (i, 0))],
        [_sds((s_len, D_MODEL)), _sds((s_len, D_MODEL)), _sds((s_len, 1))],
        sem=("parallel",))(x, x, wp, pb, ps, gain, bias)


def _pool_bwd(dy, xhat, rstd, x, wp, pb, ps, gain):
    s_len = x.shape[0]
    ts = min(ROW_TILE, s_len)
    hb = ts // POOL_HALO
    n_t = s_len // ts
    ne = ts + POOL_HALO

    def body(dy_ref, dyn_ref, xh_ref, xhn_ref, rs_ref, rsn_ref, x_ref, xp_ref, w_ref, pb_ref, ps_ref, g_ref,
             dx_ref, dg_ref, db_ref, dps_ref, dpb_ref, dw_ref):
        i = pl.program_id(0)
        more = i < n_t - 1
        dy_t, xh_t = dy_ref[...], xh_ref[...]
        dy_e = jnp.concatenate([dy_t, jnp.where(more, dyn_ref[...], 0.0)], axis=0)
        xh_e = jnp.concatenate([xh_t, xhn_ref[...]], axis=0)
        rs_e = jnp.concatenate([rs_ref[...], rsn_ref[...]], axis=0)
        dr_e = _ln_bwd(dy_e, xh_e, rs_e, g_ref[...])
        dyy_e = dr_e * ps_ref[...]
        pos_e = (i * ts + lax.broadcasted_iota(jnp.int32, (ne, 1), 0) + 1).astype(F32)
        dxs = []
        wg = _pool_groups_w(w_ref)
        for gi, win in enumerate(POOL_WINDOWS):
            sl = slice(gi * POOL_GROUP, (gi + 1) * POOL_GROUP)
            dpool = _bdot(dyy_e[:, sl], wg[gi], NT)
            s, k = dpool / jnp.minimum(pos_e, float(win)), 1
            while k < win:
                s = s + pltpu.roll(s, ne - k, 0)
                k *= 2
            dxs.append(s[:ts] - dpool[:ts])
        dx_ref[...] = ALPHA * dr_e[:ts] + jnp.concatenate(dxs, axis=1)

        x_t = x_ref[...]
        halo = jnp.where(i > 0, xp_ref[...], 0.0)
        pooled = _pooled_groups(jnp.concatenate([halo, x_t], axis=0), i * ts, ts)
        y = jnp.concatenate([_bdot(pooled[g], wg[g], NN) for g in range(4)], axis=1) + pb_ref[...]
        dr_t, dyy_t = dr_e[:ts], dyy_e[:ts]
        first = i == 0
        _acc(dg_ref, first, jnp.sum(dy_t * xh_t, axis=0, keepdims=True))
        _acc(db_ref, first, jnp.sum(dy_t, axis=0, keepdims=True))
        _acc(dps_ref, first, jnp.sum(dr_t * y, axis=0, keepdims=True))
        _acc(dpb_ref, first, jnp.sum(dyy_t, axis=0, keepdims=True))
        for g in range(4):
            _acc(dw_ref.at[g], first, _bdot(pooled[g], dyy_t[:, g * POOL_GROUP:(g + 1) * POOL_GROUP], TN))

    tile = pl.BlockSpec((ts, D_MODEL), lambda i: (i, 0))
    nxt = pl.BlockSpec((POOL_HALO, D_MODEL), lambda i: (jnp.minimum((i + 1) * hb, n_t * hb - 1), 0))
    prv = pl.BlockSpec((POOL_HALO, D_MODEL), lambda i: (jnp.maximum(i * hb - 1, 0), 0))
    rs_t = pl.BlockSpec((ts, 1), lambda i: (i, 0))
    rs_n = pl.BlockSpec((POOL_HALO, 1), lambda i: (jnp.minimum((i + 1) * hb, n_t * hb - 1), 0))
    row = _row(D_MODEL)
    return _call(
        body, "pool_bwd", (n_t,),
        [tile, nxt, tile, nxt, rs_t, rs_n, tile, prv, _full(wp.shape), row, row, row],
        [tile, row, row, row, row, _full((4, POOL_GROUP, POOL_GROUP))],
        [_sds((s_len, D_MODEL))] + [_sds((1, D_MODEL))] * 4 + [_sds((4, POOL_GROUP, POOL_GROUP))],
        sem=("arbitrary",))(dy, dy, xhat, xhat, rstd, rstd, x, x, wp, pb, ps, gain)


def _mlp_weight_specs():
    fc = D_FF // N_SHARD
    return [pl.BlockSpec((None, D_MODEL, fc), lambda i, j: (j, 0, 0)),
            pl.BlockSpec((None, fc, D_MODEL), lambda i, j: (j, 0, 0)),
            _full((N_SHARD, D_MODEL // N_SHARD, D_MODEL)),
            _row(D_MODEL),
            _full((N_SHARD, PLE_DIM, D_MODEL // N_SHARD))]


def _gate_w(gw_ref):
    return gw_ref[...].reshape(D_MODEL, D_MODEL)


def _ple_proj(pj_ref):
    return jnp.concatenate([pj_ref[s] for s in range(N_SHARD)], axis=1)


def _mlp_fwd(x, p, w1s, w2s, gw, gb, proj, gain, bias, name, gather=()):
    s_len = x.shape[0]
    ts = min(ROW_TILE, s_len)
    n_i = s_len // ts
    n_g = len(gather)

    def body(*refs):
        x_ref, p_ref, w1_ref, w2_ref, gw_ref, gb_ref, pj_ref, g_ref, b_ref = refs[:9]
        y_ref, xhat_ref, rstd_ref, a_ref = refs[9 + n_g:13 + n_g]
        acc_ref, xb_ref = refs[13 + 2 * n_g:15 + 2 * n_g]
        i, j = pl.program_id(0), pl.program_id(1)
        if n_g:
            start, finish = _gather_steps(refs[13 + n_g:13 + 2 * n_g], *refs[15 + 2 * n_g:])
            pl.when((i == 0) & (j == 0))(start)

        @pl.when(j == 0)
        def _():
            x_t = x_ref[...]
            xb_ref[...] = x_t.astype(BF16)
            gate = _sigmoid(_bdot(x_t, _gate_w(gw_ref), NN) + gb_ref[...])
            acc_ref[...] = ALPHA * x_t + gate * _bdot(p_ref[...], _ple_proj(pj_ref), NN)

        h = jnp.maximum(_bdot(xb_ref[...], w1_ref[...], NN), 0.0)
        a = (h * h).astype(BF16)
        a_ref[...] = a
        acc_ref[...] += _bdot(a, w2_ref[...], NN)

        @pl.when(j == N_SHARD - 1)
        def _():
            y_ref[...], xhat_ref[...], rstd_ref[...] = _ln_fwd(acc_ref[...], g_ref[...], b_ref[...])

        if n_g:
            pl.when((i == n_i - 1) & (j == N_SHARD - 1))(finish)

    tile = pl.BlockSpec((ts, D_MODEL), lambda i, j: (i, 0))
    row = _row(D_MODEL)
    out = _call(
        body, name, (n_i, N_SHARD),
        [tile, pl.BlockSpec((ts, PLE_DIM), lambda i, j: (i, 0))] + _mlp_weight_specs() + [row, row] + [ANY] * n_g,
        [tile, tile, pl.BlockSpec((ts, 1), lambda i, j: (i, 0)), pl.BlockSpec((ts, D_FF // N_SHARD), lambda i, j: (i, j))]
        + [ANY] * n_g,
        [_sds((s_len, D_MODEL)), _sds((s_len, D_MODEL)), _sds((s_len, 1)), _sds((s_len, D_FF), BF16)]
        + [_sds(a.shape, a.dtype) for a in gather],
        scratch=[pltpu.VMEM((ts, D_MODEL), F32), pltpu.VMEM((ts, D_MODEL), BF16)] + (_gather_sems(n_g) if n_g else []),
        sem=("arbitrary", "arbitrary"), aliases={9 + k: 4 + k for k in range(n_g)},
    )(x, p, w1s, w2s, gw, gb, proj, gain, bias, *gather)
    return out[:4], out[4:]


def _mlp_bwd(dy, xhat, rstd, x, a, p, w1s, w2s, gw, gb, proj, gain, name, scatter=()):
    s_len = x.shape[0]
    ts = min(ROW_TILE, s_len)
    fc = D_FF // N_SHARD
    n_i = s_len // ts
    n_s = len(scatter)

    def body(*refs):
        dy_ref, xh_ref, rs_ref, x_ref, a_ref, p_ref, w1_ref, w2_ref, gw_ref, gb_ref, pj_ref, g_ref = refs[:12]
        dx_ref, dh_ref, dzg_ref, dpp_ref, drb_ref, dg_ref, db_ref, dgb_ref = refs[12 + n_s:20 + n_s]
        acc_ref, dr_ref = refs[20 + 2 * n_s:22 + 2 * n_s]
        i, j = pl.program_id(0), pl.program_id(1)
        if n_s:
            start, finish = _scatter_steps(refs[12:12 + n_s], refs[20 + n_s:20 + 2 * n_s], *refs[22 + 2 * n_s:])
            pl.when((i == 0) & (j == 0))(start)

        @pl.when(j == 0)
        def _():
            dy_t, xh_t, x_t = dy_ref[...], xh_ref[...], x_ref[...]
            dr = _ln_bwd(dy_t, xh_t, rs_ref[...], g_ref[...])
            drb = dr.astype(BF16)
            dr_ref[...] = drb
            drb_ref[...] = drb
            gw_full = _gate_w(gw_ref)
            gate = _sigmoid(_bdot(x_t, gw_full, NN) + gb_ref[...])
            pp = _bdot(p_ref[...], _ple_proj(pj_ref), NN)
            dzg = dr * pp * gate * (1.0 - gate)
            dzg_ref[...] = dzg.astype(BF16)
            dpp_ref[...] = (dr * gate).astype(BF16)
            acc_ref[...] = ALPHA * dr + _bdot(dzg, gw_full, NT)
            first = i == 0
            _acc(dg_ref, first, jnp.sum(dy_t * xh_t, axis=0, keepdims=True))
            _acc(db_ref, first, jnp.sum(dy_t, axis=0, keepdims=True))
            _acc(dgb_ref, first, jnp.sum(dzg, axis=0, keepdims=True))

        dh = (_bdot(dr_ref[...], w2_ref[...], NT) * (2.0 * jnp.sqrt(a_ref[...].astype(F32)))).astype(BF16)
        dh_ref[...] = dh
        acc_ref[...] += _bdot(dh, w1_ref[...], NT)

        @pl.when(j == N_SHARD - 1)
        def _():
            dx_ref[...] = acc_ref[...]

        if n_s:
            pl.when((i == n_i - 1) & (j == N_SHARD - 1))(finish)

    tile = pl.BlockSpec((ts, D_MODEL), lambda i, j: (i, 0))
    ftile = pl.BlockSpec((ts, fc), lambda i, j: (i, j))
    row = _row(D_MODEL)
    out = _call(
        body, name, (n_i, N_SHARD),
        [tile, tile, pl.BlockSpec((ts, 1), lambda i, j: (i, 0)), tile, ftile, pl.BlockSpec((ts, PLE_DIM), lambda i, j: (i, 0))]
        + _mlp_weight_specs() + [row] + [ANY] * n_s,
        [tile, ftile, tile, tile, tile, row, row, row] + [ANY] * n_s,
        [_sds((s_len, D_MODEL)), _sds((s_len, D_FF), BF16)]
        + [_sds((s_len, D_MODEL), BF16)] * 3 + [_sds((1, D_MODEL))] * 3 + [_sds(t.shape, t.dtype) for t in scatter],
        scratch=[pltpu.VMEM((ts, D_MODEL), F32), pltpu.VMEM((ts, D_MODEL), BF16)] + (_scatter_sems(n_s) if n_s else []),
        sem=("arbitrary", "arbitrary"))(dy, xhat, rstd, x, a, p, w1s, w2s, gw, gb, proj, gain, *scatter)
    return out[:8], out[8:]


def _wgrad(a, b, name, stack_cols=False):
    s_len, m = a.shape
    n = b.shape[1]
    ts = min(2048 if a.dtype == BF16 and b.dtype == BF16 else 1024, s_len)
    tm = min(m, 1024)
    tn = n // N_SHARD if stack_cols else (1408 if n == GDN_IN_PAD else min(n, 1024))
    n_s = s_len // ts

    def body(a_ref, b_ref, o_ref):
        _acc(o_ref, pl.program_id(2) == 0, _bdot(a_ref[...], b_ref[...], TN))

    if stack_cols:
        out_spec = pl.BlockSpec((None, tm, tn), lambda mi, nj, s: (nj, mi, 0))
        out_shape = _sds((N_SHARD, m, tn))
    else:
        out_spec = pl.BlockSpec((tm, tn), lambda mi, nj, s: (mi, nj))
        out_shape = _sds((m, n))
    return _call(
        body, name, (m // tm, n // tn, n_s),
        [pl.BlockSpec((ts, tm), lambda mi, nj, s: (s, mi)), pl.BlockSpec((ts, tn), lambda mi, nj, s: (s, nj))],
        out_spec, out_shape, sem=("parallel", "parallel", "arbitrary"))(a, b)


def _matmul_nn(a, b, name, tn):
    s_len, k = a.shape
    n = b.shape[1]
    ts = min(512, s_len)

    def body(a_ref, b_ref, o_ref):
        o_ref[...] = _bdot(a_ref[...], b_ref[...], NN)

    return _call(
        body, name, (s_len // ts, n // tn),
        [pl.BlockSpec((ts, k), lambda i, j: (i, 0)), pl.BlockSpec((k, tn), lambda i, j: (0, j))],
        pl.BlockSpec((ts, tn), lambda i, j: (i, j)), _sds((s_len, n)), sem=("parallel", "parallel"))(a, b)


def _matmul_nt_add(a, b, res, name):
    s_len, k = a.shape
    n = b.shape[0]
    ts = min(512, s_len)

    def body(a_ref, b_ref, r_ref, o_ref):
        o_ref[...] = r_ref[...] + _bdot(a_ref[...], b_ref[...], NT)

    return _call(
        body, name, (s_len // ts,),
        [pl.BlockSpec((ts, k), lambda i: (i, 0)), _full((n, k)), pl.BlockSpec((ts, n), lambda i: (i, 0))],
        pl.BlockSpec((ts, n), lambda i: (i, 0)), _sds((s_len, n)), sem=("parallel",))(a, b, res)


def _act_qkv(y):
    qkv = _silu(y)
    qs, ks = [], []
    for h in range(HEADS):
        qh = qkv[:, h * HEAD_DIM:(h + 1) * HEAD_DIM]
        kh = qkv[:, D_MODEL + h * HEAD_DIM:D_MODEL + (h + 1) * HEAD_DIM]
        qs.append(qh * (lax.rsqrt(jnp.sum(qh * qh, axis=-1, keepdims=True) + L2_EPS) * HEAD_DIM ** -0.5))
        ks.append(kh * lax.rsqrt(jnp.sum(kh * kh, axis=-1, keepdims=True) + L2_EPS))
    return jnp.concatenate(qs, axis=1), jnp.concatenate(ks, axis=1), qkv[:, 2 * D_MODEL:]


def _act_gb(ba, alog_l, dtb_l, tril):
    lane = lax.broadcasted_iota(jnp.int32, ba.shape, 1)
    g = jnp.where((lane >= HEADS) & (lane < 2 * HEADS), -jnp.exp(alog_l) * _softplus(ba + dtb_l), 0.0)
    return jnp.where(lane < HEADS, _sigmoid(ba), _hdot(tril, g, NN))


def _chunk_tril(t):
    ii = lax.broadcasted_iota(jnp.int32, (t, t), 0)
    jj = lax.broadcasted_iota(jnp.int32, (t, t), 1)
    return ((ii // CHUNK == jj // CHUNK) & (ii >= jj)).astype(F32)


def _conv_rows(xe, w, n_rows):
    y = xe[CONV_HALO:CONV_HALO + n_rows] * w[CONV_WIDTH - 1]
    for j in range(CONV_WIDTH - 1):
        y = y + pltpu.roll(xe, CONV_WIDTH - 1 - j, 0)[CONV_HALO:CONV_HALO + n_rows] * w[j]
    return y


def _conv_fwd(proj, conv_w, alog_l, dtb_l):
    s_len = proj.shape[0]
    ts = min(CONV_TILE, s_len)
    hb = ts // CONV_HALO

    def body(x_ref, xp_ref, ba_ref, w_ref, al_ref, dt_ref, q_ref, k_ref, v_ref, gcb_ref):
        i = pl.program_id(0)
        halo = jnp.where(i > 0, xp_ref[...], 0.0)
        taps = [w_ref[pl.ds(j, 1), :] for j in range(CONV_WIDTH)]
        y = _conv_rows(jnp.concatenate([halo, x_ref[...]], axis=0), taps, ts)
        q_ref[...], k_ref[...], v_ref[...] = _act_qkv(y)
        gcb_ref[...] = _act_gb(ba_ref[...], al_ref[...], dt_ref[...], _chunk_tril(ts))

    tile = pl.BlockSpec((ts, D_MODEL), lambda i: (i, 0))
    return _call(
        body, "gdn_conv_fwd", (s_len // ts,),
        [pl.BlockSpec((ts, QKV_DIM), lambda i: (i, 0)),
         pl.BlockSpec((CONV_HALO, QKV_DIM), lambda i: (jnp.maximum(i * hb - 1, 0), 0)),
         pl.BlockSpec((ts, 128), lambda i: (i, BA_BLOCK)), _full((CONV_WIDTH, QKV_DIM)), _row(128), _row(128)],
        [tile, tile, tile, pl.BlockSpec((ts, 128), lambda i: (i, 0))],
        [_sds((s_len, D_MODEL))] * 3 + [_sds((s_len, 128))],
        sem=("parallel",))(proj, proj, proj, conv_w, alog_l, dtb_l)


def _conv_bwd(proj, dq, dk, dv, dgcb, dz, conv_w, alog_l, dtb_l):
    s_len = proj.shape[0]
    ts = min(CONV_TILE, s_len)
    hb = ts // CONV_HALO
    n_t = s_len // ts
    te = ts + CONV_HALO

    def body(x_ref, xp_ref, xn_ref, ba_ref, dq_ref, dqn_ref, dk_ref, dkn_ref, dv_ref, dvn_ref, dgcb_ref, dz_ref,
             w_ref, al_ref, dt_ref, dp_ref, dw_ref, dal_ref, ddt_ref):
        i = pl.program_id(0)
        more = i < n_t - 1
        w = [w_ref[pl.ds(j, 1), :] for j in range(CONV_WIDTH)]
        x_t = x_ref[...]
        xe = jnp.concatenate([jnp.where(i > 0, xp_ref[...], 0.0), x_t, xn_ref[...]], axis=0)
        y_e, act_vjp = jax.vjp(_act_qkv, _conv_rows(xe, w, te))
        ct = tuple(jnp.concatenate([t[...], jnp.where(more, n[...], 0.0)], axis=0)
                   for t, n in ((dq_ref, dqn_ref), (dk_ref, dkn_ref), (dv_ref, dvn_ref)))
        (dy_e,) = act_vjp(ct)
        ahead = [pltpu.roll(dy_e, te - (CONV_WIDTH - 1 - j), 0)[:ts] for j in range(CONV_WIDTH - 1)] + [dy_e[:ts]]
        dx = ahead[CONV_WIDTH - 1] * w[CONV_WIDTH - 1]
        for j in range(CONV_WIDTH - 1):
            dx = dx + ahead[j] * w[j]
        dws = [jnp.sum(ahead[j] * x_t, axis=0, keepdims=True) for j in range(CONV_WIDTH)]
        _, gb_vjp = jax.vjp(lambda ba, al, dt: _act_gb(ba, al, dt, _chunk_tril(ts)), ba_ref[...], al_ref[...], dt_ref[...])
        dba, dal, ddt = gb_vjp(dgcb_ref[...])
        dp_ref[...] = jnp.concatenate([dx.astype(BF16), dz_ref[...], dba.astype(BF16)], axis=1)
        first = i == 0
        for j in range(CONV_WIDTH):
            _acc(dw_ref.at[pl.ds(j, 1), :], first, dws[j])
        _acc(dal_ref, first, dal)
        _acc(ddt_ref, first, ddt)

    tile = pl.BlockSpec((ts, D_MODEL), lambda i: (i, 0))
    nxt = pl.BlockSpec((CONV_HALO, D_MODEL), lambda i: (jnp.minimum((i + 1) * hb, n_t * hb - 1), 0))
    return _call(
        body, "gdn_conv_bwd", (n_t,),
        [pl.BlockSpec((ts, QKV_DIM), lambda i: (i, 0)),
         pl.BlockSpec((CONV_HALO, QKV_DIM), lambda i: (jnp.maximum(i * hb - 1, 0), 0)),
         pl.BlockSpec((CONV_HALO, QKV_DIM), lambda i: (jnp.minimum((i + 1) * hb, n_t * hb - 1), 0)),
         pl.BlockSpec((ts, 128), lambda i: (i, BA_BLOCK)),
         tile, nxt, tile, nxt, tile, nxt, pl.BlockSpec((ts, 128), lambda i: (i, 0)), tile,
         _full((CONV_WIDTH, QKV_DIM)), _row(128), _row(128)],
        [pl.BlockSpec((ts, GDN_IN_PAD), lambda i: (i, 0)), _full((CONV_WIDTH, QKV_DIM)), _row(128), _row(128)],
        [_sds((s_len, GDN_IN_PAD), BF16), _sds((CONV_WIDTH, QKV_DIM)), _sds((1, 128)), _sds((1, 128))],
        sem=("arbitrary",))(proj, proj, proj, proj, dq, dq, dk, dk, dv, dv, dgcb, dz, conv_w, alog_l, dtb_l)


def _tri_inv(a_strict):
    ii = lax.broadcasted_iota(jnp.int32, (CHUNK, CHUNK), 0)
    jj = lax.broadcasted_iota(jnp.int32, (CHUNK, CHUNK), 1)
    x = (ii == jj).astype(F32) - a_strict
    pw = _mdot(a_strict, a_strict, BNN)
    for step in range(5):
        x = x + _mdot(x, pw, BNN)
        if step < 4:
            pw = _mdot(pw, pw, BNN)
    return x


@jax.custom_vjp
def _solved(a_strict, rhs, t, sol):
    return sol


def _solved_fwd(a_strict, rhs, t, sol):
    return sol, (t, sol)


def _solved_bwd(res, dsol):
    t, sol = res
    drhs = _mdot(t, dsol, BTN)
    return -_mdot(drhs, sol, BNT), drhs, jnp.zeros_like(t), jnp.zeros_like(sol)


_solved.defvjp(_solved_fwd, _solved_bwd)


def _prep(q, k, v, gc, beta, solve):
    ii = lax.broadcasted_iota(jnp.int32, (CHUNK, CHUNK), 0)
    jj = lax.broadcasted_iota(jnp.int32, (CHUNK, CHUNK), 1)
    causal, strict = ii >= jj, ii > jj
    gc_row = jnp.sum((ii == jj).astype(F32) * gc, axis=1, keepdims=True)
    decay = jnp.where(causal, jnp.exp(jnp.where(causal, gc - gc_row, 0.0)), 0.0)
    kb = k * beta
    a = jnp.where(strict, _bdot(kb, k, BNT) * decay, 0.0)
    eg = jnp.exp(gc)
    sol = solve(a, jnp.concatenate([v * beta, kb * eg], axis=-1))
    qk = _bdot(q, k, BNT) * decay
    last = lax.broadcasted_iota(jnp.int32, (CHUNK, 1), 0) == CHUNK - 1
    g_last = jnp.sum(jnp.where(last, gc, 0.0), axis=1, keepdims=True)
    kd = k * jnp.exp(g_last - gc)
    gl = jnp.exp(g_last) + jnp.zeros((1, 1, HEAD_DIM), F32)
    return sol[..., :HEAD_DIM], sol[..., HEAD_DIM:], qk, q * eg, kd, gl


def _prep_specs(s_len):
    rows = min(PREP_CHUNKS, s_len // CHUNK) * CHUNK
    m = rows // CHUNK
    hd = pl.BlockSpec((rows, HEAD_DIM), lambda c, h: (c, h))
    gcb = pl.BlockSpec((rows, 128), lambda c, h: (c, 0))
    qk = pl.BlockSpec((None, rows, CHUNK), lambda c, h: (h, c, 0))
    gl = pl.BlockSpec((None, m, HEADS, HEAD_DIM), lambda c, h: (c, 0, 0, 0))
    return rows, m, hd, gcb, qk, gl


def _head_cols(gcb, h, m):
    lane = lax.broadcasted_iota(jnp.int32, gcb.shape, 1)
    pick = lambda at: jnp.sum(jnp.where(lane == at, gcb, 0.0), axis=1, keepdims=True).reshape(m, CHUNK, 1)
    return pick(h + HEADS), pick(h)


def _gdn_prep(q, k, v, gcb):
    s_len = q.shape[0]
    rows, m, hd, gcb_spec, qk_spec, gl_spec = _prep_specs(s_len)

    def body(q_ref, k_ref, v_ref, gcb_ref, u_ref, w_ref, qd_ref, kd_ref, qk_ref, gl_ref, t_ref):
        r3 = lambda ref, d: ref[...].reshape(m, CHUNK, d)
        gc, beta = _head_cols(gcb_ref[...], pl.program_id(1), m)

        def solve(a, rhs):
            t = _tri_inv(a)
            t_ref[...] = t.reshape(rows, CHUNK)
            return _mdot(t, rhs, BNN)

        u, w, qk, qd, kd, gl = _prep(r3(q_ref, HEAD_DIM), r3(k_ref, HEAD_DIM), r3(v_ref, HEAD_DIM), gc, beta, solve)
        u_ref[...] = u.reshape(rows, HEAD_DIM)
        w_ref[...] = w.reshape(rows, HEAD_DIM)
        qd_ref[...] = qd.reshape(rows, HEAD_DIM).astype(BF16)
        kd_ref[...] = kd.reshape(rows, HEAD_DIM).astype(BF16)
        qk_ref[...] = qk.reshape(rows, CHUNK).astype(BF16)
        gl_ref[:, pl.ds(pl.program_id(1), 1), :] = gl

    n_g = s_len // rows
    return _call(
        body, "gdn_prep", (n_g, HEADS), [hd, hd, hd, gcb_spec], [hd, hd, hd, hd, qk_spec, gl_spec, qk_spec],
        [_sds((s_len, D_MODEL))] * 2 + [_sds((s_len, D_MODEL), BF16)] * 2
        + [_sds((HEADS, s_len, CHUNK), BF16), _sds((n_g, m, HEADS, HEAD_DIM)), _sds((HEADS, s_len, CHUNK))],
        sem=("parallel", "arbitrary"))(q, k, v, gcb)


def _gdn_prep_bwd(q, k, v, gcb, t_inv, u, w, du, dw, dqd, dkd, dqk, dgl):
    s_len = q.shape[0]
    rows, m, hd, gcb_spec, qk_spec, gl_spec = _prep_specs(s_len)

    def body(q_ref, k_ref, v_ref, gcb_ref, t_ref, u_ref, w_ref, du_ref, dw_ref, dqd_ref, dkd_ref, dqk_ref, dgl_ref,
             dq_ref, dk_ref, dv_ref, dgcb_ref):
        h = pl.program_id(1)
        r3 = lambda ref, d: ref[...].reshape(m, CHUNK, d)
        gc, beta = _head_cols(gcb_ref[...], h, m)
        t = r3(t_ref, CHUNK)
        sol = jnp.concatenate([r3(u_ref, HEAD_DIM), r3(w_ref, HEAD_DIM)], axis=-1)
        fn = lambda q_, k_, v_, gc_, bt_: _prep(q_, k_, v_, gc_, bt_, lambda a, rhs: _solved(a, rhs, t, sol))
        _, vjp = jax.vjp(fn, r3(q_ref, HEAD_DIM), r3(k_ref, HEAD_DIM), r3(v_ref, HEAD_DIM), gc, beta)
        ct = (r3(du_ref, HEAD_DIM), r3(dw_ref, HEAD_DIM), r3(dqk_ref, CHUNK), r3(dqd_ref, HEAD_DIM), r3(dkd_ref, HEAD_DIM),
              dgl_ref[:, pl.ds(h, 1), :] * (1.0 / HEAD_DIM))
        dq, dk, dv, dgc, dbt = vjp(ct)
        dq_ref[...] = dq.reshape(rows, HEAD_DIM)
        dk_ref[...] = dk.reshape(rows, HEAD_DIM)
        dv_ref[...] = dv.reshape(rows, HEAD_DIM)
        lane = lax.broadcasted_iota(jnp.int32, (rows, 128), 1)
        mine = jnp.where(lane == h, dbt.reshape(rows, 1), 0.0) + jnp.where(lane == h + HEADS, dgc.reshape(rows, 1), 0.0)
        _acc(dgcb_ref, h == 0, mine)

    return _call(
        body, "gdn_prep_bwd", (s_len // rows, HEADS),
        [hd, hd, hd, gcb_spec, qk_spec, hd, hd, hd, hd, hd, hd, qk_spec, gl_spec], [hd, hd, hd, gcb_spec],
        [_sds((s_len, D_MODEL))] * 3 + [_sds((s_len, 128))],
        sem=("parallel", "arbitrary"))(q, k, v, gcb, t_inv, u, w, du, dw, dqd, dkd, dqk, dgl)


def _scan_specs(n_c, m, reverse):
    at = (lambda n: n_c - 1 - n) if reverse else (lambda n: n)
    row = pl.BlockSpec((CHUNK, D_MODEL), lambda n: (at(n), 0))
    qk = pl.BlockSpec((HEADS, CHUNK, CHUNK), lambda n: (0, at(n), 0))
    gl = pl.BlockSpec((None, None, HEADS, HEAD_DIM), lambda n: (at(n) // m, at(n) % m, 0, 0))
    st = pl.BlockSpec((1, HEADS, HEAD_DIM, HEAD_DIM), lambda n: (at(n), 0, 0, 0))
    return row, qk, gl, st


def _gdn_scan(u, w, qd, kd, qk, gl):
    s_len = u.shape[0]
    n_c = s_len // CHUNK
    row, qk_spec, gl_spec, st_spec = _scan_specs(n_c, gl.shape[1], False)

    def body(u_ref, w_ref, qd_ref, kd_ref, qk_ref, gl_ref, o_ref, st_ref, state):
        hs = range(HEADS)
        sl = [slice(h * HEAD_DIM, (h + 1) * HEAD_DIM) for h in hs]
        first = pl.program_id(0) == 0
        s_all = [jnp.where(first, 0.0, state[h]) for h in hs]
        s_b = [s.astype(BF16) for s in s_all]
        ws = [_bdot(w_ref[:, sl[h]], s_b[h], NN) for h in hs]
        qs = [_bdot(qd_ref[:, sl[h]], s_b[h], NN) for h in hs]
        vn = [(u_ref[:, sl[h]] - ws[h]).astype(BF16) for h in hs]
        outs = [qs[h] + _bdot(qk_ref[h], vn[h], NN) for h in hs]
        nxt = [s_all[h] * gl_ref[pl.ds(h, 1), :] + _bdot(kd_ref[:, sl[h]], vn[h], TN) for h in hs]
        for h in hs:
            st_ref[0, h] = s_all[h]
            o_ref[:, sl[h]] = outs[h]
            state[h] = nxt[h]

    return _call(
        body, "gdn_scan", (n_c,), [row, row, row, row, qk_spec, gl_spec], [row, st_spec],
        [_sds((s_len, D_MODEL)), _sds((n_c, HEADS, HEAD_DIM, HEAD_DIM))],
        scratch=[pltpu.VMEM((HEADS, HEAD_DIM, HEAD_DIM), F32)], sem=("arbitrary",))(u, w, qd, kd, qk, gl)


def _gdn_scan_bwd(do, u, w, qd, kd, qk, gl, states):
    s_len = u.shape[0]
    n_c = s_len // CHUNK
    row, qk_spec, gl_spec, st_spec = _scan_specs(n_c, gl.shape[1], True)

    def body(do_ref, u_ref, w_ref, qd_ref, kd_ref, qk_ref, gl_ref, st_ref,
             du_ref, dw_ref, dqd_ref, dkd_ref, dqk_ref, dgl_ref, dstate):
        hs = range(HEADS)
        sl = [slice(h * HEAD_DIM, (h + 1) * HEAD_DIM) for h in hs]
        first = pl.program_id(0) == 0
        s_f = [st_ref[0, h] for h in hs]
        s_b = [s.astype(BF16) for s in s_f]
        ds_f = [jnp.where(first, 0.0, dstate[h]) for h in hs]
        ds_b = [d.astype(BF16) for d in ds_f]
        do_b = [do_ref[:, sl[h]].astype(BF16) for h in hs]
        w_b = [w_ref[:, sl[h]].astype(BF16) for h in hs]
        ws = [_bdot(w_b[h], s_b[h], NN) for h in hs]
        dvn = [_bdot(qk_ref[h], do_b[h], TN) + _bdot(kd_ref[:, sl[h]], ds_b[h], NN) for h in hs]
        dqd = [_bdot(do_b[h], s_b[h], NT) for h in hs]
        t_do = [_bdot(qd_ref[:, sl[h]], do_b[h], TN) for h in hs]
        vn = [(u_ref[:, sl[h]] - ws[h]).astype(BF16) for h in hs]
        dvn_b = [d.astype(BF16) for d in dvn]
        dw = [-_bdot(dvn_b[h], s_b[h], NT) for h in hs]
        dkd = [_bdot(vn[h], ds_b[h], NT) for h in hs]
        dqk = [_bdot(do_b[h], vn[h], NT) for h in hs]
        t_dv = [_bdot(w_b[h], dvn_b[h], TN) for h in hs]
        for h in hs:
            du_ref[:, sl[h]], dw_ref[:, sl[h]], dqd_ref[:, sl[h]], dkd_ref[:, sl[h]] = dvn[h], dw[h], dqd[h], dkd[h]
            dqk_ref[h] = dqk[h]
            dgl_ref[pl.ds(h, 1), :] = jnp.sum(s_f[h] * ds_f[h]) + jnp.zeros((1, HEAD_DIM), F32)
            dstate[h] = ds_f[h] * gl_ref[pl.ds(h, 1), :] + t_do[h] - t_dv[h]

    return _call(
        body, "gdn_scan_bwd", (n_c,), [row, row, row, row, row, qk_spec, gl_spec, st_spec],
        [row, row, row, row, qk_spec, gl_spec],
        [_sds((s_len, D_MODEL))] * 4 + [_sds((HEADS, s_len, CHUNK)), _sds(gl.shape)],
        scratch=[pltpu.VMEM((HEADS, HEAD_DIM, HEAD_DIM), F32)], sem=("arbitrary",))(do, u, w, qd, kd, qk, gl, states)


def _gate_norm(o, z, nw):
    outs = []
    for h in range(HEADS):
        oh = o[:, h * HEAD_DIM:(h + 1) * HEAD_DIM]
        outs.append(oh * lax.rsqrt(jnp.mean(oh * oh, axis=-1, keepdims=True) + RMS_EPS))
    return jnp.concatenate(outs, axis=1) * nw * _silu(z)


def _gdn_out_fwd(o, proj, x, w_out, nw, gain, bias):
    s_len = x.shape[0]
    ts = min(ROW_TILE, s_len)

    def body(o_ref, z_ref, x_ref, w_ref, nw_ref, g_ref, b_ref, y_ref, xhat_ref, rstd_ref):
        on = _gate_norm(o_ref[...], z_ref[...], nw_ref[...])
        r = ALPHA * x_ref[...] + _bdot(on, w_ref[...], NN)
        y_ref[...], xhat_ref[...], rstd_ref[...] = _ln_fwd(r, g_ref[...], b_ref[...])

    tile = pl.BlockSpec((ts, D_MODEL), lambda i: (i, 0))
    row = _row(D_MODEL)
    return _call(
        body, "gdn_out_fwd", (s_len // ts,),
        [tile, pl.BlockSpec((ts, D_MODEL), lambda i: (i, QKV_DIM // D_MODEL)), tile, _full((D_MODEL, D_MODEL)), row, row, row],
        [tile, tile, pl.BlockSpec((ts, 1), lambda i: (i, 0))],
        [_sds((s_len, D_MODEL)), _sds((s_len, D_MODEL)), _sds((s_len, 1))], sem=("parallel",))(o, proj, x, w_out, nw, gain, bias)


def _gdn_out_bwd(dy, xhat, rstd, o, proj, w_out, nw, gain):
    s_len = o.shape[0]
    ts = min(ROW_TILE, s_len)

    def body(dy_ref, xh_ref, rs_ref, o_ref, z_ref, w_ref, nw_ref, g_ref,
             dres_ref, do_ref, dz_ref, on_ref, drb_ref, dg_ref, db_ref, dnw_ref):
        dy_t, xh_t = dy_ref[...], xh_ref[...]
        dr = _ln_bwd(dy_t, xh_t, rs_ref[...], g_ref[...])
        dres_ref[...] = ALPHA * dr
        drb_ref[...] = dr.astype(BF16)
        on, vjp = jax.vjp(_gate_norm, o_ref[...], z_ref[...], nw_ref[...])
        on_ref[...] = on.astype(BF16)
        do, dz, dnw = vjp(_bdot(dr, w_ref[...], NT))
        do_ref[...] = do
        dz_ref[...] = dz.astype(BF16)
        first = pl.program_id(0) == 0
        _acc(dg_ref, first, jnp.sum(dy_t * xh_t, axis=0, keepdims=True))
        _acc(db_ref, first, jnp.sum(dy_t, axis=0, keepdims=True))
        _acc(dnw_ref, first, sum(dnw[:, h * HEAD_DIM:(h + 1) * HEAD_DIM] for h in range(HEADS)))

    tile = pl.BlockSpec((ts, D_MODEL), lambda i: (i, 0))
    row = _row(D_MODEL)
    return _call(
        body, "gdn_out_bwd", (s_len // ts,),
        [tile, tile, pl.BlockSpec((ts, 1), lambda i: (i, 0)), tile,
         pl.BlockSpec((ts, D_MODEL), lambda i: (i, QKV_DIM // D_MODEL)), _full((D_MODEL, D_MODEL)), row, row],
        [tile, tile, tile, tile, tile, row, row, _row(HEAD_DIM)],
        [_sds((s_len, D_MODEL))] * 2 + [_sds((s_len, D_MODEL), BF16)] * 3 + [_sds((1, D_MODEL))] * 2 + [_sds((1, HEAD_DIM))],
        sem=("arbitrary",))(dy, xhat, rstd, o, proj, w_out, nw, gain)


def _loss_head(y, target):
    s_len = y.shape[0]
    ts = min(ROW_TILE, s_len)

    def body(y_ref, t_ref, dy_ref, l_ref):
        err = y_ref[...] - t_ref[...]
        dy_ref[...] = err * (1.0 / D_MODEL)
        part = 0.5 * jnp.sum(jnp.mean(err * err, axis=-1, keepdims=True))
        _acc(l_ref, pl.program_id(0) == 0, part + jnp.zeros((1, 128), F32))

    tile = pl.BlockSpec((ts, D_MODEL), lambda i: (i, 0))
    return _call(body, "loss_head", (s_len // ts,), [tile, tile], [tile, _row(128)],
                 [_sds((s_len, D_MODEL)), _sds((1, 128))], sem=("arbitrary",))(y, target)


def _adamw(w, g, m, v, name):
    r, c = w.shape
    tr = min(ADAM_ROWS, r)

    def body(w_ref, g_ref, m_ref, v_ref, d_ref, nm_ref, nv_ref):
        g_t = g_ref[...]
        nm = ADAM_B1 * m_ref[...] + (1.0 - ADAM_B1) * g_t
        nv = ADAM_B2 * v_ref[...] + (1.0 - ADAM_B2) * (g_t * g_t)
        m_hat = nm / (1.0 - ADAM_B1 ** ADAM_STEP)
        v_hat = nv / (1.0 - ADAM_B2 ** ADAM_STEP)
        d_ref[...] = -ADAM_LR * (m_hat / (jnp.sqrt(v_hat) + ADAM_EPS) + ADAM_WD * w_ref[...])
        nm_ref[...] = nm
        nv_ref[...] = nv

    tile = pl.BlockSpec((tr, c), lambda i: (i, 0))
    return _call(body, name, (r // tr,), [tile] * 4, [tile] * 3, [_sds((r, c))] * 3, sem=("parallel",))(w, g, m, v)


def _assemble_w_in(shards):
    rows = 256
    width = GDN_IN_DIM // N_SHARD

    def body(s_ref, o_ref):
        pad = jnp.zeros((rows, GDN_IN_PAD - GDN_IN_DIM), shards.dtype)
        o_ref[...] = jnp.concatenate([s_ref[j] for j in range(N_SHARD)] + [pad], axis=1)

    return _call(body, "w_in_assemble", (D_MODEL // rows,), [pl.BlockSpec((N_SHARD, rows, width), lambda i: (0, i, 0))],
                 pl.BlockSpec((rows, GDN_IN_PAD), lambda i: (i, 0)), _sds((D_MODEL, GDN_IN_PAD), shards.dtype),
                 sem=("parallel",))(shards)


def _split_w_in(full):
    rows = 256
    width = GDN_IN_DIM // N_SHARD

    def body(f_ref, o_ref):
        f = f_ref[...]
        for j in range(N_SHARD):
            o_ref[j] = f[:, j * width:(j + 1) * width]

    return _call(body, "w_in_split", (D_MODEL // rows,), [pl.BlockSpec((rows, GDN_IN_PAD), lambda i: (i, 0))],
                 pl.BlockSpec((N_SHARD, rows, width), lambda i: (0, i, 0)), _sds((N_SHARD, D_MODEL, width), full.dtype),
                 sem=("parallel",))(full)


def _place():
    x, y, c = lax.axis_index("x"), lax.axis_index("y"), lax.axis_index("c")
    return x, y, c, [(1 - x, y), (x, 1 - y), (1 - x, 1 - y)]


def _row_tile(rows):
    return max(t for t in range(8, min(rows, 640) + 1, 8) if rows % t == 0)


def _place_shard(part, me, dtype, name, layer=0):
    _, _, r, c = part.shape
    tr = _row_tile(r)

    def body(me_ref, p_ref, o_ref):
        o_ref[...] = p_ref[...].astype(dtype)

    return pl.pallas_call(
        body, name=name, out_shape=_sds((N_SHARD, 2, r, c), dtype),
        grid_spec=pltpu.PrefetchScalarGridSpec(
            num_scalar_prefetch=1, grid=(2, r // tr),
            in_specs=[pl.BlockSpec((None, None, tr, c), lambda h, i, me_ref: (layer, h, i, 0))],
            out_specs=pl.BlockSpec((None, None, tr, c), lambda h, i, me_ref: (me_ref[0], h, i, 0))))(me, part)


def _gather_sems(n):
    return [pltpu.SemaphoreType.DMA((6 * n,)), pltpu.SemaphoreType.DMA((6 * n,))]


def _gather_steps(dsts, send_sems, recv_sems):
    n = len(dsts)
    x, y, c, chips = _place()
    me = 2 * x + y
    sibling = (x, y, 1 - c)

    def ici(k, j, slot):
        px, py = chips[j]
        view = dsts[k].at[slot, c]
        return pltpu.make_async_remote_copy(
            src_ref=view, dst_ref=view, send_sem=send_sems.at[6 * k + j],
            recv_sem=recv_sems.at[6 * k + j], device_id=(px, py, c), device_id_type=MESH)

    def d2d(k, j, half):
        px, py = chips[j]
        view = dsts[k].at[2 * px + py, half]
        return pltpu.make_async_remote_copy(
            src_ref=view, dst_ref=view, send_sem=send_sems.at[6 * k + 3 + j], recv_sem=recv_sems.at[6 * k + 3 + j],
            device_id=sibling, device_id_type=MESH)

    def start():
        for k in range(n):
            for j in range(3):
                ici(k, j, me).start()

    def finish():
        fwds = []
        for k in range(n):
            for j, (px, py) in enumerate(chips):
                ici(k, j, 2 * px + py).wait_recv()
                fwds.append(d2d(k, j, c))
                fwds[-1].start()
        for k in range(n):
            for j in range(3):
                d2d(k, j, 1 - c).wait_recv()
        for k in range(n):
            for j in range(3):
                ici(k, j, me).wait_send()
        for cp in fwds:
            cp.wait_send()

    return start, finish


def _all_gather(bufs, name):
    n = len(bufs)

    def body(*refs):
        start, finish = _gather_steps(refs[n:2 * n], *refs[2 * n:])
        start()
        finish()

    return pl.pallas_call(
        body, name=name, out_shape=[_sds(a.shape, a.dtype) for a in bufs],
        in_specs=[ANY] * n, out_specs=[ANY] * n, input_output_aliases={k: k for k in range(n)},
        scratch_shapes=_gather_sems(n))(*bufs)


def _swap_halves(pieces, name):
    n = len(pieces)

    def body(*refs):
        srcs, dsts = refs[:n], refs[n:2 * n]
        send_sems, recv_sems = refs[2 * n:]
        x, y, c, _ = _place()
        copies = []
        for k in range(n):
            hr = pieces[k].shape[1] // 2
            copies.append(pltpu.make_async_remote_copy(
                src_ref=srcs[k].at[:, pl.ds((1 - c) * hr, hr), :], dst_ref=dsts[k],
                send_sem=send_sems.at[k], recv_sem=recv_sems.at[k], device_id=(x, y, 1 - c), device_id_type=MESH))
        for cp in copies:
            cp.start()
        for cp in copies:
            cp.wait()

    return pl.pallas_call(
        body, name=name, out_shape=[_sds((N_SHARD, a.shape[1] // 2, a.shape[2])) for a in pieces],
        in_specs=[ANY] * n, out_specs=[ANY] * n,
        scratch_shapes=[pltpu.SemaphoreType.DMA((n,)), pltpu.SemaphoreType.DMA((n,))])(*pieces)


def _add_half(piece, other, place, dtype, name):
    n, hr, cols = other.shape
    tr = _row_tile(hr)

    def body(pl_ref, a_ref, b_ref, o_ref):
        o_ref[...] = (a_ref[...] + b_ref[...]).astype(dtype)

    tile = pl.BlockSpec((None, tr, cols), lambda s, i, pl_ref: (s, i, 0))
    return pl.pallas_call(
        body, name=name, out_shape=_sds(other.shape, dtype),
        grid_spec=pltpu.PrefetchScalarGridSpec(
            num_scalar_prefetch=1, grid=(n, hr // tr),
            in_specs=[pl.BlockSpec((None, None, tr, cols), lambda s, i, pl_ref: (s, pl_ref[1], i, 0)), tile],
            out_specs=tile))(place, piece.reshape(n, 2, hr, cols), other)


def _scatter_sems(n):
    return [pltpu.SemaphoreType.DMA((3 * n,)), pltpu.SemaphoreType.DMA((3 * n,))]


def _scatter_steps(srcs, dsts, send_sems, recv_sems):
    n = len(srcs)
    x, y, c, chips = _place()
    me = 2 * x + y

    def ici(k, j, src_slot, dst_slot):
        px, py = chips[j]
        return pltpu.make_async_remote_copy(
            src_ref=srcs[k].at[src_slot], dst_ref=dsts[k].at[dst_slot], send_sem=send_sems.at[3 * k + j],
            recv_sem=recv_sems.at[3 * k + j], device_id=(px, py, c), device_id_type=MESH)

    def start():
        for k in range(n):
            for j, (px, py) in enumerate(chips):
                ici(k, j, 2 * px + py, me).start()

    def finish():
        for k in range(n):
            for j, (px, py) in enumerate(chips):
                ici(k, j, me, 2 * px + py).wait_recv()
        for k in range(n):
            for j, (px, py) in enumerate(chips):
                ici(k, j, 2 * px + py, me).wait_send()

    return start, finish


def _scatter_chips(parts, name):
    n = len(parts)

    def body(*refs):
        start, finish = _scatter_steps(refs[:n], refs[n:2 * n], *refs[2 * n:])
        start()
        finish()

    return pl.pallas_call(
        body, name=name, out_shape=[_sds(a.shape, a.dtype) for a in parts],
        in_specs=[ANY] * n, out_specs=[ANY] * n, scratch_shapes=_scatter_sems(n))(*parts)


def _sum_chips(landed, own, place, name):
    _, r, cols = landed.shape
    tr = _row_tile(r)

    def body(pl_ref, q_ref, p_ref, o_ref):
        me = pl_ref[0]
        f = lambda j: jnp.where(me == j, p_ref[...], q_ref[j]).astype(F32)
        o_ref[...] = ((f(0) + f(1)) + f(2)) + f(3)

    return pl.pallas_call(
        body, name=name, out_shape=_sds((2, r, cols)),
        grid_spec=pltpu.PrefetchScalarGridSpec(
            num_scalar_prefetch=1, grid=(r // tr,),
            in_specs=[pl.BlockSpec((N_SHARD, tr, cols), lambda i, pl_ref: (0, i, 0)),
                      pl.BlockSpec((None, tr, cols), lambda i, pl_ref: (pl_ref[0], i, 0))],
            out_specs=pl.BlockSpec((None, tr, cols), lambda i, pl_ref: (pl_ref[1], i, 0))))(place, landed, own)


def _join_halves(bufs):
    n = len(bufs)

    def body(*refs):
        dsts = refs[n:2 * n]
        send_sems, recv_sems = refs[2 * n:]
        x, y, c, _ = _place()
        copies = [pltpu.make_async_remote_copy(
            src_ref=dsts[k].at[c], dst_ref=dsts[k].at[c], send_sem=send_sems.at[k], recv_sem=recv_sems.at[k],
            device_id=(x, y, 1 - c), device_id_type=MESH) for k in range(n)]
        for cp in copies:
            cp.start()
        for cp in copies:
            cp.wait()

    return pl.pallas_call(
        body, name="grads_join_halves", out_shape=[_sds(a.shape) for a in bufs], in_specs=[ANY] * n, out_specs=[ANY] * n,
        input_output_aliases={k: k for k in range(n)},
        scratch_shapes=[pltpu.SemaphoreType.DMA((n,)), pltpu.SemaphoreType.DMA((n,))])(*bufs)


GATHER_F32 = ("ln_gain", "ln_bias", "pool_b", "gdn_conv")
REPLICATED = ("pool_scale", "gdn_a_log", "gdn_dt_bias", "gdn_norm_w", "ple_gate_b")
WEIGHTS = ("ln_gain", "ln_bias", "pool_w", "pool_b", "pool_scale", "gdn_w_in", "gdn_conv", "gdn_a_log", "gdn_dt_bias",
           "gdn_norm_w", "gdn_w_out", "mlp_w1", "mlp_w2", "ple_gate_w", "ple_gate_b", "ple_proj")
SMALL_GRADS = ("ple_proj", "pool_w", "ln_gain", "ln_bias", "pool_b", "gdn_conv") + REPLICATED


def _pack(parts, lanes, row_multiple):
    flat = jnp.concatenate([a.reshape(-1) for a in parts])
    rows = -(-flat.shape[0] // (2 * lanes * row_multiple)) * row_multiple
    return jnp.pad(flat, (0, 2 * rows * lanes - flat.shape[0])).reshape(2, rows, lanes)


def _unpack(flat, shapes):
    out, off = [], 0
    for shp in shapes:
        n = math.prod(shp)
        out.append(flat[..., off:off + n].reshape(flat.shape[:-1] + tuple(shp)))
        off += n
    return out


def _pad_lanes(a, offset, width=128):
    return jnp.pad(a, ((0, 0), (offset, width - offset - a.shape[1])))


def kernel(x, p, ln_gain, ln_bias, pool_w, pool_b, pool_scale, gdn_w_in, gdn_conv, gdn_a_log, gdn_dt_bias, gdn_norm_w, gdn_w_out, mlp_w1, mlp_w2, ple_gate_w, ple_gate_b, ple_proj, loss_target, m_ln_gain, m_ln_bias, m_pool_w, m_pool_b, m_pool_scale, m_gdn_w_in, m_gdn_conv, m_gdn_a_log, m_gdn_dt_bias, m_gdn_norm_w, m_gdn_w_out, m_mlp_w1, m_mlp_w2, m_ple_gate_w, m_ple_gate_b, m_ple_proj, v_ln_gain, v_ln_bias, v_pool_w, v_pool_b, v_pool_scale, v_gdn_w_in, v_gdn_conv, v_gdn_a_log, v_gdn_dt_bias, v_gdn_norm_w, v_gdn_w_out, v_mlp_w1, v_mlp_w2, v_ple_gate_w, v_ple_gate_b, v_ple_proj):
    shard = dict(ln_gain=ln_gain, ln_bias=ln_bias, pool_w=pool_w, pool_b=pool_b, pool_scale=pool_scale, gdn_w_in=gdn_w_in,
                 gdn_conv=gdn_conv, gdn_a_log=gdn_a_log, gdn_dt_bias=gdn_dt_bias, gdn_norm_w=gdn_norm_w, gdn_w_out=gdn_w_out,
                 mlp_w1=mlp_w1, mlp_w2=mlp_w2, ple_gate_w=ple_gate_w, ple_gate_b=ple_gate_b, ple_proj=ple_proj)
    mom = dict(ln_gain=m_ln_gain, ln_bias=m_ln_bias, pool_w=m_pool_w, pool_b=m_pool_b, pool_scale=m_pool_scale,
               gdn_w_in=m_gdn_w_in, gdn_conv=m_gdn_conv, gdn_a_log=m_gdn_a_log, gdn_dt_bias=m_gdn_dt_bias,
               gdn_norm_w=m_gdn_norm_w, gdn_w_out=m_gdn_w_out, mlp_w1=m_mlp_w1, mlp_w2=m_mlp_w2, ple_gate_w=m_ple_gate_w,
               ple_gate_b=m_ple_gate_b, ple_proj=m_ple_proj)
    var = dict(ln_gain=v_ln_gain, ln_bias=v_ln_bias, pool_w=v_pool_w, pool_b=v_pool_b, pool_scale=v_pool_scale,
               gdn_w_in=v_gdn_w_in, gdn_conv=v_gdn_conv, gdn_a_log=v_gdn_a_log, gdn_dt_bias=v_gdn_dt_bias,
               gdn_norm_w=v_gdn_norm_w, gdn_w_out=v_gdn_w_out, mlp_w1=v_mlp_w1, mlp_w2=v_mlp_w2, ple_gate_w=v_ple_gate_w,
               ple_gate_b=v_ple_gate_b, ple_proj=v_ple_proj)

    xi, yi, ci = lax.axis_index("x"), lax.axis_index("y"), lax.axis_index("c")
    me = (2 * xi + yi).reshape(1).astype(jnp.int32)
    place = jnp.stack([2 * xi + yi, ci]).astype(jnp.int32)
    early = [("mlp_w1", 0), ("mlp_w2", 0), ("ple_gate_w", 0), ("ple_proj", 0), ("pool_w", 0)]
    late = [("mlp_w1", 1), ("mlp_w2", 1), ("ple_gate_w", 1), ("ple_proj", 1), ("gdn_w_out", 0), ("gdn_w_in", 0)]
    halved = lambda n: shard[n].reshape(shard[n].shape[0], 2, -1, shard[n].shape[-1])
    placed = lambda ops, tag: [_place_shard(halved(n), me, BF16, f"place_{tag}_{n}", l) for n, l in ops]
    small_in = _place_shard(_pack([shard[n] for n in GATHER_F32], 128, 8)[None], me, F32, "place_small")
    got_early = _all_gather(placed(early, "early") + [small_in], "weights_all_gather_early")
    placed_late = placed(late, "late")
    st = dict(zip(GATHER_F32, _unpack(got_early[-1].reshape(N_SHARD, -1), [shard[n].shape for n in GATHER_F32])))

    cat_last = lambda a: jnp.moveaxis(a, 0, -2).reshape(a.shape[1:-1] + (N_SHARD * a.shape[-1],))
    gain = cat_last(st["ln_gain"])
    bias = cat_last(st["ln_bias"])
    wp = got_early[4].reshape(N_SHARD, 4, POOL_GROUP // N_SHARD, POOL_GROUP)
    pb = cat_last(st["pool_b"]).reshape(1, D_MODEL)
    ps = pool_scale
    conv_w = cat_last(st["gdn_conv"])[0]
    merged = lambda g: g.reshape(N_SHARD, -1, g.shape[-1])
    mlp_w = lambda i, got: (merged(got[0]), merged(got[1]), merged(got[2]), ple_gate_b[i:i + 1], merged(got[3]))
    alog_l = _pad_lanes(gdn_a_log, HEADS)
    dtb_l = _pad_lanes(gdn_dt_bias, HEADS)
    nw = jnp.tile(gdn_norm_w, (1, HEADS))
    ln = lambda i, k: (gain[i, k][None], bias[i, k][None])

    x0 = x[0]
    p0, p1 = p[0, 0], p[1, 0]

    x1, xh1, rs1 = _pool_fwd(x0, wp, pb, ps, *ln(0, 0))
    (x2, xh2, rs2, a0), got_late = _mlp_fwd(x1, p0, *mlp_w(0, got_early), *ln(0, 1), "mlp_fwd_0", gather=placed_late)
    w_out = got_late[4].reshape(D_MODEL, D_MODEL)
    w_in = _assemble_w_in(merged(got_late[5]))
    proj = _matmul_nn(x2, w_in, "gdn_in_proj", GDN_IN_PAD // 3)
    q, k, v, gcb = _conv_fwd(proj, conv_w, alog_l, dtb_l)
    u, w, qd, kd, qk, gl, t_inv = _gdn_prep(q, k, v, gcb)
    o, states = _gdn_scan(u, w, qd, kd, qk, gl)
    x3, xh3, rs3 = _gdn_out_fwd(o, proj, x2, w_out, nw, *ln(1, 0))
    (x4, xh4, rs4, a1), _ = _mlp_fwd(x3, p1, *mlp_w(1, got_late), *ln(1, 1), "mlp_fwd_1")
    dy4, loss_l = _loss_head(x4, loss_target[0])

    g_gain = [[None, None], [None, None]]
    g_bias = [[None, None], [None, None]]

    def mlp_grads(i, dy, xh, rs, x_mid, a, p_i, got, scatter=()):
        (dx, dh, dzg, dpp, drb, dg, db, dgb), landed = _mlp_bwd(
            dy, xh, rs, x_mid, a, p_i, *mlp_w(i, got), ln(i, 1)[0], f"mlp_bwd_{i}", scatter=scatter)
        g_gain[i][1], g_bias[i][1] = dg, db
        return dx, dict(
            mlp_w1=_wgrad(x_mid, dh, f"dw1_{i}", stack_cols=True), mlp_w2=_wgrad(a, drb, f"dw2_{i}").reshape(N_SHARD, -1, D_MODEL),
            ple_gate_w=_wgrad(x_mid, dzg, f"dgate_w_{i}").reshape(N_SHARD, -1, D_MODEL),
            ple_proj=_wgrad(p_i, dpp, f"dproj_{i}", stack_cols=True), ple_gate_b=dgb), landed

    def chip_sums(pieces, wire, tag):
        others = _swap_halves(pieces, f"grads_swap_halves_{tag}")
        return [_add_half(a, b, place, t, f"grads_add_half_{tag}{i}") for i, (a, b, t) in enumerate(zip(pieces, others, wire))]

    dx3, gl1, _ = mlp_grads(1, dy4, xh4, rs4, x3, a1, p1, got_late)
    dres, do, dz, on_b, drb3, g_gain[1][0], g_bias[1][0], d_nw = _gdn_out_bwd(dx3, xh3, rs3, o, proj, w_out, nw, ln(1, 0)[0])
    d_wout = _wgrad(on_b, drb3, "dw_out").reshape(N_SHARD, -1, D_MODEL)
    du, dw, dqd, dkd, dqk, dgl = _gdn_scan_bwd(do, u, w, qd, kd, qk, gl, states)
    dq, dk, dv, dgcb = _gdn_prep_bwd(q, k, v, gcb, t_inv, u, w, du, dw, dqd, dkd, dqk, dgl)
    dproj, d_conv, d_alog_l, d_dtb_l = _conv_bwd(proj, dq, dk, dv, dgcb, dz, conv_w, alog_l, dtb_l)
    dx2 = _matmul_nt_add(dproj, w_in, dres, "gdn_in_bwd")
    d_win = _split_w_in(_wgrad(x2, dproj, "dw_in"))
    sums_late = chip_sums([gl1["mlp_w1"], gl1["mlp_w2"], gl1["ple_gate_w"], d_wout, d_win], [BF16] * 5, "late")
    dx1, gl0, landed_late = mlp_grads(0, dx2, xh2, rs2, x1, a0, p0, got_early, scatter=sums_late)
    dx0, g_gain[0][0], g_bias[0][0], d_ps, d_pb, d_wp = _pool_bwd(dx1, xh1, rs1, x0, wp, pb, ps, ln(0, 0)[0])

    split_last = lambda a: jnp.moveaxis(a.reshape(a.shape[:-1] + (N_SHARD, a.shape[-1] // N_SHARD)), -2, 0)
    small_st = dict(
        ple_proj=jnp.stack([gl0["ple_proj"], gl1["ple_proj"]], axis=1),
        pool_w=jnp.moveaxis(d_wp.reshape(4, N_SHARD, POOL_GROUP // N_SHARD, POOL_GROUP), 1, 0)[:, None],
        ln_gain=split_last(jnp.stack([jnp.concatenate(r, axis=0) for r in g_gain])),
        ln_bias=split_last(jnp.stack([jnp.concatenate(r, axis=0) for r in g_bias])),
        pool_b=split_last(d_pb.reshape(1, 4, POOL_GROUP)),
        gdn_conv=split_last(d_conv)[:, None],
    )
    rep = dict(pool_scale=d_ps, gdn_a_log=d_alog_l[:, HEADS:2 * HEADS], gdn_dt_bias=d_dtb_l[:, HEADS:2 * HEADS],
               gdn_norm_w=d_nw, ple_gate_b=jnp.concatenate([gl0["ple_gate_b"], gl1["ple_gate_b"]], axis=0))
    for n in REPLICATED:
        small_st[n] = jnp.broadcast_to(rep[n][None], (N_SHARD,) + rep[n].shape)
    small_flat = jnp.concatenate([small_st[n].reshape(N_SHARD, -1) for n in SMALL_GRADS], axis=1)
    small_rows = -(-small_flat.shape[1] // (16 * LANES)) * 16
    small_piece = jnp.pad(small_flat, ((0, 0), (0, small_rows * LANES - small_flat.shape[1]))).reshape(N_SHARD, small_rows, LANES)

    sums_early = chip_sums([gl0["mlp_w1"], gl0["mlp_w2"], gl0["ple_gate_w"], small_piece], [BF16] * 3 + [F32], "early")
    landed_early = _scatter_chips(sums_early, "grads_scatter_chips_early")
    red = _join_halves([_sum_chips(q_, p_, place, f"grads_sum_chips_{i}") for i, (q_, p_) in
                        enumerate(zip(list(landed_early) + list(landed_late), sums_early + sums_late))])
    red = [r.reshape(-1, r.shape[-1]) for r in red]
    grads = dict(mlp_w1=jnp.stack([red[0], red[4]]), mlp_w2=jnp.stack([red[1], red[5]]), ple_gate_w=jnp.stack([red[2], red[6]]),
                 gdn_w_out=red[7][None], gdn_w_in=red[8][None])
    grads.update(zip(SMALL_GRADS, _unpack(red[3].reshape(-1), [shard[n].shape for n in SMALL_GRADS])))

    delta, new_m, new_v = {}, {}, {}
    small = [n for n in WEIGHTS if shard[n].size < 128 * 128]
    for n in WEIGHTS:
        if n in small:
            continue
        to2d = lambda a, n=n: a.reshape(-1, shard[n].shape[-1])
        d2, m2, v2 = _adamw(to2d(shard[n]), to2d(grads[n]), to2d(mom[n]), to2d(var[n]), "adamw_" + n)
        delta[n], new_m[n], new_v[n] = (t.reshape(shard[n].shape) for t in (d2, m2, v2))
    pk = lambda d: _pack([d[n] for n in small], 128, 8).reshape(-1, 128)
    d2, m2, v2 = _adamw(pk(shard), pk(grads), pk(mom), pk(var), "adamw_small")
    for dst, t in ((delta, d2), (new_m, m2), (new_v, v2)):
        dst.update(zip(small, _unpack(t.reshape(-1), [shard[n].shape for n in small])))

    loss = lax.psum(loss_l[0, 0], ("x", "y", "c"))
    return (loss, dx0[None], *[grads[n] for n in WEIGHTS], *[delta[n] for n in WEIGHTS],
            *[new_m[n] for n in WEIGHTS], *[new_v[n] for n in WEIGHTS])
```

```python
import math

import jax
import jax.numpy as jnp
from jax import lax
from jax.experimental import pallas as pl
from jax.experimental.pallas import tpu as pltpu

F32 = jnp.float32
BF16 = jnp.bfloat16

D_MODEL = 1024
D_FF = 4096
PLE_DIM = 256
N_SHARD = 4
POOL_WINDOWS = (2, 4, 8, 16)
POOL_GROUP = 256
POOL_HALO = 16
HEADS = 8
HEAD_DIM = 128
CHUNK = 64
CONV_WIDTH = 4
CONV_HALO = 8
QKV_DIM = 3 * D_MODEL
GDN_IN_DIM = QKV_DIM + D_MODEL + 2 * HEADS
GDN_IN_PAD = 4224
BA_BLOCK = (QKV_DIM + D_MODEL) // 128
ALPHA = (2.0 * 2) ** 0.25
LN_EPS = 1e-5
RMS_EPS = 1e-6
L2_EPS = 1e-6
ADAM_LR, ADAM_B1, ADAM_B2, ADAM_EPS, ADAM_WD, ADAM_STEP = 0.001, 0.9, 0.999, 1e-08, 0.01, 10

ROW_TILE = 512
CONV_TILE = 256
PREP_CHUNKS = 16
SCAN_CHUNKS = 4
LANES = 1024
ADAM_ROWS = 256

NN = (((1,), (0,)), ((), ()))
NT = (((1,), (1,)), ((), ()))
TN = (((0,), (0,)), ((), ()))
BNN = (((2,), (1,)), ((0,), (0,)))
BNT = (((2,), (2,)), ((0,), (0,)))
BTN = (((1,), (1,)), ((0,), (0,)))
MESH = pl.DeviceIdType.MESH
ANY = pl.BlockSpec(memory_space=pl.ANY)


def _bdot(a, b, dims):
    return lax.dot_general(a.astype(BF16), b.astype(BF16), dims, preferred_element_type=F32)


def _hdot(a, b, dims):
    return lax.dot_general(a, b, dims, precision=lax.Precision.HIGHEST, preferred_element_type=F32)


def _mdot(a, b, dims):
    return lax.dot_general(a, b, dims, precision=lax.Precision.HIGH, preferred_element_type=F32)


def _sigmoid(x):
    return 0.5 * jnp.tanh(0.5 * x) + 0.5


def _silu(x):
    return x * _sigmoid(x)


def _softplus(x):
    return jnp.maximum(x, 0.0) + jnp.log1p(jnp.exp(-jnp.abs(x)))


def _call(body, name, grid, in_specs, out_specs, out_shape, scratch=(), sem=None, aliases=None):
    params = pltpu.CompilerParams(dimension_semantics=sem) if sem else None
    return pl.pallas_call(
        body, name=name, grid=grid, in_specs=in_specs, out_specs=out_specs, out_shape=out_shape,
        scratch_shapes=list(scratch), compiler_params=params, input_output_aliases=aliases or {})


def _row(d):
    return pl.BlockSpec((1, d), lambda *_: (0, 0))


def _full(shape):
    n = len(shape)
    return pl.BlockSpec(shape, lambda *_: (0,) * n)


def _sds(shape, dtype=F32):
    return jax.ShapeDtypeStruct(shape, dtype)


def _ln_fwd(r, gain, bias):
    mu = jnp.mean(r, axis=-1, keepdims=True)
    xc = r - mu
    rstd = lax.rsqrt(jnp.mean(xc * xc, axis=-1, keepdims=True) + LN_EPS)
    xhat = xc * rstd
    return xhat * gain + bias, xhat, rstd


def _ln_bwd(dy, xhat, rstd, gain):
    dxh = dy * gain
    m1 = jnp.mean(dxh, axis=-1, keepdims=True)
    m2 = jnp.mean(dxh * xhat, axis=-1, keepdims=True)
    return rstd * (dxh - m1 - xhat * m2)


def _acc(ref, first, val):
    @pl.when(first)
    def _():
        ref[...] = val

    @pl.when(jnp.logical_not(first))
    def _():
        ref[...] += val


def _pooled_groups(xe, t0, ts):
    pos = (t0 + lax.broadcasted_iota(jnp.int32, (ts, 1), 0) + 1).astype(F32)
    outs = []
    for gi, win in enumerate(POOL_WINDOWS):
        xs = xe[:, gi * POOL_GROUP:(gi + 1) * POOL_GROUP]
        s, k = xs, 1
        while k < win:
            s = s + pltpu.roll(s, k, 0)
            k *= 2
        mean = s[POOL_HALO:] / jnp.minimum(pos, float(win))
        outs.append(mean - xs[POOL_HALO:])
    return outs


def _pool_groups_w(w_ref):
    return [jnp.concatenate([w_ref[s, g] for s in range(N_SHARD)], axis=0) for g in range(4)]


def _pool_fwd(x, wp, pb, ps, gain, bias):
    s_len = x.shape[0]
    ts = min(ROW_TILE, s_len)
    hb = ts // POOL_HALO

    def body(x_ref, halo_ref, w_ref, pb_ref, ps_ref, g_ref, b_ref, y_ref, xhat_ref, rstd_ref):
        i = pl.program_id(0)
        x_t = x_ref[...]
        halo = jnp.where(i > 0, halo_ref[...], 0.0)
        pooled = _pooled_groups(jnp.concatenate([halo, x_t], axis=0), i * ts, ts)
        wg = _pool_groups_w(w_ref)
        y = jnp.concatenate([_bdot(pooled[g], wg[g], NN) for g in range(4)], axis=1) + pb_ref[...]
        r = ALPHA * x_t + y * ps_ref[...]
        y_ref[...], xhat_ref[...], rstd_ref[...] = _ln_fwd(r, g_ref[...], b_ref[...])

    tile = pl.BlockSpec((ts, D_MODEL), lambda i: (i, 0))
    return _call(
        body, "pool_fwd", (s_len // ts,),
        [tile, pl.BlockSpec((POOL_HALO, D_MODEL), lambda i: (jnp.maximum(i * hb - 1, 0), 0)),
         _full(wp.shape), _row(D_MODEL), _row(D_MODEL), _row(D_MODEL), _row(D_MODEL)],
        [tile, tile, pl.BlockSpec((ts, 1), lambda i: (i, 0))],
        [_sds((s_len, D_MODEL)), _sds((s_len, D_MODEL)), _sds((s_len, 1))],
        sem=("parallel",))(x, x, wp, pb, ps, gain, bias)


def _pool_bwd(dy, xhat, rstd, x, wp, pb, ps, gain, scatter=()):
    s_len = x.shape[0]
    ts = min(ROW_TILE, s_len)
    hb = ts // POOL_HALO
    n_t = s_len // ts
    ne = ts + POOL_HALO
    n_s = len(scatter)

    def body(*refs):
        dy_ref, dyn_ref, xh_ref, xhn_ref, rs_ref, rsn_ref, x_ref, xp_ref, w_ref, pb_ref, ps_ref, g_ref = refs[:12]
        dx_ref, dg_ref, db_ref, dps_ref, dpb_ref, dw_ref = refs[12 + n_s:18 + n_s]
        i = pl.program_id(0)
        if n_s:
            start, finish = _scatter_steps(refs[12:12 + n_s], refs[18 + n_s:18 + 2 * n_s], *refs[18 + 2 * n_s:])
            pl.when(i == 0)(start)
        more = i < n_t - 1
        dy_t, xh_t = dy_ref[...], xh_ref[...]
        dy_e = jnp.concatenate([dy_t, jnp.where(more, dyn_ref[...], 0.0)], axis=0)
        xh_e = jnp.concatenate([xh_t, xhn_ref[...]], axis=0)
        rs_e = jnp.concatenate([rs_ref[...], rsn_ref[...]], axis=0)
        dr_e = _ln_bwd(dy_e, xh_e, rs_e, g_ref[...])
        dyy_e = dr_e * ps_ref[...]
        pos_e = (i * ts + lax.broadcasted_iota(jnp.int32, (ne, 1), 0) + 1).astype(F32)
        dxs = []
        wg = _pool_groups_w(w_ref)
        for gi, win in enumerate(POOL_WINDOWS):
            sl = slice(gi * POOL_GROUP, (gi + 1) * POOL_GROUP)
            dpool = _bdot(dyy_e[:, sl], wg[gi], NT)
            s, k = dpool / jnp.minimum(pos_e, float(win)), 1
            while k < win:
                s = s + pltpu.roll(s, ne - k, 0)
                k *= 2
            dxs.append(s[:ts] - dpool[:ts])
        dx_ref[...] = ALPHA * dr_e[:ts] + jnp.concatenate(dxs, axis=1)

        x_t = x_ref[...]
        halo = jnp.where(i > 0, xp_ref[...], 0.0)
        pooled = _pooled_groups(jnp.concatenate([halo, x_t], axis=0), i * ts, ts)
        y = jnp.concatenate([_bdot(pooled[g], wg[g], NN) for g in range(4)], axis=1) + pb_ref[...]
        dr_t, dyy_t = dr_e[:ts], dyy_e[:ts]
        first = i == 0
        _acc(dg_ref, first, jnp.sum(dy_t * xh_t, axis=0, keepdims=True))
        _acc(db_ref, first, jnp.sum(dy_t, axis=0, keepdims=True))
        _acc(dps_ref, first, jnp.sum(dr_t * y, axis=0, keepdims=True))
        _acc(dpb_ref, first, jnp.sum(dyy_t, axis=0, keepdims=True))
        for g in range(4):
            _acc(dw_ref.at[g], first, _bdot(pooled[g], dyy_t[:, g * POOL_GROUP:(g + 1) * POOL_GROUP], TN))
        if n_s:
            pl.when(i == n_t - 1)(finish)

    tile = pl.BlockSpec((ts, D_MODEL), lambda i: (i, 0))
    nxt = pl.BlockSpec((POOL_HALO, D_MODEL), lambda i: (jnp.minimum((i + 1) * hb, n_t * hb - 1), 0))
    prv = pl.BlockSpec((POOL_HALO, D_MODEL), lambda i: (jnp.maximum(i * hb - 1, 0), 0))
    rs_t = pl.BlockSpec((ts, 1), lambda i: (i, 0))
    rs_n = pl.BlockSpec((POOL_HALO, 1), lambda i: (jnp.minimum((i + 1) * hb, n_t * hb - 1), 0))
    row = _row(D_MODEL)
    out = _call(
        body, "pool_bwd", (n_t,),
        [tile, nxt, tile, nxt, rs_t, rs_n, tile, prv, _full(wp.shape), row, row, row] + [ANY] * n_s,
        [tile, row, row, row, row, _full((4, POOL_GROUP, POOL_GROUP))] + [ANY] * n_s,
        [_sds((s_len, D_MODEL))] + [_sds((1, D_MODEL))] * 4 + [_sds((4, POOL_GROUP, POOL_GROUP))]
        + [_sds(t.shape, t.dtype) for t in scatter],
        scratch=_scatter_sems(n_s) if n_s else [], sem=("arbitrary",),
    )(dy, dy, xhat, xhat, rstd, rstd, x, x, wp, pb, ps, gain, *scatter)
    return out[:6], out[6:]


def _mlp_weight_specs():
    fc = D_FF // N_SHARD
    return [pl.BlockSpec((None, D_MODEL, fc), lambda i, j: (j, 0, 0)),
            pl.BlockSpec((None, fc, D_MODEL), lambda i, j: (j, 0, 0)),
            _full((N_SHARD, D_MODEL // N_SHARD, D_MODEL)),
            _row(D_MODEL),
            _full((N_SHARD, PLE_DIM, D_MODEL // N_SHARD))]


def _gate_w(gw_ref):
    return gw_ref[...].reshape(D_MODEL, D_MODEL)


def _ple_proj(pj_ref):
    return jnp.concatenate([pj_ref[s] for s in range(N_SHARD)], axis=1)


def _mlp_fwd(x, p, w1s, w2s, gw, gb, proj, gain, bias, name, gather=(), target=None):
    s_len = x.shape[0]
    ts = min(ROW_TILE, s_len)
    n_i = s_len // ts
    n_g = len(gather)
    has_t = target is not None
    n_in, n_out = 9 + has_t, 4 + has_t

    def body(*refs):
        x_ref, p_ref, w1_ref, w2_ref, gw_ref, gb_ref, pj_ref, g_ref, b_ref = refs[:9]
        y_ref, xhat_ref, rstd_ref, a_ref = refs[n_in + n_g:n_in + n_g + 4]
        acc_ref, xb_ref = refs[n_in + n_out + 2 * n_g:n_in + n_out + 2 * n_g + 2]
        i, j = pl.program_id(0), pl.program_id(1)
        if n_g:
            start, finish = _gather_steps(refs[n_in + n_out + n_g:n_in + n_out + 2 * n_g], *refs[n_in + n_out + 2 * n_g + 2:])
            pl.when((i == 0) & (j == 0))(start)

        @pl.when(j == 0)
        def _():
            x_t = x_ref[...]
            xb_ref[...] = x_t.astype(BF16)
            gate = _sigmoid(_bdot(x_t, _gate_w(gw_ref), NN) + gb_ref[...])
            acc_ref[...] = ALPHA * x_t + gate * _bdot(p_ref[...], _ple_proj(pj_ref), NN)

        h = jnp.maximum(_bdot(xb_ref[...], w1_ref[...], NN), 0.0)
        a = (h * h).astype(BF16)
        a_ref[...] = a
        acc_ref[...] += _bdot(a, w2_ref[...], NN)

        @pl.when(j == N_SHARD - 1)
        def _():
            y, xhat_ref[...], rstd_ref[...] = _ln_fwd(acc_ref[...], g_ref[...], b_ref[...])
            if has_t:
                err = y - refs[9][...]
                y_ref[...] = err * (1.0 / D_MODEL)
                part = 0.5 * jnp.sum(jnp.mean(err * err, axis=-1, keepdims=True))
                _acc(refs[n_in + n_g + 4], i == 0, part + jnp.zeros((1, 128), F32))
            else:
                y_ref[...] = y

        if n_g:
            pl.when((i == n_i - 1) & (j == N_SHARD - 1))(finish)

    tile = pl.BlockSpec((ts, D_MODEL), lambda i, j: (i, 0))
    row = _row(D_MODEL)
    out = _call(
        body, name, (n_i, N_SHARD),
        [tile, pl.BlockSpec((ts, PLE_DIM), lambda i, j: (i, 0))] + _mlp_weight_specs() + [row, row] + [tile] * has_t
        + [ANY] * n_g,
        [tile, tile, pl.BlockSpec((ts, 1), lambda i, j: (i, 0)), pl.BlockSpec((ts, D_FF // N_SHARD), lambda i, j: (i, j))]
        + [_row(128)] * has_t + [ANY] * n_g,
        [_sds((s_len, D_MODEL)), _sds((s_len, D_MODEL)), _sds((s_len, 1)), _sds((s_len, D_FF), BF16)]
        + [_sds((1, 128))] * has_t + [_sds(a.shape, a.dtype) for a in gather],
        scratch=[pltpu.VMEM((ts, D_MODEL), F32), pltpu.VMEM((ts, D_MODEL), BF16)] + (_gather_sems(n_g) if n_g else []),
        sem=("arbitrary", "arbitrary"), aliases={n_in + k: n_out + k for k in range(n_g)},
    )(x, p, w1s, w2s, gw, gb, proj, gain, bias, *([target] if has_t else []), *gather)
    return out[:n_out], out[n_out:]


def _mlp_bwd(dy, xhat, rstd, x, a, p, w1s, w2s, gw, gb, proj, gain, name, scatter=()):
    s_len = x.shape[0]
    ts = min(ROW_TILE, s_len)
    fc = D_FF // N_SHARD
    n_i = s_len // ts
    n_s = len(scatter)

    def body(*refs):
        dy_ref, xh_ref, rs_ref, x_ref, a_ref, p_ref, w1_ref, w2_ref, gw_ref, gb_ref, pj_ref, g_ref = refs[:12]
        dx_ref, dh_ref, dzg_ref, dpp_ref, drb_ref, dg_ref, db_ref, dgb_ref = refs[12 + n_s:20 + n_s]
        acc_ref, dr_ref = refs[20 + 2 * n_s:22 + 2 * n_s]
        i, j = pl.program_id(0), pl.program_id(1)
        if n_s:
            start, finish = _scatter_steps(refs[12:12 + n_s], refs[20 + n_s:20 + 2 * n_s], *refs[22 + 2 * n_s:])
            pl.when((i == 0) & (j == 0))(start)

        @pl.when(j == 0)
        def _():
            dy_t, xh_t, x_t = dy_ref[...], xh_ref[...], x_ref[...]
            dr = _ln_bwd(dy_t, xh_t, rs_ref[...], g_ref[...])
            drb = dr.astype(BF16)
            dr_ref[...] = drb
            drb_ref[...] = drb
            gw_full = _gate_w(gw_ref)
            gate = _sigmoid(_bdot(x_t, gw_full, NN) + gb_ref[...])
            pp = _bdot(p_ref[...], _ple_proj(pj_ref), NN)
            dzg = dr * pp * gate * (1.0 - gate)
            dzg_ref[...] = dzg.astype(BF16)
            dpp_ref[...] = (dr * gate).astype(BF16)
            acc_ref[...] = ALPHA * dr + _bdot(dzg, gw_full, NT)
            first = i == 0
            _acc(dg_ref, first, jnp.sum(dy_t * xh_t, axis=0, keepdims=True))
            _acc(db_ref, first, jnp.sum(dy_t, axis=0, keepdims=True))
            _acc(dgb_ref, first, jnp.sum(dzg, axis=0, keepdims=True))

        dh = (_bdot(dr_ref[...], w2_ref[...], NT) * (2.0 * jnp.sqrt(a_ref[...].astype(F32)))).astype(BF16)
        dh_ref[...] = dh
        acc_ref[...] += _bdot(dh, w1_ref[...], NT)

        @pl.when(j == N_SHARD - 1)
        def _():
            dx_ref[...] = acc_ref[...]

        if n_s:
            pl.when((i == n_i - 1) & (j == N_SHARD - 1))(finish)

    tile = pl.BlockSpec((ts, D_MODEL), lambda i, j: (i, 0))
    ftile = pl.BlockSpec((ts, fc), lambda i, j: (i, j))
    row = _row(D_MODEL)
    out = _call(
        body, name, (n_i, N_SHARD),
        [tile, tile, pl.BlockSpec((ts, 1), lambda i, j: (i, 0)), tile, ftile, pl.BlockSpec((ts, PLE_DIM), lambda i, j: (i, 0))]
        + _mlp_weight_specs() + [row] + [ANY] * n_s,
        [tile, ftile, tile, tile, tile, row, row, row] + [ANY] * n_s,
        [_sds((s_len, D_MODEL)), _sds((s_len, D_FF), BF16)]
        + [_sds((s_len, D_MODEL), BF16)] * 3 + [_sds((1, D_MODEL))] * 3 + [_sds(t.shape, t.dtype) for t in scatter],
        scratch=[pltpu.VMEM((ts, D_MODEL), F32), pltpu.VMEM((ts, D_MODEL), BF16)] + (_scatter_sems(n_s) if n_s else []),
        sem=("arbitrary", "arbitrary"))(dy, xhat, rstd, x, a, p, w1s, w2s, gw, gb, proj, gain, *scatter)
    return out[:8], out[8:]


def _wgrad(a, b, name, stack_cols=False):
    s_len, m = a.shape
    n = b.shape[1]
    ts = min(2048 if a.dtype == BF16 and b.dtype == BF16 else 1024, s_len)
    tm = min(m, 1024)
    tn = n // N_SHARD if stack_cols else (1408 if n == GDN_IN_PAD else min(n, 1024))
    n_s = s_len // ts

    def body(a_ref, b_ref, o_ref):
        _acc(o_ref, pl.program_id(2) == 0, _bdot(a_ref[...], b_ref[...], TN))

    if stack_cols:
        out_spec = pl.BlockSpec((None, tm, tn), lambda mi, nj, s: (nj, mi, 0))
        out_shape = _sds((N_SHARD, m, tn))
    else:
        out_spec = pl.BlockSpec((tm, tn), lambda mi, nj, s: (mi, nj))
        out_shape = _sds((m, n))
    return _call(
        body, name, (m // tm, n // tn, n_s),
        [pl.BlockSpec((ts, tm), lambda mi, nj, s: (s, mi)), pl.BlockSpec((ts, tn), lambda mi, nj, s: (s, nj))],
        out_spec, out_shape, sem=("parallel", "parallel", "arbitrary"))(a, b)


def _matmul_nn(a, b, name, tn):
    s_len, k = a.shape
    n = b.shape[1]
    ts = min(512, s_len)

    def body(a_ref, b_ref, o_ref):
        o_ref[...] = _bdot(a_ref[...], b_ref[...], NN)

    return _call(
        body, name, (s_len // ts, n // tn),
        [pl.BlockSpec((ts, k), lambda i, j: (i, 0)), pl.BlockSpec((k, tn), lambda i, j: (0, j))],
        pl.BlockSpec((ts, tn), lambda i, j: (i, j)), _sds((s_len, n)), sem=("parallel", "parallel"))(a, b)


def _matmul_nt_add(a, b, res, name):
    s_len, k = a.shape
    n = b.shape[0]
    ts = min(512, s_len)

    def body(a_ref, b_ref, r_ref, o_ref):
        o_ref[...] = r_ref[...] + _bdot(a_ref[...], b_ref[...], NT)

    return _call(
        body, name, (s_len // ts,),
        [pl.BlockSpec((ts, k), lambda i: (i, 0)), _full((n, k)), pl.BlockSpec((ts, n), lambda i: (i, 0))],
        pl.BlockSpec((ts, n), lambda i: (i, 0)), _sds((s_len, n)), sem=("parallel",))(a, b, res)


def _act_qkv(y):
    qkv = _silu(y)
    qs, ks = [], []
    for h in range(HEADS):
        qh = qkv[:, h * HEAD_DIM:(h + 1) * HEAD_DIM]
        kh = qkv[:, D_MODEL + h * HEAD_DIM:D_MODEL + (h + 1) * HEAD_DIM]
        qs.append(qh * (lax.rsqrt(jnp.sum(qh * qh, axis=-1, keepdims=True) + L2_EPS) * HEAD_DIM ** -0.5))
        ks.append(kh * lax.rsqrt(jnp.sum(kh * kh, axis=-1, keepdims=True) + L2_EPS))
    return jnp.concatenate(qs, axis=1), jnp.concatenate(ks, axis=1), qkv[:, 2 * D_MODEL:]


def _act_gb(ba, alog_l, dtb_l, tril):
    lane = lax.broadcasted_iota(jnp.int32, ba.shape, 1)
    g = jnp.where((lane >= HEADS) & (lane < 2 * HEADS), -jnp.exp(alog_l) * _softplus(ba + dtb_l), 0.0)
    return jnp.where(lane < HEADS, _sigmoid(ba), _hdot(tril, g, NN))


def _chunk_tril(t):
    ii = lax.broadcasted_iota(jnp.int32, (t, t), 0)
    jj = lax.broadcasted_iota(jnp.int32, (t, t), 1)
    return ((ii // CHUNK == jj // CHUNK) & (ii >= jj)).astype(F32)


def _conv_rows(xe, w, n_rows):
    y = xe[CONV_HALO:CONV_HALO + n_rows] * w[CONV_WIDTH - 1]
    for j in range(CONV_WIDTH - 1):
        y = y + pltpu.roll(xe, CONV_WIDTH - 1 - j, 0)[CONV_HALO:CONV_HALO + n_rows] * w[j]
    return y


def _conv_fwd(proj, conv_w, alog_l, dtb_l):
    s_len = proj.shape[0]
    ts = min(CONV_TILE, s_len)
    hb = ts // CONV_HALO

    def body(x_ref, xp_ref, ba_ref, w_ref, al_ref, dt_ref, q_ref, k_ref, v_ref, gcb_ref):
        i = pl.program_id(0)
        halo = jnp.where(i > 0, xp_ref[...], 0.0)
        taps = [w_ref[pl.ds(j, 1), :] for j in range(CONV_WIDTH)]
        y = _conv_rows(jnp.concatenate([halo, x_ref[...]], axis=0), taps, ts)
        q_ref[...], k_ref[...], v_ref[...] = _act_qkv(y)
        gcb_ref[...] = _act_gb(ba_ref[...], al_ref[...], dt_ref[...], _chunk_tril(ts))

    tile = pl.BlockSpec((ts, D_MODEL), lambda i: (i, 0))
    return _call(
        body, "gdn_conv_fwd", (s_len // ts,),
        [pl.BlockSpec((ts, QKV_DIM), lambda i: (i, 0)),
         pl.BlockSpec((CONV_HALO, QKV_DIM), lambda i: (jnp.maximum(i * hb - 1, 0), 0)),
         pl.BlockSpec((ts, 128), lambda i: (i, BA_BLOCK)), _full((CONV_WIDTH, QKV_DIM)), _row(128), _row(128)],
        [tile, tile, tile, pl.BlockSpec((ts, 128), lambda i: (i, 0))],
        [_sds((s_len, D_MODEL))] * 3 + [_sds((s_len, 128))],
        sem=("parallel",))(proj, proj, proj, conv_w, alog_l, dtb_l)


def _conv_bwd(proj, dq, dk, dv, dgcb, dz, conv_w, alog_l, dtb_l):
    s_len = proj.shape[0]
    ts = min(CONV_TILE, s_len)
    hb = ts // CONV_HALO
    n_t = s_len // ts
    te = ts + CONV_HALO

    def body(x_ref, xp_ref, xn_ref, ba_ref, dq_ref, dqn_ref, dk_ref, dkn_ref, dv_ref, dvn_ref, dgcb_ref, dz_ref,
             w_ref, al_ref, dt_ref, dp_ref, dw_ref, dal_ref, ddt_ref):
        i = pl.program_id(0)
        more = i < n_t - 1
        w = [w_ref[pl.ds(j, 1), :] for j in range(CONV_WIDTH)]
        x_t = x_ref[...]
        xe = jnp.concatenate([jnp.where(i > 0, xp_ref[...], 0.0), x_t, xn_ref[...]], axis=0)
        y_e, act_vjp = jax.vjp(_act_qkv, _conv_rows(xe, w, te))
        ct = tuple(jnp.concatenate([t[...], jnp.where(more, n[...], 0.0)], axis=0)
                   for t, n in ((dq_ref, dqn_ref), (dk_ref, dkn_ref), (dv_ref, dvn_ref)))
        (dy_e,) = act_vjp(ct)
        ahead = [pltpu.roll(dy_e, te - (CONV_WIDTH - 1 - j), 0)[:ts] for j in range(CONV_WIDTH - 1)] + [dy_e[:ts]]
        dx = ahead[CONV_WIDTH - 1] * w[CONV_WIDTH - 1]
        for j in range(CONV_WIDTH - 1):
            dx = dx + ahead[j] * w[j]
        dws = [jnp.sum(ahead[j] * x_t, axis=0, keepdims=True) for j in range(CONV_WIDTH)]
        _, gb_vjp = jax.vjp(lambda ba, al, dt: _act_gb(ba, al, dt, _chunk_tril(ts)), ba_ref[...], al_ref[...], dt_ref[...])
        dba, dal, ddt = gb_vjp(dgcb_ref[...])
        dp_ref[...] = jnp.concatenate([dx.astype(BF16), dz_ref[...], dba.astype(BF16)], axis=1)
        first = i == 0
        for j in range(CONV_WIDTH):
            _acc(dw_ref.at[pl.ds(j, 1), :], first, dws[j])
        _acc(dal_ref, first, dal)
        _acc(ddt_ref, first, ddt)

    tile = pl.BlockSpec((ts, D_MODEL), lambda i: (i, 0))
    nxt = pl.BlockSpec((CONV_HALO, D_MODEL), lambda i: (jnp.minimum((i + 1) * hb, n_t * hb - 1), 0))
    return _call(
        body, "gdn_conv_bwd", (n_t,),
        [pl.BlockSpec((ts, QKV_DIM), lambda i: (i, 0)),
         pl.BlockSpec((CONV_HALO, QKV_DIM), lambda i: (jnp.maximum(i * hb - 1, 0), 0)),
         pl.BlockSpec((CONV_HALO, QKV_DIM), lambda i: (jnp.minimum((i + 1) * hb, n_t * hb - 1), 0)),
         pl.BlockSpec((ts, 128), lambda i: (i, BA_BLOCK)),
         tile, nxt, tile, nxt, tile, nxt, pl.BlockSpec((ts, 128), lambda i: (i, 0)), tile,
         _full((CONV_WIDTH, QKV_DIM)), _row(128), _row(128)],
        [pl.BlockSpec((ts, GDN_IN_PAD), lambda i: (i, 0)), _full((CONV_WIDTH, QKV_DIM)), _row(128), _row(128)],
        [_sds((s_len, GDN_IN_PAD), BF16), _sds((CONV_WIDTH, QKV_DIM)), _sds((1, 128)), _sds((1, 128))],
        sem=("arbitrary",))(proj, proj, proj, proj, dq, dq, dk, dk, dv, dv, dgcb, dz, conv_w, alog_l, dtb_l)


def _tri_inv(a_strict):
    ii = lax.broadcasted_iota(jnp.int32, (CHUNK, CHUNK), 0)
    jj = lax.broadcasted_iota(jnp.int32, (CHUNK, CHUNK), 1)
    x = (ii == jj).astype(F32) - a_strict
    pw = _mdot(a_strict, a_strict, BNN)
    for step in range(5):
        x = x + _mdot(x, pw, BNN)
        if step < 4:
            pw = _mdot(pw, pw, BNN)
    return x


@jax.custom_vjp
def _solved(a_strict, rhs, t, sol):
    return sol


def _solved_fwd(a_strict, rhs, t, sol):
    return sol, (t, sol)


def _solved_bwd(res, dsol):
    t, sol = res
    drhs = _mdot(t, dsol, BTN)
    return -_mdot(drhs, sol, BNT), drhs, jnp.zeros_like(t), jnp.zeros_like(sol)


_solved.defvjp(_solved_fwd, _solved_bwd)


def _prep(q, k, v, gc, beta, solve):
    ii = lax.broadcasted_iota(jnp.int32, (CHUNK, CHUNK), 0)
    jj = lax.broadcasted_iota(jnp.int32, (CHUNK, CHUNK), 1)
    causal, strict = ii >= jj, ii > jj
    gc_row = jnp.sum((ii == jj).astype(F32) * gc, axis=1, keepdims=True)
    decay = jnp.where(causal, jnp.exp(jnp.where(causal, gc - gc_row, 0.0)), 0.0)
    kb = k * beta
    a = jnp.where(strict, _bdot(kb, k, BNT) * decay, 0.0)
    eg = jnp.exp(gc)
    sol = solve(a, jnp.concatenate([v * beta, kb * eg], axis=-1))
    qk = _bdot(q, k, BNT) * decay
    last = lax.broadcasted_iota(jnp.int32, (CHUNK, 1), 0) == CHUNK - 1
    g_last = jnp.sum(jnp.where(last, gc, 0.0), axis=1, keepdims=True)
    kd = k * jnp.exp(g_last - gc)
    gl = jnp.exp(g_last) + jnp.zeros((1, 1, HEAD_DIM), F32)
    return sol[..., :HEAD_DIM], sol[..., HEAD_DIM:], qk, q * eg, kd, gl


def _prep_specs(s_len):
    rows = min(PREP_CHUNKS, s_len // CHUNK) * CHUNK
    m = rows // CHUNK
    hd = pl.BlockSpec((rows, HEAD_DIM), lambda c, h: (c, h))
    gcb = pl.BlockSpec((rows, 128), lambda c, h: (c, 0))
    qk = pl.BlockSpec((None, rows, CHUNK), lambda c, h: (h, c, 0))
    gl = pl.BlockSpec((None, m, HEADS, HEAD_DIM), lambda c, h: (c, 0, 0, 0))
    return rows, m, hd, gcb, qk, gl


def _head_cols(gcb, h, m):
    lane = lax.broadcasted_iota(jnp.int32, gcb.shape, 1)
    pick = lambda at: jnp.sum(jnp.where(lane == at, gcb, 0.0), axis=1, keepdims=True).reshape(m, CHUNK, 1)
    return pick(h + HEADS), pick(h)


def _gdn_prep(q, k, v, gcb):
    s_len = q.shape[0]
    rows, m, hd, gcb_spec, qk_spec, gl_spec = _prep_specs(s_len)

    def body(q_ref, k_ref, v_ref, gcb_ref, u_ref, w_ref, qd_ref, kd_ref, qk_ref, gl_ref, t_ref):
        r3 = lambda ref, d: ref[...].reshape(m, CHUNK, d)
        gc, beta = _head_cols(gcb_ref[...], pl.program_id(1), m)

        def solve(a, rhs):
            t = _tri_inv(a)
            t_ref[...] = t.reshape(rows, CHUNK)
            return _mdot(t, rhs, BNN)

        u, w, qk, qd, kd, gl = _prep(r3(q_ref, HEAD_DIM), r3(k_ref, HEAD_DIM), r3(v_ref, HEAD_DIM), gc, beta, solve)
        u_ref[...] = u.reshape(rows, HEAD_DIM)
        w_ref[...] = w.reshape(rows, HEAD_DIM)
        qd_ref[...] = qd.reshape(rows, HEAD_DIM).astype(BF16)
        kd_ref[...] = kd.reshape(rows, HEAD_DIM).astype(BF16)
        qk_ref[...] = qk.reshape(rows, CHUNK).astype(BF16)
        gl_ref[:, pl.ds(pl.program_id(1), 1), :] = gl

    n_g = s_len // rows
    return _call(
        body, "gdn_prep", (n_g, HEADS), [hd, hd, hd, gcb_spec], [hd, hd, hd, hd, qk_spec, gl_spec, qk_spec],
        [_sds((s_len, D_MODEL))] * 2 + [_sds((s_len, D_MODEL), BF16)] * 2
        + [_sds((HEADS, s_len, CHUNK), BF16), _sds((n_g, m, HEADS, HEAD_DIM)), _sds((HEADS, s_len, CHUNK))],
        sem=("parallel", "arbitrary"))(q, k, v, gcb)


def _gdn_prep_bwd(q, k, v, gcb, t_inv, u, w, du, dw, dqd, dkd, dqk, dgl):
    s_len = q.shape[0]
    rows, m, hd, gcb_spec, qk_spec, gl_spec = _prep_specs(s_len)

    def body(q_ref, k_ref, v_ref, gcb_ref, t_ref, u_ref, w_ref, du_ref, dw_ref, dqd_ref, dkd_ref, dqk_ref, dgl_ref,
             dq_ref, dk_ref, dv_ref, dgcb_ref):
        h = pl.program_id(1)
        r3 = lambda ref, d: ref[...].reshape(m, CHUNK, d)
        gc, beta = _head_cols(gcb_ref[...], h, m)
        t = r3(t_ref, CHUNK)
        sol = jnp.concatenate([r3(u_ref, HEAD_DIM), r3(w_ref, HEAD_DIM)], axis=-1)
        fn = lambda q_, k_, v_, gc_, bt_: _prep(q_, k_, v_, gc_, bt_, lambda a, rhs: _solved(a, rhs, t, sol))
        _, vjp = jax.vjp(fn, r3(q_ref, HEAD_DIM), r3(k_ref, HEAD_DIM), r3(v_ref, HEAD_DIM), gc, beta)
        ct = (r3(du_ref, HEAD_DIM), r3(dw_ref, HEAD_DIM), r3(dqk_ref, CHUNK), r3(dqd_ref, HEAD_DIM), r3(dkd_ref, HEAD_DIM),
              dgl_ref[:, pl.ds(h, 1), :] * (1.0 / HEAD_DIM))
        dq, dk, dv, dgc, dbt = vjp(ct)
        dq_ref[...] = dq.reshape(rows, HEAD_DIM)
        dk_ref[...] = dk.reshape(rows, HEAD_DIM)
        dv_ref[...] = dv.reshape(rows, HEAD_DIM)
        lane = lax.broadcasted_iota(jnp.int32, (rows, 128), 1)
        mine = jnp.where(lane == h, dbt.reshape(rows, 1), 0.0) + jnp.where(lane == h + HEADS, dgc.reshape(rows, 1), 0.0)
        _acc(dgcb_ref, h == 0, mine)

    return _call(
        body, "gdn_prep_bwd", (s_len // rows, HEADS),
        [hd, hd, hd, gcb_spec, qk_spec, hd, hd, hd, hd, hd, hd, qk_spec, gl_spec], [hd, hd, hd, gcb_spec],
        [_sds((s_len, D_MODEL))] * 3 + [_sds((s_len, 128))],
        sem=("parallel", "arbitrary"))(q, k, v, gcb, t_inv, u, w, du, dw, dqd, dkd, dqk, dgl)


def _scan_specs(n_c, m, k, reverse):
    n_b = n_c // k
    at = (lambda n: n_b - 1 - n) if reverse else (lambda n: n)
    row = pl.BlockSpec((k * CHUNK, D_MODEL), lambda n: (at(n), 0))
    qk = pl.BlockSpec((HEADS, k * CHUNK, CHUNK), lambda n: (0, at(n), 0))
    gl = pl.BlockSpec((None, k, HEADS, HEAD_DIM), lambda n: (at(n) // (m // k), at(n) % (m // k), 0, 0))
    st = pl.BlockSpec((k, HEADS, HEAD_DIM, HEAD_DIM), lambda n: (at(n), 0, 0, 0))
    return row, qk, gl, st


def _gdn_scan(u, w, qd, kd, qk, gl):
    s_len = u.shape[0]
    n_c = s_len // CHUNK
    k = min(SCAN_CHUNKS, gl.shape[1])
    row, qk_spec, gl_spec, st_spec = _scan_specs(n_c, gl.shape[1], k, False)

    def body(u_ref, w_ref, qd_ref, kd_ref, qk_ref, gl_ref, o_ref, st_ref, state):
        hs = range(HEADS)
        sl = [slice(h * HEAD_DIM, (h + 1) * HEAD_DIM) for h in hs]
        first = pl.program_id(0) == 0
        s_all = [jnp.where(first, 0.0, state[h]) for h in hs]
        for c in range(k):
            rows = slice(c * CHUNK, (c + 1) * CHUNK)
            s_b = [s.astype(BF16) for s in s_all]
            ws = [_bdot(w_ref[rows, sl[h]], s_b[h], NN) for h in hs]
            qs = [_bdot(qd_ref[rows, sl[h]], s_b[h], NN) for h in hs]
            vn = [(u_ref[rows, sl[h]] - ws[h]).astype(BF16) for h in hs]
            outs = [qs[h] + _bdot(qk_ref[h, rows, :], vn[h], NN) for h in hs]
            nxt = [s_all[h] * gl_ref[c, pl.ds(h, 1), :] + _bdot(kd_ref[rows, sl[h]], vn[h], TN) for h in hs]
            for h in hs:
                st_ref[c, h] = s_all[h]
                o_ref[rows, sl[h]] = outs[h]
            s_all = nxt
        for h in hs:
            state[h] = s_all[h]

    return _call(
        body, "gdn_scan", (n_c // k,), [row, row, row, row, qk_spec, gl_spec], [row, st_spec],
        [_sds((s_len, D_MODEL)), _sds((n_c, HEADS, HEAD_DIM, HEAD_DIM))],
        scratch=[pltpu.VMEM((HEADS, HEAD_DIM, HEAD_DIM), F32)], sem=("arbitrary",))(u, w, qd, kd, qk, gl)


def _gdn_scan_bwd(do, u, w, qd, kd, qk, gl, states):
    s_len = u.shape[0]
    n_c = s_len // CHUNK
    k = min(SCAN_CHUNKS, gl.shape[1])
    row, qk_spec, gl_spec, st_spec = _scan_specs(n_c, gl.shape[1], k, True)

    def body(do_ref, u_ref, w_ref, qd_ref, kd_ref, qk_ref, gl_ref, st_ref,
             du_ref, dw_ref, dqd_ref, dkd_ref, dqk_ref, dgl_ref, dstate):
        hs = range(HEADS)
        sl = [slice(h * HEAD_DIM, (h + 1) * HEAD_DIM) for h in hs]
        first = pl.program_id(0) == 0
        ds_f = [jnp.where(first, 0.0, dstate[h]) for h in hs]
        for c in reversed(range(k)):
            rows = slice(c * CHUNK, (c + 1) * CHUNK)
            s_f = [st_ref[c, h] for h in hs]
            s_b = [s.astype(BF16) for s in s_f]
            ds_b = [d.astype(BF16) for d in ds_f]
            do_b = [do_ref[rows, sl[h]].astype(BF16) for h in hs]
            w_b = [w_ref[rows, sl[h]].astype(BF16) for h in hs]
            ws = [_bdot(w_b[h], s_b[h], NN) for h in hs]
            dvn = [_bdot(qk_ref[h, rows, :], do_b[h], TN) + _bdot(kd_ref[rows, sl[h]], ds_b[h], NN) for h in hs]
            dqd = [_bdot(do_b[h], s_b[h], NT) for h in hs]
            t_do = [_bdot(qd_ref[rows, sl[h]], do_b[h], TN) for h in hs]
            vn = [(u_ref[rows, sl[h]] - ws[h]).astype(BF16) for h in hs]
            dvn_b = [d.astype(BF16) for d in dvn]
            dw = [-_bdot(dvn_b[h], s_b[h], NT) for h in hs]
            dkd = [_bdot(vn[h], ds_b[h], NT) for h in hs]
            dqk = [_bdot(do_b[h], vn[h], NT) for h in hs]
            t_dv = [_bdot(w_b[h], dvn_b[h], TN) for h in hs]
            for h in hs:
                du_ref[rows, sl[h]], dw_ref[rows, sl[h]], dqd_ref[rows, sl[h]], dkd_ref[rows, sl[h]] = dvn[h], dw[h], dqd[h], dkd[h]
                dqk_ref[h, rows, :] = dqk[h]
                dgl_ref[c, pl.ds(h, 1), :] = jnp.sum(s_f[h] * ds_f[h]) + jnp.zeros((1, HEAD_DIM), F32)
            ds_f = [ds_f[h] * gl_ref[c, pl.ds(h, 1), :] + t_do[h] - t_dv[h] for h in hs]
        for h in hs:
            dstate[h] = ds_f[h]

    return _call(
        body, "gdn_scan_bwd", (n_c // k,), [row, row, row, row, row, qk_spec, gl_spec, st_spec],
        [row, row, row, row, qk_spec, gl_spec],
        [_sds((s_len, D_MODEL))] * 4 + [_sds((HEADS, s_len, CHUNK)), _sds(gl.shape)],
        scratch=[pltpu.VMEM((HEADS, HEAD_DIM, HEAD_DIM), F32)], sem=("arbitrary",))(do, u, w, qd, kd, qk, gl, states)


def _gate_norm(o, z, nw):
    outs = []
    for h in range(HEADS):
        oh = o[:, h * HEAD_DIM:(h + 1) * HEAD_DIM]
        outs.append(oh * lax.rsqrt(jnp.mean(oh * oh, axis=-1, keepdims=True) + RMS_EPS))
    return jnp.concatenate(outs, axis=1) * nw * _silu(z)


def _gdn_out_fwd(o, proj, x, w_out, nw, gain, bias):
    s_len = x.shape[0]
    ts = min(ROW_TILE, s_len)

    def body(o_ref, z_ref, x_ref, w_ref, nw_ref, g_ref, b_ref, y_ref, xhat_ref, rstd_ref):
        on = _gate_norm(o_ref[...], z_ref[...], nw_ref[...])
        r = ALPHA * x_ref[...] + _bdot(on, w_ref[...], NN)
        y_ref[...], xhat_ref[...], rstd_ref[...] = _ln_fwd(r, g_ref[...], b_ref[...])

    tile = pl.BlockSpec((ts, D_MODEL), lambda i: (i, 0))
    row = _row(D_MODEL)
    return _call(
        body, "gdn_out_fwd", (s_len // ts,),
        [tile, pl.BlockSpec((ts, D_MODEL), lambda i: (i, QKV_DIM // D_MODEL)), tile, _full((D_MODEL, D_MODEL)), row, row, row],
        [tile, tile, pl.BlockSpec((ts, 1), lambda i: (i, 0))],
        [_sds((s_len, D_MODEL)), _sds((s_len, D_MODEL)), _sds((s_len, 1))], sem=("parallel",))(o, proj, x, w_out, nw, gain, bias)


def _gdn_out_bwd(dy, xhat, rstd, o, proj, w_out, nw, gain):
    s_len = o.shape[0]
    ts = min(ROW_TILE, s_len)

    def body(dy_ref, xh_ref, rs_ref, o_ref, z_ref, w_ref, nw_ref, g_ref,
             dres_ref, do_ref, dz_ref, on_ref, drb_ref, dg_ref, db_ref, dnw_ref):
        dy_t, xh_t = dy_ref[...], xh_ref[...]
        dr = _ln_bwd(dy_t, xh_t, rs_ref[...], g_ref[...])
        dres_ref[...] = ALPHA * dr
        drb_ref[...] = dr.astype(BF16)
        on, vjp = jax.vjp(_gate_norm, o_ref[...], z_ref[...], nw_ref[...])
        on_ref[...] = on.astype(BF16)
        do, dz, dnw = vjp(_bdot(dr, w_ref[...], NT))
        do_ref[...] = do
        dz_ref[...] = dz.astype(BF16)
        first = pl.program_id(0) == 0
        _acc(dg_ref, first, jnp.sum(dy_t * xh_t, axis=0, keepdims=True))
        _acc(db_ref, first, jnp.sum(dy_t, axis=0, keepdims=True))
        _acc(dnw_ref, first, sum(dnw[:, h * HEAD_DIM:(h + 1) * HEAD_DIM] for h in range(HEADS)))

    tile = pl.BlockSpec((ts, D_MODEL), lambda i: (i, 0))
    row = _row(D_MODEL)
    return _call(
        body, "gdn_out_bwd", (s_len // ts,),
        [tile, tile, pl.BlockSpec((ts, 1), lambda i: (i, 0)), tile,
         pl.BlockSpec((ts, D_MODEL), lambda i: (i, QKV_DIM // D_MODEL)), _full((D_MODEL, D_MODEL)), row, row],
        [tile, tile, tile, tile, tile, row, row, _row(HEAD_DIM)],
        [_sds((s_len, D_MODEL))] * 2 + [_sds((s_len, D_MODEL), BF16)] * 3 + [_sds((1, D_MODEL))] * 2 + [_sds((1, HEAD_DIM))],
        sem=("arbitrary",))(dy, xhat, rstd, o, proj, w_out, nw, gain)


def _adamw(w, g, m, v, name):
    r, c = w.shape
    tr = min(ADAM_ROWS, r)

    def body(w_ref, g_ref, m_ref, v_ref, d_ref, nm_ref, nv_ref):
        g_t = g_ref[...]
        nm = ADAM_B1 * m_ref[...] + (1.0 - ADAM_B1) * g_t
        nv = ADAM_B2 * v_ref[...] + (1.0 - ADAM_B2) * (g_t * g_t)
        m_hat = nm / (1.0 - ADAM_B1 ** ADAM_STEP)
        v_hat = nv / (1.0 - ADAM_B2 ** ADAM_STEP)
        d_ref[...] = -ADAM_LR * (m_hat / (jnp.sqrt(v_hat) + ADAM_EPS) + ADAM_WD * w_ref[...])
        nm_ref[...] = nm
        nv_ref[...] = nv

    tile = pl.BlockSpec((tr, c), lambda i: (i, 0))
    return _call(body, name, (r // tr,), [tile] * 4, [tile] * 3, [_sds((r, c))] * 3, sem=("parallel",))(w, g, m, v)


def _assemble_w_in(shards):
    rows = 256
    width = GDN_IN_DIM // N_SHARD

    def body(s_ref, o_ref):
        pad = jnp.zeros((rows, GDN_IN_PAD - GDN_IN_DIM), shards.dtype)
        o_ref[...] = jnp.concatenate([s_ref[j] for j in range(N_SHARD)] + [pad], axis=1)

    return _call(body, "w_in_assemble", (D_MODEL // rows,), [pl.BlockSpec((N_SHARD, rows, width), lambda i: (0, i, 0))],
                 pl.BlockSpec((rows, GDN_IN_PAD), lambda i: (i, 0)), _sds((D_MODEL, GDN_IN_PAD), shards.dtype),
                 sem=("parallel",))(shards)


def _split_w_in(full):
    rows = 256
    width = GDN_IN_DIM // N_SHARD

    def body(f_ref, o_ref):
        f = f_ref[...]
        for j in range(N_SHARD):
            o_ref[j] = f[:, j * width:(j + 1) * width]

    return _call(body, "w_in_split", (D_MODEL // rows,), [pl.BlockSpec((rows, GDN_IN_PAD), lambda i: (i, 0))],
                 pl.BlockSpec((N_SHARD, rows, width), lambda i: (0, i, 0)), _sds((N_SHARD, D_MODEL, width), full.dtype),
                 sem=("parallel",))(full)


def _place():
    x, y, c = lax.axis_index("x"), lax.axis_index("y"), lax.axis_index("c")
    return x, y, c, [(1 - x, y), (x, 1 - y), (1 - x, 1 - y)]


def _row_tile(rows):
    return max(t for t in range(8, min(rows, 640) + 1, 8) if rows % t == 0)


def _place_shard(part, me, dtype, name, layer=0):
    _, _, r, c = part.shape
    tr = _row_tile(r)

    def body(me_ref, p_ref, o_ref):
        o_ref[...] = p_ref[...].astype(dtype)

    return pl.pallas_call(
        body, name=name, out_shape=_sds((N_SHARD, 2, r, c), dtype),
        grid_spec=pltpu.PrefetchScalarGridSpec(
            num_scalar_prefetch=1, grid=(2, r // tr),
            in_specs=[pl.BlockSpec((None, None, tr, c), lambda h, i, me_ref: (layer, h, i, 0))],
            out_specs=pl.BlockSpec((None, None, tr, c), lambda h, i, me_ref: (me_ref[0], h, i, 0))))(me, part)


def _gather_sems(n):
    return [pltpu.SemaphoreType.DMA((6 * n,)), pltpu.SemaphoreType.DMA((6 * n,))]


def _gather_steps(dsts, send_sems, recv_sems):
    n = len(dsts)
    x, y, c, chips = _place()
    me = 2 * x + y
    sibling = (x, y, 1 - c)

    def ici(k, j, slot):
        px, py = chips[j]
        view = dsts[k].at[slot, c]
        return pltpu.make_async_remote_copy(
            src_ref=view, dst_ref=view, send_sem=send_sems.at[6 * k + j],
            recv_sem=recv_sems.at[6 * k + j], device_id=(px, py, c), device_id_type=MESH)

    def d2d(k, j, half):
        px, py = chips[j]
        view = dsts[k].at[2 * px + py, half]
        return pltpu.make_async_remote_copy(
            src_ref=view, dst_ref=view, send_sem=send_sems.at[6 * k + 3 + j], recv_sem=recv_sems.at[6 * k + 3 + j],
            device_id=sibling, device_id_type=MESH)

    def start():
        for k in range(n):
            for j in range(3):
                ici(k, j, me).start()

    def finish():
        fwds = []
        for k in range(n):
            for j, (px, py) in enumerate(chips):
                ici(k, j, 2 * px + py).wait_recv()
                fwds.append(d2d(k, j, c))
                fwds[-1].start()
        for k in range(n):
            for j in range(3):
                d2d(k, j, 1 - c).wait_recv()
        for k in range(n):
            for j in range(3):
                ici(k, j, me).wait_send()
        for cp in fwds:
            cp.wait_send()

    return start, finish


def _all_gather(bufs, name):
    n = len(bufs)

    def body(*refs):
        start, finish = _gather_steps(refs[n:2 * n], *refs[2 * n:])
        start()
        finish()

    return pl.pallas_call(
        body, name=name, out_shape=[_sds(a.shape, a.dtype) for a in bufs],
        in_specs=[ANY] * n, out_specs=[ANY] * n, input_output_aliases={k: k for k in range(n)},
        scratch_shapes=_gather_sems(n))(*bufs)


def _swap_halves(pieces, name):
    n = len(pieces)

    def body(*refs):
        srcs, dsts = refs[:n], refs[n:2 * n]
        send_sems, recv_sems = refs[2 * n:]
        x, y, c, _ = _place()
        copies = []
        for k in range(n):
            hr = pieces[k].shape[1] // 2
            copies.append(pltpu.make_async_remote_copy(
                src_ref=srcs[k].at[:, pl.ds((1 - c) * hr, hr), :], dst_ref=dsts[k],
                send_sem=send_sems.at[k], recv_sem=recv_sems.at[k], device_id=(x, y, 1 - c), device_id_type=MESH))
        for cp in copies:
            cp.start()
        for cp in copies:
            cp.wait()

    return pl.pallas_call(
        body, name=name, out_shape=[_sds((N_SHARD, a.shape[1] // 2, a.shape[2])) for a in pieces],
        in_specs=[ANY] * n, out_specs=[ANY] * n,
        scratch_shapes=[pltpu.SemaphoreType.DMA((n,)), pltpu.SemaphoreType.DMA((n,))])(*pieces)


def _add_half(piece, other, place, dtype, name):
    n, hr, cols = other.shape
    tr = _row_tile(hr)

    def body(pl_ref, a_ref, b_ref, o_ref):
        o_ref[...] = (a_ref[...] + b_ref[...]).astype(dtype)

    tile = pl.BlockSpec((None, tr, cols), lambda s, i, pl_ref: (s, i, 0))
    return pl.pallas_call(
        body, name=name, out_shape=_sds(other.shape, dtype),
        grid_spec=pltpu.PrefetchScalarGridSpec(
            num_scalar_prefetch=1, grid=(n, hr // tr),
            in_specs=[pl.BlockSpec((None, None, tr, cols), lambda s, i, pl_ref: (s, pl_ref[1], i, 0)), tile],
            out_specs=tile))(place, piece.reshape(n, 2, hr, cols), other)


def _scatter_sems(n):
    return [pltpu.SemaphoreType.DMA((3 * n,)), pltpu.SemaphoreType.DMA((3 * n,))]


def _scatter_steps(srcs, dsts, send_sems, recv_sems):
    n = len(srcs)
    x, y, c, chips = _place()
    me = 2 * x + y

    def ici(k, j, src_slot, dst_slot):
        px, py = chips[j]
        return pltpu.make_async_remote_copy(
            src_ref=srcs[k].at[src_slot], dst_ref=dsts[k].at[dst_slot], send_sem=send_sems.at[3 * k + j],
            recv_sem=recv_sems.at[3 * k + j], device_id=(px, py, c), device_id_type=MESH)

    def start():
        for k in range(n):
            for j, (px, py) in enumerate(chips):
                ici(k, j, 2 * px + py, me).start()

    def finish():
        for k in range(n):
            for j, (px, py) in enumerate(chips):
                ici(k, j, me, 2 * px + py).wait_recv()
        for k in range(n):
            for j, (px, py) in enumerate(chips):
                ici(k, j, 2 * px + py, me).wait_send()

    return start, finish


def _scatter_chips(parts, name):
    n = len(parts)

    def body(*refs):
        start, finish = _scatter_steps(refs[:n], refs[n:2 * n], *refs[2 * n:])
        start()
        finish()

    return pl.pallas_call(
        body, name=name, out_shape=[_sds(a.shape, a.dtype) for a in parts],
        in_specs=[ANY] * n, out_specs=[ANY] * n, scratch_shapes=_scatter_sems(n))(*parts)


def _sum_chips(landed, own, place, name):
    _, r, cols = landed.shape
    tr = _row_tile(r)

    def body(pl_ref, q_ref, p_ref, o_ref):
        me = pl_ref[0]
        f = lambda j: jnp.where(me == j, p_ref[...], q_ref[j]).astype(F32)
        o_ref[...] = ((f(0) + f(1)) + f(2)) + f(3)

    return pl.pallas_call(
        body, name=name, out_shape=_sds((2, r, cols)),
        grid_spec=pltpu.PrefetchScalarGridSpec(
            num_scalar_prefetch=1, grid=(r // tr,),
            in_specs=[pl.BlockSpec((N_SHARD, tr, cols), lambda i, pl_ref: (0, i, 0)),
                      pl.BlockSpec((None, tr, cols), lambda i, pl_ref: (pl_ref[0], i, 0))],
            out_specs=pl.BlockSpec((None, tr, cols), lambda i, pl_ref: (pl_ref[1], i, 0))))(place, landed, own)


def _join_halves(bufs):
    n = len(bufs)

    def body(*refs):
        dsts = refs[n:2 * n]
        send_sems, recv_sems = refs[2 * n:]
        x, y, c, _ = _place()
        copies = [pltpu.make_async_remote_copy(
            src_ref=dsts[k].at[c], dst_ref=dsts[k].at[c], send_sem=send_sems.at[k], recv_sem=recv_sems.at[k],
            device_id=(x, y, 1 - c), device_id_type=MESH) for k in range(n)]
        for cp in copies:
            cp.start()
        for cp in copies:
            cp.wait()

    return pl.pallas_call(
        body, name="grads_join_halves", out_shape=[_sds(a.shape) for a in bufs], in_specs=[ANY] * n, out_specs=[ANY] * n,
        input_output_aliases={k: k for k in range(n)},
        scratch_shapes=[pltpu.SemaphoreType.DMA((n,)), pltpu.SemaphoreType.DMA((n,))])(*bufs)


GATHER_F32 = ("ln_gain", "ln_bias", "pool_b", "gdn_conv")
REPLICATED = ("pool_scale", "gdn_a_log", "gdn_dt_bias", "gdn_norm_w", "ple_gate_b")
WEIGHTS = ("ln_gain", "ln_bias", "pool_w", "pool_b", "pool_scale", "gdn_w_in", "gdn_conv", "gdn_a_log", "gdn_dt_bias",
           "gdn_norm_w", "gdn_w_out", "mlp_w1", "mlp_w2", "ple_gate_w", "ple_gate_b", "ple_proj")
SMALL_GRADS = ("ple_proj", "pool_w", "ln_gain", "ln_bias", "pool_b", "gdn_conv") + REPLICATED


def _pack(parts, lanes, row_multiple):
    flat = jnp.concatenate([a.reshape(-1) for a in parts])
    rows = -(-flat.shape[0] // (2 * lanes * row_multiple)) * row_multiple
    return jnp.pad(flat, (0, 2 * rows * lanes - flat.shape[0])).reshape(2, rows, lanes)


def _unpack(flat, shapes):
    out, off = [], 0
    for shp in shapes:
        n = math.prod(shp)
        out.append(flat[..., off:off + n].reshape(flat.shape[:-1] + tuple(shp)))
        off += n
    return out


def _pad_lanes(a, offset, width=128):
    return jnp.pad(a, ((0, 0), (offset, width - offset - a.shape[1])))


def kernel(x, p, ln_gain, ln_bias, pool_w, pool_b, pool_scale, gdn_w_in, gdn_conv, gdn_a_log, gdn_dt_bias, gdn_norm_w, gdn_w_out, mlp_w1, mlp_w2, ple_gate_w, ple_gate_b, ple_proj, loss_target, m_ln_gain, m_ln_bias, m_pool_w, m_pool_b, m_pool_scale, m_gdn_w_in, m_gdn_conv, m_gdn_a_log, m_gdn_dt_bias, m_gdn_norm_w, m_gdn_w_out, m_mlp_w1, m_mlp_w2, m_ple_gate_w, m_ple_gate_b, m_ple_proj, v_ln_gain, v_ln_bias, v_pool_w, v_pool_b, v_pool_scale, v_gdn_w_in, v_gdn_conv, v_gdn_a_log, v_gdn_dt_bias, v_gdn_norm_w, v_gdn_w_out, v_mlp_w1, v_mlp_w2, v_ple_gate_w, v_ple_gate_b, v_ple_proj):
    shard = dict(ln_gain=ln_gain, ln_bias=ln_bias, pool_w=pool_w, pool_b=pool_b, pool_scale=pool_scale, gdn_w_in=gdn_w_in,
                 gdn_conv=gdn_conv, gdn_a_log=gdn_a_log, gdn_dt_bias=gdn_dt_bias, gdn_norm_w=gdn_norm_w, gdn_w_out=gdn_w_out,
                 mlp_w1=mlp_w1, mlp_w2=mlp_w2, ple_gate_w=ple_gate_w, ple_gate_b=ple_gate_b, ple_proj=ple_proj)
    mom = dict(ln_gain=m_ln_gain, ln_bias=m_ln_bias, pool_w=m_pool_w, pool_b=m_pool_b, pool_scale=m_pool_scale,
               gdn_w_in=m_gdn_w_in, gdn_conv=m_gdn_conv, gdn_a_log=m_gdn_a_log, gdn_dt_bias=m_gdn_dt_bias,
               gdn_norm_w=m_gdn_norm_w, gdn_w_out=m_gdn_w_out, mlp_w1=m_mlp_w1, mlp_w2=m_mlp_w2, ple_gate_w=m_ple_gate_w,
               ple_gate_b=m_ple_gate_b, ple_proj=m_ple_proj)
    var = dict(ln_gain=v_ln_gain, ln_bias=v_ln_bias, pool_w=v_pool_w, pool_b=v_pool_b, pool_scale=v_pool_scale,
               gdn_w_in=v_gdn_w_in, gdn_conv=v_gdn_conv, gdn_a_log=v_gdn_a_log, gdn_dt_bias=v_gdn_dt_bias,
               gdn_norm_w=v_gdn_norm_w, gdn_w_out=v_gdn_w_out, mlp_w1=v_mlp_w1, mlp_w2=v_mlp_w2, ple_gate_w=v_ple_gate_w,
               ple_gate_b=v_ple_gate_b, ple_proj=v_ple_proj)

    xi, yi, ci = lax.axis_index("x"), lax.axis_index("y"), lax.axis_index("c")
    me = (2 * xi + yi).reshape(1).astype(jnp.int32)
    place = jnp.stack([2 * xi + yi, ci]).astype(jnp.int32)
    early = [("mlp_w1", 0), ("mlp_w2", 0), ("ple_gate_w", 0), ("ple_proj", 0), ("pool_w", 0)]
    late = [("mlp_w1", 1), ("mlp_w2", 1), ("ple_gate_w", 1), ("ple_proj", 1), ("gdn_w_out", 0), ("gdn_w_in", 0)]
    halved = lambda n: shard[n].reshape(shard[n].shape[0], 2, -1, shard[n].shape[-1])
    placed = lambda ops, tag: [_place_shard(halved(n), me, BF16, f"place_{tag}_{n}", l) for n, l in ops]
    small_in = _place_shard(_pack([shard[n] for n in GATHER_F32], 128, 8)[None], me, F32, "place_small")
    got_early = _all_gather(placed(early, "early") + [small_in], "weights_all_gather_early")
    placed_late = placed(late, "late")
    st = dict(zip(GATHER_F32, _unpack(got_early[-1].reshape(N_SHARD, -1), [shard[n].shape for n in GATHER_F32])))

    cat_last = lambda a: jnp.moveaxis(a, 0, -2).reshape(a.shape[1:-1] + (N_SHARD * a.shape[-1],))
    gain = cat_last(st["ln_gain"])
    bias = cat_last(st["ln_bias"])
    wp = got_early[4].reshape(N_SHARD, 4, POOL_GROUP // N_SHARD, POOL_GROUP)
    pb = cat_last(st["pool_b"]).reshape(1, D_MODEL)
    ps = pool_scale
    conv_w = cat_last(st["gdn_conv"])[0]
    merged = lambda g: g.reshape(N_SHARD, -1, g.shape[-1])
    mlp_w = lambda i, got: (merged(got[0]), merged(got[1]), merged(got[2]), ple_gate_b[i:i + 1], merged(got[3]))
    alog_l = _pad_lanes(gdn_a_log, HEADS)
    dtb_l = _pad_lanes(gdn_dt_bias, HEADS)
    nw = jnp.tile(gdn_norm_w, (1, HEADS))
    ln = lambda i, k: (gain[i, k][None], bias[i, k][None])

    x0 = x[0]
    p0, p1 = p[0, 0], p[1, 0]

    x1, xh1, rs1 = _pool_fwd(x0, wp, pb, ps, *ln(0, 0))
    (x2, xh2, rs2, a0), got_late = _mlp_fwd(x1, p0, *mlp_w(0, got_early), *ln(0, 1), "mlp_fwd_0", gather=placed_late)
    w_out = got_late[4].reshape(D_MODEL, D_MODEL)
    w_in = _assemble_w_in(merged(got_late[5]))
    proj = _matmul_nn(x2, w_in, "gdn_in_proj", GDN_IN_PAD // 3)
    q, k, v, gcb = _conv_fwd(proj, conv_w, alog_l, dtb_l)
    u, w, qd, kd, qk, gl, t_inv = _gdn_prep(q, k, v, gcb)
    o, states = _gdn_scan(u, w, qd, kd, qk, gl)
    x3, xh3, rs3 = _gdn_out_fwd(o, proj, x2, w_out, nw, *ln(1, 0))
    (dy4, xh4, rs4, a1, loss_l), _ = _mlp_fwd(x3, p1, *mlp_w(1, got_late), *ln(1, 1), "mlp_fwd_1", target=loss_target[0])

    g_gain = [[None, None], [None, None]]
    g_bias = [[None, None], [None, None]]

    def mlp_grads(i, dy, xh, rs, x_mid, a, p_i, got, scatter=()):
        (dx, dh, dzg, dpp, drb, dg, db, dgb), landed = _mlp_bwd(
            dy, xh, rs, x_mid, a, p_i, *mlp_w(i, got), ln(i, 1)[0], f"mlp_bwd_{i}", scatter=scatter)
        g_gain[i][1], g_bias[i][1] = dg, db
        return dx, dict(
            mlp_w1=_wgrad(x_mid, dh, f"dw1_{i}", stack_cols=True), mlp_w2=_wgrad(a, drb, f"dw2_{i}").reshape(N_SHARD, -1, D_MODEL),
            ple_gate_w=_wgrad(x_mid, dzg, f"dgate_w_{i}").reshape(N_SHARD, -1, D_MODEL),
            ple_proj=_wgrad(p_i, dpp, f"dproj_{i}", stack_cols=True), ple_gate_b=dgb), landed

    def chip_sums(pieces, wire, tag):
        others = _swap_halves(pieces, f"grads_swap_halves_{tag}")
        return [_add_half(a, b, place, t, f"grads_add_half_{tag}{i}") for i, (a, b, t) in enumerate(zip(pieces, others, wire))]

    dx3, gl1, _ = mlp_grads(1, dy4, xh4, rs4, x3, a1, p1, got_late)
    dres, do, dz, on_b, drb3, g_gain[1][0], g_bias[1][0], d_nw = _gdn_out_bwd(dx3, xh3, rs3, o, proj, w_out, nw, ln(1, 0)[0])
    d_wout = _wgrad(on_b, drb3, "dw_out").reshape(N_SHARD, -1, D_MODEL)
    du, dw, dqd, dkd, dqk, dgl = _gdn_scan_bwd(do, u, w, qd, kd, qk, gl, states)
    dq, dk, dv, dgcb = _gdn_prep_bwd(q, k, v, gcb, t_inv, u, w, du, dw, dqd, dkd, dqk, dgl)
    dproj, d_conv, d_alog_l, d_dtb_l = _conv_bwd(proj, dq, dk, dv, dgcb, dz, conv_w, alog_l, dtb_l)
    dx2 = _matmul_nt_add(dproj, w_in, dres, "gdn_in_bwd")
    d_win = _split_w_in(_wgrad(x2, dproj, "dw_in"))
    sums_late = chip_sums([gl1["mlp_w1"], gl1["mlp_w2"], gl1["ple_gate_w"], d_wout, d_win], [BF16] * 5, "late")
    dx1, gl0, landed_late = mlp_grads(0, dx2, xh2, rs2, x1, a0, p0, got_early, scatter=sums_late)
    sums_early = chip_sums([gl0["mlp_w1"], gl0["mlp_w2"], gl0["ple_gate_w"]], [BF16] * 3, "early")
    (dx0, g_gain[0][0], g_bias[0][0], d_ps, d_pb, d_wp), landed_early = _pool_bwd(
        dx1, xh1, rs1, x0, wp, pb, ps, ln(0, 0)[0], scatter=sums_early)

    split_last = lambda a: jnp.moveaxis(a.reshape(a.shape[:-1] + (N_SHARD, a.shape[-1] // N_SHARD)), -2, 0)
    small_st = dict(
        ple_proj=jnp.stack([gl0["ple_proj"], gl1["ple_proj"]], axis=1),
        pool_w=jnp.moveaxis(d_wp.reshape(4, N_SHARD, POOL_GROUP // N_SHARD, POOL_GROUP), 1, 0)[:, None],
        ln_gain=split_last(jnp.stack([jnp.concatenate(r, axis=0) for r in g_gain])),
        ln_bias=split_last(jnp.stack([jnp.concatenate(r, axis=0) for r in g_bias])),
        pool_b=split_last(d_pb.reshape(1, 4, POOL_GROUP)),
        gdn_conv=split_last(d_conv)[:, None],
    )
    rep = dict(pool_scale=d_ps, gdn_a_log=d_alog_l[:, HEADS:2 * HEADS], gdn_dt_bias=d_dtb_l[:, HEADS:2 * HEADS],
               gdn_norm_w=d_nw, ple_gate_b=jnp.concatenate([gl0["ple_gate_b"], gl1["ple_gate_b"]], axis=0))
    for n in REPLICATED:
        small_st[n] = jnp.broadcast_to(rep[n][None], (N_SHARD,) + rep[n].shape)
    small_flat = jnp.concatenate([small_st[n].reshape(N_SHARD, -1) for n in SMALL_GRADS], axis=1)
    small_rows = -(-small_flat.shape[1] // (16 * LANES)) * 16
    small_piece = jnp.pad(small_flat, ((0, 0), (0, small_rows * LANES - small_flat.shape[1]))).reshape(N_SHARD, small_rows, LANES)

    sums_small = chip_sums([small_piece], [F32], "small")
    landed_small = _scatter_chips(sums_small, "grads_scatter_chips_small")
    red = _join_halves([_sum_chips(q_, p_, place, f"grads_sum_chips_{i}") for i, (q_, p_) in
                        enumerate(zip(list(landed_early) + list(landed_small) + list(landed_late),
                                      sums_early + sums_small + sums_late))])
    red = [r.reshape(-1, r.shape[-1]) for r in red]
    grads = dict(mlp_w1=jnp.stack([red[0], red[4]]), mlp_w2=jnp.stack([red[1], red[5]]), ple_gate_w=jnp.stack([red[2], red[6]]),
                 gdn_w_out=red[7][None], gdn_w_in=red[8][None])
    grads.update(zip(SMALL_GRADS, _unpack(red[3].reshape(-1), [shard[n].shape for n in SMALL_GRADS])))

    delta, new_m, new_v = {}, {}, {}
    small = [n for n in WEIGHTS if shard[n].size < 128 * 128]
    for n in WEIGHTS:
        if n in small:
            continue
        to2d = lambda a, n=n: a.reshape(-1, shard[n].shape[-1])
        d2, m2, v2 = _adamw(to2d(shard[n]), to2d(grads[n]), to2d(mom[n]), to2d(var[n]), "adamw_" + n)
        delta[n], new_m[n], new_v[n] = (t.reshape(shard[n].shape) for t in (d2, m2, v2))
    pk = lambda d: _pack([d[n] for n in small], 128, 8).reshape(-1, 128)
    d2, m2, v2 = _adamw(pk(shard), pk(grads), pk(mom), pk(var), "adamw_small")
    for dst, t in ((delta, d2), (new_m, m2), (new_v, v2)):
        dst.update(zip(small, _unpack(t.reshape(-1), [shard[n].shape for n in small])))

    loss = lax.psum(loss_l[0, 0], ("x", "y", "c"))
    return (loss, dx0[None], *[grads[n] for n in WEIGHTS], *[delta[n] for n in WEIGHTS],
            *[new_m[n] for n in WEIGHTS], *[new_v[n] for n in WEIGHTS])
```

```python
import math

import jax
import jax.numpy as jnp
from jax import lax
from jax.experimental import pallas as pl
from jax.experimental.pallas import tpu as pltpu

F32 = jnp.float32
BF16 = jnp.bfloat16

D_MODEL = 1024
D_FF = 4096
PLE_DIM = 256
N_SHARD = 4
POOL_WINDOWS = (2, 4, 8, 16)
POOL_GROUP = 256
POOL_HALO = 16
HEADS = 8
HEAD_DIM = 128
CHUNK = 64
CONV_WIDTH = 4
CONV_HALO = 8
QKV_DIM = 3 * D_MODEL
GDN_IN_DIM = QKV_DIM + D_MODEL + 2 * HEADS
GDN_IN_PAD = 4224
BA_BLOCK = (QKV_DIM + D_MODEL) // 128
ALPHA = (2.0 * 2) ** 0.25
LN_EPS = 1e-5
RMS_EPS = 1e-6
L2_EPS = 1e-6
ADAM_LR, ADAM_B1, ADAM_B2, ADAM_EPS, ADAM_WD, ADAM_STEP = 0.001, 0.9, 0.999, 1e-08, 0.01, 10

ROW_TILE = 512
CONV_TILE = 256
PREP_CHUNKS = 16
SCAN_CHUNKS = 4
LANES = 1024
ADAM_ROWS = 256

NN = (((1,), (0,)), ((), ()))
NT = (((1,), (1,)), ((), ()))
TN = (((0,), (0,)), ((), ()))
BNN = (((2,), (1,)), ((0,), (0,)))
BNT = (((2,), (2,)), ((0,), (0,)))
BTN = (((1,), (1,)), ((0,), (0,)))
MESH = pl.DeviceIdType.MESH
ANY = pl.BlockSpec(memory_space=pl.ANY)


def _bdot(a, b, dims):
    return lax.dot_general(a.astype(BF16), b.astype(BF16), dims, preferred_element_type=F32)


def _hdot(a, b, dims):
    return lax.dot_general(a, b, dims, precision=lax.Precision.HIGHEST, preferred_element_type=F32)


def _mdot(a, b, dims):
    return lax.dot_general(a, b, dims, precision=lax.Precision.HIGH, preferred_element_type=F32)


def _sigmoid(x):
    return 0.5 * jnp.tanh(0.5 * x) + 0.5


def _silu(x):
    return x * _sigmoid(x)


def _softplus(x):
    return jnp.maximum(x, 0.0) + jnp.log1p(jnp.exp(-jnp.abs(x)))


def _call(body, name, grid, in_specs, out_specs, out_shape, scratch=(), sem=None, aliases=None):
    params = pltpu.CompilerParams(dimension_semantics=sem) if sem else None
    return pl.pallas_call(
        body, name=name, grid=grid, in_specs=in_specs, out_specs=out_specs, out_shape=out_shape,
        scratch_shapes=list(scratch), compiler_params=params, input_output_aliases=aliases or {})


def _row(d):
    return pl.BlockSpec((1, d), lambda *_: (0, 0))


def _full(shape):
    n = len(shape)
    return pl.BlockSpec(shape, lambda *_: (0,) * n)


def _sds(shape, dtype=F32):
    return jax.ShapeDtypeStruct(shape, dtype)


def _ln_fwd(r, gain, bias):
    mu = jnp.mean(r, axis=-1, keepdims=True)
    xc = r - mu
    rstd = lax.rsqrt(jnp.mean(xc * xc, axis=-1, keepdims=True) + LN_EPS)
    xhat = xc * rstd
    return xhat * gain + bias, xhat, rstd


def _ln_bwd(dy, xhat, rstd, gain):
    dxh = dy * gain
    m1 = jnp.mean(dxh, axis=-1, keepdims=True)
    m2 = jnp.mean(dxh * xhat, axis=-1, keepdims=True)
    return rstd * (dxh - m1 - xhat * m2)


def _acc(ref, first, val):
    @pl.when(first)
    def _():
        ref[...] = val

    @pl.when(jnp.logical_not(first))
    def _():
        ref[...] += val


def _pooled_groups(xe, t0, ts):
    pos = (t0 + lax.broadcasted_iota(jnp.int32, (ts, 1), 0) + 1).astype(F32)
    outs = []
    for gi, win in enumerate(POOL_WINDOWS):
        xs = xe[:, gi * POOL_GROUP:(gi + 1) * POOL_GROUP]
        s, k = xs, 1
        while k < win:
            s = s + pltpu.roll(s, k, 0)
            k *= 2
        mean = s[POOL_HALO:] / jnp.minimum(pos, float(win))
        outs.append(mean - xs[POOL_HALO:])
    return outs


def _pool_groups_w(w_ref):
    return [jnp.concatenate([w_ref[s, g] for s in range(N_SHARD)], axis=0) for g in range(4)]


def _pool_fwd(x, wp, pb, ps, gain, bias):
    s_len = x.shape[0]
    ts = min(ROW_TILE, s_len)
    hb = ts // POOL_HALO

    def body(x_ref, halo_ref, w_ref, pb_ref, ps_ref, g_ref, b_ref, y_ref, xhat_ref, rstd_ref):
        i = pl.program_id(0)
        x_t = x_ref[...]
        halo = jnp.where(i > 0, halo_ref[...], 0.0)
        pooled = _pooled_groups(jnp.concatenate([halo, x_t], axis=0), i * ts, ts)
        wg = _pool_groups_w(w_ref)
        y = jnp.concatenate([_bdot(pooled[g], wg[g], NN) for g in range(4)], axis=1) + pb_ref[...]
        r = ALPHA * x_t + y * ps_ref[...]
        y_ref[...], xhat_ref[...], rstd_ref[...] = _ln_fwd(r, g_ref[...], b_ref[...])

    tile = pl.BlockSpec((ts, D_MODEL), lambda i: (i, 0))
    return _call(
        body, "pool_fwd", (s_len // ts,),
        [tile, pl.BlockSpec((POOL_HALO, D_MODEL), lambda i: (jnp.maximum(i * hb - 1, 0), 0)),
         _full(wp.shape), _row(D_MODEL), _row(D_MODEL), _row(D_MODEL), _row(D_MODEL)],
        [tile, tile, pl.BlockSpec((ts, 1), lambda i: (i, 0))],
        [_sds((s_len, D_MODEL)), _sds((s_len, D_MODEL)), _sds((s_len, 1))],
        sem=("parallel",))(x, x, wp, pb, ps, gain, bias)


def _pool_bwd(dy, xhat, rstd, x, wp, pb, ps, gain, scatter=()):
    s_len = x.shape[0]
    ts = min(ROW_TILE, s_len)
    hb = ts // POOL_HALO
    n_t = s_len // ts
    ne = ts + POOL_HALO
    n_s = len(scatter)

    def body(*refs):
        dy_ref, dyn_ref, xh_ref, xhn_ref, rs_ref, rsn_ref, x_ref, xp_ref, w_ref, pb_ref, ps_ref, g_ref = refs[:12]
        dx_ref, dg_ref, db_ref, dps_ref, dpb_ref, dw_ref = refs[12 + n_s:18 + n_s]
        i = pl.program_id(0)
        if n_s:
            start, finish = _scatter_steps(refs[12:12 + n_s], refs[18 + n_s:18 + 2 * n_s], *refs[18 + 2 * n_s:])
            pl.when(i == 0)(start)
        more = i < n_t - 1
        dy_t, xh_t = dy_ref[...], xh_ref[...]
        dy_e = jnp.concatenate([dy_t, jnp.where(more, dyn_ref[...], 0.0)], axis=0)
        xh_e = jnp.concatenate([xh_t, xhn_ref[...]], axis=0)
        rs_e = jnp.concatenate([rs_ref[...], rsn_ref[...]], axis=0)
        dr_e = _ln_bwd(dy_e, xh_e, rs_e, g_ref[...])
        dyy_e = dr_e * ps_ref[...]
        pos_e = (i * ts + lax.broadcasted_iota(jnp.int32, (ne, 1), 0) + 1).astype(F32)
        dxs = []
        wg = _pool_groups_w(w_ref)
        for gi, win in enumerate(POOL_WINDOWS):
            sl = slice(gi * POOL_GROUP, (gi + 1) * POOL_GROUP)
            dpool = _bdot(dyy_e[:, sl], wg[gi], NT)
            s, k = dpool / jnp.minimum(pos_e, float(win)), 1
            while k < win:
                s = s + pltpu.roll(s, ne - k, 0)
                k *= 2
            dxs.append(s[:ts] - dpool[:ts])
        dx_ref[...] = ALPHA * dr_e[:ts] + jnp.concatenate(dxs, axis=1)

        x_t = x_ref[...]
        halo = jnp.where(i > 0, xp_ref[...], 0.0)
        pooled = _pooled_groups(jnp.concatenate([halo, x_t], axis=0), i * ts, ts)
        y = jnp.concatenate([_bdot(pooled[g], wg[g], NN) for g in range(4)], axis=1) + pb_ref[...]
        dr_t, dyy_t = dr_e[:ts], dyy_e[:ts]
        first = i == 0
        _acc(dg_ref, first, jnp.sum(dy_t * xh_t, axis=0, keepdims=True))
        _acc(db_ref, first, jnp.sum(dy_t, axis=0, keepdims=True))
        _acc(dps_ref, first, jnp.sum(dr_t * y, axis=0, keepdims=True))
        _acc(dpb_ref, first, jnp.sum(dyy_t, axis=0, keepdims=True))
        for g in range(4):
            _acc(dw_ref.at[g], first, _bdot(pooled[g], dyy_t[:, g * POOL_GROUP:(g + 1) * POOL_GROUP], TN))
        if n_s:
            pl.when(i == n_t - 1)(finish)

    tile = pl.BlockSpec((ts, D_MODEL), lambda i: (i, 0))
    nxt = pl.BlockSpec((POOL_HALO, D_MODEL), lambda i: (jnp.minimum((i + 1) * hb, n_t * hb - 1), 0))
    prv = pl.BlockSpec((POOL_HALO, D_MODEL), lambda i: (jnp.maximum(i * hb - 1, 0), 0))
    rs_t = pl.BlockSpec((ts, 1), lambda i: (i, 0))
    rs_n = pl.BlockSpec((POOL_HALO, 1), lambda i: (jnp.minimum((i + 1) * hb, n_t * hb - 1), 0))
    row = _row(D_MODEL)
    out = _call(
        body, "pool_bwd", (n_t,),
        [tile, nxt, tile, nxt, rs_t, rs_n, tile, prv, _full(wp.shape), row, row, row] + [ANY] * n_s,
        [tile, row, row, row, row, _full((4, POOL_GROUP, POOL_GROUP))] + [ANY] * n_s,
        [_sds((s_len, D_MODEL))] + [_sds((1, D_MODEL))] * 4 + [_sds((4, POOL_GROUP, POOL_GROUP))]
        + [_sds(t.shape, t.dtype) for t in scatter],
        scratch=_scatter_sems(n_s) if n_s else [], sem=("arbitrary",),
    )(dy, dy, xhat, xhat, rstd, rstd, x, x, wp, pb, ps, gain, *scatter)
    return out[:6], out[6:]


def _mlp_weight_specs():
    fc = D_FF // N_SHARD
    return [pl.BlockSpec((None, D_MODEL, fc), lambda i, j: (j, 0, 0)),
            pl.BlockSpec((None, fc, D_MODEL), lambda i, j: (j, 0, 0)),
            _full((N_SHARD, D_MODEL // N_SHARD, D_MODEL)),
            _row(D_MODEL),
            _full((N_SHARD, PLE_DIM, D_MODEL // N_SHARD))]


def _gate_w(gw_ref):
    return gw_ref[...].reshape(D_MODEL, D_MODEL)


def _ple_proj(pj_ref):
    return jnp.concatenate([pj_ref[s] for s in range(N_SHARD)], axis=1)


def _mlp_fwd(x, p, w1s, w2s, gw, gb, proj, gain, bias, name, gather=(), target=None):
    s_len = x.shape[0]
    ts = min(ROW_TILE, s_len)
    n_i = s_len // ts
    n_g = len(gather)
    has_t = target is not None
    n_in, n_out = 9 + has_t, 4 + has_t

    def body(*refs):
        x_ref, p_ref, w1_ref, w2_ref, gw_ref, gb_ref, pj_ref, g_ref, b_ref = refs[:9]
        y_ref, xhat_ref, rstd_ref, a_ref = refs[n_in + n_g:n_in + n_g + 4]
        acc_ref, xb_ref = refs[n_in + n_out + 2 * n_g:n_in + n_out + 2 * n_g + 2]
        i, j = pl.program_id(0), pl.program_id(1)
        if n_g:
            start, finish = _gather_steps(refs[n_in + n_out + n_g:n_in + n_out + 2 * n_g], *refs[n_in + n_out + 2 * n_g + 2:])
            pl.when((i == 0) & (j == 0))(start)

        @pl.when(j == 0)
        def _():
            x_t = x_ref[...]
            xb_ref[...] = x_t.astype(BF16)
            gate = _sigmoid(_bdot(x_t, _gate_w(gw_ref), NN) + gb_ref[...])
            acc_ref[...] = ALPHA * x_t + gate * _bdot(p_ref[...], _ple_proj(pj_ref), NN)

        h = jnp.maximum(_bdot(xb_ref[...], w1_ref[...], NN), 0.0)
        a = (h * h).astype(BF16)
        a_ref[...] = a
        acc_ref[...] += _bdot(a, w2_ref[...], NN)

        @pl.when(j == N_SHARD - 1)
        def _():
            y, xhat_ref[...], rstd_ref[...] = _ln_fwd(acc_ref[...], g_ref[...], b_ref[...])
            if has_t:
                err = y - refs[9][...]
                y_ref[...] = err * (1.0 / D_MODEL)
                part = 0.5 * jnp.sum(jnp.mean(err * err, axis=-1, keepdims=True))
                _acc(refs[n_in + n_g + 4], i == 0, part + jnp.zeros((1, 128), F32))
            else:
                y_ref[...] = y

        if n_g:
            pl.when((i == n_i - 1) & (j == N_SHARD - 1))(finish)

    tile = pl.BlockSpec((ts, D_MODEL), lambda i, j: (i, 0))
    row = _row(D_MODEL)
    out = _call(
        body, name, (n_i, N_SHARD),
        [tile, pl.BlockSpec((ts, PLE_DIM), lambda i, j: (i, 0))] + _mlp_weight_specs() + [row, row] + [tile] * has_t
        + [ANY] * n_g,
        [tile, tile, pl.BlockSpec((ts, 1), lambda i, j: (i, 0)), pl.BlockSpec((ts, D_FF // N_SHARD), lambda i, j: (i, j))]
        + [_row(128)] * has_t + [ANY] * n_g,
        [_sds((s_len, D_MODEL)), _sds((s_len, D_MODEL)), _sds((s_len, 1)), _sds((s_len, D_FF), BF16)]
        + [_sds((1, 128))] * has_t + [_sds(a.shape, a.dtype) for a in gather],
        scratch=[pltpu.VMEM((ts, D_MODEL), F32), pltpu.VMEM((ts, D_MODEL), BF16)] + (_gather_sems(n_g) if n_g else []),
        sem=("arbitrary", "arbitrary"), aliases={n_in + k: n_out + k for k in range(n_g)},
    )(x, p, w1s, w2s, gw, gb, proj, gain, bias, *([target] if has_t else []), *gather)
    return out[:n_out], out[n_out:]


def _mlp_bwd(dy, xhat, rstd, x, a, p, w1s, w2s, gw, gb, proj, gain, name, scatter=()):
    s_len = x.shape[0]
    ts = min(ROW_TILE, s_len)
    fc = D_FF // N_SHARD
    n_i = s_len // ts
    n_s = len(scatter)

    def body(*refs):
        dy_ref, xh_ref, rs_ref, x_ref, a_ref, p_ref, w1_ref, w2_ref, gw_ref, gb_ref, pj_ref, g_ref = refs[:12]
        dx_ref, dh_ref, dzg_ref, dpp_ref, drb_ref, dg_ref, db_ref, dgb_ref = refs[12 + n_s:20 + n_s]
        acc_ref, dr_ref = refs[20 + 2 * n_s:22 + 2 * n_s]
        i, j = pl.program_id(0), pl.program_id(1)
        if n_s:
            start, finish = _scatter_steps(refs[12:12 + n_s], refs[20 + n_s:20 + 2 * n_s], *refs[22 + 2 * n_s:])
            pl.when((i == 0) & (j == 0))(start)

        @pl.when(j == 0)
        def _():
            dy_t, xh_t, x_t = dy_ref[...], xh_ref[...], x_ref[...]
            dr = _ln_bwd(dy_t, xh_t, rs_ref[...], g_ref[...])
            drb = dr.astype(BF16)
            dr_ref[...] = drb
            drb_ref[...] = drb
            gw_full = _gate_w(gw_ref)
            gate = _sigmoid(_bdot(x_t, gw_full, NN) + gb_ref[...])
            pp = _bdot(p_ref[...], _ple_proj(pj_ref), NN)
            dzg = dr * pp * gate * (1.0 - gate)
            dzg_ref[...] = dzg.astype(BF16)
            dpp_ref[...] = (dr * gate).astype(BF16)
            acc_ref[...] = ALPHA * dr + _bdot(dzg, gw_full, NT)
            first = i == 0
            _acc(dg_ref, first, jnp.sum(dy_t * xh_t, axis=0, keepdims=True))
            _acc(db_ref, first, jnp.sum(dy_t, axis=0, keepdims=True))
            _acc(dgb_ref, first, jnp.sum(dzg, axis=0, keepdims=True))

        dh = (_bdot(dr_ref[...], w2_ref[...], NT) * (2.0 * jnp.sqrt(a_ref[...].astype(F32)))).astype(BF16)
        dh_ref[...] = dh
        acc_ref[...] += _bdot(dh, w1_ref[...], NT)

        @pl.when(j == N_SHARD - 1)
        def _():
            dx_ref[...] = acc_ref[...]

        if n_s:
            pl.when((i == n_i - 1) & (j == N_SHARD - 1))(finish)

    tile = pl.BlockSpec((ts, D_MODEL), lambda i, j: (i, 0))
    ftile = pl.BlockSpec((ts, fc), lambda i, j: (i, j))
    row = _row(D_MODEL)
    out = _call(
        body, name, (n_i, N_SHARD),
        [tile, tile, pl.BlockSpec((ts, 1), lambda i, j: (i, 0)), tile, ftile, pl.BlockSpec((ts, PLE_DIM), lambda i, j: (i, 0))]
        + _mlp_weight_specs() + [row] + [ANY] * n_s,
        [tile, ftile, tile, tile, tile, row, row, row] + [ANY] * n_s,
        [_sds((s_len, D_MODEL)), _sds((s_len, D_FF), BF16)]
        + [_sds((s_len, D_MODEL), BF16)] * 3 + [_sds((1, D_MODEL))] * 3 + [_sds(t.shape, t.dtype) for t in scatter],
        scratch=[pltpu.VMEM((ts, D_MODEL), F32), pltpu.VMEM((ts, D_MODEL), BF16)] + (_scatter_sems(n_s) if n_s else []),
        sem=("arbitrary", "arbitrary"))(dy, xhat, rstd, x, a, p, w1s, w2s, gw, gb, proj, gain, *scatter)
    return out[:8], out[8:]


def _wgrad(a, b, name, stack_cols=False):
    s_len, m = a.shape
    n = b.shape[1]
    ts = min(2048 if a.dtype == BF16 and b.dtype == BF16 else 1024, s_len)
    tm = min(m, 1024)
    tn = n // N_SHARD if stack_cols else (1408 if n == GDN_IN_PAD else min(n, 1024))
    n_s = s_len // ts

    def body(a_ref, b_ref, o_ref):
        _acc(o_ref, pl.program_id(2) == 0, _bdot(a_ref[...], b_ref[...], TN))

    if stack_cols:
        out_spec = pl.BlockSpec((None, tm, tn), lambda mi, nj, s: (nj, mi, 0))
        out_shape = _sds((N_SHARD, m, tn))
    else:
        out_spec = pl.BlockSpec((tm, tn), lambda mi, nj, s: (mi, nj))
        out_shape = _sds((m, n))
    return _call(
        body, name, (m // tm, n // tn, n_s),
        [pl.BlockSpec((ts, tm), lambda mi, nj, s: (s, mi)), pl.BlockSpec((ts, tn), lambda mi, nj, s: (s, nj))],
        out_spec, out_shape, sem=("parallel", "parallel", "arbitrary"))(a, b)


def _matmul_nn(a, b, name, tn):
    s_len, k = a.shape
    n = b.shape[1]
    ts = min(512, s_len)

    def body(a_ref, b_ref, o_ref):
        o_ref[...] = _bdot(a_ref[...], b_ref[...], NN)

    return _call(
        body, name, (s_len // ts, n // tn),
        [pl.BlockSpec((ts, k), lambda i, j: (i, 0)), pl.BlockSpec((k, tn), lambda i, j: (0, j))],
        pl.BlockSpec((ts, tn), lambda i, j: (i, j)), _sds((s_len, n)), sem=("parallel", "parallel"))(a, b)


def _act_qkv(y):
    qkv = _silu(y)
    qs, ks = [], []
    for h in range(HEADS):
        qh = qkv[:, h * HEAD_DIM:(h + 1) * HEAD_DIM]
        kh = qkv[:, D_MODEL + h * HEAD_DIM:D_MODEL + (h + 1) * HEAD_DIM]
        qs.append(qh * (lax.rsqrt(jnp.sum(qh * qh, axis=-1, keepdims=True) + L2_EPS) * HEAD_DIM ** -0.5))
        ks.append(kh * lax.rsqrt(jnp.sum(kh * kh, axis=-1, keepdims=True) + L2_EPS))
    return jnp.concatenate(qs, axis=1), jnp.concatenate(ks, axis=1), qkv[:, 2 * D_MODEL:]


def _act_gb(ba, alog_l, dtb_l, tril):
    lane = lax.broadcasted_iota(jnp.int32, ba.shape, 1)
    g = jnp.where((lane >= HEADS) & (lane < 2 * HEADS), -jnp.exp(alog_l) * _softplus(ba + dtb_l), 0.0)
    return jnp.where(lane < HEADS, _sigmoid(ba), _hdot(tril, g, NN))


def _chunk_tril(t):
    ii = lax.broadcasted_iota(jnp.int32, (t, t), 0)
    jj = lax.broadcasted_iota(jnp.int32, (t, t), 1)
    return ((ii // CHUNK == jj // CHUNK) & (ii >= jj)).astype(F32)


def _conv_rows(xe, w, n_rows):
    y = xe[CONV_HALO:CONV_HALO + n_rows] * w[CONV_WIDTH - 1]
    for j in range(CONV_WIDTH - 1):
        y = y + pltpu.roll(xe, CONV_WIDTH - 1 - j, 0)[CONV_HALO:CONV_HALO + n_rows] * w[j]
    return y


def _conv_fwd(proj, conv_w, alog_l, dtb_l):
    s_len = proj.shape[0]
    ts = min(CONV_TILE, s_len)
    hb = ts // CONV_HALO

    def body(x_ref, xp_ref, ba_ref, w_ref, al_ref, dt_ref, q_ref, k_ref, v_ref, gcb_ref):
        i = pl.program_id(0)
        halo = jnp.where(i > 0, xp_ref[...], 0.0)
        taps = [w_ref[pl.ds(j, 1), :] for j in range(CONV_WIDTH)]
        y = _conv_rows(jnp.concatenate([halo, x_ref[...]], axis=0), taps, ts)
        q_ref[...], k_ref[...], v_ref[...] = _act_qkv(y)
        gcb_ref[...] = _act_gb(ba_ref[...], al_ref[...], dt_ref[...], _chunk_tril(ts))

    tile = pl.BlockSpec((ts, D_MODEL), lambda i: (i, 0))
    return _call(
        body, "gdn_conv_fwd", (s_len // ts,),
        [pl.BlockSpec((ts, QKV_DIM), lambda i: (i, 0)),
         pl.BlockSpec((CONV_HALO, QKV_DIM), lambda i: (jnp.maximum(i * hb - 1, 0), 0)),
         pl.BlockSpec((ts, 128), lambda i: (i, BA_BLOCK)), _full((CONV_WIDTH, QKV_DIM)), _row(128), _row(128)],
        [tile, tile, tile, pl.BlockSpec((ts, 128), lambda i: (i, 0))],
        [_sds((s_len, D_MODEL))] * 3 + [_sds((s_len, 128))],
        sem=("parallel",))(proj, proj, proj, conv_w, alog_l, dtb_l)


def _conv_bwd(proj, dq, dk, dv, dgcb, dz, conv_w, alog_l, dtb_l, w_in, dres):
    s_len = proj.shape[0]
    ts = min(CONV_TILE, s_len)
    hb = ts // CONV_HALO
    n_t = s_len // ts
    te = ts + CONV_HALO

    def body(x_ref, xp_ref, xn_ref, ba_ref, dq_ref, dqn_ref, dk_ref, dkn_ref, dv_ref, dvn_ref, dgcb_ref, dz_ref,
             w_ref, al_ref, dt_ref, win_hbm, dres_ref, dp_ref, dx_ref, dw_ref, dal_ref, ddt_ref, win_ref, win_sem):
        i = pl.program_id(0)

        @pl.when(i == 0)
        def _():
            cp = pltpu.make_async_copy(win_hbm, win_ref, win_sem)
            cp.start()
            cp.wait()

        more = i < n_t - 1
        w = [w_ref[pl.ds(j, 1), :] for j in range(CONV_WIDTH)]
        x_t = x_ref[...]
        xe = jnp.concatenate([jnp.where(i > 0, xp_ref[...], 0.0), x_t, xn_ref[...]], axis=0)
        y_e, act_vjp = jax.vjp(_act_qkv, _conv_rows(xe, w, te))
        ct = tuple(jnp.concatenate([t[...], jnp.where(more, n[...], 0.0)], axis=0)
                   for t, n in ((dq_ref, dqn_ref), (dk_ref, dkn_ref), (dv_ref, dvn_ref)))
        (dy_e,) = act_vjp(ct)
        ahead = [pltpu.roll(dy_e, te - (CONV_WIDTH - 1 - j), 0)[:ts] for j in range(CONV_WIDTH - 1)] + [dy_e[:ts]]
        dx = ahead[CONV_WIDTH - 1] * w[CONV_WIDTH - 1]
        for j in range(CONV_WIDTH - 1):
            dx = dx + ahead[j] * w[j]
        dws = [jnp.sum(ahead[j] * x_t, axis=0, keepdims=True) for j in range(CONV_WIDTH)]
        _, gb_vjp = jax.vjp(lambda ba, al, dt: _act_gb(ba, al, dt, _chunk_tril(ts)), ba_ref[...], al_ref[...], dt_ref[...])
        dba, dal, ddt = gb_vjp(dgcb_ref[...])
        dp = jnp.concatenate([dx.astype(BF16), dz_ref[...], dba.astype(BF16)], axis=1)
        dp_ref[...] = dp
        dx_ref[...] = dres_ref[...] + _bdot(dp, win_ref[...], NT)
        first = i == 0
        for j in range(CONV_WIDTH):
            _acc(dw_ref.at[pl.ds(j, 1), :], first, dws[j])
        _acc(dal_ref, first, dal)
        _acc(ddt_ref, first, ddt)

    tile = pl.BlockSpec((ts, D_MODEL), lambda i: (i, 0))
    nxt = pl.BlockSpec((CONV_HALO, D_MODEL), lambda i: (jnp.minimum((i + 1) * hb, n_t * hb - 1), 0))
    return _call(
        body, "gdn_conv_bwd", (n_t,),
        [pl.BlockSpec((ts, QKV_DIM), lambda i: (i, 0)),
         pl.BlockSpec((CONV_HALO, QKV_DIM), lambda i: (jnp.maximum(i * hb - 1, 0), 0)),
         pl.BlockSpec((CONV_HALO, QKV_DIM), lambda i: (jnp.minimum((i + 1) * hb, n_t * hb - 1), 0)),
         pl.BlockSpec((ts, 128), lambda i: (i, BA_BLOCK)),
         tile, nxt, tile, nxt, tile, nxt, pl.BlockSpec((ts, 128), lambda i: (i, 0)), tile,
         _full((CONV_WIDTH, QKV_DIM)), _row(128), _row(128), ANY, tile],
        [pl.BlockSpec((ts, GDN_IN_PAD), lambda i: (i, 0)), tile, _full((CONV_WIDTH, QKV_DIM)), _row(128), _row(128)],
        [_sds((s_len, GDN_IN_PAD), BF16), _sds((s_len, D_MODEL)), _sds((CONV_WIDTH, QKV_DIM)), _sds((1, 128)), _sds((1, 128))],
        scratch=[pltpu.VMEM(w_in.shape, w_in.dtype), pltpu.SemaphoreType.DMA], sem=("arbitrary",),
    )(proj, proj, proj, proj, dq, dq, dk, dk, dv, dv, dgcb, dz, conv_w, alog_l, dtb_l, w_in, dres)


def _tri_inv(a_strict):
    ii = lax.broadcasted_iota(jnp.int32, (CHUNK, CHUNK), 0)
    jj = lax.broadcasted_iota(jnp.int32, (CHUNK, CHUNK), 1)
    x = (ii == jj).astype(F32) - a_strict
    pw = _bdot(a_strict, a_strict, BNN)
    for step in range(5):
        x = x + _bdot(x, pw, BNN)
        if step < 4:
            pw = _bdot(pw, pw, BNN)
    return x


@jax.custom_vjp
def _solved(a_strict, rhs, t, sol):
    return sol


def _solved_fwd(a_strict, rhs, t, sol):
    return sol, (t, sol)


def _solved_bwd(res, dsol):
    t, sol = res
    drhs = _mdot(t, dsol, BTN)
    return -_mdot(drhs, sol, BNT), drhs, jnp.zeros_like(t), jnp.zeros_like(sol)


_solved.defvjp(_solved_fwd, _solved_bwd)


def _prep(q, k, v, gc, beta, solve):
    ii = lax.broadcasted_iota(jnp.int32, (CHUNK, CHUNK), 0)
    jj = lax.broadcasted_iota(jnp.int32, (CHUNK, CHUNK), 1)
    causal, strict = ii >= jj, ii > jj
    gc_row = jnp.sum((ii == jj).astype(F32) * gc, axis=1, keepdims=True)
    decay = jnp.where(causal, jnp.exp(jnp.where(causal, gc - gc_row, 0.0)), 0.0)
    kb = k * beta
    a = jnp.where(strict, _bdot(kb, k, BNT) * decay, 0.0)
    eg = jnp.exp(gc)
    sol = solve(a, jnp.concatenate([v * beta, kb * eg], axis=-1))
    qk = _bdot(q, k, BNT) * decay
    last = lax.broadcasted_iota(jnp.int32, (CHUNK, 1), 0) == CHUNK - 1
    g_last = jnp.sum(jnp.where(last, gc, 0.0), axis=1, keepdims=True)
    kd = k * jnp.exp(g_last - gc)
    gl = jnp.exp(g_last) + jnp.zeros((1, 1, HEAD_DIM), F32)
    return sol[..., :HEAD_DIM], sol[..., HEAD_DIM:], qk, q * eg, kd, gl


def _prep_specs(s_len):
    rows = min(PREP_CHUNKS, s_len // CHUNK) * CHUNK
    m = rows // CHUNK
    hd = pl.BlockSpec((rows, HEAD_DIM), lambda c, h: (c, h))
    gcb = pl.BlockSpec((rows, 128), lambda c, h: (c, 0))
    qk = pl.BlockSpec((None, rows, CHUNK), lambda c, h: (h, c, 0))
    gl = pl.BlockSpec((None, m, HEADS, HEAD_DIM), lambda c, h: (c, 0, 0, 0))
    return rows, m, hd, gcb, qk, gl


def _head_cols(gcb, h, m):
    lane = lax.broadcasted_iota(jnp.int32, gcb.shape, 1)
    pick = lambda at: jnp.sum(jnp.where(lane == at, gcb, 0.0), axis=1, keepdims=True).reshape(m, CHUNK, 1)
    return pick(h + HEADS), pick(h)


def _gdn_prep(q, k, v, gcb):
    s_len = q.shape[0]
    rows, m, hd, gcb_spec, qk_spec, gl_spec = _prep_specs(s_len)

    def body(q_ref, k_ref, v_ref, gcb_ref, u_ref, w_ref, qd_ref, kd_ref, qk_ref, gl_ref, t_ref):
        r3 = lambda ref, d: ref[...].reshape(m, CHUNK, d)
        gc, beta = _head_cols(gcb_ref[...], pl.program_id(1), m)

        def solve(a, rhs):
            t = _tri_inv(a)
            t_ref[...] = t.reshape(rows, CHUNK)
            return _mdot(t, rhs, BNN)

        u, w, qk, qd, kd, gl = _prep(r3(q_ref, HEAD_DIM), r3(k_ref, HEAD_DIM), r3(v_ref, HEAD_DIM), gc, beta, solve)
        u_ref[...] = u.reshape(rows, HEAD_DIM)
        w_ref[...] = w.reshape(rows, HEAD_DIM)
        qd_ref[...] = qd.reshape(rows, HEAD_DIM).astype(BF16)
        kd_ref[...] = kd.reshape(rows, HEAD_DIM).astype(BF16)
        qk_ref[...] = qk.reshape(rows, CHUNK).astype(BF16)
        gl_ref[:, pl.ds(pl.program_id(1), 1), :] = gl

    n_g = s_len // rows
    return _call(
        body, "gdn_prep", (n_g, HEADS), [hd, hd, hd, gcb_spec], [hd, hd, hd, hd, qk_spec, gl_spec, qk_spec],
        [_sds((s_len, D_MODEL))] * 2 + [_sds((s_len, D_MODEL), BF16)] * 2
        + [_sds((HEADS, s_len, CHUNK), BF16), _sds((n_g, m, HEADS, HEAD_DIM)), _sds((HEADS, s_len, CHUNK))],
        sem=("parallel", "arbitrary"))(q, k, v, gcb)


def _gdn_prep_bwd(q, k, v, gcb, t_inv, u, w, du, dw, dqd, dkd, dqk, dgl):
    s_len = q.shape[0]
    rows, m, hd, gcb_spec, qk_spec, gl_spec = _prep_specs(s_len)

    def body(q_ref, k_ref, v_ref, gcb_ref, t_ref, u_ref, w_ref, du_ref, dw_ref, dqd_ref, dkd_ref, dqk_ref, dgl_ref,
             dq_ref, dk_ref, dv_ref, dgcb_ref):
        h = pl.program_id(1)
        r3 = lambda ref, d: ref[...].reshape(m, CHUNK, d)
        gc, beta = _head_cols(gcb_ref[...], h, m)
        t = r3(t_ref, CHUNK)
        sol = jnp.concatenate([r3(u_ref, HEAD_DIM), r3(w_ref, HEAD_DIM)], axis=-1)
        fn = lambda q_, k_, v_, gc_, bt_: _prep(q_, k_, v_, gc_, bt_, lambda a, rhs: _solved(a, rhs, t, sol))
        _, vjp = jax.vjp(fn, r3(q_ref, HEAD_DIM), r3(k_ref, HEAD_DIM), r3(v_ref, HEAD_DIM), gc, beta)
        ct = (r3(du_ref, HEAD_DIM), r3(dw_ref, HEAD_DIM), r3(dqk_ref, CHUNK), r3(dqd_ref, HEAD_DIM), r3(dkd_ref, HEAD_DIM),
              dgl_ref[:, pl.ds(h, 1), :] * (1.0 / HEAD_DIM))
        dq, dk, dv, dgc, dbt = vjp(ct)
        dq_ref[...] = dq.reshape(rows, HEAD_DIM)
        dk_ref[...] = dk.reshape(rows, HEAD_DIM)
        dv_ref[...] = dv.reshape(rows, HEAD_DIM)
        lane = lax.broadcasted_iota(jnp.int32, (rows, 128), 1)
        mine = jnp.where(lane == h, dbt.reshape(rows, 1), 0.0) + jnp.where(lane == h + HEADS, dgc.reshape(rows, 1), 0.0)
        _acc(dgcb_ref, h == 0, mine)

    return _call(
        body, "gdn_prep_bwd", (s_len // rows, HEADS),
        [hd, hd, hd, gcb_spec, qk_spec, hd, hd, hd, hd, hd, hd, qk_spec, gl_spec], [hd, hd, hd, gcb_spec],
        [_sds((s_len, D_MODEL))] * 3 + [_sds((s_len, 128))],
        sem=("parallel", "arbitrary"))(q, k, v, gcb, t_inv, u, w, du, dw, dqd, dkd, dqk, dgl)


def _scan_specs(n_c, m, k, reverse):
    n_b = n_c // k
    at = (lambda n: n_b - 1 - n) if reverse else (lambda n: n)
    row = pl.BlockSpec((k * CHUNK, D_MODEL), lambda n: (at(n), 0))
    qk = pl.BlockSpec((HEADS, k * CHUNK, CHUNK), lambda n: (0, at(n), 0))
    gl = pl.BlockSpec((None, k, HEADS, HEAD_DIM), lambda n: (at(n) // (m // k), at(n) % (m // k), 0, 0))
    st = pl.BlockSpec((k, HEADS, HEAD_DIM, HEAD_DIM), lambda n: (at(n), 0, 0, 0))
    return row, qk, gl, st


def _gdn_scan(u, w, qd, kd, qk, gl):
    s_len = u.shape[0]
    n_c = s_len // CHUNK
    k = min(SCAN_CHUNKS, gl.shape[1])
    row, qk_spec, gl_spec, st_spec = _scan_specs(n_c, gl.shape[1], k, False)

    def body(u_ref, w_ref, qd_ref, kd_ref, qk_ref, gl_ref, o_ref, st_ref, state):
        hs = range(HEADS)
        sl = [slice(h * HEAD_DIM, (h + 1) * HEAD_DIM) for h in hs]
        first = pl.program_id(0) == 0
        s_all = [jnp.where(first, 0.0, state[h]) for h in hs]
        for c in range(k):
            rows = slice(c * CHUNK, (c + 1) * CHUNK)
            s_b = [s.astype(BF16) for s in s_all]
            ws = [_bdot(w_ref[rows, sl[h]], s_b[h], NN) for h in hs]
            qs = [_bdot(qd_ref[rows, sl[h]], s_b[h], NN) for h in hs]
            vn = [(u_ref[rows, sl[h]] - ws[h]).astype(BF16) for h in hs]
            outs = [qs[h] + _bdot(qk_ref[h, rows, :], vn[h], NN) for h in hs]
            nxt = [s_all[h] * gl_ref[c, pl.ds(h, 1), :] + _bdot(kd_ref[rows, sl[h]], vn[h], TN) for h in hs]
            for h in hs:
                st_ref[c, h] = s_all[h]
                o_ref[rows, sl[h]] = outs[h]
            s_all = nxt
        for h in hs:
            state[h] = s_all[h]

    return _call(
        body, "gdn_scan", (n_c // k,), [row, row, row, row, qk_spec, gl_spec], [row, st_spec],
        [_sds((s_len, D_MODEL)), _sds((n_c, HEADS, HEAD_DIM, HEAD_DIM))],
        scratch=[pltpu.VMEM((HEADS, HEAD_DIM, HEAD_DIM), F32)], sem=("arbitrary",))(u, w, qd, kd, qk, gl)


def _gdn_scan_bwd(do, u, w, qd, kd, qk, gl, states):
    s_len = u.shape[0]
    n_c = s_len // CHUNK
    k = min(SCAN_CHUNKS, gl.shape[1])
    row, qk_spec, gl_spec, st_spec = _scan_specs(n_c, gl.shape[1], k, True)

    def body(do_ref, u_ref, w_ref, qd_ref, kd_ref, qk_ref, gl_ref, st_ref,
             du_ref, dw_ref, dqd_ref, dkd_ref, dqk_ref, dgl_ref, dstate):
        hs = range(HEADS)
        sl = [slice(h * HEAD_DIM, (h + 1) * HEAD_DIM) for h in hs]
        first = pl.program_id(0) == 0
        ds_f = [jnp.where(first, 0.0, dstate[h]) for h in hs]
        for c in reversed(range(k)):
            rows = slice(c * CHUNK, (c + 1) * CHUNK)
            s_f = [st_ref[c, h] for h in hs]
            s_b = [s.astype(BF16) for s in s_f]
            ds_b = [d.astype(BF16) for d in ds_f]
            do_b = [do_ref[rows, sl[h]].astype(BF16) for h in hs]
            w_b = [w_ref[rows, sl[h]].astype(BF16) for h in hs]
            ws = [_bdot(w_b[h], s_b[h], NN) for h in hs]
            dvn = [_bdot(qk_ref[h, rows, :], do_b[h], TN) + _bdot(kd_ref[rows, sl[h]], ds_b[h], NN) for h in hs]
            dqd = [_bdot(do_b[h], s_b[h], NT) for h in hs]
            t_do = [_bdot(qd_ref[rows, sl[h]], do_b[h], TN) for h in hs]
            vn = [(u_ref[rows, sl[h]] - ws[h]).astype(BF16) for h in hs]
            dvn_b = [d.astype(BF16) for d in dvn]
            dw = [-_bdot(dvn_b[h], s_b[h], NT) for h in hs]
            dkd = [_bdot(vn[h], ds_b[h], NT) for h in hs]
            dqk = [_bdot(do_b[h], vn[h], NT) for h in hs]
            t_dv = [_bdot(w_b[h], dvn_b[h], TN) for h in hs]
            for h in hs:
                du_ref[rows, sl[h]], dw_ref[rows, sl[h]], dqd_ref[rows, sl[h]], dkd_ref[rows, sl[h]] = dvn[h], dw[h], dqd[h], dkd[h]
                dqk_ref[h, rows, :] = dqk[h]
                dgl_ref[c, pl.ds(h, 1), :] = jnp.sum(s_f[h] * ds_f[h]) + jnp.zeros((1, HEAD_DIM), F32)
            ds_f = [ds_f[h] * gl_ref[c, pl.ds(h, 1), :] + t_do[h] - t_dv[h] for h in hs]
        for h in hs:
            dstate[h] = ds_f[h]

    return _call(
        body, "gdn_scan_bwd", (n_c // k,), [row, row, row, row, row, qk_spec, gl_spec, st_spec],
        [row, row, row, row, qk_spec, gl_spec],
        [_sds((s_len, D_MODEL))] * 4 + [_sds((HEADS, s_len, CHUNK)), _sds(gl.shape)],
        scratch=[pltpu.VMEM((HEADS, HEAD_DIM, HEAD_DIM), F32)], sem=("arbitrary",))(do, u, w, qd, kd, qk, gl, states)


def _gate_norm(o, z, nw):
    outs = []
    for h in range(HEADS):
        oh = o[:, h * HEAD_DIM:(h + 1) * HEAD_DIM]
        outs.append(oh * lax.rsqrt(jnp.mean(oh * oh, axis=-1, keepdims=True) + RMS_EPS))
    return jnp.concatenate(outs, axis=1) * nw * _silu(z)


def _gdn_out_fwd(o, proj, x, w_out, nw, gain, bias):
    s_len = x.shape[0]
    ts = min(ROW_TILE, s_len)

    def body(o_ref, z_ref, x_ref, w_ref, nw_ref, g_ref, b_ref, y_ref, xhat_ref, rstd_ref):
        on = _gate_norm(o_ref[...], z_ref[...], nw_ref[...])
        r = ALPHA * x_ref[...] + _bdot(on, w_ref[...], NN)
        y_ref[...], xhat_ref[...], rstd_ref[...] = _ln_fwd(r, g_ref[...], b_ref[...])

    tile = pl.BlockSpec((ts, D_MODEL), lambda i: (i, 0))
    row = _row(D_MODEL)
    return _call(
        body, "gdn_out_fwd", (s_len // ts,),
        [tile, pl.BlockSpec((ts, D_MODEL), lambda i: (i, QKV_DIM // D_MODEL)), tile, _full((D_MODEL, D_MODEL)), row, row, row],
        [tile, tile, pl.BlockSpec((ts, 1), lambda i: (i, 0))],
        [_sds((s_len, D_MODEL)), _sds((s_len, D_MODEL)), _sds((s_len, 1))], sem=("parallel",))(o, proj, x, w_out, nw, gain, bias)


def _gdn_out_bwd(dy, xhat, rstd, o, proj, w_out, nw, gain):
    s_len = o.shape[0]
    ts = min(ROW_TILE, s_len)

    def body(dy_ref, xh_ref, rs_ref, o_ref, z_ref, w_ref, nw_ref, g_ref,
             dres_ref, do_ref, dz_ref, on_ref, drb_ref, dg_ref, db_ref, dnw_ref):
        dy_t, xh_t = dy_ref[...], xh_ref[...]
        dr = _ln_bwd(dy_t, xh_t, rs_ref[...], g_ref[...])
        dres_ref[...] = ALPHA * dr
        drb_ref[...] = dr.astype(BF16)
        on, vjp = jax.vjp(_gate_norm, o_ref[...], z_ref[...], nw_ref[...])
        on_ref[...] = on.astype(BF16)
        do, dz, dnw = vjp(_bdot(dr, w_ref[...], NT))
        do_ref[...] = do
        dz_ref[...] = dz.astype(BF16)
        first = pl.program_id(0) == 0
        _acc(dg_ref, first, jnp.sum(dy_t * xh_t, axis=0, keepdims=True))
        _acc(db_ref, first, jnp.sum(dy_t, axis=0, keepdims=True))
        _acc(dnw_ref, first, sum(dnw[:, h * HEAD_DIM:(h + 1) * HEAD_DIM] for h in range(HEADS)))

    tile = pl.BlockSpec((ts, D_MODEL), lambda i: (i, 0))
    row = _row(D_MODEL)
    return _call(
        body, "gdn_out_bwd", (s_len // ts,),
        [tile, tile, pl.BlockSpec((ts, 1), lambda i: (i, 0)), tile,
         pl.BlockSpec((ts, D_MODEL), lambda i: (i, QKV_DIM // D_MODEL)), _full((D_MODEL, D_MODEL)), row, row],
        [tile, tile, tile, tile, tile, row, row, _row(HEAD_DIM)],
        [_sds((s_len, D_MODEL))] * 2 + [_sds((s_len, D_MODEL), BF16)] * 3 + [_sds((1, D_MODEL))] * 2 + [_sds((1, HEAD_DIM))],
        sem=("arbitrary",))(dy, xhat, rstd, o, proj, w_out, nw, gain)


def _adamw(w, g, m, v, name):
    r, c = w.shape
    tr = min(ADAM_ROWS, r)

    def body(w_ref, g_ref, m_ref, v_ref, d_ref, nm_ref, nv_ref):
        g_t = g_ref[...]
        nm = ADAM_B1 * m_ref[...] + (1.0 - ADAM_B1) * g_t
        nv = ADAM_B2 * v_ref[...] + (1.0 - ADAM_B2) * (g_t * g_t)
        m_hat = nm / (1.0 - ADAM_B1 ** ADAM_STEP)
        v_hat = nv / (1.0 - ADAM_B2 ** ADAM_STEP)
        d_ref[...] = -ADAM_LR * (m_hat / (jnp.sqrt(v_hat) + ADAM_EPS) + ADAM_WD * w_ref[...])
        nm_ref[...] = nm
        nv_ref[...] = nv

    tile = pl.BlockSpec((tr, c), lambda i: (i, 0))
    return _call(body, name, (r // tr,), [tile] * 4, [tile] * 3, [_sds((r, c))] * 3, sem=("parallel",))(w, g, m, v)


def _assemble_w_in(shards):
    rows = 256
    width = GDN_IN_DIM // N_SHARD

    def body(s_ref, o_ref):
        pad = jnp.zeros((rows, GDN_IN_PAD - GDN_IN_DIM), shards.dtype)
        o_ref[...] = jnp.concatenate([s_ref[j] for j in range(N_SHARD)] + [pad], axis=1)

    return _call(body, "w_in_assemble", (D_MODEL // rows,), [pl.BlockSpec((N_SHARD, rows, width), lambda i: (0, i, 0))],
                 pl.BlockSpec((rows, GDN_IN_PAD), lambda i: (i, 0)), _sds((D_MODEL, GDN_IN_PAD), shards.dtype),
                 sem=("parallel",))(shards)


def _split_w_in(full):
    rows = 256
    width = GDN_IN_DIM // N_SHARD

    def body(f_ref, o_ref):
        f = f_ref[...]
        for j in range(N_SHARD):
            o_ref[j] = f[:, j * width:(j + 1) * width]

    return _call(body, "w_in_split", (D_MODEL // rows,), [pl.BlockSpec((rows, GDN_IN_PAD), lambda i: (i, 0))],
                 pl.BlockSpec((N_SHARD, rows, width), lambda i: (0, i, 0)), _sds((N_SHARD, D_MODEL, width), full.dtype),
                 sem=("parallel",))(full)


def _place():
    x, y, c = lax.axis_index("x"), lax.axis_index("y"), lax.axis_index("c")
    return x, y, c, [(1 - x, y), (x, 1 - y), (1 - x, 1 - y)]


def _row_tile(rows):
    return max(t for t in range(8, min(rows, 640) + 1, 8) if rows % t == 0)


def _place_shard(part, me, dtype, name, layer=0):
    _, _, r, c = part.shape
    tr = _row_tile(r)

    def body(me_ref, p_ref, o_ref):
        o_ref[...] = p_ref[...].astype(dtype)

    return pl.pallas_call(
        body, name=name, out_shape=_sds((N_SHARD, 2, r, c), dtype),
        grid_spec=pltpu.PrefetchScalarGridSpec(
            num_scalar_prefetch=1, grid=(2, r // tr),
            in_specs=[pl.BlockSpec((None, None, tr, c), lambda h, i, me_ref: (layer, h, i, 0))],
            out_specs=pl.BlockSpec((None, None, tr, c), lambda h, i, me_ref: (me_ref[0], h, i, 0))))(me, part)


def _gather_sems(n):
    return [pltpu.SemaphoreType.DMA((6 * n,)), pltpu.SemaphoreType.DMA((6 * n,))]


def _gather_steps(dsts, send_sems, recv_sems):
    n = len(dsts)
    x, y, c, chips = _place()
    me = 2 * x + y
    sibling = (x, y, 1 - c)

    def ici(k, j, slot):
        px, py = chips[j]
        view = dsts[k].at[slot, c]
        return pltpu.make_async_remote_copy(
            src_ref=view, dst_ref=view, send_sem=send_sems.at[6 * k + j],
            recv_sem=recv_sems.at[6 * k + j], device_id=(px, py, c), device_id_type=MESH)

    def d2d(k, j, half):
        px, py = chips[j]
        view = dsts[k].at[2 * px + py, half]
        return pltpu.make_async_remote_copy(
            src_ref=view, dst_ref=view, send_sem=send_sems.at[6 * k + 3 + j], recv_sem=recv_sems.at[6 * k + 3 + j],
            device_id=sibling, device_id_type=MESH)

    def start():
        for k in range(n):
            for j in range(3):
                ici(k, j, me).start()

    def finish():
        fwds = []
        for k in range(n):
            for j, (px, py) in enumerate(chips):
                ici(k, j, 2 * px + py).wait_recv()
                fwds.append(d2d(k, j, c))
                fwds[-1].start()
        for k in range(n):
            for j in range(3):
                d2d(k, j, 1 - c).wait_recv()
        for k in range(n):
            for j in range(3):
                ici(k, j, me).wait_send()
        for cp in fwds:
            cp.wait_send()

    return start, finish


def _all_gather(bufs, name):
    n = len(bufs)

    def body(*refs):
        start, finish = _gather_steps(refs[n:2 * n], *refs[2 * n:])
        start()
        finish()

    return pl.pallas_call(
        body, name=name, out_shape=[_sds(a.shape, a.dtype) for a in bufs],
        in_specs=[ANY] * n, out_specs=[ANY] * n, input_output_aliases={k: k for k in range(n)},
        scratch_shapes=_gather_sems(n))(*bufs)


def _swap_halves(pieces, name):
    n = len(pieces)

    def body(*refs):
        srcs, dsts = refs[:n], refs[n:2 * n]
        send_sems, recv_sems = refs[2 * n:]
        x, y, c, _ = _place()
        copies = []
        for k in range(n):
            hr = pieces[k].shape[1] // 2
            copies.append(pltpu.make_async_remote_copy(
                src_ref=srcs[k].at[:, pl.ds((1 - c) * hr, hr), :], dst_ref=dsts[k],
                send_sem=send_sems.at[k], recv_sem=recv_sems.at[k], device_id=(x, y, 1 - c), device_id_type=MESH))
        for cp in copies:
            cp.start()
        for cp in copies:
            cp.wait()

    return pl.pallas_call(
        body, name=name, out_shape=[_sds((N_SHARD, a.shape[1] // 2, a.shape[2])) for a in pieces],
        in_specs=[ANY] * n, out_specs=[ANY] * n,
        scratch_shapes=[pltpu.SemaphoreType.DMA((n,)), pltpu.SemaphoreType.DMA((n,))])(*pieces)


def _add_half(piece, other, place, dtype, name):
    n, hr, cols = other.shape
    tr = _row_tile(hr)

    def body(pl_ref, a_ref, b_ref, o_ref):
        o_ref[...] = (a_ref[...] + b_ref[...]).astype(dtype)

    tile = pl.BlockSpec((None, tr, cols), lambda s, i, pl_ref: (s, i, 0))
    return pl.pallas_call(
        body, name=name, out_shape=_sds(other.shape, dtype),
        grid_spec=pltpu.PrefetchScalarGridSpec(
            num_scalar_prefetch=1, grid=(n, hr // tr),
            in_specs=[pl.BlockSpec((None, None, tr, cols), lambda s, i, pl_ref: (s, pl_ref[1], i, 0)), tile],
            out_specs=tile))(place, piece.reshape(n, 2, hr, cols), other)


def _scatter_sems(n):
    return [pltpu.SemaphoreType.DMA((3 * n,)), pltpu.SemaphoreType.DMA((3 * n,))]


def _scatter_steps(srcs, dsts, send_sems, recv_sems):
    n = len(srcs)
    x, y, c, chips = _place()
    me = 2 * x + y

    def ici(k, j, src_slot, dst_slot):
        px, py = chips[j]
        return pltpu.make_async_remote_copy(
            src_ref=srcs[k].at[src_slot], dst_ref=dsts[k].at[dst_slot], send_sem=send_sems.at[3 * k + j],
            recv_sem=recv_sems.at[3 * k + j], device_id=(px, py, c), device_id_type=MESH)

    def start():
        for k in range(n):
            for j, (px, py) in enumerate(chips):
                ici(k, j, 2 * px + py, me).start()

    def finish():
        for k in range(n):
            for j, (px, py) in enumerate(chips):
                ici(k, j, me, 2 * px + py).wait_recv()
        for k in range(n):
            for j, (px, py) in enumerate(chips):
                ici(k, j, 2 * px + py, me).wait_send()

    return start, finish


def _scatter_chips(parts, name):
    n = len(parts)

    def body(*refs):
        start, finish = _scatter_steps(refs[:n], refs[n:2 * n], *refs[2 * n:])
        start()
        finish()

    return pl.pallas_call(
        body, name=name, out_shape=[_sds(a.shape, a.dtype) for a in parts],
        in_specs=[ANY] * n, out_specs=[ANY] * n, scratch_shapes=_scatter_sems(n))(*parts)


def _sum_chips(landed, own, place, name):
    _, r, cols = landed.shape
    tr = _row_tile(r)

    def body(pl_ref, q_ref, p_ref, o_ref):
        me = pl_ref[0]
        f = lambda j: jnp.where(me == j, p_ref[...], q_ref[j]).astype(F32)
        o_ref[...] = ((f(0) + f(1)) + f(2)) + f(3)

    return pl.pallas_call(
        body, name=name, out_shape=_sds((2, r, cols)),
        grid_spec=pltpu.PrefetchScalarGridSpec(
            num_scalar_prefetch=1, grid=(r // tr,),
            in_specs=[pl.BlockSpec((N_SHARD, tr, cols), lambda i, pl_ref: (0, i, 0)),
                      pl.BlockSpec((None, tr, cols), lambda i, pl_ref: (pl_ref[0], i, 0))],
            out_specs=pl.BlockSpec((None, tr, cols), lambda i, pl_ref: (pl_ref[1], i, 0))))(place, landed, own)


def _join_halves(bufs):
    n = len(bufs)

    def body(*refs):
        dsts = refs[n:2 * n]
        send_sems, recv_sems = refs[2 * n:]
        x, y, c, _ = _place()
        copies = [pltpu.make_async_remote_copy(
            src_ref=dsts[k].at[c], dst_ref=dsts[k].at[c], send_sem=send_sems.at[k], recv_sem=recv_sems.at[k],
            device_id=(x, y, 1 - c), device_id_type=MESH) for k in range(n)]
        for cp in copies:
            cp.start()
        for cp in copies:
            cp.wait()

    return pl.pallas_call(
        body, name="grads_join_halves", out_shape=[_sds(a.shape) for a in bufs], in_specs=[ANY] * n, out_specs=[ANY] * n,
        input_output_aliases={k: k for k in range(n)},
        scratch_shapes=[pltpu.SemaphoreType.DMA((n,)), pltpu.SemaphoreType.DMA((n,))])(*bufs)


GATHER_F32 = ("ln_gain", "ln_bias", "pool_b", "gdn_conv")
REPLICATED = ("pool_scale", "gdn_a_log", "gdn_dt_bias", "gdn_norm_w", "ple_gate_b")
WEIGHTS = ("ln_gain", "ln_bias", "pool_w", "pool_b", "pool_scale", "gdn_w_in", "gdn_conv", "gdn_a_log", "gdn_dt_bias",
           "gdn_norm_w", "gdn_w_out", "mlp_w1", "mlp_w2", "ple_gate_w", "ple_gate_b", "ple_proj")
SMALL_GRADS = ("ple_proj", "pool_w", "ln_gain", "ln_bias", "pool_b", "gdn_conv") + REPLICATED


def _pack(parts, lanes, row_multiple):
    flat = jnp.concatenate([a.reshape(-1) for a in parts])
    rows = -(-flat.shape[0] // (2 * lanes * row_multiple)) * row_multiple
    return jnp.pad(flat, (0, 2 * rows * lanes - flat.shape[0])).reshape(2, rows, lanes)


def _unpack(flat, shapes):
    out, off = [], 0
    for shp in shapes:
        n = math.prod(shp)
        out.append(flat[..., off:off + n].reshape(flat.shape[:-1] + tuple(shp)))
        off += n
    return out


def _pad_lanes(a, offset, width=128):
    return jnp.pad(a, ((0, 0), (offset, width - offset - a.shape[1])))


def kernel(x, p, ln_gain, ln_bias, pool_w, pool_b, pool_scale, gdn_w_in, gdn_conv, gdn_a_log, gdn_dt_bias, gdn_norm_w, gdn_w_out, mlp_w1, mlp_w2, ple_gate_w, ple_gate_b, ple_proj, loss_target, m_ln_gain, m_ln_bias, m_pool_w, m_pool_b, m_pool_scale, m_gdn_w_in, m_gdn_conv, m_gdn_a_log, m_gdn_dt_bias, m_gdn_norm_w, m_gdn_w_out, m_mlp_w1, m_mlp_w2, m_ple_gate_w, m_ple_gate_b, m_ple_proj, v_ln_gain, v_ln_bias, v_pool_w, v_pool_b, v_pool_scale, v_gdn_w_in, v_gdn_conv, v_gdn_a_log, v_gdn_dt_bias, v_gdn_norm_w, v_gdn_w_out, v_mlp_w1, v_mlp_w2, v_ple_gate_w, v_ple_gate_b, v_ple_proj):
    shard = dict(ln_gain=ln_gain, ln_bias=ln_bias, pool_w=pool_w, pool_b=pool_b, pool_scale=pool_scale, gdn_w_in=gdn_w_in,
                 gdn_conv=gdn_conv, gdn_a_log=gdn_a_log, gdn_dt_bias=gdn_dt_bias, gdn_norm_w=gdn_norm_w, gdn_w_out=gdn_w_out,
                 mlp_w1=mlp_w1, mlp_w2=mlp_w2, ple_gate_w=ple_gate_w, ple_gate_b=ple_gate_b, ple_proj=ple_proj)
    mom = dict(ln_gain=m_ln_gain, ln_bias=m_ln_bias, pool_w=m_pool_w, pool_b=m_pool_b, pool_scale=m_pool_scale,
               gdn_w_in=m_gdn_w_in, gdn_conv=m_gdn_conv, gdn_a_log=m_gdn_a_log, gdn_dt_bias=m_gdn_dt_bias,
               gdn_norm_w=m_gdn_norm_w, gdn_w_out=m_gdn_w_out, mlp_w1=m_mlp_w1, mlp_w2=m_mlp_w2, ple_gate_w=m_ple_gate_w,
               ple_gate_b=m_ple_gate_b, ple_proj=m_ple_proj)
    var = dict(ln_gain=v_ln_gain, ln_bias=v_ln_bias, pool_w=v_pool_w, pool_b=v_pool_b, pool_scale=v_pool_scale,
               gdn_w_in=v_gdn_w_in, gdn_conv=v_gdn_conv, gdn_a_log=v_gdn_a_log, gdn_dt_bias=v_gdn_dt_bias,
               gdn_norm_w=v_gdn_norm_w, gdn_w_out=v_gdn_w_out, mlp_w1=v_mlp_w1, mlp_w2=v_mlp_w2, ple_gate_w=v_ple_gate_w,
               ple_gate_b=v_ple_gate_b, ple_proj=v_ple_proj)

    xi, yi, ci = lax.axis_index("x"), lax.axis_index("y"), lax.axis_index("c")
    me = (2 * xi + yi).reshape(1).astype(jnp.int32)
    place = jnp.stack([2 * xi + yi, ci]).astype(jnp.int32)
    early = [("mlp_w1", 0), ("mlp_w2", 0), ("ple_gate_w", 0), ("ple_proj", 0), ("pool_w", 0)]
    late = [("mlp_w1", 1), ("mlp_w2", 1), ("ple_gate_w", 1), ("ple_proj", 1), ("gdn_w_out", 0), ("gdn_w_in", 0)]
    halved = lambda n: shard[n].reshape(shard[n].shape[0], 2, -1, shard[n].shape[-1])
    placed = lambda ops, tag: [_place_shard(halved(n), me, BF16, f"place_{tag}_{n}", l) for n, l in ops]
    small_in = _place_shard(_pack([shard[n] for n in GATHER_F32], 128, 8)[None], me, F32, "place_small")
    got_early = _all_gather(placed(early, "early") + [small_in], "weights_all_gather_early")
    placed_late = placed(late, "late")
    st = dict(zip(GATHER_F32, _unpack(got_early[-1].reshape(N_SHARD, -1), [shard[n].shape for n in GATHER_F32])))

    cat_last = lambda a: jnp.moveaxis(a, 0, -2).reshape(a.shape[1:-1] + (N_SHARD * a.shape[-1],))
    gain = cat_last(st["ln_gain"])
    bias = cat_last(st["ln_bias"])
    wp = got_early[4].reshape(N_SHARD, 4, POOL_GROUP // N_SHARD, POOL_GROUP)
    pb = cat_last(st["pool_b"]).reshape(1, D_MODEL)
    ps = pool_scale
    conv_w = cat_last(st["gdn_conv"])[0]
    merged = lambda g: g.reshape(N_SHARD, -1, g.shape[-1])
    mlp_w = lambda i, got: (merged(got[0]), merged(got[1]), merged(got[2]), ple_gate_b[i:i + 1], merged(got[3]))
    alog_l = _pad_lanes(gdn_a_log, HEADS)
    dtb_l = _pad_lanes(gdn_dt_bias, HEADS)
    nw = jnp.tile(gdn_norm_w, (1, HEADS))
    ln = lambda i, k: (gain[i, k][None], bias[i, k][None])

    x0 = x[0]
    p0, p1 = p[0, 0], p[1, 0]

    x1, xh1, rs1 = _pool_fwd(x0, wp, pb, ps, *ln(0, 0))
    (x2, xh2, rs2, a0), got_late = _mlp_fwd(x1, p0, *mlp_w(0, got_early), *ln(0, 1), "mlp_fwd_0", gather=placed_late)
    w_out = got_late[4].reshape(D_MODEL, D_MODEL)
    w_in = _assemble_w_in(merged(got_late[5]))
    proj = _matmul_nn(x2, w_in, "gdn_in_proj", GDN_IN_PAD // 3)
    q, k, v, gcb = _conv_fwd(proj, conv_w, alog_l, dtb_l)
    u, w, qd, kd, qk, gl, t_inv = _gdn_prep(q, k, v, gcb)
    o, states = _gdn_scan(u, w, qd, kd, qk, gl)
    x3, xh3, rs3 = _gdn_out_fwd(o, proj, x2, w_out, nw, *ln(1, 0))
    (dy4, xh4, rs4, a1, loss_l), _ = _mlp_fwd(x3, p1, *mlp_w(1, got_late), *ln(1, 1), "mlp_fwd_1", target=loss_target[0])

    g_gain = [[None, None], [None, None]]
    g_bias = [[None, None], [None, None]]

    def mlp_grads(i, dy, xh, rs, x_mid, a, p_i, got, scatter=()):
        (dx, dh, dzg, dpp, drb, dg, db, dgb), landed = _mlp_bwd(
            dy, xh, rs, x_mid, a, p_i, *mlp_w(i, got), ln(i, 1)[0], f"mlp_bwd_{i}", scatter=scatter)
        g_gain[i][1], g_bias[i][1] = dg, db
        return dx, dict(
            mlp_w1=_wgrad(x_mid, dh, f"dw1_{i}", stack_cols=True), mlp_w2=_wgrad(a, drb, f"dw2_{i}").reshape(N_SHARD, -1, D_MODEL),
            ple_gate_w=_wgrad(x_mid, dzg, f"dgate_w_{i}").reshape(N_SHARD, -1, D_MODEL),
            ple_proj=_wgrad(p_i, dpp, f"dproj_{i}", stack_cols=True), ple_gate_b=dgb), landed

    def chip_sums(pieces, wire, tag):
        others = _swap_halves(pieces, f"grads_swap_halves_{tag}")
        return [_add_half(a, b, place, t, f"grads_add_half_{tag}{i}") for i, (a, b, t) in enumerate(zip(pieces, others, wire))]

    dx3, gl1, _ = mlp_grads(1, dy4, xh4, rs4, x3, a1, p1, got_late)
    dres, do, dz, on_b, drb3, g_gain[1][0], g_bias[1][0], d_nw = _gdn_out_bwd(dx3, xh3, rs3, o, proj, w_out, nw, ln(1, 0)[0])
    d_wout = _wgrad(on_b, drb3, "dw_out").reshape(N_SHARD, -1, D_MODEL)
    du, dw, dqd, dkd, dqk, dgl = _gdn_scan_bwd(do, u, w, qd, kd, qk, gl, states)
    dq, dk, dv, dgcb = _gdn_prep_bwd(q, k, v, gcb, t_inv, u, w, du, dw, dqd, dkd, dqk, dgl)
    dproj, dx2, d_conv, d_alog_l, d_dtb_l = _conv_bwd(proj, dq, dk, dv, dgcb, dz, conv_w, alog_l, dtb_l, w_in, dres)
    d_win = _split_w_in(_wgrad(x2, dproj, "dw_in"))
    sums_late = chip_sums([gl1["mlp_w1"], gl1["mlp_w2"], gl1["ple_gate_w"], d_wout, d_win], [BF16] * 5, "late")
    dx1, gl0, landed_late = mlp_grads(0, dx2, xh2, rs2, x1, a0, p0, got_early, scatter=sums_late)
    sums_early = chip_sums([gl0["mlp_w1"], gl0["mlp_w2"], gl0["ple_gate_w"]], [BF16] * 3, "early")
    (dx0, g_gain[0][0], g_bias[0][0], d_ps, d_pb, d_wp), landed_early = _pool_bwd(
        dx1, xh1, rs1, x0, wp, pb, ps, ln(0, 0)[0], scatter=sums_early)

    split_last = lambda a: jnp.moveaxis(a.reshape(a.shape[:-1] + (N_SHARD, a.shape[-1] // N_SHARD)), -2, 0)
    small_st = dict(
        ple_proj=jnp.stack([gl0["ple_proj"], gl1["ple_proj"]], axis=1),
        pool_w=jnp.moveaxis(d_wp.reshape(4, N_SHARD, POOL_GROUP // N_SHARD, POOL_GROUP), 1, 0)[:, None],
        ln_gain=split_last(jnp.stack([jnp.concatenate(r, axis=0) for r in g_gain])),
        ln_bias=split_last(jnp.stack([jnp.concatenate(r, axis=0) for r in g_bias])),
        pool_b=split_last(d_pb.reshape(1, 4, POOL_GROUP)),
        gdn_conv=split_last(d_conv)[:, None],
    )
    rep = dict(pool_scale=d_ps, gdn_a_log=d_alog_l[:, HEADS:2 * HEADS], gdn_dt_bias=d_dtb_l[:, HEADS:2 * HEADS],
               gdn_norm_w=d_nw, ple_gate_b=jnp.concatenate([gl0["ple_gate_b"], gl1["ple_gate_b"]], axis=0))
    for n in REPLICATED:
        small_st[n] = jnp.broadcast_to(rep[n][None], (N_SHARD,) + rep[n].shape)
    small_flat = jnp.concatenate([small_st[n].reshape(N_SHARD, -1) for n in SMALL_GRADS], axis=1)
    small_rows = -(-small_flat.shape[1] // (16 * LANES)) * 16
    small_piece = jnp.pad(small_flat, ((0, 0), (0, small_rows * LANES - small_flat.shape[1]))).reshape(N_SHARD, small_rows, LANES)

    sums_small = chip_sums([small_piece], [F32], "small")
    landed_small = _scatter_chips(sums_small, "grads_scatter_chips_small")
    red = _join_halves([_sum_chips(q_, p_, place, f"grads_sum_chips_{i}") for i, (q_, p_) in
                        enumerate(zip(list(landed_early) + list(landed_small) + list(landed_late),
                                      sums_early + sums_small + sums_late))])
    red = [r.reshape(-1, r.shape[-1]) for r in red]
    grads = dict(mlp_w1=jnp.stack([red[0], red[4]]), mlp_w2=jnp.stack([red[1], red[5]]), ple_gate_w=jnp.stack([red[2], red[6]]),
                 gdn_w_out=red[7][None], gdn_w_in=red[8][None])
    grads.update(zip(SMALL_GRADS, _unpack(red[3].reshape(-1), [shard[n].shape for n in SMALL_GRADS])))

    delta, new_m, new_v = {}, {}, {}
    small = [n for n in WEIGHTS if shard[n].size < 128 * 128]
    for n in WEIGHTS:
        if n in small:
            continue
        to2d = lambda a, n=n: a.reshape(-1, shard[n].shape[-1])
        d2, m2, v2 = _adamw(to2d(shard[n]), to2d(grads[n]), to2d(mom[n]), to2d(var[n]), "adamw_" + n)
        delta[n], new_m[n], new_v[n] = (t.reshape(shard[n].shape) for t in (d2, m2, v2))
    pk = lambda d: _pack([d[n] for n in small], 128, 8).reshape(-1, 128)
    d2, m2, v2 = _adamw(pk(shard), pk(grads), pk(mom), pk(var), "adamw_small")
    for dst, t in ((delta, d2), (new_m, m2), (new_v, v2)):
        dst.update(zip(small, _unpack(t.reshape(-1), [shard[n].shape for n in small])))

    loss = lax.psum(loss_l[0, 0], ("x", "y", "c"))
    return (loss, dx0[None], *[grads[n] for n in WEIGHTS], *[delta[n] for n in WEIGHTS],
            *[new_m[n] for n in WEIGHTS], *[new_v[n] for n in WEIGHTS])
```

```python
import math

import jax
import jax.numpy as jnp
from jax import lax
from jax.experimental import pallas as pl
from jax.experimental.pallas import tpu as pltpu

F32 = jnp.float32
BF16 = jnp.bfloat16

D_MODEL = 1024
D_FF = 4096
PLE_DIM = 256
N_SHARD = 4
POOL_WINDOWS = (2, 4, 8, 16)
POOL_GROUP = 256
POOL_HALO = 16
HEADS = 8
HEAD_DIM = 128
CHUNK = 64
CONV_WIDTH = 4
CONV_HALO = 8
QKV_DIM = 3 * D_MODEL
GDN_IN_DIM = QKV_DIM + D_MODEL + 2 * HEADS
GDN_IN_PAD = 4224
BA_BLOCK = (QKV_DIM + D_MODEL) // 128
ALPHA = (2.0 * 2) ** 0.25
LN_EPS = 1e-5
RMS_EPS = 1e-6
L2_EPS = 1e-6
ADAM_LR, ADAM_B1, ADAM_B2, ADAM_EPS, ADAM_WD, ADAM_STEP = 0.001, 0.9, 0.999, 1e-08, 0.01, 10

ROW_TILE = 512
CONV_TILE = 256
PREP_CHUNKS = 16
SCAN_CHUNKS = 4
LANES = 1024
ADAM_ROWS = 256

NN = (((1,), (0,)), ((), ()))
NT = (((1,), (1,)), ((), ()))
TN = (((0,), (0,)), ((), ()))
BNN = (((2,), (1,)), ((0,), (0,)))
BNT = (((2,), (2,)), ((0,), (0,)))
BTN = (((1,), (1,)), ((0,), (0,)))
MESH = pl.DeviceIdType.MESH
ANY = pl.BlockSpec(memory_space=pl.ANY)


def _bdot(a, b, dims):
    return lax.dot_general(a.astype(BF16), b.astype(BF16), dims, preferred_element_type=F32)


def _hdot(a, b, dims):
    return lax.dot_general(a, b, dims, precision=lax.Precision.HIGHEST, preferred_element_type=F32)


def _mdot(a, b, dims):
    return lax.dot_general(a, b, dims, precision=lax.Precision.HIGH, preferred_element_type=F32)


def _sigmoid(x):
    return 0.5 * jnp.tanh(0.5 * x) + 0.5


def _silu(x):
    return x * _sigmoid(x)


def _softplus(x):
    return jnp.maximum(x, 0.0) + jnp.log1p(jnp.exp(-jnp.abs(x)))


def _call(body, name, grid, in_specs, out_specs, out_shape, scratch=(), sem=None, aliases=None):
    params = pltpu.CompilerParams(dimension_semantics=sem) if sem else None
    return pl.pallas_call(
        body, name=name, grid=grid, in_specs=in_specs, out_specs=out_specs, out_shape=out_shape,
        scratch_shapes=list(scratch), compiler_params=params, input_output_aliases=aliases or {})


def _row(d):
    return pl.BlockSpec((1, d), lambda *_: (0, 0))


def _full(shape):
    n = len(shape)
    return pl.BlockSpec(shape, lambda *_: (0,) * n)


def _sds(shape, dtype=F32):
    return jax.ShapeDtypeStruct(shape, dtype)


def _ln_fwd(r, gain, bias):
    mu = jnp.mean(r, axis=-1, keepdims=True)
    xc = r - mu
    rstd = lax.rsqrt(jnp.mean(xc * xc, axis=-1, keepdims=True) + LN_EPS)
    xhat = xc * rstd
    return xhat * gain + bias, xhat, rstd


def _ln_bwd(dy, xhat, rstd, gain):
    dxh = dy * gain
    m1 = jnp.mean(dxh, axis=-1, keepdims=True)
    m2 = jnp.mean(dxh * xhat, axis=-1, keepdims=True)
    return rstd * (dxh - m1 - xhat * m2)


def _acc(ref, first, val):
    @pl.when(first)
    def _():
        ref[...] = val

    @pl.when(jnp.logical_not(first))
    def _():
        ref[...] += val


def _pooled_groups(xe, t0, ts):
    pos = (t0 + lax.broadcasted_iota(jnp.int32, (ts, 1), 0) + 1).astype(F32)
    outs = []
    for gi, win in enumerate(POOL_WINDOWS):
        xs = xe[:, gi * POOL_GROUP:(gi + 1) * POOL_GROUP]
        s, k = xs, 1
        while k < win:
            s = s + pltpu.roll(s, k, 0)
            k *= 2
        mean = s[POOL_HALO:] / jnp.minimum(pos, float(win))
        outs.append(mean - xs[POOL_HALO:])
    return outs


def _pool_groups_w(w_ref):
    return [jnp.concatenate([w_ref[s, g] for s in range(N_SHARD)], axis=0) for g in range(4)]


def _pool_fwd(x, wp, pb, ps, gain, bias):
    s_len = x.shape[0]
    ts = min(ROW_TILE, s_len)
    hb = ts // POOL_HALO

    def body(x_ref, halo_ref, w_ref, pb_ref, ps_ref, g_ref, b_ref, y_ref, xhat_ref, rstd_ref):
        i = pl.program_id(0)
        x_t = x_ref[...]
        halo = jnp.where(i > 0, halo_ref[...], 0.0)
        pooled = _pooled_groups(jnp.concatenate([halo, x_t], axis=0), i * ts, ts)
        wg = _pool_groups_w(w_ref)
        y = jnp.concatenate([_bdot(pooled[g], wg[g], NN) for g in range(4)], axis=1) + pb_ref[...]
        r = ALPHA * x_t + y * ps_ref[...]
        y_ref[...], xhat_ref[...], rstd_ref[...] = _ln_fwd(r, g_ref[...], b_ref[...])

    tile = pl.BlockSpec((ts, D_MODEL), lambda i: (i, 0))
    return _call(
        body, "pool_fwd", (s_len // ts,),
        [tile, pl.BlockSpec((POOL_HALO, D_MODEL), lambda i: (jnp.maximum(i * hb - 1, 0), 0)),
         _full(wp.shape), _row(D_MODEL), _row(D_MODEL), _row(D_MODEL), _row(D_MODEL)],
        [tile, tile, pl.BlockSpec((ts, 1), lambda i: (i, 0))],
        [_sds((s_len, D_MODEL)), _sds((s_len, D_MODEL)), _sds((s_len, 1))],
        sem=("parallel",))(x, x, wp, pb, ps, gain, bias)


def _pool_bwd(dy, xhat, rstd, x, wp, pb, ps, gain, scatter=()):
    s_len = x.shape[0]
    ts = min(ROW_TILE, s_len)
    hb = ts // POOL_HALO
    n_t = s_len // ts
    ne = ts + POOL_HALO
    n_s = len(scatter)

    def body(*refs):
        dy_ref, dyn_ref, xh_ref, xhn_ref, rs_ref, rsn_ref, x_ref, xp_ref, w_ref, pb_ref, ps_ref, g_ref = refs[:12]
        dx_ref, dg_ref, db_ref, dps_ref, dpb_ref, dw_ref = refs[12 + n_s:18 + n_s]
        i = pl.program_id(0)
        if n_s:
            start, finish = _scatter_steps(refs[12:12 + n_s], refs[18 + n_s:18 + 2 * n_s], *refs[18 + 2 * n_s:])
            pl.when(i == 0)(start)
        more = i < n_t - 1
        dy_t, xh_t = dy_ref[...], xh_ref[...]
        dy_e = jnp.concatenate([dy_t, jnp.where(more, dyn_ref[...], 0.0)], axis=0)
        xh_e = jnp.concatenate([xh_t, xhn_ref[...]], axis=0)
        rs_e = jnp.concatenate([rs_ref[...], rsn_ref[...]], axis=0)
        dr_e = _ln_bwd(dy_e, xh_e, rs_e, g_ref[...])
        dyy_e = dr_e * ps_ref[...]
        pos_e = (i * ts + lax.broadcasted_iota(jnp.int32, (ne, 1), 0) + 1).astype(F32)
        dxs = []
        wg = _pool_groups_w(w_ref)
        for gi, win in enumerate(POOL_WINDOWS):
            sl = slice(gi * POOL_GROUP, (gi + 1) * POOL_GROUP)
            dpool = _bdot(dyy_e[:, sl], wg[gi], NT)
            s, k = dpool / jnp.minimum(pos_e, float(win)), 1
            while k < win:
                s = s + pltpu.roll(s, ne - k, 0)
                k *= 2
            dxs.append(s[:ts] - dpool[:ts])
        dx_ref[...] = ALPHA * dr_e[:ts] + jnp.concatenate(dxs, axis=1)

        x_t = x_ref[...]
        halo = jnp.where(i > 0, xp_ref[...], 0.0)
        pooled = _pooled_groups(jnp.concatenate([halo, x_t], axis=0), i * ts, ts)
        y = jnp.concatenate([_bdot(pooled[g], wg[g], NN) for g in range(4)], axis=1) + pb_ref[...]
        dr_t, dyy_t = dr_e[:ts], dyy_e[:ts]
        first = i == 0
        _acc(dg_ref, first, jnp.sum(dy_t * xh_t, axis=0, keepdims=True))
        _acc(db_ref, first, jnp.sum(dy_t, axis=0, keepdims=True))
        _acc(dps_ref, first, jnp.sum(dr_t * y, axis=0, keepdims=True))
        _acc(dpb_ref, first, jnp.sum(dyy_t, axis=0, keepdims=True))
        for g in range(4):
            _acc(dw_ref.at[g], first, _bdot(pooled[g], dyy_t[:, g * POOL_GROUP:(g + 1) * POOL_GROUP], TN))
        if n_s:
            pl.when(i == n_t - 1)(finish)

    tile = pl.BlockSpec((ts, D_MODEL), lambda i: (i, 0))
    nxt = pl.BlockSpec((POOL_HALO, D_MODEL), lambda i: (jnp.minimum((i + 1) * hb, n_t * hb - 1), 0))
    prv = pl.BlockSpec((POOL_HALO, D_MODEL), lambda i: (jnp.maximum(i * hb - 1, 0), 0))
    rs_t = pl.BlockSpec((ts, 1), lambda i: (i, 0))
    rs_n = pl.BlockSpec((POOL_HALO, 1), lambda i: (jnp.minimum((i + 1) * hb, n_t * hb - 1), 0))
    row = _row(D_MODEL)
    out = _call(
        body, "pool_bwd", (n_t,),
        [tile, nxt, tile, nxt, rs_t, rs_n, tile, prv, _full(wp.shape), row, row, row] + [ANY] * n_s,
        [tile, row, row, row, row, _full((4, POOL_GROUP, POOL_GROUP))] + [ANY] * n_s,
        [_sds((s_len, D_MODEL))] + [_sds((1, D_MODEL))] * 4 + [_sds((4, POOL_GROUP, POOL_GROUP))]
        + [_sds(t.shape, t.dtype) for t in scatter],
        scratch=_scatter_sems(n_s) if n_s else [], sem=("arbitrary",),
    )(dy, dy, xhat, xhat, rstd, rstd, x, x, wp, pb, ps, gain, *scatter)
    return out[:6], out[6:]


def _mlp_weight_specs():
    fc = D_FF // N_SHARD
    return [pl.BlockSpec((None, D_MODEL, fc), lambda i, j: (j, 0, 0)),
            pl.BlockSpec((None, fc, D_MODEL), lambda i, j: (j, 0, 0)),
            _full((N_SHARD, D_MODEL // N_SHARD, D_MODEL)),
            _row(D_MODEL),
            _full((N_SHARD, PLE_DIM, D_MODEL // N_SHARD))]


def _gate_w(gw_ref):
    return gw_ref[...].reshape(D_MODEL, D_MODEL)


def _ple_proj(pj_ref):
    return jnp.concatenate([pj_ref[s] for s in range(N_SHARD)], axis=1)


def _mlp_fwd(x, p, w1s, w2s, gw, gb, proj, gain, bias, name, gather=(), target=None):
    s_len = x.shape[0]
    ts = min(ROW_TILE, s_len)
    n_i = s_len // ts
    n_g = len(gather)
    has_t = target is not None
    n_in, n_out = 9 + has_t, 5 + has_t

    def body(*refs):
        x_ref, p_ref, w1_ref, w2_ref, gw_ref, gb_ref, pj_ref, g_ref, b_ref = refs[:9]
        y_ref, xhat_ref, rstd_ref, a_ref, xbo_ref = refs[n_in + n_g:n_in + n_g + 5]
        acc_ref, xb_ref = refs[n_in + n_out + 2 * n_g:n_in + n_out + 2 * n_g + 2]
        i, j = pl.program_id(0), pl.program_id(1)
        if n_g:
            start, finish = _gather_steps(refs[n_in + n_out + n_g:n_in + n_out + 2 * n_g], *refs[n_in + n_out + 2 * n_g + 2:])
            pl.when((i == 0) & (j == 0))(start)

        @pl.when(j == 0)
        def _():
            x_t = x_ref[...]
            xb_ref[...] = x_t.astype(BF16)
            xbo_ref[...] = x_t.astype(BF16)
            gate = _sigmoid(_bdot(x_t, _gate_w(gw_ref), NN) + gb_ref[...])
            acc_ref[...] = ALPHA * x_t + gate * _bdot(p_ref[...], _ple_proj(pj_ref), NN)

        h = jnp.maximum(_bdot(xb_ref[...], w1_ref[...], NN), 0.0)
        a = (h * h).astype(BF16)
        a_ref[...] = a
        acc_ref[...] += _bdot(a, w2_ref[...], NN)

        @pl.when(j == N_SHARD - 1)
        def _():
            y, xhat_ref[...], rstd_ref[...] = _ln_fwd(acc_ref[...], g_ref[...], b_ref[...])
            if has_t:
                err = y - refs[9][...]
                y_ref[...] = err * (1.0 / D_MODEL)
                part = 0.5 * jnp.sum(jnp.mean(err * err, axis=-1, keepdims=True))
                _acc(refs[n_in + n_g + 5], i == 0, part + jnp.zeros((1, 128), F32))
            else:
                y_ref[...] = y

        if n_g:
            pl.when((i == n_i - 1) & (j == N_SHARD - 1))(finish)

    tile = pl.BlockSpec((ts, D_MODEL), lambda i, j: (i, 0))
    row = _row(D_MODEL)
    out = _call(
        body, name, (n_i, N_SHARD),
        [tile, pl.BlockSpec((ts, PLE_DIM), lambda i, j: (i, 0))] + _mlp_weight_specs() + [row, row] + [tile] * has_t
        + [ANY] * n_g,
        [tile, tile, pl.BlockSpec((ts, 1), lambda i, j: (i, 0)), pl.BlockSpec((ts, D_FF // N_SHARD), lambda i, j: (i, j)), tile]
        + [_row(128)] * has_t + [ANY] * n_g,
        [_sds((s_len, D_MODEL)), _sds((s_len, D_MODEL)), _sds((s_len, 1)), _sds((s_len, D_FF), BF16), _sds((s_len, D_MODEL), BF16)]
        + [_sds((1, 128))] * has_t + [_sds(a.shape, a.dtype) for a in gather],
        scratch=[pltpu.VMEM((ts, D_MODEL), F32), pltpu.VMEM((ts, D_MODEL), BF16)] + (_gather_sems(n_g) if n_g else []),
        sem=("arbitrary", "arbitrary"), aliases={n_in + k: n_out + k for k in range(n_g)},
    )(x, p, w1s, w2s, gw, gb, proj, gain, bias, *([target] if has_t else []), *gather)
    return out[:n_out], out[n_out:]


def _mlp_bwd(dy, xhat, rstd, x, a, p, w1s, w2s, gw, gb, proj, gain, name, scatter=()):
    s_len = x.shape[0]
    ts = min(ROW_TILE, s_len)
    fc = D_FF // N_SHARD
    n_i = s_len // ts
    n_s = len(scatter)

    def body(*refs):
        dy_ref, xh_ref, rs_ref, x_ref, a_ref, p_ref, w1_ref, w2_ref, gw_ref, gb_ref, pj_ref, g_ref = refs[:12]
        dx_ref, dh_ref, dzg_ref, dpp_ref, drb_ref, dg_ref, db_ref, dgb_ref = refs[12 + n_s:20 + n_s]
        acc_ref, dr_ref = refs[20 + 2 * n_s:22 + 2 * n_s]
        i, j = pl.program_id(0), pl.program_id(1)
        if n_s:
            start, finish = _scatter_steps(refs[12:12 + n_s], refs[20 + n_s:20 + 2 * n_s], *refs[22 + 2 * n_s:])
            pl.when((i == 0) & (j == 0))(start)

        @pl.when(j == 0)
        def _():
            dy_t, xh_t, x_t = dy_ref[...], xh_ref[...], x_ref[...]
            dr = _ln_bwd(dy_t, xh_t, rs_ref[...], g_ref[...])
            drb = dr.astype(BF16)
            dr_ref[...] = drb
            drb_ref[...] = drb
            gw_full = _gate_w(gw_ref)
            gate = _sigmoid(_bdot(x_t, gw_full, NN) + gb_ref[...])
            pp = _bdot(p_ref[...], _ple_proj(pj_ref), NN)
            dzg = dr * pp * gate * (1.0 - gate)
            dzg_ref[...] = dzg.astype(BF16)
            dpp_ref[...] = (dr * gate).astype(BF16)
            acc_ref[...] = ALPHA * dr + _bdot(dzg, gw_full, NT)
            first = i == 0
            _acc(dg_ref, first, jnp.sum(dy_t * xh_t, axis=0, keepdims=True))
            _acc(db_ref, first, jnp.sum(dy_t, axis=0, keepdims=True))
            _acc(dgb_ref, first, jnp.sum(dzg, axis=0, keepdims=True))

        dh = (_bdot(dr_ref[...], w2_ref[...], NT) * (2.0 * jnp.sqrt(a_ref[...].astype(F32)))).astype(BF16)
        dh_ref[...] = dh
        acc_ref[...] += _bdot(dh, w1_ref[...], NT)

        @pl.when(j == N_SHARD - 1)
        def _():
            dx_ref[...] = acc_ref[...]

        if n_s:
            pl.when((i == n_i - 1) & (j == N_SHARD - 1))(finish)

    tile = pl.BlockSpec((ts, D_MODEL), lambda i, j: (i, 0))
    ftile = pl.BlockSpec((ts, fc), lambda i, j: (i, j))
    row = _row(D_MODEL)
    out = _call(
        body, name, (n_i, N_SHARD),
        [tile, tile, pl.BlockSpec((ts, 1), lambda i, j: (i, 0)), tile, ftile, pl.BlockSpec((ts, PLE_DIM), lambda i, j: (i, 0))]
        + _mlp_weight_specs() + [row] + [ANY] * n_s,
        [tile, ftile, tile, tile, tile, row, row, row] + [ANY] * n_s,
        [_sds((s_len, D_MODEL)), _sds((s_len, D_FF), BF16)]
        + [_sds((s_len, D_MODEL), BF16)] * 3 + [_sds((1, D_MODEL))] * 3 + [_sds(t.shape, t.dtype) for t in scatter],
        scratch=[pltpu.VMEM((ts, D_MODEL), F32), pltpu.VMEM((ts, D_MODEL), BF16)] + (_scatter_sems(n_s) if n_s else []),
        sem=("arbitrary", "arbitrary"))(dy, xhat, rstd, x, a, p, w1s, w2s, gw, gb, proj, gain, *scatter)
    return out[:8], out[8:]


def _wgrad(a, b, name, stack_cols=False):
    s_len, m = a.shape
    n = b.shape[1]
    ts = min(2048 if a.dtype == BF16 and b.dtype == BF16 else 1024, s_len)
    tm = min(m, 1024)
    tn = n // N_SHARD if stack_cols else (1408 if n == GDN_IN_PAD else min(n, 1024))
    n_s = s_len // ts

    def body(a_ref, b_ref, o_ref):
        _acc(o_ref, pl.program_id(2) == 0, _bdot(a_ref[...], b_ref[...], TN))

    if stack_cols:
        out_spec = pl.BlockSpec((None, tm, tn), lambda mi, nj, s: (nj, mi, 0))
        out_shape = _sds((N_SHARD, m, tn))
    else:
        out_spec = pl.BlockSpec((tm, tn), lambda mi, nj, s: (mi, nj))
        out_shape = _sds((m, n))
    return _call(
        body, name, (m // tm, n // tn, n_s),
        [pl.BlockSpec((ts, tm), lambda mi, nj, s: (s, mi)), pl.BlockSpec((ts, tn), lambda mi, nj, s: (s, nj))],
        out_spec, out_shape, sem=("parallel", "parallel", "arbitrary"))(a, b)


def _act_qkv(y):
    qkv = _silu(y)
    qs, ks = [], []
    for h in range(HEADS):
        qh = qkv[:, h * HEAD_DIM:(h + 1) * HEAD_DIM]
        kh = qkv[:, D_MODEL + h * HEAD_DIM:D_MODEL + (h + 1) * HEAD_DIM]
        qs.append(qh * (lax.rsqrt(jnp.sum(qh * qh, axis=-1, keepdims=True) + L2_EPS) * HEAD_DIM ** -0.5))
        ks.append(kh * lax.rsqrt(jnp.sum(kh * kh, axis=-1, keepdims=True) + L2_EPS))
    return jnp.concatenate(qs, axis=1), jnp.concatenate(ks, axis=1), qkv[:, 2 * D_MODEL:]


def _act_gb(ba, alog_l, dtb_l, tril):
    lane = lax.broadcasted_iota(jnp.int32, ba.shape, 1)
    g = jnp.where((lane >= HEADS) & (lane < 2 * HEADS), -jnp.exp(alog_l) * _softplus(ba + dtb_l), 0.0)
    return jnp.where(lane < HEADS, _sigmoid(ba), _hdot(tril, g, NN))


def _chunk_tril(t):
    ii = lax.broadcasted_iota(jnp.int32, (t, t), 0)
    jj = lax.broadcasted_iota(jnp.int32, (t, t), 1)
    return ((ii // CHUNK == jj // CHUNK) & (ii >= jj)).astype(F32)


def _conv_rows(xe, w, n_rows):
    y = xe[CONV_HALO:CONV_HALO + n_rows] * w[CONV_WIDTH - 1]
    for j in range(CONV_WIDTH - 1):
        y = y + pltpu.roll(xe, CONV_WIDTH - 1 - j, 0)[CONV_HALO:CONV_HALO + n_rows] * w[j]
    return y


def _conv_fwd(x, w_in, conv_w, alog_l, dtb_l):
    s_len = x.shape[0]
    ts = min(CONV_TILE, s_len)
    hb = ts // CONV_HALO

    def body(x_ref, xp_ref, w_ref, al_ref, dt_ref, win_hbm, proj_ref, q_ref, k_ref, v_ref, gcb_ref, win_ref, win_sem):
        i = pl.program_id(0)

        @pl.when(i == 0)
        def _():
            cp = pltpu.make_async_copy(win_hbm, win_ref, win_sem)
            cp.start()
            cp.wait()

        proj = _bdot(x_ref[...], win_ref[...], NN)
        proj_ref[...] = proj
        halo = jnp.where(i > 0, _bdot(xp_ref[...], win_ref[:, :QKV_DIM], NN), 0.0)
        taps = [w_ref[pl.ds(j, 1), :] for j in range(CONV_WIDTH)]
        y = _conv_rows(jnp.concatenate([halo, proj[:, :QKV_DIM]], axis=0), taps, ts)
        q_ref[...], k_ref[...], v_ref[...] = _act_qkv(y)
        gcb_ref[...] = _act_gb(proj[:, BA_BLOCK * 128:], al_ref[...], dt_ref[...], _chunk_tril(ts))

    tile = pl.BlockSpec((ts, D_MODEL), lambda i: (i, 0))
    return _call(
        body, "gdn_conv_fwd", (s_len // ts,),
        [tile, pl.BlockSpec((CONV_HALO, D_MODEL), lambda i: (jnp.maximum(i * hb - 1, 0), 0)),
         _full((CONV_WIDTH, QKV_DIM)), _row(128), _row(128), ANY],
        [pl.BlockSpec((ts, GDN_IN_PAD), lambda i: (i, 0)), tile, tile, tile, pl.BlockSpec((ts, 128), lambda i: (i, 0))],
        [_sds((s_len, GDN_IN_PAD))] + [_sds((s_len, D_MODEL))] * 3 + [_sds((s_len, 128))],
        scratch=[pltpu.VMEM(w_in.shape, w_in.dtype), pltpu.SemaphoreType.DMA],
        sem=("arbitrary",))(x, x, conv_w, alog_l, dtb_l, w_in)


def _conv_bwd(proj, dq, dk, dv, dgcb, dz, conv_w, alog_l, dtb_l, w_in, dres):
    s_len = proj.shape[0]
    ts = min(CONV_TILE, s_len)
    hb = ts // CONV_HALO
    n_t = s_len // ts
    te = ts + CONV_HALO

    def body(x_ref, xp_ref, xn_ref, ba_ref, dq_ref, dqn_ref, dk_ref, dkn_ref, dv_ref, dvn_ref, dgcb_ref, dz_ref,
             w_ref, al_ref, dt_ref, win_hbm, dres_ref, dp_ref, dx_ref, dw_ref, dal_ref, ddt_ref, win_ref, win_sem):
        i = pl.program_id(0)

        @pl.when(i == 0)
        def _():
            cp = pltpu.make_async_copy(win_hbm, win_ref, win_sem)
            cp.start()
            cp.wait()

        more = i < n_t - 1
        w = [w_ref[pl.ds(j, 1), :] for j in range(CONV_WIDTH)]
        x_t = x_ref[...]
        xe = jnp.concatenate([jnp.where(i > 0, xp_ref[...], 0.0), x_t, xn_ref[...]], axis=0)
        y_e, act_vjp = jax.vjp(_act_qkv, _conv_rows(xe, w, te))
        ct = tuple(jnp.concatenate([t[...], jnp.where(more, n[...], 0.0)], axis=0)
                   for t, n in ((dq_ref, dqn_ref), (dk_ref, dkn_ref), (dv_ref, dvn_ref)))
        (dy_e,) = act_vjp(ct)
        ahead = [pltpu.roll(dy_e, te - (CONV_WIDTH - 1 - j), 0)[:ts] for j in range(CONV_WIDTH - 1)] + [dy_e[:ts]]
        dx = ahead[CONV_WIDTH - 1] * w[CONV_WIDTH - 1]
        for j in range(CONV_WIDTH - 1):
            dx = dx + ahead[j] * w[j]
        dws = [jnp.sum(ahead[j] * x_t, axis=0, keepdims=True) for j in range(CONV_WIDTH)]
        _, gb_vjp = jax.vjp(lambda ba, al, dt: _act_gb(ba, al, dt, _chunk_tril(ts)), ba_ref[...], al_ref[...], dt_ref[...])
        dba, dal, ddt = gb_vjp(dgcb_ref[...])
        dp = jnp.concatenate([dx.astype(BF16), dz_ref[...], dba.astype(BF16)], axis=1)
        dp_ref[...] = dp
        dx_ref[...] = dres_ref[...] + _bdot(dp, win_ref[...], NT)
        first = i == 0
        for j in range(CONV_WIDTH):
            _acc(dw_ref.at[pl.ds(j, 1), :], first, dws[j])
        _acc(dal_ref, first, dal)
        _acc(ddt_ref, first, ddt)

    tile = pl.BlockSpec((ts, D_MODEL), lambda i: (i, 0))
    nxt = pl.BlockSpec((CONV_HALO, D_MODEL), lambda i: (jnp.minimum((i + 1) * hb, n_t * hb - 1), 0))
    return _call(
        body, "gdn_conv_bwd", (n_t,),
        [pl.BlockSpec((ts, QKV_DIM), lambda i: (i, 0)),
         pl.BlockSpec((CONV_HALO, QKV_DIM), lambda i: (jnp.maximum(i * hb - 1, 0), 0)),
         pl.BlockSpec((CONV_HALO, QKV_DIM), lambda i: (jnp.minimum((i + 1) * hb, n_t * hb - 1), 0)),
         pl.BlockSpec((ts, 128), lambda i: (i, BA_BLOCK)),
         tile, nxt, tile, nxt, tile, nxt, pl.BlockSpec((ts, 128), lambda i: (i, 0)), tile,
         _full((CONV_WIDTH, QKV_DIM)), _row(128), _row(128), ANY, tile],
        [pl.BlockSpec((ts, GDN_IN_PAD), lambda i: (i, 0)), tile, _full((CONV_WIDTH, QKV_DIM)), _row(128), _row(128)],
        [_sds((s_len, GDN_IN_PAD), BF16), _sds((s_len, D_MODEL)), _sds((CONV_WIDTH, QKV_DIM)), _sds((1, 128)), _sds((1, 128))],
        scratch=[pltpu.VMEM(w_in.shape, w_in.dtype), pltpu.SemaphoreType.DMA], sem=("arbitrary",),
    )(proj, proj, proj, proj, dq, dq, dk, dk, dv, dv, dgcb, dz, conv_w, alog_l, dtb_l, w_in, dres)


def _tri_inv(a_strict):
    ii = lax.broadcasted_iota(jnp.int32, (CHUNK, CHUNK), 0)
    jj = lax.broadcasted_iota(jnp.int32, (CHUNK, CHUNK), 1)
    x = (ii == jj).astype(F32) - a_strict
    pw = _bdot(a_strict, a_strict, BNN)
    for step in range(5):
        x = x + _bdot(x, pw, BNN)
        if step < 4:
            pw = _bdot(pw, pw, BNN)
    return x


@jax.custom_vjp
def _solved(a_strict, rhs, t, sol):
    return sol


def _solved_fwd(a_strict, rhs, t, sol):
    return sol, (t, sol)


def _solved_bwd(res, dsol):
    t, sol = res
    drhs = _mdot(t, dsol, BTN)
    return -_mdot(drhs, sol, BNT), drhs, jnp.zeros_like(t), jnp.zeros_like(sol)


_solved.defvjp(_solved_fwd, _solved_bwd)


def _prep(q, k, v, gc, beta, solve):
    ii = lax.broadcasted_iota(jnp.int32, (CHUNK, CHUNK), 0)
    jj = lax.broadcasted_iota(jnp.int32, (CHUNK, CHUNK), 1)
    causal, strict = ii >= jj, ii > jj
    gc_row = jnp.sum((ii == jj).astype(F32) * gc, axis=1, keepdims=True)
    decay = jnp.where(causal, jnp.exp(jnp.where(causal, gc - gc_row, 0.0)), 0.0)
    kb = k * beta
    a = jnp.where(strict, _bdot(kb, k, BNT) * decay, 0.0)
    eg = jnp.exp(gc)
    sol = solve(a, jnp.concatenate([v * beta, kb * eg], axis=-1))
    qk = _bdot(q, k, BNT) * decay
    last = lax.broadcasted_iota(jnp.int32, (CHUNK, 1), 0) == CHUNK - 1
    g_last = jnp.sum(jnp.where(last, gc, 0.0), axis=1, keepdims=True)
    kd = k * jnp.exp(g_last - gc)
    gl = jnp.exp(g_last) + jnp.zeros((1, 1, HEAD_DIM), F32)
    return sol[..., :HEAD_DIM], sol[..., HEAD_DIM:], qk, q * eg, kd, gl


def _prep_specs(s_len):
    rows = min(PREP_CHUNKS, s_len // CHUNK) * CHUNK
    m = rows // CHUNK
    hd = pl.BlockSpec((rows, HEAD_DIM), lambda c, h: (c, h))
    gcb = pl.BlockSpec((rows, 128), lambda c, h: (c, 0))
    qk = pl.BlockSpec((None, rows, CHUNK), lambda c, h: (h, c, 0))
    gl = pl.BlockSpec((None, m, HEADS, HEAD_DIM), lambda c, h: (c, 0, 0, 0))
    return rows, m, hd, gcb, qk, gl


def _head_cols(gcb, h, m):
    lane = lax.broadcasted_iota(jnp.int32, gcb.shape, 1)
    pick = lambda at: jnp.sum(jnp.where(lane == at, gcb, 0.0), axis=1, keepdims=True).reshape(m, CHUNK, 1)
    return pick(h + HEADS), pick(h)


def _gdn_prep(q, k, v, gcb):
    s_len = q.shape[0]
    rows, m, hd, gcb_spec, qk_spec, gl_spec = _prep_specs(s_len)

    def body(q_ref, k_ref, v_ref, gcb_ref, u_ref, w_ref, qd_ref, kd_ref, qk_ref, gl_ref, t_ref):
        r3 = lambda ref, d: ref[...].reshape(m, CHUNK, d)
        gc, beta = _head_cols(gcb_ref[...], pl.program_id(1), m)

        def solve(a, rhs):
            t = _tri_inv(a)
            t_ref[...] = t.reshape(rows, CHUNK)
            return _mdot(t, rhs, BNN)

        u, w, qk, qd, kd, gl = _prep(r3(q_ref, HEAD_DIM), r3(k_ref, HEAD_DIM), r3(v_ref, HEAD_DIM), gc, beta, solve)
        u_ref[...] = u.reshape(rows, HEAD_DIM)
        w_ref[...] = w.reshape(rows, HEAD_DIM)
        qd_ref[...] = qd.reshape(rows, HEAD_DIM).astype(BF16)
        kd_ref[...] = kd.reshape(rows, HEAD_DIM).astype(BF16)
        qk_ref[...] = qk.reshape(rows, CHUNK).astype(BF16)
        gl_ref[:, pl.ds(pl.program_id(1), 1), :] = gl

    n_g = s_len // rows
    return _call(
        body, "gdn_prep", (n_g, HEADS), [hd, hd, hd, gcb_spec], [hd, hd, hd, hd, qk_spec, gl_spec, qk_spec],
        [_sds((s_len, D_MODEL))] * 2 + [_sds((s_len, D_MODEL), BF16)] * 2
        + [_sds((HEADS, s_len, CHUNK), BF16), _sds((n_g, m, HEADS, HEAD_DIM)), _sds((HEADS, s_len, CHUNK))],
        sem=("parallel", "arbitrary"))(q, k, v, gcb)


def _gdn_prep_bwd(q, k, v, gcb, t_inv, u, w, du, dw, dqd, dkd, dqk, dgl):
    s_len = q.shape[0]
    rows, m, hd, gcb_spec, qk_spec, gl_spec = _prep_specs(s_len)

    def body(q_ref, k_ref, v_ref, gcb_ref, t_ref, u_ref, w_ref, du_ref, dw_ref, dqd_ref, dkd_ref, dqk_ref, dgl_ref,
             dq_ref, dk_ref, dv_ref, dgcb_ref):
        h = pl.program_id(1)
        r3 = lambda ref, d: ref[...].reshape(m, CHUNK, d)
        gc, beta = _head_cols(gcb_ref[...], h, m)
        t = r3(t_ref, CHUNK)
        sol = jnp.concatenate([r3(u_ref, HEAD_DIM), r3(w_ref, HEAD_DIM)], axis=-1)
        fn = lambda q_, k_, v_, gc_, bt_: _prep(q_, k_, v_, gc_, bt_, lambda a, rhs: _solved(a, rhs, t, sol))
        _, vjp = jax.vjp(fn, r3(q_ref, HEAD_DIM), r3(k_ref, HEAD_DIM), r3(v_ref, HEAD_DIM), gc, beta)
        ct = (r3(du_ref, HEAD_DIM), r3(dw_ref, HEAD_DIM), r3(dqk_ref, CHUNK), r3(dqd_ref, HEAD_DIM), r3(dkd_ref, HEAD_DIM),
              dgl_ref[:, pl.ds(h, 1), :] * (1.0 / HEAD_DIM))
        dq, dk, dv, dgc, dbt = vjp(ct)
        dq_ref[...] = dq.reshape(rows, HEAD_DIM)
        dk_ref[...] = dk.reshape(rows, HEAD_DIM)
        dv_ref[...] = dv.reshape(rows, HEAD_DIM)
        lane = lax.broadcasted_iota(jnp.int32, (rows, 128), 1)
        mine = jnp.where(lane == h, dbt.reshape(rows, 1), 0.0) + jnp.where(lane == h + HEADS, dgc.reshape(rows, 1), 0.0)
        _acc(dgcb_ref, h == 0, mine)

    return _call(
        body, "gdn_prep_bwd", (s_len // rows, HEADS),
        [hd, hd, hd, gcb_spec, qk_spec, hd, hd, hd, hd, hd, hd, qk_spec, gl_spec], [hd, hd, hd, gcb_spec],
        [_sds((s_len, D_MODEL))] * 3 + [_sds((s_len, 128))],
        sem=("parallel", "arbitrary"))(q, k, v, gcb, t_inv, u, w, du, dw, dqd, dkd, dqk, dgl)


def _scan_specs(n_c, m, k, reverse):
    n_b = n_c // k
    at = (lambda n: n_b - 1 - n) if reverse else (lambda n: n)
    row = pl.BlockSpec((k * CHUNK, D_MODEL), lambda n: (at(n), 0))
    qk = pl.BlockSpec((HEADS, k * CHUNK, CHUNK), lambda n: (0, at(n), 0))
    gl = pl.BlockSpec((None, k, HEADS, HEAD_DIM), lambda n: (at(n) // (m // k), at(n) % (m // k), 0, 0))
    st = pl.BlockSpec((k, HEADS, HEAD_DIM, HEAD_DIM), lambda n: (at(n), 0, 0, 0))
    return row, qk, gl, st


def _gdn_scan(u, w, qd, kd, qk, gl):
    s_len = u.shape[0]
    n_c = s_len // CHUNK
    k = min(SCAN_CHUNKS, gl.shape[1])
    row, qk_spec, gl_spec, st_spec = _scan_specs(n_c, gl.shape[1], k, False)

    def body(u_ref, w_ref, qd_ref, kd_ref, qk_ref, gl_ref, o_ref, st_ref, state):
        hs = range(HEADS)
        sl = [slice(h * HEAD_DIM, (h + 1) * HEAD_DIM) for h in hs]
        first = pl.program_id(0) == 0
        s_all = [jnp.where(first, 0.0, state[h]) for h in hs]
        for c in range(k):
            rows = slice(c * CHUNK, (c + 1) * CHUNK)
            s_b = [s.astype(BF16) for s in s_all]
            ws = [_bdot(w_ref[rows, sl[h]], s_b[h], NN) for h in hs]
            qs = [_bdot(qd_ref[rows, sl[h]], s_b[h], NN) for h in hs]
            vn = [(u_ref[rows, sl[h]] - ws[h]).astype(BF16) for h in hs]
            outs = [qs[h] + _bdot(qk_ref[h, rows, :], vn[h], NN) for h in hs]
            nxt = [s_all[h] * gl_ref[c, pl.ds(h, 1), :] + _bdot(kd_ref[rows, sl[h]], vn[h], TN) for h in hs]
            for h in hs:
                st_ref[c, h] = s_all[h]
                o_ref[rows, sl[h]] = outs[h]
            s_all = nxt
        for h in hs:
            state[h] = s_all[h]

    return _call(
        body, "gdn_scan", (n_c // k,), [row, row, row, row, qk_spec, gl_spec], [row, st_spec],
        [_sds((s_len, D_MODEL)), _sds((n_c, HEADS, HEAD_DIM, HEAD_DIM))],
        scratch=[pltpu.VMEM((HEADS, HEAD_DIM, HEAD_DIM), F32)], sem=("arbitrary",))(u, w, qd, kd, qk, gl)


def _gdn_scan_bwd(do, u, w, qd, kd, qk, gl, states):
    s_len = u.shape[0]
    n_c = s_len // CHUNK
    k = min(SCAN_CHUNKS, gl.shape[1])
    row, qk_spec, gl_spec, st_spec = _scan_specs(n_c, gl.shape[1], k, True)

    def body(do_ref, u_ref, w_ref, qd_ref, kd_ref, qk_ref, gl_ref, st_ref,
             du_ref, dw_ref, dqd_ref, dkd_ref, dqk_ref, dgl_ref, dstate):
        hs = range(HEADS)
        sl = [slice(h * HEAD_DIM, (h + 1) * HEAD_DIM) for h in hs]
        first = pl.program_id(0) == 0
        ds_f = [jnp.where(first, 0.0, dstate[h]) for h in hs]
        for c in reversed(range(k)):
            rows = slice(c * CHUNK, (c + 1) * CHUNK)
            s_f = [st_ref[c, h] for h in hs]
            s_b = [s.astype(BF16) for s in s_f]
            ds_b = [d.astype(BF16) for d in ds_f]
            do_b = [do_ref[rows, sl[h]].astype(BF16) for h in hs]
            w_b = [w_ref[rows, sl[h]].astype(BF16) for h in hs]
            ws = [_bdot(w_b[h], s_b[h], NN) for h in hs]
            dvn = [_bdot(qk_ref[h, rows, :], do_b[h], TN) + _bdot(kd_ref[rows, sl[h]], ds_b[h], NN) for h in hs]
            dqd = [_bdot(do_b[h], s_b[h], NT) for h in hs]
            t_do = [_bdot(qd_ref[rows, sl[h]], do_b[h], TN) for h in hs]
            vn = [(u_ref[rows, sl[h]] - ws[h]).astype(BF16) for h in hs]
            dvn_b = [d.astype(BF16) for d in dvn]
            dw = [-_bdot(dvn_b[h], s_b[h], NT) for h in hs]
            dkd = [_bdot(vn[h], ds_b[h], NT) for h in hs]
            dqk = [_bdot(do_b[h], vn[h], NT) for h in hs]
            t_dv = [_bdot(w_b[h], dvn_b[h], TN) for h in hs]
            for h in hs:
                du_ref[rows, sl[h]], dw_ref[rows, sl[h]], dqd_ref[rows, sl[h]], dkd_ref[rows, sl[h]] = dvn[h], dw[h], dqd[h], dkd[h]
                dqk_ref[h, rows, :] = dqk[h]
                dgl_ref[c, pl.ds(h, 1), :] = jnp.sum(s_f[h] * ds_f[h]) + jnp.zeros((1, HEAD_DIM), F32)
            ds_f = [ds_f[h] * gl_ref[c, pl.ds(h, 1), :] + t_do[h] - t_dv[h] for h in hs]
        for h in hs:
            dstate[h] = ds_f[h]

    return _call(
        body, "gdn_scan_bwd", (n_c // k,), [row, row, row, row, row, qk_spec, gl_spec, st_spec],
        [row, row, row, row, qk_spec, gl_spec],
        [_sds((s_len, D_MODEL))] * 4 + [_sds((HEADS, s_len, CHUNK)), _sds(gl.shape)],
        scratch=[pltpu.VMEM((HEADS, HEAD_DIM, HEAD_DIM), F32)], sem=("arbitrary",))(do, u, w, qd, kd, qk, gl, states)


def _gate_norm(o, z, nw):
    outs = []
    for h in range(HEADS):
        oh = o[:, h * HEAD_DIM:(h + 1) * HEAD_DIM]
        outs.append(oh * lax.rsqrt(jnp.mean(oh * oh, axis=-1, keepdims=True) + RMS_EPS))
    return jnp.concatenate(outs, axis=1) * nw * _silu(z)


def _gdn_out_fwd(o, proj, x, w_out, nw, gain, bias):
    s_len = x.shape[0]
    ts = min(ROW_TILE, s_len)

    def body(o_ref, z_ref, x_ref, w_ref, nw_ref, g_ref, b_ref, y_ref, xhat_ref, rstd_ref):
        on = _gate_norm(o_ref[...], z_ref[...], nw_ref[...])
        r = ALPHA * x_ref[...] + _bdot(on, w_ref[...], NN)
        y_ref[...], xhat_ref[...], rstd_ref[...] = _ln_fwd(r, g_ref[...], b_ref[...])

    tile = pl.BlockSpec((ts, D_MODEL), lambda i: (i, 0))
    row = _row(D_MODEL)
    return _call(
        body, "gdn_out_fwd", (s_len // ts,),
        [tile, pl.BlockSpec((ts, D_MODEL), lambda i: (i, QKV_DIM // D_MODEL)), tile, _full((D_MODEL, D_MODEL)), row, row, row],
        [tile, tile, pl.BlockSpec((ts, 1), lambda i: (i, 0))],
        [_sds((s_len, D_MODEL)), _sds((s_len, D_MODEL)), _sds((s_len, 1))], sem=("parallel",))(o, proj, x, w_out, nw, gain, bias)


def _gdn_out_bwd(dy, xhat, rstd, o, proj, w_out, nw, gain):
    s_len = o.shape[0]
    ts = min(ROW_TILE, s_len)

    def body(dy_ref, xh_ref, rs_ref, o_ref, z_ref, w_ref, nw_ref, g_ref,
             dres_ref, do_ref, dz_ref, on_ref, drb_ref, dg_ref, db_ref, dnw_ref):
        dy_t, xh_t = dy_ref[...], xh_ref[...]
        dr = _ln_bwd(dy_t, xh_t, rs_ref[...], g_ref[...])
        dres_ref[...] = ALPHA * dr
        drb_ref[...] = dr.astype(BF16)
        on, vjp = jax.vjp(_gate_norm, o_ref[...], z_ref[...], nw_ref[...])
        on_ref[...] = on.astype(BF16)
        do, dz, dnw = vjp(_bdot(dr, w_ref[...], NT))
        do_ref[...] = do
        dz_ref[...] = dz.astype(BF16)
        first = pl.program_id(0) == 0
        _acc(dg_ref, first, jnp.sum(dy_t * xh_t, axis=0, keepdims=True))
        _acc(db_ref, first, jnp.sum(dy_t, axis=0, keepdims=True))
        _acc(dnw_ref, first, sum(dnw[:, h * HEAD_DIM:(h + 1) * HEAD_DIM] for h in range(HEADS)))

    tile = pl.BlockSpec((ts, D_MODEL), lambda i: (i, 0))
    row = _row(D_MODEL)
    return _call(
        body, "gdn_out_bwd", (s_len // ts,),
        [tile, tile, pl.BlockSpec((ts, 1), lambda i: (i, 0)), tile,
         pl.BlockSpec((ts, D_MODEL), lambda i: (i, QKV_DIM // D_MODEL)), _full((D_MODEL, D_MODEL)), row, row],
        [tile, tile, tile, tile, tile, row, row, _row(HEAD_DIM)],
        [_sds((s_len, D_MODEL))] * 2 + [_sds((s_len, D_MODEL), BF16)] * 3 + [_sds((1, D_MODEL))] * 2 + [_sds((1, HEAD_DIM))],
        sem=("arbitrary",))(dy, xhat, rstd, o, proj, w_out, nw, gain)


def _adamw(w, g, m, v, name):
    r, c = w.shape
    tr = min(ADAM_ROWS, r)

    def body(w_ref, g_ref, m_ref, v_ref, d_ref, nm_ref, nv_ref):
        g_t = g_ref[...]
        nm = ADAM_B1 * m_ref[...] + (1.0 - ADAM_B1) * g_t
        nv = ADAM_B2 * v_ref[...] + (1.0 - ADAM_B2) * (g_t * g_t)
        m_hat = nm / (1.0 - ADAM_B1 ** ADAM_STEP)
        v_hat = nv / (1.0 - ADAM_B2 ** ADAM_STEP)
        d_ref[...] = -ADAM_LR * (m_hat / (jnp.sqrt(v_hat) + ADAM_EPS) + ADAM_WD * w_ref[...])
        nm_ref[...] = nm
        nv_ref[...] = nv

    tile = pl.BlockSpec((tr, c), lambda i: (i, 0))
    return _call(body, name, (r // tr,), [tile] * 4, [tile] * 3, [_sds((r, c))] * 3, sem=("parallel",))(w, g, m, v)


def _assemble_w_in(shards):
    rows = 256
    width = GDN_IN_DIM // N_SHARD

    def body(s_ref, o_ref):
        pad = jnp.zeros((rows, GDN_IN_PAD - GDN_IN_DIM), shards.dtype)
        o_ref[...] = jnp.concatenate([s_ref[j] for j in range(N_SHARD)] + [pad], axis=1)

    return _call(body, "w_in_assemble", (D_MODEL // rows,), [pl.BlockSpec((N_SHARD, rows, width), lambda i: (0, i, 0))],
                 pl.BlockSpec((rows, GDN_IN_PAD), lambda i: (i, 0)), _sds((D_MODEL, GDN_IN_PAD), shards.dtype),
                 sem=("parallel",))(shards)


def _split_w_in(full):
    rows = 256
    width = GDN_IN_DIM // N_SHARD

    def body(f_ref, o_ref):
        f = f_ref[...]
        for j in range(N_SHARD):
            o_ref[j] = f[:, j * width:(j + 1) * width]

    return _call(body, "w_in_split", (D_MODEL // rows,), [pl.BlockSpec((rows, GDN_IN_PAD), lambda i: (i, 0))],
                 pl.BlockSpec((N_SHARD, rows, width), lambda i: (0, i, 0)), _sds((N_SHARD, D_MODEL, width), full.dtype),
                 sem=("parallel",))(full)


def _place():
    x, y, c = lax.axis_index("x"), lax.axis_index("y"), lax.axis_index("c")
    return x, y, c, [(1 - x, y), (x, 1 - y), (1 - x, 1 - y)]


def _row_tile(rows):
    return max(t for t in range(8, min(rows, 640) + 1, 8) if rows % t == 0)


def _place_shard(part, me, dtype, name, layer=0):
    _, _, r, c = part.shape
    tr = _row_tile(r)

    def body(me_ref, p_ref, o_ref):
        o_ref[...] = p_ref[...].astype(dtype)

    return pl.pallas_call(
        body, name=name, out_shape=_sds((N_SHARD, 2, r, c), dtype),
        grid_spec=pltpu.PrefetchScalarGridSpec(
            num_scalar_prefetch=1, grid=(2, r // tr),
            in_specs=[pl.BlockSpec((None, None, tr, c), lambda h, i, me_ref: (layer, h, i, 0))],
            out_specs=pl.BlockSpec((None, None, tr, c), lambda h, i, me_ref: (me_ref[0], h, i, 0))))(me, part)


def _gather_sems(n):
    return [pltpu.SemaphoreType.DMA((6 * n,)), pltpu.SemaphoreType.DMA((6 * n,))]


def _gather_steps(dsts, send_sems, recv_sems):
    n = len(dsts)
    x, y, c, chips = _place()
    me = 2 * x + y
    sibling = (x, y, 1 - c)

    def ici(k, j, slot):
        px, py = chips[j]
        view = dsts[k].at[slot, c]
        return pltpu.make_async_remote_copy(
            src_ref=view, dst_ref=view, send_sem=send_sems.at[6 * k + j],
            recv_sem=recv_sems.at[6 * k + j], device_id=(px, py, c), device_id_type=MESH)

    def d2d(k, j, half):
        px, py = chips[j]
        view = dsts[k].at[2 * px + py, half]
        return pltpu.make_async_remote_copy(
            src_ref=view, dst_ref=view, send_sem=send_sems.at[6 * k + 3 + j], recv_sem=recv_sems.at[6 * k + 3 + j],
            device_id=sibling, device_id_type=MESH)

    def start():
        for k in range(n):
            for j in range(3):
                ici(k, j, me).start()

    def finish():
        fwds = []
        for k in range(n):
            for j, (px, py) in enumerate(chips):
                ici(k, j, 2 * px + py).wait_recv()
                fwds.append(d2d(k, j, c))
                fwds[-1].start()
        for k in range(n):
            for j in range(3):
                d2d(k, j, 1 - c).wait_recv()
        for k in range(n):
            for j in range(3):
                ici(k, j, me).wait_send()
        for cp in fwds:
            cp.wait_send()

    return start, finish


def _all_gather(bufs, name):
    n = len(bufs)

    def body(*refs):
        start, finish = _gather_steps(refs[n:2 * n], *refs[2 * n:])
        start()
        finish()

    return pl.pallas_call(
        body, name=name, out_shape=[_sds(a.shape, a.dtype) for a in bufs],
        in_specs=[ANY] * n, out_specs=[ANY] * n, input_output_aliases={k: k for k in range(n)},
        scratch_shapes=_gather_sems(n))(*bufs)


def _swap_halves(pieces, name):
    n = len(pieces)

    def body(*refs):
        srcs, dsts = refs[:n], refs[n:2 * n]
        send_sems, recv_sems = refs[2 * n:]
        x, y, c, _ = _place()
        copies = []
        for k in range(n):
            hr = pieces[k].shape[1] // 2
            copies.append(pltpu.make_async_remote_copy(
                src_ref=srcs[k].at[:, pl.ds((1 - c) * hr, hr), :], dst_ref=dsts[k],
                send_sem=send_sems.at[k], recv_sem=recv_sems.at[k], device_id=(x, y, 1 - c), device_id_type=MESH))
        for cp in copies:
            cp.start()
        for cp in copies:
            cp.wait()

    return pl.pallas_call(
        body, name=name, out_shape=[_sds((N_SHARD, a.shape[1] // 2, a.shape[2])) for a in pieces],
        in_specs=[ANY] * n, out_specs=[ANY] * n,
        scratch_shapes=[pltpu.SemaphoreType.DMA((n,)), pltpu.SemaphoreType.DMA((n,))])(*pieces)


def _add_half(piece, other, place, dtype, name):
    n, hr, cols = other.shape
    tr = _row_tile(hr)

    def body(pl_ref, a_ref, b_ref, o_ref):
        o_ref[...] = (a_ref[...] + b_ref[...]).astype(dtype)

    tile = pl.BlockSpec((None, tr, cols), lambda s, i, pl_ref: (s, i, 0))
    return pl.pallas_call(
        body, name=name, out_shape=_sds(other.shape, dtype),
        grid_spec=pltpu.PrefetchScalarGridSpec(
            num_scalar_prefetch=1, grid=(n, hr // tr),
            in_specs=[pl.BlockSpec((None, None, tr, cols), lambda s, i, pl_ref: (s, pl_ref[1], i, 0)), tile],
            out_specs=tile))(place, piece.reshape(n, 2, hr, cols), other)


def _scatter_sems(n):
    return [pltpu.SemaphoreType.DMA((3 * n,)), pltpu.SemaphoreType.DMA((3 * n,))]


def _scatter_steps(srcs, dsts, send_sems, recv_sems):
    n = len(srcs)
    x, y, c, chips = _place()
    me = 2 * x + y

    def ici(k, j, src_slot, dst_slot):
        px, py = chips[j]
        return pltpu.make_async_remote_copy(
            src_ref=srcs[k].at[src_slot], dst_ref=dsts[k].at[dst_slot], send_sem=send_sems.at[3 * k + j],
            recv_sem=recv_sems.at[3 * k + j], device_id=(px, py, c), device_id_type=MESH)

    def start():
        for k in range(n):
            for j, (px, py) in enumerate(chips):
                ici(k, j, 2 * px + py, me).start()

    def finish():
        for k in range(n):
            for j, (px, py) in enumerate(chips):
                ici(k, j, me, 2 * px + py).wait_recv()
        for k in range(n):
            for j, (px, py) in enumerate(chips):
                ici(k, j, 2 * px + py, me).wait_send()

    return start, finish


def _scatter_chips(parts, name):
    n = len(parts)

    def body(*refs):
        start, finish = _scatter_steps(refs[:n], refs[n:2 * n], *refs[2 * n:])
        start()
        finish()

    return pl.pallas_call(
        body, name=name, out_shape=[_sds(a.shape, a.dtype) for a in parts],
        in_specs=[ANY] * n, out_specs=[ANY] * n, scratch_shapes=_scatter_sems(n))(*parts)


def _sum_chips(landed, own, place, name):
    _, r, cols = landed.shape
    tr = _row_tile(r)

    def body(pl_ref, q_ref, p_ref, o_ref):
        me = pl_ref[0]
        f = lambda j: jnp.where(me == j, p_ref[...], q_ref[j]).astype(F32)
        o_ref[...] = ((f(0) + f(1)) + f(2)) + f(3)

    return pl.pallas_call(
        body, name=name, out_shape=_sds((2, r, cols)),
        grid_spec=pltpu.PrefetchScalarGridSpec(
            num_scalar_prefetch=1, grid=(r // tr,),
            in_specs=[pl.BlockSpec((N_SHARD, tr, cols), lambda i, pl_ref: (0, i, 0)),
                      pl.BlockSpec((None, tr, cols), lambda i, pl_ref: (pl_ref[0], i, 0))],
            out_specs=pl.BlockSpec((None, tr, cols), lambda i, pl_ref: (pl_ref[1], i, 0))))(place, landed, own)


def _join_halves(bufs):
    n = len(bufs)

    def body(*refs):
        dsts = refs[n:2 * n]
        send_sems, recv_sems = refs[2 * n:]
        x, y, c, _ = _place()
        copies = [pltpu.make_async_remote_copy(
            src_ref=dsts[k].at[c], dst_ref=dsts[k].at[c], send_sem=send_sems.at[k], recv_sem=recv_sems.at[k],
            device_id=(x, y, 1 - c), device_id_type=MESH) for k in range(n)]
        for cp in copies:
            cp.start()
        for cp in copies:
            cp.wait()

    return pl.pallas_call(
        body, name="grads_join_halves", out_shape=[_sds(a.shape) for a in bufs], in_specs=[ANY] * n, out_specs=[ANY] * n,
        input_output_aliases={k: k for k in range(n)},
        scratch_shapes=[pltpu.SemaphoreType.DMA((n,)), pltpu.SemaphoreType.DMA((n,))])(*bufs)


GATHER_F32 = ("ln_gain", "ln_bias", "pool_b", "gdn_conv")
REPLICATED = ("pool_scale", "gdn_a_log", "gdn_dt_bias", "gdn_norm_w", "ple_gate_b")
WEIGHTS = ("ln_gain", "ln_bias", "pool_w", "pool_b", "pool_scale", "gdn_w_in", "gdn_conv", "gdn_a_log", "gdn_dt_bias",
           "gdn_norm_w", "gdn_w_out", "mlp_w1", "mlp_w2", "ple_gate_w", "ple_gate_b", "ple_proj")
SMALL_GRADS = ("ple_proj", "pool_w", "ln_gain", "ln_bias", "pool_b", "gdn_conv") + REPLICATED


def _pack(parts, lanes, row_multiple):
    flat = jnp.concatenate([a.reshape(-1) for a in parts])
    rows = -(-flat.shape[0] // (2 * lanes * row_multiple)) * row_multiple
    return jnp.pad(flat, (0, 2 * rows * lanes - flat.shape[0])).reshape(2, rows, lanes)


def _unpack(flat, shapes):
    out, off = [], 0
    for shp in shapes:
        n = math.prod(shp)
        out.append(flat[..., off:off + n].reshape(flat.shape[:-1] + tuple(shp)))
        off += n
    return out


def _pad_lanes(a, offset, width=128):
    return jnp.pad(a, ((0, 0), (offset, width - offset - a.shape[1])))


def kernel(x, p, ln_gain, ln_bias, pool_w, pool_b, pool_scale, gdn_w_in, gdn_conv, gdn_a_log, gdn_dt_bias, gdn_norm_w, gdn_w_out, mlp_w1, mlp_w2, ple_gate_w, ple_gate_b, ple_proj, loss_target, m_ln_gain, m_ln_bias, m_pool_w, m_pool_b, m_pool_scale, m_gdn_w_in, m_gdn_conv, m_gdn_a_log, m_gdn_dt_bias, m_gdn_norm_w, m_gdn_w_out, m_mlp_w1, m_mlp_w2, m_ple_gate_w, m_ple_gate_b, m_ple_proj, v_ln_gain, v_ln_bias, v_pool_w, v_pool_b, v_pool_scale, v_gdn_w_in, v_gdn_conv, v_gdn_a_log, v_gdn_dt_bias, v_gdn_norm_w, v_gdn_w_out, v_mlp_w1, v_mlp_w2, v_ple_gate_w, v_ple_gate_b, v_ple_proj):
    shard = dict(ln_gain=ln_gain, ln_bias=ln_bias, pool_w=pool_w, pool_b=pool_b, pool_scale=pool_scale, gdn_w_in=gdn_w_in,
                 gdn_conv=gdn_conv, gdn_a_log=gdn_a_log, gdn_dt_bias=gdn_dt_bias, gdn_norm_w=gdn_norm_w, gdn_w_out=gdn_w_out,
                 mlp_w1=mlp_w1, mlp_w2=mlp_w2, ple_gate_w=ple_gate_w, ple_gate_b=ple_gate_b, ple_proj=ple_proj)
    mom = dict(ln_gain=m_ln_gain, ln_bias=m_ln_bias, pool_w=m_pool_w, pool_b=m_pool_b, pool_scale=m_pool_scale,
               gdn_w_in=m_gdn_w_in, gdn_conv=m_gdn_conv, gdn_a_log=m_gdn_a_log, gdn_dt_bias=m_gdn_dt_bias,
               gdn_norm_w=m_gdn_norm_w, gdn_w_out=m_gdn_w_out, mlp_w1=m_mlp_w1, mlp_w2=m_mlp_w2, ple_gate_w=m_ple_gate_w,
               ple_gate_b=m_ple_gate_b, ple_proj=m_ple_proj)
    var = dict(ln_gain=v_ln_gain, ln_bias=v_ln_bias, pool_w=v_pool_w, pool_b=v_pool_b, pool_scale=v_pool_scale,
               gdn_w_in=v_gdn_w_in, gdn_conv=v_gdn_conv, gdn_a_log=v_gdn_a_log, gdn_dt_bias=v_gdn_dt_bias,
               gdn_norm_w=v_gdn_norm_w, gdn_w_out=v_gdn_w_out, mlp_w1=v_mlp_w1, mlp_w2=v_mlp_w2, ple_gate_w=v_ple_gate_w,
               ple_gate_b=v_ple_gate_b, ple_proj=v_ple_proj)

    xi, yi, ci = lax.axis_index("x"), lax.axis_index("y"), lax.axis_index("c")
    me = (2 * xi + yi).reshape(1).astype(jnp.int32)
    place = jnp.stack([2 * xi + yi, ci]).astype(jnp.int32)
    early = [("mlp_w1", 0), ("mlp_w2", 0), ("ple_gate_w", 0), ("ple_proj", 0), ("pool_w", 0)]
    late = [("mlp_w1", 1), ("mlp_w2", 1), ("ple_gate_w", 1), ("ple_proj", 1), ("gdn_w_out", 0), ("gdn_w_in", 0)]
    halved = lambda n: shard[n].reshape(shard[n].shape[0], 2, -1, shard[n].shape[-1])
    placed = lambda ops, tag: [_place_shard(halved(n), me, BF16, f"place_{tag}_{n}", l) for n, l in ops]
    small_in = _place_shard(_pack([shard[n] for n in GATHER_F32], 128, 8)[None], me, F32, "place_small")
    got_early = _all_gather(placed(early, "early") + [small_in], "weights_all_gather_early")
    placed_late = placed(late, "late")
    st = dict(zip(GATHER_F32, _unpack(got_early[-1].reshape(N_SHARD, -1), [shard[n].shape for n in GATHER_F32])))

    cat_last = lambda a: jnp.moveaxis(a, 0, -2).reshape(a.shape[1:-1] + (N_SHARD * a.shape[-1],))
    gain = cat_last(st["ln_gain"])
    bias = cat_last(st["ln_bias"])
    wp = got_early[4].reshape(N_SHARD, 4, POOL_GROUP // N_SHARD, POOL_GROUP)
    pb = cat_last(st["pool_b"]).reshape(1, D_MODEL)
    ps = pool_scale
    conv_w = cat_last(st["gdn_conv"])[0]
    merged = lambda g: g.reshape(N_SHARD, -1, g.shape[-1])
    mlp_w = lambda i, got: (merged(got[0]), merged(got[1]), merged(got[2]), ple_gate_b[i:i + 1], merged(got[3]))
    alog_l = _pad_lanes(gdn_a_log, HEADS)
    dtb_l = _pad_lanes(gdn_dt_bias, HEADS)
    nw = jnp.tile(gdn_norm_w, (1, HEADS))
    ln = lambda i, k: (gain[i, k][None], bias[i, k][None])

    x0 = x[0]
    p0, p1 = p[0, 0], p[1, 0]

    x1, xh1, rs1 = _pool_fwd(x0, wp, pb, ps, *ln(0, 0))
    (x2, xh2, rs2, a0, xb0), got_late = _mlp_fwd(x1, p0, *mlp_w(0, got_early), *ln(0, 1), "mlp_fwd_0", gather=placed_late)
    w_out = got_late[4].reshape(D_MODEL, D_MODEL)
    w_in = _assemble_w_in(merged(got_late[5]))
    proj, q, k, v, gcb = _conv_fwd(x2, w_in, conv_w, alog_l, dtb_l)
    u, w, qd, kd, qk, gl, t_inv = _gdn_prep(q, k, v, gcb)
    o, states = _gdn_scan(u, w, qd, kd, qk, gl)
    x3, xh3, rs3 = _gdn_out_fwd(o, proj, x2, w_out, nw, *ln(1, 0))
    (dy4, xh4, rs4, a1, xb1, loss_l), _ = _mlp_fwd(x3, p1, *mlp_w(1, got_late), *ln(1, 1), "mlp_fwd_1", target=loss_target[0])

    g_gain = [[None, None], [None, None]]
    g_bias = [[None, None], [None, None]]

    def mlp_grads(i, dy, xh, rs, x_mid, xb, a, p_i, got, scatter=()):
        (dx, dh, dzg, dpp, drb, dg, db, dgb), landed = _mlp_bwd(
            dy, xh, rs, x_mid, a, p_i, *mlp_w(i, got), ln(i, 1)[0], f"mlp_bwd_{i}", scatter=scatter)
        g_gain[i][1], g_bias[i][1] = dg, db
        return dx, dict(
            mlp_w1=_wgrad(xb, dh, f"dw1_{i}", stack_cols=True), mlp_w2=_wgrad(a, drb, f"dw2_{i}").reshape(N_SHARD, -1, D_MODEL),
            ple_gate_w=_wgrad(xb, dzg, f"dgate_w_{i}").reshape(N_SHARD, -1, D_MODEL),
            ple_proj=_wgrad(p_i, dpp, f"dproj_{i}", stack_cols=True), ple_gate_b=dgb), landed

    def chip_sums(pieces, wire, tag):
        others = _swap_halves(pieces, f"grads_swap_halves_{tag}")
        return [_add_half(a, b, place, t, f"grads_add_half_{tag}{i}") for i, (a, b, t) in enumerate(zip(pieces, others, wire))]

    dx3, gl1, _ = mlp_grads(1, dy4, xh4, rs4, x3, xb1, a1, p1, got_late)
    dres, do, dz, on_b, drb3, g_gain[1][0], g_bias[1][0], d_nw = _gdn_out_bwd(dx3, xh3, rs3, o, proj, w_out, nw, ln(1, 0)[0])
    d_wout = _wgrad(on_b, drb3, "dw_out").reshape(N_SHARD, -1, D_MODEL)
    du, dw, dqd, dkd, dqk, dgl = _gdn_scan_bwd(do, u, w, qd, kd, qk, gl, states)
    dq, dk, dv, dgcb = _gdn_prep_bwd(q, k, v, gcb, t_inv, u, w, du, dw, dqd, dkd, dqk, dgl)
    dproj, dx2, d_conv, d_alog_l, d_dtb_l = _conv_bwd(proj, dq, dk, dv, dgcb, dz, conv_w, alog_l, dtb_l, w_in, dres)
    d_win = _split_w_in(_wgrad(x2, dproj, "dw_in"))
    sums_late = chip_sums([gl1["mlp_w1"], gl1["mlp_w2"], gl1["ple_gate_w"], d_wout, d_win], [BF16] * 5, "late")
    dx1, gl0, landed_late = mlp_grads(0, dx2, xh2, rs2, x1, xb0, a0, p0, got_early, scatter=sums_late)
    sums_early = chip_sums([gl0["mlp_w1"], gl0["mlp_w2"], gl0["ple_gate_w"]], [BF16] * 3, "early")
    (dx0, g_gain[0][0], g_bias[0][0], d_ps, d_pb, d_wp), landed_early = _pool_bwd(
        dx1, xh1, rs1, x0, wp, pb, ps, ln(0, 0)[0], scatter=sums_early)

    split_last = lambda a: jnp.moveaxis(a.reshape(a.shape[:-1] + (N_SHARD, a.shape[-1] // N_SHARD)), -2, 0)
    small_st = dict(
        ple_proj=jnp.stack([gl0["ple_proj"], gl1["ple_proj"]], axis=1),
        pool_w=jnp.moveaxis(d_wp.reshape(4, N_SHARD, POOL_GROUP // N_SHARD, POOL_GROUP), 1, 0)[:, None],
        ln_gain=split_last(jnp.stack([jnp.concatenate(r, axis=0) for r in g_gain])),
        ln_bias=split_last(jnp.stack([jnp.concatenate(r, axis=0) for r in g_bias])),
        pool_b=split_last(d_pb.reshape(1, 4, POOL_GROUP)),
        gdn_conv=split_last(d_conv)[:, None],
    )
    rep = dict(pool_scale=d_ps, gdn_a_log=d_alog_l[:, HEADS:2 * HEADS], gdn_dt_bias=d_dtb_l[:, HEADS:2 * HEADS],
               gdn_norm_w=d_nw, ple_gate_b=jnp.concatenate([gl0["ple_gate_b"], gl1["ple_gate_b"]], axis=0))
    for n in REPLICATED:
        small_st[n] = jnp.broadcast_to(rep[n][None], (N_SHARD,) + rep[n].shape)
    small_flat = jnp.concatenate([small_st[n].reshape(N_SHARD, -1) for n in SMALL_GRADS], axis=1)
    small_rows = -(-small_flat.shape[1] // (16 * LANES)) * 16
    small_piece = jnp.pad(small_flat, ((0, 0), (0, small_rows * LANES - small_flat.shape[1]))).reshape(N_SHARD, small_rows, LANES)

    sums_small = chip_sums([small_piece], [F32], "small")
    landed_small = _scatter_chips(sums_small, "grads_scatter_chips_small")
    red = _join_halves([_sum_chips(q_, p_, place, f"grads_sum_chips_{i}") for i, (q_, p_) in
                        enumerate(zip(list(landed_early) + list(landed_small) + list(landed_late),
                                      sums_early + sums_small + sums_late))])
    red = [r.reshape(-1, r.shape[-1]) for r in red]
    grads = dict(mlp_w1=jnp.stack([red[0], red[4]]), mlp_w2=jnp.stack([red[1], red[5]]), ple_gate_w=jnp.stack([red[2], red[6]]),
                 gdn_w_out=red[7][None], gdn_w_in=red[8][None])
    grads.update(zip(SMALL_GRADS, _unpack(red[3].reshape(-1), [shard[n].shape for n in SMALL_GRADS])))

    delta, new_m, new_v = {}, {}, {}
    small = [n for n in WEIGHTS if shard[n].size < 128 * 128]
    for n in WEIGHTS:
        if n in small:
            continue
        to2d = lambda a, n=n: a.reshape(-1, shard[n].shape[-1])
        d2, m2, v2 = _adamw(to2d(shard[n]), to2d(grads[n]), to2d(mom[n]), to2d(var[n]), "adamw_" + n)
        delta[n], new_m[n], new_v[n] = (t.reshape(shard[n].shape) for t in (d2, m2, v2))
    pk = lambda d: _pack([d[n] for n in small], 128, 8).reshape(-1, 128)
    d2, m2, v2 = _adamw(pk(shard), pk(grads), pk(mom), pk(var), "adamw_small")
    for dst, t in ((delta, d2), (new_m, m2), (new_v, v2)):
        dst.update(zip(small, _unpack(t.reshape(-1), [shard[n].shape for n in small])))

    loss = lax.psum(loss_l[0, 0], ("x", "y", "c"))
    return (loss, dx0[None], *[grads[n] for n in WEIGHTS], *[delta[n] for n in WEIGHTS],
            *[new_m[n] for n in WEIGHTS], *[new_v[n] for n in WEIGHTS])
```

```python
import math

import jax
import jax.numpy as jnp
from jax import lax
from jax.experimental import pallas as pl
from jax.experimental.pallas import tpu as pltpu

F32 = jnp.float32
BF16 = jnp.bfloat16

D_MODEL = 1024
D_FF = 4096
PLE_DIM = 256
N_SHARD = 4
POOL_WINDOWS = (2, 4, 8, 16)
POOL_GROUP = 256
POOL_HALO = 16
HEADS = 8
HEAD_DIM = 128
CHUNK = 64
CONV_WIDTH = 4
CONV_HALO = 8
QKV_DIM = 3 * D_MODEL
GDN_IN_DIM = QKV_DIM + D_MODEL + 2 * HEADS
GDN_IN_PAD = 4224
BA_BLOCK = (QKV_DIM + D_MODEL) // 128
ALPHA = (2.0 * 2) ** 0.25
LN_EPS = 1e-5
RMS_EPS = 1e-6
L2_EPS = 1e-6
ADAM_LR, ADAM_B1, ADAM_B2, ADAM_EPS, ADAM_WD, ADAM_STEP = 0.001, 0.9, 0.999, 1e-08, 0.01, 10

ROW_TILE = 512
CONV_TILE = 256
PREP_CHUNKS = 16
SCAN_CHUNKS = 4
LANES = 1024
ADAM_ROWS = 256

NN = (((1,), (0,)), ((), ()))
NT = (((1,), (1,)), ((), ()))
TN = (((0,), (0,)), ((), ()))
BNN = (((2,), (1,)), ((0,), (0,)))
BNT = (((2,), (2,)), ((0,), (0,)))
BTN = (((1,), (1,)), ((0,), (0,)))
MESH = pl.DeviceIdType.MESH
ANY = pl.BlockSpec(memory_space=pl.ANY)


def _bdot(a, b, dims):
    return lax.dot_general(a.astype(BF16), b.astype(BF16), dims, preferred_element_type=F32)


def _hdot(a, b, dims):
    return lax.dot_general(a, b, dims, precision=lax.Precision.HIGHEST, preferred_element_type=F32)


def _mdot(a, b, dims):
    return lax.dot_general(a, b, dims, precision=lax.Precision.HIGH, preferred_element_type=F32)


def _sigmoid(x):
    return 0.5 * jnp.tanh(0.5 * x) + 0.5


def _silu(x):
    return x * _sigmoid(x)


def _softplus(x):
    return jnp.maximum(x, 0.0) + jnp.log1p(jnp.exp(-jnp.abs(x)))


def _call(body, name, grid, in_specs, out_specs, out_shape, scratch=(), sem=None, aliases=None):
    params = pltpu.CompilerParams(dimension_semantics=sem) if sem else None
    return pl.pallas_call(
        body, name=name, grid=grid, in_specs=in_specs, out_specs=out_specs, out_shape=out_shape,
        scratch_shapes=list(scratch), compiler_params=params, input_output_aliases=aliases or {})


def _row(d):
    return pl.BlockSpec((1, d), lambda *_: (0, 0))


def _full(shape):
    n = len(shape)
    return pl.BlockSpec(shape, lambda *_: (0,) * n)


def _sds(shape, dtype=F32):
    return jax.ShapeDtypeStruct(shape, dtype)


def _ln_fwd(r, gain, bias):
    mu = jnp.mean(r, axis=-1, keepdims=True)
    xc = r - mu
    rstd = lax.rsqrt(jnp.mean(xc * xc, axis=-1, keepdims=True) + LN_EPS)
    xhat = xc * rstd
    return xhat * gain + bias, xhat, rstd


def _ln_bwd(dy, xhat, rstd, gain):
    dxh = dy * gain
    m1 = jnp.mean(dxh, axis=-1, keepdims=True)
    m2 = jnp.mean(dxh * xhat, axis=-1, keepdims=True)
    return rstd * (dxh - m1 - xhat * m2)


def _acc(ref, first, val):
    @pl.when(first)
    def _():
        ref[...] = val

    @pl.when(jnp.logical_not(first))
    def _():
        ref[...] += val


def _pooled_groups(xe, t0, ts):
    pos = (t0 + lax.broadcasted_iota(jnp.int32, (ts, 1), 0) + 1).astype(F32)
    outs = []
    for gi, win in enumerate(POOL_WINDOWS):
        xs = xe[:, gi * POOL_GROUP:(gi + 1) * POOL_GROUP]
        s, k = xs, 1
        while k < win:
            s = s + pltpu.roll(s, k, 0)
            k *= 2
        mean = s[POOL_HALO:] / jnp.minimum(pos, float(win))
        outs.append(mean - xs[POOL_HALO:])
    return outs


def _pool_groups_w(w_ref):
    return [jnp.concatenate([w_ref[s, g] for s in range(N_SHARD)], axis=0) for g in range(4)]


def _pool_fwd(x, wp, pb, ps, gain, bias):
    s_len = x.shape[0]
    ts = min(ROW_TILE, s_len)
    hb = ts // POOL_HALO

    def body(x_ref, halo_ref, w_ref, pb_ref, ps_ref, g_ref, b_ref, y_ref, xhat_ref, rstd_ref):
        i = pl.program_id(0)
        x_t = x_ref[...]
        halo = jnp.where(i > 0, halo_ref[...], 0.0)
        pooled = _pooled_groups(jnp.concatenate([halo, x_t], axis=0), i * ts, ts)
        wg = _pool_groups_w(w_ref)
        y = jnp.concatenate([_bdot(pooled[g], wg[g], NN) for g in range(4)], axis=1) + pb_ref[...]
        r = ALPHA * x_t + y * ps_ref[...]
        y_ref[...], xhat_ref[...], rstd_ref[...] = _ln_fwd(r, g_ref[...], b_ref[...])

    tile = pl.BlockSpec((ts, D_MODEL), lambda i: (i, 0))
    return _call(
        body, "pool_fwd", (s_len // ts,),
        [tile, pl.BlockSpec((POOL_HALO, D_MODEL), lambda i: (jnp.maximum(i * hb - 1, 0), 0)),
         _full(wp.shape), _row(D_MODEL), _row(D_MODEL), _row(D_MODEL), _row(D_MODEL)],
        [tile, tile, pl.BlockSpec((ts, 1), lambda i: (i, 0))],
        [_sds((s_len, D_MODEL)), _sds((s_len, D_MODEL)), _sds((s_len, 1))],
        sem=("parallel",))(x, x, wp, pb, ps, gain, bias)


def _pool_bwd(dy, xhat, rstd, x, wp, pb, ps, gain, scatter=()):
    s_len = x.shape[0]
    ts = min(ROW_TILE, s_len)
    hb = ts // POOL_HALO
    n_t = s_len // ts
    ne = ts + POOL_HALO
    n_s = len(scatter)

    def body(*refs):
        dy_ref, dyn_ref, xh_ref, xhn_ref, rs_ref, rsn_ref, x_ref, xp_ref, w_ref, pb_ref, ps_ref, g_ref = refs[:12]
        dx_ref, dg_ref, db_ref, dps_ref, dpb_ref, dw_ref = refs[12 + n_s:18 + n_s]
        i = pl.program_id(0)
        if n_s:
            start, finish = _scatter_steps(refs[12:12 + n_s], refs[18 + n_s:18 + 2 * n_s], *refs[18 + 2 * n_s:])
            pl.when(i == 0)(start)
        more = i < n_t - 1
        dy_t, xh_t = dy_ref[...], xh_ref[...]
        dy_e = jnp.concatenate([dy_t, jnp.where(more, dyn_ref[...], 0.0)], axis=0)
        xh_e = jnp.concatenate([xh_t, xhn_ref[...]], axis=0)
        rs_e = jnp.concatenate([rs_ref[...], rsn_ref[...]], axis=0)
        dr_e = _ln_bwd(dy_e, xh_e, rs_e, g_ref[...])
        dyy_e = dr_e * ps_ref[...]
        pos_e = (i * ts + lax.broadcasted_iota(jnp.int32, (ne, 1), 0) + 1).astype(F32)
        dxs = []
        wg = _pool_groups_w(w_ref)
        for gi, win in enumerate(POOL_WINDOWS):
            sl = slice(gi * POOL_GROUP, (gi + 1) * POOL_GROUP)
            dpool = _bdot(dyy_e[:, sl], wg[gi], NT)
            s, k = dpool / jnp.minimum(pos_e, float(win)), 1
            while k < win:
                s = s + pltpu.roll(s, ne - k, 0)
                k *= 2
            dxs.append(s[:ts] - dpool[:ts])
        dx_ref[...] = ALPHA * dr_e[:ts] + jnp.concatenate(dxs, axis=1)

        x_t = x_ref[...]
        halo = jnp.where(i > 0, xp_ref[...], 0.0)
        pooled = _pooled_groups(jnp.concatenate([halo, x_t], axis=0), i * ts, ts)
        y = jnp.concatenate([_bdot(pooled[g], wg[g], NN) for g in range(4)], axis=1) + pb_ref[...]
        dr_t, dyy_t = dr_e[:ts], dyy_e[:ts]
        first = i == 0
        _acc(dg_ref, first, jnp.sum(dy_t * xh_t, axis=0, keepdims=True))
        _acc(db_ref, first, jnp.sum(dy_t, axis=0, keepdims=True))
        _acc(dps_ref, first, jnp.sum(dr_t * y, axis=0, keepdims=True))
        _acc(dpb_ref, first, jnp.sum(dyy_t, axis=0, keepdims=True))
        for g in range(4):
            _acc(dw_ref.at[g], first, _bdot(pooled[g], dyy_t[:, g * POOL_GROUP:(g + 1) * POOL_GROUP], TN))
        if n_s:
            pl.when(i == n_t - 1)(finish)

    tile = pl.BlockSpec((ts, D_MODEL), lambda i: (i, 0))
    nxt = pl.BlockSpec((POOL_HALO, D_MODEL), lambda i: (jnp.minimum((i + 1) * hb, n_t * hb - 1), 0))
    prv = pl.BlockSpec((POOL_HALO, D_MODEL), lambda i: (jnp.maximum(i * hb - 1, 0), 0))
    rs_t = pl.BlockSpec((ts, 1), lambda i: (i, 0))
    rs_n = pl.BlockSpec((POOL_HALO, 1), lambda i: (jnp.minimum((i + 1) * hb, n_t * hb - 1), 0))
    row = _row(D_MODEL)
    out = _call(
        body, "pool_bwd", (n_t,),
        [tile, nxt, tile, nxt, rs_t, rs_n, tile, prv, _full(wp.shape), row, row, row] + [ANY] * n_s,
        [tile, row, row, row, row, _full((4, POOL_GROUP, POOL_GROUP))] + [ANY] * n_s,
        [_sds((s_len, D_MODEL))] + [_sds((1, D_MODEL))] * 4 + [_sds((4, POOL_GROUP, POOL_GROUP))]
        + [_sds(t.shape, t.dtype) for t in scatter],
        scratch=_scatter_sems(n_s) if n_s else [], sem=("arbitrary",),
    )(dy, dy, xhat, xhat, rstd, rstd, x, x, wp, pb, ps, gain, *scatter)
    return out[:6], out[6:]


def _mlp_weight_specs():
    fc = D_FF // N_SHARD
    return [pl.BlockSpec((None, D_MODEL, fc), lambda i, j: (j, 0, 0)),
            pl.BlockSpec((None, fc, D_MODEL), lambda i, j: (j, 0, 0)),
            _full((N_SHARD, D_MODEL // N_SHARD, D_MODEL)),
            _row(D_MODEL),
            _full((N_SHARD, PLE_DIM, D_MODEL // N_SHARD))]


def _gate_w(gw_ref):
    return gw_ref[...].reshape(D_MODEL, D_MODEL)


def _ple_proj(pj_ref):
    return jnp.concatenate([pj_ref[s] for s in range(N_SHARD)], axis=1)


def _mlp_fwd(x, p, w1s, w2s, gw, gb, proj, gain, bias, name, gather=(), target=None):
    s_len = x.shape[0]
    ts = min(ROW_TILE, s_len)
    n_i = s_len // ts
    n_g = len(gather)
    has_t = target is not None
    n_in, n_out = 9 + has_t, 5 + has_t

    def body(*refs):
        x_ref, p_ref, w1_ref, w2_ref, gw_ref, gb_ref, pj_ref, g_ref, b_ref = refs[:9]
        y_ref, xhat_ref, rstd_ref, a_ref, xbo_ref = refs[n_in + n_g:n_in + n_g + 5]
        acc_ref, xb_ref = refs[n_in + n_out + 2 * n_g:n_in + n_out + 2 * n_g + 2]
        i, j = pl.program_id(0), pl.program_id(1)
        if n_g:
            start, finish = _gather_steps(refs[n_in + n_out + n_g:n_in + n_out + 2 * n_g], *refs[n_in + n_out + 2 * n_g + 2:])
            pl.when((i == 0) & (j == 0))(start)

        @pl.when(j == 0)
        def _():
            x_t = x_ref[...]
            xb_ref[...] = x_t.astype(BF16)
            xbo_ref[...] = x_t.astype(BF16)
            gate = _sigmoid(_bdot(x_t, _gate_w(gw_ref), NN) + gb_ref[...])
            acc_ref[...] = ALPHA * x_t + gate * _bdot(p_ref[...], _ple_proj(pj_ref), NN)

        h = jnp.maximum(_bdot(xb_ref[...], w1_ref[...], NN), 0.0)
        a = (h * h).astype(BF16)
        a_ref[...] = a
        acc_ref[...] += _bdot(a, w2_ref[...], NN)

        @pl.when(j == N_SHARD - 1)
        def _():
            y, xhat_ref[...], rstd_ref[...] = _ln_fwd(acc_ref[...], g_ref[...], b_ref[...])
            if has_t:
                err = y - refs[9][...]
                y_ref[...] = err * (1.0 / D_MODEL)
                part = 0.5 * jnp.sum(jnp.mean(err * err, axis=-1, keepdims=True))
                _acc(refs[n_in + n_g + 5], i == 0, part + jnp.zeros((1, 128), F32))
            else:
                y_ref[...] = y

        if n_g:
            pl.when((i == n_i - 1) & (j == N_SHARD - 1))(finish)

    tile = pl.BlockSpec((ts, D_MODEL), lambda i, j: (i, 0))
    row = _row(D_MODEL)
    out = _call(
        body, name, (n_i, N_SHARD),
        [tile, pl.BlockSpec((ts, PLE_DIM), lambda i, j: (i, 0))] + _mlp_weight_specs() + [row, row] + [tile] * has_t
        + [ANY] * n_g,
        [tile, tile, pl.BlockSpec((ts, 1), lambda i, j: (i, 0)), pl.BlockSpec((ts, D_FF // N_SHARD), lambda i, j: (i, j)), tile]
        + [_row(128)] * has_t + [ANY] * n_g,
        [_sds((s_len, D_MODEL)), _sds((s_len, D_MODEL)), _sds((s_len, 1)), _sds((s_len, D_FF), BF16), _sds((s_len, D_MODEL), BF16)]
        + [_sds((1, 128))] * has_t + [_sds(a.shape, a.dtype) for a in gather],
        scratch=[pltpu.VMEM((ts, D_MODEL), F32), pltpu.VMEM((ts, D_MODEL), BF16)] + (_gather_sems(n_g) if n_g else []),
        sem=("arbitrary", "arbitrary"), aliases={n_in + k: n_out + k for k in range(n_g)},
    )(x, p, w1s, w2s, gw, gb, proj, gain, bias, *([target] if has_t else []), *gather)
    return out[:n_out], out[n_out:]


def _mlp_bwd(dy, xhat, rstd, x, a, p, w1s, w2s, gw, gb, proj, gain, name, scatter=()):
    s_len = x.shape[0]
    ts = min(ROW_TILE, s_len)
    fc = D_FF // N_SHARD
    n_i = s_len // ts
    n_s = len(scatter)

    def body(*refs):
        dy_ref, xh_ref, rs_ref, x_ref, a_ref, p_ref, w1_ref, w2_ref, gw_ref, gb_ref, pj_ref, g_ref = refs[:12]
        dx_ref, dh_ref, dzg_ref, dpp_ref, drb_ref, dg_ref, db_ref, dgb_ref = refs[12 + n_s:20 + n_s]
        acc_ref, dr_ref = refs[20 + 2 * n_s:22 + 2 * n_s]
        i, j = pl.program_id(0), pl.program_id(1)
        if n_s:
            start, finish = _scatter_steps(refs[12:12 + n_s], refs[20 + n_s:20 + 2 * n_s], *refs[22 + 2 * n_s:])
            pl.when((i == 0) & (j == 0))(start)

        @pl.when(j == 0)
        def _():
            dy_t, xh_t, x_t = dy_ref[...], xh_ref[...], x_ref[...]
            dr = _ln_bwd(dy_t, xh_t, rs_ref[...], g_ref[...])
            drb = dr.astype(BF16)
            dr_ref[...] = drb
            drb_ref[...] = drb
            gw_full = _gate_w(gw_ref)
            gate = _sigmoid(_bdot(x_t, gw_full, NN) + gb_ref[...])
            pp = _bdot(p_ref[...], _ple_proj(pj_ref), NN)
            dzg = dr * pp * gate * (1.0 - gate)
            dzg_ref[...] = dzg.astype(BF16)
            dpp_ref[...] = (dr * gate).astype(BF16)
            acc_ref[...] = ALPHA * dr + _bdot(dzg, gw_full, NT)
            first = i == 0
            _acc(dg_ref, first, jnp.sum(dy_t * xh_t, axis=0, keepdims=True))
            _acc(db_ref, first, jnp.sum(dy_t, axis=0, keepdims=True))
            _acc(dgb_ref, first, jnp.sum(dzg, axis=0, keepdims=True))

        dh = (_bdot(dr_ref[...], w2_ref[...], NT) * (2.0 * jnp.sqrt(a_ref[...].astype(F32)))).astype(BF16)
        dh_ref[...] = dh
        acc_ref[...] += _bdot(dh, w1_ref[...], NT)

        @pl.when(j == N_SHARD - 1)
        def _():
            dx_ref[...] = acc_ref[...]

        if n_s:
            pl.when((i == n_i - 1) & (j == N_SHARD - 1))(finish)

    tile = pl.BlockSpec((ts, D_MODEL), lambda i, j: (i, 0))
    ftile = pl.BlockSpec((ts, fc), lambda i, j: (i, j))
    row = _row(D_MODEL)
    out = _call(
        body, name, (n_i, N_SHARD),
        [tile, tile, pl.BlockSpec((ts, 1), lambda i, j: (i, 0)), tile, ftile, pl.BlockSpec((ts, PLE_DIM), lambda i, j: (i, 0))]
        + _mlp_weight_specs() + [row] + [ANY] * n_s,
        [tile, ftile, tile, tile, tile, row, row, row] + [ANY] * n_s,
        [_sds((s_len, D_MODEL)), _sds((s_len, D_FF), BF16)]
        + [_sds((s_len, D_MODEL), BF16)] * 3 + [_sds((1, D_MODEL))] * 3 + [_sds(t.shape, t.dtype) for t in scatter],
        scratch=[pltpu.VMEM((ts, D_MODEL), F32), pltpu.VMEM((ts, D_MODEL), BF16)] + (_scatter_sems(n_s) if n_s else []),
        sem=("arbitrary", "arbitrary"))(dy, xhat, rstd, x, a, p, w1s, w2s, gw, gb, proj, gain, *scatter)
    return out[:8], out[8:]


def _wgrad(a, b, name, stack_cols=False):
    s_len, m = a.shape
    n = b.shape[1]
    ts = min(2048 if a.dtype == BF16 and b.dtype == BF16 else 1024, s_len)
    tm = min(m, 1024)
    tn = n // N_SHARD if stack_cols else (1408 if n == GDN_IN_PAD else min(n, 1024))
    n_s = s_len // ts

    def body(a_ref, b_ref, o_ref):
        _acc(o_ref, pl.program_id(2) == 0, _bdot(a_ref[...], b_ref[...], TN))

    if stack_cols:
        out_spec = pl.BlockSpec((None, tm, tn), lambda mi, nj, s: (nj, mi, 0))
        out_shape = _sds((N_SHARD, m, tn))
    else:
        out_spec = pl.BlockSpec((tm, tn), lambda mi, nj, s: (mi, nj))
        out_shape = _sds((m, n))
    return _call(
        body, name, (m // tm, n // tn, n_s),
        [pl.BlockSpec((ts, tm), lambda mi, nj, s: (s, mi)), pl.BlockSpec((ts, tn), lambda mi, nj, s: (s, nj))],
        out_spec, out_shape, sem=("parallel", "parallel", "arbitrary"))(a, b)


def _act_qkv(y):
    qkv = _silu(y)
    qs, ks = [], []
    for h in range(HEADS):
        qh = qkv[:, h * HEAD_DIM:(h + 1) * HEAD_DIM]
        kh = qkv[:, D_MODEL + h * HEAD_DIM:D_MODEL + (h + 1) * HEAD_DIM]
        qs.append(qh * (lax.rsqrt(jnp.sum(qh * qh, axis=-1, keepdims=True) + L2_EPS) * HEAD_DIM ** -0.5))
        ks.append(kh * lax.rsqrt(jnp.sum(kh * kh, axis=-1, keepdims=True) + L2_EPS))
    return jnp.concatenate(qs, axis=1), jnp.concatenate(ks, axis=1), qkv[:, 2 * D_MODEL:]


def _act_gb(ba, alog_l, dtb_l, tril):
    lane = lax.broadcasted_iota(jnp.int32, ba.shape, 1)
    g = jnp.where((lane >= HEADS) & (lane < 2 * HEADS), -jnp.exp(alog_l) * _softplus(ba + dtb_l), 0.0)
    return jnp.where(lane < HEADS, _sigmoid(ba), _hdot(tril, g, NN))


def _chunk_tril(t):
    ii = lax.broadcasted_iota(jnp.int32, (t, t), 0)
    jj = lax.broadcasted_iota(jnp.int32, (t, t), 1)
    return ((ii // CHUNK == jj // CHUNK) & (ii >= jj)).astype(F32)


def _conv_rows(xe, w, n_rows):
    y = xe[CONV_HALO:CONV_HALO + n_rows] * w[CONV_WIDTH - 1]
    for j in range(CONV_WIDTH - 1):
        y = y + pltpu.roll(xe, CONV_WIDTH - 1 - j, 0)[CONV_HALO:CONV_HALO + n_rows] * w[j]
    return y


def _conv_fwd(x, w_in, conv_w, alog_l, dtb_l):
    s_len = x.shape[0]
    ts = min(CONV_TILE, s_len)
    hb = ts // CONV_HALO

    def body(x_ref, xp_ref, w_ref, al_ref, dt_ref, win_hbm, proj_ref, y_ref, q_ref, k_ref, v_ref, gcb_ref, win_ref, win_sem):
        i = pl.program_id(0)

        @pl.when(i == 0)
        def _():
            cp = pltpu.make_async_copy(win_hbm, win_ref, win_sem)
            cp.start()
            cp.wait()

        proj = _bdot(x_ref[...], win_ref[...], NN)
        proj_ref[...] = proj
        halo = jnp.where(i > 0, _bdot(xp_ref[...], win_ref[:, :QKV_DIM], NN), 0.0)
        taps = [w_ref[pl.ds(j, 1), :] for j in range(CONV_WIDTH)]
        y = _conv_rows(jnp.concatenate([halo, proj[:, :QKV_DIM]], axis=0), taps, ts)
        y_ref[...] = y
        q_ref[...], k_ref[...], v_ref[...] = _act_qkv(y)
        gcb_ref[...] = _act_gb(proj[:, BA_BLOCK * 128:], al_ref[...], dt_ref[...], _chunk_tril(ts))

    tile = pl.BlockSpec((ts, D_MODEL), lambda i: (i, 0))
    return _call(
        body, "gdn_conv_fwd", (s_len // ts,),
        [tile, pl.BlockSpec((CONV_HALO, D_MODEL), lambda i: (jnp.maximum(i * hb - 1, 0), 0)),
         _full((CONV_WIDTH, QKV_DIM)), _row(128), _row(128), ANY],
        [pl.BlockSpec((ts, GDN_IN_PAD), lambda i: (i, 0)), pl.BlockSpec((ts, QKV_DIM), lambda i: (i, 0)), tile, tile, tile,
         pl.BlockSpec((ts, 128), lambda i: (i, 0))],
        [_sds((s_len, GDN_IN_PAD)), _sds((s_len, QKV_DIM))] + [_sds((s_len, D_MODEL))] * 3 + [_sds((s_len, 128))],
        scratch=[pltpu.VMEM(w_in.shape, w_in.dtype), pltpu.SemaphoreType.DMA],
        sem=("arbitrary",))(x, x, conv_w, alog_l, dtb_l, w_in)


def _conv_bwd(proj, y, dq, dk, dv, dgcb, dz, conv_w, alog_l, dtb_l, w_in, dres):
    s_len = proj.shape[0]
    ts = min(CONV_TILE, s_len)
    hb = ts // CONV_HALO
    n_t = s_len // ts
    te = ts + CONV_HALO

    def body(x_ref, y_ref, yn_ref, ba_ref, dq_ref, dqn_ref, dk_ref, dkn_ref, dv_ref, dvn_ref, dgcb_ref, dz_ref,
             w_ref, al_ref, dt_ref, win_hbm, dres_ref, dp_ref, dx_ref, dw_ref, dal_ref, ddt_ref, win_ref, win_sem):
        i = pl.program_id(0)

        @pl.when(i == 0)
        def _():
            cp = pltpu.make_async_copy(win_hbm, win_ref, win_sem)
            cp.start()
            cp.wait()

        more = i < n_t - 1
        w = [w_ref[pl.ds(j, 1), :] for j in range(CONV_WIDTH)]
        x_t = x_ref[...]
        _, act_vjp = jax.vjp(_act_qkv, jnp.concatenate([y_ref[...], yn_ref[...]], axis=0))
        ct = tuple(jnp.concatenate([t[...], jnp.where(more, n[...], 0.0)], axis=0)
                   for t, n in ((dq_ref, dqn_ref), (dk_ref, dkn_ref), (dv_ref, dvn_ref)))
        (dy_e,) = act_vjp(ct)
        ahead = [pltpu.roll(dy_e, te - (CONV_WIDTH - 1 - j), 0)[:ts] for j in range(CONV_WIDTH - 1)] + [dy_e[:ts]]
        dx = ahead[CONV_WIDTH - 1] * w[CONV_WIDTH - 1]
        for j in range(CONV_WIDTH - 1):
            dx = dx + ahead[j] * w[j]
        dws = [jnp.sum(ahead[j] * x_t, axis=0, keepdims=True) for j in range(CONV_WIDTH)]
        _, gb_vjp = jax.vjp(lambda ba, al, dt: _act_gb(ba, al, dt, _chunk_tril(ts)), ba_ref[...], al_ref[...], dt_ref[...])
        dba, dal, ddt = gb_vjp(dgcb_ref[...])
        dp = jnp.concatenate([dx.astype(BF16), dz_ref[...], dba.astype(BF16)], axis=1)
        dp_ref[...] = dp
        dx_ref[...] = dres_ref[...] + _bdot(dp, win_ref[...], NT)
        first = i == 0
        for j in range(CONV_WIDTH):
            _acc(dw_ref.at[pl.ds(j, 1), :], first, dws[j])
        _acc(dal_ref, first, dal)
        _acc(ddt_ref, first, ddt)

    tile = pl.BlockSpec((ts, D_MODEL), lambda i: (i, 0))
    nxt = pl.BlockSpec((CONV_HALO, D_MODEL), lambda i: (jnp.minimum((i + 1) * hb, n_t * hb - 1), 0))
    return _call(
        body, "gdn_conv_bwd", (n_t,),
        [pl.BlockSpec((ts, QKV_DIM), lambda i: (i, 0)), pl.BlockSpec((ts, QKV_DIM), lambda i: (i, 0)),
         pl.BlockSpec((CONV_HALO, QKV_DIM), lambda i: (jnp.minimum((i + 1) * hb, n_t * hb - 1), 0)),
         pl.BlockSpec((ts, 128), lambda i: (i, BA_BLOCK)),
         tile, nxt, tile, nxt, tile, nxt, pl.BlockSpec((ts, 128), lambda i: (i, 0)), tile,
         _full((CONV_WIDTH, QKV_DIM)), _row(128), _row(128), ANY, tile],
        [pl.BlockSpec((ts, GDN_IN_PAD), lambda i: (i, 0)), tile, _full((CONV_WIDTH, QKV_DIM)), _row(128), _row(128)],
        [_sds((s_len, GDN_IN_PAD), BF16), _sds((s_len, D_MODEL)), _sds((CONV_WIDTH, QKV_DIM)), _sds((1, 128)), _sds((1, 128))],
        scratch=[pltpu.VMEM(w_in.shape, w_in.dtype), pltpu.SemaphoreType.DMA], sem=("arbitrary",),
    )(proj, y, y, proj, dq, dq, dk, dk, dv, dv, dgcb, dz, conv_w, alog_l, dtb_l, w_in, dres)


def _tri_inv(a_strict):
    ii = lax.broadcasted_iota(jnp.int32, (CHUNK, CHUNK), 0)
    jj = lax.broadcasted_iota(jnp.int32, (CHUNK, CHUNK), 1)
    x = (ii == jj).astype(F32) - a_strict
    pw = _bdot(a_strict, a_strict, BNN)
    for step in range(5):
        x = x + _bdot(x, pw, BNN)
        if step < 4:
            pw = _bdot(pw, pw, BNN)
    return x


@jax.custom_vjp
def _solved(a_strict, rhs, t, sol):
    return sol


def _solved_fwd(a_strict, rhs, t, sol):
    return sol, (t, sol)


def _solved_bwd(res, dsol):
    t, sol = res
    drhs = _mdot(t, dsol, BTN)
    return -_mdot(drhs, sol, BNT), drhs, jnp.zeros_like(t), jnp.zeros_like(sol)


_solved.defvjp(_solved_fwd, _solved_bwd)


def _prep(q, k, v, gc, beta, solve):
    ii = lax.broadcasted_iota(jnp.int32, (CHUNK, CHUNK), 0)
    jj = lax.broadcasted_iota(jnp.int32, (CHUNK, CHUNK), 1)
    causal, strict = ii >= jj, ii > jj
    gc_row = jnp.sum((ii == jj).astype(F32) * gc, axis=1, keepdims=True)
    decay = jnp.where(causal, jnp.exp(jnp.where(causal, gc - gc_row, 0.0)), 0.0)
    kb = k * beta
    a = jnp.where(strict, _bdot(kb, k, BNT) * decay, 0.0)
    eg = jnp.exp(gc)
    sol = solve(a, jnp.concatenate([v * beta, kb * eg], axis=-1))
    qk = _bdot(q, k, BNT) * decay
    last = lax.broadcasted_iota(jnp.int32, (CHUNK, 1), 0) == CHUNK - 1
    g_last = jnp.sum(jnp.where(last, gc, 0.0), axis=1, keepdims=True)
    kd = k * jnp.exp(g_last - gc)
    gl = jnp.exp(g_last) + jnp.zeros((1, 1, HEAD_DIM), F32)
    return sol[..., :HEAD_DIM], sol[..., HEAD_DIM:], qk, q * eg, kd, gl


def _prep_specs(s_len):
    rows = min(PREP_CHUNKS, s_len // CHUNK) * CHUNK
    m = rows // CHUNK
    hd = pl.BlockSpec((rows, HEAD_DIM), lambda c, h: (c, h))
    gcb = pl.BlockSpec((rows, 128), lambda c, h: (c, 0))
    qk = pl.BlockSpec((None, rows, CHUNK), lambda c, h: (h, c, 0))
    gl = pl.BlockSpec((None, m, HEADS, HEAD_DIM), lambda c, h: (c, 0, 0, 0))
    return rows, m, hd, gcb, qk, gl


def _head_cols(gcb, h, m):
    lane = lax.broadcasted_iota(jnp.int32, gcb.shape, 1)
    pick = lambda at: jnp.sum(jnp.where(lane == at, gcb, 0.0), axis=1, keepdims=True).reshape(m, CHUNK, 1)
    return pick(h + HEADS), pick(h)


def _gdn_prep(q, k, v, gcb):
    s_len = q.shape[0]
    rows, m, hd, gcb_spec, qk_spec, gl_spec = _prep_specs(s_len)

    def body(q_ref, k_ref, v_ref, gcb_ref, u_ref, w_ref, qd_ref, kd_ref, qk_ref, gl_ref, t_ref):
        r3 = lambda ref, d: ref[...].reshape(m, CHUNK, d)
        gc, beta = _head_cols(gcb_ref[...], pl.program_id(1), m)

        def solve(a, rhs):
            t = _tri_inv(a)
            t_ref[...] = t.reshape(rows, CHUNK)
            return _mdot(t, rhs, BNN)

        u, w, qk, qd, kd, gl = _prep(r3(q_ref, HEAD_DIM), r3(k_ref, HEAD_DIM), r3(v_ref, HEAD_DIM), gc, beta, solve)
        u_ref[...] = u.reshape(rows, HEAD_DIM)
        w_ref[...] = w.reshape(rows, HEAD_DIM)
        qd_ref[...] = qd.reshape(rows, HEAD_DIM).astype(BF16)
        kd_ref[...] = kd.reshape(rows, HEAD_DIM).astype(BF16)
        qk_ref[...] = qk.reshape(rows, CHUNK).astype(BF16)
        gl_ref[:, pl.ds(pl.program_id(1), 1), :] = gl

    n_g = s_len // rows
    return _call(
        body, "gdn_prep", (n_g, HEADS), [hd, hd, hd, gcb_spec], [hd, hd, hd, hd, qk_spec, gl_spec, qk_spec],
        [_sds((s_len, D_MODEL))] * 2 + [_sds((s_len, D_MODEL), BF16)] * 2
        + [_sds((HEADS, s_len, CHUNK), BF16), _sds((n_g, m, HEADS, HEAD_DIM)), _sds((HEADS, s_len, CHUNK))],
        sem=("parallel", "arbitrary"))(q, k, v, gcb)


def _gdn_prep_bwd(q, k, v, gcb, t_inv, u, w, du, dw, dqd, dkd, dqk, dgl):
    s_len = q.shape[0]
    rows, m, hd, gcb_spec, qk_spec, gl_spec = _prep_specs(s_len)

    def body(q_ref, k_ref, v_ref, gcb_ref, t_ref, u_ref, w_ref, du_ref, dw_ref, dqd_ref, dkd_ref, dqk_ref, dgl_ref,
             dq_ref, dk_ref, dv_ref, dgcb_ref):
        h = pl.program_id(1)
        r3 = lambda ref, d: ref[...].reshape(m, CHUNK, d)
        gc, beta = _head_cols(gcb_ref[...], h, m)
        t = r3(t_ref, CHUNK)
        sol = jnp.concatenate([r3(u_ref, HEAD_DIM), r3(w_ref, HEAD_DIM)], axis=-1)
        fn = lambda q_, k_, v_, gc_, bt_: _prep(q_, k_, v_, gc_, bt_, lambda a, rhs: _solved(a, rhs, t, sol))
        _, vjp = jax.vjp(fn, r3(q_ref, HEAD_DIM), r3(k_ref, HEAD_DIM), r3(v_ref, HEAD_DIM), gc, beta)
        ct = (r3(du_ref, HEAD_DIM), r3(dw_ref, HEAD_DIM), r3(dqk_ref, CHUNK), r3(dqd_ref, HEAD_DIM), r3(dkd_ref, HEAD_DIM),
              dgl_ref[:, pl.ds(h, 1), :] * (1.0 / HEAD_DIM))
        dq, dk, dv, dgc, dbt = vjp(ct)
        dq_ref[...] = dq.reshape(rows, HEAD_DIM)
        dk_ref[...] = dk.reshape(rows, HEAD_DIM)
        dv_ref[...] = dv.reshape(rows, HEAD_DIM)
        lane = lax.broadcasted_iota(jnp.int32, (rows, 128), 1)
        mine = jnp.where(lane == h, dbt.reshape(rows, 1), 0.0) + jnp.where(lane == h + HEADS, dgc.reshape(rows, 1), 0.0)
        _acc(dgcb_ref, h == 0, mine)

    return _call(
        body, "gdn_prep_bwd", (s_len // rows, HEADS),
        [hd, hd, hd, gcb_spec, qk_spec, hd, hd, hd, hd, hd, hd, qk_spec, gl_spec], [hd, hd, hd, gcb_spec],
        [_sds((s_len, D_MODEL))] * 3 + [_sds((s_len, 128))],
        sem=("parallel", "arbitrary"))(q, k, v, gcb, t_inv, u, w, du, dw, dqd, dkd, dqk, dgl)


def _scan_specs(n_c, m, k, reverse):
    n_b = n_c // k
    at = (lambda n: n_b - 1 - n) if reverse else (lambda n: n)
    row = pl.BlockSpec((k * CHUNK, D_MODEL), lambda n: (at(n), 0))
    qk = pl.BlockSpec((HEADS, k * CHUNK, CHUNK), lambda n: (0, at(n), 0))
    gl = pl.BlockSpec((None, k, HEADS, HEAD_DIM), lambda n: (at(n) // (m // k), at(n) % (m // k), 0, 0))
    st = pl.BlockSpec((k, HEADS, HEAD_DIM, HEAD_DIM), lambda n: (at(n), 0, 0, 0))
    return row, qk, gl, st


def _gdn_scan(u, w, qd, kd, qk, gl):
    s_len = u.shape[0]
    n_c = s_len // CHUNK
    k = min(SCAN_CHUNKS, gl.shape[1])
    row, qk_spec, gl_spec, st_spec = _scan_specs(n_c, gl.shape[1], k, False)

    def body(u_ref, w_ref, qd_ref, kd_ref, qk_ref, gl_ref, o_ref, st_ref, state):
        hs = range(HEADS)
        sl = [slice(h * HEAD_DIM, (h + 1) * HEAD_DIM) for h in hs]
        first = pl.program_id(0) == 0
        s_all = [jnp.where(first, 0.0, state[h]) for h in hs]
        for c in range(k):
            rows = slice(c * CHUNK, (c + 1) * CHUNK)
            s_b = [s.astype(BF16) for s in s_all]
            ws = [_bdot(w_ref[rows, sl[h]], s_b[h], NN) for h in hs]
            qs = [_bdot(qd_ref[rows, sl[h]], s_b[h], NN) for h in hs]
            vn = [(u_ref[rows, sl[h]] - ws[h]).astype(BF16) for h in hs]
            outs = [qs[h] + _bdot(qk_ref[h, rows, :], vn[h], NN) for h in hs]
            nxt = [s_all[h] * gl_ref[c, pl.ds(h, 1), :] + _bdot(kd_ref[rows, sl[h]], vn[h], TN) for h in hs]
            for h in hs:
                st_ref[c, h] = s_all[h]
                o_ref[rows, sl[h]] = outs[h]
            s_all = nxt
        for h in hs:
            state[h] = s_all[h]

    return _call(
        body, "gdn_scan", (n_c // k,), [row, row, row, row, qk_spec, gl_spec], [row, st_spec],
        [_sds((s_len, D_MODEL)), _sds((n_c, HEADS, HEAD_DIM, HEAD_DIM))],
        scratch=[pltpu.VMEM((HEADS, HEAD_DIM, HEAD_DIM), F32)], sem=("arbitrary",))(u, w, qd, kd, qk, gl)


def _gdn_scan_bwd(do, u, w, qd, kd, qk, gl, states):
    s_len = u.shape[0]
    n_c = s_len // CHUNK
    k = min(SCAN_CHUNKS, gl.shape[1])
    row, qk_spec, gl_spec, st_spec = _scan_specs(n_c, gl.shape[1], k, True)

    def body(do_ref, u_ref, w_ref, qd_ref, kd_ref, qk_ref, gl_ref, st_ref,
             du_ref, dw_ref, dqd_ref, dkd_ref, dqk_ref, dgl_ref, dstate):
        hs = range(HEADS)
        sl = [slice(h * HEAD_DIM, (h + 1) * HEAD_DIM) for h in hs]
        first = pl.program_id(0) == 0
        ds_f = [jnp.where(first, 0.0, dstate[h]) for h in hs]
        for c in reversed(range(k)):
            rows = slice(c * CHUNK, (c + 1) * CHUNK)
            s_f = [st_ref[c, h] for h in hs]
            s_b = [s.astype(BF16) for s in s_f]
            ds_b = [d.astype(BF16) for d in ds_f]
            do_b = [do_ref[rows, sl[h]].astype(BF16) for h in hs]
            w_b = [w_ref[rows, sl[h]].astype(BF16) for h in hs]
            ws = [_bdot(w_b[h], s_b[h], NN) for h in hs]
            dvn = [_bdot(qk_ref[h, rows, :], do_b[h], TN) + _bdot(kd_ref[rows, sl[h]], ds_b[h], NN) for h in hs]
            dqd = [_bdot(do_b[h], s_b[h], NT) for h in hs]
            t_do = [_bdot(qd_ref[rows, sl[h]], do_b[h], TN) for h in hs]
            vn = [(u_ref[rows, sl[h]] - ws[h]).astype(BF16) for h in hs]
            dvn_b = [d.astype(BF16) for d in dvn]
            dw = [-_bdot(dvn_b[h], s_b[h], NT) for h in hs]
            dkd = [_bdot(vn[h], ds_b[h], NT) for h in hs]
            dqk = [_bdot(do_b[h], vn[h], NT) for h in hs]
            t_dv = [_bdot(w_b[h], dvn_b[h], TN) for h in hs]
            for h in hs:
                du_ref[rows, sl[h]], dw_ref[rows, sl[h]], dqd_ref[rows, sl[h]], dkd_ref[rows, sl[h]] = dvn[h], dw[h], dqd[h], dkd[h]
                dqk_ref[h, rows, :] = dqk[h]
                dgl_ref[c, pl.ds(h, 1), :] = jnp.sum(s_f[h] * ds_f[h]) + jnp.zeros((1, HEAD_DIM), F32)
            ds_f = [ds_f[h] * gl_ref[c, pl.ds(h, 1), :] + t_do[h] - t_dv[h] for h in hs]
        for h in hs:
            dstate[h] = ds_f[h]

    return _call(
        body, "gdn_scan_bwd", (n_c // k,), [row, row, row, row, row, qk_spec, gl_spec, st_spec],
        [row, row, row, row, qk_spec, gl_spec],
        [_sds((s_len, D_MODEL))] * 4 + [_sds((HEADS, s_len, CHUNK)), _sds(gl.shape)],
        scratch=[pltpu.VMEM((HEADS, HEAD_DIM, HEAD_DIM), F32)], sem=("arbitrary",))(do, u, w, qd, kd, qk, gl, states)


def _gate_norm(o, z, nw):
    outs = []
    for h in range(HEADS):
        oh = o[:, h * HEAD_DIM:(h + 1) * HEAD_DIM]
        outs.append(oh * lax.rsqrt(jnp.mean(oh * oh, axis=-1, keepdims=True) + RMS_EPS))
    return jnp.concatenate(outs, axis=1) * nw * _silu(z)


def _gdn_out_fwd(o, proj, x, w_out, nw, gain, bias):
    s_len = x.shape[0]
    ts = min(ROW_TILE, s_len)

    def body(o_ref, z_ref, x_ref, w_ref, nw_ref, g_ref, b_ref, y_ref, xhat_ref, rstd_ref):
        on = _gate_norm(o_ref[...], z_ref[...], nw_ref[...])
        r = ALPHA * x_ref[...] + _bdot(on, w_ref[...], NN)
        y_ref[...], xhat_ref[...], rstd_ref[...] = _ln_fwd(r, g_ref[...], b_ref[...])

    tile = pl.BlockSpec((ts, D_MODEL), lambda i: (i, 0))
    row = _row(D_MODEL)
    return _call(
        body, "gdn_out_fwd", (s_len // ts,),
        [tile, pl.BlockSpec((ts, D_MODEL), lambda i: (i, QKV_DIM // D_MODEL)), tile, _full((D_MODEL, D_MODEL)), row, row, row],
        [tile, tile, pl.BlockSpec((ts, 1), lambda i: (i, 0))],
        [_sds((s_len, D_MODEL)), _sds((s_len, D_MODEL)), _sds((s_len, 1))], sem=("parallel",))(o, proj, x, w_out, nw, gain, bias)


def _gdn_out_bwd(dy, xhat, rstd, o, proj, w_out, nw, gain):
    s_len = o.shape[0]
    ts = min(ROW_TILE, s_len)

    def body(dy_ref, xh_ref, rs_ref, o_ref, z_ref, w_ref, nw_ref, g_ref,
             dres_ref, do_ref, dz_ref, on_ref, drb_ref, dg_ref, db_ref, dnw_ref):
        dy_t, xh_t = dy_ref[...], xh_ref[...]
        dr = _ln_bwd(dy_t, xh_t, rs_ref[...], g_ref[...])
        dres_ref[...] = ALPHA * dr
        drb_ref[...] = dr.astype(BF16)
        on, vjp = jax.vjp(_gate_norm, o_ref[...], z_ref[...], nw_ref[...])
        on_ref[...] = on.astype(BF16)
        do, dz, dnw = vjp(_bdot(dr, w_ref[...], NT))
        do_ref[...] = do
        dz_ref[...] = dz.astype(BF16)
        first = pl.program_id(0) == 0
        _acc(dg_ref, first, jnp.sum(dy_t * xh_t, axis=0, keepdims=True))
        _acc(db_ref, first, jnp.sum(dy_t, axis=0, keepdims=True))
        _acc(dnw_ref, first, sum(dnw[:, h * HEAD_DIM:(h + 1) * HEAD_DIM] for h in range(HEADS)))

    tile = pl.BlockSpec((ts, D_MODEL), lambda i: (i, 0))
    row = _row(D_MODEL)
    return _call(
        body, "gdn_out_bwd", (s_len // ts,),
        [tile, tile, pl.BlockSpec((ts, 1), lambda i: (i, 0)), tile,
         pl.BlockSpec((ts, D_MODEL), lambda i: (i, QKV_DIM // D_MODEL)), _full((D_MODEL, D_MODEL)), row, row],
        [tile, tile, tile, tile, tile, row, row, _row(HEAD_DIM)],
        [_sds((s_len, D_MODEL))] * 2 + [_sds((s_len, D_MODEL), BF16)] * 3 + [_sds((1, D_MODEL))] * 2 + [_sds((1, HEAD_DIM))],
        sem=("arbitrary",))(dy, xhat, rstd, o, proj, w_out, nw, gain)


def _adamw(w, g, m, v, name):
    r, c = w.shape
    tr = min(ADAM_ROWS, r)
    pieces = list(g) if isinstance(g, (list, tuple)) else [g]
    n_p = len(pieces)
    blocks = [a.shape[0] // tr for a in pieces]
    first = [sum(blocks[:k]) for k in range(n_p)]

    def body(*refs):
        w_ref, g_refs, (m_ref, v_ref) = refs[0], refs[1:1 + n_p], refs[1 + n_p:3 + n_p]
        go_ref, d_ref, nm_ref, nv_ref = refs[3 + n_p:]
        i = pl.program_id(0)
        g_t = g_refs[0][...]
        for k in range(1, n_p):
            g_t = jnp.where(i >= first[k], g_refs[k][...], g_t)
        go_ref[...] = g_t
        nm = ADAM_B1 * m_ref[...] + (1.0 - ADAM_B1) * g_t
        nv = ADAM_B2 * v_ref[...] + (1.0 - ADAM_B2) * (g_t * g_t)
        m_hat = nm / (1.0 - ADAM_B1 ** ADAM_STEP)
        v_hat = nv / (1.0 - ADAM_B2 ** ADAM_STEP)
        d_ref[...] = -ADAM_LR * (m_hat / (jnp.sqrt(v_hat) + ADAM_EPS) + ADAM_WD * w_ref[...])
        nm_ref[...] = nm
        nv_ref[...] = nv

    tile = pl.BlockSpec((tr, c), lambda i: (i, 0))
    g_specs = [pl.BlockSpec((tr, c), lambda i, k=k: (jnp.clip(i - first[k], 0, blocks[k] - 1), 0)) for k in range(n_p)]
    return _call(body, name, (r // tr,), [tile] + g_specs + [tile] * 2, [tile] * 4, [_sds((r, c))] * 4,
                 sem=("parallel",))(w, *pieces, m, v)


def _assemble_w_in(shards):
    rows = 256
    width = GDN_IN_DIM // N_SHARD

    def body(s_ref, o_ref):
        pad = jnp.zeros((rows, GDN_IN_PAD - GDN_IN_DIM), shards.dtype)
        o_ref[...] = jnp.concatenate([s_ref[j] for j in range(N_SHARD)] + [pad], axis=1)

    return _call(body, "w_in_assemble", (D_MODEL // rows,), [pl.BlockSpec((N_SHARD, rows, width), lambda i: (0, i, 0))],
                 pl.BlockSpec((rows, GDN_IN_PAD), lambda i: (i, 0)), _sds((D_MODEL, GDN_IN_PAD), shards.dtype),
                 sem=("parallel",))(shards)


def _split_w_in(full):
    rows = 256
    width = GDN_IN_DIM // N_SHARD

    def body(f_ref, o_ref):
        f = f_ref[...]
        for j in range(N_SHARD):
            o_ref[j] = f[:, j * width:(j + 1) * width]

    return _call(body, "w_in_split", (D_MODEL // rows,), [pl.BlockSpec((rows, GDN_IN_PAD), lambda i: (i, 0))],
                 pl.BlockSpec((N_SHARD, rows, width), lambda i: (0, i, 0)), _sds((N_SHARD, D_MODEL, width), full.dtype),
                 sem=("parallel",))(full)


def _place():
    x, y, c = lax.axis_index("x"), lax.axis_index("y"), lax.axis_index("c")
    return x, y, c, [(1 - x, y), (x, 1 - y), (1 - x, 1 - y)]


def _row_tile(rows):
    return max(t for t in range(8, min(rows, 640) + 1, 8) if rows % t == 0)


def _place_shard(part, me, dtype, name, layer=0):
    _, _, r, c = part.shape
    tr = _row_tile(r)

    def body(me_ref, p_ref, o_ref):
        o_ref[...] = p_ref[...].astype(dtype)

    return pl.pallas_call(
        body, name=name, out_shape=_sds((N_SHARD, 2, r, c), dtype),
        grid_spec=pltpu.PrefetchScalarGridSpec(
            num_scalar_prefetch=1, grid=(2, r // tr),
            in_specs=[pl.BlockSpec((None, None, tr, c), lambda h, i, me_ref: (layer, h, i, 0))],
            out_specs=pl.BlockSpec((None, None, tr, c), lambda h, i, me_ref: (me_ref[0], h, i, 0))))(me, part)


def _gather_sems(n):
    return [pltpu.SemaphoreType.DMA((6 * n,)), pltpu.SemaphoreType.DMA((6 * n,))]


def _gather_steps(dsts, send_sems, recv_sems):
    n = len(dsts)
    x, y, c, chips = _place()
    me = 2 * x + y
    sibling = (x, y, 1 - c)

    def ici(k, j, slot):
        px, py = chips[j]
        view = dsts[k].at[slot, c]
        return pltpu.make_async_remote_copy(
            src_ref=view, dst_ref=view, send_sem=send_sems.at[6 * k + j],
            recv_sem=recv_sems.at[6 * k + j], device_id=(px, py, c), device_id_type=MESH)

    def d2d(k, j, half):
        px, py = chips[j]
        view = dsts[k].at[2 * px + py, half]
        return pltpu.make_async_remote_copy(
            src_ref=view, dst_ref=view, send_sem=send_sems.at[6 * k + 3 + j], recv_sem=recv_sems.at[6 * k + 3 + j],
            device_id=sibling, device_id_type=MESH)

    def start():
        for k in range(n):
            for j in range(3):
                ici(k, j, me).start()

    def finish():
        fwds = []
        for k in range(n):
            for j, (px, py) in enumerate(chips):
                ici(k, j, 2 * px + py).wait_recv()
                fwds.append(d2d(k, j, c))
                fwds[-1].start()
        for k in range(n):
            for j in range(3):
                d2d(k, j, 1 - c).wait_recv()
        for k in range(n):
            for j in range(3):
                ici(k, j, me).wait_send()
        for cp in fwds:
            cp.wait_send()

    return start, finish


def _all_gather(bufs, name):
    n = len(bufs)

    def body(*refs):
        start, finish = _gather_steps(refs[n:2 * n], *refs[2 * n:])
        start()
        finish()

    return pl.pallas_call(
        body, name=name, out_shape=[_sds(a.shape, a.dtype) for a in bufs],
        in_specs=[ANY] * n, out_specs=[ANY] * n, input_output_aliases={k: k for k in range(n)},
        scratch_shapes=_gather_sems(n))(*bufs)


def _swap_halves(pieces, name):
    n = len(pieces)

    def body(*refs):
        srcs, dsts = refs[:n], refs[n:2 * n]
        send_sems, recv_sems = refs[2 * n:]
        x, y, c, _ = _place()
        copies = []
        for k in range(n):
            hr = pieces[k].shape[1] // 2
            copies.append(pltpu.make_async_remote_copy(
                src_ref=srcs[k].at[:, pl.ds((1 - c) * hr, hr), :], dst_ref=dsts[k],
                send_sem=send_sems.at[k], recv_sem=recv_sems.at[k], device_id=(x, y, 1 - c), device_id_type=MESH))
        for cp in copies:
            cp.start()
        for cp in copies:
            cp.wait()

    return pl.pallas_call(
        body, name=name, out_shape=[_sds((N_SHARD, a.shape[1] // 2, a.shape[2])) for a in pieces],
        in_specs=[ANY] * n, out_specs=[ANY] * n,
        scratch_shapes=[pltpu.SemaphoreType.DMA((n,)), pltpu.SemaphoreType.DMA((n,))])(*pieces)


def _add_half(piece, other, place, dtype, name):
    n, hr, cols = other.shape
    tr = _row_tile(hr)

    def body(pl_ref, a_ref, b_ref, o_ref):
        o_ref[...] = (a_ref[...] + b_ref[...]).astype(dtype)

    tile = pl.BlockSpec((None, tr, cols), lambda s, i, pl_ref: (s, i, 0))
    return pl.pallas_call(
        body, name=name, out_shape=_sds(other.shape, dtype),
        grid_spec=pltpu.PrefetchScalarGridSpec(
            num_scalar_prefetch=1, grid=(n, hr // tr),
            in_specs=[pl.BlockSpec((None, None, tr, cols), lambda s, i, pl_ref: (s, pl_ref[1], i, 0)), tile],
            out_specs=tile))(place, piece.reshape(n, 2, hr, cols), other)


def _scatter_sems(n):
    return [pltpu.SemaphoreType.DMA((3 * n,)), pltpu.SemaphoreType.DMA((3 * n,))]


def _scatter_steps(srcs, dsts, send_sems, recv_sems):
    n = len(srcs)
    x, y, c, chips = _place()
    me = 2 * x + y

    def ici(k, j, src_slot, dst_slot):
        px, py = chips[j]
        return pltpu.make_async_remote_copy(
            src_ref=srcs[k].at[src_slot], dst_ref=dsts[k].at[dst_slot], send_sem=send_sems.at[3 * k + j],
            recv_sem=recv_sems.at[3 * k + j], device_id=(px, py, c), device_id_type=MESH)

    def start():
        for k in range(n):
            for j, (px, py) in enumerate(chips):
                ici(k, j, 2 * px + py, me).start()

    def finish():
        for k in range(n):
            for j, (px, py) in enumerate(chips):
                ici(k, j, me, 2 * px + py).wait_recv()
        for k in range(n):
            for j, (px, py) in enumerate(chips):
                ici(k, j, 2 * px + py, me).wait_send()

    return start, finish


def _scatter_chips(parts, name):
    n = len(parts)

    def body(*refs):
        start, finish = _scatter_steps(refs[:n], refs[n:2 * n], *refs[2 * n:])
        start()
        finish()

    return pl.pallas_call(
        body, name=name, out_shape=[_sds(a.shape, a.dtype) for a in parts],
        in_specs=[ANY] * n, out_specs=[ANY] * n, scratch_shapes=_scatter_sems(n))(*parts)


def _sum_chips(landed, own, place, name):
    _, r, cols = landed.shape
    tr = _row_tile(r)

    def body(pl_ref, q_ref, p_ref, o_ref):
        me = pl_ref[0]
        f = lambda j: jnp.where(me == j, p_ref[...], q_ref[j]).astype(F32)
        o_ref[...] = ((f(0) + f(1)) + f(2)) + f(3)

    return pl.pallas_call(
        body, name=name, out_shape=_sds((2, r, cols)),
        grid_spec=pltpu.PrefetchScalarGridSpec(
            num_scalar_prefetch=1, grid=(r // tr,),
            in_specs=[pl.BlockSpec((N_SHARD, tr, cols), lambda i, pl_ref: (0, i, 0)),
                      pl.BlockSpec((None, tr, cols), lambda i, pl_ref: (pl_ref[0], i, 0))],
            out_specs=pl.BlockSpec((None, tr, cols), lambda i, pl_ref: (pl_ref[1], i, 0))))(place, landed, own)


def _join_halves(bufs):
    n = len(bufs)

    def body(*refs):
        dsts = refs[n:2 * n]
        send_sems, recv_sems = refs[2 * n:]
        x, y, c, _ = _place()
        copies = [pltpu.make_async_remote_copy(
            src_ref=dsts[k].at[c], dst_ref=dsts[k].at[c], send_sem=send_sems.at[k], recv_sem=recv_sems.at[k],
            device_id=(x, y, 1 - c), device_id_type=MESH) for k in range(n)]
        for cp in copies:
            cp.start()
        for cp in copies:
            cp.wait()

    return pl.pallas_call(
        body, name="grads_join_halves", out_shape=[_sds(a.shape) for a in bufs], in_specs=[ANY] * n, out_specs=[ANY] * n,
        input_output_aliases={k: k for k in range(n)},
        scratch_shapes=[pltpu.SemaphoreType.DMA((n,)), pltpu.SemaphoreType.DMA((n,))])(*bufs)


GATHER_F32 = ("ln_gain", "ln_bias", "pool_b", "gdn_conv")
REPLICATED = ("pool_scale", "gdn_a_log", "gdn_dt_bias", "gdn_norm_w", "ple_gate_b")
WEIGHTS = ("ln_gain", "ln_bias", "pool_w", "pool_b", "pool_scale", "gdn_w_in", "gdn_conv", "gdn_a_log", "gdn_dt_bias",
           "gdn_norm_w", "gdn_w_out", "mlp_w1", "mlp_w2", "ple_gate_w", "ple_gate_b", "ple_proj")
SMALL_GRADS = ("ple_proj", "pool_w", "ln_gain", "ln_bias", "pool_b", "gdn_conv") + REPLICATED


def _pack(parts, lanes, row_multiple):
    flat = jnp.concatenate([a.reshape(-1) for a in parts])
    rows = -(-flat.shape[0] // (2 * lanes * row_multiple)) * row_multiple
    return jnp.pad(flat, (0, 2 * rows * lanes - flat.shape[0])).reshape(2, rows, lanes)


def _unpack(flat, shapes):
    out, off = [], 0
    for shp in shapes:
        n = math.prod(shp)
        out.append(flat[..., off:off + n].reshape(flat.shape[:-1] + tuple(shp)))
        off += n
    return out


def _pad_lanes(a, offset, width=128):
    return jnp.pad(a, ((0, 0), (offset, width - offset - a.shape[1])))


def kernel(x, p, ln_gain, ln_bias, pool_w, pool_b, pool_scale, gdn_w_in, gdn_conv, gdn_a_log, gdn_dt_bias, gdn_norm_w, gdn_w_out, mlp_w1, mlp_w2, ple_gate_w, ple_gate_b, ple_proj, loss_target, m_ln_gain, m_ln_bias, m_pool_w, m_pool_b, m_pool_scale, m_gdn_w_in, m_gdn_conv, m_gdn_a_log, m_gdn_dt_bias, m_gdn_norm_w, m_gdn_w_out, m_mlp_w1, m_mlp_w2, m_ple_gate_w, m_ple_gate_b, m_ple_proj, v_ln_gain, v_ln_bias, v_pool_w, v_pool_b, v_pool_scale, v_gdn_w_in, v_gdn_conv, v_gdn_a_log, v_gdn_dt_bias, v_gdn_norm_w, v_gdn_w_out, v_mlp_w1, v_mlp_w2, v_ple_gate_w, v_ple_gate_b, v_ple_proj):
    shard = dict(ln_gain=ln_gain, ln_bias=ln_bias, pool_w=pool_w, pool_b=pool_b, pool_scale=pool_scale, gdn_w_in=gdn_w_in,
                 gdn_conv=gdn_conv, gdn_a_log=gdn_a_log, gdn_dt_bias=gdn_dt_bias, gdn_norm_w=gdn_norm_w, gdn_w_out=gdn_w_out,
                 mlp_w1=mlp_w1, mlp_w2=mlp_w2, ple_gate_w=ple_gate_w, ple_gate_b=ple_gate_b, ple_proj=ple_proj)
    mom = dict(ln_gain=m_ln_gain, ln_bias=m_ln_bias, pool_w=m_pool_w, pool_b=m_pool_b, pool_scale=m_pool_scale,
               gdn_w_in=m_gdn_w_in, gdn_conv=m_gdn_conv, gdn_a_log=m_gdn_a_log, gdn_dt_bias=m_gdn_dt_bias,
               gdn_norm_w=m_gdn_norm_w, gdn_w_out=m_gdn_w_out, mlp_w1=m_mlp_w1, mlp_w2=m_mlp_w2, ple_gate_w=m_ple_gate_w,
               ple_gate_b=m_ple_gate_b, ple_proj=m_ple_proj)
    var = dict(ln_gain=v_ln_gain, ln_bias=v_ln_bias, pool_w=v_pool_w, pool_b=v_pool_b, pool_scale=v_pool_scale,
               gdn_w_in=v_gdn_w_in, gdn_conv=v_gdn_conv, gdn_a_log=v_gdn_a_log, gdn_dt_bias=v_gdn_dt_bias,
               gdn_norm_w=v_gdn_norm_w, gdn_w_out=v_gdn_w_out, mlp_w1=v_mlp_w1, mlp_w2=v_mlp_w2, ple_gate_w=v_ple_gate_w,
               ple_gate_b=v_ple_gate_b, ple_proj=v_ple_proj)

    xi, yi, ci = lax.axis_index("x"), lax.axis_index("y"), lax.axis_index("c")
    me = (2 * xi + yi).reshape(1).astype(jnp.int32)
    place = jnp.stack([2 * xi + yi, ci]).astype(jnp.int32)
    early = [("mlp_w1", 0), ("mlp_w2", 0), ("ple_gate_w", 0), ("ple_proj", 0), ("pool_w", 0)]
    late = [("mlp_w1", 1), ("mlp_w2", 1), ("ple_gate_w", 1), ("ple_proj", 1), ("gdn_w_out", 0), ("gdn_w_in", 0)]
    halved = lambda n: shard[n].reshape(shard[n].shape[0], 2, -1, shard[n].shape[-1])
    placed = lambda ops, tag: [_place_shard(halved(n), me, BF16, f"place_{tag}_{n}", l) for n, l in ops]
    small_in = _place_shard(_pack([shard[n] for n in GATHER_F32], 128, 8)[None], me, F32, "place_small")
    got_early = _all_gather(placed(early, "early") + [small_in], "weights_all_gather_early")
    placed_late = placed(late, "late")
    st = dict(zip(GATHER_F32, _unpack(got_early[-1].reshape(N_SHARD, -1), [shard[n].shape for n in GATHER_F32])))

    cat_last = lambda a: jnp.moveaxis(a, 0, -2).reshape(a.shape[1:-1] + (N_SHARD * a.shape[-1],))
    gain = cat_last(st["ln_gain"])
    bias = cat_last(st["ln_bias"])
    wp = got_early[4].reshape(N_SHARD, 4, POOL_GROUP // N_SHARD, POOL_GROUP)
    pb = cat_last(st["pool_b"]).reshape(1, D_MODEL)
    ps = pool_scale
    conv_w = cat_last(st["gdn_conv"])[0]
    merged = lambda g: g.reshape(N_SHARD, -1, g.shape[-1])
    mlp_w = lambda i, got: (merged(got[0]), merged(got[1]), merged(got[2]), ple_gate_b[i:i + 1], merged(got[3]))
    alog_l = _pad_lanes(gdn_a_log, HEADS)
    dtb_l = _pad_lanes(gdn_dt_bias, HEADS)
    nw = jnp.tile(gdn_norm_w, (1, HEADS))
    ln = lambda i, k: (gain[i, k][None], bias[i, k][None])

    x0 = x[0]
    p0, p1 = p[0, 0], p[1, 0]

    x1, xh1, rs1 = _pool_fwd(x0, wp, pb, ps, *ln(0, 0))
    (x2, xh2, rs2, a0, xb0), got_late = _mlp_fwd(x1, p0, *mlp_w(0, got_early), *ln(0, 1), "mlp_fwd_0", gather=placed_late)
    w_out = got_late[4].reshape(D_MODEL, D_MODEL)
    w_in = _assemble_w_in(merged(got_late[5]))
    proj, y_conv, q, k, v, gcb = _conv_fwd(x2, w_in, conv_w, alog_l, dtb_l)
    u, w, qd, kd, qk, gl, t_inv = _gdn_prep(q, k, v, gcb)
    o, states = _gdn_scan(u, w, qd, kd, qk, gl)
    x3, xh3, rs3 = _gdn_out_fwd(o, proj, x2, w_out, nw, *ln(1, 0))
    (dy4, xh4, rs4, a1, xb1, loss_l), _ = _mlp_fwd(x3, p1, *mlp_w(1, got_late), *ln(1, 1), "mlp_fwd_1", target=loss_target[0])

    g_gain = [[None, None], [None, None]]
    g_bias = [[None, None], [None, None]]

    def mlp_grads(i, dy, xh, rs, x_mid, xb, a, p_i, got, scatter=()):
        (dx, dh, dzg, dpp, drb, dg, db, dgb), landed = _mlp_bwd(
            dy, xh, rs, x_mid, a, p_i, *mlp_w(i, got), ln(i, 1)[0], f"mlp_bwd_{i}", scatter=scatter)
        g_gain[i][1], g_bias[i][1] = dg, db
        return dx, dict(
            mlp_w1=_wgrad(xb, dh, f"dw1_{i}", stack_cols=True), mlp_w2=_wgrad(a, drb, f"dw2_{i}").reshape(N_SHARD, -1, D_MODEL),
            ple_gate_w=_wgrad(xb, dzg, f"dgate_w_{i}").reshape(N_SHARD, -1, D_MODEL),
            ple_proj=_wgrad(p_i, dpp, f"dproj_{i}", stack_cols=True), ple_gate_b=dgb), landed

    def chip_sums(pieces, wire, tag):
        others = _swap_halves(pieces, f"grads_swap_halves_{tag}")
        return [_add_half(a, b, place, t, f"grads_add_half_{tag}{i}") for i, (a, b, t) in enumerate(zip(pieces, others, wire))]

    dx3, gl1, _ = mlp_grads(1, dy4, xh4, rs4, x3, xb1, a1, p1, got_late)
    dres, do, dz, on_b, drb3, g_gain[1][0], g_bias[1][0], d_nw = _gdn_out_bwd(dx3, xh3, rs3, o, proj, w_out, nw, ln(1, 0)[0])
    d_wout = _wgrad(on_b, drb3, "dw_out").reshape(N_SHARD, -1, D_MODEL)
    du, dw, dqd, dkd, dqk, dgl = _gdn_scan_bwd(do, u, w, qd, kd, qk, gl, states)
    dq, dk, dv, dgcb = _gdn_prep_bwd(q, k, v, gcb, t_inv, u, w, du, dw, dqd, dkd, dqk, dgl)
    dproj, dx2, d_conv, d_alog_l, d_dtb_l = _conv_bwd(proj, y_conv, dq, dk, dv, dgcb, dz, conv_w, alog_l, dtb_l, w_in, dres)
    d_win = _split_w_in(_wgrad(x2, dproj, "dw_in"))
    sums_late = chip_sums([gl1["mlp_w1"], gl1["mlp_w2"], gl1["ple_gate_w"], d_wout, d_win], [BF16] * 5, "late")
    dx1, gl0, landed_late = mlp_grads(0, dx2, xh2, rs2, x1, xb0, a0, p0, got_early, scatter=sums_late)
    sums_early = chip_sums([gl0["mlp_w1"], gl0["mlp_w2"], gl0["ple_gate_w"]], [BF16] * 3, "early")
    (dx0, g_gain[0][0], g_bias[0][0], d_ps, d_pb, d_wp), landed_early = _pool_bwd(
        dx1, xh1, rs1, x0, wp, pb, ps, ln(0, 0)[0], scatter=sums_early)

    split_last = lambda a: jnp.moveaxis(a.reshape(a.shape[:-1] + (N_SHARD, a.shape[-1] // N_SHARD)), -2, 0)
    small_st = dict(
        ple_proj=jnp.stack([gl0["ple_proj"], gl1["ple_proj"]], axis=1),
        pool_w=jnp.moveaxis(d_wp.reshape(4, N_SHARD, POOL_GROUP // N_SHARD, POOL_GROUP), 1, 0)[:, None],
        ln_gain=split_last(jnp.stack([jnp.concatenate(r, axis=0) for r in g_gain])),
        ln_bias=split_last(jnp.stack([jnp.concatenate(r, axis=0) for r in g_bias])),
        pool_b=split_last(d_pb.reshape(1, 4, POOL_GROUP)),
        gdn_conv=split_last(d_conv)[:, None],
    )
    rep = dict(pool_scale=d_ps, gdn_a_log=d_alog_l[:, HEADS:2 * HEADS], gdn_dt_bias=d_dtb_l[:, HEADS:2 * HEADS],
               gdn_norm_w=d_nw, ple_gate_b=jnp.concatenate([gl0["ple_gate_b"], gl1["ple_gate_b"]], axis=0))
    for n in REPLICATED:
        small_st[n] = jnp.broadcast_to(rep[n][None], (N_SHARD,) + rep[n].shape)
    small_flat = jnp.concatenate([small_st[n].reshape(N_SHARD, -1) for n in SMALL_GRADS], axis=1)
    small_rows = -(-small_flat.shape[1] // (16 * LANES)) * 16
    small_piece = jnp.pad(small_flat, ((0, 0), (0, small_rows * LANES - small_flat.shape[1]))).reshape(N_SHARD, small_rows, LANES)

    sums_small = chip_sums([small_piece], [F32], "small")
    landed_small = _scatter_chips(sums_small, "grads_scatter_chips_small")
    red = _join_halves([_sum_chips(q_, p_, place, f"grads_sum_chips_{i}") for i, (q_, p_) in
                        enumerate(zip(list(landed_early) + list(landed_small) + list(landed_late),
                                      sums_early + sums_small + sums_late))])
    red = [r.reshape(-1, r.shape[-1]) for r in red]
    grads = dict(mlp_w1=[red[0], red[4]], mlp_w2=[red[1], red[5]], ple_gate_w=[red[2], red[6]], gdn_w_out=red[7], gdn_w_in=red[8])
    grads.update(zip(SMALL_GRADS, _unpack(red[3].reshape(-1), [shard[n].shape for n in SMALL_GRADS])))

    delta, new_m, new_v = {}, {}, {}
    small = [n for n in WEIGHTS if shard[n].size < 128 * 128]
    for n in WEIGHTS:
        if n in small:
            continue
        to2d = lambda a, n=n: a.reshape(-1, shard[n].shape[-1])
        g_n = grads[n] if isinstance(grads[n], list) else to2d(grads[n])
        g2, d2, m2, v2 = _adamw(to2d(shard[n]), g_n, to2d(mom[n]), to2d(var[n]), "adamw_" + n)
        grads[n], delta[n], new_m[n], new_v[n] = (t.reshape(shard[n].shape) for t in (g2, d2, m2, v2))
    pk = lambda d: _pack([d[n] for n in small], 128, 8).reshape(-1, 128)
    _, d2, m2, v2 = _adamw(pk(shard), pk(grads), pk(mom), pk(var), "adamw_small")
    for dst, t in ((delta, d2), (new_m, m2), (new_v, v2)):
        dst.update(zip(small, _unpack(t.reshape(-1), [shard[n].shape for n in small])))

    loss = lax.psum(loss_l[0, 0], ("x", "y", "c"))
    return (loss, dx0[None], *[grads[n] for n in WEIGHTS], *[delta[n] for n in WEIGHTS],
            *[new_m[n] for n in WEIGHTS], *[new_v[n] for n in WEIGHTS])
```

```python
import math

import jax
import jax.numpy as jnp
from jax import lax
from jax.experimental import pallas as pl
from jax.experimental.pallas import tpu as pltpu

F32 = jnp.float32
BF16 = jnp.bfloat16

D_MODEL = 1024
D_FF = 4096
PLE_DIM = 256
N_SHARD = 4
POOL_WINDOWS = (2, 4, 8, 16)
POOL_GROUP = 256
POOL_HALO = 16
HEADS = 8
HEAD_DIM = 128
CHUNK = 64
CONV_WIDTH = 4
CONV_HALO = 8
QKV_DIM = 3 * D_MODEL
GDN_IN_DIM = QKV_DIM + D_MODEL + 2 * HEADS
GDN_IN_PAD = 4224
BA_BLOCK = (QKV_DIM + D_MODEL) // 128
ALPHA = (2.0 * 2) ** 0.25
LN_EPS = 1e-5
RMS_EPS = 1e-6
L2_EPS = 1e-6
ADAM_LR, ADAM_B1, ADAM_B2, ADAM_EPS, ADAM_WD, ADAM_STEP = 0.001, 0.9, 0.999, 1e-08, 0.01, 10

ROW_TILE = 512
CONV_TILE = 256
PREP_CHUNKS = 32
PREP_BWD_CHUNKS = 16
SCAN_CHUNKS = 8
LANES = 1024
ADAM_ROWS = 256

NN = (((1,), (0,)), ((), ()))
NT = (((1,), (1,)), ((), ()))
TN = (((0,), (0,)), ((), ()))
BNN = (((2,), (1,)), ((0,), (0,)))
BNT = (((2,), (2,)), ((0,), (0,)))
BTN = (((1,), (1,)), ((0,), (0,)))
MESH = pl.DeviceIdType.MESH
ANY = pl.BlockSpec(memory_space=pl.ANY)


def _bdot(a, b, dims):
    return lax.dot_general(a.astype(BF16), b.astype(BF16), dims, preferred_element_type=F32)


def _hdot(a, b, dims):
    return lax.dot_general(a, b, dims, precision=lax.Precision.HIGHEST, preferred_element_type=F32)


def _mdot(a, b, dims):
    return lax.dot_general(a, b, dims, precision=lax.Precision.HIGH, preferred_element_type=F32)


def _sigmoid(x):
    return 0.5 * jnp.tanh(0.5 * x) + 0.5


def _silu(x):
    return x * _sigmoid(x)


def _softplus(x):
    return jnp.maximum(x, 0.0) + jnp.log1p(jnp.exp(-jnp.abs(x)))


def _call(body, name, grid, in_specs, out_specs, out_shape, scratch=(), sem=None, aliases=None):
    params = pltpu.CompilerParams(dimension_semantics=sem) if sem else None
    return pl.pallas_call(
        body, name=name, grid=grid, in_specs=in_specs, out_specs=out_specs, out_shape=out_shape,
        scratch_shapes=list(scratch), compiler_params=params, input_output_aliases=aliases or {})


def _row(d):
    return pl.BlockSpec((1, d), lambda *_: (0, 0))


def _full(shape):
    n = len(shape)
    return pl.BlockSpec(shape, lambda *_: (0,) * n)


def _sds(shape, dtype=F32):
    return jax.ShapeDtypeStruct(shape, dtype)


def _ln_fwd(r, gain, bias):
    mu = jnp.mean(r, axis=-1, keepdims=True)
    xc = r - mu
    rstd = lax.rsqrt(jnp.mean(xc * xc, axis=-1, keepdims=True) + LN_EPS)
    xhat = xc * rstd
    return xhat * gain + bias, xhat, rstd


def _ln_bwd(dy, xhat, rstd, gain):
    dxh = dy * gain
    m1 = jnp.mean(dxh, axis=-1, keepdims=True)
    m2 = jnp.mean(dxh * xhat, axis=-1, keepdims=True)
    return rstd * (dxh - m1 - xhat * m2)


def _acc(ref, first, val):
    @pl.when(first)
    def _():
        ref[...] = val

    @pl.when(jnp.logical_not(first))
    def _():
        ref[...] += val


def _pooled_groups(xe, t0, ts):
    pos = (t0 + lax.broadcasted_iota(jnp.int32, (ts, 1), 0) + 1).astype(F32)
    outs = []
    for gi, win in enumerate(POOL_WINDOWS):
        xs = xe[:, gi * POOL_GROUP:(gi + 1) * POOL_GROUP]
        s, k = xs, 1
        while k < win:
            s = s + pltpu.roll(s, k, 0)
            k *= 2
        mean = s[POOL_HALO:] / jnp.minimum(pos, float(win))
        outs.append(mean - xs[POOL_HALO:])
    return outs


def _pool_groups_w(w_ref):
    return [jnp.concatenate([w_ref[s, g] for s in range(N_SHARD)], axis=0) for g in range(4)]


def _pool_fwd(x, wp, pb, ps, gain, bias):
    s_len = x.shape[0]
    ts = min(ROW_TILE, s_len)
    hb = ts // POOL_HALO

    def body(x_ref, halo_ref, w_ref, pb_ref, ps_ref, g_ref, b_ref, y_ref, xhat_ref, rstd_ref):
        i = pl.program_id(0)
        x_t = x_ref[...]
        halo = jnp.where(i > 0, halo_ref[...], 0.0)
        pooled = _pooled_groups(jnp.concatenate([halo, x_t], axis=0), i * ts, ts)
        wg = _pool_groups_w(w_ref)
        y = jnp.concatenate([_bdot(pooled[g], wg[g], NN) for g in range(4)], axis=1) + pb_ref[...]
        r = ALPHA * x_t + y * ps_ref[...]
        y_ref[...], xhat_ref[...], rstd_ref[...] = _ln_fwd(r, g_ref[...], b_ref[...])

    tile = pl.BlockSpec((ts, D_MODEL), lambda i: (i, 0))
    return _call(
        body, "pool_fwd", (s_len // ts,),
        [tile, pl.BlockSpec((POOL_HALO, D_MODEL), lambda i: (jnp.maximum(i * hb - 1, 0), 0)),
         _full(wp.shape), _row(D_MODEL), _row(D_MODEL), _row(D_MODEL), _row(D_MODEL)],
        [tile, tile, pl.BlockSpec((ts, 1), lambda i: (i, 0))],
        [_sds((s_len, D_MODEL)), _sds((s_len, D_MODEL)), _sds((s_len, 1))],
        sem=("parallel",))(x, x, wp, pb, ps, gain, bias)


def _pool_bwd(dy, xhat, rstd, x, wp, pb, ps, gain, scatter=()):
    s_len = x.shape[0]
    ts = min(ROW_TILE, s_len)
    hb = ts // POOL_HALO
    n_t = s_len // ts
    ne = ts + POOL_HALO
    n_s = len(scatter)

    def body(*refs):
        dy_ref, dyn_ref, xh_ref, xhn_ref, rs_ref, rsn_ref, x_ref, xp_ref, w_ref, pb_ref, ps_ref, g_ref = refs[:12]
        dx_ref, dg_ref, db_ref, dps_ref, dpb_ref, dw_ref = refs[12 + n_s:18 + n_s]
        i = pl.program_id(0)
        if n_s:
            start, finish = _scatter_steps(refs[12:12 + n_s], refs[18 + n_s:18 + 2 * n_s], *refs[18 + 2 * n_s:])
            pl.when(i == 0)(start)
        more = i < n_t - 1
        dy_t, xh_t = dy_ref[...], xh_ref[...]
        dy_e = jnp.concatenate([dy_t, jnp.where(more, dyn_ref[...], 0.0)], axis=0)
        xh_e = jnp.concatenate([xh_t, xhn_ref[...]], axis=0)
        rs_e = jnp.concatenate([rs_ref[...], rsn_ref[...]], axis=0)
        dr_e = _ln_bwd(dy_e, xh_e, rs_e, g_ref[...])
        dyy_e = dr_e * ps_ref[...]
        pos_e = (i * ts + lax.broadcasted_iota(jnp.int32, (ne, 1), 0) + 1).astype(F32)
        dxs = []
        wg = _pool_groups_w(w_ref)
        for gi, win in enumerate(POOL_WINDOWS):
            sl = slice(gi * POOL_GROUP, (gi + 1) * POOL_GROUP)
            dpool = _bdot(dyy_e[:, sl], wg[gi], NT)
            s, k = dpool / jnp.minimum(pos_e, float(win)), 1
            while k < win:
                s = s + pltpu.roll(s, ne - k, 0)
                k *= 2
            dxs.append(s[:ts] - dpool[:ts])
        dx_ref[...] = ALPHA * dr_e[:ts] + jnp.concatenate(dxs, axis=1)

        x_t = x_ref[...]
        halo = jnp.where(i > 0, xp_ref[...], 0.0)
        pooled = _pooled_groups(jnp.concatenate([halo, x_t], axis=0), i * ts, ts)
        y = jnp.concatenate([_bdot(pooled[g], wg[g], NN) for g in range(4)], axis=1) + pb_ref[...]
        dr_t, dyy_t = dr_e[:ts], dyy_e[:ts]
        first = i == 0
        _acc(dg_ref, first, jnp.sum(dy_t * xh_t, axis=0, keepdims=True))
        _acc(db_ref, first, jnp.sum(dy_t, axis=0, keepdims=True))
        _acc(dps_ref, first, jnp.sum(dr_t * y, axis=0, keepdims=True))
        _acc(dpb_ref, first, jnp.sum(dyy_t, axis=0, keepdims=True))
        for g in range(4):
            _acc(dw_ref.at[g], first, _bdot(pooled[g], dyy_t[:, g * POOL_GROUP:(g + 1) * POOL_GROUP], TN))
        if n_s:
            pl.when(i == n_t - 1)(finish)

    tile = pl.BlockSpec((ts, D_MODEL), lambda i: (i, 0))
    nxt = pl.BlockSpec((POOL_HALO, D_MODEL), lambda i: (jnp.minimum((i + 1) * hb, n_t * hb - 1), 0))
    prv = pl.BlockSpec((POOL_HALO, D_MODEL), lambda i: (jnp.maximum(i * hb - 1, 0), 0))
    rs_t = pl.BlockSpec((ts, 1), lambda i: (i, 0))
    rs_n = pl.BlockSpec((POOL_HALO, 1), lambda i: (jnp.minimum((i + 1) * hb, n_t * hb - 1), 0))
    row = _row(D_MODEL)
    out = _call(
        body, "pool_bwd", (n_t,),
        [tile, nxt, tile, nxt, rs_t, rs_n, tile, prv, _full(wp.shape), row, row, row] + [ANY] * n_s,
        [tile, row, row, row, row, _full((4, POOL_GROUP, POOL_GROUP))] + [ANY] * n_s,
        [_sds((s_len, D_MODEL))] + [_sds((1, D_MODEL))] * 4 + [_sds((4, POOL_GROUP, POOL_GROUP))]
        + [_sds(t.shape, t.dtype) for t in scatter],
        scratch=_scatter_sems(n_s) if n_s else [], sem=("arbitrary",),
    )(dy, dy, xhat, xhat, rstd, rstd, x, x, wp, pb, ps, gain, *scatter)
    return out[:6], out[6:]


def _mlp_weight_specs():
    fc = D_FF // N_SHARD
    return [pl.BlockSpec((None, D_MODEL, fc), lambda i, j: (j, 0, 0)),
            pl.BlockSpec((None, fc, D_MODEL), lambda i, j: (j, 0, 0)),
            _full((N_SHARD, D_MODEL // N_SHARD, D_MODEL)),
            _row(D_MODEL),
            _full((N_SHARD, PLE_DIM, D_MODEL // N_SHARD))]


def _gate_w(gw_ref):
    return gw_ref[...].reshape(D_MODEL, D_MODEL)


def _ple_proj(pj_ref):
    return jnp.concatenate([pj_ref[s] for s in range(N_SHARD)], axis=1)


def _mlp_fwd(x, p, w1s, w2s, gw, gb, proj, gain, bias, name, gather=(), target=None):
    s_len = x.shape[0]
    ts = min(ROW_TILE, s_len)
    n_i = s_len // ts
    n_g = len(gather)
    has_t = target is not None
    n_in, n_out = 9 + has_t, 5 + has_t

    def body(*refs):
        x_ref, p_ref, w1_ref, w2_ref, gw_ref, gb_ref, pj_ref, g_ref, b_ref = refs[:9]
        y_ref, xhat_ref, rstd_ref, a_ref, xbo_ref = refs[n_in + n_g:n_in + n_g + 5]
        acc_ref, xb_ref = refs[n_in + n_out + 2 * n_g:n_in + n_out + 2 * n_g + 2]
        i, j = pl.program_id(0), pl.program_id(1)
        if n_g:
            start, finish = _gather_steps(refs[n_in + n_out + n_g:n_in + n_out + 2 * n_g], *refs[n_in + n_out + 2 * n_g + 2:])
            pl.when((i == 0) & (j == 0))(start)

        @pl.when(j == 0)
        def _():
            x_t = x_ref[...]
            xb_ref[...] = x_t.astype(BF16)
            xbo_ref[...] = x_t.astype(BF16)
            gate = _sigmoid(_bdot(x_t, _gate_w(gw_ref), NN) + gb_ref[...])
            acc_ref[...] = ALPHA * x_t + gate * _bdot(p_ref[...], _ple_proj(pj_ref), NN)

        h = jnp.maximum(_bdot(xb_ref[...], w1_ref[...], NN), 0.0)
        a = (h * h).astype(BF16)
        a_ref[...] = a
        acc_ref[...] += _bdot(a, w2_ref[...], NN)

        @pl.when(j == N_SHARD - 1)
        def _():
            y, xhat_ref[...], rstd_ref[...] = _ln_fwd(acc_ref[...], g_ref[...], b_ref[...])
            if has_t:
                err = y - refs[9][...]
                y_ref[...] = err * (1.0 / D_MODEL)
                part = 0.5 * jnp.sum(jnp.mean(err * err, axis=-1, keepdims=True))
                _acc(refs[n_in + n_g + 5], i == 0, part + jnp.zeros((1, 128), F32))
            else:
                y_ref[...] = y

        if n_g:
            pl.when((i == n_i - 1) & (j == N_SHARD - 1))(finish)

    tile = pl.BlockSpec((ts, D_MODEL), lambda i, j: (i, 0))
    row = _row(D_MODEL)
    out = _call(
        body, name, (n_i, N_SHARD),
        [tile, pl.BlockSpec((ts, PLE_DIM), lambda i, j: (i, 0))] + _mlp_weight_specs() + [row, row] + [tile] * has_t
        + [ANY] * n_g,
        [tile, tile, pl.BlockSpec((ts, 1), lambda i, j: (i, 0)), pl.BlockSpec((ts, D_FF // N_SHARD), lambda i, j: (i, j)), tile]
        + [_row(128)] * has_t + [ANY] * n_g,
        [_sds((s_len, D_MODEL)), _sds((s_len, D_MODEL)), _sds((s_len, 1)), _sds((s_len, D_FF), BF16), _sds((s_len, D_MODEL), BF16)]
        + [_sds((1, 128))] * has_t + [_sds(a.shape, a.dtype) for a in gather],
        scratch=[pltpu.VMEM((ts, D_MODEL), F32), pltpu.VMEM((ts, D_MODEL), BF16)] + (_gather_sems(n_g) if n_g else []),
        sem=("arbitrary", "arbitrary"), aliases={n_in + k: n_out + k for k in range(n_g)},
    )(x, p, w1s, w2s, gw, gb, proj, gain, bias, *([target] if has_t else []), *gather)
    return out[:n_out], out[n_out:]


def _mlp_bwd(dy, xhat, rstd, x, a, p, w1s, w2s, gw, gb, proj, gain, name, scatter=()):
    s_len = x.shape[0]
    ts = min(ROW_TILE, s_len)
    fc = D_FF // N_SHARD
    n_i = s_len // ts
    n_s = len(scatter)

    def body(*refs):
        dy_ref, xh_ref, rs_ref, x_ref, a_ref, p_ref, w1_ref, w2_ref, gw_ref, gb_ref, pj_ref, g_ref = refs[:12]
        dx_ref, dh_ref, dzg_ref, dpp_ref, drb_ref, dg_ref, db_ref, dgb_ref = refs[12 + n_s:20 + n_s]
        acc_ref, dr_ref = refs[20 + 2 * n_s:22 + 2 * n_s]
        i, j = pl.program_id(0), pl.program_id(1)
        if n_s:
            start, finish = _scatter_steps(refs[12:12 + n_s], refs[20 + n_s:20 + 2 * n_s], *refs[22 + 2 * n_s:])
            pl.when((i == 0) & (j == 0))(start)

        @pl.when(j == 0)
        def _():
            dy_t, xh_t, x_t = dy_ref[...], xh_ref[...], x_ref[...]
            dr = _ln_bwd(dy_t, xh_t, rs_ref[...], g_ref[...])
            drb = dr.astype(BF16)
            dr_ref[...] = drb
            drb_ref[...] = drb
            gw_full = _gate_w(gw_ref)
            gate = _sigmoid(_bdot(x_t, gw_full, NN) + gb_ref[...])
            pp = _bdot(p_ref[...], _ple_proj(pj_ref), NN)
            dzg = dr * pp * gate * (1.0 - gate)
            dzg_ref[...] = dzg.astype(BF16)
            dpp_ref[...] = (dr * gate).astype(BF16)
            acc_ref[...] = ALPHA * dr + _bdot(dzg, gw_full, NT)
            first = i == 0
            _acc(dg_ref, first, jnp.sum(dy_t * xh_t, axis=0, keepdims=True))
            _acc(db_ref, first, jnp.sum(dy_t, axis=0, keepdims=True))
            _acc(dgb_ref, first, jnp.sum(dzg, axis=0, keepdims=True))

        dh = (_bdot(dr_ref[...], w2_ref[...], NT) * (2.0 * jnp.sqrt(a_ref[...].astype(F32)))).astype(BF16)
        dh_ref[...] = dh
        acc_ref[...] += _bdot(dh, w1_ref[...], NT)

        @pl.when(j == N_SHARD - 1)
        def _():
            dx_ref[...] = acc_ref[...]

        if n_s:
            pl.when((i == n_i - 1) & (j == N_SHARD - 1))(finish)

    tile = pl.BlockSpec((ts, D_MODEL), lambda i, j: (i, 0))
    ftile = pl.BlockSpec((ts, fc), lambda i, j: (i, j))
    row = _row(D_MODEL)
    out = _call(
        body, name, (n_i, N_SHARD),
        [tile, tile, pl.BlockSpec((ts, 1), lambda i, j: (i, 0)), tile, ftile, pl.BlockSpec((ts, PLE_DIM), lambda i, j: (i, 0))]
        + _mlp_weight_specs() + [row] + [ANY] * n_s,
        [tile, ftile, tile, tile, tile, row, row, row] + [ANY] * n_s,
        [_sds((s_len, D_MODEL)), _sds((s_len, D_FF), BF16)]
        + [_sds((s_len, D_MODEL), BF16)] * 3 + [_sds((1, D_MODEL))] * 3 + [_sds(t.shape, t.dtype) for t in scatter],
        scratch=[pltpu.VMEM((ts, D_MODEL), F32), pltpu.VMEM((ts, D_MODEL), BF16)] + (_scatter_sems(n_s) if n_s else []),
        sem=("arbitrary", "arbitrary"))(dy, xhat, rstd, x, a, p, w1s, w2s, gw, gb, proj, gain, *scatter)
    return out[:8], out[8:]


def _wgrad(a, b, name, stack_cols=False):
    s_len, m = a.shape
    n = b.shape[1]
    ts = min(2048 if a.dtype == BF16 and b.dtype == BF16 else 1024, s_len)
    tm = min(m, 1024)
    tn = n // N_SHARD if stack_cols else (1408 if n == GDN_IN_PAD else min(n, 1024))
    n_s = s_len // ts

    def body(a_ref, b_ref, o_ref):
        _acc(o_ref, pl.program_id(2) == 0, _bdot(a_ref[...], b_ref[...], TN))

    if stack_cols:
        out_spec = pl.BlockSpec((None, tm, tn), lambda mi, nj, s: (nj, mi, 0))
        out_shape = _sds((N_SHARD, m, tn))
    else:
        out_spec = pl.BlockSpec((tm, tn), lambda mi, nj, s: (mi, nj))
        out_shape = _sds((m, n))
    return _call(
        body, name, (m // tm, n // tn, n_s),
        [pl.BlockSpec((ts, tm), lambda mi, nj, s: (s, mi)), pl.BlockSpec((ts, tn), lambda mi, nj, s: (s, nj))],
        out_spec, out_shape, sem=("parallel", "parallel", "arbitrary"))(a, b)


def _act_qkv(y):
    qkv = _silu(y)
    qs, ks = [], []
    for h in range(HEADS):
        qh = qkv[:, h * HEAD_DIM:(h + 1) * HEAD_DIM]
        kh = qkv[:, D_MODEL + h * HEAD_DIM:D_MODEL + (h + 1) * HEAD_DIM]
        qs.append(qh * (lax.rsqrt(jnp.sum(qh * qh, axis=-1, keepdims=True) + L2_EPS) * HEAD_DIM ** -0.5))
        ks.append(kh * lax.rsqrt(jnp.sum(kh * kh, axis=-1, keepdims=True) + L2_EPS))
    return jnp.concatenate(qs, axis=1), jnp.concatenate(ks, axis=1), qkv[:, 2 * D_MODEL:]


def _act_gb(ba, alog_l, dtb_l, tril):
    lane = lax.broadcasted_iota(jnp.int32, ba.shape, 1)
    g = jnp.where((lane >= HEADS) & (lane < 2 * HEADS), -jnp.exp(alog_l) * _softplus(ba + dtb_l), 0.0)
    return jnp.where(lane < HEADS, _sigmoid(ba), _hdot(tril, g, NN))


def _chunk_tril(t):
    ii = lax.broadcasted_iota(jnp.int32, (t, t), 0)
    jj = lax.broadcasted_iota(jnp.int32, (t, t), 1)
    return ((ii // CHUNK == jj // CHUNK) & (ii >= jj)).astype(F32)


def _conv_rows(xe, w, n_rows):
    y = xe[CONV_HALO:CONV_HALO + n_rows] * w[CONV_WIDTH - 1]
    for j in range(CONV_WIDTH - 1):
        y = y + pltpu.roll(xe, CONV_WIDTH - 1 - j, 0)[CONV_HALO:CONV_HALO + n_rows] * w[j]
    return y


def _conv_fwd(x, w_in, conv_w, alog_l, dtb_l):
    s_len = x.shape[0]
    ts = min(CONV_TILE, s_len)
    hb = ts // CONV_HALO

    def body(x_ref, xp_ref, w_ref, al_ref, dt_ref, win_hbm, proj_ref, y_ref, q_ref, k_ref, v_ref, gcb_ref, win_ref, win_sem):
        i = pl.program_id(0)

        @pl.when(i == 0)
        def _():
            cp = pltpu.make_async_copy(win_hbm, win_ref, win_sem)
            cp.start()
            cp.wait()

        proj = _bdot(x_ref[...], win_ref[...], NN)
        proj_ref[...] = proj
        halo = jnp.where(i > 0, _bdot(xp_ref[...], win_ref[:, :QKV_DIM], NN), 0.0)
        taps = [w_ref[pl.ds(j, 1), :] for j in range(CONV_WIDTH)]
        y = _conv_rows(jnp.concatenate([halo, proj[:, :QKV_DIM]], axis=0), taps, ts)
        y_ref[...] = y
        q_ref[...], k_ref[...], v_ref[...] = _act_qkv(y)
        gcb_ref[...] = _act_gb(proj[:, BA_BLOCK * 128:], al_ref[...], dt_ref[...], _chunk_tril(ts))

    tile = pl.BlockSpec((ts, D_MODEL), lambda i: (i, 0))
    return _call(
        body, "gdn_conv_fwd", (s_len // ts,),
        [tile, pl.BlockSpec((CONV_HALO, D_MODEL), lambda i: (jnp.maximum(i * hb - 1, 0), 0)),
         _full((CONV_WIDTH, QKV_DIM)), _row(128), _row(128), ANY],
        [pl.BlockSpec((ts, GDN_IN_PAD), lambda i: (i, 0)), pl.BlockSpec((ts, QKV_DIM), lambda i: (i, 0)), tile, tile, tile,
         pl.BlockSpec((ts, 128), lambda i: (i, 0))],
        [_sds((s_len, GDN_IN_PAD)), _sds((s_len, QKV_DIM))] + [_sds((s_len, D_MODEL))] * 3 + [_sds((s_len, 128))],
        scratch=[pltpu.VMEM(w_in.shape, w_in.dtype), pltpu.SemaphoreType.DMA],
        sem=("arbitrary",))(x, x, conv_w, alog_l, dtb_l, w_in)


def _conv_bwd(proj, y, dq, dk, dv, dgcb, dz, conv_w, alog_l, dtb_l, w_in, dres):
    s_len = proj.shape[0]
    ts = min(CONV_TILE, s_len)
    hb = ts // CONV_HALO
    n_t = s_len // ts
    te = ts + CONV_HALO

    def body(x_ref, y_ref, yn_ref, ba_ref, dq_ref, dqn_ref, dk_ref, dkn_ref, dv_ref, dvn_ref, dgcb_ref, dz_ref,
             w_ref, al_ref, dt_ref, win_hbm, dres_ref, dp_ref, dx_ref, dw_ref, dal_ref, ddt_ref, win_ref, win_sem):
        i = pl.program_id(0)

        @pl.when(i == 0)
        def _():
            cp = pltpu.make_async_copy(win_hbm, win_ref, win_sem)
            cp.start()
            cp.wait()

        more = i < n_t - 1
        w = [w_ref[pl.ds(j, 1), :] for j in range(CONV_WIDTH)]
        x_t = x_ref[...]
        _, act_vjp = jax.vjp(_act_qkv, jnp.concatenate([y_ref[...], yn_ref[...]], axis=0))
        ct = tuple(jnp.concatenate([t[...], jnp.where(more, n[...], 0.0)], axis=0)
                   for t, n in ((dq_ref, dqn_ref), (dk_ref, dkn_ref), (dv_ref, dvn_ref)))
        (dy_e,) = act_vjp(ct)
        ahead = [pltpu.roll(dy_e, te - (CONV_WIDTH - 1 - j), 0)[:ts] for j in range(CONV_WIDTH - 1)] + [dy_e[:ts]]
        dx = ahead[CONV_WIDTH - 1] * w[CONV_WIDTH - 1]
        for j in range(CONV_WIDTH - 1):
            dx = dx + ahead[j] * w[j]
        dws = [jnp.sum(ahead[j] * x_t, axis=0, keepdims=True) for j in range(CONV_WIDTH)]
        _, gb_vjp = jax.vjp(lambda ba, al, dt: _act_gb(ba, al, dt, _chunk_tril(ts)), ba_ref[...], al_ref[...], dt_ref[...])
        dba, dal, ddt = gb_vjp(dgcb_ref[...])
        dp = jnp.concatenate([dx.astype(BF16), dz_ref[...], dba.astype(BF16)], axis=1)
        dp_ref[...] = dp
        dx_ref[...] = dres_ref[...] + _bdot(dp, win_ref[...], NT)
        first = i == 0
        for j in range(CONV_WIDTH):
            _acc(dw_ref.at[pl.ds(j, 1), :], first, dws[j])
        _acc(dal_ref, first, dal)
        _acc(ddt_ref, first, ddt)

    tile = pl.BlockSpec((ts, D_MODEL), lambda i: (i, 0))
    nxt = pl.BlockSpec((CONV_HALO, D_MODEL), lambda i: (jnp.minimum((i + 1) * hb, n_t * hb - 1), 0))
    return _call(
        body, "gdn_conv_bwd", (n_t,),
        [pl.BlockSpec((ts, QKV_DIM), lambda i: (i, 0)), pl.BlockSpec((ts, QKV_DIM), lambda i: (i, 0)),
         pl.BlockSpec((CONV_HALO, QKV_DIM), lambda i: (jnp.minimum((i + 1) * hb, n_t * hb - 1), 0)),
         pl.BlockSpec((ts, 128), lambda i: (i, BA_BLOCK)),
         tile, nxt, tile, nxt, tile, nxt, pl.BlockSpec((ts, 128), lambda i: (i, 0)), tile,
         _full((CONV_WIDTH, QKV_DIM)), _row(128), _row(128), ANY, tile],
        [pl.BlockSpec((ts, GDN_IN_PAD), lambda i: (i, 0)), tile, _full((CONV_WIDTH, QKV_DIM)), _row(128), _row(128)],
        [_sds((s_len, GDN_IN_PAD), BF16), _sds((s_len, D_MODEL)), _sds((CONV_WIDTH, QKV_DIM)), _sds((1, 128)), _sds((1, 128))],
        scratch=[pltpu.VMEM(w_in.shape, w_in.dtype), pltpu.SemaphoreType.DMA], sem=("arbitrary",),
    )(proj, y, y, proj, dq, dq, dk, dk, dv, dv, dgcb, dz, conv_w, alog_l, dtb_l, w_in, dres)


def _tri_inv(a_strict):
    ii = lax.broadcasted_iota(jnp.int32, (CHUNK, CHUNK), 0)
    jj = lax.broadcasted_iota(jnp.int32, (CHUNK, CHUNK), 1)
    x = (ii == jj).astype(F32) - a_strict
    pw = _bdot(a_strict, a_strict, BNN)
    for step in range(5):
        x = x + _bdot(x, pw, BNN)
        if step < 4:
            pw = _bdot(pw, pw, BNN)
    return x


@jax.custom_vjp
def _solved(a_strict, rhs, t, sol):
    return sol


def _solved_fwd(a_strict, rhs, t, sol):
    return sol, (t, sol)


def _solved_bwd(res, dsol):
    t, sol = res
    drhs = _mdot(t, dsol, BTN)
    return -_mdot(drhs, sol, BNT), drhs, jnp.zeros_like(t), jnp.zeros_like(sol)


_solved.defvjp(_solved_fwd, _solved_bwd)


def _prep(q, k, v, gc, beta, solve):
    ii = lax.broadcasted_iota(jnp.int32, (CHUNK, CHUNK), 0)
    jj = lax.broadcasted_iota(jnp.int32, (CHUNK, CHUNK), 1)
    causal, strict = ii >= jj, ii > jj
    gc_row = jnp.sum((ii == jj).astype(F32) * gc, axis=1, keepdims=True)
    decay = jnp.where(causal, jnp.exp(jnp.where(causal, gc - gc_row, 0.0)), 0.0)
    kb = k * beta
    a = jnp.where(strict, _bdot(kb, k, BNT) * decay, 0.0)
    eg = jnp.exp(gc)
    sol = solve(a, jnp.concatenate([v * beta, kb * eg], axis=-1))
    qk = _bdot(q, k, BNT) * decay
    last = lax.broadcasted_iota(jnp.int32, (CHUNK, 1), 0) == CHUNK - 1
    g_last = jnp.sum(jnp.where(last, gc, 0.0), axis=1, keepdims=True)
    kd = k * jnp.exp(g_last - gc)
    gl = jnp.exp(g_last) + jnp.zeros((1, 1, HEAD_DIM), F32)
    return sol[..., :HEAD_DIM], sol[..., HEAD_DIM:], qk, q * eg, kd, gl


def _prep_specs(s_len, chunks):
    gl_m = min(PREP_CHUNKS, s_len // CHUNK)
    m = min(chunks, gl_m)
    rows = m * CHUNK
    per = gl_m // m
    hd = pl.BlockSpec((rows, HEAD_DIM), lambda c, h: (c, h))
    gcb = pl.BlockSpec((rows, 128), lambda c, h: (c, 0))
    qk = pl.BlockSpec((None, rows, CHUNK), lambda c, h: (h, c, 0))
    gl = pl.BlockSpec((None, m, HEADS, HEAD_DIM), lambda c, h: (c // per, c % per, 0, 0))
    return rows, m, hd, gcb, qk, gl


def _head_cols(gcb, h, m):
    lane = lax.broadcasted_iota(jnp.int32, gcb.shape, 1)
    pick = lambda at: jnp.sum(jnp.where(lane == at, gcb, 0.0), axis=1, keepdims=True).reshape(m, CHUNK, 1)
    return pick(h + HEADS), pick(h)


def _gdn_prep(q, k, v, gcb):
    s_len = q.shape[0]
    rows, m, hd, gcb_spec, qk_spec, gl_spec = _prep_specs(s_len, PREP_CHUNKS)

    def body(q_ref, k_ref, v_ref, gcb_ref, u_ref, w_ref, qd_ref, kd_ref, qk_ref, gl_ref, t_ref):
        r3 = lambda ref, d: ref[...].reshape(m, CHUNK, d)
        gc, beta = _head_cols(gcb_ref[...], pl.program_id(1), m)

        def solve(a, rhs):
            t = _tri_inv(a)
            t_ref[...] = t.reshape(rows, CHUNK)
            return _mdot(t, rhs, BNN)

        u, w, qk, qd, kd, gl = _prep(r3(q_ref, HEAD_DIM), r3(k_ref, HEAD_DIM), r3(v_ref, HEAD_DIM), gc, beta, solve)
        u_ref[...] = u.reshape(rows, HEAD_DIM)
        w_ref[...] = w.reshape(rows, HEAD_DIM)
        qd_ref[...] = qd.reshape(rows, HEAD_DIM).astype(BF16)
        kd_ref[...] = kd.reshape(rows, HEAD_DIM).astype(BF16)
        qk_ref[...] = qk.reshape(rows, CHUNK).astype(BF16)
        gl_ref[:, pl.ds(pl.program_id(1), 1), :] = gl

    n_g = s_len // rows
    return _call(
        body, "gdn_prep", (n_g, HEADS), [hd, hd, hd, gcb_spec], [hd, hd, hd, hd, qk_spec, gl_spec, qk_spec],
        [_sds((s_len, D_MODEL))] * 2 + [_sds((s_len, D_MODEL), BF16)] * 2
        + [_sds((HEADS, s_len, CHUNK), BF16), _sds((n_g, m, HEADS, HEAD_DIM)), _sds((HEADS, s_len, CHUNK))],
        sem=("parallel", "arbitrary"))(q, k, v, gcb)


def _gdn_prep_bwd(q, k, v, gcb, t_inv, u, w, du, dw, dqd, dkd, dqk, dgl):
    s_len = q.shape[0]
    rows, m, hd, gcb_spec, qk_spec, gl_spec = _prep_specs(s_len, PREP_BWD_CHUNKS)

    def body(q_ref, k_ref, v_ref, gcb_ref, t_ref, u_ref, w_ref, du_ref, dw_ref, dqd_ref, dkd_ref, dqk_ref, dgl_ref,
             dq_ref, dk_ref, dv_ref, dgcb_ref):
        h = pl.program_id(1)
        r3 = lambda ref, d: ref[...].reshape(m, CHUNK, d)
        gc, beta = _head_cols(gcb_ref[...], h, m)
        t = r3(t_ref, CHUNK)
        sol = jnp.concatenate([r3(u_ref, HEAD_DIM), r3(w_ref, HEAD_DIM)], axis=-1)
        fn = lambda q_, k_, v_, gc_, bt_: _prep(q_, k_, v_, gc_, bt_, lambda a, rhs: _solved(a, rhs, t, sol))
        _, vjp = jax.vjp(fn, r3(q_ref, HEAD_DIM), r3(k_ref, HEAD_DIM), r3(v_ref, HEAD_DIM), gc, beta)
        ct = (r3(du_ref, HEAD_DIM), r3(dw_ref, HEAD_DIM), r3(dqk_ref, CHUNK), r3(dqd_ref, HEAD_DIM), r3(dkd_ref, HEAD_DIM),
              dgl_ref[:, pl.ds(h, 1), :] * (1.0 / HEAD_DIM))
        dq, dk, dv, dgc, dbt = vjp(ct)
        dq_ref[...] = dq.reshape(rows, HEAD_DIM)
        dk_ref[...] = dk.reshape(rows, HEAD_DIM)
        dv_ref[...] = dv.reshape(rows, HEAD_DIM)
        lane = lax.broadcasted_iota(jnp.int32, (rows, 128), 1)
        mine = jnp.where(lane == h, dbt.reshape(rows, 1), 0.0) + jnp.where(lane == h + HEADS, dgc.reshape(rows, 1), 0.0)
        _acc(dgcb_ref, h == 0, mine)

    return _call(
        body, "gdn_prep_bwd", (s_len // rows, HEADS),
        [hd, hd, hd, gcb_spec, qk_spec, hd, hd, hd, hd, hd, hd, qk_spec, gl_spec], [hd, hd, hd, gcb_spec],
        [_sds((s_len, D_MODEL))] * 3 + [_sds((s_len, 128))],
        sem=("parallel", "arbitrary"))(q, k, v, gcb, t_inv, u, w, du, dw, dqd, dkd, dqk, dgl)


def _scan_specs(n_c, m, k, reverse):
    n_b = n_c // k
    at = (lambda n: n_b - 1 - n) if reverse else (lambda n: n)
    row = pl.BlockSpec((k * CHUNK, D_MODEL), lambda n: (at(n), 0))
    qk = pl.BlockSpec((HEADS, k * CHUNK, CHUNK), lambda n: (0, at(n), 0))
    gl = pl.BlockSpec((None, k, HEADS, HEAD_DIM), lambda n: (at(n) // (m // k), at(n) % (m // k), 0, 0))
    st = pl.BlockSpec((k, HEADS, HEAD_DIM, HEAD_DIM), lambda n: (at(n), 0, 0, 0))
    return row, qk, gl, st


def _gdn_scan(u, w, qd, kd, qk, gl):
    s_len = u.shape[0]
    n_c = s_len // CHUNK
    k = min(SCAN_CHUNKS, gl.shape[1])
    row, qk_spec, gl_spec, st_spec = _scan_specs(n_c, gl.shape[1], k, False)

    def body(u_ref, w_ref, qd_ref, kd_ref, qk_ref, gl_ref, o_ref, st_ref, state):
        hs = range(HEADS)
        sl = [slice(h * HEAD_DIM, (h + 1) * HEAD_DIM) for h in hs]
        first = pl.program_id(0) == 0
        s_all = [jnp.where(first, 0.0, state[h]) for h in hs]
        for c in range(k):
            rows = slice(c * CHUNK, (c + 1) * CHUNK)
            s_b = [s.astype(BF16) for s in s_all]
            ws = [_bdot(w_ref[rows, sl[h]], s_b[h], NN) for h in hs]
            qs = [_bdot(qd_ref[rows, sl[h]], s_b[h], NN) for h in hs]
            vn = [(u_ref[rows, sl[h]] - ws[h]).astype(BF16) for h in hs]
            outs = [qs[h] + _bdot(qk_ref[h, rows, :], vn[h], NN) for h in hs]
            nxt = [s_all[h] * gl_ref[c, pl.ds(h, 1), :] + _bdot(kd_ref[rows, sl[h]], vn[h], TN) for h in hs]
            for h in hs:
                st_ref[c, h] = s_all[h]
                o_ref[rows, sl[h]] = outs[h]
            s_all = nxt
        for h in hs:
            state[h] = s_all[h]

    return _call(
        body, "gdn_scan", (n_c // k,), [row, row, row, row, qk_spec, gl_spec], [row, st_spec],
        [_sds((s_len, D_MODEL)), _sds((n_c, HEADS, HEAD_DIM, HEAD_DIM))],
        scratch=[pltpu.VMEM((HEADS, HEAD_DIM, HEAD_DIM), F32)], sem=("arbitrary",))(u, w, qd, kd, qk, gl)


def _gdn_scan_bwd(do, u, w, qd, kd, qk, gl, states):
    s_len = u.shape[0]
    n_c = s_len // CHUNK
    k = min(SCAN_CHUNKS, gl.shape[1])
    row, qk_spec, gl_spec, st_spec = _scan_specs(n_c, gl.shape[1], k, True)

    def body(do_ref, u_ref, w_ref, qd_ref, kd_ref, qk_ref, gl_ref, st_ref,
             du_ref, dw_ref, dqd_ref, dkd_ref, dqk_ref, dgl_ref, dstate):
        hs = range(HEADS)
        sl = [slice(h * HEAD_DIM, (h + 1) * HEAD_DIM) for h in hs]
        first = pl.program_id(0) == 0
        ds_f = [jnp.where(first, 0.0, dstate[h]) for h in hs]
        for c in reversed(range(k)):
            rows = slice(c * CHUNK, (c + 1) * CHUNK)
            s_f = [st_ref[c, h] for h in hs]
            s_b = [s.astype(BF16) for s in s_f]
            ds_b = [d.astype(BF16) for d in ds_f]
            do_b = [do_ref[rows, sl[h]].astype(BF16) for h in hs]
            w_b = [w_ref[rows, sl[h]].astype(BF16) for h in hs]
            ws = [_bdot(w_b[h], s_b[h], NN) for h in hs]
            dvn = [_bdot(qk_ref[h, rows, :], do_b[h], TN) + _bdot(kd_ref[rows, sl[h]], ds_b[h], NN) for h in hs]
            dqd = [_bdot(do_b[h], s_b[h], NT) for h in hs]
            t_do = [_bdot(qd_ref[rows, sl[h]], do_b[h], TN) for h in hs]
            vn = [(u_ref[rows, sl[h]] - ws[h]).astype(BF16) for h in hs]
            dvn_b = [d.astype(BF16) for d in dvn]
            dw = [-_bdot(dvn_b[h], s_b[h], NT) for h in hs]
            dkd = [_bdot(vn[h], ds_b[h], NT) for h in hs]
            dqk = [_bdot(do_b[h], vn[h], NT) for h in hs]
            t_dv = [_bdot(w_b[h], dvn_b[h], TN) for h in hs]
            for h in hs:
                du_ref[rows, sl[h]], dw_ref[rows, sl[h]], dqd_ref[rows, sl[h]], dkd_ref[rows, sl[h]] = dvn[h], dw[h], dqd[h], dkd[h]
                dqk_ref[h, rows, :] = dqk[h]
                dgl_ref[c, pl.ds(h, 1), :] = jnp.sum(s_f[h] * ds_f[h]) + jnp.zeros((1, HEAD_DIM), F32)
            ds_f = [ds_f[h] * gl_ref[c, pl.ds(h, 1), :] + t_do[h] - t_dv[h] for h in hs]
        for h in hs:
            dstate[h] = ds_f[h]

    return _call(
        body, "gdn_scan_bwd", (n_c // k,), [row, row, row, row, row, qk_spec, gl_spec, st_spec],
        [row, row, row, row, qk_spec, gl_spec],
        [_sds((s_len, D_MODEL))] * 4 + [_sds((HEADS, s_len, CHUNK)), _sds(gl.shape)],
        scratch=[pltpu.VMEM((HEADS, HEAD_DIM, HEAD_DIM), F32)], sem=("arbitrary",))(do, u, w, qd, kd, qk, gl, states)


def _gate_norm(o, z, nw):
    outs = []
    for h in range(HEADS):
        oh = o[:, h * HEAD_DIM:(h + 1) * HEAD_DIM]
        outs.append(oh * lax.rsqrt(jnp.mean(oh * oh, axis=-1, keepdims=True) + RMS_EPS))
    return jnp.concatenate(outs, axis=1) * nw * _silu(z)


def _gdn_out_fwd(o, proj, x, w_out, nw, gain, bias):
    s_len = x.shape[0]
    ts = min(ROW_TILE, s_len)

    def body(o_ref, z_ref, x_ref, w_ref, nw_ref, g_ref, b_ref, y_ref, xhat_ref, rstd_ref):
        on = _gate_norm(o_ref[...], z_ref[...], nw_ref[...])
        r = ALPHA * x_ref[...] + _bdot(on, w_ref[...], NN)
        y_ref[...], xhat_ref[...], rstd_ref[...] = _ln_fwd(r, g_ref[...], b_ref[...])

    tile = pl.BlockSpec((ts, D_MODEL), lambda i: (i, 0))
    row = _row(D_MODEL)
    return _call(
        body, "gdn_out_fwd", (s_len // ts,),
        [tile, pl.BlockSpec((ts, D_MODEL), lambda i: (i, QKV_DIM // D_MODEL)), tile, _full((D_MODEL, D_MODEL)), row, row, row],
        [tile, tile, pl.BlockSpec((ts, 1), lambda i: (i, 0))],
        [_sds((s_len, D_MODEL)), _sds((s_len, D_MODEL)), _sds((s_len, 1))], sem=("parallel",))(o, proj, x, w_out, nw, gain, bias)


def _gdn_out_bwd(dy, xhat, rstd, o, proj, w_out, nw, gain):
    s_len = o.shape[0]
    ts = min(ROW_TILE, s_len)

    def body(dy_ref, xh_ref, rs_ref, o_ref, z_ref, w_ref, nw_ref, g_ref,
             dres_ref, do_ref, dz_ref, on_ref, drb_ref, dg_ref, db_ref, dnw_ref):
        dy_t, xh_t = dy_ref[...], xh_ref[...]
        dr = _ln_bwd(dy_t, xh_t, rs_ref[...], g_ref[...])
        dres_ref[...] = ALPHA * dr
        drb_ref[...] = dr.astype(BF16)
        on, vjp = jax.vjp(_gate_norm, o_ref[...], z_ref[...], nw_ref[...])
        on_ref[...] = on.astype(BF16)
        do, dz, dnw = vjp(_bdot(dr, w_ref[...], NT))
        do_ref[...] = do
        dz_ref[...] = dz.astype(BF16)
        first = pl.program_id(0) == 0
        _acc(dg_ref, first, jnp.sum(dy_t * xh_t, axis=0, keepdims=True))
        _acc(db_ref, first, jnp.sum(dy_t, axis=0, keepdims=True))
        _acc(dnw_ref, first, sum(dnw[:, h * HEAD_DIM:(h + 1) * HEAD_DIM] for h in range(HEADS)))

    tile = pl.BlockSpec((ts, D_MODEL), lambda i: (i, 0))
    row = _row(D_MODEL)
    return _call(
        body, "gdn_out_bwd", (s_len // ts,),
        [tile, tile, pl.BlockSpec((ts, 1), lambda i: (i, 0)), tile,
         pl.BlockSpec((ts, D_MODEL), lambda i: (i, QKV_DIM // D_MODEL)), _full((D_MODEL, D_MODEL)), row, row],
        [tile, tile, tile, tile, tile, row, row, _row(HEAD_DIM)],
        [_sds((s_len, D_MODEL))] * 2 + [_sds((s_len, D_MODEL), BF16)] * 3 + [_sds((1, D_MODEL))] * 2 + [_sds((1, HEAD_DIM))],
        sem=("arbitrary",))(dy, xhat, rstd, o, proj, w_out, nw, gain)


def _adamw(w, g, m, v, name):
    r, c = w.shape
    tr = min(ADAM_ROWS, r)
    pieces = list(g) if isinstance(g, (list, tuple)) else [g]
    n_p = len(pieces)
    blocks = [a.shape[0] // tr for a in pieces]
    first = [sum(blocks[:k]) for k in range(n_p)]

    def body(*refs):
        w_ref, g_refs, (m_ref, v_ref) = refs[0], refs[1:1 + n_p], refs[1 + n_p:3 + n_p]
        go_ref, d_ref, nm_ref, nv_ref = refs[3 + n_p:]
        i = pl.program_id(0)
        g_t = g_refs[0][...]
        for k in range(1, n_p):
            g_t = jnp.where(i >= first[k], g_refs[k][...], g_t)
        go_ref[...] = g_t
        nm = ADAM_B1 * m_ref[...] + (1.0 - ADAM_B1) * g_t
        nv = ADAM_B2 * v_ref[...] + (1.0 - ADAM_B2) * (g_t * g_t)
        m_hat = nm / (1.0 - ADAM_B1 ** ADAM_STEP)
        v_hat = nv / (1.0 - ADAM_B2 ** ADAM_STEP)
        d_ref[...] = -ADAM_LR * (m_hat / (jnp.sqrt(v_hat) + ADAM_EPS) + ADAM_WD * w_ref[...])
        nm_ref[...] = nm
        nv_ref[...] = nv

    tile = pl.BlockSpec((tr, c), lambda i: (i, 0))
    g_specs = [pl.BlockSpec((tr, c), lambda i, k=k: (jnp.clip(i - first[k], 0, blocks[k] - 1), 0)) for k in range(n_p)]
    return _call(body, name, (r // tr,), [tile] + g_specs + [tile] * 2, [tile] * 4, [_sds((r, c))] * 4,
                 sem=("parallel",))(w, *pieces, m, v)


def _assemble_w_in(shards):
    rows = 256
    width = GDN_IN_DIM // N_SHARD

    def body(s_ref, o_ref):
        pad = jnp.zeros((rows, GDN_IN_PAD - GDN_IN_DIM), shards.dtype)
        o_ref[...] = jnp.concatenate([s_ref[j] for j in range(N_SHARD)] + [pad], axis=1)

    return _call(body, "w_in_assemble", (D_MODEL // rows,), [pl.BlockSpec((N_SHARD, rows, width), lambda i: (0, i, 0))],
                 pl.BlockSpec((rows, GDN_IN_PAD), lambda i: (i, 0)), _sds((D_MODEL, GDN_IN_PAD), shards.dtype),
                 sem=("parallel",))(shards)


def _split_w_in(full):
    rows = 256
    width = GDN_IN_DIM // N_SHARD

    def body(f_ref, o_ref):
        f = f_ref[...]
        for j in range(N_SHARD):
            o_ref[j] = f[:, j * width:(j + 1) * width]

    return _call(body, "w_in_split", (D_MODEL // rows,), [pl.BlockSpec((rows, GDN_IN_PAD), lambda i: (i, 0))],
                 pl.BlockSpec((N_SHARD, rows, width), lambda i: (0, i, 0)), _sds((N_SHARD, D_MODEL, width), full.dtype),
                 sem=("parallel",))(full)


def _place():
    x, y, c = lax.axis_index("x"), lax.axis_index("y"), lax.axis_index("c")
    return x, y, c, [(1 - x, y), (x, 1 - y), (1 - x, 1 - y)]


def _row_tile(rows):
    return max(t for t in range(8, min(rows, 640) + 1, 8) if rows % t == 0)


def _place_shard(part, me, dtype, name, layer=0):
    _, _, r, c = part.shape
    tr = _row_tile(r)

    def body(me_ref, p_ref, o_ref):
        o_ref[...] = p_ref[...].astype(dtype)

    return pl.pallas_call(
        body, name=name, out_shape=_sds((N_SHARD, 2, r, c), dtype),
        grid_spec=pltpu.PrefetchScalarGridSpec(
            num_scalar_prefetch=1, grid=(2, r // tr),
            in_specs=[pl.BlockSpec((None, None, tr, c), lambda h, i, me_ref: (layer, h, i, 0))],
            out_specs=pl.BlockSpec((None, None, tr, c), lambda h, i, me_ref: (me_ref[0], h, i, 0))))(me, part)


def _gather_sems(n):
    return [pltpu.SemaphoreType.DMA((6 * n,)), pltpu.SemaphoreType.DMA((6 * n,))]


def _gather_steps(dsts, send_sems, recv_sems):
    n = len(dsts)
    x, y, c, chips = _place()
    me = 2 * x + y
    sibling = (x, y, 1 - c)

    def ici(k, j, slot):
        px, py = chips[j]
        view = dsts[k].at[slot, c]
        return pltpu.make_async_remote_copy(
            src_ref=view, dst_ref=view, send_sem=send_sems.at[6 * k + j],
            recv_sem=recv_sems.at[6 * k + j], device_id=(px, py, c), device_id_type=MESH)

    def d2d(k, j, half):
        px, py = chips[j]
        view = dsts[k].at[2 * px + py, half]
        return pltpu.make_async_remote_copy(
            src_ref=view, dst_ref=view, send_sem=send_sems.at[6 * k + 3 + j], recv_sem=recv_sems.at[6 * k + 3 + j],
            device_id=sibling, device_id_type=MESH)

    def start():
        for k in range(n):
            for j in range(3):
                ici(k, j, me).start()

    def finish():
        fwds = []
        for k in range(n):
            for j, (px, py) in enumerate(chips):
                ici(k, j, 2 * px + py).wait_recv()
                fwds.append(d2d(k, j, c))
                fwds[-1].start()
        for k in range(n):
            for j in range(3):
                d2d(k, j, 1 - c).wait_recv()
        for k in range(n):
            for j in range(3):
                ici(k, j, me).wait_send()
        for cp in fwds:
            cp.wait_send()

    return start, finish


def _all_gather(bufs, name):
    n = len(bufs)

    def body(*refs):
        start, finish = _gather_steps(refs[n:2 * n], *refs[2 * n:])
        start()
        finish()

    return pl.pallas_call(
        body, name=name, out_shape=[_sds(a.shape, a.dtype) for a in bufs],
        in_specs=[ANY] * n, out_specs=[ANY] * n, input_output_aliases={k: k for k in range(n)},
        scratch_shapes=_gather_sems(n))(*bufs)


def _swap_halves(pieces, name):
    n = len(pieces)

    def body(*refs):
        srcs, dsts = refs[:n], refs[n:2 * n]
        send_sems, recv_sems = refs[2 * n:]
        x, y, c, _ = _place()
        copies = []
        for k in range(n):
            hr = pieces[k].shape[1] // 2
            copies.append(pltpu.make_async_remote_copy(
                src_ref=srcs[k].at[:, pl.ds((1 - c) * hr, hr), :], dst_ref=dsts[k],
                send_sem=send_sems.at[k], recv_sem=recv_sems.at[k], device_id=(x, y, 1 - c), device_id_type=MESH))
        for cp in copies:
            cp.start()
        for cp in copies:
            cp.wait()

    return pl.pallas_call(
        body, name=name, out_shape=[_sds((N_SHARD, a.shape[1] // 2, a.shape[2])) for a in pieces],
        in_specs=[ANY] * n, out_specs=[ANY] * n,
        scratch_shapes=[pltpu.SemaphoreType.DMA((n,)), pltpu.SemaphoreType.DMA((n,))])(*pieces)


def _add_half(piece, other, place, dtype, name):
    n, hr, cols = other.shape
    tr = _row_tile(hr)

    def body(pl_ref, a_ref, b_ref, o_ref):
        o_ref[...] = (a_ref[...] + b_ref[...]).astype(dtype)

    tile = pl.BlockSpec((None, tr, cols), lambda s, i, pl_ref: (s, i, 0))
    return pl.pallas_call(
        body, name=name, out_shape=_sds(other.shape, dtype),
        grid_spec=pltpu.PrefetchScalarGridSpec(
            num_scalar_prefetch=1, grid=(n, hr // tr),
            in_specs=[pl.BlockSpec((None, None, tr, cols), lambda s, i, pl_ref: (s, pl_ref[1], i, 0)), tile],
            out_specs=tile))(place, piece.reshape(n, 2, hr, cols), other)


def _scatter_sems(n):
    return [pltpu.SemaphoreType.DMA((3 * n,)), pltpu.SemaphoreType.DMA((3 * n,))]


def _scatter_steps(srcs, dsts, send_sems, recv_sems):
    n = len(srcs)
    x, y, c, chips = _place()
    me = 2 * x + y

    def ici(k, j, src_slot, dst_slot):
        px, py = chips[j]
        return pltpu.make_async_remote_copy(
            src_ref=srcs[k].at[src_slot], dst_ref=dsts[k].at[dst_slot], send_sem=send_sems.at[3 * k + j],
            recv_sem=recv_sems.at[3 * k + j], device_id=(px, py, c), device_id_type=MESH)

    def start():
        for k in range(n):
            for j, (px, py) in enumerate(chips):
                ici(k, j, 2 * px + py, me).start()

    def finish():
        for k in range(n):
            for j, (px, py) in enumerate(chips):
                ici(k, j, me, 2 * px + py).wait_recv()
        for k in range(n):
            for j, (px, py) in enumerate(chips):
                ici(k, j, 2 * px + py, me).wait_send()

    return start, finish


def _scatter_chips(parts, name):
    n = len(parts)

    def body(*refs):
        start, finish = _scatter_steps(refs[:n], refs[n:2 * n], *refs[2 * n:])
        start()
        finish()

    return pl.pallas_call(
        body, name=name, out_shape=[_sds(a.shape, a.dtype) for a in parts],
        in_specs=[ANY] * n, out_specs=[ANY] * n, scratch_shapes=_scatter_sems(n))(*parts)


def _sum_chips(landed, own, place, name):
    _, r, cols = landed.shape
    tr = _row_tile(r)

    def body(pl_ref, q_ref, p_ref, o_ref):
        me = pl_ref[0]
        f = lambda j: jnp.where(me == j, p_ref[...], q_ref[j]).astype(F32)
        o_ref[...] = ((f(0) + f(1)) + f(2)) + f(3)

    return pl.pallas_call(
        body, name=name, out_shape=_sds((2, r, cols)),
        grid_spec=pltpu.PrefetchScalarGridSpec(
            num_scalar_prefetch=1, grid=(r // tr,),
            in_specs=[pl.BlockSpec((N_SHARD, tr, cols), lambda i, pl_ref: (0, i, 0)),
                      pl.BlockSpec((None, tr, cols), lambda i, pl_ref: (pl_ref[0], i, 0))],
            out_specs=pl.BlockSpec((None, tr, cols), lambda i, pl_ref: (pl_ref[1], i, 0))))(place, landed, own)


def _join_halves(bufs):
    n = len(bufs)

    def body(*refs):
        dsts = refs[n:2 * n]
        send_sems, recv_sems = refs[2 * n:]
        x, y, c, _ = _place()
        copies = [pltpu.make_async_remote_copy(
            src_ref=dsts[k].at[c], dst_ref=dsts[k].at[c], send_sem=send_sems.at[k], recv_sem=recv_sems.at[k],
            device_id=(x, y, 1 - c), device_id_type=MESH) for k in range(n)]
        for cp in copies:
            cp.start()
        for cp in copies:
            cp.wait()

    return pl.pallas_call(
        body, name="grads_join_halves", out_shape=[_sds(a.shape) for a in bufs], in_specs=[ANY] * n, out_specs=[ANY] * n,
        input_output_aliases={k: k for k in range(n)},
        scratch_shapes=[pltpu.SemaphoreType.DMA((n,)), pltpu.SemaphoreType.DMA((n,))])(*bufs)


GATHER_F32 = ("ln_gain", "ln_bias", "pool_b", "gdn_conv")
REPLICATED = ("pool_scale", "gdn_a_log", "gdn_dt_bias", "gdn_norm_w", "ple_gate_b")
WEIGHTS = ("ln_gain", "ln_bias", "pool_w", "pool_b", "pool_scale", "gdn_w_in", "gdn_conv", "gdn_a_log", "gdn_dt_bias",
           "gdn_norm_w", "gdn_w_out", "mlp_w1", "mlp_w2", "ple_gate_w", "ple_gate_b", "ple_proj")
SMALL_GRADS = ("ple_proj", "pool_w", "ln_gain", "ln_bias", "pool_b", "gdn_conv") + REPLICATED


def _pack(parts, lanes, row_multiple):
    flat = jnp.concatenate([a.reshape(-1) for a in parts])
    rows = -(-flat.shape[0] // (2 * lanes * row_multiple)) * row_multiple
    return jnp.pad(flat, (0, 2 * rows * lanes - flat.shape[0])).reshape(2, rows, lanes)


def _unpack(flat, shapes):
    out, off = [], 0
    for shp in shapes:
        n = math.prod(shp)
        out.append(flat[..., off:off + n].reshape(flat.shape[:-1] + tuple(shp)))
        off += n
    return out


def _pad_lanes(a, offset, width=128):
    return jnp.pad(a, ((0, 0), (offset, width - offset - a.shape[1])))


def kernel(x, p, ln_gain, ln_bias, pool_w, pool_b, pool_scale, gdn_w_in, gdn_conv, gdn_a_log, gdn_dt_bias, gdn_norm_w, gdn_w_out, mlp_w1, mlp_w2, ple_gate_w, ple_gate_b, ple_proj, loss_target, m_ln_gain, m_ln_bias, m_pool_w, m_pool_b, m_pool_scale, m_gdn_w_in, m_gdn_conv, m_gdn_a_log, m_gdn_dt_bias, m_gdn_norm_w, m_gdn_w_out, m_mlp_w1, m_mlp_w2, m_ple_gate_w, m_ple_gate_b, m_ple_proj, v_ln_gain, v_ln_bias, v_pool_w, v_pool_b, v_pool_scale, v_gdn_w_in, v_gdn_conv, v_gdn_a_log, v_gdn_dt_bias, v_gdn_norm_w, v_gdn_w_out, v_mlp_w1, v_mlp_w2, v_ple_gate_w, v_ple_gate_b, v_ple_proj):
    shard = dict(ln_gain=ln_gain, ln_bias=ln_bias, pool_w=pool_w, pool_b=pool_b, pool_scale=pool_scale, gdn_w_in=gdn_w_in,
                 gdn_conv=gdn_conv, gdn_a_log=gdn_a_log, gdn_dt_bias=gdn_dt_bias, gdn_norm_w=gdn_norm_w, gdn_w_out=gdn_w_out,
                 mlp_w1=mlp_w1, mlp_w2=mlp_w2, ple_gate_w=ple_gate_w, ple_gate_b=ple_gate_b, ple_proj=ple_proj)
    mom = dict(ln_gain=m_ln_gain, ln_bias=m_ln_bias, pool_w=m_pool_w, pool_b=m_pool_b, pool_scale=m_pool_scale,
               gdn_w_in=m_gdn_w_in, gdn_conv=m_gdn_conv, gdn_a_log=m_gdn_a_log, gdn_dt_bias=m_gdn_dt_bias,
               gdn_norm_w=m_gdn_norm_w, gdn_w_out=m_gdn_w_out, mlp_w1=m_mlp_w1, mlp_w2=m_mlp_w2, ple_gate_w=m_ple_gate_w,
               ple_gate_b=m_ple_gate_b, ple_proj=m_ple_proj)
    var = dict(ln_gain=v_ln_gain, ln_bias=v_ln_bias, pool_w=v_pool_w, pool_b=v_pool_b, pool_scale=v_pool_scale,
               gdn_w_in=v_gdn_w_in, gdn_conv=v_gdn_conv, gdn_a_log=v_gdn_a_log, gdn_dt_bias=v_gdn_dt_bias,
               gdn_norm_w=v_gdn_norm_w, gdn_w_out=v_gdn_w_out, mlp_w1=v_mlp_w1, mlp_w2=v_mlp_w2, ple_gate_w=v_ple_gate_w,
               ple_gate_b=v_ple_gate_b, ple_proj=v_ple_proj)

    xi, yi, ci = lax.axis_index("x"), lax.axis_index("y"), lax.axis_index("c")
    me = (2 * xi + yi).reshape(1).astype(jnp.int32)
    place = jnp.stack([2 * xi + yi, ci]).astype(jnp.int32)
    early = [("mlp_w1", 0), ("mlp_w2", 0), ("ple_gate_w", 0), ("ple_proj", 0), ("pool_w", 0)]
    late = [("mlp_w1", 1), ("mlp_w2", 1), ("ple_gate_w", 1), ("ple_proj", 1), ("gdn_w_out", 0), ("gdn_w_in", 0)]
    halved = lambda n: shard[n].reshape(shard[n].shape[0], 2, -1, shard[n].shape[-1])
    placed = lambda ops, tag: [_place_shard(halved(n), me, BF16, f"place_{tag}_{n}", l) for n, l in ops]
    small_in = _place_shard(_pack([shard[n] for n in GATHER_F32], 128, 8)[None], me, F32, "place_small")
    got_early = _all_gather(placed(early, "early") + [small_in], "weights_all_gather_early")
    placed_late = placed(late, "late")
    st = dict(zip(GATHER_F32, _unpack(got_early[-1].reshape(N_SHARD, -1), [shard[n].shape for n in GATHER_F32])))

    cat_last = lambda a: jnp.moveaxis(a, 0, -2).reshape(a.shape[1:-1] + (N_SHARD * a.shape[-1],))
    gain = cat_last(st["ln_gain"])
    bias = cat_last(st["ln_bias"])
    wp = got_early[4].reshape(N_SHARD, 4, POOL_GROUP // N_SHARD, POOL_GROUP)
    pb = cat_last(st["pool_b"]).reshape(1, D_MODEL)
    ps = pool_scale
    conv_w = cat_last(st["gdn_conv"])[0]
    merged = lambda g: g.reshape(N_SHARD, -1, g.shape[-1])
    mlp_w = lambda i, got: (merged(got[0]), merged(got[1]), merged(got[2]), ple_gate_b[i:i + 1], merged(got[3]))
    alog_l = _pad_lanes(gdn_a_log, HEADS)
    dtb_l = _pad_lanes(gdn_dt_bias, HEADS)
    nw = jnp.tile(gdn_norm_w, (1, HEADS))
    ln = lambda i, k: (gain[i, k][None], bias[i, k][None])

    x0 = x[0]
    p0, p1 = p[0, 0], p[1, 0]

    x1, xh1, rs1 = _pool_fwd(x0, wp, pb, ps, *ln(0, 0))
    (x2, xh2, rs2, a0, xb0), got_late = _mlp_fwd(x1, p0, *mlp_w(0, got_early), *ln(0, 1), "mlp_fwd_0", gather=placed_late)
    w_out = got_late[4].reshape(D_MODEL, D_MODEL)
    w_in = _assemble_w_in(merged(got_late[5]))
    proj, y_conv, q, k, v, gcb = _conv_fwd(x2, w_in, conv_w, alog_l, dtb_l)
    u, w, qd, kd, qk, gl, t_inv = _gdn_prep(q, k, v, gcb)
    o, states = _gdn_scan(u, w, qd, kd, qk, gl)
    x3, xh3, rs3 = _gdn_out_fwd(o, proj, x2, w_out, nw, *ln(1, 0))
    (dy4, xh4, rs4, a1, xb1, loss_l), _ = _mlp_fwd(x3, p1, *mlp_w(1, got_late), *ln(1, 1), "mlp_fwd_1", target=loss_target[0])

    g_gain = [[None, None], [None, None]]
    g_bias = [[None, None], [None, None]]

    def mlp_grads(i, dy, xh, rs, x_mid, xb, a, p_i, got, scatter=()):
        (dx, dh, dzg, dpp, drb, dg, db, dgb), landed = _mlp_bwd(
            dy, xh, rs, x_mid, a, p_i, *mlp_w(i, got), ln(i, 1)[0], f"mlp_bwd_{i}", scatter=scatter)
        g_gain[i][1], g_bias[i][1] = dg, db
        return dx, dict(
            mlp_w1=_wgrad(xb, dh, f"dw1_{i}", stack_cols=True), mlp_w2=_wgrad(a, drb, f"dw2_{i}").reshape(N_SHARD, -1, D_MODEL),
            ple_gate_w=_wgrad(xb, dzg, f"dgate_w_{i}").reshape(N_SHARD, -1, D_MODEL),
            ple_proj=_wgrad(p_i, dpp, f"dproj_{i}", stack_cols=True), ple_gate_b=dgb), landed

    def chip_sums(pieces, wire, tag):
        others = _swap_halves(pieces, f"grads_swap_halves_{tag}")
        return [_add_half(a, b, place, t, f"grads_add_half_{tag}{i}") for i, (a, b, t) in enumerate(zip(pieces, others, wire))]

    dx3, gl1, _ = mlp_grads(1, dy4, xh4, rs4, x3, xb1, a1, p1, got_late)
    dres, do, dz, on_b, drb3, g_gain[1][0], g_bias[1][0], d_nw = _gdn_out_bwd(dx3, xh3, rs3, o, proj, w_out, nw, ln(1, 0)[0])
    d_wout = _wgrad(on_b, drb3, "dw_out").reshape(N_SHARD, -1, D_MODEL)
    du, dw, dqd, dkd, dqk, dgl = _gdn_scan_bwd(do, u, w, qd, kd, qk, gl, states)
    dq, dk, dv, dgcb = _gdn_prep_bwd(q, k, v, gcb, t_inv, u, w, du, dw, dqd, dkd, dqk, dgl)
    dproj, dx2, d_conv, d_alog_l, d_dtb_l = _conv_bwd(proj, y_conv, dq, dk, dv, dgcb, dz, conv_w, alog_l, dtb_l, w_in, dres)
    d_win = _split_w_in(_wgrad(x2, dproj, "dw_in"))
    sums_late = chip_sums([gl1["mlp_w1"], gl1["mlp_w2"], gl1["ple_gate_w"], d_wout, d_win], [BF16] * 5, "late")
    dx1, gl0, landed_late = mlp_grads(0, dx2, xh2, rs2, x1, xb0, a0, p0, got_early, scatter=sums_late)
    sums_early = chip_sums([gl0["mlp_w1"], gl0["mlp_w2"], gl0["ple_gate_w"]], [BF16] * 3, "early")
    (dx0, g_gain[0][0], g_bias[0][0], d_ps, d_pb, d_wp), landed_early = _pool_bwd(
        dx1, xh1, rs1, x0, wp, pb, ps, ln(0, 0)[0], scatter=sums_early)

    split_last = lambda a: jnp.moveaxis(a.reshape(a.shape[:-1] + (N_SHARD, a.shape[-1] // N_SHARD)), -2, 0)
    small_st = dict(
        ple_proj=jnp.stack([gl0["ple_proj"], gl1["ple_proj"]], axis=1),
        pool_w=jnp.moveaxis(d_wp.reshape(4, N_SHARD, POOL_GROUP // N_SHARD, POOL_GROUP), 1, 0)[:, None],
        ln_gain=split_last(jnp.stack([jnp.concatenate(r, axis=0) for r in g_gain])),
        ln_bias=split_last(jnp.stack([jnp.concatenate(r, axis=0) for r in g_bias])),
        pool_b=split_last(d_pb.reshape(1, 4, POOL_GROUP)),
        gdn_conv=split_last(d_conv)[:, None],
    )
    rep = dict(pool_scale=d_ps, gdn_a_log=d_alog_l[:, HEADS:2 * HEADS], gdn_dt_bias=d_dtb_l[:, HEADS:2 * HEADS],
               gdn_norm_w=d_nw, ple_gate_b=jnp.concatenate([gl0["ple_gate_b"], gl1["ple_gate_b"]], axis=0))
    for n in REPLICATED:
        small_st[n] = jnp.broadcast_to(rep[n][None], (N_SHARD,) + rep[n].shape)
    small_flat = jnp.concatenate([small_st[n].reshape(N_SHARD, -1) for n in SMALL_GRADS], axis=1)
    small_rows = -(-small_flat.shape[1] // (16 * LANES)) * 16
    small_piece = jnp.pad(small_flat, ((0, 0), (0, small_rows * LANES - small_flat.shape[1]))).reshape(N_SHARD, small_rows, LANES)

    sums_small = chip_sums([small_piece], [F32], "small")
    landed_small = _scatter_chips(sums_small, "grads_scatter_chips_small")
    red = _join_halves([_sum_chips(q_, p_, place, f"grads_sum_chips_{i}") for i, (q_, p_) in
                        enumerate(zip(list(landed_early) + list(landed_small) + list(landed_late),
                                      sums_early + sums_small + sums_late))])
    red = [r.reshape(-1, r.shape[-1]) for r in red]
    grads = dict(mlp_w1=[red[0], red[4]], mlp_w2=[red[1], red[5]], ple_gate_w=[red[2], red[6]], gdn_w_out=red[7], gdn_w_in=red[8])
    grads.update(zip(SMALL_GRADS, _unpack(red[3].reshape(-1), [shard[n].shape for n in SMALL_GRADS])))

    delta, new_m, new_v = {}, {}, {}
    small = [n for n in WEIGHTS if shard[n].size < 128 * 128]
    for n in WEIGHTS:
        if n in small:
            continue
        to2d = lambda a, n=n: a.reshape(-1, shard[n].shape[-1])
        g_n = grads[n] if isinstance(grads[n], list) else to2d(grads[n])
        g2, d2, m2, v2 = _adamw(to2d(shard[n]), g_n, to2d(mom[n]), to2d(var[n]), "adamw_" + n)
        grads[n], delta[n], new_m[n], new_v[n] = (t.reshape(shard[n].shape) for t in (g2, d2, m2, v2))
    pk = lambda d: _pack([d[n] for n in small], 128, 8).reshape(-1, 128)
    _, d2, m2, v2 = _adamw(pk(shard), pk(grads), pk(mom), pk(var), "adamw_small")
    for dst, t in ((delta, d2), (new_m, m2), (new_v, v2)):
        dst.update(zip(small, _unpack(t.reshape(-1), [shard[n].shape for n in small])))

    loss = lax.psum(loss_l[0, 0], ("x", "y", "c"))
    return (loss, dx0[None], *[grads[n] for n in WEIGHTS], *[delta[n] for n in WEIGHTS],
            *[new_m[n] for n in WEIGHTS], *[new_v[n] for n in WEIGHTS])
```

```python
import math

import jax
import jax.numpy as jnp
from jax import lax
from jax.experimental import pallas as pl
from jax.experimental.pallas import tpu as pltpu

F32 = jnp.float32
BF16 = jnp.bfloat16

D_MODEL = 1024
D_FF = 4096
PLE_DIM = 256
N_SHARD = 4
POOL_WINDOWS = (2, 4, 8, 16)
POOL_GROUP = 256
POOL_HALO = 16
HEADS = 8
HEAD_DIM = 128
CHUNK = 64
CONV_WIDTH = 4
CONV_HALO = 8
QKV_DIM = 3 * D_MODEL
GDN_IN_DIM = QKV_DIM + D_MODEL + 2 * HEADS
GDN_IN_PAD = 4224
BA_BLOCK = (QKV_DIM + D_MODEL) // 128
ALPHA = (2.0 * 2) ** 0.25
LN_EPS = 1e-5
RMS_EPS = 1e-6
L2_EPS = 1e-6
ADAM_LR, ADAM_B1, ADAM_B2, ADAM_EPS, ADAM_WD, ADAM_STEP = 0.001, 0.9, 0.999, 1e-08, 0.01, 10

ROW_TILE = 512
CONV_TILE = 256
PREP_CHUNKS = 32
PREP_BWD_CHUNKS = 16
SCAN_CHUNKS = 8
LANES = 1024
ADAM_ROWS = 256

NN = (((1,), (0,)), ((), ()))
NT = (((1,), (1,)), ((), ()))
TN = (((0,), (0,)), ((), ()))
BNN = (((2,), (1,)), ((0,), (0,)))
BNT = (((2,), (2,)), ((0,), (0,)))
BTN = (((1,), (1,)), ((0,), (0,)))
MESH = pl.DeviceIdType.MESH
ANY = pl.BlockSpec(memory_space=pl.ANY)


def _bdot(a, b, dims):
    return lax.dot_general(a.astype(BF16), b.astype(BF16), dims, preferred_element_type=F32)


def _hdot(a, b, dims):
    return lax.dot_general(a, b, dims, precision=lax.Precision.HIGHEST, preferred_element_type=F32)


def _mdot(a, b, dims):
    return lax.dot_general(a, b, dims, precision=lax.Precision.HIGH, preferred_element_type=F32)


def _sigmoid(x):
    return 0.5 * jnp.tanh(0.5 * x) + 0.5


def _silu(x):
    return x * _sigmoid(x)


def _softplus(x):
    return jnp.maximum(x, 0.0) + jnp.log1p(jnp.exp(-jnp.abs(x)))


def _call(body, name, grid, in_specs, out_specs, out_shape, scratch=(), sem=None, aliases=None):
    params = pltpu.CompilerParams(dimension_semantics=sem) if sem else None
    return pl.pallas_call(
        body, name=name, grid=grid, in_specs=in_specs, out_specs=out_specs, out_shape=out_shape,
        scratch_shapes=list(scratch), compiler_params=params, input_output_aliases=aliases or {})


def _row(d):
    return pl.BlockSpec((1, d), lambda *_: (0, 0))


def _full(shape):
    n = len(shape)
    return pl.BlockSpec(shape, lambda *_: (0,) * n)


def _sds(shape, dtype=F32):
    return jax.ShapeDtypeStruct(shape, dtype)


def _ln_fwd(r, gain, bias):
    mu = jnp.mean(r, axis=-1, keepdims=True)
    xc = r - mu
    rstd = lax.rsqrt(jnp.mean(xc * xc, axis=-1, keepdims=True) + LN_EPS)
    xhat = xc * rstd
    return xhat * gain + bias, xhat, rstd


def _ln_bwd(dy, xhat, rstd, gain):
    dxh = dy * gain
    m1 = jnp.mean(dxh, axis=-1, keepdims=True)
    m2 = jnp.mean(dxh * xhat, axis=-1, keepdims=True)
    return rstd * (dxh - m1 - xhat * m2)


def _acc(ref, first, val):
    @pl.when(first)
    def _():
        ref[...] = val

    @pl.when(jnp.logical_not(first))
    def _():
        ref[...] += val


def _pooled_groups(xe, t0, ts):
    pos = (t0 + lax.broadcasted_iota(jnp.int32, (ts, 1), 0) + 1).astype(F32)
    outs = []
    for gi, win in enumerate(POOL_WINDOWS):
        xs = xe[:, gi * POOL_GROUP:(gi + 1) * POOL_GROUP]
        s, k = xs, 1
        while k < win:
            s = s + pltpu.roll(s, k, 0)
            k *= 2
        mean = s[POOL_HALO:] / jnp.minimum(pos, float(win))
        outs.append(mean - xs[POOL_HALO:])
    return outs


def _pool_groups_w(w_ref):
    return [jnp.concatenate([w_ref[s, g] for s in range(N_SHARD)], axis=0) for g in range(4)]


def _pool_fwd(x, wp, pb, ps, gain, bias):
    s_len = x.shape[0]
    ts = min(ROW_TILE, s_len)
    hb = ts // POOL_HALO

    def body(x_ref, halo_ref, w_ref, pb_ref, ps_ref, g_ref, b_ref, y_ref, xhat_ref, rstd_ref):
        i = pl.program_id(0)
        x_t = x_ref[...]
        halo = jnp.where(i > 0, halo_ref[...], 0.0)
        pooled = _pooled_groups(jnp.concatenate([halo, x_t], axis=0), i * ts, ts)
        wg = _pool_groups_w(w_ref)
        y = jnp.concatenate([_bdot(pooled[g], wg[g], NN) for g in range(4)], axis=1) + pb_ref[...]
        r = ALPHA * x_t + y * ps_ref[...]
        y_ref[...], xhat_ref[...], rstd_ref[...] = _ln_fwd(r, g_ref[...], b_ref[...])

    tile = pl.BlockSpec((ts, D_MODEL), lambda i: (i, 0))
    return _call(
        body, "pool_fwd", (s_len // ts,),
        [tile, pl.BlockSpec((POOL_HALO, D_MODEL), lambda i: (jnp.maximum(i * hb - 1, 0), 0)),
         _full(wp.shape), _row(D_MODEL), _row(D_MODEL), _row(D_MODEL), _row(D_MODEL)],
        [tile, tile, pl.BlockSpec((ts, 1), lambda i: (i, 0))],
        [_sds((s_len, D_MODEL)), _sds((s_len, D_MODEL)), _sds((s_len, 1))],
        sem=("parallel",))(x, x, wp, pb, ps, gain, bias)


def _pool_bwd(dy, xhat, rstd, x, wp, pb, ps, gain, scatter=()):
    s_len = x.shape[0]
    ts = min(ROW_TILE, s_len)
    hb = ts // POOL_HALO
    n_t = s_len // ts
    ne = ts + POOL_HALO
    n_s = len(scatter)

    def body(*refs):
        dy_ref, dyn_ref, xh_ref, xhn_ref, rs_ref, rsn_ref, x_ref, xp_ref, w_ref, pb_ref, ps_ref, g_ref = refs[:12]
        dx_ref, dg_ref, db_ref, dps_ref, dpb_ref, dw_ref = refs[12 + n_s:18 + n_s]
        i = pl.program_id(0)
        if n_s:
            start, finish = _scatter_steps(refs[12:12 + n_s], refs[18 + n_s:18 + 2 * n_s], *refs[18 + 2 * n_s:])
            pl.when(i == 0)(start)
        more = i < n_t - 1
        dy_t, xh_t = dy_ref[...], xh_ref[...]
        dy_e = jnp.concatenate([dy_t, jnp.where(more, dyn_ref[...], 0.0)], axis=0)
        xh_e = jnp.concatenate([xh_t, xhn_ref[...]], axis=0)
        rs_e = jnp.concatenate([rs_ref[...], rsn_ref[...]], axis=0)
        dr_e = _ln_bwd(dy_e, xh_e, rs_e, g_ref[...])
        dyy_e = dr_e * ps_ref[...]
        pos_e = (i * ts + lax.broadcasted_iota(jnp.int32, (ne, 1), 0) + 1).astype(F32)
        dxs = []
        wg = _pool_groups_w(w_ref)
        for gi, win in enumerate(POOL_WINDOWS):
            sl = slice(gi * POOL_GROUP, (gi + 1) * POOL_GROUP)
            dpool = _bdot(dyy_e[:, sl], wg[gi], NT)
            s, k = dpool / jnp.minimum(pos_e, float(win)), 1
            while k < win:
                s = s + pltpu.roll(s, ne - k, 0)
                k *= 2
            dxs.append(s[:ts] - dpool[:ts])
        dx_ref[...] = ALPHA * dr_e[:ts] + jnp.concatenate(dxs, axis=1)

        x_t = x_ref[...]
        halo = jnp.where(i > 0, xp_ref[...], 0.0)
        pooled = _pooled_groups(jnp.concatenate([halo, x_t], axis=0), i * ts, ts)
        y = jnp.concatenate([_bdot(pooled[g], wg[g], NN) for g in range(4)], axis=1) + pb_ref[...]
        dr_t, dyy_t = dr_e[:ts], dyy_e[:ts]
        first = i == 0
        _acc(dg_ref, first, jnp.sum(dy_t * xh_t, axis=0, keepdims=True))
        _acc(db_ref, first, jnp.sum(dy_t, axis=0, keepdims=True))
        _acc(dps_ref, first, jnp.sum(dr_t * y, axis=0, keepdims=True))
        _acc(dpb_ref, first, jnp.sum(dyy_t, axis=0, keepdims=True))
        for g in range(4):
            _acc(dw_ref.at[g], first, _bdot(pooled[g], dyy_t[:, g * POOL_GROUP:(g + 1) * POOL_GROUP], TN))
        if n_s:
            pl.when(i == n_t - 1)(finish)

    tile = pl.BlockSpec((ts, D_MODEL), lambda i: (i, 0))
    nxt = pl.BlockSpec((POOL_HALO, D_MODEL), lambda i: (jnp.minimum((i + 1) * hb, n_t * hb - 1), 0))
    prv = pl.BlockSpec((POOL_HALO, D_MODEL), lambda i: (jnp.maximum(i * hb - 1, 0), 0))
    rs_t = pl.BlockSpec((ts, 1), lambda i: (i, 0))
    rs_n = pl.BlockSpec((POOL_HALO, 1), lambda i: (jnp.minimum((i + 1) * hb, n_t * hb - 1), 0))
    row = _row(D_MODEL)
    out = _call(
        body, "pool_bwd", (n_t,),
        [tile, nxt, tile, nxt, rs_t, rs_n, tile, prv, _full(wp.shape), row, row, row] + [ANY] * n_s,
        [tile, row, row, row, row, _full((4, POOL_GROUP, POOL_GROUP))] + [ANY] * n_s,
        [_sds((s_len, D_MODEL))] + [_sds((1, D_MODEL))] * 4 + [_sds((4, POOL_GROUP, POOL_GROUP))]
        + [_sds(t.shape, t.dtype) for t in scatter],
        scratch=_scatter_sems(n_s) if n_s else [], sem=("arbitrary",),
    )(dy, dy, xhat, xhat, rstd, rstd, x, x, wp, pb, ps, gain, *scatter)
    return out[:6], out[6:]


def _mlp_weight_specs():
    fc = D_FF // N_SHARD
    return [pl.BlockSpec((None, D_MODEL, fc), lambda i, j: (j, 0, 0)),
            pl.BlockSpec((None, fc, D_MODEL), lambda i, j: (j, 0, 0)),
            _full((N_SHARD, D_MODEL // N_SHARD, D_MODEL)),
            _row(D_MODEL),
            _full((N_SHARD, PLE_DIM, D_MODEL // N_SHARD))]


def _gate_w(gw_ref):
    return gw_ref[...].reshape(D_MODEL, D_MODEL)


def _ple_proj(pj_ref):
    return jnp.concatenate([pj_ref[s] for s in range(N_SHARD)], axis=1)


def _mlp_fwd(x, p, w1s, w2s, gw, gb, proj, gain, bias, name, gather=(), target=None):
    s_len = x.shape[0]
    ts = min(ROW_TILE, s_len)
    n_i = s_len // ts
    n_g = len(gather)
    has_t = target is not None
    n_in, n_out = 9 + has_t, 5 + has_t

    def body(*refs):
        x_ref, p_ref, w1_ref, w2_ref, gw_ref, gb_ref, pj_ref, g_ref, b_ref = refs[:9]
        y_ref, xhat_ref, rstd_ref, a_ref, xbo_ref = refs[n_in + n_g:n_in + n_g + 5]
        acc_ref, xb_ref = refs[n_in + n_out + 2 * n_g:n_in + n_out + 2 * n_g + 2]
        i, j = pl.program_id(0), pl.program_id(1)
        if n_g:
            start, finish = _gather_steps(refs[n_in + n_out + n_g:n_in + n_out + 2 * n_g], *refs[n_in + n_out + 2 * n_g + 2:])
            pl.when((i == 0) & (j == 0))(start)

        @pl.when(j == 0)
        def _():
            x_t = x_ref[...]
            xb_ref[...] = x_t.astype(BF16)
            xbo_ref[...] = x_t.astype(BF16)
            gate = _sigmoid(_bdot(x_t, _gate_w(gw_ref), NN) + gb_ref[...])
            acc_ref[...] = ALPHA * x_t + gate * _bdot(p_ref[...], _ple_proj(pj_ref), NN)

        h = jnp.maximum(_bdot(xb_ref[...], w1_ref[...], NN), 0.0)
        a = (h * h).astype(BF16)
        a_ref[...] = a
        acc_ref[...] += _bdot(a, w2_ref[...], NN)

        @pl.when(j == N_SHARD - 1)
        def _():
            y, xhat_ref[...], rstd_ref[...] = _ln_fwd(acc_ref[...], g_ref[...], b_ref[...])
            if has_t:
                err = y - refs[9][...]
                y_ref[...] = err * (1.0 / D_MODEL)
                part = 0.5 * jnp.sum(jnp.mean(err * err, axis=-1, keepdims=True))
                _acc(refs[n_in + n_g + 5], i == 0, part + jnp.zeros((1, 128), F32))
            else:
                y_ref[...] = y

        if n_g:
            pl.when((i == n_i - 1) & (j == N_SHARD - 1))(finish)

    tile = pl.BlockSpec((ts, D_MODEL), lambda i, j: (i, 0))
    row = _row(D_MODEL)
    out = _call(
        body, name, (n_i, N_SHARD),
        [tile, pl.BlockSpec((ts, PLE_DIM), lambda i, j: (i, 0))] + _mlp_weight_specs() + [row, row] + [tile] * has_t
        + [ANY] * n_g,
        [tile, tile, pl.BlockSpec((ts, 1), lambda i, j: (i, 0)), pl.BlockSpec((ts, D_FF // N_SHARD), lambda i, j: (i, j)), tile]
        + [_row(128)] * has_t + [ANY] * n_g,
        [_sds((s_len, D_MODEL)), _sds((s_len, D_MODEL)), _sds((s_len, 1)), _sds((s_len, D_FF), BF16), _sds((s_len, D_MODEL), BF16)]
        + [_sds((1, 128))] * has_t + [_sds(a.shape, a.dtype) for a in gather],
        scratch=[pltpu.VMEM((ts, D_MODEL), F32), pltpu.VMEM((ts, D_MODEL), BF16)] + (_gather_sems(n_g) if n_g else []),
        sem=("arbitrary", "arbitrary"), aliases={n_in + k: n_out + k for k in range(n_g)},
    )(x, p, w1s, w2s, gw, gb, proj, gain, bias, *([target] if has_t else []), *gather)
    return out[:n_out], out[n_out:]


def _mlp_bwd(dy, xhat, rstd, x, a, p, w1s, w2s, gw, gb, proj, gain, name, scatter=()):
    s_len = x.shape[0]
    ts = min(ROW_TILE, s_len)
    fc = D_FF // N_SHARD
    n_i = s_len // ts
    n_s = len(scatter)

    def body(*refs):
        dy_ref, xh_ref, rs_ref, x_ref, a_ref, p_ref, w1_ref, w2_ref, gw_ref, gb_ref, pj_ref, g_ref = refs[:12]
        dx_ref, dh_ref, dzg_ref, dpp_ref, drb_ref, dg_ref, db_ref, dgb_ref = refs[12 + n_s:20 + n_s]
        acc_ref, dr_ref = refs[20 + 2 * n_s:22 + 2 * n_s]
        i, j = pl.program_id(0), pl.program_id(1)
        if n_s:
            start, finish = _scatter_steps(refs[12:12 + n_s], refs[20 + n_s:20 + 2 * n_s], *refs[22 + 2 * n_s:])
            pl.when((i == 0) & (j == 0))(start)

        @pl.when(j == 0)
        def _():
            dy_t, xh_t, x_t = dy_ref[...], xh_ref[...], x_ref[...]
            dr = _ln_bwd(dy_t, xh_t, rs_ref[...], g_ref[...])
            drb = dr.astype(BF16)
            dr_ref[...] = drb
            drb_ref[...] = drb
            gw_full = _gate_w(gw_ref)
            gate = _sigmoid(_bdot(x_t, gw_full, NN) + gb_ref[...])
            pp = _bdot(p_ref[...], _ple_proj(pj_ref), NN)
            dzg = dr * pp * gate * (1.0 - gate)
            dzg_ref[...] = dzg.astype(BF16)
            dpp_ref[...] = (dr * gate).astype(BF16)
            acc_ref[...] = ALPHA * dr + _bdot(dzg, gw_full, NT)
            first = i == 0
            _acc(dg_ref, first, jnp.sum(dy_t * xh_t, axis=0, keepdims=True))
            _acc(db_ref, first, jnp.sum(dy_t, axis=0, keepdims=True))
            _acc(dgb_ref, first, jnp.sum(dzg, axis=0, keepdims=True))

        dh = (_bdot(dr_ref[...], w2_ref[...], NT) * (2.0 * jnp.sqrt(a_ref[...].astype(F32)))).astype(BF16)
        dh_ref[...] = dh
        acc_ref[...] += _bdot(dh, w1_ref[...], NT)

        @pl.when(j == N_SHARD - 1)
        def _():
            dx_ref[...] = acc_ref[...]

        if n_s:
            pl.when((i == n_i - 1) & (j == N_SHARD - 1))(finish)

    tile = pl.BlockSpec((ts, D_MODEL), lambda i, j: (i, 0))
    ftile = pl.BlockSpec((ts, fc), lambda i, j: (i, j))
    row = _row(D_MODEL)
    out = _call(
        body, name, (n_i, N_SHARD),
        [tile, tile, pl.BlockSpec((ts, 1), lambda i, j: (i, 0)), tile, ftile, pl.BlockSpec((ts, PLE_DIM), lambda i, j: (i, 0))]
        + _mlp_weight_specs() + [row] + [ANY] * n_s,
        [tile, ftile, tile, tile, tile, row, row, row] + [ANY] * n_s,
        [_sds((s_len, D_MODEL)), _sds((s_len, D_FF), BF16)]
        + [_sds((s_len, D_MODEL), BF16)] * 3 + [_sds((1, D_MODEL))] * 3 + [_sds(t.shape, t.dtype) for t in scatter],
        scratch=[pltpu.VMEM((ts, D_MODEL), F32), pltpu.VMEM((ts, D_MODEL), BF16)] + (_scatter_sems(n_s) if n_s else []),
        sem=("arbitrary", "arbitrary"))(dy, xhat, rstd, x, a, p, w1s, w2s, gw, gb, proj, gain, *scatter)
    return out[:8], out[8:]


def _wgrad(a, b, name, stack_cols=False):
    s_len, m = a.shape
    n = b.shape[1]
    ts = min(2048 if a.dtype == BF16 and b.dtype == BF16 else 1024, s_len)
    tm = min(m, 1024)
    tn = n // N_SHARD if stack_cols else (1408 if n == GDN_IN_PAD else min(n, 1024))
    n_s = s_len // ts

    def body(a_ref, b_ref, o_ref):
        _acc(o_ref, pl.program_id(2) == 0, _bdot(a_ref[...], b_ref[...], TN))

    if stack_cols:
        out_spec = pl.BlockSpec((None, tm, tn), lambda mi, nj, s: (nj, mi, 0))
        out_shape = _sds((N_SHARD, m, tn))
    else:
        out_spec = pl.BlockSpec((tm, tn), lambda mi, nj, s: (mi, nj))
        out_shape = _sds((m, n))
    return _call(
        body, name, (m // tm, n // tn, n_s),
        [pl.BlockSpec((ts, tm), lambda mi, nj, s: (s, mi)), pl.BlockSpec((ts, tn), lambda mi, nj, s: (s, nj))],
        out_spec, out_shape, sem=("parallel", "parallel", "arbitrary"))(a, b)


def _act_qkv(y):
    qkv = _silu(y)
    qs, ks = [], []
    for h in range(HEADS):
        qh = qkv[:, h * HEAD_DIM:(h + 1) * HEAD_DIM]
        kh = qkv[:, D_MODEL + h * HEAD_DIM:D_MODEL + (h + 1) * HEAD_DIM]
        qs.append(qh * (lax.rsqrt(jnp.sum(qh * qh, axis=-1, keepdims=True) + L2_EPS) * HEAD_DIM ** -0.5))
        ks.append(kh * lax.rsqrt(jnp.sum(kh * kh, axis=-1, keepdims=True) + L2_EPS))
    return jnp.concatenate(qs, axis=1), jnp.concatenate(ks, axis=1), qkv[:, 2 * D_MODEL:]


def _act_gb(ba, alog_l, dtb_l, tril):
    lane = lax.broadcasted_iota(jnp.int32, ba.shape, 1)
    g = jnp.where((lane >= HEADS) & (lane < 2 * HEADS), -jnp.exp(alog_l) * _softplus(ba + dtb_l), 0.0)
    return jnp.where(lane < HEADS, _sigmoid(ba), _hdot(tril, g, NN))


def _chunk_tril(t):
    ii = lax.broadcasted_iota(jnp.int32, (t, t), 0)
    jj = lax.broadcasted_iota(jnp.int32, (t, t), 1)
    return ((ii // CHUNK == jj // CHUNK) & (ii >= jj)).astype(F32)


def _conv_rows(xe, w, n_rows):
    y = xe[CONV_HALO:CONV_HALO + n_rows] * w[CONV_WIDTH - 1]
    for j in range(CONV_WIDTH - 1):
        y = y + pltpu.roll(xe, CONV_WIDTH - 1 - j, 0)[CONV_HALO:CONV_HALO + n_rows] * w[j]
    return y


def _conv_fwd(x, w_in, conv_w, alog_l, dtb_l):
    s_len = x.shape[0]
    ts = min(CONV_TILE, s_len)
    hb = ts // CONV_HALO

    def body(x_ref, xp_ref, w_ref, al_ref, dt_ref, win_hbm, proj_ref, y_ref, q_ref, k_ref, v_ref, gcb_ref, win_ref, win_sem):
        i = pl.program_id(0)

        @pl.when(i == 0)
        def _():
            cp = pltpu.make_async_copy(win_hbm, win_ref, win_sem)
            cp.start()
            cp.wait()

        proj = _bdot(x_ref[...], win_ref[...], NN)
        proj_ref[...] = proj
        halo = jnp.where(i > 0, _bdot(xp_ref[...], win_ref[:, :QKV_DIM], NN), 0.0)
        taps = [w_ref[pl.ds(j, 1), :] for j in range(CONV_WIDTH)]
        y = _conv_rows(jnp.concatenate([halo, proj[:, :QKV_DIM]], axis=0), taps, ts)
        y_ref[...] = y
        q_ref[...], k_ref[...], v_ref[...] = _act_qkv(y)
        gcb_ref[...] = _act_gb(proj[:, BA_BLOCK * 128:], al_ref[...], dt_ref[...], _chunk_tril(ts))

    tile = pl.BlockSpec((ts, D_MODEL), lambda i: (i, 0))
    return _call(
        body, "gdn_conv_fwd", (s_len // ts,),
        [tile, pl.BlockSpec((CONV_HALO, D_MODEL), lambda i: (jnp.maximum(i * hb - 1, 0), 0)),
         _full((CONV_WIDTH, QKV_DIM)), _row(128), _row(128), ANY],
        [pl.BlockSpec((ts, GDN_IN_PAD), lambda i: (i, 0)), pl.BlockSpec((ts, QKV_DIM), lambda i: (i, 0)), tile, tile, tile,
         pl.BlockSpec((ts, 128), lambda i: (i, 0))],
        [_sds((s_len, GDN_IN_PAD)), _sds((s_len, QKV_DIM))] + [_sds((s_len, D_MODEL))] * 3 + [_sds((s_len, 128))],
        scratch=[pltpu.VMEM(w_in.shape, w_in.dtype), pltpu.SemaphoreType.DMA],
        sem=("arbitrary",))(x, x, conv_w, alog_l, dtb_l, w_in)


def _conv_bwd(proj, y, dq, dk, dv, dgcb, dz, conv_w, alog_l, dtb_l, w_in, dres):
    s_len = proj.shape[0]
    ts = min(CONV_TILE, s_len)
    hb = ts // CONV_HALO
    n_t = s_len // ts
    te = ts + CONV_HALO

    def body(x_ref, y_ref, yn_ref, ba_ref, dq_ref, dqn_ref, dk_ref, dkn_ref, dv_ref, dvn_ref, dgcb_ref, dz_ref,
             w_ref, al_ref, dt_ref, win_hbm, dres_ref, dp_ref, dx_ref, dw_ref, dal_ref, ddt_ref, win_ref, win_sem):
        i = pl.program_id(0)

        @pl.when(i == 0)
        def _():
            cp = pltpu.make_async_copy(win_hbm, win_ref, win_sem)
            cp.start()
            cp.wait()

        more = i < n_t - 1
        w = [w_ref[pl.ds(j, 1), :] for j in range(CONV_WIDTH)]
        x_t = x_ref[...]
        _, act_vjp = jax.vjp(_act_qkv, jnp.concatenate([y_ref[...], yn_ref[...]], axis=0))
        ct = tuple(jnp.concatenate([t[...], jnp.where(more, n[...], 0.0)], axis=0)
                   for t, n in ((dq_ref, dqn_ref), (dk_ref, dkn_ref), (dv_ref, dvn_ref)))
        (dy_e,) = act_vjp(ct)
        ahead = [pltpu.roll(dy_e, te - (CONV_WIDTH - 1 - j), 0)[:ts] for j in range(CONV_WIDTH - 1)] + [dy_e[:ts]]
        dx = ahead[CONV_WIDTH - 1] * w[CONV_WIDTH - 1]
        for j in range(CONV_WIDTH - 1):
            dx = dx + ahead[j] * w[j]
        dws = [jnp.sum(ahead[j] * x_t, axis=0, keepdims=True) for j in range(CONV_WIDTH)]
        _, gb_vjp = jax.vjp(lambda ba, al, dt: _act_gb(ba, al, dt, _chunk_tril(ts)), ba_ref[...], al_ref[...], dt_ref[...])
        dba, dal, ddt = gb_vjp(dgcb_ref[...])
        dp = jnp.concatenate([dx.astype(BF16), dz_ref[...], dba.astype(BF16)], axis=1)
        dp_ref[...] = dp
        dx_ref[...] = dres_ref[...] + _bdot(dp, win_ref[...], NT)
        first = i == 0
        for j in range(CONV_WIDTH):
            _acc(dw_ref.at[pl.ds(j, 1), :], first, dws[j])
        _acc(dal_ref, first, dal)
        _acc(ddt_ref, first, ddt)

    tile = pl.BlockSpec((ts, D_MODEL), lambda i: (i, 0))
    nxt = pl.BlockSpec((CONV_HALO, D_MODEL), lambda i: (jnp.minimum((i + 1) * hb, n_t * hb - 1), 0))
    return _call(
        body, "gdn_conv_bwd", (n_t,),
        [pl.BlockSpec((ts, QKV_DIM), lambda i: (i, 0)), pl.BlockSpec((ts, QKV_DIM), lambda i: (i, 0)),
         pl.BlockSpec((CONV_HALO, QKV_DIM), lambda i: (jnp.minimum((i + 1) * hb, n_t * hb - 1), 0)),
         pl.BlockSpec((ts, 128), lambda i: (i, BA_BLOCK)),
         tile, nxt, tile, nxt, tile, nxt, pl.BlockSpec((ts, 128), lambda i: (i, 0)), tile,
         _full((CONV_WIDTH, QKV_DIM)), _row(128), _row(128), ANY, tile],
        [pl.BlockSpec((ts, GDN_IN_PAD), lambda i: (i, 0)), tile, _full((CONV_WIDTH, QKV_DIM)), _row(128), _row(128)],
        [_sds((s_len, GDN_IN_PAD), BF16), _sds((s_len, D_MODEL)), _sds((CONV_WIDTH, QKV_DIM)), _sds((1, 128)), _sds((1, 128))],
        scratch=[pltpu.VMEM(w_in.shape, w_in.dtype), pltpu.SemaphoreType.DMA], sem=("arbitrary",),
    )(proj, y, y, proj, dq, dq, dk, dk, dv, dv, dgcb, dz, conv_w, alog_l, dtb_l, w_in, dres)


def _tri_inv(a_strict):
    ii = lax.broadcasted_iota(jnp.int32, (CHUNK, CHUNK), 0)
    jj = lax.broadcasted_iota(jnp.int32, (CHUNK, CHUNK), 1)
    x = (ii == jj).astype(F32) - a_strict
    pw = _bdot(a_strict, a_strict, BNN)
    for step in range(5):
        x = x + _bdot(x, pw, BNN)
        if step < 4:
            pw = _bdot(pw, pw, BNN)
    return x


@jax.custom_vjp
def _solved(a_strict, rhs, t, sol):
    return sol


def _solved_fwd(a_strict, rhs, t, sol):
    return sol, (t, sol)


def _solved_bwd(res, dsol):
    t, sol = res
    drhs = _mdot(t, dsol, BTN)
    return -_mdot(drhs, sol, BNT), drhs, jnp.zeros_like(t), jnp.zeros_like(sol)


_solved.defvjp(_solved_fwd, _solved_bwd)


def _prep(q, k, v, gc, beta, solve):
    ii = lax.broadcasted_iota(jnp.int32, (CHUNK, CHUNK), 0)
    jj = lax.broadcasted_iota(jnp.int32, (CHUNK, CHUNK), 1)
    causal, strict = ii >= jj, ii > jj
    gc_row = jnp.sum((ii == jj).astype(F32) * gc, axis=1, keepdims=True)
    decay = jnp.where(causal, jnp.exp(jnp.where(causal, gc - gc_row, 0.0)), 0.0)
    kb = k * beta
    a = jnp.where(strict, _bdot(kb, k, BNT) * decay, 0.0)
    eg = jnp.exp(gc)
    sol = solve(a, jnp.concatenate([v * beta, kb * eg], axis=-1))
    qk = _bdot(q, k, BNT) * decay
    last = lax.broadcasted_iota(jnp.int32, (CHUNK, 1), 0) == CHUNK - 1
    g_last = jnp.sum(jnp.where(last, gc, 0.0), axis=1, keepdims=True)
    kd = k * jnp.exp(g_last - gc)
    gl = jnp.exp(g_last) + jnp.zeros((1, 1, HEAD_DIM), F32)
    return sol[..., :HEAD_DIM], sol[..., HEAD_DIM:], qk, q * eg, kd, gl


def _prep_specs(s_len, chunks):
    gl_m = min(PREP_CHUNKS, s_len // CHUNK)
    m = min(chunks, gl_m)
    rows = m * CHUNK
    per = gl_m // m
    hd = pl.BlockSpec((rows, HEAD_DIM), lambda c, h: (c, h))
    gcb = pl.BlockSpec((rows, 128), lambda c, h: (c, 0))
    qk = pl.BlockSpec((None, rows, CHUNK), lambda c, h: (h, c, 0))
    gl = pl.BlockSpec((None, m, HEADS, HEAD_DIM), lambda c, h: (c // per, c % per, 0, 0))
    return rows, m, hd, gcb, qk, gl


def _head_cols(gcb, h, m):
    lane = lax.broadcasted_iota(jnp.int32, gcb.shape, 1)
    pick = lambda at: jnp.sum(jnp.where(lane == at, gcb, 0.0), axis=1, keepdims=True).reshape(m, CHUNK, 1)
    return pick(h + HEADS), pick(h)


def _gdn_prep(q, k, v, gcb):
    s_len = q.shape[0]
    rows, m, hd, gcb_spec, qk_spec, gl_spec = _prep_specs(s_len, PREP_CHUNKS)

    def body(q_ref, k_ref, v_ref, gcb_ref, u_ref, w_ref, qd_ref, kd_ref, qk_ref, gl_ref, t_ref):
        r3 = lambda ref, d: ref[...].reshape(m, CHUNK, d)
        gc, beta = _head_cols(gcb_ref[...], pl.program_id(1), m)

        def solve(a, rhs):
            t = _tri_inv(a)
            t_ref[...] = t.reshape(rows, CHUNK)
            return _mdot(t, rhs, BNN)

        u, w, qk, qd, kd, gl = _prep(r3(q_ref, HEAD_DIM), r3(k_ref, HEAD_DIM), r3(v_ref, HEAD_DIM), gc, beta, solve)
        u_ref[...] = u.reshape(rows, HEAD_DIM)
        w_ref[...] = w.reshape(rows, HEAD_DIM)
        qd_ref[...] = qd.reshape(rows, HEAD_DIM).astype(BF16)
        kd_ref[...] = kd.reshape(rows, HEAD_DIM).astype(BF16)
        qk_ref[...] = qk.reshape(rows, CHUNK).astype(BF16)
        gl_ref[:, pl.ds(pl.program_id(1), 1), :] = gl

    n_g = s_len // rows
    return _call(
        body, "gdn_prep", (n_g, HEADS), [hd, hd, hd, gcb_spec], [hd, hd, hd, hd, qk_spec, gl_spec, qk_spec],
        [_sds((s_len, D_MODEL))] * 2 + [_sds((s_len, D_MODEL), BF16)] * 2
        + [_sds((HEADS, s_len, CHUNK), BF16), _sds((n_g, m, HEADS, HEAD_DIM)), _sds((HEADS, s_len, CHUNK))],
        sem=("parallel", "arbitrary"))(q, k, v, gcb)


def _gdn_prep_bwd(q, k, v, gcb, t_inv, u, w, du, dw, dqd, dkd, dqk, dgl):
    s_len = q.shape[0]
    rows, m, hd, gcb_spec, qk_spec, gl_spec = _prep_specs(s_len, PREP_BWD_CHUNKS)

    def body(q_ref, k_ref, v_ref, gcb_ref, t_ref, u_ref, w_ref, du_ref, dw_ref, dqd_ref, dkd_ref, dqk_ref, dgl_ref,
             dq_ref, dk_ref, dv_ref, dgcb_ref):
        h = pl.program_id(1)
        r3 = lambda ref, d: ref[...].reshape(m, CHUNK, d)
        gc, beta = _head_cols(gcb_ref[...], h, m)
        t = r3(t_ref, CHUNK)
        sol = jnp.concatenate([r3(u_ref, HEAD_DIM), r3(w_ref, HEAD_DIM)], axis=-1)
        fn = lambda q_, k_, v_, gc_, bt_: _prep(q_, k_, v_, gc_, bt_, lambda a, rhs: _solved(a, rhs, t, sol))
        _, vjp = jax.vjp(fn, r3(q_ref, HEAD_DIM), r3(k_ref, HEAD_DIM), r3(v_ref, HEAD_DIM), gc, beta)
        ct = (r3(du_ref, HEAD_DIM), r3(dw_ref, HEAD_DIM), r3(dqk_ref, CHUNK), r3(dqd_ref, HEAD_DIM), r3(dkd_ref, HEAD_DIM),
              dgl_ref[:, pl.ds(h, 1), :] * (1.0 / HEAD_DIM))
        dq, dk, dv, dgc, dbt = vjp(ct)
        dq_ref[...] = dq.reshape(rows, HEAD_DIM)
        dk_ref[...] = dk.reshape(rows, HEAD_DIM)
        dv_ref[...] = dv.reshape(rows, HEAD_DIM)
        lane = lax.broadcasted_iota(jnp.int32, (rows, 128), 1)
        mine = jnp.where(lane == h, dbt.reshape(rows, 1), 0.0) + jnp.where(lane == h + HEADS, dgc.reshape(rows, 1), 0.0)
        _acc(dgcb_ref, h == 0, mine)

    return _call(
        body, "gdn_prep_bwd", (s_len // rows, HEADS),
        [hd, hd, hd, gcb_spec, qk_spec, hd, hd, hd, hd, hd, hd, qk_spec, gl_spec], [hd, hd, hd, gcb_spec],
        [_sds((s_len, D_MODEL))] * 3 + [_sds((s_len, 128))],
        sem=("parallel", "arbitrary"))(q, k, v, gcb, t_inv, u, w, du, dw, dqd, dkd, dqk, dgl)


def _scan_specs(n_c, m, k, reverse):
    n_b = n_c // k
    at = (lambda n: n_b - 1 - n) if reverse else (lambda n: n)
    row = pl.BlockSpec((k * CHUNK, D_MODEL), lambda n: (at(n), 0))
    qk = pl.BlockSpec((HEADS, k * CHUNK, CHUNK), lambda n: (0, at(n), 0))
    gl = pl.BlockSpec((None, k, HEADS, HEAD_DIM), lambda n: (at(n) // (m // k), at(n) % (m // k), 0, 0))
    st = pl.BlockSpec((k, HEADS, HEAD_DIM, HEAD_DIM), lambda n: (at(n), 0, 0, 0))
    return row, qk, gl, st


def _gdn_scan(u, w, qd, kd, qk, gl):
    s_len = u.shape[0]
    n_c = s_len // CHUNK
    k = min(SCAN_CHUNKS, gl.shape[1])
    row, qk_spec, gl_spec, st_spec = _scan_specs(n_c, gl.shape[1], k, False)

    def body(u_ref, w_ref, qd_ref, kd_ref, qk_ref, gl_ref, o_ref, st_ref, state):
        hs = range(HEADS)
        sl = [slice(h * HEAD_DIM, (h + 1) * HEAD_DIM) for h in hs]
        first = pl.program_id(0) == 0
        s_all = [jnp.where(first, 0.0, state[h]) for h in hs]
        for c in range(k):
            rows = slice(c * CHUNK, (c + 1) * CHUNK)
            s_b = [s.astype(BF16) for s in s_all]
            ws = [_bdot(w_ref[rows, sl[h]], s_b[h], NN) for h in hs]
            qs = [_bdot(qd_ref[rows, sl[h]], s_b[h], NN) for h in hs]
            vn = [(u_ref[rows, sl[h]] - ws[h]).astype(BF16) for h in hs]
            outs = [qs[h] + _bdot(qk_ref[h, rows, :], vn[h], NN) for h in hs]
            nxt = [s_all[h] * gl_ref[c, pl.ds(h, 1), :] + _bdot(kd_ref[rows, sl[h]], vn[h], TN) for h in hs]
            for h in hs:
                st_ref[c, h] = s_all[h]
                o_ref[rows, sl[h]] = outs[h]
            s_all = nxt
        for h in hs:
            state[h] = s_all[h]

    return _call(
        body, "gdn_scan", (n_c // k,), [row, row, row, row, qk_spec, gl_spec], [row, st_spec],
        [_sds((s_len, D_MODEL)), _sds((n_c, HEADS, HEAD_DIM, HEAD_DIM))],
        scratch=[pltpu.VMEM((HEADS, HEAD_DIM, HEAD_DIM), F32)], sem=("arbitrary",))(u, w, qd, kd, qk, gl)


def _gdn_scan_bwd(do, u, w, qd, kd, qk, gl, states):
    s_len = u.shape[0]
    n_c = s_len // CHUNK
    k = min(SCAN_CHUNKS, gl.shape[1])
    row, qk_spec, gl_spec, st_spec = _scan_specs(n_c, gl.shape[1], k, True)

    def body(do_ref, u_ref, w_ref, qd_ref, kd_ref, qk_ref, gl_ref, st_ref,
             du_ref, dw_ref, dqd_ref, dkd_ref, dqk_ref, dgl_ref, dstate):
        hs = range(HEADS)
        sl = [slice(h * HEAD_DIM, (h + 1) * HEAD_DIM) for h in hs]
        first = pl.program_id(0) == 0
        ds_f = [jnp.where(first, 0.0, dstate[h]) for h in hs]
        for c in reversed(range(k)):
            rows = slice(c * CHUNK, (c + 1) * CHUNK)
            s_f = [st_ref[c, h] for h in hs]
            s_b = [s.astype(BF16) for s in s_f]
            ds_b = [d.astype(BF16) for d in ds_f]
            do_b = [do_ref[rows, sl[h]].astype(BF16) for h in hs]
            w_b = [w_ref[rows, sl[h]].astype(BF16) for h in hs]
            ws = [_bdot(w_b[h], s_b[h], NN) for h in hs]
            dvn = [_bdot(qk_ref[h, rows, :], do_b[h], TN) + _bdot(kd_ref[rows, sl[h]], ds_b[h], NN) for h in hs]
            dqd = [_bdot(do_b[h], s_b[h], NT) for h in hs]
            t_do = [_bdot(qd_ref[rows, sl[h]], do_b[h], TN) for h in hs]
            vn = [(u_ref[rows, sl[h]] - ws[h]).astype(BF16) for h in hs]
            dvn_b = [d.astype(BF16) for d in dvn]
            dw = [-_bdot(dvn_b[h], s_b[h], NT) for h in hs]
            dkd = [_bdot(vn[h], ds_b[h], NT) for h in hs]
            dqk = [_bdot(do_b[h], vn[h], NT) for h in hs]
            t_dv = [_bdot(w_b[h], dvn_b[h], TN) for h in hs]
            for h in hs:
                du_ref[rows, sl[h]], dw_ref[rows, sl[h]], dqd_ref[rows, sl[h]], dkd_ref[rows, sl[h]] = dvn[h], dw[h], dqd[h], dkd[h]
                dqk_ref[h, rows, :] = dqk[h]
                dgl_ref[c, pl.ds(h, 1), :] = jnp.sum(s_f[h] * ds_f[h]) + jnp.zeros((1, HEAD_DIM), F32)
            ds_f = [ds_f[h] * gl_ref[c, pl.ds(h, 1), :] + t_do[h] - t_dv[h] for h in hs]
        for h in hs:
            dstate[h] = ds_f[h]

    return _call(
        body, "gdn_scan_bwd", (n_c // k,), [row, row, row, row, row, qk_spec, gl_spec, st_spec],
        [row, row, row, row, qk_spec, gl_spec],
        [_sds((s_len, D_MODEL))] * 4 + [_sds((HEADS, s_len, CHUNK)), _sds(gl.shape)],
        scratch=[pltpu.VMEM((HEADS, HEAD_DIM, HEAD_DIM), F32)], sem=("arbitrary",))(do, u, w, qd, kd, qk, gl, states)


def _gate_norm(o, z, nw):
    outs = []
    for h in range(HEADS):
        oh = o[:, h * HEAD_DIM:(h + 1) * HEAD_DIM]
        outs.append(oh * lax.rsqrt(jnp.mean(oh * oh, axis=-1, keepdims=True) + RMS_EPS))
    return jnp.concatenate(outs, axis=1) * nw * _silu(z)


def _gdn_out_fwd(o, proj, x, w_out, nw, gain, bias):
    s_len = x.shape[0]
    ts = min(ROW_TILE, s_len)

    def body(o_ref, z_ref, x_ref, w_ref, nw_ref, g_ref, b_ref, y_ref, xhat_ref, rstd_ref):
        on = _gate_norm(o_ref[...], z_ref[...], nw_ref[...])
        r = ALPHA * x_ref[...] + _bdot(on, w_ref[...], NN)
        y_ref[...], xhat_ref[...], rstd_ref[...] = _ln_fwd(r, g_ref[...], b_ref[...])

    tile = pl.BlockSpec((ts, D_MODEL), lambda i: (i, 0))
    row = _row(D_MODEL)
    return _call(
        body, "gdn_out_fwd", (s_len // ts,),
        [tile, pl.BlockSpec((ts, D_MODEL), lambda i: (i, QKV_DIM // D_MODEL)), tile, _full((D_MODEL, D_MODEL)), row, row, row],
        [tile, tile, pl.BlockSpec((ts, 1), lambda i: (i, 0))],
        [_sds((s_len, D_MODEL)), _sds((s_len, D_MODEL)), _sds((s_len, 1))], sem=("parallel",))(o, proj, x, w_out, nw, gain, bias)


def _gdn_out_bwd(dy, xhat, rstd, o, proj, w_out, nw, gain):
    s_len = o.shape[0]
    ts = min(ROW_TILE, s_len)

    def body(dy_ref, xh_ref, rs_ref, o_ref, z_ref, w_ref, nw_ref, g_ref,
             dres_ref, do_ref, dz_ref, on_ref, drb_ref, dg_ref, db_ref, dnw_ref):
        dy_t, xh_t = dy_ref[...], xh_ref[...]
        dr = _ln_bwd(dy_t, xh_t, rs_ref[...], g_ref[...])
        dres_ref[...] = ALPHA * dr
        drb_ref[...] = dr.astype(BF16)
        on, vjp = jax.vjp(_gate_norm, o_ref[...], z_ref[...], nw_ref[...])
        on_ref[...] = on.astype(BF16)
        do, dz, dnw = vjp(_bdot(dr, w_ref[...], NT))
        do_ref[...] = do
        dz_ref[...] = dz.astype(BF16)
        first = pl.program_id(0) == 0
        _acc(dg_ref, first, jnp.sum(dy_t * xh_t, axis=0, keepdims=True))
        _acc(db_ref, first, jnp.sum(dy_t, axis=0, keepdims=True))
        _acc(dnw_ref, first, sum(dnw[:, h * HEAD_DIM:(h + 1) * HEAD_DIM] for h in range(HEADS)))

    tile = pl.BlockSpec((ts, D_MODEL), lambda i: (i, 0))
    row = _row(D_MODEL)
    return _call(
        body, "gdn_out_bwd", (s_len // ts,),
        [tile, tile, pl.BlockSpec((ts, 1), lambda i: (i, 0)), tile,
         pl.BlockSpec((ts, D_MODEL), lambda i: (i, QKV_DIM // D_MODEL)), _full((D_MODEL, D_MODEL)), row, row],
        [tile, tile, tile, tile, tile, row, row, _row(HEAD_DIM)],
        [_sds((s_len, D_MODEL))] * 2 + [_sds((s_len, D_MODEL), BF16)] * 3 + [_sds((1, D_MODEL))] * 2 + [_sds((1, HEAD_DIM))],
        sem=("arbitrary",))(dy, xhat, rstd, o, proj, w_out, nw, gain)


def _adamw(w, g, m, v, name):
    r, c = w.shape
    tr = min(ADAM_ROWS, r)
    pieces = list(g) if isinstance(g, (list, tuple)) else [g]
    n_p = len(pieces)
    blocks = [a.shape[0] // tr for a in pieces]
    first = [sum(blocks[:k]) for k in range(n_p)]

    def body(*refs):
        w_ref, g_refs, (m_ref, v_ref) = refs[0], refs[1:1 + n_p], refs[1 + n_p:3 + n_p]
        go_ref, d_ref, nm_ref, nv_ref = refs[3 + n_p:]
        i = pl.program_id(0)
        g_t = g_refs[0][...]
        for k in range(1, n_p):
            g_t = jnp.where(i >= first[k], g_refs[k][...], g_t)
        go_ref[...] = g_t
        nm = ADAM_B1 * m_ref[...] + (1.0 - ADAM_B1) * g_t
        nv = ADAM_B2 * v_ref[...] + (1.0 - ADAM_B2) * (g_t * g_t)
        m_hat = nm / (1.0 - ADAM_B1 ** ADAM_STEP)
        v_hat = nv / (1.0 - ADAM_B2 ** ADAM_STEP)
        d_ref[...] = -ADAM_LR * (m_hat / (jnp.sqrt(v_hat) + ADAM_EPS) + ADAM_WD * w_ref[...])
        nm_ref[...] = nm
        nv_ref[...] = nv

    tile = pl.BlockSpec((tr, c), lambda i: (i, 0))
    g_specs = [pl.BlockSpec((tr, c), lambda i, k=k: (jnp.clip(i - first[k], 0, blocks[k] - 1), 0)) for k in range(n_p)]
    return _call(body, name, (r // tr,), [tile] + g_specs + [tile] * 2, [tile] * 4, [_sds((r, c))] * 4,
                 sem=("parallel",))(w, *pieces, m, v)


def _assemble_w_in(shards):
    rows = 256
    width = GDN_IN_DIM // N_SHARD

    def body(s_ref, o_ref):
        pad = jnp.zeros((rows, GDN_IN_PAD - GDN_IN_DIM), shards.dtype)
        o_ref[...] = jnp.concatenate([s_ref[j] for j in range(N_SHARD)] + [pad], axis=1)

    return _call(body, "w_in_assemble", (D_MODEL // rows,), [pl.BlockSpec((N_SHARD, rows, width), lambda i: (0, i, 0))],
                 pl.BlockSpec((rows, GDN_IN_PAD), lambda i: (i, 0)), _sds((D_MODEL, GDN_IN_PAD), shards.dtype),
                 sem=("parallel",))(shards)


def _split_w_in(full):
    rows = 256
    width = GDN_IN_DIM // N_SHARD

    def body(f_ref, o_ref):
        f = f_ref[...]
        for j in range(N_SHARD):
            o_ref[j] = f[:, j * width:(j + 1) * width]

    return _call(body, "w_in_split", (D_MODEL // rows,), [pl.BlockSpec((rows, GDN_IN_PAD), lambda i: (i, 0))],
                 pl.BlockSpec((N_SHARD, rows, width), lambda i: (0, i, 0)), _sds((N_SHARD, D_MODEL, width), full.dtype),
                 sem=("parallel",))(full)


def _place():
    x, y, c = lax.axis_index("x"), lax.axis_index("y"), lax.axis_index("c")
    return x, y, c, [(1 - x, y), (x, 1 - y), (1 - x, 1 - y)]


def _row_tile(rows):
    return max(t for t in range(8, min(rows, 640) + 1, 8) if rows % t == 0)


def _place_shard(part, me, dtype, name, layer=0):
    _, _, r, c = part.shape
    tr = _row_tile(r)

    def body(me_ref, p_ref, o_ref):
        o_ref[...] = p_ref[...].astype(dtype)

    return pl.pallas_call(
        body, name=name, out_shape=_sds((N_SHARD, 2, r, c), dtype),
        grid_spec=pltpu.PrefetchScalarGridSpec(
            num_scalar_prefetch=1, grid=(2, r // tr),
            in_specs=[pl.BlockSpec((None, None, tr, c), lambda h, i, me_ref: (layer, h, i, 0))],
            out_specs=pl.BlockSpec((None, None, tr, c), lambda h, i, me_ref: (me_ref[0], h, i, 0))))(me, part)


def _gather_sems(n):
    return [pltpu.SemaphoreType.DMA((6 * n,)), pltpu.SemaphoreType.DMA((6 * n,))]


def _gather_steps(dsts, send_sems, recv_sems):
    n = len(dsts)
    x, y, c, chips = _place()
    me = 2 * x + y
    sibling = (x, y, 1 - c)

    def ici(k, j, slot):
        px, py = chips[j]
        view = dsts[k].at[slot, c]
        return pltpu.make_async_remote_copy(
            src_ref=view, dst_ref=view, send_sem=send_sems.at[6 * k + j],
            recv_sem=recv_sems.at[6 * k + j], device_id=(px, py, c), device_id_type=MESH)

    def d2d(k, j, half):
        px, py = chips[j]
        view = dsts[k].at[2 * px + py, half]
        return pltpu.make_async_remote_copy(
            src_ref=view, dst_ref=view, send_sem=send_sems.at[6 * k + 3 + j], recv_sem=recv_sems.at[6 * k + 3 + j],
            device_id=sibling, device_id_type=MESH)

    def start():
        for k in range(n):
            for j in range(3):
                ici(k, j, me).start()

    def finish():
        fwds = []
        for k in range(n):
            for j, (px, py) in enumerate(chips):
                ici(k, j, 2 * px + py).wait_recv()
                fwds.append(d2d(k, j, c))
                fwds[-1].start()
        for k in range(n):
            for j in range(3):
                d2d(k, j, 1 - c).wait_recv()
        for k in range(n):
            for j in range(3):
                ici(k, j, me).wait_send()
        for cp in fwds:
            cp.wait_send()

    return start, finish


def _all_gather(bufs, name):
    n = len(bufs)

    def body(*refs):
        start, finish = _gather_steps(refs[n:2 * n], *refs[2 * n:])
        start()
        finish()

    return pl.pallas_call(
        body, name=name, out_shape=[_sds(a.shape, a.dtype) for a in bufs],
        in_specs=[ANY] * n, out_specs=[ANY] * n, input_output_aliases={k: k for k in range(n)},
        scratch_shapes=_gather_sems(n))(*bufs)


def _swap_halves(pieces, name):
    n = len(pieces)

    def body(*refs):
        srcs, dsts = refs[:n], refs[n:2 * n]
        send_sems, recv_sems = refs[2 * n:]
        x, y, c, _ = _place()
        copies = []
        for k in range(n):
            hr = pieces[k].shape[1] // 2
            copies.append(pltpu.make_async_remote_copy(
                src_ref=srcs[k].at[:, pl.ds((1 - c) * hr, hr), :], dst_ref=dsts[k],
                send_sem=send_sems.at[k], recv_sem=recv_sems.at[k], device_id=(x, y, 1 - c), device_id_type=MESH))
        for cp in copies:
            cp.start()
        for cp in copies:
            cp.wait()

    return pl.pallas_call(
        body, name=name, out_shape=[_sds((N_SHARD, a.shape[1] // 2, a.shape[2])) for a in pieces],
        in_specs=[ANY] * n, out_specs=[ANY] * n,
        scratch_shapes=[pltpu.SemaphoreType.DMA((n,)), pltpu.SemaphoreType.DMA((n,))])(*pieces)


def _add_half(piece, other, place, dtype, name):
    n, hr, cols = other.shape
    tr = _row_tile(hr)

    def body(pl_ref, a_ref, b_ref, o_ref):
        o_ref[...] = (a_ref[...] + b_ref[...]).astype(dtype)

    tile = pl.BlockSpec((None, tr, cols), lambda s, i, pl_ref: (s, i, 0))
    return pl.pallas_call(
        body, name=name, out_shape=_sds(other.shape, dtype),
        grid_spec=pltpu.PrefetchScalarGridSpec(
            num_scalar_prefetch=1, grid=(n, hr // tr),
            in_specs=[pl.BlockSpec((None, None, tr, cols), lambda s, i, pl_ref: (s, pl_ref[1], i, 0)), tile],
            out_specs=tile))(place, piece.reshape(n, 2, hr, cols), other)


def _scatter_sems(n):
    return [pltpu.SemaphoreType.DMA((3 * n,)), pltpu.SemaphoreType.DMA((3 * n,))]


def _scatter_steps(srcs, dsts, send_sems, recv_sems):
    n = len(srcs)
    x, y, c, chips = _place()
    me = 2 * x + y

    def ici(k, j, src_slot, dst_slot):
        px, py = chips[j]
        return pltpu.make_async_remote_copy(
            src_ref=srcs[k].at[src_slot], dst_ref=dsts[k].at[dst_slot], send_sem=send_sems.at[3 * k + j],
            recv_sem=recv_sems.at[3 * k + j], device_id=(px, py, c), device_id_type=MESH)

    def start():
        for k in range(n):
            for j, (px, py) in enumerate(chips):
                ici(k, j, 2 * px + py, me).start()

    def finish():
        for k in range(n):
            for j, (px, py) in enumerate(chips):
                ici(k, j, me, 2 * px + py).wait_recv()
        for k in range(n):
            for j, (px, py) in enumerate(chips):
                ici(k, j, 2 * px + py, me).wait_send()

    return start, finish


def _scatter_chips(parts, name):
    n = len(parts)

    def body(*refs):
        start, finish = _scatter_steps(refs[:n], refs[n:2 * n], *refs[2 * n:])
        start()
        finish()

    return pl.pallas_call(
        body, name=name, out_shape=[_sds(a.shape, a.dtype) for a in parts],
        in_specs=[ANY] * n, out_specs=[ANY] * n, scratch_shapes=_scatter_sems(n))(*parts)


def _sum_chips(landed, own, place, name):
    _, r, cols = landed.shape
    tr = _row_tile(r)

    def body(pl_ref, q_ref, p_ref, o_ref):
        me = pl_ref[0]
        f = lambda j: jnp.where(me == j, p_ref[...], q_ref[j]).astype(F32)
        o_ref[...] = ((f(0) + f(1)) + f(2)) + f(3)

    return pl.pallas_call(
        body, name=name, out_shape=_sds((2, r, cols)),
        grid_spec=pltpu.PrefetchScalarGridSpec(
            num_scalar_prefetch=1, grid=(r // tr,),
            in_specs=[pl.BlockSpec((N_SHARD, tr, cols), lambda i, pl_ref: (0, i, 0)),
                      pl.BlockSpec((None, tr, cols), lambda i, pl_ref: (pl_ref[0], i, 0))],
            out_specs=pl.BlockSpec((None, tr, cols), lambda i, pl_ref: (pl_ref[1], i, 0))))(place, landed, own)


def _join_halves(bufs):
    n = len(bufs)

    def body(*refs):
        dsts = refs[n:2 * n]
        send_sems, recv_sems = refs[2 * n:]
        x, y, c, _ = _place()
        copies = [pltpu.make_async_remote_copy(
            src_ref=dsts[k].at[c], dst_ref=dsts[k].at[c], send_sem=send_sems.at[k], recv_sem=recv_sems.at[k],
            device_id=(x, y, 1 - c), device_id_type=MESH) for k in range(n)]
        for cp in copies:
            cp.start()
        for cp in copies:
            cp.wait()

    return pl.pallas_call(
        body, name="grads_join_halves", out_shape=[_sds(a.shape) for a in bufs], in_specs=[ANY] * n, out_specs=[ANY] * n,
        input_output_aliases={k: k for k in range(n)},
        scratch_shapes=[pltpu.SemaphoreType.DMA((n,)), pltpu.SemaphoreType.DMA((n,))])(*bufs)


GATHER_F32 = ("ln_gain", "ln_bias", "pool_b", "gdn_conv")
REPLICATED = ("pool_scale", "gdn_a_log", "gdn_dt_bias", "gdn_norm_w", "ple_gate_b")
WEIGHTS = ("ln_gain", "ln_bias", "pool_w", "pool_b", "pool_scale", "gdn_w_in", "gdn_conv", "gdn_a_log", "gdn_dt_bias",
           "gdn_norm_w", "gdn_w_out", "mlp_w1", "mlp_w2", "ple_gate_w", "ple_gate_b", "ple_proj")
SMALL_GRADS = ("ple_proj", "pool_w", "ln_gain", "ln_bias", "pool_b", "gdn_conv") + REPLICATED


def _pack(parts, lanes, row_multiple):
    flat = jnp.concatenate([a.reshape(-1) for a in parts])
    rows = -(-flat.shape[0] // (2 * lanes * row_multiple)) * row_multiple
    return jnp.pad(flat, (0, 2 * rows * lanes - flat.shape[0])).reshape(2, rows, lanes)


def _unpack(flat, shapes):
    out, off = [], 0
    for shp in shapes:
        n = math.prod(shp)
        out.append(flat[..., off:off + n].reshape(flat.shape[:-1] + tuple(shp)))
        off += n
    return out


def _pad_lanes(a, offset, width=128):
    return jnp.pad(a, ((0, 0), (offset, width - offset - a.shape[1])))


def kernel(x, p, ln_gain, ln_bias, pool_w, pool_b, pool_scale, gdn_w_in, gdn_conv, gdn_a_log, gdn_dt_bias, gdn_norm_w, gdn_w_out, mlp_w1, mlp_w2, ple_gate_w, ple_gate_b, ple_proj, loss_target, m_ln_gain, m_ln_bias, m_pool_w, m_pool_b, m_pool_scale, m_gdn_w_in, m_gdn_conv, m_gdn_a_log, m_gdn_dt_bias, m_gdn_norm_w, m_gdn_w_out, m_mlp_w1, m_mlp_w2, m_ple_gate_w, m_ple_gate_b, m_ple_proj, v_ln_gain, v_ln_bias, v_pool_w, v_pool_b, v_pool_scale, v_gdn_w_in, v_gdn_conv, v_gdn_a_log, v_gdn_dt_bias, v_gdn_norm_w, v_gdn_w_out, v_mlp_w1, v_mlp_w2, v_ple_gate_w, v_ple_gate_b, v_ple_proj):
    shard = dict(ln_gain=ln_gain, ln_bias=ln_bias, pool_w=pool_w, pool_b=pool_b, pool_scale=pool_scale, gdn_w_in=gdn_w_in,
                 gdn_conv=gdn_conv, gdn_a_log=gdn_a_log, gdn_dt_bias=gdn_dt_bias, gdn_norm_w=gdn_norm_w, gdn_w_out=gdn_w_out,
                 mlp_w1=mlp_w1, mlp_w2=mlp_w2, ple_gate_w=ple_gate_w, ple_gate_b=ple_gate_b, ple_proj=ple_proj)
    mom = dict(ln_gain=m_ln_gain, ln_bias=m_ln_bias, pool_w=m_pool_w, pool_b=m_pool_b, pool_scale=m_pool_scale,
               gdn_w_in=m_gdn_w_in, gdn_conv=m_gdn_conv, gdn_a_log=m_gdn_a_log, gdn_dt_bias=m_gdn_dt_bias,
               gdn_norm_w=m_gdn_norm_w, gdn_w_out=m_gdn_w_out, mlp_w1=m_mlp_w1, mlp_w2=m_mlp_w2, ple_gate_w=m_ple_gate_w,
               ple_gate_b=m_ple_gate_b, ple_proj=m_ple_proj)
    var = dict(ln_gain=v_ln_gain, ln_bias=v_ln_bias, pool_w=v_pool_w, pool_b=v_pool_b, pool_scale=v_pool_scale,
               gdn_w_in=v_gdn_w_in, gdn_conv=v_gdn_conv, gdn_a_log=v_gdn_a_log, gdn_dt_bias=v_gdn_dt_bias,
               gdn_norm_w=v_gdn_norm_w, gdn_w_out=v_gdn_w_out, mlp_w1=v_mlp_w1, mlp_w2=v_mlp_w2, ple_gate_w=v_ple_gate_w,
               ple_gate_b=v_ple_gate_b, ple_proj=v_ple_proj)

    xi, yi, ci = lax.axis_index("x"), lax.axis_index("y"), lax.axis_index("c")
    me = (2 * xi + yi).reshape(1).astype(jnp.int32)
    place = jnp.stack([2 * xi + yi, ci]).astype(jnp.int32)
    early = [("mlp_w1", 0), ("mlp_w2", 0), ("ple_gate_w", 0), ("ple_proj", 0), ("pool_w", 0)]
    late = [("mlp_w1", 1), ("mlp_w2", 1), ("ple_gate_w", 1), ("ple_proj", 1), ("gdn_w_out", 0), ("gdn_w_in", 0)]
    halved = lambda n: shard[n].reshape(shard[n].shape[0], 2, -1, shard[n].shape[-1])
    placed = lambda ops, tag: [_place_shard(halved(n), me, BF16, f"place_{tag}_{n}", l) for n, l in ops]
    small_in = _place_shard(_pack([shard[n] for n in GATHER_F32], 128, 8)[None], me, F32, "place_small")
    got_early = _all_gather(placed(early, "early") + [small_in], "weights_all_gather_early")
    placed_late = placed(late, "late")
    st = dict(zip(GATHER_F32, _unpack(got_early[-1].reshape(N_SHARD, -1), [shard[n].shape for n in GATHER_F32])))

    cat_last = lambda a: jnp.moveaxis(a, 0, -2).reshape(a.shape[1:-1] + (N_SHARD * a.shape[-1],))
    gain = cat_last(st["ln_gain"])
    bias = cat_last(st["ln_bias"])
    wp = got_early[4].reshape(N_SHARD, 4, POOL_GROUP // N_SHARD, POOL_GROUP)
    pb = cat_last(st["pool_b"]).reshape(1, D_MODEL)
    ps = pool_scale
    conv_w = cat_last(st["gdn_conv"])[0]
    merged = lambda g: g.reshape(N_SHARD, -1, g.shape[-1])
    mlp_w = lambda i, got: (merged(got[0]), merged(got[1]), merged(got[2]), ple_gate_b[i:i + 1], merged(got[3]))
    alog_l = _pad_lanes(gdn_a_log, HEADS)
    dtb_l = _pad_lanes(gdn_dt_bias, HEADS)
    nw = jnp.tile(gdn_norm_w, (1, HEADS))
    ln = lambda i, k: (gain[i, k][None], bias[i, k][None])

    x0 = x[0]
    p0, p1 = p[0, 0], p[1, 0]

    x1, xh1, rs1 = _pool_fwd(x0, wp, pb, ps, *ln(0, 0))
    (x2, xh2, rs2, a0, xb0), got_late = _mlp_fwd(x1, p0, *mlp_w(0, got_early), *ln(0, 1), "mlp_fwd_0", gather=placed_late)
    w_out = got_late[4].reshape(D_MODEL, D_MODEL)
    w_in = _assemble_w_in(merged(got_late[5]))
    proj, y_conv, q, k, v, gcb = _conv_fwd(x2, w_in, conv_w, alog_l, dtb_l)
    u, w, qd, kd, qk, gl, t_inv = _gdn_prep(q, k, v, gcb)
    o, states = _gdn_scan(u, w, qd, kd, qk, gl)
    x3, xh3, rs3 = _gdn_out_fwd(o, proj, x2, w_out, nw, *ln(1, 0))
    (dy4, xh4, rs4, a1, xb1, loss_l), _ = _mlp_fwd(x3, p1, *mlp_w(1, got_late), *ln(1, 1), "mlp_fwd_1", target=loss_target[0])

    g_gain = [[None, None], [None, None]]
    g_bias = [[None, None], [None, None]]

    def mlp_grads(i, dy, xh, rs, x_mid, xb, a, p_i, got, scatter=()):
        (dx, dh, dzg, dpp, drb, dg, db, dgb), landed = _mlp_bwd(
            dy, xh, rs, x_mid, a, p_i, *mlp_w(i, got), ln(i, 1)[0], f"mlp_bwd_{i}", scatter=scatter)
        g_gain[i][1], g_bias[i][1] = dg, db
        return dx, dict(
            mlp_w1=_wgrad(xb, dh, f"dw1_{i}", stack_cols=True), mlp_w2=_wgrad(a, drb, f"dw2_{i}").reshape(N_SHARD, -1, D_MODEL),
            ple_gate_w=_wgrad(xb, dzg, f"dgate_w_{i}").reshape(N_SHARD, -1, D_MODEL),
            ple_proj=jnp.moveaxis(_wgrad(p_i, dpp, f"dproj_{i}").reshape(PLE_DIM, N_SHARD, -1), 1, 0), ple_gate_b=dgb), landed

    def chip_sums(pieces, wire, tag):
        others = _swap_halves(pieces, f"grads_swap_halves_{tag}")
        return [_add_half(a, b, place, t, f"grads_add_half_{tag}{i}") for i, (a, b, t) in enumerate(zip(pieces, others, wire))]

    dx3, gl1, _ = mlp_grads(1, dy4, xh4, rs4, x3, xb1, a1, p1, got_late)
    dres, do, dz, on_b, drb3, g_gain[1][0], g_bias[1][0], d_nw = _gdn_out_bwd(dx3, xh3, rs3, o, proj, w_out, nw, ln(1, 0)[0])
    d_wout = _wgrad(on_b, drb3, "dw_out").reshape(N_SHARD, -1, D_MODEL)
    du, dw, dqd, dkd, dqk, dgl = _gdn_scan_bwd(do, u, w, qd, kd, qk, gl, states)
    dq, dk, dv, dgcb = _gdn_prep_bwd(q, k, v, gcb, t_inv, u, w, du, dw, dqd, dkd, dqk, dgl)
    dproj, dx2, d_conv, d_alog_l, d_dtb_l = _conv_bwd(proj, y_conv, dq, dk, dv, dgcb, dz, conv_w, alog_l, dtb_l, w_in, dres)
    d_win = _split_w_in(_wgrad(x2, dproj, "dw_in"))
    sums_late = chip_sums([gl1["mlp_w1"], gl1["mlp_w2"], gl1["ple_gate_w"], d_wout, d_win], [BF16] * 5, "late")
    dx1, gl0, landed_late = mlp_grads(0, dx2, xh2, rs2, x1, xb0, a0, p0, got_early, scatter=sums_late)
    sums_early = chip_sums([gl0["mlp_w1"], gl0["mlp_w2"], gl0["ple_gate_w"]], [BF16] * 3, "early")
    (dx0, g_gain[0][0], g_bias[0][0], d_ps, d_pb, d_wp), landed_early = _pool_bwd(
        dx1, xh1, rs1, x0, wp, pb, ps, ln(0, 0)[0], scatter=sums_early)

    split_last = lambda a: jnp.moveaxis(a.reshape(a.shape[:-1] + (N_SHARD, a.shape[-1] // N_SHARD)), -2, 0)
    small_st = dict(
        ple_proj=jnp.stack([gl0["ple_proj"], gl1["ple_proj"]], axis=1),
        pool_w=jnp.moveaxis(d_wp.reshape(4, N_SHARD, POOL_GROUP // N_SHARD, POOL_GROUP), 1, 0)[:, None],
        ln_gain=split_last(jnp.stack([jnp.concatenate(r, axis=0) for r in g_gain])),
        ln_bias=split_last(jnp.stack([jnp.concatenate(r, axis=0) for r in g_bias])),
        pool_b=split_last(d_pb.reshape(1, 4, POOL_GROUP)),
        gdn_conv=split_last(d_conv)[:, None],
    )
    rep = dict(pool_scale=d_ps, gdn_a_log=d_alog_l[:, HEADS:2 * HEADS], gdn_dt_bias=d_dtb_l[:, HEADS:2 * HEADS],
               gdn_norm_w=d_nw, ple_gate_b=jnp.concatenate([gl0["ple_gate_b"], gl1["ple_gate_b"]], axis=0))
    for n in REPLICATED:
        small_st[n] = jnp.broadcast_to(rep[n][None], (N_SHARD,) + rep[n].shape)
    small_flat = jnp.concatenate([small_st[n].reshape(N_SHARD, -1) for n in SMALL_GRADS], axis=1)
    small_rows = -(-small_flat.shape[1] // (16 * LANES)) * 16
    small_piece = jnp.pad(small_flat, ((0, 0), (0, small_rows * LANES - small_flat.shape[1]))).reshape(N_SHARD, small_rows, LANES)

    sums_small = chip_sums([small_piece], [F32], "small")
    landed_small = _scatter_chips(sums_small, "grads_scatter_chips_small")
    red = _join_halves([_sum_chips(q_, p_, place, f"grads_sum_chips_{i}") for i, (q_, p_) in
                        enumerate(zip(list(landed_early) + list(landed_small) + list(landed_late),
                                      sums_early + sums_small + sums_late))])
    red = [r.reshape(-1, r.shape[-1]) for r in red]
    grads = dict(mlp_w1=[red[0], red[4]], mlp_w2=[red[1], red[5]], ple_gate_w=[red[2], red[6]], gdn_w_out=red[7], gdn_w_in=red[8])
    grads.update(zip(SMALL_GRADS, _unpack(red[3].reshape(-1), [shard[n].shape for n in SMALL_GRADS])))

    delta, new_m, new_v = {}, {}, {}
    small = [n for n in WEIGHTS if shard[n].size < 128 * 128]
    for n in WEIGHTS:
        if n in small:
            continue
        to2d = lambda a, n=n: a.reshape(-1, shard[n].shape[-1])
        g_n = grads[n] if isinstance(grads[n], list) else to2d(grads[n])
        g2, d2, m2, v2 = _adamw(to2d(shard[n]), g_n, to2d(mom[n]), to2d(var[n]), "adamw_" + n)
        grads[n], delta[n], new_m[n], new_v[n] = (t.reshape(shard[n].shape) for t in (g2, d2, m2, v2))
    pk = lambda d: _pack([d[n] for n in small], 128, 8).reshape(-1, 128)
    _, d2, m2, v2 = _adamw(pk(shard), pk(grads), pk(mom), pk(var), "adamw_small")
    for dst, t in ((delta, d2), (new_m, m2), (new_v, v2)):
        dst.update(zip(small, _unpack(t.reshape(-1), [shard[n].shape for n in small])))

    loss = lax.psum(loss_l[0, 0], ("x", "y", "c"))
    return (loss, dx0[None], *[grads[n] for n in WEIGHTS], *[delta[n] for n in WEIGHTS],
            *[new_m[n] for n in WEIGHTS], *[new_v[n] for n in WEIGHTS])
```

```python
import math

import jax
import jax.numpy as jnp
from jax import lax
from jax.experimental import pallas as pl
from jax.experimental.pallas import tpu as pltpu

F32 = jnp.float32
BF16 = jnp.bfloat16

D_MODEL = 1024
D_FF = 4096
PLE_DIM = 256
N_SHARD = 4
POOL_WINDOWS = (2, 4, 8, 16)
POOL_GROUP = 256
POOL_HALO = 16
HEADS = 8
HEAD_DIM = 128
CHUNK = 64
CONV_WIDTH = 4
CONV_HALO = 8
QKV_DIM = 3 * D_MODEL
GDN_IN_DIM = QKV_DIM + D_MODEL + 2 * HEADS
GDN_IN_PAD = 4224
BA_BLOCK = (QKV_DIM + D_MODEL) // 128
ALPHA = (2.0 * 2) ** 0.25
LN_EPS = 1e-5
RMS_EPS = 1e-6
L2_EPS = 1e-6
ADAM_LR, ADAM_B1, ADAM_B2, ADAM_EPS, ADAM_WD, ADAM_STEP = 0.001, 0.9, 0.999, 1e-08, 0.01, 10

ROW_TILE = 512
CONV_TILE = 256
PREP_CHUNKS = 32
PREP_BWD_CHUNKS = 16
SCAN_CHUNKS = 8
LANES = 1024
ADAM_ROWS = 256

NN = (((1,), (0,)), ((), ()))
NT = (((1,), (1,)), ((), ()))
TN = (((0,), (0,)), ((), ()))
BNN = (((2,), (1,)), ((0,), (0,)))
BNT = (((2,), (2,)), ((0,), (0,)))
BTN = (((1,), (1,)), ((0,), (0,)))
MESH = pl.DeviceIdType.MESH
ANY = pl.BlockSpec(memory_space=pl.ANY)


def _bdot(a, b, dims):
    return lax.dot_general(a.astype(BF16), b.astype(BF16), dims, preferred_element_type=F32)


def _hdot(a, b, dims):
    return lax.dot_general(a, b, dims, precision=lax.Precision.HIGHEST, preferred_element_type=F32)


def _mdot(a, b, dims):
    return lax.dot_general(a, b, dims, precision=lax.Precision.HIGH, preferred_element_type=F32)


def _sigmoid(x):
    return 0.5 * jnp.tanh(0.5 * x) + 0.5


def _silu(x):
    return x * _sigmoid(x)


def _softplus(x):
    return jnp.maximum(x, 0.0) + jnp.log1p(jnp.exp(-jnp.abs(x)))


def _call(body, name, grid, in_specs, out_specs, out_shape, scratch=(), sem=None, aliases=None):
    params = pltpu.CompilerParams(dimension_semantics=sem) if sem else None
    return pl.pallas_call(
        body, name=name, grid=grid, in_specs=in_specs, out_specs=out_specs, out_shape=out_shape,
        scratch_shapes=list(scratch), compiler_params=params, input_output_aliases=aliases or {})


def _row(d):
    return pl.BlockSpec((1, d), lambda *_: (0, 0))


def _full(shape):
    n = len(shape)
    return pl.BlockSpec(shape, lambda *_: (0,) * n)


def _sds(shape, dtype=F32):
    return jax.ShapeDtypeStruct(shape, dtype)


def _ln_fwd(r, gain, bias):
    mu = jnp.mean(r, axis=-1, keepdims=True)
    xc = r - mu
    rstd = lax.rsqrt(jnp.mean(xc * xc, axis=-1, keepdims=True) + LN_EPS)
    xhat = xc * rstd
    return xhat * gain + bias, xhat, rstd


def _ln_bwd(dy, xhat, rstd, gain):
    dxh = dy * gain
    m1 = jnp.mean(dxh, axis=-1, keepdims=True)
    m2 = jnp.mean(dxh * xhat, axis=-1, keepdims=True)
    return rstd * (dxh - m1 - xhat * m2)


def _acc(ref, first, val):
    @pl.when(first)
    def _():
        ref[...] = val

    @pl.when(jnp.logical_not(first))
    def _():
        ref[...] += val


def _pooled_groups(xe, t0, ts):
    pos = (t0 + lax.broadcasted_iota(jnp.int32, (ts, 1), 0) + 1).astype(F32)
    outs = []
    for gi, win in enumerate(POOL_WINDOWS):
        xs = xe[:, gi * POOL_GROUP:(gi + 1) * POOL_GROUP]
        s, k = xs, 1
        while k < win:
            s = s + pltpu.roll(s, k, 0)
            k *= 2
        mean = s[POOL_HALO:] / jnp.minimum(pos, float(win))
        outs.append(mean - xs[POOL_HALO:])
    return outs


def _pool_groups_w(w_ref):
    return [jnp.concatenate([w_ref[s, g] for s in range(N_SHARD)], axis=0) for g in range(4)]


def _pool_fwd(x, wp, pb, ps, gain, bias):
    s_len = x.shape[0]
    ts = min(ROW_TILE, s_len)
    hb = ts // POOL_HALO

    def body(x_ref, halo_ref, w_ref, pb_ref, ps_ref, g_ref, b_ref, y_ref, xhat_ref, rstd_ref):
        i = pl.program_id(0)
        x_t = x_ref[...]
        halo = jnp.where(i > 0, halo_ref[...], 0.0)
        pooled = _pooled_groups(jnp.concatenate([halo, x_t], axis=0), i * ts, ts)
        wg = _pool_groups_w(w_ref)
        y = jnp.concatenate([_bdot(pooled[g], wg[g], NN) for g in range(4)], axis=1) + pb_ref[...]
        r = ALPHA * x_t + y * ps_ref[...]
        y_ref[...], xhat_ref[...], rstd_ref[...] = _ln_fwd(r, g_ref[...], b_ref[...])

    tile = pl.BlockSpec((ts, D_MODEL), lambda i: (i, 0))
    return _call(
        body, "pool_fwd", (s_len // ts,),
        [tile, pl.BlockSpec((POOL_HALO, D_MODEL), lambda i: (jnp.maximum(i * hb - 1, 0), 0)),
         _full(wp.shape), _row(D_MODEL), _row(D_MODEL), _row(D_MODEL), _row(D_MODEL)],
        [tile, tile, pl.BlockSpec((ts, 1), lambda i: (i, 0))],
        [_sds((s_len, D_MODEL)), _sds((s_len, D_MODEL)), _sds((s_len, 1))],
        sem=("parallel",))(x, x, wp, pb, ps, gain, bias)


def _pool_bwd(dy, xhat, rstd, x, wp, pb, ps, gain, scatter=()):
    s_len = x.shape[0]
    ts = min(ROW_TILE, s_len)
    hb = ts // POOL_HALO
    n_t = s_len // ts
    ne = ts + POOL_HALO
    n_s = len(scatter)

    def body(*refs):
        dy_ref, dyn_ref, xh_ref, xhn_ref, rs_ref, rsn_ref, x_ref, xp_ref, w_ref, pb_ref, ps_ref, g_ref = refs[:12]
        dx_ref, dg_ref, db_ref, dps_ref, dpb_ref, dw_ref = refs[12 + n_s:18 + n_s]
        i = pl.program_id(0)
        if n_s:
            start, finish = _scatter_steps(refs[12:12 + n_s], refs[18 + n_s:18 + 2 * n_s], *refs[18 + 2 * n_s:])
            pl.when(i == 0)(start)
        more = i < n_t - 1
        dy_t, xh_t = dy_ref[...], xh_ref[...]
        dy_e = jnp.concatenate([dy_t, jnp.where(more, dyn_ref[...], 0.0)], axis=0)
        xh_e = jnp.concatenate([xh_t, xhn_ref[...]], axis=0)
        rs_e = jnp.concatenate([rs_ref[...], rsn_ref[...]], axis=0)
        dr_e = _ln_bwd(dy_e, xh_e, rs_e, g_ref[...])
        dyy_e = dr_e * ps_ref[...]
        pos_e = (i * ts + lax.broadcasted_iota(jnp.int32, (ne, 1), 0) + 1).astype(F32)
        dxs = []
        wg = _pool_groups_w(w_ref)
        for gi, win in enumerate(POOL_WINDOWS):
            sl = slice(gi * POOL_GROUP, (gi + 1) * POOL_GROUP)
            dpool = _bdot(dyy_e[:, sl], wg[gi], NT)
            s, k = dpool / jnp.minimum(pos_e, float(win)), 1
            while k < win:
                s = s + pltpu.roll(s, ne - k, 0)
                k *= 2
            dxs.append(s[:ts] - dpool[:ts])
        dx_ref[...] = ALPHA * dr_e[:ts] + jnp.concatenate(dxs, axis=1)

        x_t = x_ref[...]
        halo = jnp.where(i > 0, xp_ref[...], 0.0)
        pooled = _pooled_groups(jnp.concatenate([halo, x_t], axis=0), i * ts, ts)
        y = jnp.concatenate([_bdot(pooled[g], wg[g], NN) for g in range(4)], axis=1) + pb_ref[...]
        dr_t, dyy_t = dr_e[:ts], dyy_e[:ts]
        first = i == 0
        _acc(dg_ref, first, jnp.sum(dy_t * xh_t, axis=0, keepdims=True))
        _acc(db_ref, first, jnp.sum(dy_t, axis=0, keepdims=True))
        _acc(dps_ref, first, jnp.sum(dr_t * y, axis=0, keepdims=True))
        _acc(dpb_ref, first, jnp.sum(dyy_t, axis=0, keepdims=True))
        for g in range(4):
            _acc(dw_ref.at[g], first, _bdot(pooled[g], dyy_t[:, g * POOL_GROUP:(g + 1) * POOL_GROUP], TN))
        if n_s:
            pl.when(i == n_t - 1)(finish)

    tile = pl.BlockSpec((ts, D_MODEL), lambda i: (i, 0))
    nxt = pl.BlockSpec((POOL_HALO, D_MODEL), lambda i: (jnp.minimum((i + 1) * hb, n_t * hb - 1), 0))
    prv = pl.BlockSpec((POOL_HALO, D_MODEL), lambda i: (jnp.maximum(i * hb - 1, 0), 0))
    rs_t = pl.BlockSpec((ts, 1), lambda i: (i, 0))
    rs_n = pl.BlockSpec((POOL_HALO, 1), lambda i: (jnp.minimum((i + 1) * hb, n_t * hb - 1), 0))
    row = _row(D_MODEL)
    out = _call(
        body, "pool_bwd", (n_t,),
        [tile, nxt, tile, nxt, rs_t, rs_n, tile, prv, _full(wp.shape), row, row, row] + [ANY] * n_s,
        [tile, row, row, row, row, _full((4, POOL_GROUP, POOL_GROUP))] + [ANY] * n_s,
        [_sds((s_len, D_MODEL))] + [_sds((1, D_MODEL))] * 4 + [_sds((4, POOL_GROUP, POOL_GROUP))]
        + [_sds(t.shape, t.dtype) for t in scatter],
        scratch=_scatter_sems(n_s) if n_s else [], sem=("arbitrary",),
    )(dy, dy, xhat, xhat, rstd, rstd, x, x, wp, pb, ps, gain, *scatter)
    return out[:6], out[6:]


def _mlp_weight_specs():
    fc = D_FF // N_SHARD
    return [pl.BlockSpec((None, D_MODEL, fc), lambda i, j: (j, 0, 0)),
            pl.BlockSpec((None, fc, D_MODEL), lambda i, j: (j, 0, 0)),
            _full((N_SHARD, D_MODEL // N_SHARD, D_MODEL)),
            _row(D_MODEL),
            _full((N_SHARD, PLE_DIM, D_MODEL // N_SHARD))]


def _gate_w(gw_ref):
    return gw_ref[...].reshape(D_MODEL, D_MODEL)


def _ple_proj(pj_ref):
    return jnp.concatenate([pj_ref[s] for s in range(N_SHARD)], axis=1)


def _mlp_fwd(x, p, w1s, w2s, gw, gb, proj, gain, bias, name, gather=(), target=None):
    s_len = x.shape[0]
    ts = min(ROW_TILE, s_len)
    n_i = s_len // ts
    n_g = len(gather)
    has_t = target is not None
    n_in, n_out = 9 + has_t, 5 + has_t

    def body(*refs):
        x_ref, p_ref, w1_ref, w2_ref, gw_ref, gb_ref, pj_ref, g_ref, b_ref = refs[:9]
        y_ref, xhat_ref, rstd_ref, a_ref, xbo_ref = refs[n_in + n_g:n_in + n_g + 5]
        acc_ref, xb_ref = refs[n_in + n_out + 2 * n_g:n_in + n_out + 2 * n_g + 2]
        i, j = pl.program_id(0), pl.program_id(1)
        if n_g:
            start, finish = _gather_steps(refs[n_in + n_out + n_g:n_in + n_out + 2 * n_g], *refs[n_in + n_out + 2 * n_g + 2:])
            pl.when((i == 0) & (j == 0))(start)

        @pl.when(j == 0)
        def _():
            x_t = x_ref[...]
            xb_ref[...] = x_t.astype(BF16)
            xbo_ref[...] = x_t.astype(BF16)
            gate = _sigmoid(_bdot(x_t, _gate_w(gw_ref), NN) + gb_ref[...])
            acc_ref[...] = ALPHA * x_t + gate * _bdot(p_ref[...], _ple_proj(pj_ref), NN)

        h = jnp.maximum(_bdot(xb_ref[...], w1_ref[...], NN), 0.0)
        a = (h * h).astype(BF16)
        a_ref[...] = a
        acc_ref[...] += _bdot(a, w2_ref[...], NN)

        @pl.when(j == N_SHARD - 1)
        def _():
            y, xhat_ref[...], rstd_ref[...] = _ln_fwd(acc_ref[...], g_ref[...], b_ref[...])
            if has_t:
                err = y - refs[9][...]
                y_ref[...] = err * (1.0 / D_MODEL)
                part = 0.5 * jnp.sum(jnp.mean(err * err, axis=-1, keepdims=True))
                _acc(refs[n_in + n_g + 5], i == 0, part + jnp.zeros((1, 128), F32))
            else:
                y_ref[...] = y

        if n_g:
            pl.when((i == n_i - 1) & (j == N_SHARD - 1))(finish)

    tile = pl.BlockSpec((ts, D_MODEL), lambda i, j: (i, 0))
    row = _row(D_MODEL)
    out = _call(
        body, name, (n_i, N_SHARD),
        [tile, pl.BlockSpec((ts, PLE_DIM), lambda i, j: (i, 0))] + _mlp_weight_specs() + [row, row] + [tile] * has_t
        + [ANY] * n_g,
        [tile, tile, pl.BlockSpec((ts, 1), lambda i, j: (i, 0)), pl.BlockSpec((ts, D_FF // N_SHARD), lambda i, j: (i, j)), tile]
        + [_row(128)] * has_t + [ANY] * n_g,
        [_sds((s_len, D_MODEL)), _sds((s_len, D_MODEL)), _sds((s_len, 1)), _sds((s_len, D_FF), BF16), _sds((s_len, D_MODEL), BF16)]
        + [_sds((1, 128))] * has_t + [_sds(a.shape, a.dtype) for a in gather],
        scratch=[pltpu.VMEM((ts, D_MODEL), F32), pltpu.VMEM((ts, D_MODEL), BF16)] + (_gather_sems(n_g) if n_g else []),
        sem=("arbitrary", "arbitrary"), aliases={n_in + k: n_out + k for k in range(n_g)},
    )(x, p, w1s, w2s, gw, gb, proj, gain, bias, *([target] if has_t else []), *gather)
    return out[:n_out], out[n_out:]


def _mlp_bwd(dy, xhat, rstd, x, a, p, w1s, w2s, gw, gb, proj, gain, name, scatter=()):
    s_len = x.shape[0]
    ts = min(ROW_TILE, s_len)
    fc = D_FF // N_SHARD
    n_i = s_len // ts
    n_s = len(scatter)

    def body(*refs):
        dy_ref, xh_ref, rs_ref, x_ref, a_ref, p_ref, w1_ref, w2_ref, gw_ref, gb_ref, pj_ref, g_ref = refs[:12]
        dx_ref, dh_ref, dzg_ref, dpp_ref, drb_ref, dg_ref, db_ref, dgb_ref = refs[12 + n_s:20 + n_s]
        acc_ref, dr_ref = refs[20 + 2 * n_s:22 + 2 * n_s]
        i, j = pl.program_id(0), pl.program_id(1)
        if n_s:
            start, finish = _scatter_steps(refs[12:12 + n_s], refs[20 + n_s:20 + 2 * n_s], *refs[22 + 2 * n_s:])
            pl.when((i == 0) & (j == 0))(start)

        @pl.when(j == 0)
        def _():
            dy_t, xh_t, x_t = dy_ref[...], xh_ref[...], x_ref[...]
            dr = _ln_bwd(dy_t, xh_t, rs_ref[...], g_ref[...])
            drb = dr.astype(BF16)
            dr_ref[...] = drb
            drb_ref[...] = drb
            gw_full = _gate_w(gw_ref)
            gate = _sigmoid(_bdot(x_t, gw_full, NN) + gb_ref[...])
            pp = _bdot(p_ref[...], _ple_proj(pj_ref), NN)
            dzg = dr * pp * gate * (1.0 - gate)
            dzg_ref[...] = dzg.astype(BF16)
            dpp_ref[...] = (dr * gate).astype(BF16)
            acc_ref[...] = ALPHA * dr + _bdot(dzg, gw_full, NT)
            first = i == 0
            _acc(dg_ref, first, jnp.sum(dy_t * xh_t, axis=0, keepdims=True))
            _acc(db_ref, first, jnp.sum(dy_t, axis=0, keepdims=True))
            _acc(dgb_ref, first, jnp.sum(dzg, axis=0, keepdims=True))

        dh = (_bdot(dr_ref[...], w2_ref[...], NT) * (2.0 * jnp.sqrt(a_ref[...].astype(F32)))).astype(BF16)
        dh_ref[...] = dh
        acc_ref[...] += _bdot(dh, w1_ref[...], NT)

        @pl.when(j == N_SHARD - 1)
        def _():
            dx_ref[...] = acc_ref[...]

        if n_s:
            pl.when((i == n_i - 1) & (j == N_SHARD - 1))(finish)

    tile = pl.BlockSpec((ts, D_MODEL), lambda i, j: (i, 0))
    ftile = pl.BlockSpec((ts, fc), lambda i, j: (i, j))
    row = _row(D_MODEL)
    out = _call(
        body, name, (n_i, N_SHARD),
        [tile, tile, pl.BlockSpec((ts, 1), lambda i, j: (i, 0)), tile, ftile, pl.BlockSpec((ts, PLE_DIM), lambda i, j: (i, 0))]
        + _mlp_weight_specs() + [row] + [ANY] * n_s,
        [tile, ftile, tile, tile, tile, row, row, row] + [ANY] * n_s,
        [_sds((s_len, D_MODEL)), _sds((s_len, D_FF), BF16)]
        + [_sds((s_len, D_MODEL), BF16)] * 3 + [_sds((1, D_MODEL))] * 3 + [_sds(t.shape, t.dtype) for t in scatter],
        scratch=[pltpu.VMEM((ts, D_MODEL), F32), pltpu.VMEM((ts, D_MODEL), BF16)] + (_scatter_sems(n_s) if n_s else []),
        sem=("arbitrary", "arbitrary"))(dy, xhat, rstd, x, a, p, w1s, w2s, gw, gb, proj, gain, *scatter)
    return out[:8], out[8:]


def _wgrad(a, b, name, stack_cols=False):
    s_len, m = a.shape
    n = b.shape[1]
    ts = min(2048 if a.dtype == BF16 and b.dtype == BF16 else 1024, s_len)
    tm = min(m, 1024)
    tn = n // N_SHARD if stack_cols else (1408 if n == GDN_IN_PAD else min(n, 1024))
    n_s = s_len // ts

    def body(a_ref, b_ref, o_ref):
        _acc(o_ref, pl.program_id(2) == 0, _bdot(a_ref[...], b_ref[...], TN))

    if stack_cols:
        out_spec = pl.BlockSpec((None, tm, tn), lambda mi, nj, s: (nj, mi, 0))
        out_shape = _sds((N_SHARD, m, tn))
    else:
        out_spec = pl.BlockSpec((tm, tn), lambda mi, nj, s: (mi, nj))
        out_shape = _sds((m, n))
    return _call(
        body, name, (m // tm, n // tn, n_s),
        [pl.BlockSpec((ts, tm), lambda mi, nj, s: (s, mi)), pl.BlockSpec((ts, tn), lambda mi, nj, s: (s, nj))],
        out_spec, out_shape, sem=("parallel", "parallel", "arbitrary"))(a, b)


def _act_qkv(y):
    qkv = _silu(y)
    qs, ks = [], []
    for h in range(HEADS):
        qh = qkv[:, h * HEAD_DIM:(h + 1) * HEAD_DIM]
        kh = qkv[:, D_MODEL + h * HEAD_DIM:D_MODEL + (h + 1) * HEAD_DIM]
        qs.append(qh * (lax.rsqrt(jnp.sum(qh * qh, axis=-1, keepdims=True) + L2_EPS) * HEAD_DIM ** -0.5))
        ks.append(kh * lax.rsqrt(jnp.sum(kh * kh, axis=-1, keepdims=True) + L2_EPS))
    return jnp.concatenate(qs, axis=1), jnp.concatenate(ks, axis=1), qkv[:, 2 * D_MODEL:]


def _act_gb(ba, alog_l, dtb_l, tril):
    lane = lax.broadcasted_iota(jnp.int32, ba.shape, 1)
    g = jnp.where((lane >= HEADS) & (lane < 2 * HEADS), -jnp.exp(alog_l) * _softplus(ba + dtb_l), 0.0)
    return jnp.where(lane < HEADS, _sigmoid(ba), _hdot(tril, g, NN))


def _chunk_tril(t):
    ii = lax.broadcasted_iota(jnp.int32, (t, t), 0)
    jj = lax.broadcasted_iota(jnp.int32, (t, t), 1)
    return ((ii // CHUNK == jj // CHUNK) & (ii >= jj)).astype(F32)


def _conv_rows(xe, w, n_rows):
    y = xe[CONV_HALO:CONV_HALO + n_rows] * w[CONV_WIDTH - 1]
    for j in range(CONV_WIDTH - 1):
        y = y + pltpu.roll(xe, CONV_WIDTH - 1 - j, 0)[CONV_HALO:CONV_HALO + n_rows] * w[j]
    return y


def _conv_fwd(x, w_in, conv_w, alog_l, dtb_l):
    s_len = x.shape[0]
    ts = min(CONV_TILE, s_len)
    hb = ts // CONV_HALO

    def body(x_ref, xp_ref, w_ref, al_ref, dt_ref, win_hbm, proj_ref, y_ref, q_ref, k_ref, v_ref, gcb_ref, win_ref, win_sem):
        i = pl.program_id(0)

        @pl.when(i == 0)
        def _():
            cp = pltpu.make_async_copy(win_hbm, win_ref, win_sem)
            cp.start()
            cp.wait()

        proj = _bdot(x_ref[...], win_ref[...], NN)
        proj_ref[...] = proj
        halo = jnp.where(i > 0, _bdot(xp_ref[...], win_ref[:, :QKV_DIM], NN), 0.0)
        taps = [w_ref[pl.ds(j, 1), :] for j in range(CONV_WIDTH)]
        y = _conv_rows(jnp.concatenate([halo, proj[:, :QKV_DIM]], axis=0), taps, ts)
        y_ref[...] = y
        q_ref[...], k_ref[...], v_ref[...] = _act_qkv(y)
        gcb_ref[...] = _act_gb(proj[:, BA_BLOCK * 128:], al_ref[...], dt_ref[...], _chunk_tril(ts))

    tile = pl.BlockSpec((ts, D_MODEL), lambda i: (i, 0))
    return _call(
        body, "gdn_conv_fwd", (s_len // ts,),
        [tile, pl.BlockSpec((CONV_HALO, D_MODEL), lambda i: (jnp.maximum(i * hb - 1, 0), 0)),
         _full((CONV_WIDTH, QKV_DIM)), _row(128), _row(128), ANY],
        [pl.BlockSpec((ts, GDN_IN_PAD), lambda i: (i, 0)), pl.BlockSpec((ts, QKV_DIM), lambda i: (i, 0)), tile, tile, tile,
         pl.BlockSpec((ts, 128), lambda i: (i, 0))],
        [_sds((s_len, GDN_IN_PAD)), _sds((s_len, QKV_DIM))] + [_sds((s_len, D_MODEL))] * 3 + [_sds((s_len, 128))],
        scratch=[pltpu.VMEM(w_in.shape, w_in.dtype), pltpu.SemaphoreType.DMA],
        sem=("arbitrary",))(x, x, conv_w, alog_l, dtb_l, w_in)


def _conv_bwd(proj, y, dq, dk, dv, dgcb, dz, conv_w, alog_l, dtb_l, w_in, dres):
    s_len = proj.shape[0]
    ts = min(CONV_TILE, s_len)
    hb = ts // CONV_HALO
    n_t = s_len // ts
    te = ts + CONV_HALO

    def body(x_ref, y_ref, yn_ref, ba_ref, dq_ref, dqn_ref, dk_ref, dkn_ref, dv_ref, dvn_ref, dgcb_ref, dz_ref,
             w_ref, al_ref, dt_ref, win_hbm, dres_ref, dp_ref, dx_ref, dw_ref, dal_ref, ddt_ref, win_ref, win_sem):
        i = pl.program_id(0)

        @pl.when(i == 0)
        def _():
            cp = pltpu.make_async_copy(win_hbm, win_ref, win_sem)
            cp.start()
            cp.wait()

        more = i < n_t - 1
        w = [w_ref[pl.ds(j, 1), :] for j in range(CONV_WIDTH)]
        x_t = x_ref[...]
        _, act_vjp = jax.vjp(_act_qkv, jnp.concatenate([y_ref[...], yn_ref[...]], axis=0))
        ct = tuple(jnp.concatenate([t[...], jnp.where(more, n[...], 0.0)], axis=0)
                   for t, n in ((dq_ref, dqn_ref), (dk_ref, dkn_ref), (dv_ref, dvn_ref)))
        (dy_e,) = act_vjp(ct)
        ahead = [pltpu.roll(dy_e, te - (CONV_WIDTH - 1 - j), 0)[:ts] for j in range(CONV_WIDTH - 1)] + [dy_e[:ts]]
        dx = ahead[CONV_WIDTH - 1] * w[CONV_WIDTH - 1]
        for j in range(CONV_WIDTH - 1):
            dx = dx + ahead[j] * w[j]
        dws = [jnp.sum(ahead[j] * x_t, axis=0, keepdims=True) for j in range(CONV_WIDTH)]
        _, gb_vjp = jax.vjp(lambda ba, al, dt: _act_gb(ba, al, dt, _chunk_tril(ts)), ba_ref[...], al_ref[...], dt_ref[...])
        dba, dal, ddt = gb_vjp(dgcb_ref[...])
        dp = jnp.concatenate([dx.astype(BF16), dz_ref[...], dba.astype(BF16)], axis=1)
        dp_ref[...] = dp
        dx_ref[...] = dres_ref[...] + _bdot(dp, win_ref[...], NT)
        first = i == 0
        for j in range(CONV_WIDTH):
            _acc(dw_ref.at[pl.ds(j, 1), :], first, dws[j])
        _acc(dal_ref, first, dal)
        _acc(ddt_ref, first, ddt)

    tile = pl.BlockSpec((ts, D_MODEL), lambda i: (i, 0))
    nxt = pl.BlockSpec((CONV_HALO, D_MODEL), lambda i: (jnp.minimum((i + 1) * hb, n_t * hb - 1), 0))
    return _call(
        body, "gdn_conv_bwd", (n_t,),
        [pl.BlockSpec((ts, QKV_DIM), lambda i: (i, 0)), pl.BlockSpec((ts, QKV_DIM), lambda i: (i, 0)),
         pl.BlockSpec((CONV_HALO, QKV_DIM), lambda i: (jnp.minimum((i + 1) * hb, n_t * hb - 1), 0)),
         pl.BlockSpec((ts, 128), lambda i: (i, BA_BLOCK)),
         tile, nxt, tile, nxt, tile, nxt, pl.BlockSpec((ts, 128), lambda i: (i, 0)), tile,
         _full((CONV_WIDTH, QKV_DIM)), _row(128), _row(128), ANY, tile],
        [pl.BlockSpec((ts, GDN_IN_PAD), lambda i: (i, 0)), tile, _full((CONV_WIDTH, QKV_DIM)), _row(128), _row(128)],
        [_sds((s_len, GDN_IN_PAD), BF16), _sds((s_len, D_MODEL)), _sds((CONV_WIDTH, QKV_DIM)), _sds((1, 128)), _sds((1, 128))],
        scratch=[pltpu.VMEM(w_in.shape, w_in.dtype), pltpu.SemaphoreType.DMA], sem=("arbitrary",),
    )(proj, y, y, proj, dq, dq, dk, dk, dv, dv, dgcb, dz, conv_w, alog_l, dtb_l, w_in, dres)


def _tri_inv(a_strict):
    ii = lax.broadcasted_iota(jnp.int32, (CHUNK, CHUNK), 0)
    jj = lax.broadcasted_iota(jnp.int32, (CHUNK, CHUNK), 1)
    x = (ii == jj).astype(F32) - a_strict
    pw = _bdot(a_strict, a_strict, BNN)
    for step in range(5):
        x = x + _bdot(x, pw, BNN)
        if step < 4:
            pw = _bdot(pw, pw, BNN)
    return x


@jax.custom_vjp
def _solved(a_strict, rhs, t, sol):
    return sol


def _solved_fwd(a_strict, rhs, t, sol):
    return sol, (t, sol)


def _solved_bwd(res, dsol):
    t, sol = res
    drhs = _mdot(t, dsol, BTN)
    return -_mdot(drhs, sol, BNT), drhs, jnp.zeros_like(t), jnp.zeros_like(sol)


_solved.defvjp(_solved_fwd, _solved_bwd)


def _prep(q, k, v, gc, beta, solve):
    ii = lax.broadcasted_iota(jnp.int32, (CHUNK, CHUNK), 0)
    jj = lax.broadcasted_iota(jnp.int32, (CHUNK, CHUNK), 1)
    causal, strict = ii >= jj, ii > jj
    gc_row = jnp.sum((ii == jj).astype(F32) * gc, axis=1, keepdims=True)
    decay = jnp.where(causal, jnp.exp(jnp.where(causal, gc - gc_row, 0.0)), 0.0)
    kb = k * beta
    a = jnp.where(strict, _bdot(kb, k, BNT) * decay, 0.0)
    eg = jnp.exp(gc)
    sol = solve(a, jnp.concatenate([v * beta, kb * eg], axis=-1))
    qk = _bdot(q, k, BNT) * decay
    last = lax.broadcasted_iota(jnp.int32, (CHUNK, 1), 0) == CHUNK - 1
    g_last = jnp.sum(jnp.where(last, gc, 0.0), axis=1, keepdims=True)
    kd = k * jnp.exp(g_last - gc)
    gl = jnp.exp(g_last) + jnp.zeros((1, 1, HEAD_DIM), F32)
    return sol[..., :HEAD_DIM], sol[..., HEAD_DIM:], qk, q * eg, kd, gl


def _prep_specs(s_len, chunks):
    gl_m = min(PREP_CHUNKS, s_len // CHUNK)
    m = min(chunks, gl_m)
    rows = m * CHUNK
    per = gl_m // m
    hd = pl.BlockSpec((rows, HEAD_DIM), lambda c, h: (c, h))
    gcb = pl.BlockSpec((rows, 128), lambda c, h: (c, 0))
    qk = pl.BlockSpec((None, rows, CHUNK), lambda c, h: (h, c, 0))
    gl = pl.BlockSpec((None, m, HEADS, HEAD_DIM), lambda c, h: (c // per, c % per, 0, 0))
    return rows, m, hd, gcb, qk, gl


def _head_cols(gcb, h, m):
    lane = lax.broadcasted_iota(jnp.int32, gcb.shape, 1)
    pick = lambda at: jnp.sum(jnp.where(lane == at, gcb, 0.0), axis=1, keepdims=True).reshape(m, CHUNK, 1)
    return pick(h + HEADS), pick(h)


def _gdn_prep(q, k, v, gcb):
    s_len = q.shape[0]
    rows, m, hd, gcb_spec, qk_spec, gl_spec = _prep_specs(s_len, PREP_CHUNKS)

    def body(q_ref, k_ref, v_ref, gcb_ref, u_ref, w_ref, qd_ref, kd_ref, qk_ref, gl_ref, t_ref):
        r3 = lambda ref, d: ref[...].reshape(m, CHUNK, d)
        gc, beta = _head_cols(gcb_ref[...], pl.program_id(1), m)

        def solve(a, rhs):
            t = _tri_inv(a)
            t_ref[...] = t.reshape(rows, CHUNK)
            return _mdot(t, rhs, BNN)

        u, w, qk, qd, kd, gl = _prep(r3(q_ref, HEAD_DIM), r3(k_ref, HEAD_DIM), r3(v_ref, HEAD_DIM), gc, beta, solve)
        u_ref[...] = u.reshape(rows, HEAD_DIM)
        w_ref[...] = w.reshape(rows, HEAD_DIM)
        qd_ref[...] = qd.reshape(rows, HEAD_DIM).astype(BF16)
        kd_ref[...] = kd.reshape(rows, HEAD_DIM).astype(BF16)
        qk_ref[...] = qk.reshape(rows, CHUNK).astype(BF16)
        gl_ref[:, pl.ds(pl.program_id(1), 1), :] = gl

    n_g = s_len // rows
    return _call(
        body, "gdn_prep", (n_g, HEADS), [hd, hd, hd, gcb_spec], [hd, hd, hd, hd, qk_spec, gl_spec, qk_spec],
        [_sds((s_len, D_MODEL))] * 2 + [_sds((s_len, D_MODEL), BF16)] * 2
        + [_sds((HEADS, s_len, CHUNK), BF16), _sds((n_g, m, HEADS, HEAD_DIM)), _sds((HEADS, s_len, CHUNK))],
        sem=("parallel", "arbitrary"))(q, k, v, gcb)


def _gdn_prep_bwd(q, k, v, gcb, t_inv, u, w, du, dw, dqd, dkd, dqk, dgl):
    s_len = q.shape[0]
    rows, m, hd, gcb_spec, qk_spec, gl_spec = _prep_specs(s_len, PREP_BWD_CHUNKS)

    def body(q_ref, k_ref, v_ref, gcb_ref, t_ref, u_ref, w_ref, du_ref, dw_ref, dqd_ref, dkd_ref, dqk_ref, dgl_ref,
             dq_ref, dk_ref, dv_ref, dgcb_ref):
        h = pl.program_id(1)
        r3 = lambda ref, d: ref[...].reshape(m, CHUNK, d)
        gc, beta = _head_cols(gcb_ref[...], h, m)
        t = r3(t_ref, CHUNK)
        sol = jnp.concatenate([r3(u_ref, HEAD_DIM), r3(w_ref, HEAD_DIM)], axis=-1)
        fn = lambda q_, k_, v_, gc_, bt_: _prep(q_, k_, v_, gc_, bt_, lambda a, rhs: _solved(a, rhs, t, sol))
        _, vjp = jax.vjp(fn, r3(q_ref, HEAD_DIM), r3(k_ref, HEAD_DIM), r3(v_ref, HEAD_DIM), gc, beta)
        ct = (r3(du_ref, HEAD_DIM), r3(dw_ref, HEAD_DIM), r3(dqk_ref, CHUNK), r3(dqd_ref, HEAD_DIM), r3(dkd_ref, HEAD_DIM),
              dgl_ref[:, pl.ds(h, 1), :] * (1.0 / HEAD_DIM))
        dq, dk, dv, dgc, dbt = vjp(ct)
        dq_ref[...] = dq.reshape(rows, HEAD_DIM)
        dk_ref[...] = dk.reshape(rows, HEAD_DIM)
        dv_ref[...] = dv.reshape(rows, HEAD_DIM)
        lane = lax.broadcasted_iota(jnp.int32, (rows, 128), 1)
        mine = jnp.where(lane == h, dbt.reshape(rows, 1), 0.0) + jnp.where(lane == h + HEADS, dgc.reshape(rows, 1), 0.0)
        _acc(dgcb_ref, h == 0, mine)

    return _call(
        body, "gdn_prep_bwd", (s_len // rows, HEADS),
        [hd, hd, hd, gcb_spec, qk_spec, hd, hd, hd, hd, hd, hd, qk_spec, gl_spec], [hd, hd, hd, gcb_spec],
        [_sds((s_len, D_MODEL))] * 3 + [_sds((s_len, 128))],
        sem=("parallel", "arbitrary"))(q, k, v, gcb, t_inv, u, w, du, dw, dqd, dkd, dqk, dgl)


def _scan_specs(n_c, m, k, reverse):
    n_b = n_c // k
    at = (lambda n: n_b - 1 - n) if reverse else (lambda n: n)
    row = pl.BlockSpec((k * CHUNK, D_MODEL), lambda n: (at(n), 0))
    qk = pl.BlockSpec((HEADS, k * CHUNK, CHUNK), lambda n: (0, at(n), 0))
    gl = pl.BlockSpec((None, k, HEADS, HEAD_DIM), lambda n: (at(n) // (m // k), at(n) % (m // k), 0, 0))
    st = pl.BlockSpec((k, HEADS, HEAD_DIM, HEAD_DIM), lambda n: (at(n), 0, 0, 0))
    return row, qk, gl, st


def _gdn_scan(u, w, qd, kd, qk, gl):
    s_len = u.shape[0]
    n_c = s_len // CHUNK
    k = min(SCAN_CHUNKS, gl.shape[1])
    row, qk_spec, gl_spec, st_spec = _scan_specs(n_c, gl.shape[1], k, False)

    def body(u_ref, w_ref, qd_ref, kd_ref, qk_ref, gl_ref, o_ref, st_ref, state):
        hs = range(HEADS)
        sl = [slice(h * HEAD_DIM, (h + 1) * HEAD_DIM) for h in hs]
        first = pl.program_id(0) == 0
        s_all = [jnp.where(first, 0.0, state[h]) for h in hs]
        for c in range(k):
            rows = slice(c * CHUNK, (c + 1) * CHUNK)
            s_b = [s.astype(BF16) for s in s_all]
            ws = [_bdot(w_ref[rows, sl[h]], s_b[h], NN) for h in hs]
            qs = [_bdot(qd_ref[rows, sl[h]], s_b[h], NN) for h in hs]
            vn = [(u_ref[rows, sl[h]] - ws[h]).astype(BF16) for h in hs]
            outs = [qs[h] + _bdot(qk_ref[h, rows, :], vn[h], NN) for h in hs]
            nxt = [s_all[h] * gl_ref[c, pl.ds(h, 1), :] + _bdot(kd_ref[rows, sl[h]], vn[h], TN) for h in hs]
            for h in hs:
                st_ref[c, h] = s_b[h]
                o_ref[rows, sl[h]] = outs[h]
            s_all = nxt
        for h in hs:
            state[h] = s_all[h]

    return _call(
        body, "gdn_scan", (n_c // k,), [row, row, row, row, qk_spec, gl_spec], [row, st_spec],
        [_sds((s_len, D_MODEL)), _sds((n_c, HEADS, HEAD_DIM, HEAD_DIM), BF16)],
        scratch=[pltpu.VMEM((HEADS, HEAD_DIM, HEAD_DIM), F32)], sem=("arbitrary",))(u, w, qd, kd, qk, gl)


def _gdn_scan_bwd(do, u, w, qd, kd, qk, gl, states):
    s_len = u.shape[0]
    n_c = s_len // CHUNK
    k = min(SCAN_CHUNKS, gl.shape[1])
    row, qk_spec, gl_spec, st_spec = _scan_specs(n_c, gl.shape[1], k, True)

    def body(do_ref, u_ref, w_ref, qd_ref, kd_ref, qk_ref, gl_ref, st_ref,
             du_ref, dw_ref, dqd_ref, dkd_ref, dqk_ref, dgl_ref, dstate):
        hs = range(HEADS)
        sl = [slice(h * HEAD_DIM, (h + 1) * HEAD_DIM) for h in hs]
        first = pl.program_id(0) == 0
        ds_f = [jnp.where(first, 0.0, dstate[h]) for h in hs]
        for c in reversed(range(k)):
            rows = slice(c * CHUNK, (c + 1) * CHUNK)
            s_b = [st_ref[c, h] for h in hs]
            ds_b = [d.astype(BF16) for d in ds_f]
            do_b = [do_ref[rows, sl[h]].astype(BF16) for h in hs]
            w_b = [w_ref[rows, sl[h]].astype(BF16) for h in hs]
            ws = [_bdot(w_b[h], s_b[h], NN) for h in hs]
            dvn = [_bdot(qk_ref[h, rows, :], do_b[h], TN) + _bdot(kd_ref[rows, sl[h]], ds_b[h], NN) for h in hs]
            dqd = [_bdot(do_b[h], s_b[h], NT) for h in hs]
            t_do = [_bdot(qd_ref[rows, sl[h]], do_b[h], TN) for h in hs]
            vn = [(u_ref[rows, sl[h]] - ws[h]).astype(BF16) for h in hs]
            dvn_b = [d.astype(BF16) for d in dvn]
            dw = [-_bdot(dvn_b[h], s_b[h], NT) for h in hs]
            dkd = [_bdot(vn[h], ds_b[h], NT) for h in hs]
            dqk = [_bdot(do_b[h], vn[h], NT) for h in hs]
            t_dv = [_bdot(w_b[h], dvn_b[h], TN) for h in hs]
            for h in hs:
                du_ref[rows, sl[h]], dw_ref[rows, sl[h]], dqd_ref[rows, sl[h]], dkd_ref[rows, sl[h]] = dvn[h], dw[h], dqd[h], dkd[h]
                dqk_ref[h, rows, :] = dqk[h]
                dgl_ref[c, pl.ds(h, 1), :] = jnp.sum(s_b[h].astype(F32) * ds_f[h]) + jnp.zeros((1, HEAD_DIM), F32)
            ds_f = [ds_f[h] * gl_ref[c, pl.ds(h, 1), :] + t_do[h] - t_dv[h] for h in hs]
        for h in hs:
            dstate[h] = ds_f[h]

    return _call(
        body, "gdn_scan_bwd", (n_c // k,), [row, row, row, row, row, qk_spec, gl_spec, st_spec],
        [row, row, row, row, qk_spec, gl_spec],
        [_sds((s_len, D_MODEL))] * 4 + [_sds((HEADS, s_len, CHUNK)), _sds(gl.shape)],
        scratch=[pltpu.VMEM((HEADS, HEAD_DIM, HEAD_DIM), F32)], sem=("arbitrary",))(do, u, w, qd, kd, qk, gl, states)


def _gate_norm(o, z, nw):
    outs = []
    for h in range(HEADS):
        oh = o[:, h * HEAD_DIM:(h + 1) * HEAD_DIM]
        outs.append(oh * lax.rsqrt(jnp.mean(oh * oh, axis=-1, keepdims=True) + RMS_EPS))
    return jnp.concatenate(outs, axis=1) * nw * _silu(z)


def _gdn_out_fwd(o, proj, x, w_out, nw, gain, bias):
    s_len = x.shape[0]
    ts = min(ROW_TILE, s_len)

    def body(o_ref, z_ref, x_ref, w_ref, nw_ref, g_ref, b_ref, y_ref, xhat_ref, rstd_ref):
        on = _gate_norm(o_ref[...], z_ref[...], nw_ref[...])
        r = ALPHA * x_ref[...] + _bdot(on, w_ref[...], NN)
        y_ref[...], xhat_ref[...], rstd_ref[...] = _ln_fwd(r, g_ref[...], b_ref[...])

    tile = pl.BlockSpec((ts, D_MODEL), lambda i: (i, 0))
    row = _row(D_MODEL)
    return _call(
        body, "gdn_out_fwd", (s_len // ts,),
        [tile, pl.BlockSpec((ts, D_MODEL), lambda i: (i, QKV_DIM // D_MODEL)), tile, _full((D_MODEL, D_MODEL)), row, row, row],
        [tile, tile, pl.BlockSpec((ts, 1), lambda i: (i, 0))],
        [_sds((s_len, D_MODEL)), _sds((s_len, D_MODEL)), _sds((s_len, 1))], sem=("parallel",))(o, proj, x, w_out, nw, gain, bias)


def _gdn_out_bwd(dy, xhat, rstd, o, proj, w_out, nw, gain):
    s_len = o.shape[0]
    ts = min(ROW_TILE, s_len)

    def body(dy_ref, xh_ref, rs_ref, o_ref, z_ref, w_ref, nw_ref, g_ref,
             dres_ref, do_ref, dz_ref, on_ref, drb_ref, dg_ref, db_ref, dnw_ref):
        dy_t, xh_t = dy_ref[...], xh_ref[...]
        dr = _ln_bwd(dy_t, xh_t, rs_ref[...], g_ref[...])
        dres_ref[...] = ALPHA * dr
        drb_ref[...] = dr.astype(BF16)
        on, vjp = jax.vjp(_gate_norm, o_ref[...], z_ref[...], nw_ref[...])
        on_ref[...] = on.astype(BF16)
        do, dz, dnw = vjp(_bdot(dr, w_ref[...], NT))
        do_ref[...] = do
        dz_ref[...] = dz.astype(BF16)
        first = pl.program_id(0) == 0
        _acc(dg_ref, first, jnp.sum(dy_t * xh_t, axis=0, keepdims=True))
        _acc(db_ref, first, jnp.sum(dy_t, axis=0, keepdims=True))
        _acc(dnw_ref, first, sum(dnw[:, h * HEAD_DIM:(h + 1) * HEAD_DIM] for h in range(HEADS)))

    tile = pl.BlockSpec((ts, D_MODEL), lambda i: (i, 0))
    row = _row(D_MODEL)
    return _call(
        body, "gdn_out_bwd", (s_len // ts,),
        [tile, tile, pl.BlockSpec((ts, 1), lambda i: (i, 0)), tile,
         pl.BlockSpec((ts, D_MODEL), lambda i: (i, QKV_DIM // D_MODEL)), _full((D_MODEL, D_MODEL)), row, row],
        [tile, tile, tile, tile, tile, row, row, _row(HEAD_DIM)],
        [_sds((s_len, D_MODEL))] * 2 + [_sds((s_len, D_MODEL), BF16)] * 3 + [_sds((1, D_MODEL))] * 2 + [_sds((1, HEAD_DIM))],
        sem=("arbitrary",))(dy, xhat, rstd, o, proj, w_out, nw, gain)


def _adamw(w, g, m, v, name):
    r, c = w.shape
    tr = min(ADAM_ROWS, r)
    pieces = list(g) if isinstance(g, (list, tuple)) else [g]
    n_p = len(pieces)
    blocks = [a.shape[0] // tr for a in pieces]
    first = [sum(blocks[:k]) for k in range(n_p)]

    def body(*refs):
        w_ref, g_refs, (m_ref, v_ref) = refs[0], refs[1:1 + n_p], refs[1 + n_p:3 + n_p]
        go_ref, d_ref, nm_ref, nv_ref = refs[3 + n_p:]
        i = pl.program_id(0)
        g_t = g_refs[0][...]
        for k in range(1, n_p):
            g_t = jnp.where(i >= first[k], g_refs[k][...], g_t)
        go_ref[...] = g_t
        nm = ADAM_B1 * m_ref[...] + (1.0 - ADAM_B1) * g_t
        nv = ADAM_B2 * v_ref[...] + (1.0 - ADAM_B2) * (g_t * g_t)
        m_hat = nm / (1.0 - ADAM_B1 ** ADAM_STEP)
        v_hat = nv / (1.0 - ADAM_B2 ** ADAM_STEP)
        d_ref[...] = -ADAM_LR * (m_hat / (jnp.sqrt(v_hat) + ADAM_EPS) + ADAM_WD * w_ref[...])
        nm_ref[...] = nm
        nv_ref[...] = nv

    tile = pl.BlockSpec((tr, c), lambda i: (i, 0))
    g_specs = [pl.BlockSpec((tr, c), lambda i, k=k: (jnp.clip(i - first[k], 0, blocks[k] - 1), 0)) for k in range(n_p)]
    return _call(body, name, (r // tr,), [tile] + g_specs + [tile] * 2, [tile] * 4, [_sds((r, c))] * 4,
                 sem=("parallel",))(w, *pieces, m, v)


def _assemble_w_in(shards):
    rows = 256
    width = GDN_IN_DIM // N_SHARD

    def body(s_ref, o_ref):
        pad = jnp.zeros((rows, GDN_IN_PAD - GDN_IN_DIM), shards.dtype)
        o_ref[...] = jnp.concatenate([s_ref[j] for j in range(N_SHARD)] + [pad], axis=1)

    return _call(body, "w_in_assemble", (D_MODEL // rows,), [pl.BlockSpec((N_SHARD, rows, width), lambda i: (0, i, 0))],
                 pl.BlockSpec((rows, GDN_IN_PAD), lambda i: (i, 0)), _sds((D_MODEL, GDN_IN_PAD), shards.dtype),
                 sem=("parallel",))(shards)


def _split_w_in(full):
    rows = 256
    width = GDN_IN_DIM // N_SHARD

    def body(f_ref, o_ref):
        f = f_ref[...]
        for j in range(N_SHARD):
            o_ref[j] = f[:, j * width:(j + 1) * width]

    return _call(body, "w_in_split", (D_MODEL // rows,), [pl.BlockSpec((rows, GDN_IN_PAD), lambda i: (i, 0))],
                 pl.BlockSpec((N_SHARD, rows, width), lambda i: (0, i, 0)), _sds((N_SHARD, D_MODEL, width), full.dtype),
                 sem=("parallel",))(full)


def _place():
    x, y, c = lax.axis_index("x"), lax.axis_index("y"), lax.axis_index("c")
    return x, y, c, [(1 - x, y), (x, 1 - y), (1 - x, 1 - y)]


def _row_tile(rows):
    return max(t for t in range(8, min(rows, 640) + 1, 8) if rows % t == 0)


def _place_shard(part, me, dtype, name, layer=0):
    _, _, r, c = part.shape
    tr = _row_tile(r)

    def body(me_ref, p_ref, o_ref):
        o_ref[...] = p_ref[...].astype(dtype)

    return pl.pallas_call(
        body, name=name, out_shape=_sds((N_SHARD, 2, r, c), dtype),
        grid_spec=pltpu.PrefetchScalarGridSpec(
            num_scalar_prefetch=1, grid=(2, r // tr),
            in_specs=[pl.BlockSpec((None, None, tr, c), lambda h, i, me_ref: (layer, h, i, 0))],
            out_specs=pl.BlockSpec((None, None, tr, c), lambda h, i, me_ref: (me_ref[0], h, i, 0))))(me, part)


def _gather_sems(n):
    return [pltpu.SemaphoreType.DMA((6 * n,)), pltpu.SemaphoreType.DMA((6 * n,))]


def _gather_steps(dsts, send_sems, recv_sems):
    n = len(dsts)
    x, y, c, chips = _place()
    me = 2 * x + y
    sibling = (x, y, 1 - c)

    def ici(k, j, slot):
        px, py = chips[j]
        view = dsts[k].at[slot, c]
        return pltpu.make_async_remote_copy(
            src_ref=view, dst_ref=view, send_sem=send_sems.at[6 * k + j],
            recv_sem=recv_sems.at[6 * k + j], device_id=(px, py, c), device_id_type=MESH)

    def d2d(k, j, half):
        px, py = chips[j]
        view = dsts[k].at[2 * px + py, half]
        return pltpu.make_async_remote_copy(
            src_ref=view, dst_ref=view, send_sem=send_sems.at[6 * k + 3 + j], recv_sem=recv_sems.at[6 * k + 3 + j],
            device_id=sibling, device_id_type=MESH)

    def start():
        for k in range(n):
            for j in range(3):
                ici(k, j, me).start()

    def finish():
        fwds = []
        for k in range(n):
            for j, (px, py) in enumerate(chips):
                ici(k, j, 2 * px + py).wait_recv()
                fwds.append(d2d(k, j, c))
                fwds[-1].start()
        for k in range(n):
            for j in range(3):
                d2d(k, j, 1 - c).wait_recv()
        for k in range(n):
            for j in range(3):
                ici(k, j, me).wait_send()
        for cp in fwds:
            cp.wait_send()

    return start, finish


def _all_gather(bufs, name):
    n = len(bufs)

    def body(*refs):
        start, finish = _gather_steps(refs[n:2 * n], *refs[2 * n:])
        start()
        finish()

    return pl.pallas_call(
        body, name=name, out_shape=[_sds(a.shape, a.dtype) for a in bufs],
        in_specs=[ANY] * n, out_specs=[ANY] * n, input_output_aliases={k: k for k in range(n)},
        scratch_shapes=_gather_sems(n))(*bufs)


def _swap_halves(pieces, name):
    n = len(pieces)

    def body(*refs):
        srcs, dsts = refs[:n], refs[n:2 * n]
        send_sems, recv_sems = refs[2 * n:]
        x, y, c, _ = _place()
        copies = []
        for k in range(n):
            hr = pieces[k].shape[1] // 2
            copies.append(pltpu.make_async_remote_copy(
                src_ref=srcs[k].at[:, pl.ds((1 - c) * hr, hr), :], dst_ref=dsts[k],
                send_sem=send_sems.at[k], recv_sem=recv_sems.at[k], device_id=(x, y, 1 - c), device_id_type=MESH))
        for cp in copies:
            cp.start()
        for cp in copies:
            cp.wait()

    return pl.pallas_call(
        body, name=name, out_shape=[_sds((N_SHARD, a.shape[1] // 2, a.shape[2])) for a in pieces],
        in_specs=[ANY] * n, out_specs=[ANY] * n,
        scratch_shapes=[pltpu.SemaphoreType.DMA((n,)), pltpu.SemaphoreType.DMA((n,))])(*pieces)


def _add_half(piece, other, place, dtype, name):
    n, hr, cols = other.shape
    tr = _row_tile(hr)

    def body(pl_ref, a_ref, b_ref, o_ref):
        o_ref[...] = (a_ref[...] + b_ref[...]).astype(dtype)

    tile = pl.BlockSpec((None, tr, cols), lambda s, i, pl_ref: (s, i, 0))
    return pl.pallas_call(
        body, name=name, out_shape=_sds(other.shape, dtype),
        grid_spec=pltpu.PrefetchScalarGridSpec(
            num_scalar_prefetch=1, grid=(n, hr // tr),
            in_specs=[pl.BlockSpec((None, None, tr, cols), lambda s, i, pl_ref: (s, pl_ref[1], i, 0)), tile],
            out_specs=tile))(place, piece.reshape(n, 2, hr, cols), other)


def _scatter_sems(n):
    return [pltpu.SemaphoreType.DMA((3 * n,)), pltpu.SemaphoreType.DMA((3 * n,))]


def _scatter_steps(srcs, dsts, send_sems, recv_sems):
    n = len(srcs)
    x, y, c, chips = _place()
    me = 2 * x + y

    def ici(k, j, src_slot, dst_slot):
        px, py = chips[j]
        return pltpu.make_async_remote_copy(
            src_ref=srcs[k].at[src_slot], dst_ref=dsts[k].at[dst_slot], send_sem=send_sems.at[3 * k + j],
            recv_sem=recv_sems.at[3 * k + j], device_id=(px, py, c), device_id_type=MESH)

    def start():
        for k in range(n):
            for j, (px, py) in enumerate(chips):
                ici(k, j, 2 * px + py, me).start()

    def finish():
        for k in range(n):
            for j, (px, py) in enumerate(chips):
                ici(k, j, me, 2 * px + py).wait_recv()
        for k in range(n):
            for j, (px, py) in enumerate(chips):
                ici(k, j, 2 * px + py, me).wait_send()

    return start, finish


def _scatter_chips(parts, name):
    n = len(parts)

    def body(*refs):
        start, finish = _scatter_steps(refs[:n], refs[n:2 * n], *refs[2 * n:])
        start()
        finish()

    return pl.pallas_call(
        body, name=name, out_shape=[_sds(a.shape, a.dtype) for a in parts],
        in_specs=[ANY] * n, out_specs=[ANY] * n, scratch_shapes=_scatter_sems(n))(*parts)


def _sum_chips(landed, own, place, name):
    _, r, cols = landed.shape
    tr = _row_tile(r)

    def body(pl_ref, q_ref, p_ref, o_ref):
        me = pl_ref[0]
        f = lambda j: jnp.where(me == j, p_ref[...], q_ref[j]).astype(F32)
        o_ref[...] = ((f(0) + f(1)) + f(2)) + f(3)

    return pl.pallas_call(
        body, name=name, out_shape=_sds((2, r, cols)),
        grid_spec=pltpu.PrefetchScalarGridSpec(
            num_scalar_prefetch=1, grid=(r // tr,),
            in_specs=[pl.BlockSpec((N_SHARD, tr, cols), lambda i, pl_ref: (0, i, 0)),
                      pl.BlockSpec((None, tr, cols), lambda i, pl_ref: (pl_ref[0], i, 0))],
            out_specs=pl.BlockSpec((None, tr, cols), lambda i, pl_ref: (pl_ref[1], i, 0))))(place, landed, own)


def _join_halves(bufs):
    n = len(bufs)

    def body(*refs):
        dsts = refs[n:2 * n]
        send_sems, recv_sems = refs[2 * n:]
        x, y, c, _ = _place()
        copies = [pltpu.make_async_remote_copy(
            src_ref=dsts[k].at[c], dst_ref=dsts[k].at[c], send_sem=send_sems.at[k], recv_sem=recv_sems.at[k],
            device_id=(x, y, 1 - c), device_id_type=MESH) for k in range(n)]
        for cp in copies:
            cp.start()
        for cp in copies:
            cp.wait()

    return pl.pallas_call(
        body, name="grads_join_halves", out_shape=[_sds(a.shape) for a in bufs], in_specs=[ANY] * n, out_specs=[ANY] * n,
        input_output_aliases={k: k for k in range(n)},
        scratch_shapes=[pltpu.SemaphoreType.DMA((n,)), pltpu.SemaphoreType.DMA((n,))])(*bufs)


GATHER_F32 = ("ln_gain", "ln_bias", "pool_b", "gdn_conv")
REPLICATED = ("pool_scale", "gdn_a_log", "gdn_dt_bias", "gdn_norm_w", "ple_gate_b")
WEIGHTS = ("ln_gain", "ln_bias", "pool_w", "pool_b", "pool_scale", "gdn_w_in", "gdn_conv", "gdn_a_log", "gdn_dt_bias",
           "gdn_norm_w", "gdn_w_out", "mlp_w1", "mlp_w2", "ple_gate_w", "ple_gate_b", "ple_proj")
SMALL_GRADS = ("ple_proj", "pool_w", "ln_gain", "ln_bias", "pool_b", "gdn_conv") + REPLICATED


def _pack(parts, lanes, row_multiple):
    flat = jnp.concatenate([a.reshape(-1) for a in parts])
    rows = -(-flat.shape[0] // (2 * lanes * row_multiple)) * row_multiple
    return jnp.pad(flat, (0, 2 * rows * lanes - flat.shape[0])).reshape(2, rows, lanes)


def _unpack(flat, shapes):
    out, off = [], 0
    for shp in shapes:
        n = math.prod(shp)
        out.append(flat[..., off:off + n].reshape(flat.shape[:-1] + tuple(shp)))
        off += n
    return out


def _pad_lanes(a, offset, width=128):
    return jnp.pad(a, ((0, 0), (offset, width - offset - a.shape[1])))


def kernel(x, p, ln_gain, ln_bias, pool_w, pool_b, pool_scale, gdn_w_in, gdn_conv, gdn_a_log, gdn_dt_bias, gdn_norm_w, gdn_w_out, mlp_w1, mlp_w2, ple_gate_w, ple_gate_b, ple_proj, loss_target, m_ln_gain, m_ln_bias, m_pool_w, m_pool_b, m_pool_scale, m_gdn_w_in, m_gdn_conv, m_gdn_a_log, m_gdn_dt_bias, m_gdn_norm_w, m_gdn_w_out, m_mlp_w1, m_mlp_w2, m_ple_gate_w, m_ple_gate_b, m_ple_proj, v_ln_gain, v_ln_bias, v_pool_w, v_pool_b, v_pool_scale, v_gdn_w_in, v_gdn_conv, v_gdn_a_log, v_gdn_dt_bias, v_gdn_norm_w, v_gdn_w_out, v_mlp_w1, v_mlp_w2, v_ple_gate_w, v_ple_gate_b, v_ple_proj):
    shard = dict(ln_gain=ln_gain, ln_bias=ln_bias, pool_w=pool_w, pool_b=pool_b, pool_scale=pool_scale, gdn_w_in=gdn_w_in,
                 gdn_conv=gdn_conv, gdn_a_log=gdn_a_log, gdn_dt_bias=gdn_dt_bias, gdn_norm_w=gdn_norm_w, gdn_w_out=gdn_w_out,
                 mlp_w1=mlp_w1, mlp_w2=mlp_w2, ple_gate_w=ple_gate_w, ple_gate_b=ple_gate_b, ple_proj=ple_proj)
    mom = dict(ln_gain=m_ln_gain, ln_bias=m_ln_bias, pool_w=m_pool_w, pool_b=m_pool_b, pool_scale=m_pool_scale,
               gdn_w_in=m_gdn_w_in, gdn_conv=m_gdn_conv, gdn_a_log=m_gdn_a_log, gdn_dt_bias=m_gdn_dt_bias,
               gdn_norm_w=m_gdn_norm_w, gdn_w_out=m_gdn_w_out, mlp_w1=m_mlp_w1, mlp_w2=m_mlp_w2, ple_gate_w=m_ple_gate_w,
               ple_gate_b=m_ple_gate_b, ple_proj=m_ple_proj)
    var = dict(ln_gain=v_ln_gain, ln_bias=v_ln_bias, pool_w=v_pool_w, pool_b=v_pool_b, pool_scale=v_pool_scale,
               gdn_w_in=v_gdn_w_in, gdn_conv=v_gdn_conv, gdn_a_log=v_gdn_a_log, gdn_dt_bias=v_gdn_dt_bias,
               gdn_norm_w=v_gdn_norm_w, gdn_w_out=v_gdn_w_out, mlp_w1=v_mlp_w1, mlp_w2=v_mlp_w2, ple_gate_w=v_ple_gate_w,
               ple_gate_b=v_ple_gate_b, ple_proj=v_ple_proj)

    xi, yi, ci = lax.axis_index("x"), lax.axis_index("y"), lax.axis_index("c")
    me = (2 * xi + yi).reshape(1).astype(jnp.int32)
    place = jnp.stack([2 * xi + yi, ci]).astype(jnp.int32)
    early = [("mlp_w1", 0), ("mlp_w2", 0), ("ple_gate_w", 0), ("ple_proj", 0), ("pool_w", 0)]
    late = [("mlp_w1", 1), ("mlp_w2", 1), ("ple_gate_w", 1), ("ple_proj", 1), ("gdn_w_out", 0), ("gdn_w_in", 0)]
    halved = lambda n: shard[n].reshape(shard[n].shape[0], 2, -1, shard[n].shape[-1])
    placed = lambda ops, tag: [_place_shard(halved(n), me, BF16, f"place_{tag}_{n}", l) for n, l in ops]
    small_in = _place_shard(_pack([shard[n] for n in GATHER_F32], 128, 8)[None], me, F32, "place_small")
    got_early = _all_gather(placed(early, "early") + [small_in], "weights_all_gather_early")
    placed_late = placed(late, "late")
    st = dict(zip(GATHER_F32, _unpack(got_early[-1].reshape(N_SHARD, -1), [shard[n].shape for n in GATHER_F32])))

    cat_last = lambda a: jnp.moveaxis(a, 0, -2).reshape(a.shape[1:-1] + (N_SHARD * a.shape[-1],))
    gain = cat_last(st["ln_gain"])
    bias = cat_last(st["ln_bias"])
    wp = got_early[4].reshape(N_SHARD, 4, POOL_GROUP // N_SHARD, POOL_GROUP)
    pb = cat_last(st["pool_b"]).reshape(1, D_MODEL)
    ps = pool_scale
    conv_w = cat_last(st["gdn_conv"])[0]
    merged = lambda g: g.reshape(N_SHARD, -1, g.shape[-1])
    mlp_w = lambda i, got: (merged(got[0]), merged(got[1]), merged(got[2]), ple_gate_b[i:i + 1], merged(got[3]))
    alog_l = _pad_lanes(gdn_a_log, HEADS)
    dtb_l = _pad_lanes(gdn_dt_bias, HEADS)
    nw = jnp.tile(gdn_norm_w, (1, HEADS))
    ln = lambda i, k: (gain[i, k][None], bias[i, k][None])

    x0 = x[0]
    p0, p1 = p[0, 0], p[1, 0]

    x1, xh1, rs1 = _pool_fwd(x0, wp, pb, ps, *ln(0, 0))
    (x2, xh2, rs2, a0, xb0), got_late = _mlp_fwd(x1, p0, *mlp_w(0, got_early), *ln(0, 1), "mlp_fwd_0", gather=placed_late)
    w_out = got_late[4].reshape(D_MODEL, D_MODEL)
    w_in = _assemble_w_in(merged(got_late[5]))
    proj, y_conv, q, k, v, gcb = _conv_fwd(x2, w_in, conv_w, alog_l, dtb_l)
    u, w, qd, kd, qk, gl, t_inv = _gdn_prep(q, k, v, gcb)
    o, states = _gdn_scan(u, w, qd, kd, qk, gl)
    x3, xh3, rs3 = _gdn_out_fwd(o, proj, x2, w_out, nw, *ln(1, 0))
    (dy4, xh4, rs4, a1, xb1, loss_l), _ = _mlp_fwd(x3, p1, *mlp_w(1, got_late), *ln(1, 1), "mlp_fwd_1", target=loss_target[0])

    g_gain = [[None, None], [None, None]]
    g_bias = [[None, None], [None, None]]

    def mlp_grads(i, dy, xh, rs, x_mid, xb, a, p_i, got, scatter=()):
        (dx, dh, dzg, dpp, drb, dg, db, dgb), landed = _mlp_bwd(
            dy, xh, rs, x_mid, a, p_i, *mlp_w(i, got), ln(i, 1)[0], f"mlp_bwd_{i}", scatter=scatter)
        g_gain[i][1], g_bias[i][1] = dg, db
        return dx, dict(
            mlp_w1=_wgrad(xb, dh, f"dw1_{i}", stack_cols=True), mlp_w2=_wgrad(a, drb, f"dw2_{i}").reshape(N_SHARD, -1, D_MODEL),
            ple_gate_w=_wgrad(xb, dzg, f"dgate_w_{i}").reshape(N_SHARD, -1, D_MODEL),
            ple_proj=jnp.moveaxis(_wgrad(p_i, dpp, f"dproj_{i}").reshape(PLE_DIM, N_SHARD, -1), 1, 0), ple_gate_b=dgb), landed

    def chip_sums(pieces, wire, tag):
        others = _swap_halves(pieces, f"grads_swap_halves_{tag}")
        return [_add_half(a, b, place, t, f"grads_add_half_{tag}{i}") for i, (a, b, t) in enumerate(zip(pieces, others, wire))]

    dx3, gl1, _ = mlp_grads(1, dy4, xh4, rs4, x3, xb1, a1, p1, got_late)
    dres, do, dz, on_b, drb3, g_gain[1][0], g_bias[1][0], d_nw = _gdn_out_bwd(dx3, xh3, rs3, o, proj, w_out, nw, ln(1, 0)[0])
    d_wout = _wgrad(on_b, drb3, "dw_out").reshape(N_SHARD, -1, D_MODEL)
    du, dw, dqd, dkd, dqk, dgl = _gdn_scan_bwd(do, u, w, qd, kd, qk, gl, states)
    dq, dk, dv, dgcb = _gdn_prep_bwd(q, k, v, gcb, t_inv, u, w, du, dw, dqd, dkd, dqk, dgl)
    dproj, dx2, d_conv, d_alog_l, d_dtb_l = _conv_bwd(proj, y_conv, dq, dk, dv, dgcb, dz, conv_w, alog_l, dtb_l, w_in, dres)
    d_win = _split_w_in(_wgrad(x2, dproj, "dw_in"))
    sums_late = chip_sums([gl1["mlp_w1"], gl1["mlp_w2"], gl1["ple_gate_w"], d_wout, d_win], [BF16] * 5, "late")
    dx1, gl0, landed_late = mlp_grads(0, dx2, xh2, rs2, x1, xb0, a0, p0, got_early, scatter=sums_late)
    sums_early = chip_sums([gl0["mlp_w1"], gl0["mlp_w2"], gl0["ple_gate_w"]], [BF16] * 3, "early")
    (dx0, g_gain[0][0], g_bias[0][0], d_ps, d_pb, d_wp), landed_early = _pool_bwd(
        dx1, xh1, rs1, x0, wp, pb, ps, ln(0, 0)[0], scatter=sums_early)

    split_last = lambda a: jnp.moveaxis(a.reshape(a.shape[:-1] + (N_SHARD, a.shape[-1] // N_SHARD)), -2, 0)
    small_st = dict(
        ple_proj=jnp.stack([gl0["ple_proj"], gl1["ple_proj"]], axis=1),
        pool_w=jnp.moveaxis(d_wp.reshape(4, N_SHARD, POOL_GROUP // N_SHARD, POOL_GROUP), 1, 0)[:, None],
        ln_gain=split_last(jnp.stack([jnp.concatenate(r, axis=0) for r in g_gain])),
        ln_bias=split_last(jnp.stack([jnp.concatenate(r, axis=0) for r in g_bias])),
        pool_b=split_last(d_pb.reshape(1, 4, POOL_GROUP)),
        gdn_conv=split_last(d_conv)[:, None],
    )
    rep = dict(pool_scale=d_ps, gdn_a_log=d_alog_l[:, HEADS:2 * HEADS], gdn_dt_bias=d_dtb_l[:, HEADS:2 * HEADS],
               gdn_norm_w=d_nw, ple_gate_b=jnp.concatenate([gl0["ple_gate_b"], gl1["ple_gate_b"]], axis=0))
    for n in REPLICATED:
        small_st[n] = jnp.broadcast_to(rep[n][None], (N_SHARD,) + rep[n].shape)
    small_flat = jnp.concatenate([small_st[n].reshape(N_SHARD, -1) for n in SMALL_GRADS], axis=1)
    small_rows = -(-small_flat.shape[1] // (16 * LANES)) * 16
    small_piece = jnp.pad(small_flat, ((0, 0), (0, small_rows * LANES - small_flat.shape[1]))).reshape(N_SHARD, small_rows, LANES)

    sums_small = chip_sums([small_piece], [F32], "small")
    landed_small = _scatter_chips(sums_small, "grads_scatter_chips_small")
    red = _join_halves([_sum_chips(q_, p_, place, f"grads_sum_chips_{i}") for i, (q_, p_) in
                        enumerate(zip(list(landed_early) + list(landed_small) + list(landed_late),
                                      sums_early + sums_small + sums_late))])
    red = [r.reshape(-1, r.shape[-1]) for r in red]
    grads = dict(mlp_w1=[red[0], red[4]], mlp_w2=[red[1], red[5]], ple_gate_w=[red[2], red[6]], gdn_w_out=red[7], gdn_w_in=red[8])
    grads.update(zip(SMALL_GRADS, _unpack(red[3].reshape(-1), [shard[n].shape for n in SMALL_GRADS])))

    delta, new_m, new_v = {}, {}, {}
    small = [n for n in WEIGHTS if shard[n].size < 128 * 128]
    for n in WEIGHTS:
        if n in small:
            continue
        to2d = lambda a, n=n: a.reshape(-1, shard[n].shape[-1])
        g_n = grads[n] if isinstance(grads[n], list) else to2d(grads[n])
        g2, d2, m2, v2 = _adamw(to2d(shard[n]), g_n, to2d(mom[n]), to2d(var[n]), "adamw_" + n)
        grads[n], delta[n], new_m[n], new_v[n] = (t.reshape(shard[n].shape) for t in (g2, d2, m2, v2))
    pk = lambda d: _pack([d[n] for n in small], 128, 8).reshape(-1, 128)
    _, d2, m2, v2 = _adamw(pk(shard), pk(grads), pk(mom), pk(var), "adamw_small")
    for dst, t in ((delta, d2), (new_m, m2), (new_v, v2)):
        dst.update(zip(small, _unpack(t.reshape(-1), [shard[n].shape for n in small])))

    loss = lax.psum(loss_l[0, 0], ("x", "y", "c"))
    return (loss, dx0[None], *[grads[n] for n in WEIGHTS], *[delta[n] for n in WEIGHTS],
            *[new_m[n] for n in WEIGHTS], *[new_v[n] for n in WEIGHTS])
```

```python
import math

import jax
import jax.numpy as jnp
from jax import lax
from jax.experimental import pallas as pl
from jax.experimental.pallas import tpu as pltpu

F32 = jnp.float32
BF16 = jnp.bfloat16

D_MODEL = 1024
D_FF = 4096
PLE_DIM = 256
N_SHARD = 4
POOL_WINDOWS = (2, 4, 8, 16)
POOL_GROUP = 256
POOL_HALO = 16
HEADS = 8
HEAD_DIM = 128
CHUNK = 64
CONV_WIDTH = 4
CONV_HALO = 8
QKV_DIM = 3 * D_MODEL
GDN_IN_DIM = QKV_DIM + D_MODEL + 2 * HEADS
GDN_IN_PAD = 4224
BA_BLOCK = (QKV_DIM + D_MODEL) // 128
ALPHA = (2.0 * 2) ** 0.25
LN_EPS = 1e-5
RMS_EPS = 1e-6
L2_EPS = 1e-6
ADAM_LR, ADAM_B1, ADAM_B2, ADAM_EPS, ADAM_WD, ADAM_STEP = 0.001, 0.9, 0.999, 1e-08, 0.01, 10

ROW_TILE = 512
CONV_TILE = 256
PREP_CHUNKS = 32
PREP_BWD_CHUNKS = 16
SCAN_CHUNKS = 8
LANES = 1024
ADAM_ROWS = 256

NN = (((1,), (0,)), ((), ()))
NT = (((1,), (1,)), ((), ()))
TN = (((0,), (0,)), ((), ()))
BNN = (((2,), (1,)), ((0,), (0,)))
BNT = (((2,), (2,)), ((0,), (0,)))
BTN = (((1,), (1,)), ((0,), (0,)))
MESH = pl.DeviceIdType.MESH
ANY = pl.BlockSpec(memory_space=pl.ANY)


def _bdot(a, b, dims):
    return lax.dot_general(a.astype(BF16), b.astype(BF16), dims, preferred_element_type=F32)


def _hdot(a, b, dims):
    return lax.dot_general(a, b, dims, precision=lax.Precision.HIGHEST, preferred_element_type=F32)


def _mdot(a, b, dims):
    return lax.dot_general(a, b, dims, precision=lax.Precision.HIGH, preferred_element_type=F32)


def _sigmoid(x):
    return 0.5 * jnp.tanh(0.5 * x) + 0.5


def _silu(x):
    return x * _sigmoid(x)


def _softplus(x):
    return jnp.maximum(x, 0.0) + jnp.log1p(jnp.exp(-jnp.abs(x)))


def _call(body, name, grid, in_specs, out_specs, out_shape, scratch=(), sem=None, aliases=None):
    params = pltpu.CompilerParams(dimension_semantics=sem) if sem else None
    return pl.pallas_call(
        body, name=name, grid=grid, in_specs=in_specs, out_specs=out_specs, out_shape=out_shape,
        scratch_shapes=list(scratch), compiler_params=params, input_output_aliases=aliases or {})


def _row(d):
    return pl.BlockSpec((1, d), lambda *_: (0, 0))


def _full(shape):
    n = len(shape)
    return pl.BlockSpec(shape, lambda *_: (0,) * n)


def _sds(shape, dtype=F32):
    return jax.ShapeDtypeStruct(shape, dtype)


def _ln_fwd(r, gain, bias):
    mu = jnp.mean(r, axis=-1, keepdims=True)
    xc = r - mu
    rstd = lax.rsqrt(jnp.mean(xc * xc, axis=-1, keepdims=True) + LN_EPS)
    xhat = xc * rstd
    return xhat * gain + bias, xhat, rstd


def _ln_bwd(dy, xhat, rstd, gain):
    dxh = dy * gain
    m1 = jnp.mean(dxh, axis=-1, keepdims=True)
    m2 = jnp.mean(dxh * xhat, axis=-1, keepdims=True)
    return rstd * (dxh - m1 - xhat * m2)


def _acc(ref, first, val):
    @pl.when(first)
    def _():
        ref[...] = val

    @pl.when(jnp.logical_not(first))
    def _():
        ref[...] += val


def _pooled_groups(xe, t0, ts):
    pos = (t0 + lax.broadcasted_iota(jnp.int32, (ts, 1), 0) + 1).astype(F32)
    outs = []
    for gi, win in enumerate(POOL_WINDOWS):
        xs = xe[:, gi * POOL_GROUP:(gi + 1) * POOL_GROUP]
        s, k = xs, 1
        while k < win:
            s = s + pltpu.roll(s, k, 0)
            k *= 2
        mean = s[POOL_HALO:] / jnp.minimum(pos, float(win))
        outs.append(mean - xs[POOL_HALO:])
    return outs


def _pool_groups_w(w_ref):
    return [jnp.concatenate([w_ref[s, g] for s in range(N_SHARD)], axis=0) for g in range(4)]


def _pool_fwd(x, wp, pb, ps, gain, bias):
    s_len = x.shape[0]
    ts = min(ROW_TILE, s_len)
    hb = ts // POOL_HALO

    def body(x_ref, halo_ref, w_ref, pb_ref, ps_ref, g_ref, b_ref, y_ref, xhat_ref, rstd_ref):
        i = pl.program_id(0)
        x_t = x_ref[...]
        halo = jnp.where(i > 0, halo_ref[...], 0.0)
        pooled = _pooled_groups(jnp.concatenate([halo, x_t], axis=0), i * ts, ts)
        wg = _pool_groups_w(w_ref)
        y = jnp.concatenate([_bdot(pooled[g], wg[g], NN) for g in range(4)], axis=1) + pb_ref[...]
        r = ALPHA * x_t + y * ps_ref[...]
        y_ref[...], xhat_ref[...], rstd_ref[...] = _ln_fwd(r, g_ref[...], b_ref[...])

    tile = pl.BlockSpec((ts, D_MODEL), lambda i: (i, 0))
    return _call(
        body, "pool_fwd", (s_len // ts,),
        [tile, pl.BlockSpec((POOL_HALO, D_MODEL), lambda i: (jnp.maximum(i * hb - 1, 0), 0)),
         _full(wp.shape), _row(D_MODEL), _row(D_MODEL), _row(D_MODEL), _row(D_MODEL)],
        [tile, tile, pl.BlockSpec((ts, 1), lambda i: (i, 0))],
        [_sds((s_len, D_MODEL)), _sds((s_len, D_MODEL)), _sds((s_len, 1))],
        sem=("parallel",))(x, x, wp, pb, ps, gain, bias)


def _pool_bwd(dy, xhat, rstd, x, wp, pb, ps, gain, scatter=()):
    s_len = x.shape[0]
    ts = min(ROW_TILE, s_len)
    hb = ts // POOL_HALO
    n_t = s_len // ts
    ne = ts + POOL_HALO
    n_s = len(scatter)

    def body(*refs):
        dy_ref, dyn_ref, xh_ref, xhn_ref, rs_ref, rsn_ref, x_ref, xp_ref, w_ref, pb_ref, ps_ref, g_ref = refs[:12]
        dx_ref, dg_ref, db_ref, dps_ref, dpb_ref, dw_ref = refs[12 + n_s:18 + n_s]
        i = pl.program_id(0)
        if n_s:
            start, finish = _scatter_steps(refs[12:12 + n_s], refs[18 + n_s:18 + 2 * n_s], *refs[18 + 2 * n_s:])
            pl.when(i == 0)(start)
        more = i < n_t - 1
        dy_t, xh_t = dy_ref[...], xh_ref[...]
        dy_e = jnp.concatenate([dy_t, jnp.where(more, dyn_ref[...], 0.0)], axis=0)
        xh_e = jnp.concatenate([xh_t, xhn_ref[...]], axis=0)
        rs_e = jnp.concatenate([rs_ref[...], rsn_ref[...]], axis=0)
        dr_e = _ln_bwd(dy_e, xh_e, rs_e, g_ref[...])
        dyy_e = dr_e * ps_ref[...]
        pos_e = (i * ts + lax.broadcasted_iota(jnp.int32, (ne, 1), 0) + 1).astype(F32)
        dxs = []
        wg = _pool_groups_w(w_ref)
        for gi, win in enumerate(POOL_WINDOWS):
            sl = slice(gi * POOL_GROUP, (gi + 1) * POOL_GROUP)
            dpool = _bdot(dyy_e[:, sl], wg[gi], NT)
            s, k = dpool / jnp.minimum(pos_e, float(win)), 1
            while k < win:
                s = s + pltpu.roll(s, ne - k, 0)
                k *= 2
            dxs.append(s[:ts] - dpool[:ts])
        dx_ref[...] = ALPHA * dr_e[:ts] + jnp.concatenate(dxs, axis=1)

        x_t = x_ref[...]
        halo = jnp.where(i > 0, xp_ref[...], 0.0)
        pooled = _pooled_groups(jnp.concatenate([halo, x_t], axis=0), i * ts, ts)
        y = jnp.concatenate([_bdot(pooled[g], wg[g], NN) for g in range(4)], axis=1) + pb_ref[...]
        dr_t, dyy_t = dr_e[:ts], dyy_e[:ts]
        first = i == 0
        _acc(dg_ref, first, jnp.sum(dy_t * xh_t, axis=0, keepdims=True))
        _acc(db_ref, first, jnp.sum(dy_t, axis=0, keepdims=True))
        _acc(dps_ref, first, jnp.sum(dr_t * y, axis=0, keepdims=True))
        _acc(dpb_ref, first, jnp.sum(dyy_t, axis=0, keepdims=True))
        for g in range(4):
            _acc(dw_ref.at[g], first, _bdot(pooled[g], dyy_t[:, g * POOL_GROUP:(g + 1) * POOL_GROUP], TN))
        if n_s:
            pl.when(i == n_t - 1)(finish)

    tile = pl.BlockSpec((ts, D_MODEL), lambda i: (i, 0))
    nxt = pl.BlockSpec((POOL_HALO, D_MODEL), lambda i: (jnp.minimum((i + 1) * hb, n_t * hb - 1), 0))
    prv = pl.BlockSpec((POOL_HALO, D_MODEL), lambda i: (jnp.maximum(i * hb - 1, 0), 0))
    rs_t = pl.BlockSpec((ts, 1), lambda i: (i, 0))
    rs_n = pl.BlockSpec((POOL_HALO, 1), lambda i: (jnp.minimum((i + 1) * hb, n_t * hb - 1), 0))
    row = _row(D_MODEL)
    out = _call(
        body, "pool_bwd", (n_t,),
        [tile, nxt, tile, nxt, rs_t, rs_n, tile, prv, _full(wp.shape), row, row, row] + [ANY] * n_s,
        [tile, row, row, row, row, _full((4, POOL_GROUP, POOL_GROUP))] + [ANY] * n_s,
        [_sds((s_len, D_MODEL))] + [_sds((1, D_MODEL))] * 4 + [_sds((4, POOL_GROUP, POOL_GROUP))]
        + [_sds(t.shape, t.dtype) for t in scatter],
        scratch=_scatter_sems(n_s) if n_s else [], sem=("arbitrary",),
    )(dy, dy, xhat, xhat, rstd, rstd, x, x, wp, pb, ps, gain, *scatter)
    return out[:6], out[6:]


def _mlp_weight_specs():
    fc = D_FF // N_SHARD
    return [pl.BlockSpec((None, D_MODEL, fc), lambda i, j: (j, 0, 0)),
            pl.BlockSpec((None, fc, D_MODEL), lambda i, j: (j, 0, 0)),
            _full((N_SHARD, D_MODEL // N_SHARD, D_MODEL)),
            _row(D_MODEL),
            _full((N_SHARD, PLE_DIM, D_MODEL // N_SHARD))]


def _gate_w(gw_ref):
    return gw_ref[...].reshape(D_MODEL, D_MODEL)


def _ple_proj(pj_ref):
    return jnp.concatenate([pj_ref[s] for s in range(N_SHARD)], axis=1)


def _mlp_fwd(x, p, w1s, w2s, gw, gb, proj, gain, bias, name, gather=(), target=None):
    s_len = x.shape[0]
    ts = min(ROW_TILE, s_len)
    n_i = s_len // ts
    n_g = len(gather)
    has_t = target is not None
    n_in, n_out = 9 + has_t, 5 + has_t

    def body(*refs):
        x_ref, p_ref, w1_ref, w2_ref, gw_ref, gb_ref, pj_ref, g_ref, b_ref = refs[:9]
        y_ref, xhat_ref, rstd_ref, a_ref, xbo_ref = refs[n_in + n_g:n_in + n_g + 5]
        acc_ref, xb_ref = refs[n_in + n_out + 2 * n_g:n_in + n_out + 2 * n_g + 2]
        i, j = pl.program_id(0), pl.program_id(1)
        if n_g:
            start, finish = _gather_steps(refs[n_in + n_out + n_g:n_in + n_out + 2 * n_g], *refs[n_in + n_out + 2 * n_g + 2:])
            pl.when((i == 0) & (j == 0))(start)

        @pl.when(j == 0)
        def _():
            x_t = x_ref[...]
            xb_ref[...] = x_t.astype(BF16)
            xbo_ref[...] = x_t.astype(BF16)
            gate = _sigmoid(_bdot(x_t, _gate_w(gw_ref), NN) + gb_ref[...])
            acc_ref[...] = ALPHA * x_t + gate * _bdot(p_ref[...], _ple_proj(pj_ref), NN)

        h = jnp.maximum(_bdot(xb_ref[...], w1_ref[...], NN), 0.0)
        a = (h * h).astype(BF16)
        a_ref[...] = a
        acc_ref[...] += _bdot(a, w2_ref[...], NN)

        @pl.when(j == N_SHARD - 1)
        def _():
            y, xhat_ref[...], rstd_ref[...] = _ln_fwd(acc_ref[...], g_ref[...], b_ref[...])
            if has_t:
                err = y - refs[9][...]
                y_ref[...] = err * (1.0 / D_MODEL)
                part = 0.5 * jnp.sum(jnp.mean(err * err, axis=-1, keepdims=True))
                _acc(refs[n_in + n_g + 5], i == 0, part + jnp.zeros((1, 128), F32))
            else:
                y_ref[...] = y

        if n_g:
            pl.when((i == n_i - 1) & (j == N_SHARD - 1))(finish)

    tile = pl.BlockSpec((ts, D_MODEL), lambda i, j: (i, 0))
    row = _row(D_MODEL)
    out = _call(
        body, name, (n_i, N_SHARD),
        [tile, pl.BlockSpec((ts, PLE_DIM), lambda i, j: (i, 0))] + _mlp_weight_specs() + [row, row] + [tile] * has_t
        + [ANY] * n_g,
        [tile, tile, pl.BlockSpec((ts, 1), lambda i, j: (i, 0)), pl.BlockSpec((ts, D_FF // N_SHARD), lambda i, j: (i, j)), tile]
        + [_row(128)] * has_t + [ANY] * n_g,
        [_sds((s_len, D_MODEL)), _sds((s_len, D_MODEL)), _sds((s_len, 1)), _sds((s_len, D_FF), BF16), _sds((s_len, D_MODEL), BF16)]
        + [_sds((1, 128))] * has_t + [_sds(a.shape, a.dtype) for a in gather],
        scratch=[pltpu.VMEM((ts, D_MODEL), F32), pltpu.VMEM((ts, D_MODEL), BF16)] + (_gather_sems(n_g) if n_g else []),
        sem=("arbitrary", "arbitrary"), aliases={n_in + k: n_out + k for k in range(n_g)},
    )(x, p, w1s, w2s, gw, gb, proj, gain, bias, *([target] if has_t else []), *gather)
    return out[:n_out], out[n_out:]


def _mlp_bwd(dy, xhat, rstd, x, a, p, w1s, w2s, gw, gb, proj, gain, name, scatter=()):
    s_len = x.shape[0]
    ts = min(ROW_TILE, s_len)
    fc = D_FF // N_SHARD
    n_i = s_len // ts
    n_s = len(scatter)

    def body(*refs):
        dy_ref, xh_ref, rs_ref, x_ref, a_ref, p_ref, w1_ref, w2_ref, gw_ref, gb_ref, pj_ref, g_ref = refs[:12]
        dx_ref, dh_ref, dzg_ref, dpp_ref, drb_ref, dg_ref, db_ref, dgb_ref = refs[12 + n_s:20 + n_s]
        acc_ref, dr_ref = refs[20 + 2 * n_s:22 + 2 * n_s]
        i, j = pl.program_id(0), pl.program_id(1)
        if n_s:
            start, finish = _scatter_steps(refs[12:12 + n_s], refs[20 + n_s:20 + 2 * n_s], *refs[22 + 2 * n_s:])
            pl.when((i == 0) & (j == 0))(start)

        @pl.when(j == 0)
        def _():
            dy_t, xh_t, x_t = dy_ref[...], xh_ref[...], x_ref[...]
            dr = _ln_bwd(dy_t, xh_t, rs_ref[...], g_ref[...])
            drb = dr.astype(BF16)
            dr_ref[...] = drb
            drb_ref[...] = drb
            gw_full = _gate_w(gw_ref)
            gate = _sigmoid(_bdot(x_t, gw_full, NN) + gb_ref[...])
            pp = _bdot(p_ref[...], _ple_proj(pj_ref), NN)
            dzg = dr * pp * gate * (1.0 - gate)
            dzg_ref[...] = dzg.astype(BF16)
            dpp_ref[...] = (dr * gate).astype(BF16)
            acc_ref[...] = ALPHA * dr + _bdot(dzg, gw_full, NT)
            first = i == 0
            _acc(dg_ref, first, jnp.sum(dy_t * xh_t, axis=0, keepdims=True))
            _acc(db_ref, first, jnp.sum(dy_t, axis=0, keepdims=True))
            _acc(dgb_ref, first, jnp.sum(dzg, axis=0, keepdims=True))

        dh = (_bdot(dr_ref[...], w2_ref[...], NT) * (2.0 * jnp.sqrt(a_ref[...].astype(F32)))).astype(BF16)
        dh_ref[...] = dh
        acc_ref[...] += _bdot(dh, w1_ref[...], NT)

        @pl.when(j == N_SHARD - 1)
        def _():
            dx_ref[...] = acc_ref[...]

        if n_s:
            pl.when((i == n_i - 1) & (j == N_SHARD - 1))(finish)

    tile = pl.BlockSpec((ts, D_MODEL), lambda i, j: (i, 0))
    ftile = pl.BlockSpec((ts, fc), lambda i, j: (i, j))
    row = _row(D_MODEL)
    out = _call(
        body, name, (n_i, N_SHARD),
        [tile, tile, pl.BlockSpec((ts, 1), lambda i, j: (i, 0)), tile, ftile, pl.BlockSpec((ts, PLE_DIM), lambda i, j: (i, 0))]
        + _mlp_weight_specs() + [row] + [ANY] * n_s,
        [tile, ftile, tile, tile, tile, row, row, row] + [ANY] * n_s,
        [_sds((s_len, D_MODEL)), _sds((s_len, D_FF), BF16)]
        + [_sds((s_len, D_MODEL), BF16)] * 3 + [_sds((1, D_MODEL))] * 3 + [_sds(t.shape, t.dtype) for t in scatter],
        scratch=[pltpu.VMEM((ts, D_MODEL), F32), pltpu.VMEM((ts, D_MODEL), BF16)] + (_scatter_sems(n_s) if n_s else []),
        sem=("arbitrary", "arbitrary"))(dy, xhat, rstd, x, a, p, w1s, w2s, gw, gb, proj, gain, *scatter)
    return out[:8], out[8:]


def _wgrad(a, b, name, stack_cols=False):
    s_len, m = a.shape
    n = b.shape[1]
    ts = min(2048 if a.dtype == BF16 and b.dtype == BF16 else 1024, s_len)
    tm = min(m, 1024)
    tn = n // N_SHARD if stack_cols else (1408 if n == GDN_IN_PAD else min(n, 1024))
    n_s = s_len // ts

    def body(a_ref, b_ref, o_ref):
        _acc(o_ref, pl.program_id(2) == 0, _bdot(a_ref[...], b_ref[...], TN))

    if stack_cols:
        out_spec = pl.BlockSpec((None, tm, tn), lambda mi, nj, s: (nj, mi, 0))
        out_shape = _sds((N_SHARD, m, tn))
    else:
        out_spec = pl.BlockSpec((tm, tn), lambda mi, nj, s: (mi, nj))
        out_shape = _sds((m, n))
    return _call(
        body, name, (m // tm, n // tn, n_s),
        [pl.BlockSpec((ts, tm), lambda mi, nj, s: (s, mi)), pl.BlockSpec((ts, tn), lambda mi, nj, s: (s, nj))],
        out_spec, out_shape, sem=("parallel", "parallel", "arbitrary"))(a, b)


def _act_q(y):
    a = _silu(y)
    return a * (lax.rsqrt(jnp.sum(a * a, axis=-1, keepdims=True) + L2_EPS) * HEAD_DIM ** -0.5)


def _act_k(y):
    a = _silu(y)
    return a * lax.rsqrt(jnp.sum(a * a, axis=-1, keepdims=True) + L2_EPS)


_ACT_HEAD = (_act_q, _act_k, _silu)


def _act_gb(ba, alog_l, dtb_l, tril):
    lane = lax.broadcasted_iota(jnp.int32, ba.shape, 1)
    g = jnp.where((lane >= HEADS) & (lane < 2 * HEADS), -jnp.exp(alog_l) * _softplus(ba + dtb_l), 0.0)
    return jnp.where(lane < HEADS, _sigmoid(ba), _hdot(tril, g, NN))


def _chunk_tril(t):
    ii = lax.broadcasted_iota(jnp.int32, (t, t), 0)
    jj = lax.broadcasted_iota(jnp.int32, (t, t), 1)
    return ((ii // CHUNK == jj // CHUNK) & (ii >= jj)).astype(F32)


def _conv_rows(xe, w, n_rows):
    y = xe[CONV_HALO:CONV_HALO + n_rows] * w[CONV_WIDTH - 1]
    for j in range(CONV_WIDTH - 1):
        y = y + pltpu.roll(xe, CONV_WIDTH - 1 - j, 0)[CONV_HALO:CONV_HALO + n_rows] * w[j]
    return y


def _conv_fwd(x, w_in, conv_w, alog_l, dtb_l):
    s_len = x.shape[0]
    ts = min(CONV_TILE, s_len)
    hb = ts // CONV_HALO

    def body(x_ref, xp_ref, w_ref, al_ref, dt_ref, win_hbm, proj_ref, y_ref, q_ref, k_ref, v_ref, gcb_ref, win_ref, win_sem):
        i = pl.program_id(0)

        @pl.when(i == 0)
        def _():
            cp = pltpu.make_async_copy(win_hbm, win_ref, win_sem)
            cp.start()
            cp.wait()

        proj_ref[...] = _bdot(x_ref[...], win_ref[...], NN)
        halo = jnp.where(i > 0, _bdot(xp_ref[...], win_ref[:, :QKV_DIM], NN), 0.0)
        outs = (q_ref, k_ref, v_ref)
        for g in range(QKV_DIM // HEAD_DIM):
            cs = slice(g * HEAD_DIM, (g + 1) * HEAD_DIM)
            hs = slice((g % HEADS) * HEAD_DIM, (g % HEADS + 1) * HEAD_DIM)
            taps = [w_ref[pl.ds(j, 1), cs] for j in range(CONV_WIDTH)]
            y = _conv_rows(jnp.concatenate([halo[:, cs], proj_ref[:, cs]], axis=0), taps, ts)
            y_ref[:, cs] = y
            outs[g // HEADS][:, hs] = _ACT_HEAD[g // HEADS](y)
        gcb_ref[...] = _act_gb(proj_ref[:, BA_BLOCK * 128:], al_ref[...], dt_ref[...], _chunk_tril(ts))

    tile = pl.BlockSpec((ts, D_MODEL), lambda i: (i, 0))
    return _call(
        body, "gdn_conv_fwd", (s_len // ts,),
        [tile, pl.BlockSpec((CONV_HALO, D_MODEL), lambda i: (jnp.maximum(i * hb - 1, 0), 0)),
         _full((CONV_WIDTH, QKV_DIM)), _row(128), _row(128), ANY],
        [pl.BlockSpec((ts, GDN_IN_PAD), lambda i: (i, 0)), pl.BlockSpec((ts, QKV_DIM), lambda i: (i, 0)), tile, tile, tile,
         pl.BlockSpec((ts, 128), lambda i: (i, 0))],
        [_sds((s_len, GDN_IN_PAD)), _sds((s_len, QKV_DIM))] + [_sds((s_len, D_MODEL))] * 3 + [_sds((s_len, 128))],
        scratch=[pltpu.VMEM(w_in.shape, w_in.dtype), pltpu.SemaphoreType.DMA],
        sem=("arbitrary",))(x, x, conv_w, alog_l, dtb_l, w_in)


def _conv_bwd(proj, y, dq, dk, dv, dgcb, dz, conv_w, alog_l, dtb_l, w_in, dres):
    s_len = proj.shape[0]
    ts = min(CONV_TILE, s_len)
    hb = ts // CONV_HALO
    n_t = s_len // ts
    te = ts + CONV_HALO

    def body(x_ref, y_ref, yn_ref, ba_ref, dq_ref, dqn_ref, dk_ref, dkn_ref, dv_ref, dvn_ref, dgcb_ref, dz_ref,
             w_ref, al_ref, dt_ref, win_hbm, dres_ref, dp_ref, dx_ref, dw_ref, dal_ref, ddt_ref, win_ref, win_sem):
        i = pl.program_id(0)

        @pl.when(i == 0)
        def _():
            cp = pltpu.make_async_copy(win_hbm, win_ref, win_sem)
            cp.start()
            cp.wait()

        more = i < n_t - 1
        dws = [[] for _ in range(CONV_WIDTH)]
        cts = ((dq_ref, dqn_ref), (dk_ref, dkn_ref), (dv_ref, dvn_ref))
        for g in range(QKV_DIM // HEAD_DIM):
            cs = slice(g * HEAD_DIM, (g + 1) * HEAD_DIM)
            hs = slice((g % HEADS) * HEAD_DIM, (g % HEADS + 1) * HEAD_DIM)
            t_ref, n_ref = cts[g // HEADS]
            ct = jnp.concatenate([t_ref[:, hs], jnp.where(more, n_ref[:, hs], 0.0)], axis=0)
            _, act_vjp = jax.vjp(_ACT_HEAD[g // HEADS], jnp.concatenate([y_ref[:, cs], yn_ref[:, cs]], axis=0))
            (dy_e,) = act_vjp(ct)
            ahead = [pltpu.roll(dy_e, te - (CONV_WIDTH - 1 - j), 0)[:ts] for j in range(CONV_WIDTH - 1)] + [dy_e[:ts]]
            dx = ahead[CONV_WIDTH - 1] * w_ref[pl.ds(CONV_WIDTH - 1, 1), cs]
            for j in range(CONV_WIDTH - 1):
                dx = dx + ahead[j] * w_ref[pl.ds(j, 1), cs]
            dp_ref[:, cs] = dx.astype(BF16)
            x_g = x_ref[:, cs]
            for j in range(CONV_WIDTH):
                dws[j].append(jnp.sum(ahead[j] * x_g, axis=0, keepdims=True))
        dws = [jnp.concatenate(d, axis=1) for d in dws]
        _, gb_vjp = jax.vjp(lambda ba, al, dt: _act_gb(ba, al, dt, _chunk_tril(ts)), ba_ref[...], al_ref[...], dt_ref[...])
        dba, dal, ddt = gb_vjp(dgcb_ref[...])
        dp_ref[:, QKV_DIM:QKV_DIM + D_MODEL] = dz_ref[...]
        dp_ref[:, QKV_DIM + D_MODEL:] = dba.astype(BF16)
        dx_ref[...] = dres_ref[...] + _bdot(dp_ref[...], win_ref[...], NT)
        first = i == 0
        for j in range(CONV_WIDTH):
            _acc(dw_ref.at[pl.ds(j, 1), :], first, dws[j])
        _acc(dal_ref, first, dal)
        _acc(ddt_ref, first, ddt)

    tile = pl.BlockSpec((ts, D_MODEL), lambda i: (i, 0))
    nxt = pl.BlockSpec((CONV_HALO, D_MODEL), lambda i: (jnp.minimum((i + 1) * hb, n_t * hb - 1), 0))
    return _call(
        body, "gdn_conv_bwd", (n_t,),
        [pl.BlockSpec((ts, QKV_DIM), lambda i: (i, 0)), pl.BlockSpec((ts, QKV_DIM), lambda i: (i, 0)),
         pl.BlockSpec((CONV_HALO, QKV_DIM), lambda i: (jnp.minimum((i + 1) * hb, n_t * hb - 1), 0)),
         pl.BlockSpec((ts, 128), lambda i: (i, BA_BLOCK)),
         tile, nxt, tile, nxt, tile, nxt, pl.BlockSpec((ts, 128), lambda i: (i, 0)), tile,
         _full((CONV_WIDTH, QKV_DIM)), _row(128), _row(128), ANY, tile],
        [pl.BlockSpec((ts, GDN_IN_PAD), lambda i: (i, 0)), tile, _full((CONV_WIDTH, QKV_DIM)), _row(128), _row(128)],
        [_sds((s_len, GDN_IN_PAD), BF16), _sds((s_len, D_MODEL)), _sds((CONV_WIDTH, QKV_DIM)), _sds((1, 128)), _sds((1, 128))],
        scratch=[pltpu.VMEM(w_in.shape, w_in.dtype), pltpu.SemaphoreType.DMA], sem=("arbitrary",),
    )(proj, y, y, proj, dq, dq, dk, dk, dv, dv, dgcb, dz, conv_w, alog_l, dtb_l, w_in, dres)


def _tri_inv(a_strict):
    ii = lax.broadcasted_iota(jnp.int32, (CHUNK, CHUNK), 0)
    jj = lax.broadcasted_iota(jnp.int32, (CHUNK, CHUNK), 1)
    x = (ii == jj).astype(F32) - a_strict
    pw = _bdot(a_strict, a_strict, BNN)
    for step in range(5):
        x = x + _bdot(x, pw, BNN)
        if step < 4:
            pw = _bdot(pw, pw, BNN)
    return x


@jax.custom_vjp
def _solved(a_strict, rhs, t, sol):
    return sol


def _solved_fwd(a_strict, rhs, t, sol):
    return sol, (t, sol)


def _solved_bwd(res, dsol):
    t, sol = res
    drhs = _mdot(t, dsol, BTN)
    return -_mdot(drhs, sol, BNT), drhs, jnp.zeros_like(t), jnp.zeros_like(sol)


_solved.defvjp(_solved_fwd, _solved_bwd)


def _prep(q, k, v, gc, beta, solve):
    ii = lax.broadcasted_iota(jnp.int32, (CHUNK, CHUNK), 0)
    jj = lax.broadcasted_iota(jnp.int32, (CHUNK, CHUNK), 1)
    causal, strict = ii >= jj, ii > jj
    gc_row = jnp.sum((ii == jj).astype(F32) * gc, axis=1, keepdims=True)
    decay = jnp.where(causal, jnp.exp(jnp.where(causal, gc - gc_row, 0.0)), 0.0)
    kb = k * beta
    a = jnp.where(strict, _bdot(kb, k, BNT) * decay, 0.0)
    eg = jnp.exp(gc)
    sol = solve(a, jnp.concatenate([v * beta, kb * eg], axis=-1))
    qk = _bdot(q, k, BNT) * decay
    last = lax.broadcasted_iota(jnp.int32, (CHUNK, 1), 0) == CHUNK - 1
    g_last = jnp.sum(jnp.where(last, gc, 0.0), axis=1, keepdims=True)
    kd = k * jnp.exp(g_last - gc)
    gl = jnp.exp(g_last) + jnp.zeros((1, 1, HEAD_DIM), F32)
    return sol[..., :HEAD_DIM], sol[..., HEAD_DIM:], qk, q * eg, kd, gl


def _prep_specs(s_len, chunks):
    gl_m = min(PREP_CHUNKS, s_len // CHUNK)
    m = min(chunks, gl_m)
    rows = m * CHUNK
    per = gl_m // m
    hd = pl.BlockSpec((rows, HEAD_DIM), lambda c, h: (c, h))
    gcb = pl.BlockSpec((rows, 128), lambda c, h: (c, 0))
    qk = pl.BlockSpec((None, rows, CHUNK), lambda c, h: (h, c, 0))
    gl = pl.BlockSpec((None, m, HEADS, HEAD_DIM), lambda c, h: (c // per, c % per, 0, 0))
    return rows, m, hd, gcb, qk, gl


def _head_cols(gcb, h, m):
    lane = lax.broadcasted_iota(jnp.int32, gcb.shape, 1)
    pick = lambda at: jnp.sum(jnp.where(lane == at, gcb, 0.0), axis=1, keepdims=True).reshape(m, CHUNK, 1)
    return pick(h + HEADS), pick(h)


def _gdn_prep(q, k, v, gcb):
    s_len = q.shape[0]
    rows, m, hd, gcb_spec, qk_spec, gl_spec = _prep_specs(s_len, PREP_CHUNKS)

    def body(q_ref, k_ref, v_ref, gcb_ref, u_ref, w_ref, qd_ref, kd_ref, qk_ref, gl_ref, t_ref):
        r3 = lambda ref, d: ref[...].reshape(m, CHUNK, d)
        gc, beta = _head_cols(gcb_ref[...], pl.program_id(1), m)

        def solve(a, rhs):
            t = _tri_inv(a)
            t_ref[...] = t.reshape(rows, CHUNK)
            return _mdot(t, rhs, BNN)

        u, w, qk, qd, kd, gl = _prep(r3(q_ref, HEAD_DIM), r3(k_ref, HEAD_DIM), r3(v_ref, HEAD_DIM), gc, beta, solve)
        u_ref[...] = u.reshape(rows, HEAD_DIM)
        w_ref[...] = w.reshape(rows, HEAD_DIM)
        qd_ref[...] = qd.reshape(rows, HEAD_DIM).astype(BF16)
        kd_ref[...] = kd.reshape(rows, HEAD_DIM).astype(BF16)
        qk_ref[...] = qk.reshape(rows, CHUNK).astype(BF16)
        gl_ref[:, pl.ds(pl.program_id(1), 1), :] = gl

    n_g = s_len // rows
    return _call(
        body, "gdn_prep", (n_g, HEADS), [hd, hd, hd, gcb_spec], [hd, hd, hd, hd, qk_spec, gl_spec, qk_spec],
        [_sds((s_len, D_MODEL))] * 2 + [_sds((s_len, D_MODEL), BF16)] * 2
        + [_sds((HEADS, s_len, CHUNK), BF16), _sds((n_g, m, HEADS, HEAD_DIM)), _sds((HEADS, s_len, CHUNK))],
        sem=("parallel", "arbitrary"))(q, k, v, gcb)


def _gdn_prep_bwd(q, k, v, gcb, t_inv, u, w, du, dw, dqd, dkd, dqk, dgl):
    s_len = q.shape[0]
    rows, m, hd, gcb_spec, qk_spec, gl_spec = _prep_specs(s_len, PREP_BWD_CHUNKS)

    def body(q_ref, k_ref, v_ref, gcb_ref, t_ref, u_ref, w_ref, du_ref, dw_ref, dqd_ref, dkd_ref, dqk_ref, dgl_ref,
             dq_ref, dk_ref, dv_ref, dgcb_ref):
        h = pl.program_id(1)
        r3 = lambda ref, d: ref[...].reshape(m, CHUNK, d)
        gc, beta = _head_cols(gcb_ref[...], h, m)
        t = r3(t_ref, CHUNK)
        sol = jnp.concatenate([r3(u_ref, HEAD_DIM), r3(w_ref, HEAD_DIM)], axis=-1)
        fn = lambda q_, k_, v_, gc_, bt_: _prep(q_, k_, v_, gc_, bt_, lambda a, rhs: _solved(a, rhs, t, sol))
        _, vjp = jax.vjp(fn, r3(q_ref, HEAD_DIM), r3(k_ref, HEAD_DIM), r3(v_ref, HEAD_DIM), gc, beta)
        ct = (r3(du_ref, HEAD_DIM), r3(dw_ref, HEAD_DIM), r3(dqk_ref, CHUNK), r3(dqd_ref, HEAD_DIM), r3(dkd_ref, HEAD_DIM),
              dgl_ref[:, pl.ds(h, 1), :] * (1.0 / HEAD_DIM))
        dq, dk, dv, dgc, dbt = vjp(ct)
        dq_ref[...] = dq.reshape(rows, HEAD_DIM)
        dk_ref[...] = dk.reshape(rows, HEAD_DIM)
        dv_ref[...] = dv.reshape(rows, HEAD_DIM)
        lane = lax.broadcasted_iota(jnp.int32, (rows, 128), 1)
        mine = jnp.where(lane == h, dbt.reshape(rows, 1), 0.0) + jnp.where(lane == h + HEADS, dgc.reshape(rows, 1), 0.0)
        _acc(dgcb_ref, h == 0, mine)

    return _call(
        body, "gdn_prep_bwd", (s_len // rows, HEADS),
        [hd, hd, hd, gcb_spec, qk_spec, hd, hd, hd, hd, hd, hd, qk_spec, gl_spec], [hd, hd, hd, gcb_spec],
        [_sds((s_len, D_MODEL))] * 3 + [_sds((s_len, 128))],
        sem=("parallel", "arbitrary"))(q, k, v, gcb, t_inv, u, w, du, dw, dqd, dkd, dqk, dgl)


def _scan_specs(n_c, m, k, reverse):
    n_b = n_c // k
    at = (lambda n: n_b - 1 - n) if reverse else (lambda n: n)
    row = pl.BlockSpec((k * CHUNK, D_MODEL), lambda n: (at(n), 0))
    qk = pl.BlockSpec((HEADS, k * CHUNK, CHUNK), lambda n: (0, at(n), 0))
    gl = pl.BlockSpec((None, k, HEADS, HEAD_DIM), lambda n: (at(n) // (m // k), at(n) % (m // k), 0, 0))
    st = pl.BlockSpec((k, HEADS, HEAD_DIM, HEAD_DIM), lambda n: (at(n), 0, 0, 0))
    return row, qk, gl, st


def _gdn_scan(u, w, qd, kd, qk, gl):
    s_len = u.shape[0]
    n_c = s_len // CHUNK
    k = min(SCAN_CHUNKS, gl.shape[1])
    row, qk_spec, gl_spec, st_spec = _scan_specs(n_c, gl.shape[1], k, False)

    def body(u_ref, w_ref, qd_ref, kd_ref, qk_ref, gl_ref, o_ref, st_ref, state):
        hs = range(HEADS)
        sl = [slice(h * HEAD_DIM, (h + 1) * HEAD_DIM) for h in hs]
        first = pl.program_id(0) == 0
        s_all = [jnp.where(first, 0.0, state[h]) for h in hs]
        for c in range(k):
            rows = slice(c * CHUNK, (c + 1) * CHUNK)
            s_b = [s.astype(BF16) for s in s_all]
            ws = [_bdot(w_ref[rows, sl[h]], s_b[h], NN) for h in hs]
            qs = [_bdot(qd_ref[rows, sl[h]], s_b[h], NN) for h in hs]
            vn = [(u_ref[rows, sl[h]] - ws[h]).astype(BF16) for h in hs]
            outs = [qs[h] + _bdot(qk_ref[h, rows, :], vn[h], NN) for h in hs]
            nxt = [s_all[h] * gl_ref[c, pl.ds(h, 1), :] + _bdot(kd_ref[rows, sl[h]], vn[h], TN) for h in hs]
            for h in hs:
                st_ref[c, h] = s_b[h]
                o_ref[rows, sl[h]] = outs[h]
            s_all = nxt
        for h in hs:
            state[h] = s_all[h]

    return _call(
        body, "gdn_scan", (n_c // k,), [row, row, row, row, qk_spec, gl_spec], [row, st_spec],
        [_sds((s_len, D_MODEL)), _sds((n_c, HEADS, HEAD_DIM, HEAD_DIM), BF16)],
        scratch=[pltpu.VMEM((HEADS, HEAD_DIM, HEAD_DIM), F32)], sem=("arbitrary",))(u, w, qd, kd, qk, gl)


def _gdn_scan_bwd(do, u, w, qd, kd, qk, gl, states):
    s_len = u.shape[0]
    n_c = s_len // CHUNK
    k = min(SCAN_CHUNKS, gl.shape[1])
    row, qk_spec, gl_spec, st_spec = _scan_specs(n_c, gl.shape[1], k, True)

    def body(do_ref, u_ref, w_ref, qd_ref, kd_ref, qk_ref, gl_ref, st_ref,
             du_ref, dw_ref, dqd_ref, dkd_ref, dqk_ref, dgl_ref, dstate):
        hs = range(HEADS)
        sl = [slice(h * HEAD_DIM, (h + 1) * HEAD_DIM) for h in hs]
        first = pl.program_id(0) == 0
        ds_f = [jnp.where(first, 0.0, dstate[h]) for h in hs]
        for c in reversed(range(k)):
            rows = slice(c * CHUNK, (c + 1) * CHUNK)
            s_b = [st_ref[c, h] for h in hs]
            ds_b = [d.astype(BF16) for d in ds_f]
            do_b = [do_ref[rows, sl[h]].astype(BF16) for h in hs]
            w_b = [w_ref[rows, sl[h]].astype(BF16) for h in hs]
            ws = [_bdot(w_b[h], s_b[h], NN) for h in hs]
            dvn = [_bdot(qk_ref[h, rows, :], do_b[h], TN) + _bdot(kd_ref[rows, sl[h]], ds_b[h], NN) for h in hs]
            dqd = [_bdot(do_b[h], s_b[h], NT) for h in hs]
            t_do = [_bdot(qd_ref[rows, sl[h]], do_b[h], TN) for h in hs]
            vn = [(u_ref[rows, sl[h]] - ws[h]).astype(BF16) for h in hs]
            dvn_b = [d.astype(BF16) for d in dvn]
            dw = [-_bdot(dvn_b[h], s_b[h], NT) for h in hs]
            dkd = [_bdot(vn[h], ds_b[h], NT) for h in hs]
            dqk = [_bdot(do_b[h], vn[h], NT) for h in hs]
            t_dv = [_bdot(w_b[h], dvn_b[h], TN) for h in hs]
            for h in hs:
                du_ref[rows, sl[h]], dw_ref[rows, sl[h]], dqd_ref[rows, sl[h]], dkd_ref[rows, sl[h]] = dvn[h], dw[h], dqd[h], dkd[h]
                dqk_ref[h, rows, :] = dqk[h]
                dgl_ref[c, pl.ds(h, 1), :] = jnp.sum(s_b[h].astype(F32) * ds_f[h]) + jnp.zeros((1, HEAD_DIM), F32)
            ds_f = [ds_f[h] * gl_ref[c, pl.ds(h, 1), :] + t_do[h] - t_dv[h] for h in hs]
        for h in hs:
            dstate[h] = ds_f[h]

    return _call(
        body, "gdn_scan_bwd", (n_c // k,), [row, row, row, row, row, qk_spec, gl_spec, st_spec],
        [row, row, row, row, qk_spec, gl_spec],
        [_sds((s_len, D_MODEL))] * 4 + [_sds((HEADS, s_len, CHUNK)), _sds(gl.shape)],
        scratch=[pltpu.VMEM((HEADS, HEAD_DIM, HEAD_DIM), F32)], sem=("arbitrary",))(do, u, w, qd, kd, qk, gl, states)


def _gate_norm(o, z, nw):
    outs = []
    for h in range(HEADS):
        oh = o[:, h * HEAD_DIM:(h + 1) * HEAD_DIM]
        outs.append(oh * lax.rsqrt(jnp.mean(oh * oh, axis=-1, keepdims=True) + RMS_EPS))
    return jnp.concatenate(outs, axis=1) * nw * _silu(z)


def _gdn_out_fwd(o, proj, x, w_out, nw, gain, bias):
    s_len = x.shape[0]
    ts = min(ROW_TILE, s_len)

    def body(o_ref, z_ref, x_ref, w_ref, nw_ref, g_ref, b_ref, y_ref, xhat_ref, rstd_ref):
        on = _gate_norm(o_ref[...], z_ref[...], nw_ref[...])
        r = ALPHA * x_ref[...] + _bdot(on, w_ref[...], NN)
        y_ref[...], xhat_ref[...], rstd_ref[...] = _ln_fwd(r, g_ref[...], b_ref[...])

    tile = pl.BlockSpec((ts, D_MODEL), lambda i: (i, 0))
    row = _row(D_MODEL)
    return _call(
        body, "gdn_out_fwd", (s_len // ts,),
        [tile, pl.BlockSpec((ts, D_MODEL), lambda i: (i, QKV_DIM // D_MODEL)), tile, _full((D_MODEL, D_MODEL)), row, row, row],
        [tile, tile, pl.BlockSpec((ts, 1), lambda i: (i, 0))],
        [_sds((s_len, D_MODEL)), _sds((s_len, D_MODEL)), _sds((s_len, 1))], sem=("parallel",))(o, proj, x, w_out, nw, gain, bias)


def _gdn_out_bwd(dy, xhat, rstd, o, proj, w_out, nw, gain):
    s_len = o.shape[0]
    ts = min(ROW_TILE, s_len)

    def body(dy_ref, xh_ref, rs_ref, o_ref, z_ref, w_ref, nw_ref, g_ref,
             dres_ref, do_ref, dz_ref, on_ref, drb_ref, dg_ref, db_ref, dnw_ref):
        dy_t, xh_t = dy_ref[...], xh_ref[...]
        dr = _ln_bwd(dy_t, xh_t, rs_ref[...], g_ref[...])
        dres_ref[...] = ALPHA * dr
        drb_ref[...] = dr.astype(BF16)
        on, vjp = jax.vjp(_gate_norm, o_ref[...], z_ref[...], nw_ref[...])
        on_ref[...] = on.astype(BF16)
        do, dz, dnw = vjp(_bdot(dr, w_ref[...], NT))
        do_ref[...] = do
        dz_ref[...] = dz.astype(BF16)
        first = pl.program_id(0) == 0
        _acc(dg_ref, first, jnp.sum(dy_t * xh_t, axis=0, keepdims=True))
        _acc(db_ref, first, jnp.sum(dy_t, axis=0, keepdims=True))
        _acc(dnw_ref, first, sum(dnw[:, h * HEAD_DIM:(h + 1) * HEAD_DIM] for h in range(HEADS)))

    tile = pl.BlockSpec((ts, D_MODEL), lambda i: (i, 0))
    row = _row(D_MODEL)
    return _call(
        body, "gdn_out_bwd", (s_len // ts,),
        [tile, tile, pl.BlockSpec((ts, 1), lambda i: (i, 0)), tile,
         pl.BlockSpec((ts, D_MODEL), lambda i: (i, QKV_DIM // D_MODEL)), _full((D_MODEL, D_MODEL)), row, row],
        [tile, tile, tile, tile, tile, row, row, _row(HEAD_DIM)],
        [_sds((s_len, D_MODEL))] * 2 + [_sds((s_len, D_MODEL), BF16)] * 3 + [_sds((1, D_MODEL))] * 2 + [_sds((1, HEAD_DIM))],
        sem=("arbitrary",))(dy, xhat, rstd, o, proj, w_out, nw, gain)


def _adamw(w, g, m, v, name):
    r, c = w.shape
    tr = min(ADAM_ROWS, r)
    pieces = list(g) if isinstance(g, (list, tuple)) else [g]
    n_p = len(pieces)
    blocks = [a.shape[0] // tr for a in pieces]
    first = [sum(blocks[:k]) for k in range(n_p)]

    def body(*refs):
        w_ref, g_refs, (m_ref, v_ref) = refs[0], refs[1:1 + n_p], refs[1 + n_p:3 + n_p]
        go_ref, d_ref, nm_ref, nv_ref = refs[3 + n_p:]
        i = pl.program_id(0)
        g_t = g_refs[0][...]
        for k in range(1, n_p):
            g_t = jnp.where(i >= first[k], g_refs[k][...], g_t)
        go_ref[...] = g_t
        nm = ADAM_B1 * m_ref[...] + (1.0 - ADAM_B1) * g_t
        nv = ADAM_B2 * v_ref[...] + (1.0 - ADAM_B2) * (g_t * g_t)
        m_hat = nm / (1.0 - ADAM_B1 ** ADAM_STEP)
        v_hat = nv / (1.0 - ADAM_B2 ** ADAM_STEP)
        d_ref[...] = -ADAM_LR * (m_hat / (jnp.sqrt(v_hat) + ADAM_EPS) + ADAM_WD * w_ref[...])
        nm_ref[...] = nm
        nv_ref[...] = nv

    tile = pl.BlockSpec((tr, c), lambda i: (i, 0))
    g_specs = [pl.BlockSpec((tr, c), lambda i, k=k: (jnp.clip(i - first[k], 0, blocks[k] - 1), 0)) for k in range(n_p)]
    return _call(body, name, (r // tr,), [tile] + g_specs + [tile] * 2, [tile] * 4, [_sds((r, c))] * 4,
                 sem=("parallel",))(w, *pieces, m, v)


def _assemble_w_in(shards):
    rows = 256
    width = GDN_IN_DIM // N_SHARD

    def body(s_ref, o_ref):
        pad = jnp.zeros((rows, GDN_IN_PAD - GDN_IN_DIM), shards.dtype)
        o_ref[...] = jnp.concatenate([s_ref[j] for j in range(N_SHARD)] + [pad], axis=1)

    return _call(body, "w_in_assemble", (D_MODEL // rows,), [pl.BlockSpec((N_SHARD, rows, width), lambda i: (0, i, 0))],
                 pl.BlockSpec((rows, GDN_IN_PAD), lambda i: (i, 0)), _sds((D_MODEL, GDN_IN_PAD), shards.dtype),
                 sem=("parallel",))(shards)


def _split_w_in(full):
    rows = 256
    width = GDN_IN_DIM // N_SHARD

    def body(f_ref, o_ref):
        f = f_ref[...]
        for j in range(N_SHARD):
            o_ref[j] = f[:, j * width:(j + 1) * width]

    return _call(body, "w_in_split", (D_MODEL // rows,), [pl.BlockSpec((rows, GDN_IN_PAD), lambda i: (i, 0))],
                 pl.BlockSpec((N_SHARD, rows, width), lambda i: (0, i, 0)), _sds((N_SHARD, D_MODEL, width), full.dtype),
                 sem=("parallel",))(full)


def _place():
    x, y, c = lax.axis_index("x"), lax.axis_index("y"), lax.axis_index("c")
    return x, y, c, [(1 - x, y), (x, 1 - y), (1 - x, 1 - y)]


def _row_tile(rows):
    return max(t for t in range(8, min(rows, 640) + 1, 8) if rows % t == 0)


def _place_shard(part, me, dtype, name, layer=0):
    _, _, r, c = part.shape
    tr = _row_tile(r)

    def body(me_ref, p_ref, o_ref):
        o_ref[...] = p_ref[...].astype(dtype)

    return pl.pallas_call(
        body, name=name, out_shape=_sds((N_SHARD, 2, r, c), dtype),
        grid_spec=pltpu.PrefetchScalarGridSpec(
            num_scalar_prefetch=1, grid=(2, r // tr),
            in_specs=[pl.BlockSpec((None, None, tr, c), lambda h, i, me_ref: (layer, h, i, 0))],
            out_specs=pl.BlockSpec((None, None, tr, c), lambda h, i, me_ref: (me_ref[0], h, i, 0))))(me, part)


def _gather_sems(n):
    return [pltpu.SemaphoreType.DMA((6 * n,)), pltpu.SemaphoreType.DMA((6 * n,))]


def _gather_steps(dsts, send_sems, recv_sems):
    n = len(dsts)
    x, y, c, chips = _place()
    me = 2 * x + y
    sibling = (x, y, 1 - c)

    def ici(k, j, slot):
        px, py = chips[j]
        view = dsts[k].at[slot, c]
        return pltpu.make_async_remote_copy(
            src_ref=view, dst_ref=view, send_sem=send_sems.at[6 * k + j],
            recv_sem=recv_sems.at[6 * k + j], device_id=(px, py, c), device_id_type=MESH)

    def d2d(k, j, half):
        px, py = chips[j]
        view = dsts[k].at[2 * px + py, half]
        return pltpu.make_async_remote_copy(
            src_ref=view, dst_ref=view, send_sem=send_sems.at[6 * k + 3 + j], recv_sem=recv_sems.at[6 * k + 3 + j],
            device_id=sibling, device_id_type=MESH)

    def start():
        for k in range(n):
            for j in range(3):
                ici(k, j, me).start()

    def finish():
        fwds = []
        for k in range(n):
            for j, (px, py) in enumerate(chips):
                ici(k, j, 2 * px + py).wait_recv()
                fwds.append(d2d(k, j, c))
                fwds[-1].start()
        for k in range(n):
            for j in range(3):
                d2d(k, j, 1 - c).wait_recv()
        for k in range(n):
            for j in range(3):
                ici(k, j, me).wait_send()
        for cp in fwds:
            cp.wait_send()

    return start, finish


def _all_gather(bufs, name):
    n = len(bufs)

    def body(*refs):
        start, finish = _gather_steps(refs[n:2 * n], *refs[2 * n:])
        start()
        finish()

    return pl.pallas_call(
        body, name=name, out_shape=[_sds(a.shape, a.dtype) for a in bufs],
        in_specs=[ANY] * n, out_specs=[ANY] * n, input_output_aliases={k: k for k in range(n)},
        scratch_shapes=_gather_sems(n))(*bufs)


def _swap_halves(pieces, name):
    n = len(pieces)

    def body(*refs):
        srcs, dsts = refs[:n], refs[n:2 * n]
        send_sems, recv_sems = refs[2 * n:]
        x, y, c, _ = _place()
        copies = []
        for k in range(n):
            hr = pieces[k].shape[1] // 2
            copies.append(pltpu.make_async_remote_copy(
                src_ref=srcs[k].at[:, pl.ds((1 - c) * hr, hr), :], dst_ref=dsts[k],
                send_sem=send_sems.at[k], recv_sem=recv_sems.at[k], device_id=(x, y, 1 - c), device_id_type=MESH))
        for cp in copies:
            cp.start()
        for cp in copies:
            cp.wait()

    return pl.pallas_call(
        body, name=name, out_shape=[_sds((N_SHARD, a.shape[1] // 2, a.shape[2])) for a in pieces],
        in_specs=[ANY] * n, out_specs=[ANY] * n,
        scratch_shapes=[pltpu.SemaphoreType.DMA((n,)), pltpu.SemaphoreType.DMA((n,))])(*pieces)


def _add_half(piece, other, place, dtype, name):
    n, hr, cols = other.shape
    tr = _row_tile(hr)

    def body(pl_ref, a_ref, b_ref, o_ref):
        o_ref[...] = (a_ref[...] + b_ref[...]).astype(dtype)

    tile = pl.BlockSpec((None, tr, cols), lambda s, i, pl_ref: (s, i, 0))
    return pl.pallas_call(
        body, name=name, out_shape=_sds(other.shape, dtype),
        grid_spec=pltpu.PrefetchScalarGridSpec(
            num_scalar_prefetch=1, grid=(n, hr // tr),
            in_specs=[pl.BlockSpec((None, None, tr, cols), lambda s, i, pl_ref: (s, pl_ref[1], i, 0)), tile],
            out_specs=tile))(place, piece.reshape(n, 2, hr, cols), other)


def _scatter_sems(n):
    return [pltpu.SemaphoreType.DMA((3 * n,)), pltpu.SemaphoreType.DMA((3 * n,))]


def _scatter_steps(srcs, dsts, send_sems, recv_sems):
    n = len(srcs)
    x, y, c, chips = _place()
    me = 2 * x + y

    def ici(k, j, src_slot, dst_slot):
        px, py = chips[j]
        return pltpu.make_async_remote_copy(
            src_ref=srcs[k].at[src_slot], dst_ref=dsts[k].at[dst_slot], send_sem=send_sems.at[3 * k + j],
            recv_sem=recv_sems.at[3 * k + j], device_id=(px, py, c), device_id_type=MESH)

    def start():
        for k in range(n):
            for j, (px, py) in enumerate(chips):
                ici(k, j, 2 * px + py, me).start()

    def finish():
        for k in range(n):
            for j, (px, py) in enumerate(chips):
                ici(k, j, me, 2 * px + py).wait_recv()
        for k in range(n):
            for j, (px, py) in enumerate(chips):
                ici(k, j, 2 * px + py, me).wait_send()

    return start, finish


def _scatter_chips(parts, name):
    n = len(parts)

    def body(*refs):
        start, finish = _scatter_steps(refs[:n], refs[n:2 * n], *refs[2 * n:])
        start()
        finish()

    return pl.pallas_call(
        body, name=name, out_shape=[_sds(a.shape, a.dtype) for a in parts],
        in_specs=[ANY] * n, out_specs=[ANY] * n, scratch_shapes=_scatter_sems(n))(*parts)


def _sum_chips(landed, own, place, name):
    _, r, cols = landed.shape
    tr = _row_tile(r)

    def body(pl_ref, q_ref, p_ref, o_ref):
        me = pl_ref[0]
        f = lambda j: jnp.where(me == j, p_ref[...], q_ref[j]).astype(F32)
        o_ref[...] = ((f(0) + f(1)) + f(2)) + f(3)

    return pl.pallas_call(
        body, name=name, out_shape=_sds((2, r, cols)),
        grid_spec=pltpu.PrefetchScalarGridSpec(
            num_scalar_prefetch=1, grid=(r // tr,),
            in_specs=[pl.BlockSpec((N_SHARD, tr, cols), lambda i, pl_ref: (0, i, 0)),
                      pl.BlockSpec((None, tr, cols), lambda i, pl_ref: (pl_ref[0], i, 0))],
            out_specs=pl.BlockSpec((None, tr, cols), lambda i, pl_ref: (pl_ref[1], i, 0))))(place, landed, own)


def _join_halves(bufs):
    n = len(bufs)

    def body(*refs):
        dsts = refs[n:2 * n]
        send_sems, recv_sems = refs[2 * n:]
        x, y, c, _ = _place()
        copies = [pltpu.make_async_remote_copy(
            src_ref=dsts[k].at[c], dst_ref=dsts[k].at[c], send_sem=send_sems.at[k], recv_sem=recv_sems.at[k],
            device_id=(x, y, 1 - c), device_id_type=MESH) for k in range(n)]
        for cp in copies:
            cp.start()
        for cp in copies:
            cp.wait()

    return pl.pallas_call(
        body, name="grads_join_halves", out_shape=[_sds(a.shape) for a in bufs], in_specs=[ANY] * n, out_specs=[ANY] * n,
        input_output_aliases={k: k for k in range(n)},
        scratch_shapes=[pltpu.SemaphoreType.DMA((n,)), pltpu.SemaphoreType.DMA((n,))])(*bufs)


GATHER_F32 = ("ln_gain", "ln_bias", "pool_b", "gdn_conv")
REPLICATED = ("pool_scale", "gdn_a_log", "gdn_dt_bias", "gdn_norm_w", "ple_gate_b")
WEIGHTS = ("ln_gain", "ln_bias", "pool_w", "pool_b", "pool_scale", "gdn_w_in", "gdn_conv", "gdn_a_log", "gdn_dt_bias",
           "gdn_norm_w", "gdn_w_out", "mlp_w1", "mlp_w2", "ple_gate_w", "ple_gate_b", "ple_proj")
SMALL_GRADS = ("ple_proj", "pool_w", "ln_gain", "ln_bias", "pool_b", "gdn_conv") + REPLICATED


def _pack(parts, lanes, row_multiple):
    flat = jnp.concatenate([a.reshape(-1) for a in parts])
    rows = -(-flat.shape[0] // (2 * lanes * row_multiple)) * row_multiple
    return jnp.pad(flat, (0, 2 * rows * lanes - flat.shape[0])).reshape(2, rows, lanes)


def _unpack(flat, shapes):
    out, off = [], 0
    for shp in shapes:
        n = math.prod(shp)
        out.append(flat[..., off:off + n].reshape(flat.shape[:-1] + tuple(shp)))
        off += n
    return out


def _pad_lanes(a, offset, width=128):
    return jnp.pad(a, ((0, 0), (offset, width - offset - a.shape[1])))


def kernel(x, p, ln_gain, ln_bias, pool_w, pool_b, pool_scale, gdn_w_in, gdn_conv, gdn_a_log, gdn_dt_bias, gdn_norm_w, gdn_w_out, mlp_w1, mlp_w2, ple_gate_w, ple_gate_b, ple_proj, loss_target, m_ln_gain, m_ln_bias, m_pool_w, m_pool_b, m_pool_scale, m_gdn_w_in, m_gdn_conv, m_gdn_a_log, m_gdn_dt_bias, m_gdn_norm_w, m_gdn_w_out, m_mlp_w1, m_mlp_w2, m_ple_gate_w, m_ple_gate_b, m_ple_proj, v_ln_gain, v_ln_bias, v_pool_w, v_pool_b, v_pool_scale, v_gdn_w_in, v_gdn_conv, v_gdn_a_log, v_gdn_dt_bias, v_gdn_norm_w, v_gdn_w_out, v_mlp_w1, v_mlp_w2, v_ple_gate_w, v_ple_gate_b, v_ple_proj):
    shard = dict(ln_gain=ln_gain, ln_bias=ln_bias, pool_w=pool_w, pool_b=pool_b, pool_scale=pool_scale, gdn_w_in=gdn_w_in,
                 gdn_conv=gdn_conv, gdn_a_log=gdn_a_log, gdn_dt_bias=gdn_dt_bias, gdn_norm_w=gdn_norm_w, gdn_w_out=gdn_w_out,
                 mlp_w1=mlp_w1, mlp_w2=mlp_w2, ple_gate_w=ple_gate_w, ple_gate_b=ple_gate_b, ple_proj=ple_proj)
    mom = dict(ln_gain=m_ln_gain, ln_bias=m_ln_bias, pool_w=m_pool_w, pool_b=m_pool_b, pool_scale=m_pool_scale,
               gdn_w_in=m_gdn_w_in, gdn_conv=m_gdn_conv, gdn_a_log=m_gdn_a_log, gdn_dt_bias=m_gdn_dt_bias,
               gdn_norm_w=m_gdn_norm_w, gdn_w_out=m_gdn_w_out, mlp_w1=m_mlp_w1, mlp_w2=m_mlp_w2, ple_gate_w=m_ple_gate_w,
               ple_gate_b=m_ple_gate_b, ple_proj=m_ple_proj)
    var = dict(ln_gain=v_ln_gain, ln_bias=v_ln_bias, pool_w=v_pool_w, pool_b=v_pool_b, pool_scale=v_pool_scale,
               gdn_w_in=v_gdn_w_in, gdn_conv=v_gdn_conv, gdn_a_log=v_gdn_a_log, gdn_dt_bias=v_gdn_dt_bias,
               gdn_norm_w=v_gdn_norm_w, gdn_w_out=v_gdn_w_out, mlp_w1=v_mlp_w1, mlp_w2=v_mlp_w2, ple_gate_w=v_ple_gate_w,
               ple_gate_b=v_ple_gate_b, ple_proj=v_ple_proj)

    xi, yi, ci = lax.axis_index("x"), lax.axis_index("y"), lax.axis_index("c")
    me = (2 * xi + yi).reshape(1).astype(jnp.int32)
    place = jnp.stack([2 * xi + yi, ci]).astype(jnp.int32)
    early = [("mlp_w1", 0), ("mlp_w2", 0), ("ple_gate_w", 0), ("ple_proj", 0), ("pool_w", 0)]
    late = [("mlp_w1", 1), ("mlp_w2", 1), ("ple_gate_w", 1), ("ple_proj", 1), ("gdn_w_out", 0), ("gdn_w_in", 0)]
    halved = lambda n: shard[n].reshape(shard[n].shape[0], 2, -1, shard[n].shape[-1])
    placed = lambda ops, tag: [_place_shard(halved(n), me, BF16, f"place_{tag}_{n}", l) for n, l in ops]
    small_in = _place_shard(_pack([shard[n] for n in GATHER_F32], 128, 8)[None], me, F32, "place_small")
    got_early = _all_gather(placed(early, "early") + [small_in], "weights_all_gather_early")
    placed_late = placed(late, "late")
    st = dict(zip(GATHER_F32, _unpack(got_early[-1].reshape(N_SHARD, -1), [shard[n].shape for n in GATHER_F32])))

    cat_last = lambda a: jnp.moveaxis(a, 0, -2).reshape(a.shape[1:-1] + (N_SHARD * a.shape[-1],))
    gain = cat_last(st["ln_gain"])
    bias = cat_last(st["ln_bias"])
    wp = got_early[4].reshape(N_SHARD, 4, POOL_GROUP // N_SHARD, POOL_GROUP)
    pb = cat_last(st["pool_b"]).reshape(1, D_MODEL)
    ps = pool_scale
    conv_w = cat_last(st["gdn_conv"])[0]
    merged = lambda g: g.reshape(N_SHARD, -1, g.shape[-1])
    mlp_w = lambda i, got: (merged(got[0]), merged(got[1]), merged(got[2]), ple_gate_b[i:i + 1], merged(got[3]))
    alog_l = _pad_lanes(gdn_a_log, HEADS)
    dtb_l = _pad_lanes(gdn_dt_bias, HEADS)
    nw = jnp.tile(gdn_norm_w, (1, HEADS))
    ln = lambda i, k: (gain[i, k][None], bias[i, k][None])

    x0 = x[0]
    p0, p1 = p[0, 0], p[1, 0]

    x1, xh1, rs1 = _pool_fwd(x0, wp, pb, ps, *ln(0, 0))
    (x2, xh2, rs2, a0, xb0), got_late = _mlp_fwd(x1, p0, *mlp_w(0, got_early), *ln(0, 1), "mlp_fwd_0", gather=placed_late)
    w_out = got_late[4].reshape(D_MODEL, D_MODEL)
    w_in = _assemble_w_in(merged(got_late[5]))
    proj, y_conv, q, k, v, gcb = _conv_fwd(x2, w_in, conv_w, alog_l, dtb_l)
    u, w, qd, kd, qk, gl, t_inv = _gdn_prep(q, k, v, gcb)
    o, states = _gdn_scan(u, w, qd, kd, qk, gl)
    x3, xh3, rs3 = _gdn_out_fwd(o, proj, x2, w_out, nw, *ln(1, 0))
    (dy4, xh4, rs4, a1, xb1, loss_l), _ = _mlp_fwd(x3, p1, *mlp_w(1, got_late), *ln(1, 1), "mlp_fwd_1", target=loss_target[0])

    g_gain = [[None, None], [None, None]]
    g_bias = [[None, None], [None, None]]

    def mlp_grads(i, dy, xh, rs, x_mid, xb, a, p_i, got, scatter=()):
        (dx, dh, dzg, dpp, drb, dg, db, dgb), landed = _mlp_bwd(
            dy, xh, rs, x_mid, a, p_i, *mlp_w(i, got), ln(i, 1)[0], f"mlp_bwd_{i}", scatter=scatter)
        g_gain[i][1], g_bias[i][1] = dg, db
        return dx, dict(
            mlp_w1=_wgrad(xb, dh, f"dw1_{i}", stack_cols=True), mlp_w2=_wgrad(a, drb, f"dw2_{i}").reshape(N_SHARD, -1, D_MODEL),
            ple_gate_w=_wgrad(xb, dzg, f"dgate_w_{i}").reshape(N_SHARD, -1, D_MODEL),
            ple_proj=jnp.moveaxis(_wgrad(p_i, dpp, f"dproj_{i}").reshape(PLE_DIM, N_SHARD, -1), 1, 0), ple_gate_b=dgb), landed

    def chip_sums(pieces, wire, tag):
        others = _swap_halves(pieces, f"grads_swap_halves_{tag}")
        return [_add_half(a, b, place, t, f"grads_add_half_{tag}{i}") for i, (a, b, t) in enumerate(zip(pieces, others, wire))]

    dx3, gl1, _ = mlp_grads(1, dy4, xh4, rs4, x3, xb1, a1, p1, got_late)
    dres, do, dz, on_b, drb3, g_gain[1][0], g_bias[1][0], d_nw = _gdn_out_bwd(dx3, xh3, rs3, o, proj, w_out, nw, ln(1, 0)[0])
    d_wout = _wgrad(on_b, drb3, "dw_out").reshape(N_SHARD, -1, D_MODEL)
    du, dw, dqd, dkd, dqk, dgl = _gdn_scan_bwd(do, u, w, qd, kd, qk, gl, states)
    dq, dk, dv, dgcb = _gdn_prep_bwd(q, k, v, gcb, t_inv, u, w, du, dw, dqd, dkd, dqk, dgl)
    dproj, dx2, d_conv, d_alog_l, d_dtb_l = _conv_bwd(proj, y_conv, dq, dk, dv, dgcb, dz, conv_w, alog_l, dtb_l, w_in, dres)
    d_win = _split_w_in(_wgrad(x2, dproj, "dw_in"))
    sums_late = chip_sums([gl1["mlp_w1"], gl1["mlp_w2"], gl1["ple_gate_w"], d_wout, d_win], [BF16] * 5, "late")
    dx1, gl0, landed_late = mlp_grads(0, dx2, xh2, rs2, x1, xb0, a0, p0, got_early, scatter=sums_late)
    sums_early = chip_sums([gl0["mlp_w1"], gl0["mlp_w2"], gl0["ple_gate_w"]], [BF16] * 3, "early")
    (dx0, g_gain[0][0], g_bias[0][0], d_ps, d_pb, d_wp), landed_early = _pool_bwd(
        dx1, xh1, rs1, x0, wp, pb, ps, ln(0, 0)[0], scatter=sums_early)

    split_last = lambda a: jnp.moveaxis(a.reshape(a.shape[:-1] + (N_SHARD, a.shape[-1] // N_SHARD)), -2, 0)
    small_st = dict(
        ple_proj=jnp.stack([gl0["ple_proj"], gl1["ple_proj"]], axis=1),
        pool_w=jnp.moveaxis(d_wp.reshape(4, N_SHARD, POOL_GROUP // N_SHARD, POOL_GROUP), 1, 0)[:, None],
        ln_gain=split_last(jnp.stack([jnp.concatenate(r, axis=0) for r in g_gain])),
        ln_bias=split_last(jnp.stack([jnp.concatenate(r, axis=0) for r in g_bias])),
        pool_b=split_last(d_pb.reshape(1, 4, POOL_GROUP)),
        gdn_conv=split_last(d_conv)[:, None],
    )
    rep = dict(pool_scale=d_ps, gdn_a_log=d_alog_l[:, HEADS:2 * HEADS], gdn_dt_bias=d_dtb_l[:, HEADS:2 * HEADS],
               gdn_norm_w=d_nw, ple_gate_b=jnp.concatenate([gl0["ple_gate_b"], gl1["ple_gate_b"]], axis=0))
    for n in REPLICATED:
        small_st[n] = jnp.broadcast_to(rep[n][None], (N_SHARD,) + rep[n].shape)
    small_flat = jnp.concatenate([small_st[n].reshape(N_SHARD, -1) for n in SMALL_GRADS], axis=1)
    small_rows = -(-small_flat.shape[1] // (16 * LANES)) * 16
    small_piece = jnp.pad(small_flat, ((0, 0), (0, small_rows * LANES - small_flat.shape[1]))).reshape(N_SHARD, small_rows, LANES)

    sums_small = chip_sums([small_piece], [F32], "small")
    landed_small = _scatter_chips(sums_small, "grads_scatter_chips_small")
    red = _join_halves([_sum_chips(q_, p_, place, f"grads_sum_chips_{i}") for i, (q_, p_) in
                        enumerate(zip(list(landed_early) + list(landed_small) + list(landed_late),
                                      sums_early + sums_small + sums_late))])
    red = [r.reshape(-1, r.shape[-1]) for r in red]
    grads = dict(mlp_w1=[red[0], red[4]], mlp_w2=[red[1], red[5]], ple_gate_w=[red[2], red[6]], gdn_w_out=red[7], gdn_w_in=red[8])
    grads.update(zip(SMALL_GRADS, _unpack(red[3].reshape(-1), [shard[n].shape for n in SMALL_GRADS])))

    delta, new_m, new_v = {}, {}, {}
    small = [n for n in WEIGHTS if shard[n].size < 128 * 128]
    for n in WEIGHTS:
        if n in small:
            continue
        to2d = lambda a, n=n: a.reshape(-1, shard[n].shape[-1])
        g_n = grads[n] if isinstance(grads[n], list) else to2d(grads[n])
        g2, d2, m2, v2 = _adamw(to2d(shard[n]), g_n, to2d(mom[n]), to2d(var[n]), "adamw_" + n)
        grads[n], delta[n], new_m[n], new_v[n] = (t.reshape(shard[n].shape) for t in (g2, d2, m2, v2))
    pk = lambda d: _pack([d[n] for n in small], 128, 8).reshape(-1, 128)
    _, d2, m2, v2 = _adamw(pk(shard), pk(grads), pk(mom), pk(var), "adamw_small")
    for dst, t in ((delta, d2), (new_m, m2), (new_v, v2)):
        dst.update(zip(small, _unpack(t.reshape(-1), [shard[n].shape for n in small])))

    loss = lax.psum(loss_l[0, 0], ("x", "y", "c"))
    return (loss, dx0[None], *[grads[n] for n in WEIGHTS], *[delta[n] for n in WEIGHTS],
            *[new_m[n] for n in WEIGHTS], *[new_v[n] for n in WEIGHTS])
```

```python
import math

import jax
import jax.numpy as jnp
from jax import lax
from jax.experimental import pallas as pl
from jax.experimental.pallas import tpu as pltpu

F32 = jnp.float32
BF16 = jnp.bfloat16

D_MODEL = 1024
D_FF = 4096
PLE_DIM = 256
N_SHARD = 4
POOL_WINDOWS = (2, 4, 8, 16)
POOL_GROUP = 256
POOL_HALO = 16
HEADS = 8
HEAD_DIM = 128
CHUNK = 64
CONV_WIDTH = 4
CONV_HALO = 8
QKV_DIM = 3 * D_MODEL
GDN_IN_DIM = QKV_DIM + D_MODEL + 2 * HEADS
GDN_IN_PAD = 4224
BA_BLOCK = (QKV_DIM + D_MODEL) // 128
ALPHA = (2.0 * 2) ** 0.25
LN_EPS = 1e-5
RMS_EPS = 1e-6
L2_EPS = 1e-6
ADAM_LR, ADAM_B1, ADAM_B2, ADAM_EPS, ADAM_WD, ADAM_STEP = 0.001, 0.9, 0.999, 1e-08, 0.01, 10

ROW_TILE = 512
CONV_TILE = 256
PREP_CHUNKS = 32
PREP_BWD_CHUNKS = 16
SCAN_CHUNKS = 8
LANES = 1024
ADAM_ROWS = 256

NN = (((1,), (0,)), ((), ()))
NT = (((1,), (1,)), ((), ()))
TN = (((0,), (0,)), ((), ()))
BNN = (((2,), (1,)), ((0,), (0,)))
BNT = (((2,), (2,)), ((0,), (0,)))
BTN = (((1,), (1,)), ((0,), (0,)))
MESH = pl.DeviceIdType.MESH
ANY = pl.BlockSpec(memory_space=pl.ANY)


def _bdot(a, b, dims):
    return lax.dot_general(a.astype(BF16), b.astype(BF16), dims, preferred_element_type=F32)


def _hdot(a, b, dims):
    return lax.dot_general(a, b, dims, precision=lax.Precision.HIGHEST, preferred_element_type=F32)


def _mdot(a, b, dims):
    return lax.dot_general(a, b, dims, precision=lax.Precision.HIGH, preferred_element_type=F32)


def _sigmoid(x):
    return 0.5 * jnp.tanh(0.5 * x) + 0.5


def _silu(x):
    return x * _sigmoid(x)


def _softplus(x):
    return jnp.maximum(x, 0.0) + jnp.log1p(jnp.exp(-jnp.abs(x)))


def _call(body, name, grid, in_specs, out_specs, out_shape, scratch=(), sem=None, aliases=None):
    params = pltpu.CompilerParams(dimension_semantics=sem) if sem else None
    return pl.pallas_call(
        body, name=name, grid=grid, in_specs=in_specs, out_specs=out_specs, out_shape=out_shape,
        scratch_shapes=list(scratch), compiler_params=params, input_output_aliases=aliases or {})


def _row(d):
    return pl.BlockSpec((1, d), lambda *_: (0, 0))


def _full(shape):
    n = len(shape)
    return pl.BlockSpec(shape, lambda *_: (0,) * n)


def _sds(shape, dtype=F32):
    return jax.ShapeDtypeStruct(shape, dtype)


def _ln_fwd(r, gain, bias):
    mu = jnp.mean(r, axis=-1, keepdims=True)
    xc = r - mu
    rstd = lax.rsqrt(jnp.mean(xc * xc, axis=-1, keepdims=True) + LN_EPS)
    xhat = xc * rstd
    return xhat * gain + bias, xhat, rstd


def _ln_bwd(dy, xhat, rstd, gain):
    dxh = dy * gain
    m1 = jnp.mean(dxh, axis=-1, keepdims=True)
    m2 = jnp.mean(dxh * xhat, axis=-1, keepdims=True)
    return rstd * (dxh - m1 - xhat * m2)


def _acc(ref, first, val):
    @pl.when(first)
    def _():
        ref[...] = val

    @pl.when(jnp.logical_not(first))
    def _():
        ref[...] += val


def _pooled_groups(xe, t0, ts):
    pos = (t0 + lax.broadcasted_iota(jnp.int32, (ts, 1), 0) + 1).astype(F32)
    outs = []
    for gi, win in enumerate(POOL_WINDOWS):
        xs = xe[:, gi * POOL_GROUP:(gi + 1) * POOL_GROUP]
        s, k = xs, 1
        while k < win:
            s = s + pltpu.roll(s, k, 0)
            k *= 2
        mean = s[POOL_HALO:] / jnp.minimum(pos, float(win))
        outs.append(mean - xs[POOL_HALO:])
    return outs


def _pool_groups_w(w_ref):
    return [jnp.concatenate([w_ref[s, g] for s in range(N_SHARD)], axis=0) for g in range(4)]


def _pool_fwd(x, wp, pb, ps, gain, bias):
    s_len = x.shape[0]
    ts = min(ROW_TILE, s_len)
    hb = ts // POOL_HALO

    def body(x_ref, halo_ref, w_ref, pb_ref, ps_ref, g_ref, b_ref, y_ref, xhat_ref, rstd_ref):
        i = pl.program_id(0)
        x_t = x_ref[...]
        halo = jnp.where(i > 0, halo_ref[...], 0.0)
        pooled = _pooled_groups(jnp.concatenate([halo, x_t], axis=0), i * ts, ts)
        wg = _pool_groups_w(w_ref)
        y = jnp.concatenate([_bdot(pooled[g], wg[g], NN) for g in range(4)], axis=1) + pb_ref[...]
        r = ALPHA * x_t + y * ps_ref[...]
        y_ref[...], xhat_ref[...], rstd_ref[...] = _ln_fwd(r, g_ref[...], b_ref[...])

    tile = pl.BlockSpec((ts, D_MODEL), lambda i: (i, 0))
    return _call(
        body, "pool_fwd", (s_len // ts,),
        [tile, pl.BlockSpec((POOL_HALO, D_MODEL), lambda i: (jnp.maximum(i * hb - 1, 0), 0)),
         _full(wp.shape), _row(D_MODEL), _row(D_MODEL), _row(D_MODEL), _row(D_MODEL)],
        [tile, tile, pl.BlockSpec((ts, 1), lambda i: (i, 0))],
        [_sds((s_len, D_MODEL)), _sds((s_len, D_MODEL)), _sds((s_len, 1))],
        sem=("parallel",))(x, x, wp, pb, ps, gain, bias)


def _pool_bwd(dy, xhat, rstd, x, wp, pb, ps, gain, scatter=()):
    s_len = x.shape[0]
    ts = min(ROW_TILE, s_len)
    hb = ts // POOL_HALO
    n_t = s_len // ts
    ne = ts + POOL_HALO
    n_s = len(scatter)

    def body(*refs):
        dy_ref, dyn_ref, xh_ref, xhn_ref, rs_ref, rsn_ref, x_ref, xp_ref, w_ref, pb_ref, ps_ref, g_ref = refs[:12]
        dx_ref, dg_ref, db_ref, dps_ref, dpb_ref, dw_ref = refs[12 + n_s:18 + n_s]
        i = pl.program_id(0)
        if n_s:
            start, finish = _scatter_steps(refs[12:12 + n_s], refs[18 + n_s:18 + 2 * n_s], *refs[18 + 2 * n_s:])
            pl.when(i == 0)(start)
        more = i < n_t - 1
        dy_t, xh_t = dy_ref[...], xh_ref[...]
        dy_e = jnp.concatenate([dy_t, jnp.where(more, dyn_ref[...], 0.0)], axis=0)
        xh_e = jnp.concatenate([xh_t, xhn_ref[...]], axis=0)
        rs_e = jnp.concatenate([rs_ref[...], rsn_ref[...]], axis=0)
        dr_e = _ln_bwd(dy_e, xh_e, rs_e, g_ref[...])
        dyy_e = dr_e * ps_ref[...]
        pos_e = (i * ts + lax.broadcasted_iota(jnp.int32, (ne, 1), 0) + 1).astype(F32)
        dxs = []
        wg = _pool_groups_w(w_ref)
        for gi, win in enumerate(POOL_WINDOWS):
            sl = slice(gi * POOL_GROUP, (gi + 1) * POOL_GROUP)
            dpool = _bdot(dyy_e[:, sl], wg[gi], NT)
            s, k = dpool / jnp.minimum(pos_e, float(win)), 1
            while k < win:
                s = s + pltpu.roll(s, ne - k, 0)
                k *= 2
            dxs.append(s[:ts] - dpool[:ts])
        dx_ref[...] = ALPHA * dr_e[:ts] + jnp.concatenate(dxs, axis=1)

        x_t = x_ref[...]
        halo = jnp.where(i > 0, xp_ref[...], 0.0)
        pooled = _pooled_groups(jnp.concatenate([halo, x_t], axis=0), i * ts, ts)
        y = jnp.concatenate([_bdot(pooled[g], wg[g], NN) for g in range(4)], axis=1) + pb_ref[...]
        dr_t, dyy_t = dr_e[:ts], dyy_e[:ts]
        first = i == 0
        _acc(dg_ref, first, jnp.sum(dy_t * xh_t, axis=0, keepdims=True))
        _acc(db_ref, first, jnp.sum(dy_t, axis=0, keepdims=True))
        _acc(dps_ref, first, jnp.sum(dr_t * y, axis=0, keepdims=True))
        _acc(dpb_ref, first, jnp.sum(dyy_t, axis=0, keepdims=True))
        for g in range(4):
            _acc(dw_ref.at[g], first, _bdot(pooled[g], dyy_t[:, g * POOL_GROUP:(g + 1) * POOL_GROUP], TN))
        if n_s:
            pl.when(i == n_t - 1)(finish)

    tile = pl.BlockSpec((ts, D_MODEL), lambda i: (i, 0))
    nxt = pl.BlockSpec((POOL_HALO, D_MODEL), lambda i: (jnp.minimum((i + 1) * hb, n_t * hb - 1), 0))
    prv = pl.BlockSpec((POOL_HALO, D_MODEL), lambda i: (jnp.maximum(i * hb - 1, 0), 0))
    rs_t = pl.BlockSpec((ts, 1), lambda i: (i, 0))
    rs_n = pl.BlockSpec((POOL_HALO, 1), lambda i: (jnp.minimum((i + 1) * hb, n_t * hb - 1), 0))
    row = _row(D_MODEL)
    out = _call(
        body, "pool_bwd", (n_t,),
        [tile, nxt, tile, nxt, rs_t, rs_n, tile, prv, _full(wp.shape), row, row, row] + [ANY] * n_s,
        [tile, row, row, row, row, _full((4, POOL_GROUP, POOL_GROUP))] + [ANY] * n_s,
        [_sds((s_len, D_MODEL))] + [_sds((1, D_MODEL))] * 4 + [_sds((4, POOL_GROUP, POOL_GROUP))]
        + [_sds(t.shape, t.dtype) for t in scatter],
        scratch=_scatter_sems(n_s) if n_s else [], sem=("arbitrary",),
    )(dy, dy, xhat, xhat, rstd, rstd, x, x, wp, pb, ps, gain, *scatter)
    return out[:6], out[6:]


def _mlp_weight_specs():
    fc = D_FF // N_SHARD
    return [pl.BlockSpec((None, D_MODEL, fc), lambda i, j: (j, 0, 0)),
            pl.BlockSpec((None, fc, D_MODEL), lambda i, j: (j, 0, 0)),
            _full((N_SHARD, D_MODEL // N_SHARD, D_MODEL)),
            _row(D_MODEL),
            _full((N_SHARD, PLE_DIM, D_MODEL // N_SHARD))]


def _gate_w(gw_ref):
    return gw_ref[...].reshape(D_MODEL, D_MODEL)


def _ple_proj(pj_ref):
    return jnp.concatenate([pj_ref[s] for s in range(N_SHARD)], axis=1)


def _mlp_fwd(x, p, w1s, w2s, gw, gb, proj, gain, bias, name, gather=(), target=None):
    s_len = x.shape[0]
    ts = min(ROW_TILE, s_len)
    n_i = s_len // ts
    n_g = len(gather)
    has_t = target is not None
    n_in, n_out = 9 + has_t, 5 + has_t

    def body(*refs):
        x_ref, p_ref, w1_ref, w2_ref, gw_ref, gb_ref, pj_ref, g_ref, b_ref = refs[:9]
        y_ref, xhat_ref, rstd_ref, a_ref, xbo_ref = refs[n_in + n_g:n_in + n_g + 5]
        acc_ref, xb_ref = refs[n_in + n_out + 2 * n_g:n_in + n_out + 2 * n_g + 2]
        i, j = pl.program_id(0), pl.program_id(1)
        if n_g:
            start, finish = _gather_steps(refs[n_in + n_out + n_g:n_in + n_out + 2 * n_g], *refs[n_in + n_out + 2 * n_g + 2:])
            pl.when((i == 0) & (j == 0))(start)

        @pl.when(j == 0)
        def _():
            x_t = x_ref[...]
            xb_ref[...] = x_t.astype(BF16)
            xbo_ref[...] = x_t.astype(BF16)
            gate = _sigmoid(_bdot(x_t, _gate_w(gw_ref), NN) + gb_ref[...])
            acc_ref[...] = ALPHA * x_t + gate * _bdot(p_ref[...], _ple_proj(pj_ref), NN)

        h = jnp.maximum(_bdot(xb_ref[...], w1_ref[...], NN), 0.0)
        a = (h * h).astype(BF16)
        a_ref[...] = a
        acc_ref[...] += _bdot(a, w2_ref[...], NN)

        @pl.when(j == N_SHARD - 1)
        def _():
            y, xhat_ref[...], rstd_ref[...] = _ln_fwd(acc_ref[...], g_ref[...], b_ref[...])
            if has_t:
                err = y - refs[9][...]
                y_ref[...] = err * (1.0 / D_MODEL)
                part = 0.5 * jnp.sum(jnp.mean(err * err, axis=-1, keepdims=True))
                _acc(refs[n_in + n_g + 5], i == 0, part + jnp.zeros((1, 128), F32))
            else:
                y_ref[...] = y

        if n_g:
            pl.when((i == n_i - 1) & (j == N_SHARD - 1))(finish)

    tile = pl.BlockSpec((ts, D_MODEL), lambda i, j: (i, 0))
    row = _row(D_MODEL)
    out = _call(
        body, name, (n_i, N_SHARD),
        [tile, pl.BlockSpec((ts, PLE_DIM), lambda i, j: (i, 0))] + _mlp_weight_specs() + [row, row] + [tile] * has_t
        + [ANY] * n_g,
        [tile, tile, pl.BlockSpec((ts, 1), lambda i, j: (i, 0)), pl.BlockSpec((ts, D_FF // N_SHARD), lambda i, j: (i, j)), tile]
        + [_row(128)] * has_t + [ANY] * n_g,
        [_sds((s_len, D_MODEL)), _sds((s_len, D_MODEL)), _sds((s_len, 1)), _sds((s_len, D_FF), BF16), _sds((s_len, D_MODEL), BF16)]
        + [_sds((1, 128))] * has_t + [_sds(a.shape, a.dtype) for a in gather],
        scratch=[pltpu.VMEM((ts, D_MODEL), F32), pltpu.VMEM((ts, D_MODEL), BF16)] + (_gather_sems(n_g) if n_g else []),
        sem=("arbitrary", "arbitrary"), aliases={n_in + k: n_out + k for k in range(n_g)},
    )(x, p, w1s, w2s, gw, gb, proj, gain, bias, *([target] if has_t else []), *gather)
    return out[:n_out], out[n_out:]


def _mlp_bwd(dy, xhat, rstd, x, a, p, w1s, w2s, gw, gb, proj, gain, name, scatter=()):
    s_len = x.shape[0]
    ts = min(ROW_TILE, s_len)
    fc = D_FF // N_SHARD
    n_i = s_len // ts
    n_s = len(scatter)

    def body(*refs):
        dy_ref, xh_ref, rs_ref, x_ref, a_ref, p_ref, w1_ref, w2_ref, gw_ref, gb_ref, pj_ref, g_ref = refs[:12]
        dx_ref, dh_ref, dzg_ref, dpp_ref, drb_ref, dg_ref, db_ref, dgb_ref = refs[12 + n_s:20 + n_s]
        acc_ref, dr_ref = refs[20 + 2 * n_s:22 + 2 * n_s]
        i, j = pl.program_id(0), pl.program_id(1)
        if n_s:
            start, finish = _scatter_steps(refs[12:12 + n_s], refs[20 + n_s:20 + 2 * n_s], *refs[22 + 2 * n_s:])
            pl.when((i == 0) & (j == 0))(start)

        @pl.when(j == 0)
        def _():
            dy_t, xh_t, x_t = dy_ref[...], xh_ref[...], x_ref[...]
            dr = _ln_bwd(dy_t, xh_t, rs_ref[...], g_ref[...])
            drb = dr.astype(BF16)
            dr_ref[...] = drb
            drb_ref[...] = drb
            gw_full = _gate_w(gw_ref)
            gate = _sigmoid(_bdot(x_t, gw_full, NN) + gb_ref[...])
            pp = _bdot(p_ref[...], _ple_proj(pj_ref), NN)
            dzg = dr * pp * gate * (1.0 - gate)
            dzg_ref[...] = dzg.astype(BF16)
            dpp_ref[...] = (dr * gate).astype(BF16)
            acc_ref[...] = ALPHA * dr + _bdot(dzg, gw_full, NT)
            first = i == 0
            _acc(dg_ref, first, jnp.sum(dy_t * xh_t, axis=0, keepdims=True))
            _acc(db_ref, first, jnp.sum(dy_t, axis=0, keepdims=True))
            _acc(dgb_ref, first, jnp.sum(dzg, axis=0, keepdims=True))

        dh = (_bdot(dr_ref[...], w2_ref[...], NT) * (2.0 * jnp.sqrt(a_ref[...].astype(F32)))).astype(BF16)
        dh_ref[...] = dh
        acc_ref[...] += _bdot(dh, w1_ref[...], NT)

        @pl.when(j == N_SHARD - 1)
        def _():
            dx_ref[...] = acc_ref[...]

        if n_s:
            pl.when((i == n_i - 1) & (j == N_SHARD - 1))(finish)

    tile = pl.BlockSpec((ts, D_MODEL), lambda i, j: (i, 0))
    ftile = pl.BlockSpec((ts, fc), lambda i, j: (i, j))
    row = _row(D_MODEL)
    out = _call(
        body, name, (n_i, N_SHARD),
        [tile, tile, pl.BlockSpec((ts, 1), lambda i, j: (i, 0)), tile, ftile, pl.BlockSpec((ts, PLE_DIM), lambda i, j: (i, 0))]
        + _mlp_weight_specs() + [row] + [ANY] * n_s,
        [tile, ftile, tile, tile, tile, row, row, row] + [ANY] * n_s,
        [_sds((s_len, D_MODEL)), _sds((s_len, D_FF), BF16)]
        + [_sds((s_len, D_MODEL), BF16)] * 3 + [_sds((1, D_MODEL))] * 3 + [_sds(t.shape, t.dtype) for t in scatter],
        scratch=[pltpu.VMEM((ts, D_MODEL), F32), pltpu.VMEM((ts, D_MODEL), BF16)] + (_scatter_sems(n_s) if n_s else []),
        sem=("arbitrary", "arbitrary"))(dy, xhat, rstd, x, a, p, w1s, w2s, gw, gb, proj, gain, *scatter)
    return out[:8], out[8:]


def _wgrad(a, b, name, stack_cols=False):
    s_len, m = a.shape
    n = b.shape[1]
    ts = min(2048 if a.dtype == BF16 and b.dtype == BF16 else 1024, s_len)
    tm = min(m, 1024)
    tn = n // N_SHARD if stack_cols else (1408 if n == GDN_IN_PAD else min(n, 1024))
    n_s = s_len // ts

    def body(a_ref, b_ref, o_ref):
        _acc(o_ref, pl.program_id(2) == 0, _bdot(a_ref[...], b_ref[...], TN))

    if stack_cols:
        out_spec = pl.BlockSpec((None, tm, tn), lambda mi, nj, s: (nj, mi, 0))
        out_shape = _sds((N_SHARD, m, tn))
    else:
        out_spec = pl.BlockSpec((tm, tn), lambda mi, nj, s: (mi, nj))
        out_shape = _sds((m, n))
    return _call(
        body, name, (m // tm, n // tn, n_s),
        [pl.BlockSpec((ts, tm), lambda mi, nj, s: (s, mi)), pl.BlockSpec((ts, tn), lambda mi, nj, s: (s, nj))],
        out_spec, out_shape, sem=("parallel", "parallel", "arbitrary"))(a, b)


def _act_qkv(y):
    qkv = _silu(y)
    qs, ks = [], []
    for h in range(HEADS):
        qh = qkv[:, h * HEAD_DIM:(h + 1) * HEAD_DIM]
        kh = qkv[:, D_MODEL + h * HEAD_DIM:D_MODEL + (h + 1) * HEAD_DIM]
        qs.append(qh * (lax.rsqrt(jnp.sum(qh * qh, axis=-1, keepdims=True) + L2_EPS) * HEAD_DIM ** -0.5))
        ks.append(kh * lax.rsqrt(jnp.sum(kh * kh, axis=-1, keepdims=True) + L2_EPS))
    return jnp.concatenate(qs, axis=1), jnp.concatenate(ks, axis=1), qkv[:, 2 * D_MODEL:]


def _act_gb(ba, alog_l, dtb_l, tril):
    lane = lax.broadcasted_iota(jnp.int32, ba.shape, 1)
    g = jnp.where((lane >= HEADS) & (lane < 2 * HEADS), -jnp.exp(alog_l) * _softplus(ba + dtb_l), 0.0)
    return jnp.where(lane < HEADS, _sigmoid(ba), _hdot(tril, g, NN))


def _chunk_tril(t):
    ii = lax.broadcasted_iota(jnp.int32, (t, t), 0)
    jj = lax.broadcasted_iota(jnp.int32, (t, t), 1)
    return ((ii // CHUNK == jj // CHUNK) & (ii >= jj)).astype(F32)


def _conv_rows(xe, w, n_rows):
    y = xe[CONV_HALO:CONV_HALO + n_rows] * w[CONV_WIDTH - 1]
    for j in range(CONV_WIDTH - 1):
        y = y + pltpu.roll(xe, CONV_WIDTH - 1 - j, 0)[CONV_HALO:CONV_HALO + n_rows] * w[j]
    return y


def _conv_fwd(x, w_in, conv_w, alog_l, dtb_l):
    s_len = x.shape[0]
    ts = min(CONV_TILE, s_len)
    hb = ts // CONV_HALO

    def body(x_ref, xp_ref, w_ref, al_ref, dt_ref, win_hbm, proj_ref, y_ref, q_ref, k_ref, v_ref, gcb_ref, xb_ref,
             win_ref, win_sem):
        i = pl.program_id(0)

        @pl.when(i == 0)
        def _():
            cp = pltpu.make_async_copy(win_hbm, win_ref, win_sem)
            cp.start()
            cp.wait()

        xb = x_ref[...].astype(BF16)
        xb_ref[...] = xb
        proj = _bdot(xb, win_ref[...], NN)
        proj_ref[...] = proj
        halo = jnp.where(i > 0, _bdot(xp_ref[...], win_ref[:, :QKV_DIM], NN), 0.0)
        taps = [w_ref[pl.ds(j, 1), :] for j in range(CONV_WIDTH)]
        y = _conv_rows(jnp.concatenate([halo, proj[:, :QKV_DIM]], axis=0), taps, ts)
        y_ref[...] = y
        q_ref[...], k_ref[...], v_ref[...] = _act_qkv(y)
        gcb_ref[...] = _act_gb(proj[:, BA_BLOCK * 128:], al_ref[...], dt_ref[...], _chunk_tril(ts))

    tile = pl.BlockSpec((ts, D_MODEL), lambda i: (i, 0))
    return _call(
        body, "gdn_conv_fwd", (s_len // ts,),
        [tile, pl.BlockSpec((CONV_HALO, D_MODEL), lambda i: (jnp.maximum(i * hb - 1, 0), 0)),
         _full((CONV_WIDTH, QKV_DIM)), _row(128), _row(128), ANY],
        [pl.BlockSpec((ts, GDN_IN_PAD), lambda i: (i, 0)), pl.BlockSpec((ts, QKV_DIM), lambda i: (i, 0)), tile, tile, tile,
         pl.BlockSpec((ts, 128), lambda i: (i, 0)), tile],
        [_sds((s_len, GDN_IN_PAD)), _sds((s_len, QKV_DIM))] + [_sds((s_len, D_MODEL))] * 3 + [_sds((s_len, 128)),
                                                                                              _sds((s_len, D_MODEL), BF16)],
        scratch=[pltpu.VMEM(w_in.shape, w_in.dtype), pltpu.SemaphoreType.DMA],
        sem=("arbitrary",))(x, x, conv_w, alog_l, dtb_l, w_in)


def _conv_bwd(proj, y, dq, dk, dv, dgcb, dz, conv_w, alog_l, dtb_l, w_in, dres):
    s_len = proj.shape[0]
    ts = min(CONV_TILE, s_len)
    hb = ts // CONV_HALO
    n_t = s_len // ts
    te = ts + CONV_HALO

    def body(x_ref, y_ref, yn_ref, ba_ref, dq_ref, dqn_ref, dk_ref, dkn_ref, dv_ref, dvn_ref, dgcb_ref, dz_ref,
             w_ref, al_ref, dt_ref, win_hbm, dres_ref, dp_ref, dx_ref, dw_ref, dal_ref, ddt_ref, win_ref, win_sem):
        i = pl.program_id(0)

        @pl.when(i == 0)
        def _():
            cp = pltpu.make_async_copy(win_hbm, win_ref, win_sem)
            cp.start()
            cp.wait()

        more = i < n_t - 1
        w = [w_ref[pl.ds(j, 1), :] for j in range(CONV_WIDTH)]
        x_t = x_ref[...]
        _, act_vjp = jax.vjp(_act_qkv, jnp.concatenate([y_ref[...], yn_ref[...]], axis=0))
        ct = tuple(jnp.concatenate([t[...], jnp.where(more, n[...], 0.0)], axis=0)
                   for t, n in ((dq_ref, dqn_ref), (dk_ref, dkn_ref), (dv_ref, dvn_ref)))
        (dy_e,) = act_vjp(ct)
        ahead = [pltpu.roll(dy_e, te - (CONV_WIDTH - 1 - j), 0)[:ts] for j in range(CONV_WIDTH - 1)] + [dy_e[:ts]]
        dx = ahead[CONV_WIDTH - 1] * w[CONV_WIDTH - 1]
        for j in range(CONV_WIDTH - 1):
            dx = dx + ahead[j] * w[j]
        dws = [jnp.sum(ahead[j] * x_t, axis=0, keepdims=True) for j in range(CONV_WIDTH)]
        _, gb_vjp = jax.vjp(lambda ba, al, dt: _act_gb(ba, al, dt, _chunk_tril(ts)), ba_ref[...], al_ref[...], dt_ref[...])
        dba, dal, ddt = gb_vjp(dgcb_ref[...])
        dp = jnp.concatenate([dx.astype(BF16), dz_ref[...], dba.astype(BF16)], axis=1)
        dp_ref[...] = dp
        dx_ref[...] = dres_ref[...] + _bdot(dp, win_ref[...], NT)
        first = i == 0
        for j in range(CONV_WIDTH):
            _acc(dw_ref.at[pl.ds(j, 1), :], first, dws[j])
        _acc(dal_ref, first, dal)
        _acc(ddt_ref, first, ddt)

    tile = pl.BlockSpec((ts, D_MODEL), lambda i: (i, 0))
    nxt = pl.BlockSpec((CONV_HALO, D_MODEL), lambda i: (jnp.minimum((i + 1) * hb, n_t * hb - 1), 0))
    return _call(
        body, "gdn_conv_bwd", (n_t,),
        [pl.BlockSpec((ts, QKV_DIM), lambda i: (i, 0)), pl.BlockSpec((ts, QKV_DIM), lambda i: (i, 0)),
         pl.BlockSpec((CONV_HALO, QKV_DIM), lambda i: (jnp.minimum((i + 1) * hb, n_t * hb - 1), 0)),
         pl.BlockSpec((ts, 128), lambda i: (i, BA_BLOCK)),
         tile, nxt, tile, nxt, tile, nxt, pl.BlockSpec((ts, 128), lambda i: (i, 0)), tile,
         _full((CONV_WIDTH, QKV_DIM)), _row(128), _row(128), ANY, tile],
        [pl.BlockSpec((ts, GDN_IN_PAD), lambda i: (i, 0)), tile, _full((CONV_WIDTH, QKV_DIM)), _row(128), _row(128)],
        [_sds((s_len, GDN_IN_PAD), BF16), _sds((s_len, D_MODEL)), _sds((CONV_WIDTH, QKV_DIM)), _sds((1, 128)), _sds((1, 128))],
        scratch=[pltpu.VMEM(w_in.shape, w_in.dtype), pltpu.SemaphoreType.DMA], sem=("arbitrary",),
    )(proj, y, y, proj, dq, dq, dk, dk, dv, dv, dgcb, dz, conv_w, alog_l, dtb_l, w_in, dres)


def _tri_inv(a_strict):
    ii = lax.broadcasted_iota(jnp.int32, (CHUNK, CHUNK), 0)
    jj = lax.broadcasted_iota(jnp.int32, (CHUNK, CHUNK), 1)
    x = (ii == jj).astype(F32) - a_strict
    pw = _bdot(a_strict, a_strict, BNN)
    for step in range(5):
        x = x + _bdot(x, pw, BNN)
        if step < 4:
            pw = _bdot(pw, pw, BNN)
    return x


@jax.custom_vjp
def _solved(a_strict, rhs, t, sol):
    return sol


def _solved_fwd(a_strict, rhs, t, sol):
    return sol, (t, sol)


def _solved_bwd(res, dsol):
    t, sol = res
    drhs = _mdot(t, dsol, BTN)
    return -_mdot(drhs, sol, BNT), drhs, jnp.zeros_like(t), jnp.zeros_like(sol)


_solved.defvjp(_solved_fwd, _solved_bwd)


def _prep(q, k, v, gc, beta, solve):
    ii = lax.broadcasted_iota(jnp.int32, (CHUNK, CHUNK), 0)
    jj = lax.broadcasted_iota(jnp.int32, (CHUNK, CHUNK), 1)
    causal, strict = ii >= jj, ii > jj
    gc_row = jnp.sum((ii == jj).astype(F32) * gc, axis=1, keepdims=True)
    decay = jnp.where(causal, jnp.exp(jnp.where(causal, gc - gc_row, 0.0)), 0.0)
    kb = k * beta
    a = jnp.where(strict, _bdot(kb, k, BNT) * decay, 0.0)
    eg = jnp.exp(gc)
    sol = solve(a, jnp.concatenate([v * beta, kb * eg], axis=-1))
    qk = _bdot(q, k, BNT) * decay
    last = lax.broadcasted_iota(jnp.int32, (CHUNK, 1), 0) == CHUNK - 1
    g_last = jnp.sum(jnp.where(last, gc, 0.0), axis=1, keepdims=True)
    kd = k * jnp.exp(g_last - gc)
    gl = jnp.exp(g_last) + jnp.zeros((1, 1, HEAD_DIM), F32)
    return sol[..., :HEAD_DIM], sol[..., HEAD_DIM:], qk, q * eg, kd, gl


def _prep_specs(s_len, chunks):
    gl_m = min(PREP_CHUNKS, s_len // CHUNK)
    m = min(chunks, gl_m)
    rows = m * CHUNK
    per = gl_m // m
    hd = pl.BlockSpec((rows, HEAD_DIM), lambda c, h: (c, h))
    gcb = pl.BlockSpec((rows, 128), lambda c, h: (c, 0))
    qk = pl.BlockSpec((None, rows, CHUNK), lambda c, h: (h, c, 0))
    gl = pl.BlockSpec((None, m, HEADS, HEAD_DIM), lambda c, h: (c // per, c % per, 0, 0))
    return rows, m, hd, gcb, qk, gl


def _head_cols(gcb, h, m):
    lane = lax.broadcasted_iota(jnp.int32, gcb.shape, 1)
    pick = lambda at: jnp.sum(jnp.where(lane == at, gcb, 0.0), axis=1, keepdims=True).reshape(m, CHUNK, 1)
    return pick(h + HEADS), pick(h)


def _gdn_prep(q, k, v, gcb):
    s_len = q.shape[0]
    rows, m, hd, gcb_spec, qk_spec, gl_spec = _prep_specs(s_len, PREP_CHUNKS)

    def body(q_ref, k_ref, v_ref, gcb_ref, u_ref, w_ref, qd_ref, kd_ref, qk_ref, gl_ref, t_ref):
        r3 = lambda ref, d: ref[...].reshape(m, CHUNK, d)
        gc, beta = _head_cols(gcb_ref[...], pl.program_id(1), m)

        def solve(a, rhs):
            t = _tri_inv(a)
            t_ref[...] = t.reshape(rows, CHUNK)
            return _mdot(t, rhs, BNN)

        u, w, qk, qd, kd, gl = _prep(r3(q_ref, HEAD_DIM), r3(k_ref, HEAD_DIM), r3(v_ref, HEAD_DIM), gc, beta, solve)
        u_ref[...] = u.reshape(rows, HEAD_DIM)
        w_ref[...] = w.reshape(rows, HEAD_DIM)
        qd_ref[...] = qd.reshape(rows, HEAD_DIM).astype(BF16)
        kd_ref[...] = kd.reshape(rows, HEAD_DIM).astype(BF16)
        qk_ref[...] = qk.reshape(rows, CHUNK).astype(BF16)
        gl_ref[:, pl.ds(pl.program_id(1), 1), :] = gl

    n_g = s_len // rows
    return _call(
        body, "gdn_prep", (n_g, HEADS), [hd, hd, hd, gcb_spec], [hd, hd, hd, hd, qk_spec, gl_spec, qk_spec],
        [_sds((s_len, D_MODEL))] * 2 + [_sds((s_len, D_MODEL), BF16)] * 2
        + [_sds((HEADS, s_len, CHUNK), BF16), _sds((n_g, m, HEADS, HEAD_DIM)), _sds((HEADS, s_len, CHUNK))],
        sem=("parallel", "arbitrary"))(q, k, v, gcb)


def _gdn_prep_bwd(q, k, v, gcb, t_inv, u, w, du, dw, dqd, dkd, dqk, dgl):
    s_len = q.shape[0]
    rows, m, hd, gcb_spec, qk_spec, gl_spec = _prep_specs(s_len, PREP_BWD_CHUNKS)

    def body(q_ref, k_ref, v_ref, gcb_ref, t_ref, u_ref, w_ref, du_ref, dw_ref, dqd_ref, dkd_ref, dqk_ref, dgl_ref,
             dq_ref, dk_ref, dv_ref, dgcb_ref):
        h = pl.program_id(1)
        r3 = lambda ref, d: ref[...].reshape(m, CHUNK, d)
        gc, beta = _head_cols(gcb_ref[...], h, m)
        t = r3(t_ref, CHUNK)
        sol = jnp.concatenate([r3(u_ref, HEAD_DIM), r3(w_ref, HEAD_DIM)], axis=-1)
        fn = lambda q_, k_, v_, gc_, bt_: _prep(q_, k_, v_, gc_, bt_, lambda a, rhs: _solved(a, rhs, t, sol))
        _, vjp = jax.vjp(fn, r3(q_ref, HEAD_DIM), r3(k_ref, HEAD_DIM), r3(v_ref, HEAD_DIM), gc, beta)
        ct = (r3(du_ref, HEAD_DIM), r3(dw_ref, HEAD_DIM), r3(dqk_ref, CHUNK), r3(dqd_ref, HEAD_DIM), r3(dkd_ref, HEAD_DIM),
              dgl_ref[:, pl.ds(h, 1), :] * (1.0 / HEAD_DIM))
        dq, dk, dv, dgc, dbt = vjp(ct)
        dq_ref[...] = dq.reshape(rows, HEAD_DIM)
        dk_ref[...] = dk.reshape(rows, HEAD_DIM)
        dv_ref[...] = dv.reshape(rows, HEAD_DIM)
        lane = lax.broadcasted_iota(jnp.int32, (rows, 128), 1)
        mine = jnp.where(lane == h, dbt.reshape(rows, 1), 0.0) + jnp.where(lane == h + HEADS, dgc.reshape(rows, 1), 0.0)
        _acc(dgcb_ref, h == 0, mine)

    return _call(
        body, "gdn_prep_bwd", (s_len // rows, HEADS),
        [hd, hd, hd, gcb_spec, qk_spec, hd, hd, hd, hd, hd, hd, qk_spec, gl_spec], [hd, hd, hd, gcb_spec],
        [_sds((s_len, D_MODEL))] * 3 + [_sds((s_len, 128))],
        sem=("parallel", "arbitrary"))(q, k, v, gcb, t_inv, u, w, du, dw, dqd, dkd, dqk, dgl)


def _scan_specs(n_c, m, k, reverse):
    n_b = n_c // k
    at = (lambda n: n_b - 1 - n) if reverse else (lambda n: n)
    row = pl.BlockSpec((k * CHUNK, D_MODEL), lambda n: (at(n), 0))
    qk = pl.BlockSpec((HEADS, k * CHUNK, CHUNK), lambda n: (0, at(n), 0))
    gl = pl.BlockSpec((None, k, HEADS, HEAD_DIM), lambda n: (at(n) // (m // k), at(n) % (m // k), 0, 0))
    st = pl.BlockSpec((k, HEADS, HEAD_DIM, HEAD_DIM), lambda n: (at(n), 0, 0, 0))
    return row, qk, gl, st


def _gdn_scan(u, w, qd, kd, qk, gl):
    s_len = u.shape[0]
    n_c = s_len // CHUNK
    k = min(SCAN_CHUNKS, gl.shape[1])
    row, qk_spec, gl_spec, st_spec = _scan_specs(n_c, gl.shape[1], k, False)

    def body(u_ref, w_ref, qd_ref, kd_ref, qk_ref, gl_ref, o_ref, st_ref, state):
        hs = range(HEADS)
        sl = [slice(h * HEAD_DIM, (h + 1) * HEAD_DIM) for h in hs]
        first = pl.program_id(0) == 0
        s_all = [jnp.where(first, 0.0, state[h]) for h in hs]
        for c in range(k):
            rows = slice(c * CHUNK, (c + 1) * CHUNK)
            s_b = [s.astype(BF16) for s in s_all]
            ws = [_bdot(w_ref[rows, sl[h]], s_b[h], NN) for h in hs]
            qs = [_bdot(qd_ref[rows, sl[h]], s_b[h], NN) for h in hs]
            vn = [(u_ref[rows, sl[h]] - ws[h]).astype(BF16) for h in hs]
            outs = [qs[h] + _bdot(qk_ref[h, rows, :], vn[h], NN) for h in hs]
            nxt = [s_all[h] * gl_ref[c, pl.ds(h, 1), :] + _bdot(kd_ref[rows, sl[h]], vn[h], TN) for h in hs]
            for h in hs:
                st_ref[c, h] = s_b[h]
                o_ref[rows, sl[h]] = outs[h]
            s_all = nxt
        for h in hs:
            state[h] = s_all[h]

    return _call(
        body, "gdn_scan", (n_c // k,), [row, row, row, row, qk_spec, gl_spec], [row, st_spec],
        [_sds((s_len, D_MODEL)), _sds((n_c, HEADS, HEAD_DIM, HEAD_DIM), BF16)],
        scratch=[pltpu.VMEM((HEADS, HEAD_DIM, HEAD_DIM), F32)], sem=("arbitrary",))(u, w, qd, kd, qk, gl)


def _gdn_scan_bwd(do, u, w, qd, kd, qk, gl, states):
    s_len = u.shape[0]
    n_c = s_len // CHUNK
    k = min(SCAN_CHUNKS, gl.shape[1])
    row, qk_spec, gl_spec, st_spec = _scan_specs(n_c, gl.shape[1], k, True)

    def body(do_ref, u_ref, w_ref, qd_ref, kd_ref, qk_ref, gl_ref, st_ref,
             du_ref, dw_ref, dqd_ref, dkd_ref, dqk_ref, dgl_ref, dstate):
        hs = range(HEADS)
        sl = [slice(h * HEAD_DIM, (h + 1) * HEAD_DIM) for h in hs]
        first = pl.program_id(0) == 0
        ds_f = [jnp.where(first, 0.0, dstate[h]) for h in hs]
        for c in reversed(range(k)):
            rows = slice(c * CHUNK, (c + 1) * CHUNK)
            s_b = [st_ref[c, h] for h in hs]
            ds_b = [d.astype(BF16) for d in ds_f]
            do_b = [do_ref[rows, sl[h]].astype(BF16) for h in hs]
            w_b = [w_ref[rows, sl[h]].astype(BF16) for h in hs]
            ws = [_bdot(w_b[h], s_b[h], NN) for h in hs]
            dvn = [_bdot(qk_ref[h, rows, :], do_b[h], TN) + _bdot(kd_ref[rows, sl[h]], ds_b[h], NN) for h in hs]
            dqd = [_bdot(do_b[h], s_b[h], NT) for h in hs]
            t_do = [_bdot(qd_ref[rows, sl[h]], do_b[h], TN) for h in hs]
            vn = [(u_ref[rows, sl[h]] - ws[h]).astype(BF16) for h in hs]
            dvn_b = [d.astype(BF16) for d in dvn]
            dw = [-_bdot(dvn_b[h], s_b[h], NT) for h in hs]
            dkd = [_bdot(vn[h], ds_b[h], NT) for h in hs]
            dqk = [_bdot(do_b[h], vn[h], NT) for h in hs]
            t_dv = [_bdot(w_b[h], dvn_b[h], TN) for h in hs]
            for h in hs:
                du_ref[rows, sl[h]], dw_ref[rows, sl[h]], dqd_ref[rows, sl[h]], dkd_ref[rows, sl[h]] = dvn[h], dw[h], dqd[h], dkd[h]
                dqk_ref[h, rows, :] = dqk[h]
                dgl_ref[c, pl.ds(h, 1), :] = jnp.sum(s_b[h].astype(F32) * ds_f[h]) + jnp.zeros((1, HEAD_DIM), F32)
            ds_f = [ds_f[h] * gl_ref[c, pl.ds(h, 1), :] + t_do[h] - t_dv[h] for h in hs]
        for h in hs:
            dstate[h] = ds_f[h]

    return _call(
        body, "gdn_scan_bwd", (n_c // k,), [row, row, row, row, row, qk_spec, gl_spec, st_spec],
        [row, row, row, row, qk_spec, gl_spec],
        [_sds((s_len, D_MODEL))] * 4 + [_sds((HEADS, s_len, CHUNK)), _sds(gl.shape)],
        scratch=[pltpu.VMEM((HEADS, HEAD_DIM, HEAD_DIM), F32)], sem=("arbitrary",))(do, u, w, qd, kd, qk, gl, states)


def _gate_norm(o, z, nw):
    outs = []
    for h in range(HEADS):
        oh = o[:, h * HEAD_DIM:(h + 1) * HEAD_DIM]
        outs.append(oh * lax.rsqrt(jnp.mean(oh * oh, axis=-1, keepdims=True) + RMS_EPS))
    return jnp.concatenate(outs, axis=1) * nw * _silu(z)


def _gdn_out_fwd(o, proj, x, w_out, nw, gain, bias):
    s_len = x.shape[0]
    ts = min(ROW_TILE, s_len)

    def body(o_ref, z_ref, x_ref, w_ref, nw_ref, g_ref, b_ref, y_ref, xhat_ref, rstd_ref):
        on = _gate_norm(o_ref[...], z_ref[...], nw_ref[...])
        r = ALPHA * x_ref[...] + _bdot(on, w_ref[...], NN)
        y_ref[...], xhat_ref[...], rstd_ref[...] = _ln_fwd(r, g_ref[...], b_ref[...])

    tile = pl.BlockSpec((ts, D_MODEL), lambda i: (i, 0))
    row = _row(D_MODEL)
    return _call(
        body, "gdn_out_fwd", (s_len // ts,),
        [tile, pl.BlockSpec((ts, D_MODEL), lambda i: (i, QKV_DIM // D_MODEL)), tile, _full((D_MODEL, D_MODEL)), row, row, row],
        [tile, tile, pl.BlockSpec((ts, 1), lambda i: (i, 0))],
        [_sds((s_len, D_MODEL)), _sds((s_len, D_MODEL)), _sds((s_len, 1))], sem=("parallel",))(o, proj, x, w_out, nw, gain, bias)


def _gdn_out_bwd(dy, xhat, rstd, o, proj, w_out, nw, gain):
    s_len = o.shape[0]
    ts = min(ROW_TILE, s_len)

    def body(dy_ref, xh_ref, rs_ref, o_ref, z_ref, w_ref, nw_ref, g_ref,
             dres_ref, do_ref, dz_ref, on_ref, drb_ref, dg_ref, db_ref, dnw_ref):
        dy_t, xh_t = dy_ref[...], xh_ref[...]
        dr = _ln_bwd(dy_t, xh_t, rs_ref[...], g_ref[...])
        dres_ref[...] = ALPHA * dr
        drb_ref[...] = dr.astype(BF16)
        on, vjp = jax.vjp(_gate_norm, o_ref[...], z_ref[...], nw_ref[...])
        on_ref[...] = on.astype(BF16)
        do, dz, dnw = vjp(_bdot(dr, w_ref[...], NT))
        do_ref[...] = do
        dz_ref[...] = dz.astype(BF16)
        first = pl.program_id(0) == 0
        _acc(dg_ref, first, jnp.sum(dy_t * xh_t, axis=0, keepdims=True))
        _acc(db_ref, first, jnp.sum(dy_t, axis=0, keepdims=True))
        _acc(dnw_ref, first, sum(dnw[:, h * HEAD_DIM:(h + 1) * HEAD_DIM] for h in range(HEADS)))

    tile = pl.BlockSpec((ts, D_MODEL), lambda i: (i, 0))
    row = _row(D_MODEL)
    return _call(
        body, "gdn_out_bwd", (s_len // ts,),
        [tile, tile, pl.BlockSpec((ts, 1), lambda i: (i, 0)), tile,
         pl.BlockSpec((ts, D_MODEL), lambda i: (i, QKV_DIM // D_MODEL)), _full((D_MODEL, D_MODEL)), row, row],
        [tile, tile, tile, tile, tile, row, row, _row(HEAD_DIM)],
        [_sds((s_len, D_MODEL))] * 2 + [_sds((s_len, D_MODEL), BF16)] * 3 + [_sds((1, D_MODEL))] * 2 + [_sds((1, HEAD_DIM))],
        sem=("arbitrary",))(dy, xhat, rstd, o, proj, w_out, nw, gain)


def _adamw(w, g, m, v, name):
    r, c = w.shape
    tr = min(ADAM_ROWS, r)
    pieces = list(g) if isinstance(g, (list, tuple)) else [g]
    n_p = len(pieces)
    blocks = [a.shape[0] // tr for a in pieces]
    first = [sum(blocks[:k]) for k in range(n_p)]

    def body(*refs):
        w_ref, g_refs, (m_ref, v_ref) = refs[0], refs[1:1 + n_p], refs[1 + n_p:3 + n_p]
        go_ref, d_ref, nm_ref, nv_ref = refs[3 + n_p:]
        i = pl.program_id(0)
        g_t = g_refs[0][...]
        for k in range(1, n_p):
            g_t = jnp.where(i >= first[k], g_refs[k][...], g_t)
        go_ref[...] = g_t
        nm = ADAM_B1 * m_ref[...] + (1.0 - ADAM_B1) * g_t
        nv = ADAM_B2 * v_ref[...] + (1.0 - ADAM_B2) * (g_t * g_t)
        m_hat = nm / (1.0 - ADAM_B1 ** ADAM_STEP)
        v_hat = nv / (1.0 - ADAM_B2 ** ADAM_STEP)
        d_ref[...] = -ADAM_LR * (m_hat / (jnp.sqrt(v_hat) + ADAM_EPS) + ADAM_WD * w_ref[...])
        nm_ref[...] = nm
        nv_ref[...] = nv

    tile = pl.BlockSpec((tr, c), lambda i: (i, 0))
    g_specs = [pl.BlockSpec((tr, c), lambda i, k=k: (jnp.clip(i - first[k], 0, blocks[k] - 1), 0)) for k in range(n_p)]
    return _call(body, name, (r // tr,), [tile] + g_specs + [tile] * 2, [tile] * 4, [_sds((r, c))] * 4,
                 sem=("parallel",))(w, *pieces, m, v)


def _assemble_w_in(shards):
    rows = 256
    width = GDN_IN_DIM // N_SHARD

    def body(s_ref, o_ref):
        pad = jnp.zeros((rows, GDN_IN_PAD - GDN_IN_DIM), shards.dtype)
        o_ref[...] = jnp.concatenate([s_ref[j] for j in range(N_SHARD)] + [pad], axis=1)

    return _call(body, "w_in_assemble", (D_MODEL // rows,), [pl.BlockSpec((N_SHARD, rows, width), lambda i: (0, i, 0))],
                 pl.BlockSpec((rows, GDN_IN_PAD), lambda i: (i, 0)), _sds((D_MODEL, GDN_IN_PAD), shards.dtype),
                 sem=("parallel",))(shards)


def _split_w_in(full):
    rows = 256
    width = GDN_IN_DIM // N_SHARD

    def body(f_ref, o_ref):
        f = f_ref[...]
        for j in range(N_SHARD):
            o_ref[j] = f[:, j * width:(j + 1) * width]

    return _call(body, "w_in_split", (D_MODEL // rows,), [pl.BlockSpec((rows, GDN_IN_PAD), lambda i: (i, 0))],
                 pl.BlockSpec((N_SHARD, rows, width), lambda i: (0, i, 0)), _sds((N_SHARD, D_MODEL, width), full.dtype),
                 sem=("parallel",))(full)


def _place():
    x, y, c = lax.axis_index("x"), lax.axis_index("y"), lax.axis_index("c")
    return x, y, c, [(1 - x, y), (x, 1 - y), (1 - x, 1 - y)]


def _row_tile(rows):
    return max(t for t in range(8, min(rows, 640) + 1, 8) if rows % t == 0)


def _place_shard(part, me, dtype, name, layer=0):
    _, _, r, c = part.shape
    tr = _row_tile(r)

    def body(me_ref, p_ref, o_ref):
        o_ref[...] = p_ref[...].astype(dtype)

    return pl.pallas_call(
        body, name=name, out_shape=_sds((N_SHARD, 2, r, c), dtype),
        grid_spec=pltpu.PrefetchScalarGridSpec(
            num_scalar_prefetch=1, grid=(2, r // tr),
            in_specs=[pl.BlockSpec((None, None, tr, c), lambda h, i, me_ref: (layer, h, i, 0))],
            out_specs=pl.BlockSpec((None, None, tr, c), lambda h, i, me_ref: (me_ref[0], h, i, 0))))(me, part)


def _gather_sems(n):
    return [pltpu.SemaphoreType.DMA((6 * n,)), pltpu.SemaphoreType.DMA((6 * n,))]


def _gather_steps(dsts, send_sems, recv_sems):
    n = len(dsts)
    x, y, c, chips = _place()
    me = 2 * x + y
    sibling = (x, y, 1 - c)

    def ici(k, j, slot):
        px, py = chips[j]
        view = dsts[k].at[slot, c]
        return pltpu.make_async_remote_copy(
            src_ref=view, dst_ref=view, send_sem=send_sems.at[6 * k + j],
            recv_sem=recv_sems.at[6 * k + j], device_id=(px, py, c), device_id_type=MESH)

    def d2d(k, j, half):
        px, py = chips[j]
        view = dsts[k].at[2 * px + py, half]
        return pltpu.make_async_remote_copy(
            src_ref=view, dst_ref=view, send_sem=send_sems.at[6 * k + 3 + j], recv_sem=recv_sems.at[6 * k + 3 + j],
            device_id=sibling, device_id_type=MESH)

    def start():
        for k in range(n):
            for j in range(3):
                ici(k, j, me).start()

    def finish():
        fwds = []
        for k in range(n):
            for j, (px, py) in enumerate(chips):
                ici(k, j, 2 * px + py).wait_recv()
                fwds.append(d2d(k, j, c))
                fwds[-1].start()
        for k in range(n):
            for j in range(3):
                d2d(k, j, 1 - c).wait_recv()
        for k in range(n):
            for j in range(3):
                ici(k, j, me).wait_send()
        for cp in fwds:
            cp.wait_send()

    return start, finish


def _all_gather(bufs, name):
    n = len(bufs)

    def body(*refs):
        start, finish = _gather_steps(refs[n:2 * n], *refs[2 * n:])
        start()
        finish()

    return pl.pallas_call(
        body, name=name, out_shape=[_sds(a.shape, a.dtype) for a in bufs],
        in_specs=[ANY] * n, out_specs=[ANY] * n, input_output_aliases={k: k for k in range(n)},
        scratch_shapes=_gather_sems(n))(*bufs)


def _swap_halves(pieces, name):
    n = len(pieces)

    def body(*refs):
        srcs, dsts = refs[:n], refs[n:2 * n]
        send_sems, recv_sems = refs[2 * n:]
        x, y, c, _ = _place()
        copies = []
        for k in range(n):
            hr = pieces[k].shape[1] // 2
            copies.append(pltpu.make_async_remote_copy(
                src_ref=srcs[k].at[:, pl.ds((1 - c) * hr, hr), :], dst_ref=dsts[k],
                send_sem=send_sems.at[k], recv_sem=recv_sems.at[k], device_id=(x, y, 1 - c), device_id_type=MESH))
        for cp in copies:
            cp.start()
        for cp in copies:
            cp.wait()

    return pl.pallas_call(
        body, name=name, out_shape=[_sds((N_SHARD, a.shape[1] // 2, a.shape[2])) for a in pieces],
        in_specs=[ANY] * n, out_specs=[ANY] * n,
        scratch_shapes=[pltpu.SemaphoreType.DMA((n,)), pltpu.SemaphoreType.DMA((n,))])(*pieces)


def _add_half(piece, other, place, dtype, name):
    n, hr, cols = other.shape
    tr = _row_tile(hr)

    def body(pl_ref, a_ref, b_ref, o_ref):
        o_ref[...] = (a_ref[...] + b_ref[...]).astype(dtype)

    tile = pl.BlockSpec((None, tr, cols), lambda s, i, pl_ref: (s, i, 0))
    return pl.pallas_call(
        body, name=name, out_shape=_sds(other.shape, dtype),
        grid_spec=pltpu.PrefetchScalarGridSpec(
            num_scalar_prefetch=1, grid=(n, hr // tr),
            in_specs=[pl.BlockSpec((None, None, tr, cols), lambda s, i, pl_ref: (s, pl_ref[1], i, 0)), tile],
            out_specs=tile))(place, piece.reshape(n, 2, hr, cols), other)


def _scatter_sems(n):
    return [pltpu.SemaphoreType.DMA((3 * n,)), pltpu.SemaphoreType.DMA((3 * n,))]


def _scatter_steps(srcs, dsts, send_sems, recv_sems):
    n = len(srcs)
    x, y, c, chips = _place()
    me = 2 * x + y

    def ici(k, j, src_slot, dst_slot):
        px, py = chips[j]
        return pltpu.make_async_remote_copy(
            src_ref=srcs[k].at[src_slot], dst_ref=dsts[k].at[dst_slot], send_sem=send_sems.at[3 * k + j],
            recv_sem=recv_sems.at[3 * k + j], device_id=(px, py, c), device_id_type=MESH)

    def start():
        for k in range(n):
            for j, (px, py) in enumerate(chips):
                ici(k, j, 2 * px + py, me).start()

    def finish():
        for k in range(n):
            for j, (px, py) in enumerate(chips):
                ici(k, j, me, 2 * px + py).wait_recv()
        for k in range(n):
            for j, (px, py) in enumerate(chips):
                ici(k, j, 2 * px + py, me).wait_send()

    return start, finish


def _scatter_chips(parts, name):
    n = len(parts)

    def body(*refs):
        start, finish = _scatter_steps(refs[:n], refs[n:2 * n], *refs[2 * n:])
        start()
        finish()

    return pl.pallas_call(
        body, name=name, out_shape=[_sds(a.shape, a.dtype) for a in parts],
        in_specs=[ANY] * n, out_specs=[ANY] * n, scratch_shapes=_scatter_sems(n))(*parts)


def _sum_chips(landed, own, place, name):
    _, r, cols = landed.shape
    tr = _row_tile(r)

    def body(pl_ref, q_ref, p_ref, o_ref):
        me = pl_ref[0]
        f = lambda j: jnp.where(me == j, p_ref[...], q_ref[j]).astype(F32)
        o_ref[...] = ((f(0) + f(1)) + f(2)) + f(3)

    return pl.pallas_call(
        body, name=name, out_shape=_sds((2, r, cols)),
        grid_spec=pltpu.PrefetchScalarGridSpec(
            num_scalar_prefetch=1, grid=(r // tr,),
            in_specs=[pl.BlockSpec((N_SHARD, tr, cols), lambda i, pl_ref: (0, i, 0)),
                      pl.BlockSpec((None, tr, cols), lambda i, pl_ref: (pl_ref[0], i, 0))],
            out_specs=pl.BlockSpec((None, tr, cols), lambda i, pl_ref: (pl_ref[1], i, 0))))(place, landed, own)


def _join_halves(bufs):
    n = len(bufs)

    def body(*refs):
        dsts = refs[n:2 * n]
        send_sems, recv_sems = refs[2 * n:]
        x, y, c, _ = _place()
        copies = [pltpu.make_async_remote_copy(
            src_ref=dsts[k].at[c], dst_ref=dsts[k].at[c], send_sem=send_sems.at[k], recv_sem=recv_sems.at[k],
            device_id=(x, y, 1 - c), device_id_type=MESH) for k in range(n)]
        for cp in copies:
            cp.start()
        for cp in copies:
            cp.wait()

    return pl.pallas_call(
        body, name="grads_join_halves", out_shape=[_sds(a.shape) for a in bufs], in_specs=[ANY] * n, out_specs=[ANY] * n,
        input_output_aliases={k: k for k in range(n)},
        scratch_shapes=[pltpu.SemaphoreType.DMA((n,)), pltpu.SemaphoreType.DMA((n,))])(*bufs)


GATHER_F32 = ("ln_gain", "ln_bias", "pool_b", "gdn_conv")
REPLICATED = ("pool_scale", "gdn_a_log", "gdn_dt_bias", "gdn_norm_w", "ple_gate_b")
WEIGHTS = ("ln_gain", "ln_bias", "pool_w", "pool_b", "pool_scale", "gdn_w_in", "gdn_conv", "gdn_a_log", "gdn_dt_bias",
           "gdn_norm_w", "gdn_w_out", "mlp_w1", "mlp_w2", "ple_gate_w", "ple_gate_b", "ple_proj")
SMALL_GRADS = ("ple_proj", "pool_w", "ln_gain", "ln_bias", "pool_b", "gdn_conv") + REPLICATED


def _pack(parts, lanes, row_multiple):
    flat = jnp.concatenate([a.reshape(-1) for a in parts])
    rows = -(-flat.shape[0] // (2 * lanes * row_multiple)) * row_multiple
    return jnp.pad(flat, (0, 2 * rows * lanes - flat.shape[0])).reshape(2, rows, lanes)


def _unpack(flat, shapes):
    out, off = [], 0
    for shp in shapes:
        n = math.prod(shp)
        out.append(flat[..., off:off + n].reshape(flat.shape[:-1] + tuple(shp)))
        off += n
    return out


def _pad_lanes(a, offset, width=128):
    return jnp.pad(a, ((0, 0), (offset, width - offset - a.shape[1])))


def kernel(x, p, ln_gain, ln_bias, pool_w, pool_b, pool_scale, gdn_w_in, gdn_conv, gdn_a_log, gdn_dt_bias, gdn_norm_w, gdn_w_out, mlp_w1, mlp_w2, ple_gate_w, ple_gate_b, ple_proj, loss_target, m_ln_gain, m_ln_bias, m_pool_w, m_pool_b, m_pool_scale, m_gdn_w_in, m_gdn_conv, m_gdn_a_log, m_gdn_dt_bias, m_gdn_norm_w, m_gdn_w_out, m_mlp_w1, m_mlp_w2, m_ple_gate_w, m_ple_gate_b, m_ple_proj, v_ln_gain, v_ln_bias, v_pool_w, v_pool_b, v_pool_scale, v_gdn_w_in, v_gdn_conv, v_gdn_a_log, v_gdn_dt_bias, v_gdn_norm_w, v_gdn_w_out, v_mlp_w1, v_mlp_w2, v_ple_gate_w, v_ple_gate_b, v_ple_proj):
    shard = dict(ln_gain=ln_gain, ln_bias=ln_bias, pool_w=pool_w, pool_b=pool_b, pool_scale=pool_scale, gdn_w_in=gdn_w_in,
                 gdn_conv=gdn_conv, gdn_a_log=gdn_a_log, gdn_dt_bias=gdn_dt_bias, gdn_norm_w=gdn_norm_w, gdn_w_out=gdn_w_out,
                 mlp_w1=mlp_w1, mlp_w2=mlp_w2, ple_gate_w=ple_gate_w, ple_gate_b=ple_gate_b, ple_proj=ple_proj)
    mom = dict(ln_gain=m_ln_gain, ln_bias=m_ln_bias, pool_w=m_pool_w, pool_b=m_pool_b, pool_scale=m_pool_scale,
               gdn_w_in=m_gdn_w_in, gdn_conv=m_gdn_conv, gdn_a_log=m_gdn_a_log, gdn_dt_bias=m_gdn_dt_bias,
               gdn_norm_w=m_gdn_norm_w, gdn_w_out=m_gdn_w_out, mlp_w1=m_mlp_w1, mlp_w2=m_mlp_w2, ple_gate_w=m_ple_gate_w,
               ple_gate_b=m_ple_gate_b, ple_proj=m_ple_proj)
    var = dict(ln_gain=v_ln_gain, ln_bias=v_ln_bias, pool_w=v_pool_w, pool_b=v_pool_b, pool_scale=v_pool_scale,
               gdn_w_in=v_gdn_w_in, gdn_conv=v_gdn_conv, gdn_a_log=v_gdn_a_log, gdn_dt_bias=v_gdn_dt_bias,
               gdn_norm_w=v_gdn_norm_w, gdn_w_out=v_gdn_w_out, mlp_w1=v_mlp_w1, mlp_w2=v_mlp_w2, ple_gate_w=v_ple_gate_w,
               ple_gate_b=v_ple_gate_b, ple_proj=v_ple_proj)

    xi, yi, ci = lax.axis_index("x"), lax.axis_index("y"), lax.axis_index("c")
    me = (2 * xi + yi).reshape(1).astype(jnp.int32)
    place = jnp.stack([2 * xi + yi, ci]).astype(jnp.int32)
    early = [("mlp_w1", 0), ("mlp_w2", 0), ("ple_gate_w", 0), ("ple_proj", 0), ("pool_w", 0)]
    late = [("mlp_w1", 1), ("mlp_w2", 1), ("ple_gate_w", 1), ("ple_proj", 1), ("gdn_w_out", 0), ("gdn_w_in", 0)]
    halved = lambda n: shard[n].reshape(shard[n].shape[0], 2, -1, shard[n].shape[-1])
    placed = lambda ops, tag: [_place_shard(halved(n), me, BF16, f"place_{tag}_{n}", l) for n, l in ops]
    small_in = _place_shard(_pack([shard[n] for n in GATHER_F32], 128, 8)[None], me, F32, "place_small")
    got_early = _all_gather(placed(early, "early") + [small_in], "weights_all_gather_early")
    placed_late = placed(late, "late")
    st = dict(zip(GATHER_F32, _unpack(got_early[-1].reshape(N_SHARD, -1), [shard[n].shape for n in GATHER_F32])))

    cat_last = lambda a: jnp.moveaxis(a, 0, -2).reshape(a.shape[1:-1] + (N_SHARD * a.shape[-1],))
    gain = cat_last(st["ln_gain"])
    bias = cat_last(st["ln_bias"])
    wp = got_early[4].reshape(N_SHARD, 4, POOL_GROUP // N_SHARD, POOL_GROUP)
    pb = cat_last(st["pool_b"]).reshape(1, D_MODEL)
    ps = pool_scale
    conv_w = cat_last(st["gdn_conv"])[0]
    merged = lambda g: g.reshape(N_SHARD, -1, g.shape[-1])
    mlp_w = lambda i, got: (merged(got[0]), merged(got[1]), merged(got[2]), ple_gate_b[i:i + 1], merged(got[3]))
    alog_l = _pad_lanes(gdn_a_log, HEADS)
    dtb_l = _pad_lanes(gdn_dt_bias, HEADS)
    nw = jnp.tile(gdn_norm_w, (1, HEADS))
    ln = lambda i, k: (gain[i, k][None], bias[i, k][None])

    x0 = x[0]
    p0, p1 = p[0, 0], p[1, 0]

    x1, xh1, rs1 = _pool_fwd(x0, wp, pb, ps, *ln(0, 0))
    (x2, xh2, rs2, a0, xb0), got_late = _mlp_fwd(x1, p0, *mlp_w(0, got_early), *ln(0, 1), "mlp_fwd_0", gather=placed_late)
    w_out = got_late[4].reshape(D_MODEL, D_MODEL)
    w_in = _assemble_w_in(merged(got_late[5]))
    proj, y_conv, q, k, v, gcb, xb2 = _conv_fwd(x2, w_in, conv_w, alog_l, dtb_l)
    u, w, qd, kd, qk, gl, t_inv = _gdn_prep(q, k, v, gcb)
    o, states = _gdn_scan(u, w, qd, kd, qk, gl)
    x3, xh3, rs3 = _gdn_out_fwd(o, proj, x2, w_out, nw, *ln(1, 0))
    (dy4, xh4, rs4, a1, xb1, loss_l), _ = _mlp_fwd(x3, p1, *mlp_w(1, got_late), *ln(1, 1), "mlp_fwd_1", target=loss_target[0])

    g_gain = [[None, None], [None, None]]
    g_bias = [[None, None], [None, None]]

    def mlp_grads(i, dy, xh, rs, x_mid, xb, a, p_i, got, scatter=()):
        (dx, dh, dzg, dpp, drb, dg, db, dgb), landed = _mlp_bwd(
            dy, xh, rs, x_mid, a, p_i, *mlp_w(i, got), ln(i, 1)[0], f"mlp_bwd_{i}", scatter=scatter)
        g_gain[i][1], g_bias[i][1] = dg, db
        return dx, dict(
            mlp_w1=_wgrad(xb, dh, f"dw1_{i}", stack_cols=True), mlp_w2=_wgrad(a, drb, f"dw2_{i}").reshape(N_SHARD, -1, D_MODEL),
            ple_gate_w=_wgrad(xb, dzg, f"dgate_w_{i}").reshape(N_SHARD, -1, D_MODEL),
            ple_proj=jnp.moveaxis(_wgrad(p_i, dpp, f"dproj_{i}").reshape(PLE_DIM, N_SHARD, -1), 1, 0), ple_gate_b=dgb), landed

    def chip_sums(pieces, wire, tag):
        others = _swap_halves(pieces, f"grads_swap_halves_{tag}")
        return [_add_half(a, b, place, t, f"grads_add_half_{tag}{i}") for i, (a, b, t) in enumerate(zip(pieces, others, wire))]

    dx3, gl1, _ = mlp_grads(1, dy4, xh4, rs4, x3, xb1, a1, p1, got_late)
    dres, do, dz, on_b, drb3, g_gain[1][0], g_bias[1][0], d_nw = _gdn_out_bwd(dx3, xh3, rs3, o, proj, w_out, nw, ln(1, 0)[0])
    d_wout = _wgrad(on_b, drb3, "dw_out").reshape(N_SHARD, -1, D_MODEL)
    du, dw, dqd, dkd, dqk, dgl = _gdn_scan_bwd(do, u, w, qd, kd, qk, gl, states)
    dq, dk, dv, dgcb = _gdn_prep_bwd(q, k, v, gcb, t_inv, u, w, du, dw, dqd, dkd, dqk, dgl)
    dproj, dx2, d_conv, d_alog_l, d_dtb_l = _conv_bwd(proj, y_conv, dq, dk, dv, dgcb, dz, conv_w, alog_l, dtb_l, w_in, dres)
    d_win = _split_w_in(_wgrad(xb2, dproj, "dw_in"))
    sums_late = chip_sums([gl1["mlp_w1"], gl1["mlp_w2"], gl1["ple_gate_w"], d_wout, d_win], [BF16] * 5, "late")
    dx1, gl0, landed_late = mlp_grads(0, dx2, xh2, rs2, x1, xb0, a0, p0, got_early, scatter=sums_late)
    sums_early = chip_sums([gl0["mlp_w1"], gl0["mlp_w2"], gl0["ple_gate_w"]], [BF16] * 3, "early")
    (dx0, g_gain[0][0], g_bias[0][0], d_ps, d_pb, d_wp), landed_early = _pool_bwd(
        dx1, xh1, rs1, x0, wp, pb, ps, ln(0, 0)[0], scatter=sums_early)

    split_last = lambda a: jnp.moveaxis(a.reshape(a.shape[:-1] + (N_SHARD, a.shape[-1] // N_SHARD)), -2, 0)
    small_st = dict(
        ple_proj=jnp.stack([gl0["ple_proj"], gl1["ple_proj"]], axis=1),
        pool_w=jnp.moveaxis(d_wp.reshape(4, N_SHARD, POOL_GROUP // N_SHARD, POOL_GROUP), 1, 0)[:, None],
        ln_gain=split_last(jnp.stack([jnp.concatenate(r, axis=0) for r in g_gain])),
        ln_bias=split_last(jnp.stack([jnp.concatenate(r, axis=0) for r in g_bias])),
        pool_b=split_last(d_pb.reshape(1, 4, POOL_GROUP)),
        gdn_conv=split_last(d_conv)[:, None],
    )
    rep = dict(pool_scale=d_ps, gdn_a_log=d_alog_l[:, HEADS:2 * HEADS], gdn_dt_bias=d_dtb_l[:, HEADS:2 * HEADS],
               gdn_norm_w=d_nw, ple_gate_b=jnp.concatenate([gl0["ple_gate_b"], gl1["ple_gate_b"]], axis=0))
    for n in REPLICATED:
        small_st[n] = jnp.broadcast_to(rep[n][None], (N_SHARD,) + rep[n].shape)
    small_flat = jnp.concatenate([small_st[n].reshape(N_SHARD, -1) for n in SMALL_GRADS], axis=1)
    small_rows = -(-small_flat.shape[1] // (16 * LANES)) * 16
    small_piece = jnp.pad(small_flat, ((0, 0), (0, small_rows * LANES - small_flat.shape[1]))).reshape(N_SHARD, small_rows, LANES)

    sums_small = chip_sums([small_piece], [F32], "small")
    landed_small = _scatter_chips(sums_small, "grads_scatter_chips_small")
    red = _join_halves([_sum_chips(q_, p_, place, f"grads_sum_chips_{i}") for i, (q_, p_) in
                        enumerate(zip(list(landed_early) + list(landed_small) + list(landed_late),
                                      sums_early + sums_small + sums_late))])
    red = [r.reshape(-1, r.shape[-1]) for r in red]
    grads = dict(mlp_w1=[red[0], red[4]], mlp_w2=[red[1], red[5]], ple_gate_w=[red[2], red[6]], gdn_w_out=red[7], gdn_w_in=red[8])
    grads.update(zip(SMALL_GRADS, _unpack(red[3].reshape(-1), [shard[n].shape for n in SMALL_GRADS])))

    delta, new_m, new_v = {}, {}, {}
    small = [n for n in WEIGHTS if shard[n].size < 128 * 128]
    for n in WEIGHTS:
        if n in small:
            continue
        to2d = lambda a, n=n: a.reshape(-1, shard[n].shape[-1])
        g_n = grads[n] if isinstance(grads[n], list) else to2d(grads[n])
        g2, d2, m2, v2 = _adamw(to2d(shard[n]), g_n, to2d(mom[n]), to2d(var[n]), "adamw_" + n)
        grads[n], delta[n], new_m[n], new_v[n] = (t.reshape(shard[n].shape) for t in (g2, d2, m2, v2))
    pk = lambda d: _pack([d[n] for n in small], 128, 8).reshape(-1, 128)
    _, d2, m2, v2 = _adamw(pk(shard), pk(grads), pk(mom), pk(var), "adamw_small")
    for dst, t in ((delta, d2), (new_m, m2), (new_v, v2)):
        dst.update(zip(small, _unpack(t.reshape(-1), [shard[n].shape for n in small])))

    loss = lax.psum(loss_l[0, 0], ("x", "y", "c"))
    return (loss, dx0[None], *[grads[n] for n in WEIGHTS], *[delta[n] for n in WEIGHTS],
            *[new_m[n] for n in WEIGHTS], *[new_v[n] for n in WEIGHTS])
```

```python
import math

import jax
import jax.numpy as jnp
from jax import lax
from jax.experimental import pallas as pl
from jax.experimental.pallas import tpu as pltpu

F32 = jnp.float32
BF16 = jnp.bfloat16

D_MODEL = 1024
D_FF = 4096
PLE_DIM = 256
N_SHARD = 4
POOL_WINDOWS = (2, 4, 8, 16)
POOL_GROUP = 256
POOL_HALO = 16
HEADS = 8
HEAD_DIM = 128
CHUNK = 64
CONV_WIDTH = 4
CONV_HALO = 8
QKV_DIM = 3 * D_MODEL
GDN_IN_DIM = QKV_DIM + D_MODEL + 2 * HEADS
GDN_IN_PAD = 4224
BA_BLOCK = (QKV_DIM + D_MODEL) // 128
ALPHA = (2.0 * 2) ** 0.25
LN_EPS = 1e-5
RMS_EPS = 1e-6
L2_EPS = 1e-6
ADAM_LR, ADAM_B1, ADAM_B2, ADAM_EPS, ADAM_WD, ADAM_STEP = 0.001, 0.9, 0.999, 1e-08, 0.01, 10

ROW_TILE = 512
CONV_TILE = 256
PREP_CHUNKS = 32
PREP_BWD_CHUNKS = 16
SCAN_CHUNKS = 8
LANES = 1024
ADAM_ROWS = 256

NN = (((1,), (0,)), ((), ()))
NT = (((1,), (1,)), ((), ()))
TN = (((0,), (0,)), ((), ()))
BNN = (((2,), (1,)), ((0,), (0,)))
BNT = (((2,), (2,)), ((0,), (0,)))
BTN = (((1,), (1,)), ((0,), (0,)))
MESH = pl.DeviceIdType.MESH
ANY = pl.BlockSpec(memory_space=pl.ANY)


def _bdot(a, b, dims):
    return lax.dot_general(a.astype(BF16), b.astype(BF16), dims, preferred_element_type=F32)


def _hdot(a, b, dims):
    return lax.dot_general(a, b, dims, precision=lax.Precision.HIGHEST, preferred_element_type=F32)


def _mdot(a, b, dims):
    return lax.dot_general(a, b, dims, precision=lax.Precision.HIGH, preferred_element_type=F32)


def _sigmoid(x):
    return 0.5 * jnp.tanh(0.5 * x) + 0.5


def _silu(x):
    return x * _sigmoid(x)


def _softplus(x):
    return jnp.maximum(x, 0.0) + jnp.log1p(jnp.exp(-jnp.abs(x)))


def _call(body, name, grid, in_specs, out_specs, out_shape, scratch=(), sem=None, aliases=None):
    params = pltpu.CompilerParams(dimension_semantics=sem) if sem else None
    return pl.pallas_call(
        body, name=name, grid=grid, in_specs=in_specs, out_specs=out_specs, out_shape=out_shape,
        scratch_shapes=list(scratch), compiler_params=params, input_output_aliases=aliases or {})


def _row(d):
    return pl.BlockSpec((1, d), lambda *_: (0, 0))


def _full(shape):
    n = len(shape)
    return pl.BlockSpec(shape, lambda *_: (0,) * n)


def _sds(shape, dtype=F32):
    return jax.ShapeDtypeStruct(shape, dtype)


def _ln_fwd(r, gain, bias):
    mu = jnp.mean(r, axis=-1, keepdims=True)
    xc = r - mu
    rstd = lax.rsqrt(jnp.mean(xc * xc, axis=-1, keepdims=True) + LN_EPS)
    xhat = xc * rstd
    return xhat * gain + bias, xhat, rstd


def _ln_bwd(dy, xhat, rstd, gain):
    dxh = dy * gain
    m1 = jnp.mean(dxh, axis=-1, keepdims=True)
    m2 = jnp.mean(dxh * xhat, axis=-1, keepdims=True)
    return rstd * (dxh - m1 - xhat * m2)


def _acc(ref, first, val):
    @pl.when(first)
    def _():
        ref[...] = val

    @pl.when(jnp.logical_not(first))
    def _():
        ref[...] += val


def _pooled_groups(xe, t0, ts):
    pos = (t0 + lax.broadcasted_iota(jnp.int32, (ts, 1), 0) + 1).astype(F32)
    outs = []
    for gi, win in enumerate(POOL_WINDOWS):
        xs = xe[:, gi * POOL_GROUP:(gi + 1) * POOL_GROUP]
        s, k = xs, 1
        while k < win:
            s = s + pltpu.roll(s, k, 0)
            k *= 2
        mean = s[POOL_HALO:] / jnp.minimum(pos, float(win))
        outs.append(mean - xs[POOL_HALO:])
    return outs


def _pool_groups_w(w_ref):
    return [jnp.concatenate([w_ref[s, g] for s in range(N_SHARD)], axis=0) for g in range(4)]


def _pool_fwd(x, wp, pb, ps, gain, bias):
    s_len = x.shape[0]
    ts = min(ROW_TILE, s_len)
    hb = ts // POOL_HALO

    def body(x_ref, halo_ref, w_ref, pb_ref, ps_ref, g_ref, b_ref, y_ref, xhat_ref, rstd_ref):
        i = pl.program_id(0)
        x_t = x_ref[...]
        halo = jnp.where(i > 0, halo_ref[...], 0.0)
        pooled = _pooled_groups(jnp.concatenate([halo, x_t], axis=0), i * ts, ts)
        wg = _pool_groups_w(w_ref)
        y = jnp.concatenate([_bdot(pooled[g], wg[g], NN) for g in range(4)], axis=1) + pb_ref[...]
        r = ALPHA * x_t + y * ps_ref[...]
        y_ref[...], xhat_ref[...], rstd_ref[...] = _ln_fwd(r, g_ref[...], b_ref[...])

    tile = pl.BlockSpec((ts, D_MODEL), lambda i: (i, 0))
    return _call(
        body, "pool_fwd", (s_len // ts,),
        [tile, pl.BlockSpec((POOL_HALO, D_MODEL), lambda i: (jnp.maximum(i * hb - 1, 0), 0)),
         _full(wp.shape), _row(D_MODEL), _row(D_MODEL), _row(D_MODEL), _row(D_MODEL)],
        [tile, tile, pl.BlockSpec((ts, 1), lambda i: (i, 0))],
        [_sds((s_len, D_MODEL)), _sds((s_len, D_MODEL)), _sds((s_len, 1))],
        sem=("parallel",))(x, x, wp, pb, ps, gain, bias)


def _pool_bwd(dy, xhat, rstd, x, wp, pb, ps, gain, scatter=()):
    s_len = x.shape[0]
    ts = min(ROW_TILE, s_len)
    hb = ts // POOL_HALO
    n_t = s_len // ts
    ne = ts + POOL_HALO
    n_s = len(scatter)

    def body(*refs):
        dy_ref, dyn_ref, xh_ref, xhn_ref, rs_ref, rsn_ref, x_ref, xp_ref, w_ref, pb_ref, ps_ref, g_ref = refs[:12]
        dx_ref, dg_ref, db_ref, dps_ref, dpb_ref, dw_ref = refs[12 + n_s:18 + n_s]
        i = pl.program_id(0)
        if n_s:
            start, finish = _scatter_steps(refs[12:12 + n_s], refs[18 + n_s:18 + 2 * n_s], *refs[18 + 2 * n_s:])
            pl.when(i == 0)(start)
        more = i < n_t - 1
        dy_t, xh_t = dy_ref[...], xh_ref[...]
        dy_e = jnp.concatenate([dy_t, jnp.where(more, dyn_ref[...], 0.0)], axis=0)
        xh_e = jnp.concatenate([xh_t, xhn_ref[...]], axis=0)
        rs_e = jnp.concatenate([rs_ref[...], rsn_ref[...]], axis=0)
        dr_e = _ln_bwd(dy_e, xh_e, rs_e, g_ref[...])
        dyy_e = dr_e * ps_ref[...]
        pos_e = (i * ts + lax.broadcasted_iota(jnp.int32, (ne, 1), 0) + 1).astype(F32)
        dxs = []
        wg = _pool_groups_w(w_ref)
        for gi, win in enumerate(POOL_WINDOWS):
            sl = slice(gi * POOL_GROUP, (gi + 1) * POOL_GROUP)
            dpool = _bdot(dyy_e[:, sl], wg[gi], NT)
            s, k = dpool / jnp.minimum(pos_e, float(win)), 1
            while k < win:
                s = s + pltpu.roll(s, ne - k, 0)
                k *= 2
            dxs.append(s[:ts] - dpool[:ts])
        dx_ref[...] = ALPHA * dr_e[:ts] + jnp.concatenate(dxs, axis=1)

        x_t = x_ref[...]
        halo = jnp.where(i > 0, xp_ref[...], 0.0)
        pooled = _pooled_groups(jnp.concatenate([halo, x_t], axis=0), i * ts, ts)
        y = jnp.concatenate([_bdot(pooled[g], wg[g], NN) for g in range(4)], axis=1) + pb_ref[...]
        dr_t, dyy_t = dr_e[:ts], dyy_e[:ts]
        first = i == 0
        _acc(dg_ref, first, jnp.sum(dy_t * xh_t, axis=0, keepdims=True))
        _acc(db_ref, first, jnp.sum(dy_t, axis=0, keepdims=True))
        _acc(dps_ref, first, jnp.sum(dr_t * y, axis=0, keepdims=True))
        _acc(dpb_ref, first, jnp.sum(dyy_t, axis=0, keepdims=True))
        for g in range(4):
            _acc(dw_ref.at[g], first, _bdot(pooled[g], dyy_t[:, g * POOL_GROUP:(g + 1) * POOL_GROUP], TN))
        if n_s:
            pl.when(i == n_t - 1)(finish)

    tile = pl.BlockSpec((ts, D_MODEL), lambda i: (i, 0))
    nxt = pl.BlockSpec((POOL_HALO, D_MODEL), lambda i: (jnp.minimum((i + 1) * hb, n_t * hb - 1), 0))
    prv = pl.BlockSpec((POOL_HALO, D_MODEL), lambda i: (jnp.maximum(i * hb - 1, 0), 0))
    rs_t = pl.BlockSpec((ts, 1), lambda i: (i, 0))
    rs_n = pl.BlockSpec((POOL_HALO, 1), lambda i: (jnp.minimum((i + 1) * hb, n_t * hb - 1), 0))
    row = _row(D_MODEL)
    out = _call(
        body, "pool_bwd", (n_t,),
        [tile, nxt, tile, nxt, rs_t, rs_n, tile, prv, _full(wp.shape), row, row, row] + [ANY] * n_s,
        [tile, row, row, row, row, _full((4, POOL_GROUP, POOL_GROUP))] + [ANY] * n_s,
        [_sds((s_len, D_MODEL))] + [_sds((1, D_MODEL))] * 4 + [_sds((4, POOL_GROUP, POOL_GROUP))]
        + [_sds(t.shape, t.dtype) for t in scatter],
        scratch=_scatter_sems(n_s) if n_s else [], sem=("arbitrary",),
    )(dy, dy, xhat, xhat, rstd, rstd, x, x, wp, pb, ps, gain, *scatter)
    return out[:6], out[6:]


def _mlp_weight_specs():
    fc = D_FF // N_SHARD
    return [pl.BlockSpec((None, D_MODEL, fc), lambda i, j: (j, 0, 0)),
            pl.BlockSpec((None, fc, D_MODEL), lambda i, j: (j, 0, 0)),
            _full((N_SHARD, D_MODEL // N_SHARD, D_MODEL)),
            _row(D_MODEL),
            _full((N_SHARD, PLE_DIM, D_MODEL // N_SHARD))]


def _gate_w(gw_ref):
    return gw_ref[...].reshape(D_MODEL, D_MODEL)


def _ple_proj(pj_ref):
    return jnp.concatenate([pj_ref[s] for s in range(N_SHARD)], axis=1)


def _mlp_fwd(x, p, w1s, w2s, gw, gb, proj, gain, bias, name, gather=(), target=None):
    s_len = x.shape[0]
    ts = min(ROW_TILE, s_len)
    n_i = s_len // ts
    n_g = len(gather)
    has_t = target is not None
    n_in, n_out = 9 + has_t, 5 + has_t

    def body(*refs):
        x_ref, p_ref, w1_ref, w2_ref, gw_ref, gb_ref, pj_ref, g_ref, b_ref = refs[:9]
        y_ref, xhat_ref, rstd_ref, a_ref, xbo_ref = refs[n_in + n_g:n_in + n_g + 5]
        acc_ref, xb_ref = refs[n_in + n_out + 2 * n_g:n_in + n_out + 2 * n_g + 2]
        i, j = pl.program_id(0), pl.program_id(1)
        if n_g:
            start, finish = _gather_steps(refs[n_in + n_out + n_g:n_in + n_out + 2 * n_g], *refs[n_in + n_out + 2 * n_g + 2:])
            pl.when((i == 0) & (j == 0))(start)

        @pl.when(j == 0)
        def _():
            x_t = x_ref[...]
            xb_ref[...] = x_t.astype(BF16)
            xbo_ref[...] = x_t.astype(BF16)
            gate = _sigmoid(_bdot(x_t, _gate_w(gw_ref), NN) + gb_ref[...])
            acc_ref[...] = ALPHA * x_t + gate * _bdot(p_ref[...], _ple_proj(pj_ref), NN)

        h = jnp.maximum(_bdot(xb_ref[...], w1_ref[...], NN), 0.0)
        a = (h * h).astype(BF16)
        a_ref[...] = a
        acc_ref[...] += _bdot(a, w2_ref[...], NN)

        @pl.when(j == N_SHARD - 1)
        def _():
            y, xhat_ref[...], rstd_ref[...] = _ln_fwd(acc_ref[...], g_ref[...], b_ref[...])
            if has_t:
                err = y - refs[9][...]
                y_ref[...] = err * (1.0 / D_MODEL)
                part = 0.5 * jnp.sum(jnp.mean(err * err, axis=-1, keepdims=True))
                _acc(refs[n_in + n_g + 5], i == 0, part + jnp.zeros((1, 128), F32))
            else:
                y_ref[...] = y

        if n_g:
            pl.when((i == n_i - 1) & (j == N_SHARD - 1))(finish)

    tile = pl.BlockSpec((ts, D_MODEL), lambda i, j: (i, 0))
    row = _row(D_MODEL)
    out = _call(
        body, name, (n_i, N_SHARD),
        [tile, pl.BlockSpec((ts, PLE_DIM), lambda i, j: (i, 0))] + _mlp_weight_specs() + [row, row] + [tile] * has_t
        + [ANY] * n_g,
        [tile, tile, pl.BlockSpec((ts, 1), lambda i, j: (i, 0)), pl.BlockSpec((ts, D_FF // N_SHARD), lambda i, j: (i, j)), tile]
        + [_row(128)] * has_t + [ANY] * n_g,
        [_sds((s_len, D_MODEL)), _sds((s_len, D_MODEL)), _sds((s_len, 1)), _sds((s_len, D_FF), BF16), _sds((s_len, D_MODEL), BF16)]
        + [_sds((1, 128))] * has_t + [_sds(a.shape, a.dtype) for a in gather],
        scratch=[pltpu.VMEM((ts, D_MODEL), F32), pltpu.VMEM((ts, D_MODEL), BF16)] + (_gather_sems(n_g) if n_g else []),
        sem=("arbitrary", "arbitrary"), aliases={n_in + k: n_out + k for k in range(n_g)},
    )(x, p, w1s, w2s, gw, gb, proj, gain, bias, *([target] if has_t else []), *gather)
    return out[:n_out], out[n_out:]


def _mlp_bwd(dy, xhat, rstd, x, a, p, w1s, w2s, gw, gb, proj, gain, name, scatter=()):
    s_len = x.shape[0]
    ts = min(ROW_TILE, s_len)
    fc = D_FF // N_SHARD
    n_i = s_len // ts
    n_s = len(scatter)

    def body(*refs):
        dy_ref, xh_ref, rs_ref, x_ref, a_ref, p_ref, w1_ref, w2_ref, gw_ref, gb_ref, pj_ref, g_ref = refs[:12]
        dx_ref, dh_ref, dzg_ref, dpp_ref, drb_ref, dg_ref, db_ref, dgb_ref = refs[12 + n_s:20 + n_s]
        acc_ref, dr_ref = refs[20 + 2 * n_s:22 + 2 * n_s]
        i, j = pl.program_id(0), pl.program_id(1)
        if n_s:
            start, finish = _scatter_steps(refs[12:12 + n_s], refs[20 + n_s:20 + 2 * n_s], *refs[22 + 2 * n_s:])
            pl.when((i == 0) & (j == 0))(start)

        @pl.when(j == 0)
        def _():
            dy_t, xh_t, x_t = dy_ref[...], xh_ref[...], x_ref[...]
            dr = _ln_bwd(dy_t, xh_t, rs_ref[...], g_ref[...])
            drb = dr.astype(BF16)
            dr_ref[...] = drb
            drb_ref[...] = drb
            gw_full = _gate_w(gw_ref)
            gate = _sigmoid(_bdot(x_t, gw_full, NN) + gb_ref[...])
            pp = _bdot(p_ref[...], _ple_proj(pj_ref), NN)
            dzg = dr * pp * gate * (1.0 - gate)
            dzg_ref[...] = dzg.astype(BF16)
            dpp_ref[...] = (dr * gate).astype(BF16)
            acc_ref[...] = ALPHA * dr + _bdot(dzg, gw_full, NT)
            first = i == 0
            _acc(dg_ref, first, jnp.sum(dy_t * xh_t, axis=0, keepdims=True))
            _acc(db_ref, first, jnp.sum(dy_t, axis=0, keepdims=True))
            _acc(dgb_ref, first, jnp.sum(dzg, axis=0, keepdims=True))

        dh = (_bdot(dr_ref[...], w2_ref[...], NT) * (2.0 * jnp.sqrt(a_ref[...].astype(F32)))).astype(BF16)
        dh_ref[...] = dh
        acc_ref[...] += _bdot(dh, w1_ref[...], NT)

        @pl.when(j == N_SHARD - 1)
        def _():
            dx_ref[...] = acc_ref[...]

        if n_s:
            pl.when((i == n_i - 1) & (j == N_SHARD - 1))(finish)

    tile = pl.BlockSpec((ts, D_MODEL), lambda i, j: (i, 0))
    ftile = pl.BlockSpec((ts, fc), lambda i, j: (i, j))
    row = _row(D_MODEL)
    out = _call(
        body, name, (n_i, N_SHARD),
        [tile, tile, pl.BlockSpec((ts, 1), lambda i, j: (i, 0)), tile, ftile, pl.BlockSpec((ts, PLE_DIM), lambda i, j: (i, 0))]
        + _mlp_weight_specs() + [row] + [ANY] * n_s,
        [tile, ftile, tile, tile, tile, row, row, row] + [ANY] * n_s,
        [_sds((s_len, D_MODEL)), _sds((s_len, D_FF), BF16)]
        + [_sds((s_len, D_MODEL), BF16)] * 3 + [_sds((1, D_MODEL))] * 3 + [_sds(t.shape, t.dtype) for t in scatter],
        scratch=[pltpu.VMEM((ts, D_MODEL), F32), pltpu.VMEM((ts, D_MODEL), BF16)] + (_scatter_sems(n_s) if n_s else []),
        sem=("arbitrary", "arbitrary"))(dy, xhat, rstd, x, a, p, w1s, w2s, gw, gb, proj, gain, *scatter)
    return out[:8], out[8:]


def _wgrad(a, b, name, stack_cols=False):
    s_len, m = a.shape
    n = b.shape[1]
    ts = min(2048 if a.dtype == BF16 and b.dtype == BF16 else 1024, s_len)
    tm = min(m, 1024)
    tn = n // N_SHARD if stack_cols else (1408 if n == GDN_IN_PAD else min(n, 1024))
    n_s = s_len // ts

    def body(a_ref, b_ref, o_ref):
        _acc(o_ref, pl.program_id(2) == 0, _bdot(a_ref[...], b_ref[...], TN))

    if stack_cols:
        out_spec = pl.BlockSpec((None, tm, tn), lambda mi, nj, s: (nj, mi, 0))
        out_shape = _sds((N_SHARD, m, tn))
    else:
        out_spec = pl.BlockSpec((tm, tn), lambda mi, nj, s: (mi, nj))
        out_shape = _sds((m, n))
    return _call(
        body, name, (m // tm, n // tn, n_s),
        [pl.BlockSpec((ts, tm), lambda mi, nj, s: (s, mi)), pl.BlockSpec((ts, tn), lambda mi, nj, s: (s, nj))],
        out_spec, out_shape, sem=("parallel", "parallel", "arbitrary"))(a, b)


def _act_qkv(y):
    qkv = _silu(y)
    qs, ks = [], []
    for h in range(HEADS):
        qh = qkv[:, h * HEAD_DIM:(h + 1) * HEAD_DIM]
        kh = qkv[:, D_MODEL + h * HEAD_DIM:D_MODEL + (h + 1) * HEAD_DIM]
        qs.append(qh * (lax.rsqrt(jnp.sum(qh * qh, axis=-1, keepdims=True) + L2_EPS) * HEAD_DIM ** -0.5))
        ks.append(kh * lax.rsqrt(jnp.sum(kh * kh, axis=-1, keepdims=True) + L2_EPS))
    return jnp.concatenate(qs, axis=1), jnp.concatenate(ks, axis=1), qkv[:, 2 * D_MODEL:]


def _act_gb(ba, alog_l, dtb_l, tril):
    lane = lax.broadcasted_iota(jnp.int32, ba.shape, 1)
    g = jnp.where((lane >= HEADS) & (lane < 2 * HEADS), -jnp.exp(alog_l) * _softplus(ba + dtb_l), 0.0)
    return jnp.where(lane < HEADS, _sigmoid(ba), _hdot(tril, g, NN))


def _chunk_tril(t):
    ii = lax.broadcasted_iota(jnp.int32, (t, t), 0)
    jj = lax.broadcasted_iota(jnp.int32, (t, t), 1)
    return ((ii // CHUNK == jj // CHUNK) & (ii >= jj)).astype(F32)


def _conv_rows(xe, w, n_rows):
    y = xe[CONV_HALO:CONV_HALO + n_rows] * w[CONV_WIDTH - 1]
    for j in range(CONV_WIDTH - 1):
        y = y + pltpu.roll(xe, CONV_WIDTH - 1 - j, 0)[CONV_HALO:CONV_HALO + n_rows] * w[j]
    return y


def _conv_fwd(x, w_in, conv_w, alog_l, dtb_l):
    s_len = x.shape[0]
    ts = min(CONV_TILE, s_len)
    hb = ts // CONV_HALO

    def body(x_ref, xp_ref, w_ref, al_ref, dt_ref, win_hbm, proj_ref, y_ref, q_ref, k_ref, v_ref, gcb_ref, win_ref, win_sem):
        i = pl.program_id(0)

        @pl.when(i == 0)
        def _():
            cp = pltpu.make_async_copy(win_hbm, win_ref, win_sem)
            cp.start()
            cp.wait()

        proj = _bdot(x_ref[...], win_ref[...], NN)
        proj_ref[...] = proj
        halo = jnp.where(i > 0, _bdot(xp_ref[...], win_ref[:, :QKV_DIM], NN), 0.0)
        taps = [w_ref[pl.ds(j, 1), :] for j in range(CONV_WIDTH)]
        y = _conv_rows(jnp.concatenate([halo, proj[:, :QKV_DIM]], axis=0), taps, ts)
        y_ref[...] = y
        q_ref[...], k_ref[...], v_ref[...] = _act_qkv(y)
        gcb_ref[...] = _act_gb(proj[:, BA_BLOCK * 128:], al_ref[...], dt_ref[...], _chunk_tril(ts))

    tile = pl.BlockSpec((ts, D_MODEL), lambda i: (i, 0))
    return _call(
        body, "gdn_conv_fwd", (s_len // ts,),
        [tile, pl.BlockSpec((CONV_HALO, D_MODEL), lambda i: (jnp.maximum(i * hb - 1, 0), 0)),
         _full((CONV_WIDTH, QKV_DIM)), _row(128), _row(128), ANY],
        [pl.BlockSpec((ts, GDN_IN_PAD), lambda i: (i, 0)), pl.BlockSpec((ts, QKV_DIM), lambda i: (i, 0)), tile, tile, tile,
         pl.BlockSpec((ts, 128), lambda i: (i, 0))],
        [_sds((s_len, GDN_IN_PAD)), _sds((s_len, QKV_DIM))] + [_sds((s_len, D_MODEL))] * 3 + [_sds((s_len, 128))],
        scratch=[pltpu.VMEM(w_in.shape, w_in.dtype), pltpu.SemaphoreType.DMA],
        sem=("arbitrary",))(x, x, conv_w, alog_l, dtb_l, w_in)


def _conv_bwd(proj, y, dq, dk, dv, dgcb, dz, conv_w, alog_l, dtb_l, w_in, dres):
    s_len = proj.shape[0]
    ts = min(CONV_TILE, s_len)
    hb = ts // CONV_HALO
    n_t = s_len // ts
    te = ts + CONV_HALO

    def body(x_ref, y_ref, yn_ref, ba_ref, dq_ref, dqn_ref, dk_ref, dkn_ref, dv_ref, dvn_ref, dgcb_ref, dz_ref,
             w_ref, al_ref, dt_ref, win_hbm, dres_ref, dp_ref, dx_ref, dw_ref, dal_ref, ddt_ref, win_ref, win_sem):
        i = pl.program_id(0)

        @pl.when(i == 0)
        def _():
            cp = pltpu.make_async_copy(win_hbm, win_ref, win_sem)
            cp.start()
            cp.wait()

        more = i < n_t - 1
        w = [w_ref[pl.ds(j, 1), :] for j in range(CONV_WIDTH)]
        x_t = x_ref[...]
        _, act_vjp = jax.vjp(_act_qkv, jnp.concatenate([y_ref[...], yn_ref[...]], axis=0))
        ct = tuple(jnp.concatenate([t[...], jnp.where(more, n[...], 0.0)], axis=0)
                   for t, n in ((dq_ref, dqn_ref), (dk_ref, dkn_ref), (dv_ref, dvn_ref)))
        (dy_e,) = act_vjp(ct)
        ahead = [pltpu.roll(dy_e, te - (CONV_WIDTH - 1 - j), 0)[:ts] for j in range(CONV_WIDTH - 1)] + [dy_e[:ts]]
        dx = ahead[CONV_WIDTH - 1] * w[CONV_WIDTH - 1]
        for j in range(CONV_WIDTH - 1):
            dx = dx + ahead[j] * w[j]
        dws = [jnp.sum(ahead[j] * x_t, axis=0, keepdims=True) for j in range(CONV_WIDTH)]
        _, gb_vjp = jax.vjp(lambda ba, al, dt: _act_gb(ba, al, dt, _chunk_tril(ts)), ba_ref[...], al_ref[...], dt_ref[...])
        dba, dal, ddt = gb_vjp(dgcb_ref[...])
        dp = jnp.concatenate([dx.astype(BF16), dz_ref[...], dba.astype(BF16)], axis=1)
        dp_ref[...] = dp
        dx_ref[...] = dres_ref[...] + _bdot(dp, win_ref[...], NT)
        first = i == 0
        for j in range(CONV_WIDTH):
            _acc(dw_ref.at[pl.ds(j, 1), :], first, dws[j])
        _acc(dal_ref, first, dal)
        _acc(ddt_ref, first, ddt)

    tile = pl.BlockSpec((ts, D_MODEL), lambda i: (i, 0))
    nxt = pl.BlockSpec((CONV_HALO, D_MODEL), lambda i: (jnp.minimum((i + 1) * hb, n_t * hb - 1), 0))
    return _call(
        body, "gdn_conv_bwd", (n_t,),
        [pl.BlockSpec((ts, QKV_DIM), lambda i: (i, 0)), pl.BlockSpec((ts, QKV_DIM), lambda i: (i, 0)),
         pl.BlockSpec((CONV_HALO, QKV_DIM), lambda i: (jnp.minimum((i + 1) * hb, n_t * hb - 1), 0)),
         pl.BlockSpec((ts, 128), lambda i: (i, BA_BLOCK)),
         tile, nxt, tile, nxt, tile, nxt, pl.BlockSpec((ts, 128), lambda i: (i, 0)), tile,
         _full((CONV_WIDTH, QKV_DIM)), _row(128), _row(128), ANY, tile],
        [pl.BlockSpec((ts, GDN_IN_PAD), lambda i: (i, 0)), tile, _full((CONV_WIDTH, QKV_DIM)), _row(128), _row(128)],
        [_sds((s_len, GDN_IN_PAD), BF16), _sds((s_len, D_MODEL)), _sds((CONV_WIDTH, QKV_DIM)), _sds((1, 128)), _sds((1, 128))],
        scratch=[pltpu.VMEM(w_in.shape, w_in.dtype), pltpu.SemaphoreType.DMA], sem=("arbitrary",),
    )(proj, y, y, proj, dq, dq, dk, dk, dv, dv, dgcb, dz, conv_w, alog_l, dtb_l, w_in, dres)


def _tri_inv(a_strict):
    ii = lax.broadcasted_iota(jnp.int32, (CHUNK, CHUNK), 0)
    jj = lax.broadcasted_iota(jnp.int32, (CHUNK, CHUNK), 1)
    x = (ii == jj).astype(F32) - a_strict
    pw = _bdot(a_strict, a_strict, BNN)
    for step in range(5):
        x = x + _bdot(x, pw, BNN)
        if step < 4:
            pw = _bdot(pw, pw, BNN)
    return x


@jax.custom_vjp
def _solved(a_strict, rhs, t, sol):
    return sol


def _solved_fwd(a_strict, rhs, t, sol):
    return sol, (t, sol)


def _solved_bwd(res, dsol):
    t, sol = res
    drhs = _mdot(t, dsol, BTN)
    return -_mdot(drhs, sol, BNT), drhs, jnp.zeros_like(t), jnp.zeros_like(sol)


_solved.defvjp(_solved_fwd, _solved_bwd)


def _prep(q, k, v, gc, beta, solve):
    ii = lax.broadcasted_iota(jnp.int32, (CHUNK, CHUNK), 0)
    jj = lax.broadcasted_iota(jnp.int32, (CHUNK, CHUNK), 1)
    causal, strict = ii >= jj, ii > jj
    gc_row = jnp.sum((ii == jj).astype(F32) * gc, axis=1, keepdims=True)
    decay = jnp.where(causal, jnp.exp(jnp.where(causal, gc - gc_row, 0.0)), 0.0)
    kb = k * beta
    a = jnp.where(strict, _bdot(kb, k, BNT) * decay, 0.0)
    eg = jnp.exp(gc)
    sol = solve(a, jnp.concatenate([v * beta, kb * eg], axis=-1))
    qk = _bdot(q, k, BNT) * decay
    last = lax.broadcasted_iota(jnp.int32, (CHUNK, 1), 0) == CHUNK - 1
    g_last = jnp.sum(jnp.where(last, gc, 0.0), axis=1, keepdims=True)
    kd = k * jnp.exp(g_last - gc)
    gl = jnp.exp(g_last) + jnp.zeros((1, 1, HEAD_DIM), F32)
    return sol[..., :HEAD_DIM], sol[..., HEAD_DIM:], qk, q * eg, kd, gl


def _prep_specs(s_len, chunks):
    gl_m = min(PREP_CHUNKS, s_len // CHUNK)
    m = min(chunks, gl_m)
    rows = m * CHUNK
    per = gl_m // m
    hd = pl.BlockSpec((rows, HEAD_DIM), lambda c, h: (c, h))
    gcb = pl.BlockSpec((rows, 128), lambda c, h: (c, 0))
    qk = pl.BlockSpec((None, rows, CHUNK), lambda c, h: (h, c, 0))
    gl = pl.BlockSpec((None, m, HEADS, HEAD_DIM), lambda c, h: (c // per, c % per, 0, 0))
    return rows, m, hd, gcb, qk, gl


def _head_cols(gcb, h, m):
    lane = lax.broadcasted_iota(jnp.int32, gcb.shape, 1)
    pick = lambda at: jnp.sum(jnp.where(lane == at, gcb, 0.0), axis=1, keepdims=True).reshape(m, CHUNK, 1)
    return pick(h + HEADS), pick(h)


def _gdn_prep(q, k, v, gcb):
    s_len = q.shape[0]
    rows, m, hd, gcb_spec, qk_spec, gl_spec = _prep_specs(s_len, PREP_CHUNKS)

    def body(q_ref, k_ref, v_ref, gcb_ref, u_ref, w_ref, qd_ref, kd_ref, qk_ref, gl_ref, t_ref):
        r3 = lambda ref, d: ref[...].reshape(m, CHUNK, d)
        gc, beta = _head_cols(gcb_ref[...], pl.program_id(1), m)

        def solve(a, rhs):
            t = _tri_inv(a)
            t_ref[...] = t.reshape(rows, CHUNK)
            return _mdot(t, rhs, BNN)

        u, w, qk, qd, kd, gl = _prep(r3(q_ref, HEAD_DIM), r3(k_ref, HEAD_DIM), r3(v_ref, HEAD_DIM), gc, beta, solve)
        u_ref[...] = u.reshape(rows, HEAD_DIM)
        w_ref[...] = w.reshape(rows, HEAD_DIM)
        qd_ref[...] = qd.reshape(rows, HEAD_DIM).astype(BF16)
        kd_ref[...] = kd.reshape(rows, HEAD_DIM).astype(BF16)
        qk_ref[...] = qk.reshape(rows, CHUNK).astype(BF16)
        gl_ref[:, pl.ds(pl.program_id(1), 1), :] = gl

    n_g = s_len // rows
    return _call(
        body, "gdn_prep", (n_g, HEADS), [hd, hd, hd, gcb_spec], [hd, hd, hd, hd, qk_spec, gl_spec, qk_spec],
        [_sds((s_len, D_MODEL))] * 2 + [_sds((s_len, D_MODEL), BF16)] * 2
        + [_sds((HEADS, s_len, CHUNK), BF16), _sds((n_g, m, HEADS, HEAD_DIM)), _sds((HEADS, s_len, CHUNK))],
        sem=("parallel", "arbitrary"))(q, k, v, gcb)


def _gdn_prep_bwd(q, k, v, gcb, t_inv, u, w, du, dw, dqd, dkd, dqk, dgl):
    s_len = q.shape[0]
    rows, m, hd, gcb_spec, qk_spec, gl_spec = _prep_specs(s_len, PREP_BWD_CHUNKS)

    def body(q_ref, k_ref, v_ref, gcb_ref, t_ref, u_ref, w_ref, du_ref, dw_ref, dqd_ref, dkd_ref, dqk_ref, dgl_ref,
             dq_ref, dk_ref, dv_ref, dgcb_ref):
        h = pl.program_id(1)
        r3 = lambda ref, d: ref[...].reshape(m, CHUNK, d)
        gc, beta = _head_cols(gcb_ref[...], h, m)
        t = r3(t_ref, CHUNK)
        sol = jnp.concatenate([r3(u_ref, HEAD_DIM), r3(w_ref, HEAD_DIM)], axis=-1)
        fn = lambda q_, k_, v_, gc_, bt_: _prep(q_, k_, v_, gc_, bt_, lambda a, rhs: _solved(a, rhs, t, sol))
        _, vjp = jax.vjp(fn, r3(q_ref, HEAD_DIM), r3(k_ref, HEAD_DIM), r3(v_ref, HEAD_DIM), gc, beta)
        ct = (r3(du_ref, HEAD_DIM), r3(dw_ref, HEAD_DIM), r3(dqk_ref, CHUNK), r3(dqd_ref, HEAD_DIM), r3(dkd_ref, HEAD_DIM),
              dgl_ref[:, pl.ds(h, 1), :] * (1.0 / HEAD_DIM))
        dq, dk, dv, dgc, dbt = vjp(ct)
        dq_ref[...] = dq.reshape(rows, HEAD_DIM)
        dk_ref[...] = dk.reshape(rows, HEAD_DIM)
        dv_ref[...] = dv.reshape(rows, HEAD_DIM)
        lane = lax.broadcasted_iota(jnp.int32, (rows, 128), 1)
        mine = jnp.where(lane == h, dbt.reshape(rows, 1), 0.0) + jnp.where(lane == h + HEADS, dgc.reshape(rows, 1), 0.0)
        _acc(dgcb_ref, h == 0, mine)

    return _call(
        body, "gdn_prep_bwd", (s_len // rows, HEADS),
        [hd, hd, hd, gcb_spec, qk_spec, hd, hd, hd, hd, hd, hd, qk_spec, gl_spec], [hd, hd, hd, gcb_spec],
        [_sds((s_len, D_MODEL))] * 3 + [_sds((s_len, 128))],
        sem=("parallel", "arbitrary"))(q, k, v, gcb, t_inv, u, w, du, dw, dqd, dkd, dqk, dgl)


def _scan_specs(n_c, m, k, reverse):
    n_b = n_c // k
    at = (lambda n: n_b - 1 - n) if reverse else (lambda n: n)
    row = pl.BlockSpec((k * CHUNK, D_MODEL), lambda n: (at(n), 0))
    qk = pl.BlockSpec((HEADS, k * CHUNK, CHUNK), lambda n: (0, at(n), 0))
    gl = pl.BlockSpec((None, k, HEADS, HEAD_DIM), lambda n: (at(n) // (m // k), at(n) % (m // k), 0, 0))
    st = pl.BlockSpec((k, HEADS, HEAD_DIM, HEAD_DIM), lambda n: (at(n), 0, 0, 0))
    return row, qk, gl, st


def _gdn_scan(u, w, qd, kd, qk, gl):
    s_len = u.shape[0]
    n_c = s_len // CHUNK
    k = min(SCAN_CHUNKS, gl.shape[1])
    row, qk_spec, gl_spec, st_spec = _scan_specs(n_c, gl.shape[1], k, False)

    def body(u_ref, w_ref, qd_ref, kd_ref, qk_ref, gl_ref, o_ref, st_ref, state):
        hs = range(HEADS)
        sl = [slice(h * HEAD_DIM, (h + 1) * HEAD_DIM) for h in hs]
        first = pl.program_id(0) == 0
        s_all = [jnp.where(first, 0.0, state[h]) for h in hs]
        for c in range(k):
            rows = slice(c * CHUNK, (c + 1) * CHUNK)
            s_b = [s.astype(BF16) for s in s_all]
            ws = [_bdot(w_ref[rows, sl[h]], s_b[h], NN) for h in hs]
            qs = [_bdot(qd_ref[rows, sl[h]], s_b[h], NN) for h in hs]
            vn = [(u_ref[rows, sl[h]] - ws[h]).astype(BF16) for h in hs]
            outs = [qs[h] + _bdot(qk_ref[h, rows, :], vn[h], NN) for h in hs]
            nxt = [s_all[h] * gl_ref[c, pl.ds(h, 1), :] + _bdot(kd_ref[rows, sl[h]], vn[h], TN) for h in hs]
            for h in hs:
                st_ref[c, h] = s_b[h]
                o_ref[rows, sl[h]] = outs[h]
            s_all = nxt
        for h in hs:
            state[h] = s_all[h]

    return _call(
        body, "gdn_scan", (n_c // k,), [row, row, row, row, qk_spec, gl_spec], [row, st_spec],
        [_sds((s_len, D_MODEL)), _sds((n_c, HEADS, HEAD_DIM, HEAD_DIM), BF16)],
        scratch=[pltpu.VMEM((HEADS, HEAD_DIM, HEAD_DIM), F32)], sem=("arbitrary",))(u, w, qd, kd, qk, gl)


def _gdn_scan_bwd(do, u, w, qd, kd, qk, gl, states):
    s_len = u.shape[0]
    n_c = s_len // CHUNK
    k = min(SCAN_CHUNKS, gl.shape[1])
    row, qk_spec, gl_spec, st_spec = _scan_specs(n_c, gl.shape[1], k, True)

    def body(do_ref, u_ref, w_ref, qd_ref, kd_ref, qk_ref, gl_ref, st_ref,
             du_ref, dw_ref, dqd_ref, dkd_ref, dqk_ref, dgl_ref, dstate):
        hs = range(HEADS)
        sl = [slice(h * HEAD_DIM, (h + 1) * HEAD_DIM) for h in hs]
        first = pl.program_id(0) == 0
        ds_f = [jnp.where(first, 0.0, dstate[h]) for h in hs]
        for c in reversed(range(k)):
            rows = slice(c * CHUNK, (c + 1) * CHUNK)
            s_b = [st_ref[c, h] for h in hs]
            ds_b = [d.astype(BF16) for d in ds_f]
            do_b = [do_ref[rows, sl[h]].astype(BF16) for h in hs]
            w_b = [w_ref[rows, sl[h]].astype(BF16) for h in hs]
            ws = [_bdot(w_b[h], s_b[h], NN) for h in hs]
            dvn = [_bdot(qk_ref[h, rows, :], do_b[h], TN) + _bdot(kd_ref[rows, sl[h]], ds_b[h], NN) for h in hs]
            dqd = [_bdot(do_b[h], s_b[h], NT) for h in hs]
            t_do = [_bdot(qd_ref[rows, sl[h]], do_b[h], TN) for h in hs]
            vn = [(u_ref[rows, sl[h]] - ws[h]).astype(BF16) for h in hs]
            dvn_b = [d.astype(BF16) for d in dvn]
            dw = [-_bdot(dvn_b[h], s_b[h], NT) for h in hs]
            dkd = [_bdot(vn[h], ds_b[h], NT) for h in hs]
            dqk = [_bdot(do_b[h], vn[h], NT) for h in hs]
            t_dv = [_bdot(w_b[h], dvn_b[h], TN) for h in hs]
            for h in hs:
                du_ref[rows, sl[h]], dw_ref[rows, sl[h]], dqd_ref[rows, sl[h]], dkd_ref[rows, sl[h]] = dvn[h], dw[h], dqd[h], dkd[h]
                dqk_ref[h, rows, :] = dqk[h]
                dgl_ref[c, pl.ds(h, 1), :] = jnp.sum(s_b[h].astype(F32) * ds_f[h]) + jnp.zeros((1, HEAD_DIM), F32)
            ds_f = [ds_f[h] * gl_ref[c, pl.ds(h, 1), :] + t_do[h] - t_dv[h] for h in hs]
        for h in hs:
            dstate[h] = ds_f[h]

    return _call(
        body, "gdn_scan_bwd", (n_c // k,), [row, row, row, row, row, qk_spec, gl_spec, st_spec],
        [row, row, row, row, qk_spec, gl_spec],
        [_sds((s_len, D_MODEL))] * 4 + [_sds((HEADS, s_len, CHUNK)), _sds(gl.shape)],
        scratch=[pltpu.VMEM((HEADS, HEAD_DIM, HEAD_DIM), F32)], sem=("arbitrary",))(do, u, w, qd, kd, qk, gl, states)


def _gate_norm(o, z, nw):
    outs = []
    for h in range(HEADS):
        oh = o[:, h * HEAD_DIM:(h + 1) * HEAD_DIM]
        outs.append(oh * lax.rsqrt(jnp.mean(oh * oh, axis=-1, keepdims=True) + RMS_EPS))
    return jnp.concatenate(outs, axis=1) * nw * _silu(z)


def _gdn_out_fwd(o, proj, x, w_out, nw, gain, bias):
    s_len = x.shape[0]
    ts = min(ROW_TILE, s_len)

    def body(o_ref, z_ref, x_ref, w_ref, nw_ref, g_ref, b_ref, y_ref, xhat_ref, rstd_ref):
        on = _gate_norm(o_ref[...], z_ref[...], nw_ref[...])
        r = ALPHA * x_ref[...] + _bdot(on, w_ref[...], NN)
        y_ref[...], xhat_ref[...], rstd_ref[...] = _ln_fwd(r, g_ref[...], b_ref[...])

    tile = pl.BlockSpec((ts, D_MODEL), lambda i: (i, 0))
    row = _row(D_MODEL)
    return _call(
        body, "gdn_out_fwd", (s_len // ts,),
        [tile, pl.BlockSpec((ts, D_MODEL), lambda i: (i, QKV_DIM // D_MODEL)), tile, _full((D_MODEL, D_MODEL)), row, row, row],
        [tile, tile, pl.BlockSpec((ts, 1), lambda i: (i, 0))],
        [_sds((s_len, D_MODEL)), _sds((s_len, D_MODEL)), _sds((s_len, 1))], sem=("parallel",))(o, proj, x, w_out, nw, gain, bias)


def _gdn_out_bwd(dy, xhat, rstd, o, proj, w_out, nw, gain):
    s_len = o.shape[0]
    ts = min(ROW_TILE, s_len)

    def body(dy_ref, xh_ref, rs_ref, o_ref, z_ref, w_ref, nw_ref, g_ref,
             dres_ref, do_ref, dz_ref, on_ref, drb_ref, dg_ref, db_ref, dnw_ref):
        dy_t, xh_t = dy_ref[...], xh_ref[...]
        dr = _ln_bwd(dy_t, xh_t, rs_ref[...], g_ref[...])
        dres_ref[...] = ALPHA * dr
        drb_ref[...] = dr.astype(BF16)
        on, vjp = jax.vjp(_gate_norm, o_ref[...], z_ref[...], nw_ref[...])
        on_ref[...] = on.astype(BF16)
        do, dz, dnw = vjp(_bdot(dr, w_ref[...], NT))
        do_ref[...] = do
        dz_ref[...] = dz.astype(BF16)
        first = pl.program_id(0) == 0
        _acc(dg_ref, first, jnp.sum(dy_t * xh_t, axis=0, keepdims=True))
        _acc(db_ref, first, jnp.sum(dy_t, axis=0, keepdims=True))
        _acc(dnw_ref, first, sum(dnw[:, h * HEAD_DIM:(h + 1) * HEAD_DIM] for h in range(HEADS)))

    tile = pl.BlockSpec((ts, D_MODEL), lambda i: (i, 0))
    row = _row(D_MODEL)
    return _call(
        body, "gdn_out_bwd", (s_len // ts,),
        [tile, tile, pl.BlockSpec((ts, 1), lambda i: (i, 0)), tile,
         pl.BlockSpec((ts, D_MODEL), lambda i: (i, QKV_DIM // D_MODEL)), _full((D_MODEL, D_MODEL)), row, row],
        [tile, tile, tile, tile, tile, row, row, _row(HEAD_DIM)],
        [_sds((s_len, D_MODEL))] * 2 + [_sds((s_len, D_MODEL), BF16)] * 3 + [_sds((1, D_MODEL))] * 2 + [_sds((1, HEAD_DIM))],
        sem=("arbitrary",))(dy, xhat, rstd, o, proj, w_out, nw, gain)


def _adamw(w, g, m, v, name):
    r, c = w.shape
    tr = min(ADAM_ROWS, r)
    pieces = list(g) if isinstance(g, (list, tuple)) else [g]
    n_p = len(pieces)
    blocks = [a.shape[0] // tr for a in pieces]
    first = [sum(blocks[:k]) for k in range(n_p)]

    def body(*refs):
        w_ref, g_refs, (m_ref, v_ref) = refs[0], refs[1:1 + n_p], refs[1 + n_p:3 + n_p]
        go_ref, d_ref, nm_ref, nv_ref = refs[3 + n_p:]
        i = pl.program_id(0)
        g_t = g_refs[0][...]
        for k in range(1, n_p):
            g_t = jnp.where(i >= first[k], g_refs[k][...], g_t)
        go_ref[...] = g_t
        nm = ADAM_B1 * m_ref[...] + (1.0 - ADAM_B1) * g_t
        nv = ADAM_B2 * v_ref[...] + (1.0 - ADAM_B2) * (g_t * g_t)
        m_hat = nm / (1.0 - ADAM_B1 ** ADAM_STEP)
        v_hat = nv / (1.0 - ADAM_B2 ** ADAM_STEP)
        d_ref[...] = -ADAM_LR * (m_hat / (jnp.sqrt(v_hat) + ADAM_EPS) + ADAM_WD * w_ref[...])
        nm_ref[...] = nm
        nv_ref[...] = nv

    tile = pl.BlockSpec((tr, c), lambda i: (i, 0))
    g_specs = [pl.BlockSpec((tr, c), lambda i, k=k: (jnp.clip(i - first[k], 0, blocks[k] - 1), 0)) for k in range(n_p)]
    return _call(body, name, (r // tr,), [tile] + g_specs + [tile] * 2, [tile] * 4, [_sds((r, c))] * 4,
                 sem=("parallel",))(w, *pieces, m, v)


def _assemble_w_in(shards):
    rows = 256
    width = GDN_IN_DIM // N_SHARD

    def body(s_ref, o_ref):
        pad = jnp.zeros((rows, GDN_IN_PAD - GDN_IN_DIM), shards.dtype)
        o_ref[...] = jnp.concatenate([s_ref[j] for j in range(N_SHARD)] + [pad], axis=1)

    return _call(body, "w_in_assemble", (D_MODEL // rows,), [pl.BlockSpec((N_SHARD, rows, width), lambda i: (0, i, 0))],
                 pl.BlockSpec((rows, GDN_IN_PAD), lambda i: (i, 0)), _sds((D_MODEL, GDN_IN_PAD), shards.dtype),
                 sem=("parallel",))(shards)


def _split_w_in(full):
    rows = 256
    width = GDN_IN_DIM // N_SHARD

    def body(f_ref, o_ref):
        f = f_ref[...]
        for j in range(N_SHARD):
            o_ref[j] = f[:, j * width:(j + 1) * width]

    return _call(body, "w_in_split", (D_MODEL // rows,), [pl.BlockSpec((rows, GDN_IN_PAD), lambda i: (i, 0))],
                 pl.BlockSpec((N_SHARD, rows, width), lambda i: (0, i, 0)), _sds((N_SHARD, D_MODEL, width), full.dtype),
                 sem=("parallel",))(full)


def _place():
    x, y, c = lax.axis_index("x"), lax.axis_index("y"), lax.axis_index("c")
    return x, y, c, [(1 - x, y), (x, 1 - y), (1 - x, 1 - y)]


def _row_tile(rows):
    return max(t for t in range(8, min(rows, 640) + 1, 8) if rows % t == 0)


def _place_shard(part, me, dtype, name, layer=0):
    _, _, r, c = part.shape
    tr = _row_tile(r)

    def body(me_ref, p_ref, o_ref):
        o_ref[...] = p_ref[...].astype(dtype)

    return pl.pallas_call(
        body, name=name, out_shape=_sds((N_SHARD, 2, r, c), dtype),
        grid_spec=pltpu.PrefetchScalarGridSpec(
            num_scalar_prefetch=1, grid=(2, r // tr),
            in_specs=[pl.BlockSpec((None, None, tr, c), lambda h, i, me_ref: (layer, h, i, 0))],
            out_specs=pl.BlockSpec((None, None, tr, c), lambda h, i, me_ref: (me_ref[0], h, i, 0))))(me, part)


def _gather_sems(n):
    return [pltpu.SemaphoreType.DMA((6 * n,)), pltpu.SemaphoreType.DMA((6 * n,))]


def _gather_steps(dsts, send_sems, recv_sems):
    n = len(dsts)
    x, y, c, chips = _place()
    me = 2 * x + y
    sibling = (x, y, 1 - c)

    def ici(k, j, slot):
        px, py = chips[j]
        view = dsts[k].at[slot, c]
        return pltpu.make_async_remote_copy(
            src_ref=view, dst_ref=view, send_sem=send_sems.at[6 * k + j],
            recv_sem=recv_sems.at[6 * k + j], device_id=(px, py, c), device_id_type=MESH)

    def d2d(k, j, half):
        px, py = chips[j]
        view = dsts[k].at[2 * px + py, half]
        return pltpu.make_async_remote_copy(
            src_ref=view, dst_ref=view, send_sem=send_sems.at[6 * k + 3 + j], recv_sem=recv_sems.at[6 * k + 3 + j],
            device_id=sibling, device_id_type=MESH)

    def start():
        for k in range(n):
            for j in range(3):
                ici(k, j, me).start()

    def finish():
        fwds = []
        for k in range(n):
            for j, (px, py) in enumerate(chips):
                ici(k, j, 2 * px + py).wait_recv()
                fwds.append(d2d(k, j, c))
                fwds[-1].start()
        for k in range(n):
            for j in range(3):
                d2d(k, j, 1 - c).wait_recv()
        for k in range(n):
            for j in range(3):
                ici(k, j, me).wait_send()
        for cp in fwds:
            cp.wait_send()

    return start, finish


def _all_gather(bufs, name):
    n = len(bufs)

    def body(*refs):
        start, finish = _gather_steps(refs[n:2 * n], *refs[2 * n:])
        start()
        finish()

    return pl.pallas_call(
        body, name=name, out_shape=[_sds(a.shape, a.dtype) for a in bufs],
        in_specs=[ANY] * n, out_specs=[ANY] * n, input_output_aliases={k: k for k in range(n)},
        scratch_shapes=_gather_sems(n))(*bufs)


def _swap_halves(pieces, name):
    n = len(pieces)

    def body(*refs):
        srcs, dsts = refs[:n], refs[n:2 * n]
        send_sems, recv_sems = refs[2 * n:]
        x, y, c, _ = _place()
        copies = []
        for k in range(n):
            hr = pieces[k].shape[1] // 2
            copies.append(pltpu.make_async_remote_copy(
                src_ref=srcs[k].at[:, pl.ds((1 - c) * hr, hr), :], dst_ref=dsts[k],
                send_sem=send_sems.at[k], recv_sem=recv_sems.at[k], device_id=(x, y, 1 - c), device_id_type=MESH))
        for cp in copies:
            cp.start()
        for cp in copies:
            cp.wait()

    return pl.pallas_call(
        body, name=name, out_shape=[_sds((N_SHARD, a.shape[1] // 2, a.shape[2])) for a in pieces],
        in_specs=[ANY] * n, out_specs=[ANY] * n,
        scratch_shapes=[pltpu.SemaphoreType.DMA((n,)), pltpu.SemaphoreType.DMA((n,))])(*pieces)


def _add_half(piece, other, place, dtype, name):
    n, hr, cols = other.shape
    tr = _row_tile(hr)

    def body(pl_ref, a_ref, b_ref, o_ref):
        o_ref[...] = (a_ref[...] + b_ref[...]).astype(dtype)

    tile = pl.BlockSpec((None, tr, cols), lambda s, i, pl_ref: (s, i, 0))
    return pl.pallas_call(
        body, name=name, out_shape=_sds(other.shape, dtype),
        grid_spec=pltpu.PrefetchScalarGridSpec(
            num_scalar_prefetch=1, grid=(n, hr // tr),
            in_specs=[pl.BlockSpec((None, None, tr, cols), lambda s, i, pl_ref: (s, pl_ref[1], i, 0)), tile],
            out_specs=tile))(place, piece.reshape(n, 2, hr, cols), other)


def _scatter_sems(n):
    return [pltpu.SemaphoreType.DMA((3 * n,)), pltpu.SemaphoreType.DMA((3 * n,))]


def _scatter_steps(srcs, dsts, send_sems, recv_sems):
    n = len(srcs)
    x, y, c, chips = _place()
    me = 2 * x + y

    def ici(k, j, src_slot, dst_slot):
        px, py = chips[j]
        return pltpu.make_async_remote_copy(
            src_ref=srcs[k].at[src_slot], dst_ref=dsts[k].at[dst_slot], send_sem=send_sems.at[3 * k + j],
            recv_sem=recv_sems.at[3 * k + j], device_id=(px, py, c), device_id_type=MESH)

    def start():
        for k in range(n):
            for j, (px, py) in enumerate(chips):
                ici(k, j, 2 * px + py, me).start()

    def finish():
        for k in range(n):
            for j, (px, py) in enumerate(chips):
                ici(k, j, me, 2 * px + py).wait_recv()
        for k in range(n):
            for j, (px, py) in enumerate(chips):
                ici(k, j, 2 * px + py, me).wait_send()

    return start, finish


def _scatter_chips(parts, name):
    n = len(parts)

    def body(*refs):
        start, finish = _scatter_steps(refs[:n], refs[n:2 * n], *refs[2 * n:])
        start()
        finish()

    return pl.pallas_call(
        body, name=name, out_shape=[_sds(a.shape, a.dtype) for a in parts],
        in_specs=[ANY] * n, out_specs=[ANY] * n, scratch_shapes=_scatter_sems(n))(*parts)


def _sum_chips(landed, own, place, name):
    _, r, cols = landed.shape
    tr = _row_tile(r)

    def body(pl_ref, q_ref, p_ref, o_ref):
        me = pl_ref[0]
        f = lambda j: jnp.where(me == j, p_ref[...], q_ref[j]).astype(F32)
        o_ref[...] = ((f(0) + f(1)) + f(2)) + f(3)

    return pl.pallas_call(
        body, name=name, out_shape=_sds((2, r, cols)),
        grid_spec=pltpu.PrefetchScalarGridSpec(
            num_scalar_prefetch=1, grid=(r // tr,),
            in_specs=[pl.BlockSpec((N_SHARD, tr, cols), lambda i, pl_ref: (0, i, 0)),
                      pl.BlockSpec((None, tr, cols), lambda i, pl_ref: (pl_ref[0], i, 0))],
            out_specs=pl.BlockSpec((None, tr, cols), lambda i, pl_ref: (pl_ref[1], i, 0))))(place, landed, own)


def _join_halves(bufs):
    n = len(bufs)

    def body(*refs):
        dsts = refs[n:2 * n]
        send_sems, recv_sems = refs[2 * n:]
        x, y, c, _ = _place()
        copies = [pltpu.make_async_remote_copy(
            src_ref=dsts[k].at[c], dst_ref=dsts[k].at[c], send_sem=send_sems.at[k], recv_sem=recv_sems.at[k],
            device_id=(x, y, 1 - c), device_id_type=MESH) for k in range(n)]
        for cp in copies:
            cp.start()
        for cp in copies:
            cp.wait()

    return pl.pallas_call(
        body, name="grads_join_halves", out_shape=[_sds(a.shape) for a in bufs], in_specs=[ANY] * n, out_specs=[ANY] * n,
        input_output_aliases={k: k for k in range(n)},
        scratch_shapes=[pltpu.SemaphoreType.DMA((n,)), pltpu.SemaphoreType.DMA((n,))])(*bufs)


GATHER_F32 = ("ln_gain", "ln_bias", "pool_b", "gdn_conv")
REPLICATED = ("pool_scale", "gdn_a_log", "gdn_dt_bias", "gdn_norm_w", "ple_gate_b")
WEIGHTS = ("ln_gain", "ln_bias", "pool_w", "pool_b", "pool_scale", "gdn_w_in", "gdn_conv", "gdn_a_log", "gdn_dt_bias",
           "gdn_norm_w", "gdn_w_out", "mlp_w1", "mlp_w2", "ple_gate_w", "ple_gate_b", "ple_proj")
SMALL_GRADS = ("ple_proj", "pool_w", "ln_gain", "ln_bias", "pool_b", "gdn_conv") + REPLICATED


def _pack(parts, lanes, row_multiple):
    flat = jnp.concatenate([a.reshape(-1) for a in parts])
    rows = -(-flat.shape[0] // (2 * lanes * row_multiple)) * row_multiple
    return jnp.pad(flat, (0, 2 * rows * lanes - flat.shape[0])).reshape(2, rows, lanes)


def _unpack(flat, shapes):
    out, off = [], 0
    for shp in shapes:
        n = math.prod(shp)
        out.append(flat[..., off:off + n].reshape(flat.shape[:-1] + tuple(shp)))
        off += n
    return out


def _pad_lanes(a, offset, width=128):
    return jnp.pad(a, ((0, 0), (offset, width - offset - a.shape[1])))


def kernel(x, p, ln_gain, ln_bias, pool_w, pool_b, pool_scale, gdn_w_in, gdn_conv, gdn_a_log, gdn_dt_bias, gdn_norm_w, gdn_w_out, mlp_w1, mlp_w2, ple_gate_w, ple_gate_b, ple_proj, loss_target, m_ln_gain, m_ln_bias, m_pool_w, m_pool_b, m_pool_scale, m_gdn_w_in, m_gdn_conv, m_gdn_a_log, m_gdn_dt_bias, m_gdn_norm_w, m_gdn_w_out, m_mlp_w1, m_mlp_w2, m_ple_gate_w, m_ple_gate_b, m_ple_proj, v_ln_gain, v_ln_bias, v_pool_w, v_pool_b, v_pool_scale, v_gdn_w_in, v_gdn_conv, v_gdn_a_log, v_gdn_dt_bias, v_gdn_norm_w, v_gdn_w_out, v_mlp_w1, v_mlp_w2, v_ple_gate_w, v_ple_gate_b, v_ple_proj):
    shard = dict(ln_gain=ln_gain, ln_bias=ln_bias, pool_w=pool_w, pool_b=pool_b, pool_scale=pool_scale, gdn_w_in=gdn_w_in,
                 gdn_conv=gdn_conv, gdn_a_log=gdn_a_log, gdn_dt_bias=gdn_dt_bias, gdn_norm_w=gdn_norm_w, gdn_w_out=gdn_w_out,
                 mlp_w1=mlp_w1, mlp_w2=mlp_w2, ple_gate_w=ple_gate_w, ple_gate_b=ple_gate_b, ple_proj=ple_proj)
    mom = dict(ln_gain=m_ln_gain, ln_bias=m_ln_bias, pool_w=m_pool_w, pool_b=m_pool_b, pool_scale=m_pool_scale,
               gdn_w_in=m_gdn_w_in, gdn_conv=m_gdn_conv, gdn_a_log=m_gdn_a_log, gdn_dt_bias=m_gdn_dt_bias,
               gdn_norm_w=m_gdn_norm_w, gdn_w_out=m_gdn_w_out, mlp_w1=m_mlp_w1, mlp_w2=m_mlp_w2, ple_gate_w=m_ple_gate_w,
               ple_gate_b=m_ple_gate_b, ple_proj=m_ple_proj)
    var = dict(ln_gain=v_ln_gain, ln_bias=v_ln_bias, pool_w=v_pool_w, pool_b=v_pool_b, pool_scale=v_pool_scale,
               gdn_w_in=v_gdn_w_in, gdn_conv=v_gdn_conv, gdn_a_log=v_gdn_a_log, gdn_dt_bias=v_gdn_dt_bias,
               gdn_norm_w=v_gdn_norm_w, gdn_w_out=v_gdn_w_out, mlp_w1=v_mlp_w1, mlp_w2=v_mlp_w2, ple_gate_w=v_ple_gate_w,
               ple_gate_b=v_ple_gate_b, ple_proj=v_ple_proj)

    xi, yi, ci = lax.axis_index("x"), lax.axis_index("y"), lax.axis_index("c")
    me = (2 * xi + yi).reshape(1).astype(jnp.int32)
    place = jnp.stack([2 * xi + yi, ci]).astype(jnp.int32)
    early = [("mlp_w1", 0), ("mlp_w2", 0), ("ple_gate_w", 0), ("ple_proj", 0), ("pool_w", 0)]
    late = [("mlp_w1", 1), ("mlp_w2", 1), ("ple_gate_w", 1), ("ple_proj", 1), ("gdn_w_out", 0), ("gdn_w_in", 0)]
    halved = lambda n: shard[n].reshape(shard[n].shape[0], 2, -1, shard[n].shape[-1])
    placed = lambda ops, tag: [_place_shard(halved(n), me, BF16, f"place_{tag}_{n}", l) for n, l in ops]
    small_in = _place_shard(_pack([shard[n] for n in GATHER_F32], 128, 8)[None], me, F32, "place_small")
    got_early = _all_gather(placed(early, "early") + [small_in], "weights_all_gather_early")
    placed_late = placed(late, "late")
    st = dict(zip(GATHER_F32, _unpack(got_early[-1].reshape(N_SHARD, -1), [shard[n].shape for n in GATHER_F32])))

    cat_last = lambda a: jnp.moveaxis(a, 0, -2).reshape(a.shape[1:-1] + (N_SHARD * a.shape[-1],))
    gain = cat_last(st["ln_gain"])
    bias = cat_last(st["ln_bias"])
    wp = got_early[4].reshape(N_SHARD, 4, POOL_GROUP // N_SHARD, POOL_GROUP)
    pb = cat_last(st["pool_b"]).reshape(1, D_MODEL)
    ps = pool_scale
    conv_w = cat_last(st["gdn_conv"])[0]
    merged = lambda g: g.reshape(N_SHARD, -1, g.shape[-1])
    mlp_w = lambda i, got: (merged(got[0]), merged(got[1]), merged(got[2]), ple_gate_b[i:i + 1], merged(got[3]))
    alog_l = _pad_lanes(gdn_a_log, HEADS)
    dtb_l = _pad_lanes(gdn_dt_bias, HEADS)
    nw = jnp.tile(gdn_norm_w, (1, HEADS))
    ln = lambda i, k: (gain[i, k][None], bias[i, k][None])

    x0 = x[0]
    p0, p1 = p[0, 0], p[1, 0]

    x1, xh1, rs1 = _pool_fwd(x0, wp, pb, ps, *ln(0, 0))
    (x2, xh2, rs2, a0, xb0), got_late = _mlp_fwd(x1, p0, *mlp_w(0, got_early), *ln(0, 1), "mlp_fwd_0", gather=placed_late)
    w_out = got_late[4].reshape(D_MODEL, D_MODEL)
    w_in = _assemble_w_in(merged(got_late[5]))
    proj, y_conv, q, k, v, gcb = _conv_fwd(x2, w_in, conv_w, alog_l, dtb_l)
    u, w, qd, kd, qk, gl, t_inv = _gdn_prep(q, k, v, gcb)
    o, states = _gdn_scan(u, w, qd, kd, qk, gl)
    x3, xh3, rs3 = _gdn_out_fwd(o, proj, x2, w_out, nw, *ln(1, 0))
    (dy4, xh4, rs4, a1, xb1, loss_l), _ = _mlp_fwd(x3, p1, *mlp_w(1, got_late), *ln(1, 1), "mlp_fwd_1", target=loss_target[0])

    g_gain = [[None, None], [None, None]]
    g_bias = [[None, None], [None, None]]

    def mlp_grads(i, dy, xh, rs, x_mid, xb, a, p_i, got, scatter=()):
        (dx, dh, dzg, dpp, drb, dg, db, dgb), landed = _mlp_bwd(
            dy, xh, rs, x_mid, a, p_i, *mlp_w(i, got), ln(i, 1)[0], f"mlp_bwd_{i}", scatter=scatter)
        g_gain[i][1], g_bias[i][1] = dg, db
        return dx, dict(
            mlp_w1=_wgrad(xb, dh, f"dw1_{i}", stack_cols=True), mlp_w2=_wgrad(a, drb, f"dw2_{i}").reshape(N_SHARD, -1, D_MODEL),
            ple_gate_w=_wgrad(xb, dzg, f"dgate_w_{i}").reshape(N_SHARD, -1, D_MODEL),
            ple_proj=jnp.moveaxis(_wgrad(p_i, dpp, f"dproj_{i}").reshape(PLE_DIM, N_SHARD, -1), 1, 0), ple_gate_b=dgb), landed

    def chip_sums(pieces, wire, tag):
        others = _swap_halves(pieces, f"grads_swap_halves_{tag}")
        return [_add_half(a, b, place, t, f"grads_add_half_{tag}{i}") for i, (a, b, t) in enumerate(zip(pieces, others, wire))]

    dx3, gl1, _ = mlp_grads(1, dy4, xh4, rs4, x3, xb1, a1, p1, got_late)
    dres, do, dz, on_b, drb3, g_gain[1][0], g_bias[1][0], d_nw = _gdn_out_bwd(dx3, xh3, rs3, o, proj, w_out, nw, ln(1, 0)[0])
    d_wout = _wgrad(on_b, drb3, "dw_out").reshape(N_SHARD, -1, D_MODEL)
    du, dw, dqd, dkd, dqk, dgl = _gdn_scan_bwd(do, u, w, qd, kd, qk, gl, states)
    dq, dk, dv, dgcb = _gdn_prep_bwd(q, k, v, gcb, t_inv, u, w, du, dw, dqd, dkd, dqk, dgl)
    dproj, dx2, d_conv, d_alog_l, d_dtb_l = _conv_bwd(proj, y_conv, dq, dk, dv, dgcb, dz, conv_w, alog_l, dtb_l, w_in, dres)
    d_win = _split_w_in(_wgrad(x2, dproj, "dw_in"))
    sums_late = chip_sums([gl1["mlp_w1"], gl1["mlp_w2"], gl1["ple_gate_w"], d_wout, d_win], [BF16] * 5, "late")
    dx1, gl0, landed_late = mlp_grads(0, dx2, xh2, rs2, x1, xb0, a0, p0, got_early, scatter=sums_late)
    sums_early = chip_sums([gl0["mlp_w1"], gl0["mlp_w2"]], [BF16] * 2, "early")
    (dx0, g_gain[0][0], g_bias[0][0], d_ps, d_pb, d_wp), landed_early = _pool_bwd(
        dx1, xh1, rs1, x0, wp, pb, ps, ln(0, 0)[0], scatter=sums_early)

    split_last = lambda a: jnp.moveaxis(a.reshape(a.shape[:-1] + (N_SHARD, a.shape[-1] // N_SHARD)), -2, 0)
    small_st = dict(
        ple_proj=jnp.stack([gl0["ple_proj"], gl1["ple_proj"]], axis=1),
        pool_w=jnp.moveaxis(d_wp.reshape(4, N_SHARD, POOL_GROUP // N_SHARD, POOL_GROUP), 1, 0)[:, None],
        ln_gain=split_last(jnp.stack([jnp.concatenate(r, axis=0) for r in g_gain])),
        ln_bias=split_last(jnp.stack([jnp.concatenate(r, axis=0) for r in g_bias])),
        pool_b=split_last(d_pb.reshape(1, 4, POOL_GROUP)),
        gdn_conv=split_last(d_conv)[:, None],
    )
    rep = dict(pool_scale=d_ps, gdn_a_log=d_alog_l[:, HEADS:2 * HEADS], gdn_dt_bias=d_dtb_l[:, HEADS:2 * HEADS],
               gdn_norm_w=d_nw, ple_gate_b=jnp.concatenate([gl0["ple_gate_b"], gl1["ple_gate_b"]], axis=0))
    for n in REPLICATED:
        small_st[n] = jnp.broadcast_to(rep[n][None], (N_SHARD,) + rep[n].shape)
    small_flat = jnp.concatenate([small_st[n].reshape(N_SHARD, -1) for n in SMALL_GRADS], axis=1)
    small_rows = -(-small_flat.shape[1] // (16 * LANES)) * 16
    small_piece = jnp.pad(small_flat, ((0, 0), (0, small_rows * LANES - small_flat.shape[1]))).reshape(N_SHARD, small_rows, LANES)

    sums_small = chip_sums([gl0["ple_gate_w"], small_piece], [BF16, F32], "small")
    landed_small = _scatter_chips(sums_small, "grads_scatter_chips_small")
    red = _join_halves([_sum_chips(q_, p_, place, f"grads_sum_chips_{i}") for i, (q_, p_) in
                        enumerate(zip(list(landed_early) + list(landed_small) + list(landed_late),
                                      sums_early + sums_small + sums_late))])
    red = [r.reshape(-1, r.shape[-1]) for r in red]
    grads = dict(mlp_w1=[red[0], red[4]], mlp_w2=[red[1], red[5]], ple_gate_w=[red[2], red[6]], gdn_w_out=red[7], gdn_w_in=red[8])
    grads.update(zip(SMALL_GRADS, _unpack(red[3].reshape(-1), [shard[n].shape for n in SMALL_GRADS])))

    delta, new_m, new_v = {}, {}, {}
    small = [n for n in WEIGHTS if shard[n].size < 128 * 128]
    for n in WEIGHTS:
        if n in small:
            continue
        to2d = lambda a, n=n: a.reshape(-1, shard[n].shape[-1])
        g_n = grads[n] if isinstance(grads[n], list) else to2d(grads[n])
        g2, d2, m2, v2 = _adamw(to2d(shard[n]), g_n, to2d(mom[n]), to2d(var[n]), "adamw_" + n)
        grads[n], delta[n], new_m[n], new_v[n] = (t.reshape(shard[n].shape) for t in (g2, d2, m2, v2))
    pk = lambda d: _pack([d[n] for n in small], 128, 8).reshape(-1, 128)
    _, d2, m2, v2 = _adamw(pk(shard), pk(grads), pk(mom), pk(var), "adamw_small")
    for dst, t in ((delta, d2), (new_m, m2), (new_v, v2)):
        dst.update(zip(small, _unpack(t.reshape(-1), [shard[n].shape for n in small])))

    loss = lax.psum(loss_l[0, 0], ("x", "y", "c"))
    return (loss, dx0[None], *[grads[n] for n in WEIGHTS], *[delta[n] for n in WEIGHTS],
            *[new_m[n] for n in WEIGHTS], *[new_v[n] for n in WEIGHTS])
```
